```python
import math
import jax, jax.numpy as jnp
from jax import lax
import numpy as np

D_MODEL = 1024
BATCH = 32
SEQ = 256
DEPTH = 2
DEC_BATCH = 2
DEC_SEQ = 2048
PAST_LEN = 512

GRID_W = 64
N_MIXERS = 4
D_BRANCH = D_MODEL // N_MIXERS
D_MIX = N_MIXERS * D_BRANCH
EPS = 1e-6

MLA_HEADS = 4
MLA_Q_RANK = 3 * D_MODEL // 16
MLA_KV_RANK = D_MODEL // 8
MLA_NOPE = 64
MLA_ROPE = 32
MLA_QK = MLA_NOPE + MLA_ROPE
MLA_V = D_BRANCH // MLA_HEADS
ROPE_BASE = 10000.0
Q_BLOCK = 128

HY_ORDER = 2
HY_SHORT = 3
HY_BANDS = 16
HY_FEAT = 1 + 2 * HY_BANDS
HY_HIDDEN = 64
HY_SHIFT = 0.05
HY_FAST_DECAY = 0.3
HY_SLOW_DECAY = 1.5
HY_TARGET = 1e-2

S5_GROUP = 16
S5_GROUPS = D_BRANCH // S5_GROUP
S5_STATE = 64
S5_DT_MIN = 1e-3
S5_DT_MAX = 1e-1

GLA_HEADS = 4
GLA_DK = D_BRANCH // 2 // GLA_HEADS
GLA_DV = D_BRANCH // GLA_HEADS
GLA_GATE_RANK = 16
GLA_TAU = 16.0
GLA_CHUNK = 64

IN_SPLITS = (MLA_Q_RANK, MLA_KV_RANK, MLA_ROPE, D_BRANCH,
             3 * D_BRANCH, D_BRANCH,
             D_BRANCH, D_BRANCH,
             GLA_HEADS * GLA_DK, GLA_HEADS * GLA_DK, GLA_HEADS * GLA_DV,
             2 * GLA_GATE_RANK, D_BRANCH)
N_IN = sum(IN_SPLITS)

kernel_name = 'hybrid_prefix_diffusion_step'


def split_points():
    return np.cumsum(IN_SPLITS)[:-1].tolist()


def rmsnorm(x, w):
    xf = x.astype(jnp.float32)
    y = xf * lax.rsqrt(jnp.mean(xf * xf, axis=-1, keepdims=True) + EPS)
    return (y * w.astype(jnp.float32)).astype(x.dtype)


def rope_1d(x, pos):
    d = x.shape[-1]
    inv = ROPE_BASE ** (-jnp.arange(0, d, 2, dtype=jnp.float32) / d)
    ang = pos.astype(jnp.float32)[:, None] * inv[None, :]
    ang = jnp.concatenate([ang, ang], axis=-1)[None, :, None, :]
    x1, x2 = jnp.split(x, 2, axis=-1)
    rot = jnp.concatenate([-x2, x1], axis=-1)
    return (x * jnp.cos(ang) + rot * jnp.sin(ang)).astype(x.dtype)


def rope_tail(x, row, col):
    r = x[..., MLA_NOPE:]
    half = MLA_ROPE // 2
    r = jnp.concatenate([rope_1d(r[..., :half], row), rope_1d(r[..., half:], col)], axis=-1)
    return jnp.concatenate([x[..., :MLA_NOPE], r], axis=-1)


def block_attention(q, k, v):
    b, lq, h, dq = q.shape
    nb = lq // Q_BLOCK
    qb = jnp.moveaxis(q.reshape(b, nb, Q_BLOCK, h, dq), 1, 0)

    def one(qi):
        s = jnp.einsum('bqhd,bkhd->bhqk', qi, k).astype(jnp.float32) * (MLA_QK ** -0.5)
        pr = jax.nn.softmax(s, axis=-1).astype(v.dtype)
        return jnp.einsum('bhqk,bkhd->bqhd', pr, v)

    o = lax.map(one, qb)
    return jnp.moveaxis(o, 0, 1).reshape(b, lq, h, v.shape[-1])


def mla_keys_values(c_kv, k_rope, p):
    b, L, _ = c_kv.shape
    kv = (c_kv @ p['mla_w_ukv']).reshape(b, L, MLA_HEADS, MLA_NOPE + MLA_V)
    kr = jnp.broadcast_to(k_rope[:, :, None, :], (b, L, MLA_HEADS, MLA_ROPE)).astype(kv.dtype)
    k = rmsnorm(jnp.concatenate([kv[..., :MLA_NOPE], kr], axis=-1), p['mla_k_norm'])
    return k, kv[..., MLA_NOPE:]


def hyena_filters(L, p):
    pos = jnp.arange(L, dtype=jnp.float32)
    t = pos / L
    w = 2.0 * math.pi * pos / L
    bands = jnp.linspace(1e-4, HY_BANDS - 1, HY_BANDS, dtype=jnp.float32)
    feat = jnp.concatenate([t[:, None], jnp.cos(w[:, None] * bands), jnp.sin(w[:, None] * bands)], axis=-1)
    hid = jnp.sin(p['hy_freq1'] * (feat @ p['hy_w1'] + p['hy_b1']))
    hid = jnp.sin(p['hy_freq2'] * (hid @ p['hy_w2'] + p['hy_b2']))
    filt = (hid @ p['hy_w3']).astype(jnp.float32).reshape(L, 2, HY_ORDER, D_BRANCH)
    deltas = jnp.linspace(math.log(1.0 / HY_TARGET) / HY_FAST_DECAY,
                          math.log(1.0 / HY_TARGET) / HY_SLOW_DECAY, D_BRANCH, dtype=jnp.float32)
    window = jnp.exp(-t[:, None] * deltas[None, :]) + HY_SHIFT
    filt = filt * window[:, None, None, :]
    kern = jnp.concatenate([filt[:, 0], jnp.zeros((1, HY_ORDER, D_BRANCH), jnp.float32),
                            filt[1:, 1][::-1]], axis=0)
    kern = kern / jnp.sum(jnp.abs(kern), axis=0, keepdims=True)
    return jnp.fft.rfft(kern, axis=0)


def fft_conv(u, kf):
    L = u.shape[1]
    uf = jnp.fft.rfft(u, n=2 * L, axis=1)
    return jnp.fft.irfft(uf * kf[None], n=2 * L, axis=1)[:, :L]


def hyena_mixer(z, p):
    L = z.shape[1]
    z = lax.conv_general_dilated(z, p['hy_conv_w'][:, None, :].astype(z.dtype), (1,),
                                 ((HY_SHORT // 2, HY_SHORT // 2),),
                                 dimension_numbers=('NWC', 'WIO', 'NWC'),
                                 feature_group_count=z.shape[-1]) + p['hy_conv_b']
    v, x1, x2 = jnp.split(z.astype(jnp.float32), 3, axis=-1)
    kf = hyena_filters(L, p)
    bias = p['hy_bias'].astype(jnp.float32)
    y = x1 * (fft_conv(v, kf[:, 0]) + bias[0] * v)
    y = x2 * (fft_conv(y, kf[:, 1]) + bias[1] * y)
    return y


def s5_discretise(p, d):
    f32 = jnp.float32
    a = lax.complex(jnp.minimum(p['s5_a_re'][d].astype(f32), -1e-4), p['s5_a_im'][d].astype(f32))
    dt = jnp.exp(p['s5_log_dt'][d].astype(f32))[:, None]
    a_bar = jnp.exp(a * dt)
    bmat = lax.complex(p['s5_b_re'][d].astype(f32), p['s5_b_im'][d].astype(f32))
    b_bar = ((a_bar - 1.0) / a)[..., None] * bmat
    cmat = lax.complex(p['s5_c_re'][d].astype(f32), p['s5_c_im'][d].astype(f32))
    return a_bar, b_bar, cmat


def s5_scan(ug, a_bar, b_bar, h0, reverse):
    bu = jnp.einsum('blgi,gpi->blgp', ug.astype(jnp.complex64), b_bar)
    a = jnp.broadcast_to(a_bar, bu.shape)

    def combine(e1, e2):
        a1, b1 = e1
        a2, b2 = e2
        return a1 * a2, a2 * b1 + b2

    a_cum, h = lax.associative_scan(combine, (a, bu), reverse=reverse, axis=1)
    return h + a_cum * h0[:, None]


def s5_mixer(u, p, h0):
    b, L, _ = u.shape
    uf = u.astype(jnp.float32)
    ug = uf.reshape(b, L, S5_GROUPS, S5_GROUP)
    y = p['s5_d'].astype(jnp.float32) * uf
    finals = []
    for d in range(2):
        a_bar, b_bar, cmat = s5_discretise(p, d)
        h = s5_scan(ug, a_bar, b_bar, h0[:, d], reverse=(d == 1))
        y = y + jnp.real(jnp.einsum('blgp,gip->blgi', h, cmat)).reshape(b, L, D_BRANCH)
        finals.append(h[:, -1] if d == 0 else h[:, 0])
    g = jax.nn.gelu(y)
    out = g * jax.nn.sigmoid(g @ p['s5_glu_w'] + p['s5_glu_b'])
    return out, jnp.stack(finals, axis=1)


def gla_chunked(q, k, v, log_a, s0):
    b, L, h, _ = q.shape
    n = L // GLA_CHUNK

    def to_chunks(x):
        return jnp.moveaxis(x.reshape(b, n, GLA_CHUNK, h, x.shape[-1]), 1, 0)

    mask = jnp.tril(jnp.ones((GLA_CHUNK, GLA_CHUNK), dtype=bool))[None, :, :, None, None]

    def step(s, inp):
        qc, kc, vc, gc = inp
        bc = jnp.cumsum(gc, axis=1)
        diff = bc[:, :, None] - bc[:, None, :]
        decay = jnp.exp(jnp.where(mask, diff, -jnp.inf))
        att = jnp.einsum('btshd,bthd,bshd->bhts', decay, qc, kc)
        o = (jnp.einsum('bhts,bshe->bthe', att, vc)
             + jnp.einsum('bthd,bhde->bthe', qc * jnp.exp(bc), s))
        blast = bc[:, -1]
        s_new = (jnp.exp(blast)[..., None] * s
                 + jnp.einsum('bshd,bshe->bhde', kc * jnp.exp(blast[:, None] - bc), vc))
        return s_new, o

    s_fin, o = lax.scan(step, s0, (to_chunks(q), to_chunks(k), to_chunks(v), to_chunks(log_a)))
    return jnp.moveaxis(o, 0, 1).reshape(b, L, h, v.shape[-1]), s_fin


def gla_mixer(q, k, v, g_lr, p, s0):
    b, L, _ = q.shape
    f32 = jnp.float32
    qh = q.astype(f32).reshape(b, L, GLA_HEADS, GLA_DK) * (GLA_DK ** -0.5)
    kh = k.astype(f32).reshape(b, L, GLA_HEADS, GLA_DK)
    vh = v.astype(f32).reshape(b, L, GLA_HEADS, GLA_DV)
    g_lr = g_lr.astype(f32).reshape(b, L, 2, GLA_GATE_RANK)
    flip = lambda t: t[:, ::-1]
    outs, finals = [], []
    for d in range(2):
        log_a = jax.nn.log_sigmoid(g_lr[:, :, d] @ p['gla_gw'][d] + p['gla_gb'][d]) / GLA_TAU
        log_a = log_a.astype(f32).reshape(b, L, GLA_HEADS, GLA_DK)
        if d == 0:
            od, sf = gla_chunked(qh, kh, vh, log_a, s0[:, d])
        else:
            od, sf = gla_chunked(flip(qh), flip(kh), flip(vh), flip(log_a), s0[:, d])
            od = flip(od)
        outs.append(od)
        finals.append(sf)
    o = rmsnorm(outs[0] + outs[1], p['gla_norm']).reshape(b, L, D_BRANCH)
    return o, jnp.stack(finals, axis=1)


def trunk_layer(x, cond, p, pos=None, ctx=None):
    b, L, _ = x.shape
    f32 = jnp.float32
    mod = jax.nn.silu(cond) @ p['ada_w'] + p['ada_b']
    shift, scale, gate = jnp.split(mod[:, None, :], 3, axis=-1)
    h = rmsnorm(x, p['norm_w']) * (1.0 + scale) + shift
    (c_q, c_kv, k_rope, g_mla, hy_in, g_hy, s5_in, g_s5,
     gla_q, gla_k, gla_v, gla_g, g_gla) = jnp.split(h @ p['w_in'], split_points(), axis=-1)

    c_q = rmsnorm(c_q, p['mla_qa_norm'])
    c_kv = rmsnorm(c_kv, p['mla_kva_norm'])
    q = rmsnorm((c_q @ p['mla_w_uq']).reshape(b, L, MLA_HEADS, MLA_QK), p['mla_q_norm'])
    k, v = mla_keys_values(c_kv, k_rope, p)
    if ctx is None:
        o_mla = block_attention(q, k, v)
        s5_h0 = jnp.zeros((b, 2, S5_GROUPS, S5_STATE), jnp.complex64)
        gla_s0 = jnp.zeros((b, 2, GLA_HEADS, GLA_DK, GLA_DV), f32)
    else:
        ctx_ckv, ctx_krope, ctx_s5, ctx_gla = ctx
        row, col = pos
        k_ctx, v_ctx = mla_keys_values(ctx_ckv, ctx_krope, p)
        q = rope_tail(q, row, col)
        k = rope_tail(k, row, col)
        o_mla = block_attention(q, jnp.concatenate([k_ctx.astype(k.dtype), k], axis=1),
                                jnp.concatenate([v_ctx.astype(v.dtype), v], axis=1))
        s5_h0 = lax.complex(ctx_s5[..., 0].astype(f32), ctx_s5[..., 1].astype(f32))
        gla_s0 = ctx_gla.astype(f32)

    o_hy = hyena_mixer(hy_in, p)
    o_s5, s5_fin = s5_mixer(s5_in, p, s5_h0)
    o_gla, gla_fin = gla_mixer(gla_q, gla_k, gla_v, gla_g, p, gla_s0)

    branches = jnp.concatenate([
        o_mla.reshape(b, L, D_BRANCH) * jax.nn.silu(g_mla),
        o_hy * jax.nn.silu(g_hy),
        o_s5 * jax.nn.silu(g_s5),
        o_gla * jax.nn.silu(g_gla)], axis=-1).astype(x.dtype)
    x = (x + gate * (branches @ p['w_out'])).astype(x.dtype)
    if ctx is None:
        s5_state = jnp.stack([jnp.real(s5_fin), jnp.imag(s5_fin)], axis=-1)
        return x, (c_kv, k_rope, s5_state, gla_fin)
    return x, None


def setup_inputs(seed: int = 0) -> dict:
    key = jax.random.key(seed)
    ks = iter(jax.random.split(key, 48))
    f32 = jnp.float32

    def nrm(shape, scale):
        return jax.random.normal(next(ks), shape, f32) * scale

    def gain(shape):
        return 1.0 + nrm(shape, 0.02)

    a_im = jnp.broadcast_to(math.pi * jnp.arange(S5_STATE, dtype=f32), (DEPTH, 2, S5_GROUPS, S5_STATE))
    return {
        'x_prompt': nrm((BATCH, SEQ, D_MODEL), 1.0),
        'x_sample': nrm((DEC_BATCH, DEC_SEQ, D_MODEL), 1.0),
        'c': nrm((DEC_BATCH, D_MODEL), 1.0),
        'cache_mla_ckv': nrm((DEC_BATCH, DEPTH, PAST_LEN, MLA_KV_RANK), 1.0),
        'cache_mla_krope': nrm((DEC_BATCH, DEPTH, PAST_LEN, MLA_ROPE), 1.0),
        'state_s5': nrm((DEC_BATCH, DEPTH, 2, S5_GROUPS, S5_STATE, 2), 1.0),
        'state_gla': nrm((DEC_BATCH, DEPTH, 2, GLA_HEADS, GLA_DK, GLA_DV), 1.0),
        'c_ctx': nrm((D_MODEL,), 1.0),
        'norm_w': gain((DEPTH, D_MODEL)),
        'ada_w': nrm((DEPTH, D_MODEL, 3 * D_MODEL), 0.5 * D_MODEL ** -0.5),
        'ada_b': nrm((DEPTH, 3 * D_MODEL), 0.02),
        'w_in': nrm((DEPTH, D_MODEL, N_IN), D_MODEL ** -0.5),
        'w_out': nrm((DEPTH, D_MIX, D_MODEL), D_MIX ** -0.5),
        'mla_qa_norm': gain((DEPTH, MLA_Q_RANK)),
        'mla_kva_norm': gain((DEPTH, MLA_KV_RANK)),
        'mla_w_uq': nrm((DEPTH, MLA_Q_RANK, MLA_HEADS * MLA_QK), MLA_Q_RANK ** -0.5),
        'mla_w_ukv': nrm((DEPTH, MLA_KV_RANK, MLA_HEADS * (MLA_NOPE + MLA_V)), MLA_KV_RANK ** -0.5),
        'mla_q_norm': gain((DEPTH, MLA_QK)),
        'mla_k_norm': gain((DEPTH, MLA_QK)),
        'hy_conv_w': nrm((DEPTH, HY_SHORT, 3 * D_BRANCH), HY_SHORT ** -0.5),
        'hy_conv_b': nrm((DEPTH, 3 * D_BRANCH), 0.02),
        'hy_w1': nrm((DEPTH, HY_FEAT, HY_HIDDEN), HY_FEAT ** -0.5),
        'hy_b1': nrm((DEPTH, HY_HIDDEN), 0.02),
        'hy_freq1': gain((DEPTH, HY_HIDDEN)),
        'hy_w2': nrm((DEPTH, HY_HIDDEN, HY_HIDDEN), HY_HIDDEN ** -0.5),
        'hy_b2': nrm((DEPTH, HY_HIDDEN), 0.02),
        'hy_freq2': gain((DEPTH, HY_HIDDEN)),
        'hy_w3': nrm((DEPTH, HY_HIDDEN, 2 * HY_ORDER * D_BRANCH), HY_HIDDEN ** -0.5),
        'hy_bias': nrm((DEPTH, HY_ORDER, D_BRANCH), 1.0),
        's5_a_re': -0.5 + nrm((DEPTH, 2, S5_GROUPS, S5_STATE), 0.01),
        's5_a_im': a_im + nrm((DEPTH, 2, S5_GROUPS, S5_STATE), 0.01),
        's5_log_dt': jax.random.uniform(next(ks), (DEPTH, 2, S5_GROUPS), f32,
                                        math.log(S5_DT_MIN), math.log(S5_DT_MAX)),
        's5_b_re': nrm((DEPTH, 2, S5_GROUPS, S5_STATE, S5_GROUP), (2 * S5_GROUP) ** -0.5),
        's5_b_im': nrm((DEPTH, 2, S5_GROUPS, S5_STATE, S5_GROUP), (2 * S5_GROUP) ** -0.5),
        's5_c_re': nrm((DEPTH, 2, S5_GROUPS, S5_GROUP, S5_STATE), (2 * S5_STATE) ** -0.5),
        's5_c_im': nrm((DEPTH, 2, S5_GROUPS, S5_GROUP, S5_STATE), (2 * S5_STATE) ** -0.5),
        's5_d': nrm((DEPTH, D_BRANCH), 1.0),
        's5_glu_w': nrm((DEPTH, D_BRANCH, D_BRANCH), D_BRANCH ** -0.5),
        's5_glu_b': nrm((DEPTH, D_BRANCH), 0.02),
        'gla_gw': nrm((DEPTH, 2, GLA_GATE_RANK, GLA_HEADS * GLA_DK), GLA_GATE_RANK ** -0.5),
        'gla_gb': nrm((DEPTH, 2, GLA_HEADS * GLA_DK), 0.02),
        'gla_norm': gain((DEPTH, GLA_DV)),
    }


def reference(x_prompt, x_sample, c, cache_mla_ckv, cache_mla_krope, state_s5, state_gla, c_ctx,
              norm_w, ada_w, ada_b, w_in, w_out,
              mla_qa_norm, mla_kva_norm, mla_w_uq, mla_w_ukv, mla_q_norm, mla_k_norm,
              hy_conv_w, hy_conv_b, hy_w1, hy_b1, hy_freq1, hy_w2, hy_b2, hy_freq2, hy_w3, hy_bias,
              s5_a_re, s5_a_im, s5_log_dt, s5_b_re, s5_b_im, s5_c_re, s5_c_im, s5_d, s5_glu_w, s5_glu_b,
              gla_gw, gla_gb, gla_norm):
    weights = {
        'norm_w': norm_w, 'ada_w': ada_w, 'ada_b': ada_b, 'w_in': w_in, 'w_out': w_out,
        'mla_qa_norm': mla_qa_norm, 'mla_kva_norm': mla_kva_norm, 'mla_w_uq': mla_w_uq,
        'mla_w_ukv': mla_w_ukv, 'mla_q_norm': mla_q_norm, 'mla_k_norm': mla_k_norm,
        'hy_conv_w': hy_conv_w, 'hy_conv_b': hy_conv_b, 'hy_w1': hy_w1, 'hy_b1': hy_b1,
        'hy_freq1': hy_freq1, 'hy_w2': hy_w2, 'hy_b2': hy_b2, 'hy_freq2': hy_freq2,
        'hy_w3': hy_w3, 'hy_bias': hy_bias,
        's5_a_re': s5_a_re, 's5_a_im': s5_a_im, 's5_log_dt': s5_log_dt, 's5_b_re': s5_b_re,
        's5_b_im': s5_b_im, 's5_c_re': s5_c_re, 's5_c_im': s5_c_im, 's5_d': s5_d,
        's5_glu_w': s5_glu_w, 's5_glu_b': s5_glu_b,
        'gla_gw': gla_gw, 'gla_gb': gla_gb, 'gla_norm': gla_norm,
    }
    n_rows = x_sample.shape[1] // GRID_W
    row = jnp.repeat(jnp.arange(n_rows), GRID_W)
    col = jnp.tile(jnp.arange(GRID_W), n_rows)

    y_prompt, y_sample = x_prompt, x_sample
    ckv_l, krope_l, s5_l, gla_l = [], [], [], []
    for l in range(DEPTH):
        p = {name: w[l] for name, w in weights.items()}
        y_prompt, (ckv, krope, s5s, glas) = trunk_layer(y_prompt, c_ctx[None, :], p)
        ckv_l.append(ckv)
        krope_l.append(krope)
        s5_l.append(s5s)
        gla_l.append(glas)
        y_sample, _ = trunk_layer(y_sample, c, p, pos=(row, col),
                                  ctx=(cache_mla_ckv[:, l], cache_mla_krope[:, l],
                                       state_s5[:, l], state_gla[:, l]))
    new_mla_ckv = jnp.stack(ckv_l, axis=1)
    new_mla_krope = jnp.stack(krope_l, axis=1)
    new_s5 = jnp.stack(s5_l, axis=1)
    new_gla = jnp.stack(gla_l, axis=1)
    return (y_prompt, y_sample, new_mla_ckv, new_mla_krope, new_s5, new_gla)
```

```python
import functools
import math

import numpy as np
import jax
import jax.numpy as jnp
from jax import lax
from jax.experimental import pallas as pl
from jax.experimental.pallas import tpu as pltpu

F32 = jnp.float32
BF16 = jnp.bfloat16

D_MODEL = 1024
DEPTH = 2
GRID_W = 64
D_BRANCH = 256
EPS = 1e-6

MLA_HEADS = 4
MLA_Q_RANK = 192
MLA_KV_RANK = 128
MLA_NOPE = 64
MLA_ROPE = 32
MLA_QK = 96
MLA_V = 64
ROPE_BASE = 10000.0
HEAD_PAD = 128

HY_BANDS = 16
HY_FEAT = 33
HY_HIDDEN = 64
HY_SHIFT = 0.05
HY_FAST_DECAY = 0.3
HY_SLOW_DECAY = 1.5
HY_TARGET = 1e-2

S5_GROUP = 16
S5_GROUPS = 16
S5_STATE = 64
S5_N = S5_GROUPS * S5_STATE
S5_ROWS = 512
S5_SEG = 256

GLA_HEADS = 4
GLA_DK = 32
GLA_DV = 64
GLA_RANK = 16
GLA_TAU = 16.0
GLA_CHUNK = 64

SEG_MLA = (0, 512)
SEG_GATE = (512, 1536)
SEG_HY = (1536, 2304)
SEG_S5 = (2304, 2560)
SEG_GLA = (2560, 3200)
N_PROJ = 3200
KROPE_COL = 384 + 64

VMEM_LIMIT = 48 * 1024 * 1024


def _cp(*sem):
    return pltpu.CompilerParams(dimension_semantics=sem, vmem_limit_bytes=VMEM_LIMIT)


def _dot(a, b):
    return jnp.dot(a, b, preferred_element_type=F32)


def _dot_nt(a, b):
    return lax.dot_general(a, b, (((1,), (1,)), ((), ())), preferred_element_type=F32)


def _dot_tn(a, b):
    return lax.dot_general(a, b, (((0,), (0,)), ((), ())), preferred_element_type=F32)


def _split2(x):
    hi = x.astype(BF16)
    lo = (x - hi.astype(F32)).astype(BF16)
    return hi, lo


def _split3(x):
    h1 = x.astype(BF16)
    r1 = x - h1.astype(F32)
    h2 = r1.astype(BF16)
    h3 = (r1 - h2.astype(F32)).astype(BF16)
    return h1, h2, h3


def _dot3(a, b):
    a1, a2 = _split2(a)
    b1, b2 = _split2(b)
    return _dot(a1, b1) + (_dot(a1, b2) + _dot(a2, b1))


def _silu(z):
    return z / (1.0 + jnp.exp(-z))


def _mod_kernel(c_ref, w_ref, b_ref, o_ref):
    s = _silu(c_ref[...])
    o_ref[0] = _dot(s.astype(BF16), w_ref[0].astype(BF16)) + b_ref[0]


def _modulation(conds, ada_w, ada_b):
    d = D_MODEL
    return pl.pallas_call(
        _mod_kernel,
        grid=(DEPTH, 3),
        in_specs=[pl.BlockSpec((8, d), lambda l, j: (0, 0)),
                  pl.BlockSpec((1, d, d), lambda l, j: (l, 0, j)),
                  pl.BlockSpec((1, 1, d), lambda l, j: (l, 0, j))],
        out_specs=pl.BlockSpec((1, 8, d), lambda l, j: (l, 0, j)),
        out_shape=jax.ShapeDtypeStruct((DEPTH, 8, 3 * d), F32),
        compiler_params=_cp("arbitrary", "arbitrary"),
        name="modulation",
    )(conds, ada_w, ada_b.reshape(DEPTH, 1, 3 * d))


def _inproj_kernel(x_ref, mod_ref, nw_ref, w_ref, o_mla, o_g, o_hy, o_s5, o_gla):
    x = x_ref[...]
    ms = jnp.mean(x * x, axis=-1, keepdims=True)
    y = x * lax.rsqrt(ms + EPS) * nw_ref[...]
    h = (y * (1.0 + mod_ref[0, 1:2, :]) + mod_ref[0, 0:1, :]).astype(BF16)
    for o, (lo, hi) in ((o_mla, SEG_MLA), (o_g, SEG_GATE), (o_hy, SEG_HY), (o_s5, SEG_S5), (o_gla, SEG_GLA)):
        o[...] = _dot(h, w_ref[:, lo:hi])


def _inproj(x, mod, mod_row, norm_w, w_p, tm):
    n, d = x.shape
    widths = [hi - lo for lo, hi in (SEG_MLA, SEG_GATE, SEG_HY, SEG_S5, SEG_GLA)]
    return pl.pallas_call(
        _inproj_kernel,
        grid=(n // tm,),
        in_specs=[pl.BlockSpec((tm, d), lambda i: (i, 0)),
                  pl.BlockSpec((1, 3, d), lambda i: (mod_row(i), 0, 0)),
                  pl.BlockSpec((1, d), lambda i: (0, 0)),
                  pl.BlockSpec((d, N_PROJ), lambda i: (0, 0))],
        out_specs=[pl.BlockSpec((tm, w), lambda i: (i, 0)) for w in widths],
        out_shape=[jax.ShapeDtypeStruct((n, w), F32) for w in widths],
        compiler_params=_cp("arbitrary"),
        name="inproj",
    )(x, mod, norm_w, w_p)


def _head_norm(xh, w):
    ms = jnp.sum(xh * xh, axis=-1, keepdims=True) * (1.0 / MLA_QK)
    return xh * lax.rsqrt(ms + EPS) * w


def _rope(xh, cos, sin_a, sin_b):
    return xh * cos + pltpu.roll(xh, HEAD_PAD - 8, 1) * sin_a + pltpu.roll(xh, 8, 1) * sin_b


def _mla_prep_kernel(has_q, rope, *refs):
    refs = list(refs)
    m_ref = refs.pop(0)
    if has_q:
        qan_ref, wuq_ref, qn_ref, kvn_ref = refs[:4]
        refs = refs[4:]
    wuk_ref, wuv_ref, kn_ref = refs[:3]
    refs = refs[3:]
    if rope:
        cos_ref, sa_ref, sb_ref = refs[:3]
        refs = refs[3:]
        cos, sa, sb = cos_ref[...], sa_ref[...], sb_ref[...]
    if has_q:
        q_ref, k_ref, v_ref, ckv_ref = refs
    else:
        k_ref, v_ref = refs
    m = m_ref[...]
    if has_q:
        cq = m[:, 0:256]
        ms = jnp.sum(cq * cq, axis=-1, keepdims=True) * (1.0 / MLA_Q_RANK)
        cqn = cq * lax.rsqrt(ms + EPS) * qan_ref[...]
        q = _dot(cqn.astype(BF16), wuq_ref[...])
        ckv = m[:, 256:384]
        ckvn = ckv * lax.rsqrt(jnp.mean(ckv * ckv, axis=-1, keepdims=True) + EPS) * kvn_ref[...]
        ckv_ref[...] = ckvn
        kr = m[:, 384:512]
    else:
        ckvn = m[:, 0:128]
        kr = m[:, 128:256]
    cb = ckvn.astype(BF16)
    kup = _dot(cb, wuk_ref[...])
    v_ref[...] = _dot(cb, wuv_ref[...]).astype(BF16)
    for h in range(MLA_HEADS):
        sl = slice(HEAD_PAD * h, HEAD_PAD * (h + 1))
        kh = _head_norm(kup[:, sl] + kr, kn_ref[...])
        if rope:
            kh = _rope(kh, cos, sa, sb)
        k_ref[:, sl] = kh.astype(BF16)
        if has_q:
            qh = _head_norm(q[:, sl], qn_ref[...])
            if rope:
                qh = _rope(qh, cos, sa, sb)
            q_ref[:, sl] = (qh * (MLA_QK ** -0.5)).astype(BF16)


def _mla_prep(m, wl, rope_tabs, seq_len, tm, has_q):
    n, wm = m.shape
    rope = rope_tabs is not None
    full = lambda shape: pl.BlockSpec(shape, lambda i: (0,) * len(shape))
    args, specs = [m], [pl.BlockSpec((tm, wm), lambda i: (i, 0))]
    if has_q:
        args += [wl["qa_norm"], wl["w_uq"], wl["q_norm"], wl["kva_norm"]]
        specs += [full((1, 256)), full((256, 512)), full((1, 128)), full((1, 128))]
    args += [wl["w_uk"], wl["w_uv"], wl["k_norm"]]
    specs += [full((128, 512)), full((128, 256)), full((1, 128))]
    if rope:
        nt = seq_len // tm
        args += list(rope_tabs)
        specs += [pl.BlockSpec((tm, HEAD_PAD), lambda i: (i % nt, 0))] * 3
    row = lambda w: pl.BlockSpec((tm, w), lambda i: (i, 0))
    out_specs = [row(512), row(256)]
    out_shape = [jax.ShapeDtypeStruct((n, 512), BF16), jax.ShapeDtypeStruct((n, 256), BF16)]
    if has_q:
        out_specs = [row(512)] + out_specs + [row(128)]
        out_shape = [jax.ShapeDtypeStruct((n, 512), BF16)] + out_shape + [jax.ShapeDtypeStruct((n, 128), F32)]
    return pl.pallas_call(
        functools.partial(_mla_prep_kernel, has_q, rope),
        grid=(n // tm,),
        in_specs=specs, out_specs=out_specs, out_shape=out_shape,
        compiler_params=_cp("arbitrary"),
        name="mla_prep",
    )(*args)


def _attn_kernel(q_ref, k_ref, v_ref, o_ref):
    low = lax.broadcasted_iota(jnp.int32, (1, HEAD_PAD), 1) < MLA_V
    for pair in range(MLA_HEADS // 2):
        vp = v_ref[:, HEAD_PAD * pair:HEAD_PAD * (pair + 1)]
        zero = jnp.zeros_like(vp)
        v_half = (jnp.where(low, vp, zero), jnp.where(low, zero, vp))
        acc = None
        for j in range(2):
            sl = slice(HEAD_PAD * (2 * pair + j), HEAD_PAD * (2 * pair + j + 1))
            s = _dot_nt(q_ref[:, sl], k_ref[:, sl])
            p = jnp.exp(s - jnp.max(s, axis=-1, keepdims=True))
            o = _dot(p.astype(BF16), v_half[j]) / jnp.sum(p, axis=-1, keepdims=True)
            acc = o if acc is None else acc + o
        o_ref[:, HEAD_PAD * pair:HEAD_PAD * (pair + 1)] = acc


def _attention(q, k, v, batch, lq, lk, tq):
    nq = lq // tq
    return pl.pallas_call(
        _attn_kernel,
        grid=(batch, nq),
        in_specs=[pl.BlockSpec((tq, 512), lambda b, i: (b * nq + i, 0)),
                  pl.BlockSpec((lk, 512), lambda b, i: (b, 0)),
                  pl.BlockSpec((lk, 256), lambda b, i: (b, 0))],
        out_specs=pl.BlockSpec((tq, 256), lambda b, i: (b * nq + i, 0)),
        out_shape=jax.ShapeDtypeStruct((batch * lq, 256), F32),
        compiler_params=_cp("arbitrary", "arbitrary"),
        name="attention",
    )(q, k, v)


def _hy_conv_kernel(x_ref, w_ref, b_ref, z_ref):
    x = x_ref[...]
    n = x.shape[0]
    row = lax.broadcasted_iota(jnp.int32, (n, 1), 0)
    xm = jnp.where(row == 0, 0.0, pltpu.roll(x, 1, 0))
    xp = jnp.where(row == n - 1, 0.0, pltpu.roll(x, n - 1, 0))
    z_ref[...] = w_ref[0:1, :] * xm + w_ref[1:2, :] * x + w_ref[2:3, :] * xp + b_ref[...]


def _hy_conv(x, w, b, batch, seq_len):
    c = D_BRANCH
    return pl.pallas_call(
        _hy_conv_kernel,
        grid=(batch, 3),
        in_specs=[pl.BlockSpec((seq_len, c), lambda i, j: (i, j)),
                  pl.BlockSpec((3, c), lambda i, j: (0, j)),
                  pl.BlockSpec((1, c), lambda i, j: (0, j))],
        out_specs=pl.BlockSpec((seq_len, c), lambda i, j: (i, j)),
        out_shape=jax.ShapeDtypeStruct(x.shape, F32),
        compiler_params=_cp("arbitrary", "arbitrary"),
        name="hy_conv",
    )(x, w, b)


def _dft_fwd_kernel(a_ref, x_ref, o_ref):
    o_ref[...] = _dot(a_ref[...], x_ref[...].astype(BF16))


def _dft_fwd(f_mat, x, batch, col_block, width):
    m, k = f_mat.shape
    tm = min(m, 512)
    nm = m // tm
    return pl.pallas_call(
        _dft_fwd_kernel,
        grid=(batch, nm),
        in_specs=[pl.BlockSpec((tm, k), lambda b, i: (i, 0)),
                  pl.BlockSpec((k, width), lambda b, i: (b, col_block))],
        out_specs=pl.BlockSpec((tm, width), lambda b, i: (b * nm + i, 0)),
        out_shape=jax.ShapeDtypeStruct((batch * m, width), F32),
        compiler_params=_cp("arbitrary", "arbitrary"),
        name="dft_fwd",
    )(f_mat, x)


def _hy_inv_kernel(ft_ref, u_ref, kf_ref, x1_ref, v_ref, bias_ref, o_ref, y_sc):
    half = u_ref.shape[0] // 2
    blk = 256

    @pl.when(pl.program_id(1) == 0)
    def _():
        def body(i, carry):
            r = pl.multiple_of(i * blk, blk)
            ut, ub = u_ref[pl.ds(r, blk), :], u_ref[pl.ds(half + r, blk), :]
            kt, kb = kf_ref[pl.ds(r, blk), :], kf_ref[pl.ds(half + r, blk), :]
            row0 = (lax.broadcasted_iota(jnp.int32, (blk, 1), 0) + r) == 0
            tt, bb = ut * kt, ub * kb
            y_sc[pl.ds(r, blk), :] = jnp.where(row0, tt, tt - bb).astype(BF16)
            y_sc[pl.ds(half + r, blk), :] = jnp.where(row0, bb, ut * kb + ub * kt).astype(BF16)
            return carry
        lax.fori_loop(0, half // blk, body, 0)

    y = _dot(ft_ref[...], y_sc[...])
    o_ref[...] = x1_ref[...] * (y + bias_ref[0] * v_ref[...])


def _hy_inv(ft_mat, u, kf, order, x1, x1_col, v, v_col, bias, batch):
    seq_len, m2 = ft_mat.shape
    c = D_BRANCH
    tm = min(seq_len, 512)
    nm = seq_len // tm
    return pl.pallas_call(
        _hy_inv_kernel,
        grid=(batch, nm),
        in_specs=[pl.BlockSpec((tm, m2), lambda b, i: (i, 0)),
                  pl.BlockSpec((m2, c), lambda b, i: (b, 0)),
                  pl.BlockSpec((m2, c), lambda b, i: (0, order)),
                  pl.BlockSpec((tm, c), lambda b, i: (b * nm + i, x1_col)),
                  pl.BlockSpec((tm, c), lambda b, i: (b * nm + i, v_col)),
                  pl.BlockSpec((1, 1, c), lambda b, i: (order, 0, 0))],
        out_specs=pl.BlockSpec((tm, c), lambda b, i: (b * nm + i, 0)),
        out_shape=jax.ShapeDtypeStruct((batch * seq_len, c), F32),
        scratch_shapes=[pltpu.VMEM((m2, c), BF16)],
        compiler_params=_cp("arbitrary", "arbitrary"),
        name="hy_inv",
    )(ft_mat, u, kf, x1, v, bias.reshape(2, 1, c))


def _hy_mlp_kernel(feat_ref, w1_ref, b1_ref, f1_ref, w2_ref, b2_ref, f2_ref, w3_ref, win_ref,
                   filt_ref, nrm_ref):
    i = pl.program_id(0)
    tl = feat_ref.shape[0]
    h = jnp.sin(f1_ref[...] * (_dot3(feat_ref[...], w1_ref[...]) + b1_ref[...]))
    h = jnp.sin(f2_ref[...] * (_dot3(h, w2_ref[...]) + b2_ref[...]))
    filt = _dot3(h, w3_ref[...])
    win = win_ref[...]
    win4 = jnp.concatenate([win, win, win, win], axis=1)
    row0 = (lax.broadcasted_iota(jnp.int32, (tl, 1), 0) + i * tl) == 0
    look_ahead = lax.broadcasted_iota(jnp.int32, (1, 1024), 1) >= 512
    filt = jnp.where(jnp.logical_and(row0, look_ahead), 0.0, filt * win4)
    filt_ref[0] = filt[:, 0:512]
    filt_ref[1] = filt[:, 512:1024]
    part = jnp.sum(jnp.abs(filt[:, 0:512]) + jnp.abs(filt[:, 512:1024]), axis=0, keepdims=True)

    @pl.when(i == 0)
    def _():
        nrm_ref[...] = jnp.zeros_like(nrm_ref)
    nrm_ref[...] += jnp.broadcast_to(part, nrm_ref.shape)


def _hy_mlp(feat, wl, win):
    seq_len = feat.shape[0]
    tl = 256
    full = lambda shape: pl.BlockSpec(shape, lambda i: (0,) * len(shape))
    return pl.pallas_call(
        _hy_mlp_kernel,
        grid=(seq_len // tl,),
        in_specs=[pl.BlockSpec((tl, 128), lambda i: (i, 0)),
                  full((128, 128)), full((1, 128)), full((1, 128)),
                  full((128, 128)), full((1, 128)), full((1, 128)),
                  full((128, 1024)),
                  pl.BlockSpec((tl, 256), lambda i: (i, 0))],
        out_specs=[pl.BlockSpec((2, tl, 512), lambda i: (0, i, 0)), full((8, 512))],
        out_shape=[jax.ShapeDtypeStruct((2, seq_len, 512), F32), jax.ShapeDtypeStruct((8, 512), F32)],
        compiler_params=_cp("arbitrary"),
        name="hy_mlp",
    )(feat, wl["hy_w1"], wl["hy_b1"], wl["hy_f1"], wl["hy_w2"], wl["hy_b2"], wl["hy_f2"], wl["hy_w3"], win)


def _hy_kf_kernel(seq_len, a_ref, b_ref, n_ref, o_ref):
    i = pl.program_id(0)
    tr = a_ref.shape[0]
    grow = lax.broadcasted_iota(jnp.int32, (tr, 1), 0) + i * tr
    first = jnp.logical_or(grow == 0, grow == seq_len)
    imag = grow > seq_len
    a, b = a_ref[...], b_ref[...]
    kf = jnp.where(imag, a - b, a + b)
    scale = jnp.where(first, 0.5 / seq_len, 1.0 / seq_len) / n_ref[0:1, :]
    o_ref[...] = kf * scale


def _hy_kf(kraw, nrm, seq_len):
    m2 = 2 * seq_len
    tr = 512
    nt = m2 // tr
    return pl.pallas_call(
        functools.partial(_hy_kf_kernel, seq_len),
        grid=(nt,),
        in_specs=[pl.BlockSpec((tr, 512), lambda i: (i, 0)),
                  pl.BlockSpec((tr, 512), lambda i: (nt + i, 0)),
                  pl.BlockSpec((8, 512), lambda i: (0, 0))],
        out_specs=pl.BlockSpec((tr, 512), lambda i: (i, 0)),
        out_shape=jax.ShapeDtypeStruct((m2, 512), F32),
        compiler_params=_cp("arbitrary"),
        name="hy_kf",
    )(kraw, kraw, nrm)


def _s5_discretise(are_ref, aim_ref, ldt_ref):
    ar = jnp.minimum(are_ref[0], -1e-4)
    ai = aim_ref[0]
    dt = jnp.exp(ldt_ref[0])
    e = jnp.exp(ar * dt)
    return ar, ai, e * jnp.cos(ai * dt), e * jnp.sin(ai * dt)


def _s5_scan_kernel(nseq, emit_y, u_ref, hin_ref, are_ref, aim_ref, ldt_ref, bre_ref, bim_ref, *rest):
    if emit_y:
        cre_ref, cim_ref, y_ref, hfin_ref, wb_sc, ab_sc, s_sc, hc_sc, wc_sc = rest
    else:
        hfin_ref, wb_sc, ab_sc, s_sc, hc_sc = rest
    d = pl.program_id(0)
    c = pl.program_id(1)
    n = S5_N

    @pl.when(c == 0)
    def _():
        ar, ai, abr, abi = _s5_discretise(are_ref, aim_ref, ldt_ref)
        ab_sc[0:1, :] = abr
        ab_sc[1:2, :] = abi
        den = 1.0 / (ar * ar + ai * ai)
        cr = ((abr - 1.0) * ar + abi * ai) * den
        ci = (abi * ar - (abr - 1.0) * ai) * den
        bre, bim = bre_ref[0], bim_ref[0]
        wb_sc[:, 0:n] = (cr * bre - ci * bim).astype(BF16)
        wb_sc[:, n:2 * n] = (cr * bim + ci * bre).astype(BF16)
        if emit_y:
            wc_sc[0:n, :] = cre_ref[0].astype(BF16)
            wc_sc[n:2 * n, :] = (-cim_ref[0]).astype(BF16)
        hc_sc[...] = hin_ref[0]

    steps = s_sc.shape[0] // nseq
    s_sc[...] = _dot(u_ref[...].astype(BF16), wb_sc[...])
    lb = 256
    for j in range(n // lb):
        lr = slice(lb * j, lb * (j + 1))
        li = slice(n + lb * j, n + lb * (j + 1))
        abr = ab_sc[0:1, lr]
        abi = ab_sc[1:2, lr]

        def body(t, carry):
            hr, hi = carry
            idx = t + d * (steps - 1 - 2 * t)
            rows = pl.ds(pl.multiple_of(idx * nseq, nseq), nseq)
            nr = abr * hr - abi * hi + s_sc[rows, lr]
            ni = abr * hi + abi * hr + s_sc[rows, li]
            s_sc[rows, lr] = nr
            s_sc[rows, li] = ni
            return nr, ni

        hr, hi = lax.fori_loop(0, steps, body, (hc_sc[:, lr], hc_sc[:, li]))
        hc_sc[:, lr] = hr
        hc_sc[:, li] = hi
    if emit_y:
        y_ref[0] = _dot(s_sc[...].astype(BF16), wc_sc[...])

    @pl.when(c == pl.num_programs(1) - 1)
    def _():
        hfin_ref[0] = hc_sc[...]


def _s5_scan(u_tm, hin, wl, nseq, emit_y):
    rows_total = u_tm.shape[0]
    nc = rows_total // S5_ROWS
    n = S5_N
    chunk = lambda d, c: c + d * (nc - 1 - 2 * c)
    per_dir = lambda shape: pl.BlockSpec((1,) + shape, lambda d, c: (d,) + (0,) * len(shape))
    args = [u_tm, hin, wl["s5_are"], wl["s5_aim"], wl["s5_ldt"], wl["s5_bre"], wl["s5_bim"]]
    specs = [pl.BlockSpec((S5_ROWS, D_BRANCH), lambda d, c: (chunk(d, c), 0)),
             per_dir((nseq, 2 * n)), per_dir((1, n)), per_dir((1, n)), per_dir((1, n)),
             per_dir((D_BRANCH, n)), per_dir((D_BRANCH, n))]
    out_specs = [per_dir((nseq, 2 * n))]
    out_shape = [jax.ShapeDtypeStruct((2, nseq, 2 * n), F32)]
    scratch = [pltpu.VMEM((D_BRANCH, 2 * n), BF16), pltpu.VMEM((8, n), F32),
               pltpu.VMEM((S5_ROWS, 2 * n), F32), pltpu.VMEM((nseq, 2 * n), F32)]
    if emit_y:
        args += [wl["s5_cre"], wl["s5_cim"]]
        specs += [per_dir((n, D_BRANCH)), per_dir((n, D_BRANCH))]
        out_specs = [pl.BlockSpec((1, S5_ROWS, D_BRANCH), lambda d, c: (d, chunk(d, c), 0))] + out_specs
        out_shape = [jax.ShapeDtypeStruct((2, rows_total, D_BRANCH), F32)] + out_shape
        scratch += [pltpu.VMEM((2 * n, D_BRANCH), BF16)]
    return pl.pallas_call(
        functools.partial(_s5_scan_kernel, nseq, emit_y),
        grid=(2, nc),
        in_specs=specs, out_specs=out_specs, out_shape=out_shape, scratch_shapes=scratch,
        compiler_params=_cp("arbitrary", "arbitrary"),
        name="s5_scan" if emit_y else "s5_scan_finals",
    )(*args)


def _s5_chain_kernel(batch, nseg, f_ref, h0_ref, are_ref, aim_ref, ldt_ref, o_ref):
    d = pl.program_id(0)
    n = S5_N
    _, _, pr, pi = _s5_discretise(are_ref, aim_ref, ldt_ref)
    for _ in range(int(math.log2(S5_SEG))):
        pr, pi = pr * pr - pi * pi, 2.0 * pr * pi
    f = f_ref[0]
    fr, fi = f[:, 0:n], f[:, n:2 * n]
    h0 = h0_ref[0]
    h0r, h0i = h0[:, 0:n], h0[:, n:2 * n]
    nrow = batch * nseg
    row = lax.broadcasted_iota(jnp.int32, (nrow, 1), 0)

    def run(shift, keep):
        xr, xi = h0r, h0i
        for _ in range(nseg - 1):
            zr = fr + pr * xr - pi * xi
            zi = fi + pr * xi + pi * xr
            xr = h0r + jnp.where(keep, pltpu.roll(zr, shift, 0), 0.0)
            xi = h0i + jnp.where(keep, pltpu.roll(zi, shift, 0), 0.0)
        o_ref[0, :, 0:n] = xr
        o_ref[0, :, n:2 * n] = xi

    @pl.when(d == 0)
    def _():
        run(batch, row >= batch)

    @pl.when(d == 1)
    def _():
        run(nrow - batch, row < nrow - batch)


def _s5_chain(fin, h0rows, wl, batch, nseg):
    nrow = batch * nseg
    n = S5_N
    per_dir = lambda shape: pl.BlockSpec((1,) + shape, lambda d: (d,) + (0,) * len(shape))
    return pl.pallas_call(
        functools.partial(_s5_chain_kernel, batch, nseg),
        grid=(2,),
        in_specs=[per_dir((nrow, 2 * n)), per_dir((nrow, 2 * n)), per_dir((1, n)), per_dir((1, n)), per_dir((1, n))],
        out_specs=per_dir((nrow, 2 * n)),
        out_shape=jax.ShapeDtypeStruct((2, nrow, 2 * n), F32),
        compiler_params=_cp("arbitrary"),
        name="s5_chain",
    )(fin, h0rows, wl["s5_are"], wl["s5_aim"], wl["s5_ldt"])


def _s5_post_kernel(u_ref, yf_ref, yb_ref, d_ref, w_ref, b_ref, o_ref):
    y = d_ref[...] * u_ref[...] + yf_ref[0] + yb_ref[0]
    g = 0.5 * y * (1.0 + jnp.tanh(math.sqrt(2.0 / math.pi) * (y + 0.044715 * (y * y * y))))
    z = _dot(g.astype(BF16), w_ref[...]) + b_ref[...]
    o_ref[...] = g / (1.0 + jnp.exp(-z))


def _s5_post(u_tm, y2, wl):
    n = u_tm.shape[0]
    tm = 512
    c = D_BRANCH
    return pl.pallas_call(
        _s5_post_kernel,
        grid=(n // tm,),
        in_specs=[pl.BlockSpec((tm, c), lambda i: (i, 0)),
                  pl.BlockSpec((1, tm, c), lambda i: (0, i, 0)),
                  pl.BlockSpec((1, tm, c), lambda i: (1, i, 0)),
                  pl.BlockSpec((1, c), lambda i: (0, 0)),
                  pl.BlockSpec((c, c), lambda i: (0, 0)),
                  pl.BlockSpec((1, c), lambda i: (0, 0))],
        out_specs=pl.BlockSpec((tm, c), lambda i: (i, 0)),
        out_shape=jax.ShapeDtypeStruct((n, c), F32),
        compiler_params=_cp("arbitrary"),
        name="s5_post",
    )(u_tm, y2, y2, wl["s5_d"], wl["s5_glu_w"], wl["s5_glu_b"])


def _gla_kernel(seq_len, q_ref, k_ref, v_ref, g_ref, gw_ref, gb_ref, s0_ref, o_ref, sfin_ref, s_sc):
    d = pl.program_id(1)
    ck = GLA_CHUNK
    nchunk = seq_len // ck
    sign = 1 - 2 * d
    s_sc[...] = s0_ref[0, 0]
    ti = lax.broadcasted_iota(jnp.int32, (ck, ck), 0)
    si = lax.broadcasted_iota(jnp.int32, (ck, ck), 1)
    tri = jnp.where((si - ti) * sign <= 0, 1.0, 0.0).astype(BF16)
    t2 = lax.broadcasted_iota(jnp.int32, (ck, GLA_HEADS * ck), 0)
    s2 = jnp.bitwise_and(lax.broadcasted_iota(jnp.int32, (ck, GLA_HEADS * ck), 1), ck - 1)
    causal = (s2 - t2) * sign <= 0
    head_k = lax.shift_right_logical(lax.broadcasted_iota(jnp.int32, (1, GLA_HEADS * GLA_DK), 1), 5)
    head_v = lax.shift_right_logical(lax.broadcasted_iota(jnp.int32, (1, GLA_HEADS * GLA_DV), 1), 6)
    head_r = lax.shift_right_logical(lax.broadcasted_iota(jnp.int32, (GLA_HEADS * GLA_DK, 1), 0), 5)
    blockdiag = head_r == head_v

    def body(c, carry):
        ci = c + d * (nchunk - 1 - 2 * c)
        rows = pl.ds(pl.multiple_of(ci * ck, ck), ck)
        q = q_ref[rows, :] * (GLA_DK ** -0.5)
        k = k_ref[rows, :]
        v = v_ref[rows, :]
        x = _dot(g_ref[rows, :].astype(BF16), gw_ref[0]) + gb_ref[0]
        la = (jnp.minimum(x, 0.0) - jnp.log(1.0 + jnp.exp(-jnp.abs(x)))) * (1.0 / GLA_TAU)
        h1, h2, h3 = _split3(la)
        bc = _dot(tri, h1) + (_dot(tri, h2) + _dot(tri, h3))
        tot = jnp.sum(la, axis=0, keepdims=True)
        mid = bc[ck // 2:ck // 2 + 1, :]
        qt = (q * jnp.exp(bc - mid)).astype(BF16)
        kt = k * jnp.exp(mid - bc)
        kexp = jnp.concatenate([jnp.where(head_k == h, kt, 0.0) for h in range(GLA_HEADS)], axis=0).astype(BF16)
        att = jnp.where(causal, _dot_nt(qt, kexp), 0.0)
        vexp = jnp.concatenate([jnp.where(head_v == h, v, 0.0) for h in range(GLA_HEADS)], axis=0).astype(BF16)
        s = s_sc[...]
        o = _dot(att.astype(BF16), vexp) + _dot((q * jnp.exp(bc)).astype(BF16), s.astype(BF16))
        o_ref[0, rows, :] = o
        kl = (k * jnp.exp(tot - bc)).astype(BF16)
        upd = _dot_tn(kl, v.astype(BF16))
        dec = jnp.transpose(jnp.broadcast_to(jnp.exp(tot), (GLA_HEADS * GLA_DK, GLA_HEADS * GLA_DK)))
        s_sc[...] = jnp.concatenate([dec, dec], axis=1) * s + jnp.where(blockdiag, upd, 0.0)
        return carry

    lax.fori_loop(0, nchunk, body, 0)
    sfin_ref[0, 0] = s_sc[...]


def _gla(gla_in, wl, s0, batch, seq_len):
    n = gla_in.shape[0]
    dk, dv = GLA_HEADS * GLA_DK, GLA_HEADS * GLA_DV
    return pl.pallas_call(
        functools.partial(_gla_kernel, seq_len),
        grid=(batch, 2),
        in_specs=[pl.BlockSpec((seq_len, dk), lambda b, d: (b, 0)),
                  pl.BlockSpec((seq_len, dk), lambda b, d: (b, 1)),
                  pl.BlockSpec((seq_len, dv), lambda b, d: (b, 1)),
                  pl.BlockSpec((seq_len, dk), lambda b, d: (b, 4)),
                  pl.BlockSpec((1, dk, dk), lambda b, d: (d, 0, 0)),
                  pl.BlockSpec((1, 1, dk), lambda b, d: (d, 0, 0)),
                  pl.BlockSpec((1, 1, dk, dv), lambda b, d: (b, d, 0, 0))],
        out_specs=[pl.BlockSpec((1, seq_len, dv), lambda b, d: (d, b, 0)),
                   pl.BlockSpec((1, 1, dk, dv), lambda b, d: (b, d, 0, 0))],
        out_shape=[jax.ShapeDtypeStruct((2, n, dv), F32), jax.ShapeDtypeStruct((batch, 2, dk, dv), F32)],
        scratch_shapes=[pltpu.VMEM((dk, dv), F32)],
        compiler_params=_cp("arbitrary", "arbitrary"),
        name="gla",
    )(gla_in, gla_in, gla_in, gla_in, wl["gla_gw"], wl["gla_gb"], s0)


def _outproj_kernel(x_ref, mod_ref, g_ref, om_ref, oh_ref, os_ref, gf_ref, gb_ref, gn_ref, hm_ref, w_ref, y_ref):
    c = D_BRANCH
    g = g_ref[...]
    acc = _dot((om_ref[...] * _silu(g[:, 0:c])).astype(BF16), w_ref[0:c, :])
    acc += _dot((oh_ref[...] * _silu(g[:, c:2 * c])).astype(BF16), w_ref[c:2 * c, :])
    acc += _dot((os_ref[...] * _silu(g[:, 2 * c:3 * c])).astype(BF16), w_ref[2 * c:3 * c, :])
    og = gf_ref[0] + gb_ref[0]
    hi, lo = _split2(og * og)
    ms = _dot(hi, hm_ref[...]) + _dot(lo, hm_ref[...])
    ogn = og * lax.rsqrt(ms + EPS) * gn_ref[...]
    acc += _dot((ogn * _silu(g[:, 3 * c:4 * c])).astype(BF16), w_ref[3 * c:4 * c, :])
    y_ref[...] = x_ref[...] + mod_ref[0, 2:3, :] * acc


def _outproj(x, mod, mod_row, gates, o_mla, o_hy, o_s5, o_gla, wl, tm):
    n, d = x.shape
    c = D_BRANCH
    row = lambda w: pl.BlockSpec((tm, w), lambda i: (i, 0))
    return pl.pallas_call(
        _outproj_kernel,
        grid=(n // tm,),
        in_specs=[row(d),
                  pl.BlockSpec((1, 3, d), lambda i: (mod_row(i), 0, 0)),
                  row(d), row(c), row(c), row(c),
                  pl.BlockSpec((1, tm, c), lambda i: (0, i, 0)),
                  pl.BlockSpec((1, tm, c), lambda i: (1, i, 0)),
                  pl.BlockSpec((1, c), lambda i: (0, 0)),
                  pl.BlockSpec((c, c), lambda i: (0, 0)),
                  pl.BlockSpec((d, d), lambda i: (0, 0))],
        out_specs=row(d),
        out_shape=jax.ShapeDtypeStruct((n, d), F32),
        compiler_params=_cp("arbitrary"),
        name="outproj",
    )(x, mod, gates, o_mla, o_hy, o_s5, o_gla, o_gla, wl["gla_norm"], wl["head_mean"], wl["w_out"])


def _rope_tables(seq_len):
    pos = np.arange(seq_len)
    inv = ROPE_BASE ** (-np.arange(0, 16, 2, dtype=np.float64) / 16.0)
    cos = np.ones((seq_len, HEAD_PAD))
    sin_a = np.zeros((seq_len, HEAD_PAD))
    sin_b = np.zeros((seq_len, HEAD_PAD))
    for base, p in ((MLA_NOPE, pos // GRID_W), (MLA_NOPE + 16, pos % GRID_W)):
        ang = p[:, None].astype(np.float64) * inv[None, :]
        cos[:, base:base + 8] = np.cos(ang)
        cos[:, base + 8:base + 16] = np.cos(ang)
        sin_a[:, base:base + 8] = -np.sin(ang)
        sin_b[:, base + 8:base + 16] = np.sin(ang)
    return tuple(jnp.asarray(t, F32) for t in (cos, sin_a, sin_b))


def _dft_matrices(seq_len):
    n2 = 2 * seq_len
    j = np.arange(seq_len)[:, None]
    t = np.arange(seq_len)[None, :]
    ang = (2.0 * np.pi / n2) * ((j * t) % n2)
    top = np.cos(ang)
    bot = -np.sin(ang)
    bot[0, :] = np.where(np.arange(seq_len) % 2 == 0, 1.0, -1.0)
    f = np.concatenate([top, bot], axis=0)
    return jnp.asarray(f, F32).astype(BF16), jnp.asarray(f.T, F32).astype(BF16)


def _hyena_tables(seq_len):
    pos = np.arange(seq_len, dtype=np.float64)
    t = pos / seq_len
    w = 2.0 * np.pi * pos / seq_len
    bands = np.linspace(1e-4, HY_BANDS - 1, HY_BANDS)
    feat = np.zeros((seq_len, 128))
    feat[:, 0] = t
    feat[:, 1:1 + HY_BANDS] = np.cos(w[:, None] * bands)
    feat[:, 1 + HY_BANDS:HY_FEAT] = np.sin(w[:, None] * bands)
    deltas = np.linspace(math.log(1.0 / HY_TARGET) / HY_FAST_DECAY, math.log(1.0 / HY_TARGET) / HY_SLOW_DECAY,
                         D_BRANCH)
    win = np.exp(-t[:, None] * deltas[None, :]) + HY_SHIFT
    return jnp.asarray(feat, F32), jnp.asarray(win, F32)


def _pad_to(a, shape):
    return jnp.pad(a, [(0, s - d) for s, d in zip(shape, a.shape)])


def _layer_weights(l, p):
    z = lambda *s: jnp.zeros(s, F32)
    w_in = p["w_in"][l]
    col = lambda lo, hi: w_in[:, lo:hi]
    d = D_MODEL
    w_p = jnp.concatenate([
        col(0, 192), z(d, 64), col(192, 320), z(d, 64), col(320, 352), z(d, 32),
        col(352, 608), col(1376, 1632), col(1888, 2144), col(2688, 2944),
        col(608, 1376), col(1632, 1888),
        col(2144, 2272), col(2272, 2400), col(2400, 2656), col(2656, 2688), z(d, 96)], axis=1).astype(BF16)
    wl = {"w_in": w_p, "norm_w": p["norm_w"][l].reshape(1, d), "w_out": p["w_out"][l].astype(BF16)}
    wl["qa_norm"] = _pad_to(p["mla_qa_norm"][l].reshape(1, -1), (1, 256))
    w_uq = _pad_to(p["mla_w_uq"][l].reshape(MLA_Q_RANK, MLA_HEADS, MLA_QK), (256, MLA_HEADS, HEAD_PAD))
    wl["w_uq"] = w_uq.reshape(256, MLA_HEADS * HEAD_PAD).astype(BF16)
    wl["q_norm"] = _pad_to(p["mla_q_norm"][l].reshape(1, -1), (1, HEAD_PAD))
    wl["k_norm"] = _pad_to(p["mla_k_norm"][l].reshape(1, -1), (1, HEAD_PAD))
    wl["kva_norm"] = p["mla_kva_norm"][l].reshape(1, -1)
    w_ukv = p["mla_w_ukv"][l].reshape(MLA_KV_RANK, MLA_HEADS, MLA_NOPE + MLA_V)
    wl["w_uk"] = _pad_to(w_ukv[:, :, :MLA_NOPE], (MLA_KV_RANK, MLA_HEADS, HEAD_PAD)).reshape(MLA_KV_RANK, -1).astype(BF16)
    wl["w_uv"] = w_ukv[:, :, MLA_NOPE:].reshape(MLA_KV_RANK, MLA_HEADS * MLA_V).astype(BF16)
    wl["hy_conv_w"] = p["hy_conv_w"][l]
    wl["hy_conv_b"] = p["hy_conv_b"][l].reshape(1, -1)
    wl["hy_w1"] = _pad_to(p["hy_w1"][l], (128, 128))
    wl["hy_b1"] = _pad_to(p["hy_b1"][l].reshape(1, -1), (1, 128))
    wl["hy_f1"] = _pad_to(p["hy_freq1"][l].reshape(1, -1), (1, 128))
    wl["hy_w2"] = _pad_to(p["hy_w2"][l], (128, 128))
    wl["hy_b2"] = _pad_to(p["hy_b2"][l].reshape(1, -1), (1, 128))
    wl["hy_f2"] = _pad_to(p["hy_freq2"][l].reshape(1, -1), (1, 128))
    wl["hy_w3"] = _pad_to(p["hy_w3"][l], (128, 1024))
    wl["hy_bias"] = p["hy_bias"][l]
    eye_g = jnp.eye(S5_GROUPS, dtype=F32)
    flat = lambda a: a[l].reshape(2, 1, S5_N)
    wl["s5_are"], wl["s5_aim"] = flat(p["s5_a_re"]), flat(p["s5_a_im"])
    wl["s5_ldt"] = jnp.repeat(p["s5_log_dt"][l], S5_STATE, axis=-1).reshape(2, 1, S5_N)
    bd_b = lambda a: jnp.einsum("dgpi,gh->dgihp", a[l], eye_g).reshape(2, D_BRANCH, S5_N)
    bd_c = lambda a: jnp.einsum("dgip,gh->dgphi", a[l], eye_g).reshape(2, S5_N, D_BRANCH)
    wl["s5_bre"], wl["s5_bim"] = bd_b(p["s5_b_re"]), bd_b(p["s5_b_im"])
    wl["s5_cre"], wl["s5_cim"] = bd_c(p["s5_c_re"]), bd_c(p["s5_c_im"])
    wl["s5_d"] = p["s5_d"][l].reshape(1, -1)
    wl["s5_glu_w"] = p["s5_glu_w"][l].astype(BF16)
    wl["s5_glu_b"] = p["s5_glu_b"][l].reshape(1, -1)
    gw = p["gla_gw"][l]
    dk = GLA_HEADS * GLA_DK
    wl["gla_gw"] = jnp.stack([_pad_to(jnp.pad(gw[i], ((GLA_RANK * i, 0), (0, 0))), (dk, dk)) for i in range(2)]).astype(BF16)
    wl["gla_gb"] = p["gla_gb"][l].reshape(2, 1, dk)
    wl["gla_norm"] = jnp.tile(p["gla_norm"][l], GLA_HEADS).reshape(1, -1)
    head = np.arange(D_BRANCH) // GLA_DV
    wl["head_mean"] = jnp.asarray((head[:, None] == head[None, :]) / GLA_DV, BF16)
    return wl


def _hyena_filters(wl, tabs):
    feat, win, f_mat, _ = tabs
    seq_len = feat.shape[0]
    filt, nrm = _hy_mlp(feat, wl, win)
    kraw = _dft_fwd(f_mat, filt.reshape(2 * seq_len, 512), 2, 0, 512)
    return _hy_kf(kraw, nrm, seq_len)


def _trunk_layer(x, mod, mod_row, wl, batch, seq_len, hy_tabs, rope_tabs=None, ctx=None):
    n = batch * seq_len
    tm = 512 if seq_len % 512 == 0 else seq_len
    mla_in, gates, hy_in, s5_in, gla_in = _inproj(x, mod, mod_row, wl["norm_w"], wl["w_in"], tm)

    q, k, v, ckv = _mla_prep(mla_in, wl, rope_tabs, seq_len, min(seq_len, 512), True)
    lk = seq_len
    if ctx is not None:
        k_ctx, v_ctx = _mla_prep(ctx["mla"], wl, None, ctx["past"], 512, False)
        cat = lambda a, b: jnp.concatenate([a.reshape(batch, ctx["past"], -1), b.reshape(batch, seq_len, -1)], axis=1)
        lk = ctx["past"] + seq_len
        k = cat(k_ctx, k).reshape(batch * lk, -1)
        v = cat(v_ctx, v).reshape(batch * lk, -1)
    o_mla = _attention(q, k, v, batch, seq_len, lk, 256)

    _, _, f_mat, ft_mat = hy_tabs
    kf = _hyena_filters(wl, hy_tabs)
    z = _hy_conv(hy_in, wl["hy_conv_w"], wl["hy_conv_b"], batch, seq_len)
    u = _dft_fwd(f_mat, z, batch, 0, D_BRANCH)
    y1 = _hy_inv(ft_mat, u, kf, 0, z, 1, z, 0, wl["hy_bias"], batch)
    u = _dft_fwd(f_mat, y1, batch, 0, D_BRANCH)
    o_hy = _hy_inv(ft_mat, u, kf, 1, z, 2, y1, 0, wl["hy_bias"], batch)

    nseg = seq_len // S5_SEG
    nseq = batch * nseg
    to_tm = lambda a: a.reshape(batch, nseg, S5_SEG, -1).transpose(2, 1, 0, 3).reshape(n, -1)
    from_tm = lambda a: a.reshape(S5_SEG, nseg, batch, -1).transpose(2, 1, 0, 3).reshape(n, -1)
    u_tm = to_tm(s5_in)
    if ctx is None:
        hin = jnp.zeros((2, nseq, 2 * S5_N), F32)
    else:
        (fin,) = _s5_scan(u_tm, jnp.zeros((2, nseq, 2 * S5_N), F32), wl, nseq, False)
        hin = _s5_chain(fin, ctx["s5_h0"], wl, batch, nseg)
    y2, s5_fin = _s5_scan(u_tm, hin, wl, nseq, True)
    o_s5 = from_tm(_s5_post(u_tm, y2, wl))

    s0 = jnp.zeros((batch, 2, GLA_HEADS * GLA_DK, GLA_HEADS * GLA_DV), F32) if ctx is None else ctx["gla_s0"]
    o_gla, gla_fin = _gla(gla_in, wl, s0, batch, seq_len)

    y = _outproj(x, mod, mod_row, gates, o_mla, o_hy, o_s5, o_gla, wl, tm)
    return y, (ckv, mla_in[:, KROPE_COL:KROPE_COL + MLA_ROPE], s5_fin, gla_fin)


def kernel(x_prompt, x_sample, c, cache_mla_ckv, cache_mla_krope, state_s5, state_gla, c_ctx, norm_w, ada_w, ada_b, w_in, w_out, mla_qa_norm, mla_kva_norm, mla_w_uq, mla_w_ukv, mla_q_norm, mla_k_norm, hy_conv_w, hy_conv_b, hy_w1, hy_b1, hy_freq1, hy_w2, hy_b2, hy_freq2, hy_w3, hy_bias, s5_a_re, s5_a_im, s5_log_dt, s5_b_re, s5_b_im, s5_c_re, s5_c_im, s5_d, s5_glu_w, s5_glu_b, gla_gw, gla_gb, gla_norm):
    params = dict(norm_w=norm_w, w_in=w_in, w_out=w_out, mla_qa_norm=mla_qa_norm, mla_kva_norm=mla_kva_norm,
                  mla_w_uq=mla_w_uq, mla_w_ukv=mla_w_ukv, mla_q_norm=mla_q_norm, mla_k_norm=mla_k_norm,
                  hy_conv_w=hy_conv_w, hy_conv_b=hy_conv_b, hy_w1=hy_w1, hy_b1=hy_b1, hy_freq1=hy_freq1,
                  hy_w2=hy_w2, hy_b2=hy_b2, hy_freq2=hy_freq2, hy_w3=hy_w3, hy_bias=hy_bias,
                  s5_a_re=s5_a_re, s5_a_im=s5_a_im, s5_log_dt=s5_log_dt, s5_b_re=s5_b_re, s5_b_im=s5_b_im,
                  s5_c_re=s5_c_re, s5_c_im=s5_c_im, s5_d=s5_d, s5_glu_w=s5_glu_w, s5_glu_b=s5_glu_b,
                  gla_gw=gla_gw, gla_gb=gla_gb, gla_norm=gla_norm)
    bp, lp, d = x_prompt.shape
    bs, ls, _ = x_sample.shape
    past = cache_mla_ckv.shape[2]
    n_s5 = S5_N

    conds = jnp.concatenate([c_ctx[None, :], c, jnp.zeros((8 - 1 - bs, d), F32)], axis=0)
    mods = _modulation(conds, ada_w, ada_b).reshape(DEPTH, 8, 3, d)

    tabs_p = _hyena_tables(lp) + _dft_matrices(lp)
    tabs_s = _hyena_tables(ls) + _dft_matrices(ls)
    rope_tabs = _rope_tables(ls)
    tm_s = 512
    nseg = ls // S5_SEG

    y_p = x_prompt.reshape(bp * lp, d)
    y_s = x_sample.reshape(bs * ls, d)
    ckv_l, krope_l, s5_l, gla_l = [], [], [], []
    for l in range(DEPTH):
        wl = _layer_weights(l, params)
        y_p, (ckv, krope, s5_fin, gla_fin) = _trunk_layer(y_p, mods[l], lambda i: 0, wl, bp, lp, tabs_p)
        ckv_l.append(ckv.reshape(bp, lp, -1))
        krope_l.append(krope.reshape(bp, lp, -1))
        s5_l.append(jnp.stack([s5_fin[:, :, :n_s5], s5_fin[:, :, n_s5:]], axis=-1)
                    .reshape(2, bp, S5_GROUPS, S5_STATE, 2).transpose(1, 0, 2, 3, 4))
        gf = gla_fin.reshape(bp, 2, GLA_HEADS, GLA_DK, GLA_HEADS, GLA_DV)
        gla_l.append(jnp.stack([gf[:, :, h, :, h, :] for h in range(GLA_HEADS)], axis=2))

        mla_ctx = jnp.concatenate([cache_mla_ckv[:, l], jnp.zeros((bs, past, 64), F32), cache_mla_krope[:, l],
                                   jnp.zeros((bs, past, 32), F32)], axis=-1).reshape(bs * past, 256)
        st = state_s5[:, l]
        h0 = jnp.concatenate([st[..., 0].reshape(bs, 2, n_s5), st[..., 1].reshape(bs, 2, n_s5)], axis=-1)
        h0 = h0.transpose(1, 0, 2)
        h0rows = jnp.zeros((2, nseg * bs, 2 * n_s5), F32)
        h0rows = h0rows.at[0, :bs].set(h0[0]).at[1, (nseg - 1) * bs:].set(h0[1])
        eye_h = jnp.eye(GLA_HEADS, dtype=F32)
        gla_s0 = jnp.einsum("bdhke,hg->bdhkge", state_gla[:, l], eye_h).reshape(
            bs, 2, GLA_HEADS * GLA_DK, GLA_HEADS * GLA_DV)
        ctx = {"mla": mla_ctx, "past": past, "s5_h0": h0rows, "gla_s0": gla_s0}
        y_s, _ = _trunk_layer(y_s, mods[l], lambda i: 1 + (i * tm_s) // ls, wl, bs, ls, tabs_s, rope_tabs, ctx)

    return (y_p.reshape(bp, lp, d), y_s.reshape(bs, ls, d),
            jnp.stack(ckv_l, axis=1), jnp.stack(krope_l, axis=1), jnp.stack(s5_l, axis=1), jnp.stack(gla_l, axis=1))
```

```python
import functools
import math

import numpy as np
import jax
import jax.numpy as jnp
from jax import lax
from jax.experimental import pallas as pl
from jax.experimental.pallas import tpu as pltpu

F32 = jnp.float32
BF16 = jnp.bfloat16

D_MODEL = 1024
DEPTH = 2
GRID_W = 64
D_BRANCH = 256
EPS = 1e-6

MLA_HEADS = 4
MLA_Q_RANK = 192
MLA_KV_RANK = 128
MLA_NOPE = 64
MLA_ROPE = 32
MLA_QK = 96
MLA_V = 64
ROPE_BASE = 10000.0
HEAD_PAD = 128

HY_BANDS = 16
HY_FEAT = 33
HY_HIDDEN = 64
HY_SHIFT = 0.05
HY_FAST_DECAY = 0.3
HY_SLOW_DECAY = 1.5
HY_TARGET = 1e-2

S5_GROUP = 16
S5_GROUPS = 16
S5_STATE = 64
S5_N = S5_GROUPS * S5_STATE
S5_ROWS = 512
S5_SEG = 256

GLA_HEADS = 4
GLA_DK = 32
GLA_DV = 64
GLA_RANK = 16
GLA_TAU = 16.0
GLA_CHUNK = 64
GLA_SUPER = 256

SEG_MLA = (0, 512)
SEG_GATE = (512, 1536)
SEG_HY = (1536, 2304)
SEG_S5 = (2304, 2560)
SEG_GLA = (2560, 3200)
N_PROJ = 3200
KROPE_COL = 384 + 64

VMEM_LIMIT = 48 * 1024 * 1024


def _cp(*sem):
    return pltpu.CompilerParams(dimension_semantics=sem, vmem_limit_bytes=VMEM_LIMIT)


def _dot(a, b):
    return jnp.dot(a, b, preferred_element_type=F32)


def _dot_nt(a, b):
    return lax.dot_general(a, b, (((1,), (1,)), ((), ())), preferred_element_type=F32)


def _dot_tn(a, b):
    return lax.dot_general(a, b, (((0,), (0,)), ((), ())), preferred_element_type=F32)


def _split2(x):
    hi = x.astype(BF16)
    lo = (x - hi.astype(F32)).astype(BF16)
    return hi, lo


def _split3(x):
    h1 = x.astype(BF16)
    r1 = x - h1.astype(F32)
    h2 = r1.astype(BF16)
    h3 = (r1 - h2.astype(F32)).astype(BF16)
    return h1, h2, h3


def _dot3(a, b):
    a1, a2 = _split2(a)
    b1, b2 = _split2(b)
    return _dot(a1, b1) + (_dot(a1, b2) + _dot(a2, b1))


def _silu(z):
    return z / (1.0 + jnp.exp(-z))


def _mod_kernel(c_ref, w_ref, b_ref, o_ref):
    s = _silu(c_ref[...])
    o_ref[0] = _dot(s.astype(BF16), w_ref[0].astype(BF16)) + b_ref[0]


def _modulation(conds, ada_w, ada_b):
    d = D_MODEL
    return pl.pallas_call(
        _mod_kernel,
        grid=(DEPTH, 3),
        in_specs=[pl.BlockSpec((8, d), lambda l, j: (0, 0)),
                  pl.BlockSpec((1, d, d), lambda l, j: (l, 0, j)),
                  pl.BlockSpec((1, 1, d), lambda l, j: (l, 0, j))],
        out_specs=pl.BlockSpec((1, 8, d), lambda l, j: (l, 0, j)),
        out_shape=jax.ShapeDtypeStruct((DEPTH, 8, 3 * d), F32),
        compiler_params=_cp("arbitrary", "arbitrary"),
        name="modulation",
    )(conds, ada_w, ada_b.reshape(DEPTH, 1, 3 * d))


def _inproj_kernel(x_ref, mod_ref, nw_ref, w_ref, o_mla, o_g, o_hy, o_s5, o_gla):
    x = x_ref[...]
    ms = jnp.mean(x * x, axis=-1, keepdims=True)
    y = x * lax.rsqrt(ms + EPS) * nw_ref[...]
    h = (y * (1.0 + mod_ref[0, 1:2, :]) + mod_ref[0, 0:1, :]).astype(BF16)
    for o, (lo, hi) in ((o_mla, SEG_MLA), (o_g, SEG_GATE), (o_hy, SEG_HY), (o_s5, SEG_S5), (o_gla, SEG_GLA)):
        o[...] = _dot(h, w_ref[:, lo:hi])


def _inproj(x, mod, mod_row, norm_w, w_p, tm):
    n, d = x.shape
    widths = [hi - lo for lo, hi in (SEG_MLA, SEG_GATE, SEG_HY, SEG_S5, SEG_GLA)]
    return pl.pallas_call(
        _inproj_kernel,
        grid=(n // tm,),
        in_specs=[pl.BlockSpec((tm, d), lambda i: (i, 0)),
                  pl.BlockSpec((1, 3, d), lambda i: (mod_row(i), 0, 0)),
                  pl.BlockSpec((1, d), lambda i: (0, 0)),
                  pl.BlockSpec((d, N_PROJ), lambda i: (0, 0))],
        out_specs=[pl.BlockSpec((tm, w), lambda i: (i, 0)) for w in widths],
        out_shape=[jax.ShapeDtypeStruct((n, w), F32) for w in widths],
        compiler_params=_cp("arbitrary"),
        name="inproj",
    )(x, mod, norm_w, w_p)


def _head_norm(xh, w):
    ms = jnp.sum(xh * xh, axis=-1, keepdims=True) * (1.0 / MLA_QK)
    return xh * lax.rsqrt(ms + EPS) * w


def _rope(xh, cos, sin_a, sin_b):
    return xh * cos + pltpu.roll(xh, HEAD_PAD - 8, 1) * sin_a + pltpu.roll(xh, 8, 1) * sin_b


def _mla_prep_kernel(has_q, rope, *refs):
    refs = list(refs)
    m_ref = refs.pop(0)
    if has_q:
        qan_ref, wuq_ref, qn_ref, kvn_ref = refs[:4]
        refs = refs[4:]
    wuk_ref, wuv_ref, kn_ref = refs[:3]
    refs = refs[3:]
    if rope:
        cos_ref, sa_ref, sb_ref = refs[:3]
        refs = refs[3:]
        cos, sa, sb = cos_ref[...], sa_ref[...], sb_ref[...]
    if has_q:
        q_ref, k_ref, v_ref, ckv_ref = refs
    else:
        k_ref, v_ref = refs
    m = m_ref[...]
    if has_q:
        cq = m[:, 0:256]
        ms = jnp.sum(cq * cq, axis=-1, keepdims=True) * (1.0 / MLA_Q_RANK)
        cqn = cq * lax.rsqrt(ms + EPS) * qan_ref[...]
        q = _dot(cqn.astype(BF16), wuq_ref[...])
        ckv = m[:, 256:384]
        ckvn = ckv * lax.rsqrt(jnp.mean(ckv * ckv, axis=-1, keepdims=True) + EPS) * kvn_ref[...]
        ckv_ref[...] = ckvn
        kr = m[:, 384:512]
    else:
        ckvn = m[:, 0:128]
        kr = m[:, 128:256]
    cb = ckvn.astype(BF16)
    kup = _dot(cb, wuk_ref[...])
    v_ref[...] = _dot(cb, wuv_ref[...]).astype(BF16)
    for h in range(MLA_HEADS):
        sl = slice(HEAD_PAD * h, HEAD_PAD * (h + 1))
        kh = _head_norm(kup[:, sl] + kr, kn_ref[...])
        if rope:
            kh = _rope(kh, cos, sa, sb)
        k_ref[:, sl] = kh.astype(BF16)
        if has_q:
            qh = _head_norm(q[:, sl], qn_ref[...])
            if rope:
                qh = _rope(qh, cos, sa, sb)
            q_ref[:, sl] = (qh * (MLA_QK ** -0.5)).astype(BF16)


def _mla_prep(m, wl, rope_tabs, seq_len, tm, has_q):
    n, wm = m.shape
    rope = rope_tabs is not None
    full = lambda shape: pl.BlockSpec(shape, lambda i: (0,) * len(shape))
    args, specs = [m], [pl.BlockSpec((tm, wm), lambda i: (i, 0))]
    if has_q:
        args += [wl["qa_norm"], wl["w_uq"], wl["q_norm"], wl["kva_norm"]]
        specs += [full((1, 256)), full((256, 512)), full((1, 128)), full((1, 128))]
    args += [wl["w_uk"], wl["w_uv"], wl["k_norm"]]
    specs += [full((128, 512)), full((128, 256)), full((1, 128))]
    if rope:
        nt = seq_len // tm
        args += list(rope_tabs)
        specs += [pl.BlockSpec((tm, HEAD_PAD), lambda i: (i % nt, 0))] * 3
    row = lambda w: pl.BlockSpec((tm, w), lambda i: (i, 0))
    out_specs = [row(512), row(256)]
    out_shape = [jax.ShapeDtypeStruct((n, 512), BF16), jax.ShapeDtypeStruct((n, 256), BF16)]
    if has_q:
        out_specs = [row(512)] + out_specs + [row(128)]
        out_shape = [jax.ShapeDtypeStruct((n, 512), BF16)] + out_shape + [jax.ShapeDtypeStruct((n, 128), F32)]
    return pl.pallas_call(
        functools.partial(_mla_prep_kernel, has_q, rope),
        grid=(n // tm,),
        in_specs=specs, out_specs=out_specs, out_shape=out_shape,
        compiler_params=_cp("arbitrary"),
        name="mla_prep",
    )(*args)


def _attn_kernel(nparts, q_ref, *refs):
    kv = [(refs[2 * i], refs[2 * i + 1]) for i in range(nparts)]
    o_ref = refs[2 * nparts]
    low = lax.broadcasted_iota(jnp.int32, (1, HEAD_PAD), 1) < MLA_V
    for pair in range(MLA_HEADS // 2):
        v_half = []
        for _, v_ref in kv:
            vp = v_ref[:, HEAD_PAD * pair:HEAD_PAD * (pair + 1)]
            zero = jnp.zeros_like(vp)
            v_half.append((jnp.where(low, vp, zero), jnp.where(low, zero, vp)))
        acc = None
        for j in range(2):
            sl = slice(HEAD_PAD * (2 * pair + j), HEAD_PAD * (2 * pair + j + 1))
            s = [_dot_nt(q_ref[:, sl], k_ref[:, sl]) for k_ref, _ in kv]
            m = functools.reduce(jnp.maximum, [jnp.max(x, axis=-1, keepdims=True) for x in s])
            p = [jnp.exp(x - m) for x in s]
            den = functools.reduce(jnp.add, [jnp.sum(x, axis=-1, keepdims=True) for x in p])
            num = functools.reduce(jnp.add, [_dot(x.astype(BF16), vh[j]) for x, vh in zip(p, v_half)])
            o = num / den
            acc = o if acc is None else acc + o
        o_ref[:, HEAD_PAD * pair:HEAD_PAD * (pair + 1)] = acc


def _attention(q, kv_parts, batch, lq, tq):
    nq = lq // tq
    args, specs = [q], [pl.BlockSpec((tq, 512), lambda b, i: (b * nq + i, 0))]
    for k, v, lk in kv_parts:
        args += [k, v]
        specs += [pl.BlockSpec((lk, 512), lambda b, i: (b, 0)), pl.BlockSpec((lk, 256), lambda b, i: (b, 0))]
    return pl.pallas_call(
        functools.partial(_attn_kernel, len(kv_parts)),
        grid=(batch, nq),
        in_specs=specs,
        out_specs=pl.BlockSpec((tq, 256), lambda b, i: (b * nq + i, 0)),
        out_shape=jax.ShapeDtypeStruct((batch * lq, 256), F32),
        compiler_params=_cp("arbitrary", "arbitrary"),
        name="attention",
    )(*args)


def _hy_conv_kernel(x_ref, w_ref, b_ref, z_ref):
    x = x_ref[...]
    n = x.shape[0]
    row = lax.broadcasted_iota(jnp.int32, (n, 1), 0)
    xm = jnp.where(row == 0, 0.0, pltpu.roll(x, 1, 0))
    xp = jnp.where(row == n - 1, 0.0, pltpu.roll(x, n - 1, 0))
    z_ref[...] = w_ref[0:1, :] * xm + w_ref[1:2, :] * x + w_ref[2:3, :] * xp + b_ref[...]


def _hy_conv(x, w, b, batch, seq_len):
    c = D_BRANCH
    return pl.pallas_call(
        _hy_conv_kernel,
        grid=(batch, 3),
        in_specs=[pl.BlockSpec((seq_len, c), lambda i, j: (i, j)),
                  pl.BlockSpec((3, c), lambda i, j: (0, j)),
                  pl.BlockSpec((1, c), lambda i, j: (0, j))],
        out_specs=pl.BlockSpec((seq_len, c), lambda i, j: (i, j)),
        out_shape=jax.ShapeDtypeStruct(x.shape, F32),
        compiler_params=_cp("arbitrary", "arbitrary"),
        name="hy_conv",
    )(x, w, b)


def _hyena_fused_kernel(seq_len, nb, x_ref, cw_ref, cb_ref, f_ref, ft_ref, kf_ref, bias_ref, o_ref):
    c = D_BRANCH
    x = x_ref[...]
    n = x.shape[0]
    pos = jnp.bitwise_and(lax.broadcasted_iota(jnp.int32, (n, 1), 0), seq_len - 1)
    xm = jnp.where(pos == 0, 0.0, pltpu.roll(x, 1, 0))
    xp = jnp.where(pos == seq_len - 1, 0.0, pltpu.roll(x, n - 1, 0))
    z = cw_ref[0:1, :] * xm + cw_ref[1:2, :] * x + cw_ref[2:3, :] * xp + cb_ref[...]
    row0 = lax.broadcasted_iota(jnp.int32, (seq_len, 1), 0) == 0
    f, ft = f_ref[...], ft_ref[...]

    def long_conv(v, order):
        u = _dot(f, v.astype(BF16))
        ut, ub = u[0:seq_len], u[seq_len:2 * seq_len]
        kt = kf_ref[0:seq_len, order * c:(order + 1) * c]
        kb = kf_ref[seq_len:2 * seq_len, order * c:(order + 1) * c]
        tt, bb = ut * kt, ub * kb
        y = jnp.concatenate([jnp.where(row0, tt, tt - bb), jnp.where(row0, bb, ut * kb + ub * kt)], axis=0)
        return _dot(ft, y.astype(BF16))

    for j in range(nb):
        rows = slice(j * seq_len, (j + 1) * seq_len)
        v, x1, x2 = z[rows, 0:c], z[rows, c:2 * c], z[rows, 2 * c:3 * c]
        y1 = x1 * (long_conv(v, 0) + bias_ref[0:1, :] * v)
        o_ref[rows, :] = x2 * (long_conv(y1, 1) + bias_ref[1:2, :] * y1)


def _hyena_fused(x, wl, kf, f_mat, ft_mat, batch, seq_len):
    c = D_BRANCH
    nb = max(1, min(batch, 1024 // seq_len))
    rows = nb * seq_len
    full = lambda a: pl.BlockSpec(a.shape, lambda i: (0,) * a.ndim)
    return pl.pallas_call(
        functools.partial(_hyena_fused_kernel, seq_len, nb),
        grid=(batch // nb,),
        in_specs=[pl.BlockSpec((rows, 3 * c), lambda i: (i, 0)),
                  full(wl["hy_conv_w"]), full(wl["hy_conv_b"]), full(f_mat), full(ft_mat), full(kf),
                  full(wl["hy_bias"])],
        out_specs=pl.BlockSpec((rows, c), lambda i: (i, 0)),
        out_shape=jax.ShapeDtypeStruct((batch * seq_len, c), F32),
        compiler_params=_cp("arbitrary"),
        name="hyena_fused",
    )(x, wl["hy_conv_w"], wl["hy_conv_b"], f_mat, ft_mat, kf, wl["hy_bias"])


def _dft_fwd_kernel(a_ref, x_ref, o_ref):
    o_ref[...] = _dot(a_ref[...], x_ref[...].astype(BF16))


def _dft_fwd(f_mat, x, batch, col_block, width):
    m, k = f_mat.shape
    tm = min(m, 512)
    nm = m // tm
    return pl.pallas_call(
        _dft_fwd_kernel,
        grid=(batch, nm),
        in_specs=[pl.BlockSpec((tm, k), lambda b, i: (i, 0)),
                  pl.BlockSpec((k, width), lambda b, i: (b, col_block))],
        out_specs=pl.BlockSpec((tm, width), lambda b, i: (b * nm + i, 0)),
        out_shape=jax.ShapeDtypeStruct((batch * m, width), F32),
        compiler_params=_cp("arbitrary", "arbitrary"),
        name="dft_fwd",
    )(f_mat, x)


def _hy_inv_kernel(ft_ref, u_ref, kf_ref, x1_ref, v_ref, bias_ref, o_ref, y_sc):
    half = u_ref.shape[0] // 2
    blk = 256

    @pl.when(pl.program_id(1) == 0)
    def _():
        def body(i, carry):
            r = pl.multiple_of(i * blk, blk)
            ut, ub = u_ref[pl.ds(r, blk), :], u_ref[pl.ds(half + r, blk), :]
            kt, kb = kf_ref[pl.ds(r, blk), :], kf_ref[pl.ds(half + r, blk), :]
            row0 = (lax.broadcasted_iota(jnp.int32, (blk, 1), 0) + r) == 0
            tt, bb = ut * kt, ub * kb
            y_sc[pl.ds(r, blk), :] = jnp.where(row0, tt, tt - bb).astype(BF16)
            y_sc[pl.ds(half + r, blk), :] = jnp.where(row0, bb, ut * kb + ub * kt).astype(BF16)
            return carry
        lax.fori_loop(0, half // blk, body, 0)

    y = _dot(ft_ref[...], y_sc[...])
    o_ref[...] = x1_ref[...] * (y + bias_ref[0] * v_ref[...])


def _hy_inv(ft_mat, u, kf, order, x1, x1_col, v, v_col, bias, batch):
    seq_len, m2 = ft_mat.shape
    c = D_BRANCH
    tm = min(seq_len, 512)
    nm = seq_len // tm
    return pl.pallas_call(
        _hy_inv_kernel,
        grid=(batch, nm),
        in_specs=[pl.BlockSpec((tm, m2), lambda b, i: (i, 0)),
                  pl.BlockSpec((m2, c), lambda b, i: (b, 0)),
                  pl.BlockSpec((m2, c), lambda b, i: (0, order)),
                  pl.BlockSpec((tm, c), lambda b, i: (b * nm + i, x1_col)),
                  pl.BlockSpec((tm, c), lambda b, i: (b * nm + i, v_col)),
                  pl.BlockSpec((1, 1, c), lambda b, i: (order, 0, 0))],
        out_specs=pl.BlockSpec((tm, c), lambda b, i: (b * nm + i, 0)),
        out_shape=jax.ShapeDtypeStruct((batch * seq_len, c), F32),
        scratch_shapes=[pltpu.VMEM((m2, c), BF16)],
        compiler_params=_cp("arbitrary", "arbitrary"),
        name="hy_inv",
    )(ft_mat, u, kf, x1, v, bias.reshape(2, 1, c))


def _hy_mlp_kernel(feat_ref, w1_ref, b1_ref, f1_ref, w2_ref, b2_ref, f2_ref, w3_ref, win_ref,
                   filt_ref, nrm_ref):
    i = pl.program_id(0)
    tl = feat_ref.shape[0]
    h = jnp.sin(f1_ref[...] * (_dot3(feat_ref[...], w1_ref[...]) + b1_ref[...]))
    h = jnp.sin(f2_ref[...] * (_dot3(h, w2_ref[...]) + b2_ref[...]))
    filt = _dot3(h, w3_ref[...])
    win = win_ref[...]
    win4 = jnp.concatenate([win, win, win, win], axis=1)
    row0 = (lax.broadcasted_iota(jnp.int32, (tl, 1), 0) + i * tl) == 0
    look_ahead = lax.broadcasted_iota(jnp.int32, (1, 1024), 1) >= 512
    filt = jnp.where(jnp.logical_and(row0, look_ahead), 0.0, filt * win4)
    filt_ref[0] = filt[:, 0:512]
    filt_ref[1] = filt[:, 512:1024]
    part = jnp.sum(jnp.abs(filt[:, 0:512]) + jnp.abs(filt[:, 512:1024]), axis=0, keepdims=True)

    @pl.when(i == 0)
    def _():
        nrm_ref[...] = jnp.zeros_like(nrm_ref)
    nrm_ref[...] += jnp.broadcast_to(part, nrm_ref.shape)


def _hy_mlp(feat, wl, win):
    seq_len = feat.shape[0]
    tl = 256
    full = lambda shape: pl.BlockSpec(shape, lambda i: (0,) * len(shape))
    return pl.pallas_call(
        _hy_mlp_kernel,
        grid=(seq_len // tl,),
        in_specs=[pl.BlockSpec((tl, 128), lambda i: (i, 0)),
                  full((128, 128)), full((1, 128)), full((1, 128)),
                  full((128, 128)), full((1, 128)), full((1, 128)),
                  full((128, 1024)),
                  pl.BlockSpec((tl, 256), lambda i: (i, 0))],
        out_specs=[pl.BlockSpec((2, tl, 512), lambda i: (0, i, 0)), full((8, 512))],
        out_shape=[jax.ShapeDtypeStruct((2, seq_len, 512), F32), jax.ShapeDtypeStruct((8, 512), F32)],
        compiler_params=_cp("arbitrary"),
        name="hy_mlp",
    )(feat, wl["hy_w1"], wl["hy_b1"], wl["hy_f1"], wl["hy_w2"], wl["hy_b2"], wl["hy_f2"], wl["hy_w3"], win)


def _hy_kf_kernel(seq_len, a_ref, b_ref, n_ref, o_ref):
    i = pl.program_id(0)
    tr = a_ref.shape[0]
    grow = lax.broadcasted_iota(jnp.int32, (tr, 1), 0) + i * tr
    first = jnp.logical_or(grow == 0, grow == seq_len)
    imag = grow > seq_len
    a, b = a_ref[...], b_ref[...]
    kf = jnp.where(imag, a - b, a + b)
    scale = jnp.where(first, 0.5 / seq_len, 1.0 / seq_len) / n_ref[0:1, :]
    o_ref[...] = kf * scale


def _hy_kf(kraw, nrm, seq_len):
    m2 = 2 * seq_len
    tr = 512
    nt = m2 // tr
    return pl.pallas_call(
        functools.partial(_hy_kf_kernel, seq_len),
        grid=(nt,),
        in_specs=[pl.BlockSpec((tr, 512), lambda i: (i, 0)),
                  pl.BlockSpec((tr, 512), lambda i: (nt + i, 0)),
                  pl.BlockSpec((8, 512), lambda i: (0, 0))],
        out_specs=pl.BlockSpec((tr, 512), lambda i: (i, 0)),
        out_shape=jax.ShapeDtypeStruct((m2, 512), F32),
        compiler_params=_cp("arbitrary"),
        name="hy_kf",
    )(kraw, kraw, nrm)


def _s5_discretise(are_ref, aim_ref, ldt_ref):
    ar = jnp.minimum(are_ref[0], -1e-4)
    ai = aim_ref[0]
    dt = jnp.exp(ldt_ref[0])
    e = jnp.exp(ar * dt)
    return ar, ai, e * jnp.cos(ai * dt), e * jnp.sin(ai * dt)


def _s5_scan_kernel(nseq, emit_y, u_ref, hin_ref, are_ref, aim_ref, ldt_ref, bre_ref, bim_ref, *rest):
    if emit_y:
        cre_ref, cim_ref, y_ref, hfin_ref, wb_sc, ab_sc, s_sc, hc_sc, lhs_sc, wc_sc = rest
    else:
        hfin_ref, wb_sc, ab_sc, s_sc, hc_sc, lhs_sc = rest
    d = pl.program_id(0)
    c = pl.program_id(1)
    n = S5_N

    @pl.when(c == 0)
    def _():
        ar, ai, abr, abi = _s5_discretise(are_ref, aim_ref, ldt_ref)
        ab_sc[0:1, :] = abr
        ab_sc[1:2, :] = abi
        den = 1.0 / (ar * ar + ai * ai)
        cr = ((abr - 1.0) * ar + abi * ai) * den
        ci = (abi * ar - (abr - 1.0) * ai) * den
        bre, bim = bre_ref[0], bim_ref[0]
        wb_sc[:, 0:n] = (cr * bre - ci * bim).astype(BF16)
        wb_sc[:, n:2 * n] = (cr * bim + ci * bre).astype(BF16)
        if emit_y:
            wc_sc[0:n, :] = cre_ref[0].astype(BF16)
            wc_sc[n:2 * n, :] = (-cim_ref[0]).astype(BF16)
        hc_sc[...] = hin_ref[0]

    steps = u_ref.shape[1]
    for t in range(steps):
        lhs_sc[t * nseq:(t + 1) * nseq, :] = u_ref[:, t, :].astype(BF16)
    s_sc[...] = _dot(lhs_sc[...], wb_sc[...])
    lb = 256
    for j in range(n // lb):
        lr = slice(lb * j, lb * (j + 1))
        li = slice(n + lb * j, n + lb * (j + 1))
        abr = ab_sc[0:1, lr]
        abi = ab_sc[1:2, lr]

        def body(t, carry):
            hr, hi = carry
            idx = t + d * (steps - 1 - 2 * t)
            rows = pl.ds(pl.multiple_of(idx * nseq, nseq), nseq)
            nr = abr * hr - abi * hi + s_sc[rows, lr]
            ni = abr * hi + abi * hr + s_sc[rows, li]
            s_sc[rows, lr] = nr
            s_sc[rows, li] = ni
            return nr, ni

        hr, hi = lax.fori_loop(0, steps, body, (hc_sc[:, lr], hc_sc[:, li]))
        hc_sc[:, lr] = hr
        hc_sc[:, li] = hi
    if emit_y:
        y = _dot(s_sc[...].astype(BF16), wc_sc[...])
        for t in range(steps):
            y_ref[0, :, t, :] = y[t * nseq:(t + 1) * nseq]

    @pl.when(c == pl.num_programs(1) - 1)
    def _():
        hfin_ref[0] = hc_sc[...]


def _s5_scan(u, hin, wl, emit_y):
    nseq, nstep, _ = u.shape
    steps = S5_ROWS // nseq
    nc = nstep // steps
    n = S5_N
    chunk = lambda d, c: c + d * (nc - 1 - 2 * c)
    per_dir = lambda shape: pl.BlockSpec((1,) + shape, lambda d, c: (d,) + (0,) * len(shape))
    args = [u, hin, wl["s5_are"], wl["s5_aim"], wl["s5_ldt"], wl["s5_bre"], wl["s5_bim"]]
    specs = [pl.BlockSpec((nseq, steps, D_BRANCH), lambda d, c: (0, chunk(d, c), 0)),
             per_dir((nseq, 2 * n)), per_dir((1, n)), per_dir((1, n)), per_dir((1, n)),
             per_dir((D_BRANCH, n)), per_dir((D_BRANCH, n))]
    out_specs = [per_dir((nseq, 2 * n))]
    out_shape = [jax.ShapeDtypeStruct((2, nseq, 2 * n), F32)]
    scratch = [pltpu.VMEM((D_BRANCH, 2 * n), BF16), pltpu.VMEM((8, n), F32),
               pltpu.VMEM((S5_ROWS, 2 * n), F32), pltpu.VMEM((nseq, 2 * n), F32),
               pltpu.VMEM((S5_ROWS, D_BRANCH), BF16)]
    if emit_y:
        args += [wl["s5_cre"], wl["s5_cim"]]
        specs += [per_dir((n, D_BRANCH)), per_dir((n, D_BRANCH))]
        out_specs = [pl.BlockSpec((1, nseq, steps, D_BRANCH), lambda d, c: (d, 0, chunk(d, c), 0))] + out_specs
        out_shape = [jax.ShapeDtypeStruct((2, nseq, nstep, D_BRANCH), F32)] + out_shape
        scratch += [pltpu.VMEM((2 * n, D_BRANCH), BF16)]
    return pl.pallas_call(
        functools.partial(_s5_scan_kernel, nseq, emit_y),
        grid=(2, nc),
        in_specs=specs, out_specs=out_specs, out_shape=out_shape, scratch_shapes=scratch,
        compiler_params=_cp("arbitrary", "arbitrary"),
        name="s5_scan" if emit_y else "s5_scan_finals",
    )(*args)


def _s5_chain_kernel(batch, nseg, f_ref, h0_ref, are_ref, aim_ref, ldt_ref, o_ref):
    d = pl.program_id(0)
    n = S5_N
    _, _, pr, pi = _s5_discretise(are_ref, aim_ref, ldt_ref)
    for _ in range(int(math.log2(S5_SEG))):
        pr, pi = pr * pr - pi * pi, 2.0 * pr * pi
    f = f_ref[0]
    fr, fi = f[:, 0:n], f[:, n:2 * n]
    h0 = h0_ref[0]
    h0r, h0i = h0[:, 0:n], h0[:, n:2 * n]
    nrow = batch * nseg
    seg = jnp.bitwise_and(lax.broadcasted_iota(jnp.int32, (nrow, 1), 0), nseg - 1)

    def run(shift, keep):
        xr, xi = h0r, h0i
        for _ in range(nseg - 1):
            zr = fr + pr * xr - pi * xi
            zi = fi + pr * xi + pi * xr
            xr = h0r + jnp.where(keep, pltpu.roll(zr, shift, 0), 0.0)
            xi = h0i + jnp.where(keep, pltpu.roll(zi, shift, 0), 0.0)
        o_ref[0, :, 0:n] = xr
        o_ref[0, :, n:2 * n] = xi

    @pl.when(d == 0)
    def _():
        run(1, seg != 0)

    @pl.when(d == 1)
    def _():
        run(nrow - 1, seg != nseg - 1)


def _s5_chain(fin, h0rows, wl, batch, nseg):
    nrow = batch * nseg
    n = S5_N
    per_dir = lambda shape: pl.BlockSpec((1,) + shape, lambda d: (d,) + (0,) * len(shape))
    return pl.pallas_call(
        functools.partial(_s5_chain_kernel, batch, nseg),
        grid=(2,),
        in_specs=[per_dir((nrow, 2 * n)), per_dir((nrow, 2 * n)), per_dir((1, n)), per_dir((1, n)), per_dir((1, n))],
        out_specs=per_dir((nrow, 2 * n)),
        out_shape=jax.ShapeDtypeStruct((2, nrow, 2 * n), F32),
        compiler_params=_cp("arbitrary"),
        name="s5_chain",
    )(fin, h0rows, wl["s5_are"], wl["s5_aim"], wl["s5_ldt"])


def _s5_post_kernel(u_ref, yf_ref, yb_ref, d_ref, w_ref, b_ref, o_ref):
    y = d_ref[...] * u_ref[...] + yf_ref[0] + yb_ref[0]
    g = 0.5 * y * (1.0 + jnp.tanh(math.sqrt(2.0 / math.pi) * (y + 0.044715 * (y * y * y))))
    z = _dot(g.astype(BF16), w_ref[...]) + b_ref[...]
    o_ref[...] = g / (1.0 + jnp.exp(-z))


def _s5_post(u_tm, y2, wl):
    n = u_tm.shape[0]
    tm = 512
    c = D_BRANCH
    return pl.pallas_call(
        _s5_post_kernel,
        grid=(n // tm,),
        in_specs=[pl.BlockSpec((tm, c), lambda i: (i, 0)),
                  pl.BlockSpec((1, tm, c), lambda i: (0, i, 0)),
                  pl.BlockSpec((1, tm, c), lambda i: (1, i, 0)),
                  pl.BlockSpec((1, c), lambda i: (0, 0)),
                  pl.BlockSpec((c, c), lambda i: (0, 0)),
                  pl.BlockSpec((1, c), lambda i: (0, 0))],
        out_specs=pl.BlockSpec((tm, c), lambda i: (i, 0)),
        out_shape=jax.ShapeDtypeStruct((n, c), F32),
        compiler_params=_cp("arbitrary"),
        name="s5_post",
    )(u_tm, y2, y2, wl["s5_d"], wl["s5_glu_w"], wl["s5_glu_b"])


def _gla_kernel(seq_len, nb, q_ref, k_ref, v_ref, g_ref, gw_ref, gb_ref, s0_ref, o_ref, sfin_ref,
                qe_sc, upd_sc, dec_sc, sall_sc):
    d = pl.program_id(1)
    sign = 1 - 2 * d
    ck, sup = GLA_CHUNK, GLA_SUPER
    cps = sup // ck
    nsup, nchunk = seq_len // sup, seq_len // ck
    dk, dv = GLA_HEADS * GLA_DK, GLA_HEADS * GLA_DV
    r = lax.broadcasted_iota(jnp.int32, (sup, sup), 0)
    s = lax.broadcasted_iota(jnp.int32, (sup, sup), 1)
    same = lax.shift_right_logical(r, 6) == lax.shift_right_logical(s, 6)
    tri = jnp.logical_and(same, (s - r) * sign <= 0)
    cum_lhs = jnp.concatenate([jnp.where(tri, 1.0, 0.0), jnp.where(same, 1.0, 0.0)], axis=0).astype(BF16)
    head_k = lax.shift_right_logical(lax.broadcasted_iota(jnp.int32, (1, dk), 1), 5)
    head_v = lax.shift_right_logical(lax.broadcasted_iota(jnp.int32, (1, dv), 1), 6)
    blockdiag = lax.shift_right_logical(lax.broadcasted_iota(jnp.int32, (dv, 1), 0), 6) == head_k

    for j in range(nb):
        base = j * seq_len

        def sup_body(i, carry, j=j, base=base):
            rows = pl.ds(pl.multiple_of(base + i * sup, sup), sup)
            q = q_ref[rows, :] * (GLA_DK ** -0.5)
            k = k_ref[rows, :]
            v = v_ref[rows, :]
            x = _dot(g_ref[rows, :].astype(BF16), gw_ref[0]) + gb_ref[0]
            la = (jnp.minimum(x, 0.0) - jnp.log(1.0 + jnp.exp(-jnp.abs(x)))) * (1.0 / GLA_TAU)
            cs = _dot(cum_lhs, jnp.concatenate(_split3(la), axis=1))
            cs = cs[:, 0:dk] + (cs[:, dk:2 * dk] + cs[:, 2 * dk:3 * dk])
            bc, tot = cs[0:sup], cs[sup:2 * sup]
            ref = 0.5 * tot
            qt = q * jnp.exp(bc - ref)
            kt = (k * jnp.exp(ref - bc)).astype(BF16)
            lhs = jnp.concatenate([jnp.where(head_k == h, qt, 0.0) for h in range(GLA_HEADS)], axis=0).astype(BF16)
            p = _dot_nt(lhs, kt)
            o = None
            for h in range(GLA_HEADS):
                att = jnp.where(tri, p[h * sup:(h + 1) * sup], 0.0).astype(BF16)
                oh = _dot(att, jnp.where(head_v == h, v, 0.0).astype(BF16))
                o = oh if o is None else o + oh
            o_ref[0, rows, :] = o
            qe_sc[j, pl.ds(pl.multiple_of(i * sup, sup), sup), :] = (q * jnp.exp(bc)).astype(BF16)
            kl = (k * jnp.exp(tot - bc)).astype(BF16)
            vb = v.astype(BF16)
            dec = jnp.exp(tot)
            for c in range(cps):
                cr = slice(c * ck, (c + 1) * ck)
                upd_sc[j, i * cps + c] = jnp.where(blockdiag, _dot_tn(vb[cr], kl[cr]), 0.0)
                dec_sc[j, i * cps + c] = dec[c * ck:c * ck + 8]
            return carry

        lax.fori_loop(0, nsup, sup_body, 0)

        def state_body(c, st, j=j):
            ci = c + d * (nchunk - 1 - 2 * c)
            sall_sc[j, ci] = st.astype(BF16)
            return dec_sc[j, ci][0:1, :] * st + upd_sc[j, ci]

        sfin_ref[j, 0] = lax.fori_loop(0, nchunk, state_body, s0_ref[j, 0])

        def inter_body(c, carry, j=j, base=base):
            rows = pl.ds(pl.multiple_of(base + c * ck, ck), ck)
            o_ref[0, rows, :] += _dot_nt(qe_sc[j, pl.ds(pl.multiple_of(c * ck, ck), ck), :], sall_sc[j, c])
            return carry

        lax.fori_loop(0, nchunk, inter_body, 0)


def _gla(gla_in, wl, s0t, batch, seq_len):
    n = gla_in.shape[0]
    dk, dv = GLA_HEADS * GLA_DK, GLA_HEADS * GLA_DV
    nb = max(1, min(batch, 1024 // seq_len))
    rows = nb * seq_len
    nchunk = seq_len // GLA_CHUNK
    return pl.pallas_call(
        functools.partial(_gla_kernel, seq_len, nb),
        grid=(batch // nb, 2),
        in_specs=[pl.BlockSpec((rows, dk), lambda b, d: (b, 0)),
                  pl.BlockSpec((rows, dk), lambda b, d: (b, 1)),
                  pl.BlockSpec((rows, dv), lambda b, d: (b, 1)),
                  pl.BlockSpec((rows, dk), lambda b, d: (b, 4)),
                  pl.BlockSpec((1, dk, dk), lambda b, d: (d, 0, 0)),
                  pl.BlockSpec((1, 1, dk), lambda b, d: (d, 0, 0)),
                  pl.BlockSpec((nb, 1, dv, dk), lambda b, d: (b, d, 0, 0))],
        out_specs=[pl.BlockSpec((1, rows, dv), lambda b, d: (d, b, 0)),
                   pl.BlockSpec((nb, 1, dv, dk), lambda b, d: (b, d, 0, 0))],
        out_shape=[jax.ShapeDtypeStruct((2, n, dv), F32), jax.ShapeDtypeStruct((batch, 2, dv, dk), F32)],
        scratch_shapes=[pltpu.VMEM((nb, seq_len, dk), BF16),
                        pltpu.VMEM((nb, nchunk, dv, dk), F32),
                        pltpu.VMEM((nb, nchunk, 8, dk), F32),
                        pltpu.VMEM((nb, nchunk, dv, dk), BF16)],
        compiler_params=_cp("arbitrary", "arbitrary"),
        name="gla",
    )(gla_in, gla_in, gla_in, gla_in, wl["gla_gw"], wl["gla_gb"], s0t)


def _outproj_kernel(x_ref, mod_ref, g_ref, om_ref, oh_ref, os_ref, gf_ref, gb_ref, gn_ref, hm_ref, w_ref, y_ref):
    c = D_BRANCH
    g = g_ref[...]
    acc = _dot((om_ref[...] * _silu(g[:, 0:c])).astype(BF16), w_ref[0:c, :])
    acc += _dot((oh_ref[...] * _silu(g[:, c:2 * c])).astype(BF16), w_ref[c:2 * c, :])
    acc += _dot((os_ref[...] * _silu(g[:, 2 * c:3 * c])).astype(BF16), w_ref[2 * c:3 * c, :])
    og = gf_ref[0] + gb_ref[0]
    hi, lo = _split2(og * og)
    ms = _dot(hi, hm_ref[...]) + _dot(lo, hm_ref[...])
    ogn = og * lax.rsqrt(ms + EPS) * gn_ref[...]
    acc += _dot((ogn * _silu(g[:, 3 * c:4 * c])).astype(BF16), w_ref[3 * c:4 * c, :])
    y_ref[...] = x_ref[...] + mod_ref[0, 2:3, :] * acc


def _outproj(x, mod, mod_row, gates, o_mla, o_hy, o_s5, o_gla, wl, tm):
    n, d = x.shape
    c = D_BRANCH
    row = lambda w: pl.BlockSpec((tm, w), lambda i: (i, 0))
    return pl.pallas_call(
        _outproj_kernel,
        grid=(n // tm,),
        in_specs=[row(d),
                  pl.BlockSpec((1, 3, d), lambda i: (mod_row(i), 0, 0)),
                  row(d), row(c), row(c), row(c),
                  pl.BlockSpec((1, tm, c), lambda i: (0, i, 0)),
                  pl.BlockSpec((1, tm, c), lambda i: (1, i, 0)),
                  pl.BlockSpec((1, c), lambda i: (0, 0)),
                  pl.BlockSpec((c, c), lambda i: (0, 0)),
                  pl.BlockSpec((d, d), lambda i: (0, 0))],
        out_specs=row(d),
        out_shape=jax.ShapeDtypeStruct((n, d), F32),
        compiler_params=_cp("arbitrary"),
        name="outproj",
    )(x, mod, gates, o_mla, o_hy, o_s5, o_gla, o_gla, wl["gla_norm"], wl["head_mean"], wl["w_out"])


def _rope_tables(seq_len):
    pos = np.arange(seq_len)
    inv = ROPE_BASE ** (-np.arange(0, 16, 2, dtype=np.float64) / 16.0)
    cos = np.ones((seq_len, HEAD_PAD))
    sin_a = np.zeros((seq_len, HEAD_PAD))
    sin_b = np.zeros((seq_len, HEAD_PAD))
    for base, p in ((MLA_NOPE, pos // GRID_W), (MLA_NOPE + 16, pos % GRID_W)):
        ang = p[:, None].astype(np.float64) * inv[None, :]
        cos[:, base:base + 8] = np.cos(ang)
        cos[:, base + 8:base + 16] = np.cos(ang)
        sin_a[:, base:base + 8] = -np.sin(ang)
        sin_b[:, base + 8:base + 16] = np.sin(ang)
    return tuple(jnp.asarray(t, F32) for t in (cos, sin_a, sin_b))


def _dft_matrices(seq_len):
    n2 = 2 * seq_len
    j = np.arange(seq_len)[:, None]
    t = np.arange(seq_len)[None, :]
    ang = (2.0 * np.pi / n2) * ((j * t) % n2)
    top = np.cos(ang)
    bot = -np.sin(ang)
    bot[0, :] = np.where(np.arange(seq_len) % 2 == 0, 1.0, -1.0)
    f = np.concatenate([top, bot], axis=0)
    return jnp.asarray(f, F32).astype(BF16), jnp.asarray(f.T, F32).astype(BF16)


def _hyena_tables(seq_len):
    pos = np.arange(seq_len, dtype=np.float64)
    t = pos / seq_len
    w = 2.0 * np.pi * pos / seq_len
    bands = np.linspace(1e-4, HY_BANDS - 1, HY_BANDS)
    feat = np.zeros((seq_len, 128))
    feat[:, 0] = t
    feat[:, 1:1 + HY_BANDS] = np.cos(w[:, None] * bands)
    feat[:, 1 + HY_BANDS:HY_FEAT] = np.sin(w[:, None] * bands)
    deltas = np.linspace(math.log(1.0 / HY_TARGET) / HY_FAST_DECAY, math.log(1.0 / HY_TARGET) / HY_SLOW_DECAY,
                         D_BRANCH)
    win = np.exp(-t[:, None] * deltas[None, :]) + HY_SHIFT
    return jnp.asarray(feat, F32), jnp.asarray(win, F32)


def _pad_to(a, shape):
    return jnp.pad(a, [(0, s - d) for s, d in zip(shape, a.shape)])


def _layer_weights(l, p):
    z = lambda *s: jnp.zeros(s, F32)
    w_in = p["w_in"][l]
    col = lambda lo, hi: w_in[:, lo:hi]
    d = D_MODEL
    w_p = jnp.concatenate([
        col(0, 192), z(d, 64), col(192, 320), z(d, 64), col(320, 352), z(d, 32),
        col(352, 608), col(1376, 1632), col(1888, 2144), col(2688, 2944),
        col(608, 1376), col(1632, 1888),
        col(2144, 2272), col(2272, 2400), col(2400, 2656), col(2656, 2688), z(d, 96)], axis=1).astype(BF16)
    wl = {"w_in": w_p, "norm_w": p["norm_w"][l].reshape(1, d), "w_out": p["w_out"][l].astype(BF16)}
    wl["qa_norm"] = _pad_to(p["mla_qa_norm"][l].reshape(1, -1), (1, 256))
    w_uq = _pad_to(p["mla_w_uq"][l].reshape(MLA_Q_RANK, MLA_HEADS, MLA_QK), (256, MLA_HEADS, HEAD_PAD))
    wl["w_uq"] = w_uq.reshape(256, MLA_HEADS * HEAD_PAD).astype(BF16)
    wl["q_norm"] = _pad_to(p["mla_q_norm"][l].reshape(1, -1), (1, HEAD_PAD))
    wl["k_norm"] = _pad_to(p["mla_k_norm"][l].reshape(1, -1), (1, HEAD_PAD))
    wl["kva_norm"] = p["mla_kva_norm"][l].reshape(1, -1)
    w_ukv = p["mla_w_ukv"][l].reshape(MLA_KV_RANK, MLA_HEADS, MLA_NOPE + MLA_V)
    wl["w_uk"] = _pad_to(w_ukv[:, :, :MLA_NOPE], (MLA_KV_RANK, MLA_HEADS, HEAD_PAD)).reshape(MLA_KV_RANK, -1).astype(BF16)
    wl["w_uv"] = w_ukv[:, :, MLA_NOPE:].reshape(MLA_KV_RANK, MLA_HEADS * MLA_V).astype(BF16)
    wl["hy_conv_w"] = p["hy_conv_w"][l]
    wl["hy_conv_b"] = p["hy_conv_b"][l].reshape(1, -1)
    wl["hy_w1"] = _pad_to(p["hy_w1"][l], (128, 128))
    wl["hy_b1"] = _pad_to(p["hy_b1"][l].reshape(1, -1), (1, 128))
    wl["hy_f1"] = _pad_to(p["hy_freq1"][l].reshape(1, -1), (1, 128))
    wl["hy_w2"] = _pad_to(p["hy_w2"][l], (128, 128))
    wl["hy_b2"] = _pad_to(p["hy_b2"][l].reshape(1, -1), (1, 128))
    wl["hy_f2"] = _pad_to(p["hy_freq2"][l].reshape(1, -1), (1, 128))
    wl["hy_w3"] = _pad_to(p["hy_w3"][l], (128, 1024))
    wl["hy_bias"] = p["hy_bias"][l]
    eye_g = jnp.eye(S5_GROUPS, dtype=F32)
    flat = lambda a: a[l].reshape(2, 1, S5_N)
    wl["s5_are"], wl["s5_aim"] = flat(p["s5_a_re"]), flat(p["s5_a_im"])
    wl["s5_ldt"] = jnp.repeat(p["s5_log_dt"][l], S5_STATE, axis=-1).reshape(2, 1, S5_N)
    bd_b = lambda a: jnp.einsum("dgpi,gh->dgihp", a[l], eye_g).reshape(2, D_BRANCH, S5_N)
    bd_c = lambda a: jnp.einsum("dgip,gh->dgphi", a[l], eye_g).reshape(2, S5_N, D_BRANCH)
    wl["s5_bre"], wl["s5_bim"] = bd_b(p["s5_b_re"]), bd_b(p["s5_b_im"])
    wl["s5_cre"], wl["s5_cim"] = bd_c(p["s5_c_re"]), bd_c(p["s5_c_im"])
    wl["s5_d"] = p["s5_d"][l].reshape(1, -1)
    wl["s5_glu_w"] = p["s5_glu_w"][l].astype(BF16)
    wl["s5_glu_b"] = p["s5_glu_b"][l].reshape(1, -1)
    gw = p["gla_gw"][l]
    dk = GLA_HEADS * GLA_DK
    wl["gla_gw"] = jnp.stack([_pad_to(jnp.pad(gw[i], ((GLA_RANK * i, 0), (0, 0))), (dk, dk)) for i in range(2)]).astype(BF16)
    wl["gla_gb"] = p["gla_gb"][l].reshape(2, 1, dk)
    wl["gla_norm"] = jnp.tile(p["gla_norm"][l], GLA_HEADS).reshape(1, -1)
    head = np.arange(D_BRANCH) // GLA_DV
    wl["head_mean"] = jnp.asarray((head[:, None] == head[None, :]) / GLA_DV, BF16)
    return wl


def _hyena_filters(wl, tabs):
    feat, win, f_mat, _ = tabs
    seq_len = feat.shape[0]
    filt, nrm = _hy_mlp(feat, wl, win)
    kraw = _dft_fwd(f_mat, filt.reshape(2 * seq_len, 512), 2, 0, 512)
    return _hy_kf(kraw, nrm, seq_len)


def _trunk_layer(x, mod, mod_row, wl, batch, seq_len, hy_tabs, rope_tabs=None, ctx=None):
    n = batch * seq_len
    tm = 512
    mla_in, gates, hy_in, s5_in, gla_in = _inproj(x, mod, mod_row, wl["norm_w"], wl["w_in"], tm)

    q, k, v, ckv = _mla_prep(mla_in, wl, rope_tabs, seq_len, tm, True)
    kv_parts = [(k, v, seq_len)]
    if ctx is not None:
        k_ctx, v_ctx = _mla_prep(ctx["mla"], wl, None, ctx["past"], 512, False)
        kv_parts = [(k_ctx, v_ctx, ctx["past"])] + kv_parts
    o_mla = _attention(q, kv_parts, batch, seq_len, 256)

    _, _, f_mat, ft_mat = hy_tabs
    kf = _hyena_filters(wl, hy_tabs)
    if seq_len <= 512:
        o_hy = _hyena_fused(hy_in, wl, kf, f_mat, ft_mat, batch, seq_len)
    else:
        z = _hy_conv(hy_in, wl["hy_conv_w"], wl["hy_conv_b"], batch, seq_len)
        u = _dft_fwd(f_mat, z, batch, 0, D_BRANCH)
        y1 = _hy_inv(ft_mat, u, kf, 0, z, 1, z, 0, wl["hy_bias"], batch)
        u = _dft_fwd(f_mat, y1, batch, 0, D_BRANCH)
        o_hy = _hy_inv(ft_mat, u, kf, 1, z, 2, y1, 0, wl["hy_bias"], batch)

    nseg = seq_len // S5_SEG
    nseq = batch * nseg
    u_seg = s5_in.reshape(nseq, S5_SEG, D_BRANCH)
    if ctx is None:
        hin = jnp.zeros((2, nseq, 2 * S5_N), F32)
    else:
        (fin,) = _s5_scan(u_seg, jnp.zeros((2, nseq, 2 * S5_N), F32), wl, False)
        hin = _s5_chain(fin, ctx["s5_h0"], wl, batch, nseg)
    y2, s5_fin = _s5_scan(u_seg, hin, wl, True)
    o_s5 = _s5_post(s5_in, y2.reshape(2, n, D_BRANCH), wl)

    s0 = jnp.zeros((batch, 2, GLA_HEADS * GLA_DV, GLA_HEADS * GLA_DK), F32) if ctx is None else ctx["gla_s0"]
    o_gla, gla_fin = _gla(gla_in, wl, s0, batch, seq_len)

    y = _outproj(x, mod, mod_row, gates, o_mla, o_hy, o_s5, o_gla, wl, tm)
    return y, (ckv, mla_in[:, KROPE_COL:KROPE_COL + MLA_ROPE], s5_fin, gla_fin)


def kernel(x_prompt, x_sample, c, cache_mla_ckv, cache_mla_krope, state_s5, state_gla, c_ctx, norm_w, ada_w, ada_b, w_in, w_out, mla_qa_norm, mla_kva_norm, mla_w_uq, mla_w_ukv, mla_q_norm, mla_k_norm, hy_conv_w, hy_conv_b, hy_w1, hy_b1, hy_freq1, hy_w2, hy_b2, hy_freq2, hy_w3, hy_bias, s5_a_re, s5_a_im, s5_log_dt, s5_b_re, s5_b_im, s5_c_re, s5_c_im, s5_d, s5_glu_w, s5_glu_b, gla_gw, gla_gb, gla_norm):
    params = dict(norm_w=norm_w, w_in=w_in, w_out=w_out, mla_qa_norm=mla_qa_norm, mla_kva_norm=mla_kva_norm,
                  mla_w_uq=mla_w_uq, mla_w_ukv=mla_w_ukv, mla_q_norm=mla_q_norm, mla_k_norm=mla_k_norm,
                  hy_conv_w=hy_conv_w, hy_conv_b=hy_conv_b, hy_w1=hy_w1, hy_b1=hy_b1, hy_freq1=hy_freq1,
                  hy_w2=hy_w2, hy_b2=hy_b2, hy_freq2=hy_freq2, hy_w3=hy_w3, hy_bias=hy_bias,
                  s5_a_re=s5_a_re, s5_a_im=s5_a_im, s5_log_dt=s5_log_dt, s5_b_re=s5_b_re, s5_b_im=s5_b_im,
                  s5_c_re=s5_c_re, s5_c_im=s5_c_im, s5_d=s5_d, s5_glu_w=s5_glu_w, s5_glu_b=s5_glu_b,
                  gla_gw=gla_gw, gla_gb=gla_gb, gla_norm=gla_norm)
    bp, lp, d = x_prompt.shape
    bs, ls, _ = x_sample.shape
    past = cache_mla_ckv.shape[2]
    n_s5 = S5_N

    conds = jnp.concatenate([c_ctx[None, :], c, jnp.zeros((8 - 1 - bs, d), F32)], axis=0)
    mods = _modulation(conds, ada_w, ada_b).reshape(DEPTH, 8, 3, d)

    tabs_p = _hyena_tables(lp) + _dft_matrices(lp)
    tabs_s = _hyena_tables(ls) + _dft_matrices(ls)
    rope_tabs = _rope_tables(ls)
    tm_s = 512
    nseg = ls // S5_SEG

    y_p = x_prompt.reshape(bp * lp, d)
    y_s = x_sample.reshape(bs * ls, d)
    ckv_l, krope_l, s5_l, gla_l = [], [], [], []
    for l in range(DEPTH):
        wl = _layer_weights(l, params)
        y_p, (ckv, krope, s5_fin, gla_fin) = _trunk_layer(y_p, mods[l], lambda i: 0, wl, bp, lp, tabs_p)
        ckv_l.append(ckv.reshape(bp, lp, -1))
        krope_l.append(krope.reshape(bp, lp, -1))
        s5_l.append(jnp.stack([s5_fin[:, :, :n_s5], s5_fin[:, :, n_s5:]], axis=-1)
                    .reshape(2, bp, S5_GROUPS, S5_STATE, 2).transpose(1, 0, 2, 3, 4))
        gf = gla_fin.reshape(bp, 2, GLA_HEADS, GLA_DV, GLA_HEADS, GLA_DK)
        gla_l.append(jnp.stack([gf[:, :, h, :, h, :] for h in range(GLA_HEADS)], axis=2).swapaxes(-1, -2))

        mla_ctx = jnp.concatenate([cache_mla_ckv[:, l], jnp.zeros((bs, past, 64), F32), cache_mla_krope[:, l],
                                   jnp.zeros((bs, past, 32), F32)], axis=-1).reshape(bs * past, 256)
        st = state_s5[:, l]
        h0 = jnp.concatenate([st[..., 0].reshape(bs, 2, n_s5), st[..., 1].reshape(bs, 2, n_s5)], axis=-1)
        h0 = h0.transpose(1, 0, 2)
        h0rows = jnp.zeros((2, nseg * bs, 2 * n_s5), F32)
        h0rows = h0rows.at[0, 0::nseg].set(h0[0]).at[1, nseg - 1::nseg].set(h0[1])
        eye_h = jnp.eye(GLA_HEADS, dtype=F32)
        gla_s0 = jnp.einsum("bdhke,hg->bdhegk", state_gla[:, l], eye_h).reshape(
            bs, 2, GLA_HEADS * GLA_DV, GLA_HEADS * GLA_DK)
        ctx = {"mla": mla_ctx, "past": past, "s5_h0": h0rows, "gla_s0": gla_s0}
        y_s, _ = _trunk_layer(y_s, mods[l], lambda i: 1 + (i * tm_s) // ls, wl, bs, ls, tabs_s, rope_tabs, ctx)

    return (y_p.reshape(bp, lp, d), y_s.reshape(bs, ls, d),
            jnp.stack(ckv_l, axis=1), jnp.stack(krope_l, axis=1), jnp.stack(s5_l, axis=1), jnp.stack(gla_l, axis=1))
```

```python
import functools
import math

import numpy as np
import jax
import jax.numpy as jnp
from jax import lax
from jax.experimental import pallas as pl
from jax.experimental.pallas import tpu as pltpu

F32 = jnp.float32
BF16 = jnp.bfloat16

D_MODEL = 1024
DEPTH = 2
GRID_W = 64
D_BRANCH = 256
EPS = 1e-6

MLA_HEADS = 4
MLA_Q_RANK = 192
MLA_KV_RANK = 128
MLA_NOPE = 64
MLA_ROPE = 32
MLA_QK = 96
MLA_V = 64
ROPE_BASE = 10000.0
HEAD_PAD = 128

HY_BANDS = 16
HY_FEAT = 33
HY_HIDDEN = 64
HY_SHIFT = 0.05
HY_FAST_DECAY = 0.3
HY_SLOW_DECAY = 1.5
HY_TARGET = 1e-2

S5_GROUP = 16
S5_GROUPS = 16
S5_STATE = 64
S5_N = S5_GROUPS * S5_STATE
S5_ROWS = 512
S5_SEG = 256

GLA_HEADS = 4
GLA_DK = 32
GLA_DV = 64
GLA_RANK = 16
GLA_TAU = 16.0
GLA_CHUNK = 64
GLA_SUPER = 256

SEG_MLA = (0, 512)
SEG_GATE = (512, 1536)
SEG_HY = (1536, 2304)
SEG_S5 = (2304, 2560)
SEG_GLA = (2560, 3200)
N_PROJ = 3200

VMEM_LIMIT = 48 * 1024 * 1024


def _cp(*sem):
    return pltpu.CompilerParams(dimension_semantics=sem, vmem_limit_bytes=VMEM_LIMIT)


def _dot(a, b):
    return jnp.dot(a, b, preferred_element_type=F32)


def _dot_nt(a, b):
    return lax.dot_general(a, b, (((1,), (1,)), ((), ())), preferred_element_type=F32)


def _dot_tn(a, b):
    return lax.dot_general(a, b, (((0,), (0,)), ((), ())), preferred_element_type=F32)


def _split2(x):
    hi = x.astype(BF16)
    lo = (x - hi.astype(F32)).astype(BF16)
    return hi, lo


def _split3(x):
    h1 = x.astype(BF16)
    r1 = x - h1.astype(F32)
    h2 = r1.astype(BF16)
    h3 = (r1 - h2.astype(F32)).astype(BF16)
    return h1, h2, h3


def _dot3(a, b):
    a1, a2 = _split2(a)
    b1, b2 = _split2(b)
    return _dot(a1, b1) + (_dot(a1, b2) + _dot(a2, b1))


def _silu(z):
    return z / (1.0 + jnp.exp(-z))


def _mod_kernel(c_ref, w_ref, b_ref, o_ref):
    s = _silu(c_ref[...])
    o_ref[0] = _dot(s.astype(BF16), w_ref[0].astype(BF16)) + b_ref[0]


def _modulation(conds, ada_w, ada_b):
    d = D_MODEL
    return pl.pallas_call(
        _mod_kernel,
        grid=(DEPTH, 3),
        in_specs=[pl.BlockSpec((8, d), lambda l, j: (0, 0)),
                  pl.BlockSpec((1, d, d), lambda l, j: (l, 0, j)),
                  pl.BlockSpec((1, 1, d), lambda l, j: (l, 0, j))],
        out_specs=pl.BlockSpec((1, 8, d), lambda l, j: (l, 0, j)),
        out_shape=jax.ShapeDtypeStruct((DEPTH, 8, 3 * d), F32),
        compiler_params=_cp("arbitrary", "arbitrary"),
        name="modulation",
    )(conds, ada_w, ada_b.reshape(DEPTH, 1, 3 * d))


def _inproj_kernel(x_ref, mod_ref, nw_ref, w_ref, o_mla, o_g, o_hy, o_s5, o_gla):
    x = x_ref[...]
    ms = jnp.mean(x * x, axis=-1, keepdims=True)
    y = x * lax.rsqrt(ms + EPS) * nw_ref[...]
    h = (y * (1.0 + mod_ref[0, 1:2, :]) + mod_ref[0, 0:1, :]).astype(BF16)
    for o, (lo, hi) in ((o_mla, SEG_MLA), (o_g, SEG_GATE), (o_hy, SEG_HY), (o_s5, SEG_S5), (o_gla, SEG_GLA)):
        o[...] = _dot(h, w_ref[:, lo:hi])


def _inproj(x, mod, mod_row, norm_w, w_p, tm):
    n, d = x.shape
    widths = [hi - lo for lo, hi in (SEG_MLA, SEG_GATE, SEG_HY, SEG_S5, SEG_GLA)]
    return pl.pallas_call(
        _inproj_kernel,
        grid=(n // tm,),
        in_specs=[pl.BlockSpec((tm, d), lambda i: (i, 0)),
                  pl.BlockSpec((1, 3, d), lambda i: (mod_row(i), 0, 0)),
                  pl.BlockSpec((1, d), lambda i: (0, 0)),
                  pl.BlockSpec((d, N_PROJ), lambda i: (0, 0))],
        out_specs=[pl.BlockSpec((tm, w), lambda i: (i, 0)) for w in widths],
        out_shape=[jax.ShapeDtypeStruct((n, w), F32) for w in widths],
        compiler_params=_cp("arbitrary"),
        name="inproj",
    )(x, mod, norm_w, w_p)


def _head_norm(xh, w):
    ms = jnp.sum(xh * xh, axis=-1, keepdims=True) * (1.0 / MLA_QK)
    return xh * lax.rsqrt(ms + EPS) * w


def _rope(xh, cos, sin_a, sin_b):
    return xh * cos + pltpu.roll(xh, HEAD_PAD - 8, 1) * sin_a + pltpu.roll(xh, 8, 1) * sin_b


def _mla_prep_kernel(has_q, rope, *refs):
    refs = list(refs)
    m_ref = refs.pop(0)
    if has_q:
        qan_ref, wuq_ref, qn_ref, kvn_ref = refs[:4]
        refs = refs[4:]
    wuk_ref, wuv_ref, kn_ref = refs[:3]
    refs = refs[3:]
    if rope:
        cos_ref, sa_ref, sb_ref = refs[:3]
        refs = refs[3:]
        cos, sa, sb = cos_ref[...], sa_ref[...], sb_ref[...]
    if has_q:
        q_ref, k_ref, v_ref, ckv_ref, kro_ref = refs
    else:
        k_ref, v_ref = refs
    m = m_ref[...]
    if has_q:
        cq = m[:, 0:256]
        ms = jnp.sum(cq * cq, axis=-1, keepdims=True) * (1.0 / MLA_Q_RANK)
        cqn = cq * lax.rsqrt(ms + EPS) * qan_ref[...]
        q = _dot(cqn.astype(BF16), wuq_ref[...])
        ckv = m[:, 256:384]
        ckvn = ckv * lax.rsqrt(jnp.mean(ckv * ckv, axis=-1, keepdims=True) + EPS) * kvn_ref[...]
        ckv_ref[...] = ckvn
        kr = m[:, 384:512]
        kro_ref[...] = kr[:, MLA_NOPE:MLA_NOPE + MLA_ROPE]
    else:
        ckvn = m[:, 0:128]
        kr = m[:, 128:256]
    cb = ckvn.astype(BF16)
    kup = _dot(cb, wuk_ref[...])
    v_ref[...] = _dot(cb, wuv_ref[...]).astype(BF16)
    for h in range(MLA_HEADS):
        sl = slice(HEAD_PAD * h, HEAD_PAD * (h + 1))
        kh = _head_norm(kup[:, sl] + kr, kn_ref[...])
        if rope:
            kh = _rope(kh, cos, sa, sb)
        k_ref[:, sl] = kh.astype(BF16)
        if has_q:
            qh = _head_norm(q[:, sl], qn_ref[...])
            if rope:
                qh = _rope(qh, cos, sa, sb)
            q_ref[:, sl] = (qh * (MLA_QK ** -0.5)).astype(BF16)


def _mla_prep(m, wl, rope_tabs, seq_len, tm, has_q):
    n, wm = m.shape
    rope = rope_tabs is not None
    full = lambda shape: pl.BlockSpec(shape, lambda i: (0,) * len(shape))
    args, specs = [m], [pl.BlockSpec((tm, wm), lambda i: (i, 0))]
    if has_q:
        args += [wl["qa_norm"], wl["w_uq"], wl["q_norm"], wl["kva_norm"]]
        specs += [full((1, 256)), full((256, 512)), full((1, 128)), full((1, 128))]
    args += [wl["w_uk"], wl["w_uv"], wl["k_norm"]]
    specs += [full((128, 512)), full((128, 256)), full((1, 128))]
    if rope:
        nt = seq_len // tm
        args += list(rope_tabs)
        specs += [pl.BlockSpec((tm, HEAD_PAD), lambda i: (i % nt, 0))] * 3
    row = lambda w: pl.BlockSpec((tm, w), lambda i: (i, 0))
    out_specs = [row(512), row(256)]
    out_shape = [jax.ShapeDtypeStruct((n, 512), BF16), jax.ShapeDtypeStruct((n, 256), BF16)]
    if has_q:
        out_specs = [row(512)] + out_specs + [row(128), row(MLA_ROPE)]
        out_shape = ([jax.ShapeDtypeStruct((n, 512), BF16)] + out_shape
                     + [jax.ShapeDtypeStruct((n, 128), F32), jax.ShapeDtypeStruct((n, MLA_ROPE), F32)])
    return pl.pallas_call(
        functools.partial(_mla_prep_kernel, has_q, rope),
        grid=(n // tm,),
        in_specs=specs, out_specs=out_specs, out_shape=out_shape,
        compiler_params=_cp("arbitrary"),
        name="mla_prep",
    )(*args)


def _attn_kernel(nparts, q_ref, *refs):
    kv = [(refs[2 * i], refs[2 * i + 1]) for i in range(nparts)]
    o_ref = refs[2 * nparts]
    low = lax.broadcasted_iota(jnp.int32, (1, HEAD_PAD), 1) < MLA_V
    for pair in range(MLA_HEADS // 2):
        v_half = []
        for _, v_ref in kv:
            vp = v_ref[:, HEAD_PAD * pair:HEAD_PAD * (pair + 1)]
            zero = jnp.zeros_like(vp)
            v_half.append((jnp.where(low, vp, zero), jnp.where(low, zero, vp)))
        acc = None
        for j in range(2):
            sl = slice(HEAD_PAD * (2 * pair + j), HEAD_PAD * (2 * pair + j + 1))
            s = [_dot_nt(q_ref[:, sl], k_ref[:, sl]) for k_ref, _ in kv]
            m = functools.reduce(jnp.maximum, [jnp.max(x, axis=-1, keepdims=True) for x in s])
            p = [jnp.exp(x - m) for x in s]
            den = functools.reduce(jnp.add, [jnp.sum(x, axis=-1, keepdims=True) for x in p])
            num = functools.reduce(jnp.add, [_dot(x.astype(BF16), vh[j]) for x, vh in zip(p, v_half)])
            o = num / den
            acc = o if acc is None else acc + o
        o_ref[:, HEAD_PAD * pair:HEAD_PAD * (pair + 1)] = acc


def _attention(q, kv_parts, batch, lq, tq):
    nq = lq // tq
    args, specs = [q], [pl.BlockSpec((tq, 512), lambda b, i: (b * nq + i, 0))]
    for k, v, lk in kv_parts:
        args += [k, v]
        specs += [pl.BlockSpec((lk, 512), lambda b, i: (b, 0)), pl.BlockSpec((lk, 256), lambda b, i: (b, 0))]
    return pl.pallas_call(
        functools.partial(_attn_kernel, len(kv_parts)),
        grid=(batch, nq),
        in_specs=specs,
        out_specs=pl.BlockSpec((tq, 256), lambda b, i: (b * nq + i, 0)),
        out_shape=jax.ShapeDtypeStruct((batch * lq, 256), F32),
        compiler_params=_cp("arbitrary", "arbitrary"),
        name="attention",
    )(*args)


def _hy_conv_kernel(x_ref, w_ref, b_ref, z_ref):
    x = x_ref[...]
    n = x.shape[0]
    row = lax.broadcasted_iota(jnp.int32, (n, 1), 0)
    xm = jnp.where(row == 0, 0.0, pltpu.roll(x, 1, 0))
    xp = jnp.where(row == n - 1, 0.0, pltpu.roll(x, n - 1, 0))
    z_ref[...] = w_ref[0:1, :] * xm + w_ref[1:2, :] * x + w_ref[2:3, :] * xp + b_ref[...]


def _hy_conv(x, w, b, batch, seq_len):
    c = D_BRANCH
    return pl.pallas_call(
        _hy_conv_kernel,
        grid=(batch, 3),
        in_specs=[pl.BlockSpec((seq_len, c), lambda i, j: (i, j)),
                  pl.BlockSpec((3, c), lambda i, j: (0, j)),
                  pl.BlockSpec((1, c), lambda i, j: (0, j))],
        out_specs=pl.BlockSpec((seq_len, c), lambda i, j: (i, j)),
        out_shape=jax.ShapeDtypeStruct(x.shape, F32),
        compiler_params=_cp("arbitrary", "arbitrary"),
        name="hy_conv",
    )(x, w, b)


def _hyena_fused_kernel(seq_len, nb, x_ref, cw_ref, cb_ref, f_ref, ft_ref, kf_ref, bias_ref, o_ref):
    c = D_BRANCH
    x = x_ref[...]
    n = x.shape[0]
    pos = jnp.bitwise_and(lax.broadcasted_iota(jnp.int32, (n, 1), 0), seq_len - 1)
    xm = jnp.where(pos == 0, 0.0, pltpu.roll(x, 1, 0))
    xp = jnp.where(pos == seq_len - 1, 0.0, pltpu.roll(x, n - 1, 0))
    z = cw_ref[0:1, :] * xm + cw_ref[1:2, :] * x + cw_ref[2:3, :] * xp + cb_ref[...]
    row0 = lax.broadcasted_iota(jnp.int32, (seq_len, 1), 0) == 0
    f, ft = f_ref[...], ft_ref[...]

    def long_conv(v, order):
        u = _dot(f, v.astype(BF16))
        ut, ub = u[0:seq_len], u[seq_len:2 * seq_len]
        kt = kf_ref[0:seq_len, order * c:(order + 1) * c]
        kb = kf_ref[seq_len:2 * seq_len, order * c:(order + 1) * c]
        tt, bb = ut * kt, ub * kb
        y = jnp.concatenate([jnp.where(row0, tt, tt - bb), jnp.where(row0, bb, ut * kb + ub * kt)], axis=0)
        return _dot(ft, y.astype(BF16))

    for j in range(nb):
        rows = slice(j * seq_len, (j + 1) * seq_len)
        v, x1, x2 = z[rows, 0:c], z[rows, c:2 * c], z[rows, 2 * c:3 * c]
        y1 = x1 * (long_conv(v, 0) + bias_ref[0:1, :] * v)
        o_ref[rows, :] = x2 * (long_conv(y1, 1) + bias_ref[1:2, :] * y1)


def _hyena_fused(x, wl, kf, f_mat, ft_mat, batch, seq_len):
    c = D_BRANCH
    nb = max(1, min(batch, 1024 // seq_len))
    rows = nb * seq_len
    full = lambda a: pl.BlockSpec(a.shape, lambda i: (0,) * a.ndim)
    return pl.pallas_call(
        functools.partial(_hyena_fused_kernel, seq_len, nb),
        grid=(batch // nb,),
        in_specs=[pl.BlockSpec((rows, 3 * c), lambda i: (i, 0)),
                  full(wl["hy_conv_w"]), full(wl["hy_conv_b"]), full(f_mat), full(ft_mat), full(kf),
                  full(wl["hy_bias"])],
        out_specs=pl.BlockSpec((rows, c), lambda i: (i, 0)),
        out_shape=jax.ShapeDtypeStruct((batch * seq_len, c), F32),
        compiler_params=_cp("arbitrary"),
        name="hyena_fused",
    )(x, wl["hy_conv_w"], wl["hy_conv_b"], f_mat, ft_mat, kf, wl["hy_bias"])


def _dft_fwd_kernel(a_ref, x_ref, o_ref):
    o_ref[...] = _dot(a_ref[...], x_ref[...].astype(BF16))


def _dft_fwd(f_mat, x, batch, col_block, width):
    m, k = f_mat.shape
    tm = min(m, 512)
    nm = m // tm
    return pl.pallas_call(
        _dft_fwd_kernel,
        grid=(batch, nm),
        in_specs=[pl.BlockSpec((tm, k), lambda b, i: (i, 0)),
                  pl.BlockSpec((k, width), lambda b, i: (b, col_block))],
        out_specs=pl.BlockSpec((tm, width), lambda b, i: (b * nm + i, 0)),
        out_shape=jax.ShapeDtypeStruct((batch * m, width), F32),
        compiler_params=_cp("arbitrary", "arbitrary"),
        name="dft_fwd",
    )(f_mat, x)


def _hy_inv_kernel(ft_ref, u_ref, kf_ref, x1_ref, v_ref, bias_ref, o_ref, y_sc):
    half = u_ref.shape[0] // 2
    blk = 256

    @pl.when(pl.program_id(1) == 0)
    def _():
        def body(i, carry):
            r = pl.multiple_of(i * blk, blk)
            ut, ub = u_ref[pl.ds(r, blk), :], u_ref[pl.ds(half + r, blk), :]
            kt, kb = kf_ref[pl.ds(r, blk), :], kf_ref[pl.ds(half + r, blk), :]
            row0 = (lax.broadcasted_iota(jnp.int32, (blk, 1), 0) + r) == 0
            tt, bb = ut * kt, ub * kb
            y_sc[pl.ds(r, blk), :] = jnp.where(row0, tt, tt - bb).astype(BF16)
            y_sc[pl.ds(half + r, blk), :] = jnp.where(row0, bb, ut * kb + ub * kt).astype(BF16)
            return carry
        lax.fori_loop(0, half // blk, body, 0)

    y = _dot(ft_ref[...], y_sc[...])
    o_ref[...] = x1_ref[...] * (y + bias_ref[0] * v_ref[...])


def _hy_inv(ft_mat, u, kf, order, x1, x1_col, v, v_col, bias, batch):
    seq_len, m2 = ft_mat.shape
    c = D_BRANCH
    tm = min(seq_len, 512)
    nm = seq_len // tm
    return pl.pallas_call(
        _hy_inv_kernel,
        grid=(batch, nm),
        in_specs=[pl.BlockSpec((tm, m2), lambda b, i: (i, 0)),
                  pl.BlockSpec((m2, c), lambda b, i: (b, 0)),
                  pl.BlockSpec((m2, c), lambda b, i: (0, order)),
                  pl.BlockSpec((tm, c), lambda b, i: (b * nm + i, x1_col)),
                  pl.BlockSpec((tm, c), lambda b, i: (b * nm + i, v_col)),
                  pl.BlockSpec((1, 1, c), lambda b, i: (order, 0, 0))],
        out_specs=pl.BlockSpec((tm, c), lambda b, i: (b * nm + i, 0)),
        out_shape=jax.ShapeDtypeStruct((batch * seq_len, c), F32),
        scratch_shapes=[pltpu.VMEM((m2, c), BF16)],
        compiler_params=_cp("arbitrary", "arbitrary"),
        name="hy_inv",
    )(ft_mat, u, kf, x1, v, bias.reshape(2, 1, c))


def _hy_mlp_kernel(feat_ref, w1_ref, b1_ref, f1_ref, w2_ref, b2_ref, f2_ref, w3_ref, win_ref,
                   filt_ref, nrm_ref):
    i = pl.program_id(0)
    tl = feat_ref.shape[0]
    h = jnp.sin(f1_ref[...] * (_dot3(feat_ref[...], w1_ref[...]) + b1_ref[...]))
    h = jnp.sin(f2_ref[...] * (_dot3(h, w2_ref[...]) + b2_ref[...]))
    filt = _dot3(h, w3_ref[...])
    win = win_ref[...]
    win4 = jnp.concatenate([win, win, win, win], axis=1)
    row0 = (lax.broadcasted_iota(jnp.int32, (tl, 1), 0) + i * tl) == 0
    look_ahead = lax.broadcasted_iota(jnp.int32, (1, 1024), 1) >= 512
    filt = jnp.where(jnp.logical_and(row0, look_ahead), 0.0, filt * win4)
    filt_ref[0] = filt[:, 0:512]
    filt_ref[1] = filt[:, 512:1024]
    part = jnp.sum(jnp.abs(filt[:, 0:512]) + jnp.abs(filt[:, 512:1024]), axis=0, keepdims=True)

    @pl.when(i == 0)
    def _():
        nrm_ref[...] = jnp.zeros_like(nrm_ref)
    nrm_ref[...] += jnp.broadcast_to(part, nrm_ref.shape)


def _hy_mlp(feat, wl, win):
    seq_len = feat.shape[0]
    tl = 256
    full = lambda shape: pl.BlockSpec(shape, lambda i: (0,) * len(shape))
    return pl.pallas_call(
        _hy_mlp_kernel,
        grid=(seq_len // tl,),
        in_specs=[pl.BlockSpec((tl, 128), lambda i: (i, 0)),
                  full((128, 128)), full((1, 128)), full((1, 128)),
                  full((128, 128)), full((1, 128)), full((1, 128)),
                  full((128, 1024)),
                  pl.BlockSpec((tl, 256), lambda i: (i, 0))],
        out_specs=[pl.BlockSpec((2, tl, 512), lambda i: (0, i, 0)), full((8, 512))],
        out_shape=[jax.ShapeDtypeStruct((2, seq_len, 512), F32), jax.ShapeDtypeStruct((8, 512), F32)],
        compiler_params=_cp("arbitrary"),
        name="hy_mlp",
    )(feat, wl["hy_w1"], wl["hy_b1"], wl["hy_f1"], wl["hy_w2"], wl["hy_b2"], wl["hy_f2"], wl["hy_w3"], win)


def _hy_kf_kernel(seq_len, a_ref, b_ref, n_ref, o_ref):
    i = pl.program_id(0)
    tr = a_ref.shape[0]
    grow = lax.broadcasted_iota(jnp.int32, (tr, 1), 0) + i * tr
    first = jnp.logical_or(grow == 0, grow == seq_len)
    imag = grow > seq_len
    a, b = a_ref[...], b_ref[...]
    kf = jnp.where(imag, a - b, a + b)
    scale = jnp.where(first, 0.5 / seq_len, 1.0 / seq_len) / n_ref[0:1, :]
    o_ref[...] = kf * scale


def _hy_kf(kraw, nrm, seq_len):
    m2 = 2 * seq_len
    tr = 512
    nt = m2 // tr
    return pl.pallas_call(
        functools.partial(_hy_kf_kernel, seq_len),
        grid=(nt,),
        in_specs=[pl.BlockSpec((tr, 512), lambda i: (i, 0)),
                  pl.BlockSpec((tr, 512), lambda i: (nt + i, 0)),
                  pl.BlockSpec((8, 512), lambda i: (0, 0))],
        out_specs=pl.BlockSpec((tr, 512), lambda i: (i, 0)),
        out_shape=jax.ShapeDtypeStruct((m2, 512), F32),
        compiler_params=_cp("arbitrary"),
        name="hy_kf",
    )(kraw, kraw, nrm)


def _s5_discretise(are_ref, aim_ref, ldt_ref):
    ar = jnp.minimum(are_ref[0], -1e-4)
    ai = aim_ref[0]
    dt = jnp.exp(ldt_ref[0])
    e = jnp.exp(ar * dt)
    return ar, ai, e * jnp.cos(ai * dt), e * jnp.sin(ai * dt)


def _s5_scan_kernel(nseq, emit_y, u_ref, hin_ref, are_ref, aim_ref, ldt_ref, bre_ref, bim_ref, *rest):
    if emit_y:
        cre_ref, cim_ref, y_ref, hfin_ref, wb_sc, ab_sc, s_sc, hc_sc, lhs_sc, wc_sc = rest
    else:
        hfin_ref, wb_sc, ab_sc, s_sc, hc_sc, lhs_sc = rest
    d = pl.program_id(0)
    c = pl.program_id(1)
    n = S5_N

    @pl.when(c == 0)
    def _():
        ar, ai, abr, abi = _s5_discretise(are_ref, aim_ref, ldt_ref)
        ab_sc[0:1, :] = abr
        ab_sc[1:2, :] = abi
        den = 1.0 / (ar * ar + ai * ai)
        cr = ((abr - 1.0) * ar + abi * ai) * den
        ci = (abi * ar - (abr - 1.0) * ai) * den
        bre, bim = bre_ref[0], bim_ref[0]
        wb_sc[:, 0:n] = (cr * bre - ci * bim).astype(BF16)
        wb_sc[:, n:2 * n] = (cr * bim + ci * bre).astype(BF16)
        if emit_y:
            wc_sc[0:n, :] = cre_ref[0].astype(BF16)
            wc_sc[n:2 * n, :] = (-cim_ref[0]).astype(BF16)
        hc_sc[...] = hin_ref[0]

    steps = u_ref.shape[1]
    for t in range(steps):
        lhs_sc[t * nseq:(t + 1) * nseq, :] = u_ref[:, t, :].astype(BF16)
    lb = 256

    def scan_chunk(order):
        lhs = lhs_sc[...]
        y = None
        for j in range(n // lb):
            lr = slice(lb * j, lb * (j + 1))
            li = slice(n + lb * j, n + lb * (j + 1))
            s_sc[:, lr] = _dot(lhs, wb_sc[:, lr])
            s_sc[:, li] = _dot(lhs, wb_sc[:, li])
            abr = ab_sc[0:1, lr]
            abi = ab_sc[1:2, lr]
            hr, hi = hc_sc[:, lr], hc_sc[:, li]
            for t in order:
                rows = slice(t * nseq, (t + 1) * nseq)
                hr, hi = abr * hr - abi * hi + s_sc[rows, lr], abr * hi + abi * hr + s_sc[rows, li]
                s_sc[rows, lr] = hr
                s_sc[rows, li] = hi
            hc_sc[:, lr] = hr
            hc_sc[:, li] = hi
            if emit_y:
                yj = _dot(s_sc[:, lr].astype(BF16), wc_sc[lr, :]) + _dot(s_sc[:, li].astype(BF16), wc_sc[li, :])
                y = yj if y is None else y + yj
        if emit_y:
            for t in range(steps):
                y_ref[0, :, t, :] = y[t * nseq:(t + 1) * nseq]

    @pl.when(d == 0)
    def _():
        scan_chunk(range(steps))

    @pl.when(d == 1)
    def _():
        scan_chunk(range(steps - 1, -1, -1))

    @pl.when(c == pl.num_programs(1) - 1)
    def _():
        hfin_ref[0] = hc_sc[...]


def _s5_scan(u, hin, wl, emit_y):
    nseq, nstep, _ = u.shape
    steps = S5_ROWS // nseq
    nc = nstep // steps
    n = S5_N
    chunk = lambda d, c: c + d * (nc - 1 - 2 * c)
    per_dir = lambda shape: pl.BlockSpec((1,) + shape, lambda d, c: (d,) + (0,) * len(shape))
    args = [u, hin, wl["s5_are"], wl["s5_aim"], wl["s5_ldt"], wl["s5_bre"], wl["s5_bim"]]
    specs = [pl.BlockSpec((nseq, steps, D_BRANCH), lambda d, c: (0, chunk(d, c), 0)),
             per_dir((nseq, 2 * n)), per_dir((1, n)), per_dir((1, n)), per_dir((1, n)),
             per_dir((D_BRANCH, n)), per_dir((D_BRANCH, n))]
    out_specs = [per_dir((nseq, 2 * n))]
    out_shape = [jax.ShapeDtypeStruct((2, nseq, 2 * n), F32)]
    scratch = [pltpu.VMEM((D_BRANCH, 2 * n), BF16), pltpu.VMEM((8, n), F32),
               pltpu.VMEM((S5_ROWS, 2 * n), F32), pltpu.VMEM((nseq, 2 * n), F32),
               pltpu.VMEM((S5_ROWS, D_BRANCH), BF16)]
    if emit_y:
        args += [wl["s5_cre"], wl["s5_cim"]]
        specs += [per_dir((n, D_BRANCH)), per_dir((n, D_BRANCH))]
        out_specs = [pl.BlockSpec((1, nseq, steps, D_BRANCH), lambda d, c: (d, 0, chunk(d, c), 0))] + out_specs
        out_shape = [jax.ShapeDtypeStruct((2, nseq, nstep, D_BRANCH), F32)] + out_shape
        scratch += [pltpu.VMEM((2 * n, D_BRANCH), BF16)]
    return pl.pallas_call(
        functools.partial(_s5_scan_kernel, nseq, emit_y),
        grid=(2, nc),
        in_specs=specs, out_specs=out_specs, out_shape=out_shape, scratch_shapes=scratch,
        compiler_params=_cp("arbitrary", "arbitrary"),
        name="s5_scan" if emit_y else "s5_scan_finals",
    )(*args)


def _s5_chain_kernel(batch, nseg, f_ref, h0_ref, are_ref, aim_ref, ldt_ref, o_ref):
    d = pl.program_id(0)
    n = S5_N
    _, _, pr, pi = _s5_discretise(are_ref, aim_ref, ldt_ref)
    for _ in range(int(math.log2(S5_SEG))):
        pr, pi = pr * pr - pi * pi, 2.0 * pr * pi
    f = f_ref[0]
    fr, fi = f[:, 0:n], f[:, n:2 * n]
    h0 = h0_ref[0]
    h0r, h0i = h0[:, 0:n], h0[:, n:2 * n]
    nrow = batch * nseg
    seg = jnp.bitwise_and(lax.broadcasted_iota(jnp.int32, (nrow, 1), 0), nseg - 1)

    def run(shift, keep):
        xr, xi = h0r, h0i
        for _ in range(nseg - 1):
            zr = fr + pr * xr - pi * xi
            zi = fi + pr * xi + pi * xr
            xr = h0r + jnp.where(keep, pltpu.roll(zr, shift, 0), 0.0)
            xi = h0i + jnp.where(keep, pltpu.roll(zi, shift, 0), 0.0)
        o_ref[0, :, 0:n] = xr
        o_ref[0, :, n:2 * n] = xi

    @pl.when(d == 0)
    def _():
        run(1, seg != 0)

    @pl.when(d == 1)
    def _():
        run(nrow - 1, seg != nseg - 1)


def _s5_chain(fin, h0rows, wl, batch, nseg):
    nrow = batch * nseg
    n = S5_N
    per_dir = lambda shape: pl.BlockSpec((1,) + shape, lambda d: (d,) + (0,) * len(shape))
    return pl.pallas_call(
        functools.partial(_s5_chain_kernel, batch, nseg),
        grid=(2,),
        in_specs=[per_dir((nrow, 2 * n)), per_dir((nrow, 2 * n)), per_dir((1, n)), per_dir((1, n)), per_dir((1, n))],
        out_specs=per_dir((nrow, 2 * n)),
        out_shape=jax.ShapeDtypeStruct((2, nrow, 2 * n), F32),
        compiler_params=_cp("arbitrary"),
        name="s5_chain",
    )(fin, h0rows, wl["s5_are"], wl["s5_aim"], wl["s5_ldt"])


def _s5_post_kernel(u_ref, yf_ref, yb_ref, d_ref, w_ref, b_ref, o_ref):
    y = d_ref[...] * u_ref[...] + yf_ref[0] + yb_ref[0]
    g = 0.5 * y * (1.0 + jnp.tanh(math.sqrt(2.0 / math.pi) * (y + 0.044715 * (y * y * y))))
    z = _dot(g.astype(BF16), w_ref[...]) + b_ref[...]
    o_ref[...] = g / (1.0 + jnp.exp(-z))


def _s5_post(u_tm, y2, wl):
    n = u_tm.shape[0]
    tm = 512
    c = D_BRANCH
    return pl.pallas_call(
        _s5_post_kernel,
        grid=(n // tm,),
        in_specs=[pl.BlockSpec((tm, c), lambda i: (i, 0)),
                  pl.BlockSpec((1, tm, c), lambda i: (0, i, 0)),
                  pl.BlockSpec((1, tm, c), lambda i: (1, i, 0)),
                  pl.BlockSpec((1, c), lambda i: (0, 0)),
                  pl.BlockSpec((c, c), lambda i: (0, 0)),
                  pl.BlockSpec((1, c), lambda i: (0, 0))],
        out_specs=pl.BlockSpec((tm, c), lambda i: (i, 0)),
        out_shape=jax.ShapeDtypeStruct((n, c), F32),
        compiler_params=_cp("arbitrary"),
        name="s5_post",
    )(u_tm, y2, y2, wl["s5_d"], wl["s5_glu_w"], wl["s5_glu_b"])


def _gla_kernel(seq_len, nb, q_ref, k_ref, v_ref, g_ref, gw_ref, gb_ref, s0_ref, o_ref, sfin_ref,
                qe_sc, upd_sc, dec_sc, sall_sc):
    d = pl.program_id(1)
    sign = 1 - 2 * d
    ck, sup = GLA_CHUNK, GLA_SUPER
    cps = sup // ck
    nsup, nchunk = seq_len // sup, seq_len // ck
    dk, dv = GLA_HEADS * GLA_DK, GLA_HEADS * GLA_DV
    r = lax.broadcasted_iota(jnp.int32, (sup, sup), 0)
    s = lax.broadcasted_iota(jnp.int32, (sup, sup), 1)
    same = lax.shift_right_logical(r, 6) == lax.shift_right_logical(s, 6)
    tri = jnp.logical_and(same, (s - r) * sign <= 0)
    cum_lhs = jnp.where(tri, 1.0, 0.0).astype(BF16)
    pos = jnp.bitwise_and(lax.broadcasted_iota(jnp.int32, (ck, 1), 0), ck - 1)
    is_last = pos == (ck - 1) * (1 - d)
    row_chunk = lax.shift_right_logical(lax.broadcasted_iota(jnp.int32, (sup, 1), 0), 6)
    head_k = lax.shift_right_logical(lax.broadcasted_iota(jnp.int32, (1, dk), 1), 5)
    head_v = lax.shift_right_logical(lax.broadcasted_iota(jnp.int32, (1, dv), 1), 6)
    blockdiag = lax.shift_right_logical(lax.broadcasted_iota(jnp.int32, (dv, 1), 0), 6) == head_k

    for j in range(nb):
        base = j * seq_len

        def sup_body(i, carry, j=j, base=base):
            rows = pl.ds(pl.multiple_of(base + i * sup, sup), sup)
            q = q_ref[rows, :] * (GLA_DK ** -0.5)
            k = k_ref[rows, :]
            v = v_ref[rows, :]
            x = _dot(g_ref[rows, :].astype(BF16), gw_ref[0]) + gb_ref[0]
            la = (jnp.minimum(x, 0.0) - jnp.log(1.0 + jnp.exp(-jnp.abs(x)))) * (1.0 / GLA_TAU)
            cs = _dot(cum_lhs, jnp.concatenate(_split2(la), axis=1))
            bc = cs[:, 0:dk] + cs[:, dk:2 * dk]
            tots = [jnp.sum(jnp.where(is_last, bc[c * ck:(c + 1) * ck], 0.0), axis=0, keepdims=True)
                    for c in range(cps)]
            tot = jnp.concatenate([jnp.broadcast_to(t, (ck, dk)) for t in tots], axis=0)
            ref = 0.5 * tot
            qt = q * jnp.exp(bc - ref)
            kt = (k * jnp.exp(ref - bc)).astype(BF16)
            lhs = jnp.concatenate([jnp.where(head_k == h, qt, 0.0) for h in range(GLA_HEADS)], axis=0).astype(BF16)
            p = _dot_nt(lhs, kt)
            o = None
            for h in range(GLA_HEADS):
                att = jnp.where(tri, p[h * sup:(h + 1) * sup], 0.0).astype(BF16)
                oh = _dot(att, jnp.where(head_v == h, v, 0.0).astype(BF16))
                o = oh if o is None else o + oh
            o_ref[0, rows, :] = o
            qe_sc[j, pl.ds(pl.multiple_of(i * sup, sup), sup), :] = (q * jnp.exp(bc)).astype(BF16)
            kl = (k * jnp.exp(tot - bc)).astype(BF16)
            vb = v.astype(BF16)
            for c in range(cps):
                cr = slice(c * ck, (c + 1) * ck)
                upd_sc[j, i * cps + c] = jnp.where(blockdiag, _dot_tn(vb[cr], kl[cr]), 0.0)
                dec_sc[j, i * cps + c] = jnp.broadcast_to(jnp.exp(tots[c]), (8, dk))
            return carry

        lax.fori_loop(0, nsup, sup_body, 0, unroll=2 if nsup % 2 == 0 else 1)

        def state_body(c, st, j=j):
            ci = c + d * (nchunk - 1 - 2 * c)
            sall_sc[j, ci] = st.astype(BF16)
            return dec_sc[j, ci][0:1, :] * st + upd_sc[j, ci]

        st_fin = jnp.transpose(lax.fori_loop(0, nchunk, state_body, s0_ref[j, 0]))
        for h in range(GLA_HEADS):
            sfin_ref[j, 0, h] = st_fin[h * GLA_DK:(h + 1) * GLA_DK, h * GLA_DV:(h + 1) * GLA_DV]

        def inter_body(i, carry, j=j, base=base):
            rows = pl.ds(pl.multiple_of(base + i * sup, sup), sup)
            qe = qe_sc[j, pl.ds(pl.multiple_of(i * sup, sup), sup), :]
            zero = jnp.zeros_like(qe)
            lhs = jnp.concatenate([jnp.where(row_chunk == c, qe, zero) for c in range(cps)], axis=1)
            st = jnp.concatenate([sall_sc[j, i * cps + c] for c in range(cps)], axis=1)
            o_ref[0, rows, :] += _dot_nt(lhs, st)
            return carry

        lax.fori_loop(0, nsup, inter_body, 0, unroll=2 if nsup % 2 == 0 else 1)


def _gla(gla_in, wl, s0t, batch, seq_len):
    n = gla_in.shape[0]
    dk, dv = GLA_HEADS * GLA_DK, GLA_HEADS * GLA_DV
    nb = max(1, min(batch, 1024 // seq_len))
    rows = nb * seq_len
    nchunk = seq_len // GLA_CHUNK
    return pl.pallas_call(
        functools.partial(_gla_kernel, seq_len, nb),
        grid=(batch // nb, 2),
        in_specs=[pl.BlockSpec((rows, dk), lambda b, d: (b, 0)),
                  pl.BlockSpec((rows, dk), lambda b, d: (b, 1)),
                  pl.BlockSpec((rows, dv), lambda b, d: (b, 1)),
                  pl.BlockSpec((rows, dk), lambda b, d: (b, 4)),
                  pl.BlockSpec((1, dk, dk), lambda b, d: (d, 0, 0)),
                  pl.BlockSpec((1, 1, dk), lambda b, d: (d, 0, 0)),
                  pl.BlockSpec((nb, 1, dv, dk), lambda b, d: (b, d, 0, 0))],
        out_specs=[pl.BlockSpec((1, rows, dv), lambda b, d: (d, b, 0)),
                   pl.BlockSpec((nb, 1, GLA_HEADS, GLA_DK, GLA_DV), lambda b, d: (b, d, 0, 0, 0))],
        out_shape=[jax.ShapeDtypeStruct((2, n, dv), F32),
                   jax.ShapeDtypeStruct((batch, 2, GLA_HEADS, GLA_DK, GLA_DV), F32)],
        scratch_shapes=[pltpu.VMEM((nb, seq_len, dk), BF16),
                        pltpu.VMEM((nb, nchunk, dv, dk), F32),
                        pltpu.VMEM((nb, nchunk, 8, dk), F32),
                        pltpu.VMEM((nb, nchunk, dv, dk), BF16)],
        compiler_params=_cp("arbitrary", "arbitrary"),
        name="gla",
    )(gla_in, gla_in, gla_in, gla_in, wl["gla_gw"], wl["gla_gb"], s0t)


def _outproj_kernel(x_ref, mod_ref, g_ref, om_ref, oh_ref, os_ref, gf_ref, gb_ref, gn_ref, hm_ref, w_ref, y_ref):
    c = D_BRANCH
    g = g_ref[...]
    acc = _dot((om_ref[...] * _silu(g[:, 0:c])).astype(BF16), w_ref[0:c, :])
    acc += _dot((oh_ref[...] * _silu(g[:, c:2 * c])).astype(BF16), w_ref[c:2 * c, :])
    acc += _dot((os_ref[...] * _silu(g[:, 2 * c:3 * c])).astype(BF16), w_ref[2 * c:3 * c, :])
    og = gf_ref[0] + gb_ref[0]
    hi, lo = _split2(og * og)
    ms = _dot(hi, hm_ref[...]) + _dot(lo, hm_ref[...])
    ogn = og * lax.rsqrt(ms + EPS) * gn_ref[...]
    acc += _dot((ogn * _silu(g[:, 3 * c:4 * c])).astype(BF16), w_ref[3 * c:4 * c, :])
    y_ref[...] = x_ref[...] + mod_ref[0, 2:3, :] * acc


def _outproj(x, mod, mod_row, gates, o_mla, o_hy, o_s5, o_gla, wl, tm):
    n, d = x.shape
    c = D_BRANCH
    row = lambda w: pl.BlockSpec((tm, w), lambda i: (i, 0))
    return pl.pallas_call(
        _outproj_kernel,
        grid=(n // tm,),
        in_specs=[row(d),
                  pl.BlockSpec((1, 3, d), lambda i: (mod_row(i), 0, 0)),
                  row(d), row(c), row(c), row(c),
                  pl.BlockSpec((1, tm, c), lambda i: (0, i, 0)),
                  pl.BlockSpec((1, tm, c), lambda i: (1, i, 0)),
                  pl.BlockSpec((1, c), lambda i: (0, 0)),
                  pl.BlockSpec((c, c), lambda i: (0, 0)),
                  pl.BlockSpec((d, d), lambda i: (0, 0))],
        out_specs=row(d),
        out_shape=jax.ShapeDtypeStruct((n, d), F32),
        compiler_params=_cp("arbitrary"),
        name="outproj",
    )(x, mod, gates, o_mla, o_hy, o_s5, o_gla, o_gla, wl["gla_norm"], wl["head_mean"], wl["w_out"])


def _rope_tables(seq_len):
    pos = np.arange(seq_len)
    inv = ROPE_BASE ** (-np.arange(0, 16, 2, dtype=np.float64) / 16.0)
    cos = np.ones((seq_len, HEAD_PAD))
    sin_a = np.zeros((seq_len, HEAD_PAD))
    sin_b = np.zeros((seq_len, HEAD_PAD))
    for base, p in ((MLA_NOPE, pos // GRID_W), (MLA_NOPE + 16, pos % GRID_W)):
        ang = p[:, None].astype(np.float64) * inv[None, :]
        cos[:, base:base + 8] = np.cos(ang)
        cos[:, base + 8:base + 16] = np.cos(ang)
        sin_a[:, base:base + 8] = -np.sin(ang)
        sin_b[:, base + 8:base + 16] = np.sin(ang)
    return tuple(jnp.asarray(t, F32) for t in (cos, sin_a, sin_b))


def _dft_matrices(seq_len):
    n2 = 2 * seq_len
    j = np.arange(seq_len)[:, None]
    t = np.arange(seq_len)[None, :]
    ang = (2.0 * np.pi / n2) * ((j * t) % n2)
    top = np.cos(ang)
    bot = -np.sin(ang)
    bot[0, :] = np.where(np.arange(seq_len) % 2 == 0, 1.0, -1.0)
    f = np.concatenate([top, bot], axis=0)
    return jnp.asarray(f, F32).astype(BF16), jnp.asarray(f.T, F32).astype(BF16)


def _hyena_tables(seq_len):
    pos = np.arange(seq_len, dtype=np.float64)
    t = pos / seq_len
    w = 2.0 * np.pi * pos / seq_len
    bands = np.linspace(1e-4, HY_BANDS - 1, HY_BANDS)
    feat = np.zeros((seq_len, 128))
    feat[:, 0] = t
    feat[:, 1:1 + HY_BANDS] = np.cos(w[:, None] * bands)
    feat[:, 1 + HY_BANDS:HY_FEAT] = np.sin(w[:, None] * bands)
    deltas = np.linspace(math.log(1.0 / HY_TARGET) / HY_FAST_DECAY, math.log(1.0 / HY_TARGET) / HY_SLOW_DECAY,
                         D_BRANCH)
    win = np.exp(-t[:, None] * deltas[None, :]) + HY_SHIFT
    return jnp.asarray(feat, F32), jnp.asarray(win, F32)


def _pad_to(a, shape):
    return jnp.pad(a, [(0, s - d) for s, d in zip(shape, a.shape)])


def _layer_weights(l, p):
    z = lambda *s: jnp.zeros(s, F32)
    w_in = p["w_in"][l]
    col = lambda lo, hi: w_in[:, lo:hi]
    d = D_MODEL
    w_p = jnp.concatenate([
        col(0, 192), z(d, 64), col(192, 320), z(d, 64), col(320, 352), z(d, 32),
        col(352, 608), col(1376, 1632), col(1888, 2144), col(2688, 2944),
        col(608, 1376), col(1632, 1888),
        col(2144, 2272), col(2272, 2400), col(2400, 2656), col(2656, 2688), z(d, 96)], axis=1).astype(BF16)
    wl = {"w_in": w_p, "norm_w": p["norm_w"][l].reshape(1, d), "w_out": p["w_out"][l].astype(BF16)}
    wl["qa_norm"] = _pad_to(p["mla_qa_norm"][l].reshape(1, -1), (1, 256))
    w_uq = _pad_to(p["mla_w_uq"][l].reshape(MLA_Q_RANK, MLA_HEADS, MLA_QK), (256, MLA_HEADS, HEAD_PAD))
    wl["w_uq"] = w_uq.reshape(256, MLA_HEADS * HEAD_PAD).astype(BF16)
    wl["q_norm"] = _pad_to(p["mla_q_norm"][l].reshape(1, -1), (1, HEAD_PAD))
    wl["k_norm"] = _pad_to(p["mla_k_norm"][l].reshape(1, -1), (1, HEAD_PAD))
    wl["kva_norm"] = p["mla_kva_norm"][l].reshape(1, -1)
    w_ukv = p["mla_w_ukv"][l].reshape(MLA_KV_RANK, MLA_HEADS, MLA_NOPE + MLA_V)
    wl["w_uk"] = _pad_to(w_ukv[:, :, :MLA_NOPE], (MLA_KV_RANK, MLA_HEADS, HEAD_PAD)).reshape(MLA_KV_RANK, -1).astype(BF16)
    wl["w_uv"] = w_ukv[:, :, MLA_NOPE:].reshape(MLA_KV_RANK, MLA_HEADS * MLA_V).astype(BF16)
    wl["hy_conv_w"] = p["hy_conv_w"][l]
    wl["hy_conv_b"] = p["hy_conv_b"][l].reshape(1, -1)
    wl["hy_w1"] = _pad_to(p["hy_w1"][l], (128, 128))
    wl["hy_b1"] = _pad_to(p["hy_b1"][l].reshape(1, -1), (1, 128))
    wl["hy_f1"] = _pad_to(p["hy_freq1"][l].reshape(1, -1), (1, 128))
    wl["hy_w2"] = _pad_to(p["hy_w2"][l], (128, 128))
    wl["hy_b2"] = _pad_to(p["hy_b2"][l].reshape(1, -1), (1, 128))
    wl["hy_f2"] = _pad_to(p["hy_freq2"][l].reshape(1, -1), (1, 128))
    wl["hy_w3"] = _pad_to(p["hy_w3"][l], (128, 1024))
    wl["hy_bias"] = p["hy_bias"][l]
    flat = lambda a: a[l].reshape(2, 1, S5_N)
    wl["s5_are"], wl["s5_aim"] = flat(p["s5_a_re"]), flat(p["s5_a_im"])
    wl["s5_ldt"] = jnp.repeat(p["s5_log_dt"][l], S5_STATE, axis=-1).reshape(2, 1, S5_N)
    same_group = jnp.asarray((np.arange(D_BRANCH) // S5_GROUP)[:, None] == (np.arange(S5_N) // S5_STATE)[None, :])
    bd_b = lambda a: jnp.where(same_group, jnp.tile(
        a[l].transpose(0, 1, 3, 2).reshape(2, D_BRANCH, S5_STATE), (1, 1, S5_GROUPS)), 0.0)
    bd_c = lambda a: jnp.where(same_group.T, jnp.tile(
        a[l].transpose(0, 1, 3, 2).reshape(2, S5_N, S5_GROUP), (1, 1, S5_GROUPS)), 0.0)
    wl["s5_bre"], wl["s5_bim"] = bd_b(p["s5_b_re"]), bd_b(p["s5_b_im"])
    wl["s5_cre"], wl["s5_cim"] = bd_c(p["s5_c_re"]), bd_c(p["s5_c_im"])
    wl["s5_d"] = p["s5_d"][l].reshape(1, -1)
    wl["s5_glu_w"] = p["s5_glu_w"][l].astype(BF16)
    wl["s5_glu_b"] = p["s5_glu_b"][l].reshape(1, -1)
    gw = p["gla_gw"][l]
    dk = GLA_HEADS * GLA_DK
    wl["gla_gw"] = jnp.stack([_pad_to(jnp.pad(gw[i], ((GLA_RANK * i, 0), (0, 0))), (dk, dk)) for i in range(2)]).astype(BF16)
    wl["gla_gb"] = p["gla_gb"][l].reshape(2, 1, dk)
    wl["gla_norm"] = jnp.tile(p["gla_norm"][l], GLA_HEADS).reshape(1, -1)
    head = np.arange(D_BRANCH) // GLA_DV
    wl["head_mean"] = jnp.asarray((head[:, None] == head[None, :]) / GLA_DV, BF16)
    return wl


def _hyena_filters(wl, tabs):
    feat, win, f_mat, _ = tabs
    seq_len = feat.shape[0]
    filt, nrm = _hy_mlp(feat, wl, win)
    kraw = _dft_fwd(f_mat, filt.reshape(2 * seq_len, 512), 2, 0, 512)
    return _hy_kf(kraw, nrm, seq_len)


def _trunk_layer(x, mod, mod_row, wl, batch, seq_len, hy_tabs, rope_tabs=None, ctx=None):
    n = batch * seq_len
    tm = 512
    mla_in, gates, hy_in, s5_in, gla_in = _inproj(x, mod, mod_row, wl["norm_w"], wl["w_in"], tm)

    q, k, v, ckv, krope = _mla_prep(mla_in, wl, rope_tabs, seq_len, tm, True)
    kv_parts = [(k, v, seq_len)]
    if ctx is not None:
        k_ctx, v_ctx = _mla_prep(ctx["mla"], wl, None, ctx["past"], 512, False)
        kv_parts = [(k_ctx, v_ctx, ctx["past"])] + kv_parts
    o_mla = _attention(q, kv_parts, batch, seq_len, 256)

    _, _, f_mat, ft_mat = hy_tabs
    kf = _hyena_filters(wl, hy_tabs)
    if seq_len <= 512:
        o_hy = _hyena_fused(hy_in, wl, kf, f_mat, ft_mat, batch, seq_len)
    else:
        z = _hy_conv(hy_in, wl["hy_conv_w"], wl["hy_conv_b"], batch, seq_len)
        u = _dft_fwd(f_mat, z, batch, 0, D_BRANCH)
        y1 = _hy_inv(ft_mat, u, kf, 0, z, 1, z, 0, wl["hy_bias"], batch)
        u = _dft_fwd(f_mat, y1, batch, 0, D_BRANCH)
        o_hy = _hy_inv(ft_mat, u, kf, 1, z, 2, y1, 0, wl["hy_bias"], batch)

    nseg = seq_len // S5_SEG
    nseq = batch * nseg
    u_seg = s5_in.reshape(nseq, S5_SEG, D_BRANCH)
    if ctx is None:
        hin = jnp.zeros((2, nseq, 2 * S5_N), F32)
    else:
        (fin,) = _s5_scan(u_seg, jnp.zeros((2, nseq, 2 * S5_N), F32), wl, False)
        hin = _s5_chain(fin, ctx["s5_h0"], wl, batch, nseg)
    y2, s5_fin = _s5_scan(u_seg, hin, wl, True)
    o_s5 = _s5_post(s5_in, y2.reshape(2, n, D_BRANCH), wl)

    s0 = jnp.zeros((batch, 2, GLA_HEADS * GLA_DV, GLA_HEADS * GLA_DK), F32) if ctx is None else ctx["gla_s0"]
    o_gla, gla_fin = _gla(gla_in, wl, s0, batch, seq_len)

    y = _outproj(x, mod, mod_row, gates, o_mla, o_hy, o_s5, o_gla, wl, tm)
    return y, (ckv, krope, s5_fin, gla_fin)


def kernel(x_prompt, x_sample, c, cache_mla_ckv, cache_mla_krope, state_s5, state_gla, c_ctx, norm_w, ada_w, ada_b, w_in, w_out, mla_qa_norm, mla_kva_norm, mla_w_uq, mla_w_ukv, mla_q_norm, mla_k_norm, hy_conv_w, hy_conv_b, hy_w1, hy_b1, hy_freq1, hy_w2, hy_b2, hy_freq2, hy_w3, hy_bias, s5_a_re, s5_a_im, s5_log_dt, s5_b_re, s5_b_im, s5_c_re, s5_c_im, s5_d, s5_glu_w, s5_glu_b, gla_gw, gla_gb, gla_norm):
    params = dict(norm_w=norm_w, w_in=w_in, w_out=w_out, mla_qa_norm=mla_qa_norm, mla_kva_norm=mla_kva_norm,
                  mla_w_uq=mla_w_uq, mla_w_ukv=mla_w_ukv, mla_q_norm=mla_q_norm, mla_k_norm=mla_k_norm,
                  hy_conv_w=hy_conv_w, hy_conv_b=hy_conv_b, hy_w1=hy_w1, hy_b1=hy_b1, hy_freq1=hy_freq1,
                  hy_w2=hy_w2, hy_b2=hy_b2, hy_freq2=hy_freq2, hy_w3=hy_w3, hy_bias=hy_bias,
                  s5_a_re=s5_a_re, s5_a_im=s5_a_im, s5_log_dt=s5_log_dt, s5_b_re=s5_b_re, s5_b_im=s5_b_im,
                  s5_c_re=s5_c_re, s5_c_im=s5_c_im, s5_d=s5_d, s5_glu_w=s5_glu_w, s5_glu_b=s5_glu_b,
                  gla_gw=gla_gw, gla_gb=gla_gb, gla_norm=gla_norm)
    bp, lp, d = x_prompt.shape
    bs, ls, _ = x_sample.shape
    past = cache_mla_ckv.shape[2]
    n_s5 = S5_N

    conds = jnp.concatenate([c_ctx[None, :], c, jnp.zeros((8 - 1 - bs, d), F32)], axis=0)
    mods = _modulation(conds, ada_w, ada_b).reshape(DEPTH, 8, 3, d)

    tabs_p = _hyena_tables(lp) + _dft_matrices(lp)
    tabs_s = _hyena_tables(ls) + _dft_matrices(ls)
    rope_tabs = _rope_tables(ls)
    tm_s = 512
    nseg = ls // S5_SEG

    y_p = x_prompt.reshape(bp * lp, d)
    y_s = x_sample.reshape(bs * ls, d)
    ckv_l, krope_l, s5_l, gla_l = [], [], [], []
    for l in range(DEPTH):
        wl = _layer_weights(l, params)
        y_p, (ckv, krope, s5_fin, gla_fin) = _trunk_layer(y_p, mods[l], lambda i: 0, wl, bp, lp, tabs_p)
        ckv_l.append(ckv.reshape(bp, lp, -1))
        krope_l.append(krope.reshape(bp, lp, -1))
        s5_l.append(jnp.stack([s5_fin[:, :, :n_s5], s5_fin[:, :, n_s5:]], axis=-1)
                    .reshape(2, bp, S5_GROUPS, S5_STATE, 2).transpose(1, 0, 2, 3, 4))
        gla_l.append(gla_fin)

        mla_ctx = jnp.concatenate([cache_mla_ckv[:, l], jnp.zeros((bs, past, 64), F32), cache_mla_krope[:, l],
                                   jnp.zeros((bs, past, 32), F32)], axis=-1).reshape(bs * past, 256)
        st = state_s5[:, l]
        h0 = jnp.concatenate([st[..., 0].reshape(bs, 2, n_s5), st[..., 1].reshape(bs, 2, n_s5)], axis=-1)
        h0 = h0.transpose(1, 0, 2)
        h0rows = jnp.zeros((2, nseg * bs, 2 * n_s5), F32)
        h0rows = h0rows.at[0, 0::nseg].set(h0[0]).at[1, nseg - 1::nseg].set(h0[1])
        eye_h = jnp.eye(GLA_HEADS, dtype=F32)
        gla_s0 = jnp.einsum("bdhke,hg->bdhegk", state_gla[:, l], eye_h).reshape(
            bs, 2, GLA_HEADS * GLA_DV, GLA_HEADS * GLA_DK)
        ctx = {"mla": mla_ctx, "past": past, "s5_h0": h0rows, "gla_s0": gla_s0}
        y_s, _ = _trunk_layer(y_s, mods[l], lambda i: 1 + (i * tm_s) // ls, wl, bs, ls, tabs_s, rope_tabs, ctx)

    return (y_p.reshape(bp, lp, d), y_s.reshape(bs, ls, d),
            jnp.stack(ckv_l, axis=1), jnp.stack(krope_l, axis=1), jnp.stack(s5_l, axis=1), jnp.stack(gla_l, axis=1))
```

```python
import functools
import math

import numpy as np
import jax
import jax.numpy as jnp
from jax import lax
from jax.experimental import pallas as pl
from jax.experimental.pallas import tpu as pltpu

F32 = jnp.float32
BF16 = jnp.bfloat16

D_MODEL = 1024
DEPTH = 2
GRID_W = 64
D_BRANCH = 256
EPS = 1e-6

MLA_HEADS = 4
MLA_Q_RANK = 192
MLA_KV_RANK = 128
MLA_NOPE = 64
MLA_ROPE = 32
MLA_QK = 96
MLA_V = 64
ROPE_BASE = 10000.0
HEAD_PAD = 128

HY_BANDS = 16
HY_FEAT = 33
HY_HIDDEN = 64
HY_SHIFT = 0.05
HY_FAST_DECAY = 0.3
HY_SLOW_DECAY = 1.5
HY_TARGET = 1e-2
HY_BLOCK = 512

S5_GROUP = 16
S5_GROUPS = 16
S5_STATE = 64
S5_N = S5_GROUPS * S5_STATE
S5_ROWS = 512
S5_SEG = 256

GLA_HEADS = 4
GLA_DK = 32
GLA_DV = 64
GLA_RANK = 16
GLA_TAU = 16.0
GLA_CHUNK = 64
GLA_SUPER = 256

SEG_MLA = (0, 512)
SEG_GATE = (512, 1536)
SEG_HY = (1536, 2304)
SEG_S5 = (2304, 2560)
SEG_GLA = (2560, 3200)
N_PROJ = 3200

VMEM_LIMIT = 48 * 1024 * 1024


def _cp(*sem):
    return pltpu.CompilerParams(dimension_semantics=sem, vmem_limit_bytes=VMEM_LIMIT)


def _dot(a, b):
    return jnp.dot(a, b, preferred_element_type=F32)


def _dot_nt(a, b):
    return lax.dot_general(a, b, (((1,), (1,)), ((), ())), preferred_element_type=F32)


def _dot_tn(a, b):
    return lax.dot_general(a, b, (((0,), (0,)), ((), ())), preferred_element_type=F32)


def _split2(x):
    hi = x.astype(BF16)
    lo = (x - hi.astype(F32)).astype(BF16)
    return hi, lo


def _split3(x):
    h1 = x.astype(BF16)
    r1 = x - h1.astype(F32)
    h2 = r1.astype(BF16)
    h3 = (r1 - h2.astype(F32)).astype(BF16)
    return h1, h2, h3


def _dot3(a, b):
    a1, a2 = _split2(a)
    b1, b2 = _split2(b)
    return _dot(a1, b1) + (_dot(a1, b2) + _dot(a2, b1))


def _silu(z):
    return z / (1.0 + jnp.exp(-z))


def _mod_kernel(c_ref, w_ref, b_ref, o_ref):
    s = _silu(c_ref[...])
    o_ref[0] = _dot(s.astype(BF16), w_ref[0].astype(BF16)) + b_ref[0]


def _modulation(conds, ada_w, ada_b):
    d = D_MODEL
    return pl.pallas_call(
        _mod_kernel,
        grid=(DEPTH, 3),
        in_specs=[pl.BlockSpec((8, d), lambda l, j: (0, 0)),
                  pl.BlockSpec((1, d, d), lambda l, j: (l, 0, j)),
                  pl.BlockSpec((1, 1, d), lambda l, j: (l, 0, j))],
        out_specs=pl.BlockSpec((1, 8, d), lambda l, j: (l, 0, j)),
        out_shape=jax.ShapeDtypeStruct((DEPTH, 8, 3 * d), F32),
        compiler_params=_cp("arbitrary", "arbitrary"),
        name="modulation",
    )(conds, ada_w, ada_b.reshape(DEPTH, 1, 3 * d))


def _inproj_kernel(x_ref, mod_ref, nw_ref, w_ref, o_mla, o_g, o_hy, o_s5, o_gla):
    x = x_ref[...]
    ms = jnp.mean(x * x, axis=-1, keepdims=True)
    y = x * lax.rsqrt(ms + EPS) * nw_ref[...]
    h = (y * (1.0 + mod_ref[0, 1:2, :]) + mod_ref[0, 0:1, :]).astype(BF16)
    for o, (lo, hi) in ((o_mla, SEG_MLA), (o_g, SEG_GATE), (o_hy, SEG_HY), (o_s5, SEG_S5), (o_gla, SEG_GLA)):
        o[...] = _dot(h, w_ref[:, lo:hi])


def _inproj(x, mod, mod_row, norm_w, w_p, tm):
    n, d = x.shape
    widths = [hi - lo for lo, hi in (SEG_MLA, SEG_GATE, SEG_HY, SEG_S5, SEG_GLA)]
    return pl.pallas_call(
        _inproj_kernel,
        grid=(n // tm,),
        in_specs=[pl.BlockSpec((tm, d), lambda i: (i, 0)),
                  pl.BlockSpec((1, 3, d), lambda i: (mod_row(i), 0, 0)),
                  pl.BlockSpec((1, d), lambda i: (0, 0)),
                  pl.BlockSpec((d, N_PROJ), lambda i: (0, 0))],
        out_specs=[pl.BlockSpec((tm, w), lambda i: (i, 0)) for w in widths],
        out_shape=[jax.ShapeDtypeStruct((n, w), F32) for w in widths],
        compiler_params=_cp("arbitrary"),
        name="inproj",
    )(x, mod, norm_w, w_p)


def _head_norm(xh, w):
    ms = jnp.sum(xh * xh, axis=-1, keepdims=True) * (1.0 / MLA_QK)
    return xh * lax.rsqrt(ms + EPS) * w


def _rope(xh, cos, sin_a, sin_b):
    return xh * cos + pltpu.roll(xh, HEAD_PAD - 8, 1) * sin_a + pltpu.roll(xh, 8, 1) * sin_b


def _mla_prep_kernel(has_q, rope, *refs):
    refs = list(refs)
    m_ref = refs.pop(0)
    if has_q:
        qan_ref, wuq_ref, qn_ref, kvn_ref = refs[:4]
        refs = refs[4:]
    wuk_ref, wuv_ref, kn_ref = refs[:3]
    refs = refs[3:]
    if rope:
        cos_ref, sa_ref, sb_ref = refs[:3]
        refs = refs[3:]
        cos, sa, sb = cos_ref[...], sa_ref[...], sb_ref[...]
    if has_q:
        q_ref, k_ref, v_ref, ckv_ref, kro_ref = refs
    else:
        k_ref, v_ref = refs
    m = m_ref[...]
    if has_q:
        cq = m[:, 0:256]
        ms = jnp.sum(cq * cq, axis=-1, keepdims=True) * (1.0 / MLA_Q_RANK)
        cqn = cq * lax.rsqrt(ms + EPS) * qan_ref[...]
        q = _dot(cqn.astype(BF16), wuq_ref[...])
        ckv = m[:, 256:384]
        ckvn = ckv * lax.rsqrt(jnp.mean(ckv * ckv, axis=-1, keepdims=True) + EPS) * kvn_ref[...]
        ckv_ref[...] = ckvn
        kr = m[:, 384:512]
        kro_ref[...] = kr[:, MLA_NOPE:MLA_NOPE + MLA_ROPE]
    else:
        ckvn = m[:, 0:128]
        kr = m[:, 128:256]
    cb = ckvn.astype(BF16)
    kup = _dot(cb, wuk_ref[...])
    v_ref[...] = _dot(cb, wuv_ref[...]).astype(BF16)
    for h in range(MLA_HEADS):
        sl = slice(HEAD_PAD * h, HEAD_PAD * (h + 1))
        kh = _head_norm(kup[:, sl] + kr, kn_ref[...])
        if rope:
            kh = _rope(kh, cos, sa, sb)
        k_ref[:, sl] = kh.astype(BF16)
        if has_q:
            qh = _head_norm(q[:, sl], qn_ref[...])
            if rope:
                qh = _rope(qh, cos, sa, sb)
            q_ref[:, sl] = (qh * (MLA_QK ** -0.5)).astype(BF16)


def _mla_prep(m, wl, rope_tabs, seq_len, tm, has_q):
    n, wm = m.shape
    rope = rope_tabs is not None
    full = lambda shape: pl.BlockSpec(shape, lambda i: (0,) * len(shape))
    args, specs = [m], [pl.BlockSpec((tm, wm), lambda i: (i, 0))]
    if has_q:
        args += [wl["qa_norm"], wl["w_uq"], wl["q_norm"], wl["kva_norm"]]
        specs += [full((1, 256)), full((256, 512)), full((1, 128)), full((1, 128))]
    args += [wl["w_uk"], wl["w_uv"], wl["k_norm"]]
    specs += [full((128, 512)), full((128, 256)), full((1, 128))]
    if rope:
        nt = seq_len // tm
        args += list(rope_tabs)
        specs += [pl.BlockSpec((tm, HEAD_PAD), lambda i: (i % nt, 0))] * 3
    row = lambda w: pl.BlockSpec((tm, w), lambda i: (i, 0))
    out_specs = [row(512), row(256)]
    out_shape = [jax.ShapeDtypeStruct((n, 512), BF16), jax.ShapeDtypeStruct((n, 256), BF16)]
    if has_q:
        out_specs = [row(512)] + out_specs + [row(128), row(MLA_ROPE)]
        out_shape = ([jax.ShapeDtypeStruct((n, 512), BF16)] + out_shape
                     + [jax.ShapeDtypeStruct((n, 128), F32), jax.ShapeDtypeStruct((n, MLA_ROPE), F32)])
    return pl.pallas_call(
        functools.partial(_mla_prep_kernel, has_q, rope),
        grid=(n // tm,),
        in_specs=specs, out_specs=out_specs, out_shape=out_shape,
        compiler_params=_cp("arbitrary"),
        name="mla_prep",
    )(*args)


def _attn_kernel(nparts, q_ref, *refs):
    kv = [(refs[2 * i], refs[2 * i + 1]) for i in range(nparts)]
    o_ref = refs[2 * nparts]
    low = lax.broadcasted_iota(jnp.int32, (1, HEAD_PAD), 1) < MLA_V
    for pair in range(MLA_HEADS // 2):
        v_half = []
        for _, v_ref in kv:
            vp = v_ref[:, HEAD_PAD * pair:HEAD_PAD * (pair + 1)]
            zero = jnp.zeros_like(vp)
            v_half.append((jnp.where(low, vp, zero), jnp.where(low, zero, vp)))
        acc = None
        for j in range(2):
            sl = slice(HEAD_PAD * (2 * pair + j), HEAD_PAD * (2 * pair + j + 1))
            s = [_dot_nt(q_ref[:, sl], k_ref[:, sl]) for k_ref, _ in kv]
            m = functools.reduce(jnp.maximum, [jnp.max(x, axis=-1, keepdims=True) for x in s])
            p = [jnp.exp(x - m) for x in s]
            den = functools.reduce(jnp.add, [jnp.sum(x, axis=-1, keepdims=True) for x in p])
            num = functools.reduce(jnp.add, [_dot(x.astype(BF16), vh[j]) for x, vh in zip(p, v_half)])
            o = num / den
            acc = o if acc is None else acc + o
        o_ref[:, HEAD_PAD * pair:HEAD_PAD * (pair + 1)] = acc


def _attention(q, kv_parts, batch, lq, tq):
    nq = lq // tq
    args, specs = [q], [pl.BlockSpec((tq, 512), lambda b, i: (b * nq + i, 0))]
    for k, v, lk in kv_parts:
        args += [k, v]
        specs += [pl.BlockSpec((lk, 512), lambda b, i: (b, 0)), pl.BlockSpec((lk, 256), lambda b, i: (b, 0))]
    return pl.pallas_call(
        functools.partial(_attn_kernel, len(kv_parts)),
        grid=(batch, nq),
        in_specs=specs,
        out_specs=pl.BlockSpec((tq, 256), lambda b, i: (b * nq + i, 0)),
        out_shape=jax.ShapeDtypeStruct((batch * lq, 256), F32),
        compiler_params=_cp("arbitrary", "arbitrary"),
        name="attention",
    )(*args)


def _hyena_kernel(seq_len, bk, nseq, x_ref, cw_ref, cb_ref, fo_ref, go_ref, k_ref, bias_ref, o_ref,
                  u_sc, y_sc, z_sc):
    c = D_BRANCH
    n = nseq * seq_len
    nblk = seq_len // bk
    pos = jnp.bitwise_and(lax.broadcasted_iota(jnp.int32, (n, 1), 0), seq_len - 1)
    first, last = pos == 0, pos == seq_len - 1

    def short_conv(g):
        cols = slice(g * c, (g + 1) * c)
        x = x_ref[:, cols]
        xm = jnp.where(first, 0.0, pltpu.roll(x, 1, 0))
        xp = jnp.where(last, 0.0, pltpu.roll(x, n - 1, 0))
        return cw_ref[0:1, cols] * xm + cw_ref[1:2, cols] * x + cw_ref[2:3, cols] * xp + cb_ref[:, cols]

    fo, go = fo_ref[...], go_ref[...]

    def long_conv(v, order, emit):
        cols = slice(order * c, (order + 1) * c)
        for j in range(nblk):
            u_sc[j] = _dot(fo, v[j * bk:(j + 1) * bk].astype(BF16))
        rc = 64
        for i in range(nblk):
            def mix(r, carry, i=i):
                top = pl.ds(pl.multiple_of(r * rc, rc), rc)
                bot = pl.ds(pl.multiple_of(bk + r * rc, rc), rc)
                at = ab = None
                for j in range(nblk):
                    q = i - j + nblk - 1
                    kt, kb = k_ref[q, top, cols], k_ref[q, bot, cols]
                    ut, ub = u_sc[j, top, :], u_sc[j, bot, :]
                    pt, pb = ut * kt - ub * kb, ut * kb + ub * kt
                    at, ab = (pt, pb) if at is None else (at + pt, ab + pb)
                z_sc[top, :] = at.astype(BF16)
                z_sc[bot, :] = ab.astype(BF16)
                return carry

            lax.fori_loop(0, bk // rc, mix, 0)
            emit(i, _dot(go, z_sc[...]))

    v_all, x1_all, x2_all = short_conv(0), short_conv(1), short_conv(2)
    for s in range(nseq):
        base = s * seq_len
        v = v_all[base:base + seq_len]

        def emit1(i, y, base=base, v=v):
            r = slice(i * bk, (i + 1) * bk)
            y_sc[r, :] = x1_all[base + i * bk:base + (i + 1) * bk] * (y + bias_ref[0:1, :] * v[r])

        long_conv(v, 0, emit1)
        y1 = y_sc[...]

        def emit2(i, y, base=base, y1=y1):
            r = slice(i * bk, (i + 1) * bk)
            o_ref[base + i * bk:base + (i + 1) * bk, :] = (
                x2_all[base + i * bk:base + (i + 1) * bk] * (y + bias_ref[1:2, :] * y1[r]))

        long_conv(y1, 1, emit2)


def _hyena(x, wl, kspec, fo, go, batch, seq_len):
    c = D_BRANCH
    bk = fo.shape[1]
    nseq = max(1, min(batch, 1024 // seq_len))
    rows = nseq * seq_len
    nblk = seq_len // bk
    full = lambda a: pl.BlockSpec(a.shape, lambda i: (0,) * a.ndim)
    return pl.pallas_call(
        functools.partial(_hyena_kernel, seq_len, bk, nseq),
        grid=(batch // nseq,),
        in_specs=[pl.BlockSpec((rows, 3 * c), lambda i: (i, 0)),
                  full(wl["hy_conv_w"]), full(wl["hy_conv_b"]), full(fo), full(go),
                  pl.BlockSpec(kspec.shape, lambda i: (0, 0, 0), pipeline_mode=pl.Buffered(1)),
                  full(wl["hy_bias"])],
        out_specs=pl.BlockSpec((rows, c), lambda i: (i, 0)),
        out_shape=jax.ShapeDtypeStruct((batch * seq_len, c), F32),
        scratch_shapes=[pltpu.VMEM((nblk, 2 * bk, c), F32), pltpu.VMEM((seq_len, c), F32),
                        pltpu.VMEM((2 * bk, c), BF16)],
        compiler_params=pltpu.CompilerParams(dimension_semantics=("arbitrary",), vmem_limit_bytes=56 * 1024 * 1024),
        name="hyena",
    )(x, wl["hy_conv_w"], wl["hy_conv_b"], fo, go, kspec, wl["hy_bias"])


def _hy_mlp_kernel(feat_ref, w1_ref, b1_ref, f1_ref, w2_ref, b2_ref, f2_ref, w3_ref, win_ref,
                   filt_ref, nrm_ref):
    i = pl.program_id(0)
    tl = feat_ref.shape[0]
    h = jnp.sin(f1_ref[...] * (_dot3(feat_ref[...], w1_ref[...]) + b1_ref[...]))
    h = jnp.sin(f2_ref[...] * (_dot3(h, w2_ref[...]) + b2_ref[...]))
    filt = _dot3(h, w3_ref[...])
    win = win_ref[...]
    win4 = jnp.concatenate([win, win, win, win], axis=1)
    row0 = (lax.broadcasted_iota(jnp.int32, (tl, 1), 0) + i * tl) == 0
    look_ahead = lax.broadcasted_iota(jnp.int32, (1, 1024), 1) >= 512
    filt = jnp.where(jnp.logical_and(row0, look_ahead), 0.0, filt * win4)
    filt_ref[0] = filt[:, 0:512]
    filt_ref[1] = filt[:, 512:1024]
    part = jnp.sum(jnp.abs(filt[:, 0:512]) + jnp.abs(filt[:, 512:1024]), axis=0, keepdims=True)

    @pl.when(i == 0)
    def _():
        nrm_ref[...] = jnp.zeros_like(nrm_ref)
    nrm_ref[...] += jnp.broadcast_to(part, nrm_ref.shape)


def _hy_mlp(feat, wl, win):
    seq_len = feat.shape[0]
    tl = 256
    full = lambda shape: pl.BlockSpec(shape, lambda i: (0,) * len(shape))
    return pl.pallas_call(
        _hy_mlp_kernel,
        grid=(seq_len // tl,),
        in_specs=[pl.BlockSpec((tl, 128), lambda i: (i, 0)),
                  full((128, 128)), full((1, 128)), full((1, 128)),
                  full((128, 128)), full((1, 128)), full((1, 128)),
                  full((128, 1024)),
                  pl.BlockSpec((tl, 256), lambda i: (i, 0))],
        out_specs=[pl.BlockSpec((2, tl, 512), lambda i: (0, i, 0)), full((8, 512))],
        out_shape=[jax.ShapeDtypeStruct((2, seq_len, 512), F32), jax.ShapeDtypeStruct((8, 512), F32)],
        compiler_params=_cp("arbitrary"),
        name="hy_mlp",
    )(feat, wl["hy_w1"], wl["hy_b1"], wl["hy_f1"], wl["hy_w2"], wl["hy_b2"], wl["hy_f2"], wl["hy_w3"], win)


def _hy_kspec_kernel(lo_ref, hi_ref, fo_ref, fb_ref, n_ref, o_ref):
    bk = lo_ref.shape[0]
    k = _dot(fo_ref[...], hi_ref[...].astype(BF16)) + _dot(fb_ref[...], lo_ref[...].astype(BF16))
    o_ref[0] = k * ((1.0 / bk) / n_ref[0:1, :])


def _hy_kspec(kern_lin, nrm, fo, fb):
    n2, bk = fo.shape
    nq = kern_lin.shape[0] // bk - 1
    full = lambda a: pl.BlockSpec(a.shape, lambda q: (0,) * a.ndim)
    return pl.pallas_call(
        _hy_kspec_kernel,
        grid=(nq,),
        in_specs=[pl.BlockSpec((bk, 512), lambda q: (q, 0)),
                  pl.BlockSpec((bk, 512), lambda q: (q + 1, 0)),
                  full(fo), full(fb), full(nrm)],
        out_specs=pl.BlockSpec((1, n2, 512), lambda q: (q, 0, 0)),
        out_shape=jax.ShapeDtypeStruct((nq, n2, 512), F32),
        compiler_params=_cp("arbitrary"),
        name="hy_kspec",
    )(kern_lin, kern_lin, fo, fb, nrm)


def _s5_discretise(are_ref, aim_ref, ldt_ref):
    ar = jnp.minimum(are_ref[0], -1e-4)
    ai = aim_ref[0]
    dt = jnp.exp(ldt_ref[0])
    e = jnp.exp(ar * dt)
    return ar, ai, e * jnp.cos(ai * dt), e * jnp.sin(ai * dt)


def _s5_scan_kernel(nseq, emit_y, u_ref, hin_ref, are_ref, aim_ref, ldt_ref, bre_ref, bim_ref, *rest):
    if emit_y:
        cre_ref, cim_ref, y_ref, hfin_ref, wb_sc, ab_sc, s_sc, hc_sc, lhs_sc, wc_sc = rest
    else:
        hfin_ref, wb_sc, ab_sc, s_sc, hc_sc, lhs_sc = rest
    d = pl.program_id(0)
    c = pl.program_id(1)
    n = S5_N

    @pl.when(c == 0)
    def _():
        ar, ai, abr, abi = _s5_discretise(are_ref, aim_ref, ldt_ref)
        ab_sc[0:1, :] = abr
        ab_sc[1:2, :] = abi
        den = 1.0 / (ar * ar + ai * ai)
        cr = ((abr - 1.0) * ar + abi * ai) * den
        ci = (abi * ar - (abr - 1.0) * ai) * den
        bre, bim = bre_ref[0], bim_ref[0]
        wb_sc[:, 0:n] = (cr * bre - ci * bim).astype(BF16)
        wb_sc[:, n:2 * n] = (cr * bim + ci * bre).astype(BF16)
        if emit_y:
            wc_sc[0:n, :] = cre_ref[0].astype(BF16)
            wc_sc[n:2 * n, :] = (-cim_ref[0]).astype(BF16)
        hc_sc[...] = hin_ref[0]

    steps = u_ref.shape[1]
    for t in range(steps):
        lhs_sc[t * nseq:(t + 1) * nseq, :] = u_ref[:, t, :].astype(BF16)
    lb = 256

    def scan_chunk(order):
        lhs = lhs_sc[...]
        y = None
        for j in range(n // lb):
            lr = slice(lb * j, lb * (j + 1))
            li = slice(n + lb * j, n + lb * (j + 1))
            s_sc[:, lr] = _dot(lhs, wb_sc[:, lr])
            s_sc[:, li] = _dot(lhs, wb_sc[:, li])
            abr = ab_sc[0:1, lr]
            abi = ab_sc[1:2, lr]
            hr, hi = hc_sc[:, lr], hc_sc[:, li]
            for t in order:
                rows = slice(t * nseq, (t + 1) * nseq)
                hr, hi = abr * hr - abi * hi + s_sc[rows, lr], abr * hi + abi * hr + s_sc[rows, li]
                s_sc[rows, lr] = hr
                s_sc[rows, li] = hi
            hc_sc[:, lr] = hr
            hc_sc[:, li] = hi
            if emit_y:
                yj = _dot(s_sc[:, lr].astype(BF16), wc_sc[lr, :]) + _dot(s_sc[:, li].astype(BF16), wc_sc[li, :])
                y = yj if y is None else y + yj
        if emit_y:
            for t in range(steps):
                y_ref[0, :, t, :] = y[t * nseq:(t + 1) * nseq]

    @pl.when(d == 0)
    def _():
        scan_chunk(range(steps))

    @pl.when(d == 1)
    def _():
        scan_chunk(range(steps - 1, -1, -1))

    @pl.when(c == pl.num_programs(1) - 1)
    def _():
        hfin_ref[0] = hc_sc[...]


def _s5_scan(u, hin, wl, emit_y):
    nseq, nstep, _ = u.shape
    steps = S5_ROWS // nseq
    nc = nstep // steps
    n = S5_N
    chunk = lambda d, c: c + d * (nc - 1 - 2 * c)
    per_dir = lambda shape: pl.BlockSpec((1,) + shape, lambda d, c: (d,) + (0,) * len(shape))
    args = [u, hin, wl["s5_are"], wl["s5_aim"], wl["s5_ldt"], wl["s5_bre"], wl["s5_bim"]]
    specs = [pl.BlockSpec((nseq, steps, D_BRANCH), lambda d, c: (0, chunk(d, c), 0)),
             per_dir((nseq, 2 * n)), per_dir((1, n)), per_dir((1, n)), per_dir((1, n)),
             per_dir((D_BRANCH, n)), per_dir((D_BRANCH, n))]
    out_specs = [per_dir((nseq, 2 * n))]
    out_shape = [jax.ShapeDtypeStruct((2, nseq, 2 * n), F32)]
    scratch = [pltpu.VMEM((D_BRANCH, 2 * n), BF16), pltpu.VMEM((8, n), F32),
               pltpu.VMEM((S5_ROWS, 2 * n), F32), pltpu.VMEM((nseq, 2 * n), F32),
               pltpu.VMEM((S5_ROWS, D_BRANCH), BF16)]
    if emit_y:
        args += [wl["s5_cre"], wl["s5_cim"]]
        specs += [per_dir((n, D_BRANCH)), per_dir((n, D_BRANCH))]
        out_specs = [pl.BlockSpec((1, nseq, steps, D_BRANCH), lambda d, c: (d, 0, chunk(d, c), 0))] + out_specs
        out_shape = [jax.ShapeDtypeStruct((2, nseq, nstep, D_BRANCH), F32)] + out_shape
        scratch += [pltpu.VMEM((2 * n, D_BRANCH), BF16)]
    return pl.pallas_call(
        functools.partial(_s5_scan_kernel, nseq, emit_y),
        grid=(2, nc),
        in_specs=specs, out_specs=out_specs, out_shape=out_shape, scratch_shapes=scratch,
        compiler_params=_cp("arbitrary", "arbitrary"),
        name="s5_scan" if emit_y else "s5_scan_finals",
    )(*args)


def _s5_chain_kernel(batch, nseg, f_ref, h0_ref, are_ref, aim_ref, ldt_ref, o_ref):
    d = pl.program_id(0)
    n = S5_N
    _, _, pr, pi = _s5_discretise(are_ref, aim_ref, ldt_ref)
    for _ in range(int(math.log2(S5_SEG))):
        pr, pi = pr * pr - pi * pi, 2.0 * pr * pi
    f = f_ref[0]
    fr, fi = f[:, 0:n], f[:, n:2 * n]
    h0 = h0_ref[0]
    h0r, h0i = h0[:, 0:n], h0[:, n:2 * n]
    nrow = batch * nseg
    seg = jnp.bitwise_and(lax.broadcasted_iota(jnp.int32, (nrow, 1), 0), nseg - 1)

    def run(shift, keep):
        xr, xi = h0r, h0i
        for _ in range(nseg - 1):
            zr = fr + pr * xr - pi * xi
            zi = fi + pr * xi + pi * xr
            xr = h0r + jnp.where(keep, pltpu.roll(zr, shift, 0), 0.0)
            xi = h0i + jnp.where(keep, pltpu.roll(zi, shift, 0), 0.0)
        o_ref[0, :, 0:n] = xr
        o_ref[0, :, n:2 * n] = xi

    @pl.when(d == 0)
    def _():
        run(1, seg != 0)

    @pl.when(d == 1)
    def _():
        run(nrow - 1, seg != nseg - 1)


def _s5_chain(fin, h0rows, wl, batch, nseg):
    nrow = batch * nseg
    n = S5_N
    per_dir = lambda shape: pl.BlockSpec((1,) + shape, lambda d: (d,) + (0,) * len(shape))
    return pl.pallas_call(
        functools.partial(_s5_chain_kernel, batch, nseg),
        grid=(2,),
        in_specs=[per_dir((nrow, 2 * n)), per_dir((nrow, 2 * n)), per_dir((1, n)), per_dir((1, n)), per_dir((1, n))],
        out_specs=per_dir((nrow, 2 * n)),
        out_shape=jax.ShapeDtypeStruct((2, nrow, 2 * n), F32),
        compiler_params=_cp("arbitrary"),
        name="s5_chain",
    )(fin, h0rows, wl["s5_are"], wl["s5_aim"], wl["s5_ldt"])


def _s5_post_kernel(u_ref, yf_ref, yb_ref, d_ref, w_ref, b_ref, o_ref):
    y = d_ref[...] * u_ref[...] + yf_ref[0] + yb_ref[0]
    g = 0.5 * y * (1.0 + jnp.tanh(math.sqrt(2.0 / math.pi) * (y + 0.044715 * (y * y * y))))
    z = _dot(g.astype(BF16), w_ref[...]) + b_ref[...]
    o_ref[...] = g / (1.0 + jnp.exp(-z))


def _s5_post(u_tm, y2, wl):
    n = u_tm.shape[0]
    tm = 512
    c = D_BRANCH
    return pl.pallas_call(
        _s5_post_kernel,
        grid=(n // tm,),
        in_specs=[pl.BlockSpec((tm, c), lambda i: (i, 0)),
                  pl.BlockSpec((1, tm, c), lambda i: (0, i, 0)),
                  pl.BlockSpec((1, tm, c), lambda i: (1, i, 0)),
                  pl.BlockSpec((1, c), lambda i: (0, 0)),
                  pl.BlockSpec((c, c), lambda i: (0, 0)),
                  pl.BlockSpec((1, c), lambda i: (0, 0))],
        out_specs=pl.BlockSpec((tm, c), lambda i: (i, 0)),
        out_shape=jax.ShapeDtypeStruct((n, c), F32),
        compiler_params=_cp("arbitrary"),
        name="s5_post",
    )(u_tm, y2, y2, wl["s5_d"], wl["s5_glu_w"], wl["s5_glu_b"])


def _gla_kernel(seq_len, nb, q_ref, k_ref, v_ref, g_ref, gw_ref, gb_ref, s0_ref, o_ref, sfin_ref,
                qe_sc, upd_sc, dec_sc, sall_sc):
    d = pl.program_id(1)
    sign = 1 - 2 * d
    ck, sup = GLA_CHUNK, GLA_SUPER
    cps = sup // ck
    nsup, nchunk = seq_len // sup, seq_len // ck
    dk, dv = GLA_HEADS * GLA_DK, GLA_HEADS * GLA_DV
    r = lax.broadcasted_iota(jnp.int32, (sup, sup), 0)
    s = lax.broadcasted_iota(jnp.int32, (sup, sup), 1)
    same = lax.shift_right_logical(r, 6) == lax.shift_right_logical(s, 6)
    tri = jnp.logical_and(same, (s - r) * sign <= 0)
    cum_lhs = jnp.where(tri, 1.0, 0.0).astype(BF16)
    pos = jnp.bitwise_and(lax.broadcasted_iota(jnp.int32, (ck, 1), 0), ck - 1)
    is_last = pos == (ck - 1) * (1 - d)
    row_chunk = lax.shift_right_logical(lax.broadcasted_iota(jnp.int32, (sup, 1), 0), 6)
    head_k = lax.shift_right_logical(lax.broadcasted_iota(jnp.int32, (1, dk), 1), 5)
    head_v = lax.shift_right_logical(lax.broadcasted_iota(jnp.int32, (1, dv), 1), 6)
    blockdiag = lax.shift_right_logical(lax.broadcasted_iota(jnp.int32, (dv, 1), 0), 6) == head_k

    for j in range(nb):
        base = j * seq_len

        def sup_body(i, carry, j=j, base=base):
            rows = pl.ds(pl.multiple_of(base + i * sup, sup), sup)
            q = q_ref[rows, :] * (GLA_DK ** -0.5)
            k = k_ref[rows, :]
            v = v_ref[rows, :]
            x = _dot(g_ref[rows, :].astype(BF16), gw_ref[0]) + gb_ref[0]
            la = (jnp.minimum(x, 0.0) - jnp.log(1.0 + jnp.exp(-jnp.abs(x)))) * (1.0 / GLA_TAU)
            cs = _dot(cum_lhs, jnp.concatenate(_split2(la), axis=1))
            bc = cs[:, 0:dk] + cs[:, dk:2 * dk]
            tots = [jnp.sum(jnp.where(is_last, bc[c * ck:(c + 1) * ck], 0.0), axis=0, keepdims=True)
                    for c in range(cps)]
            tot = jnp.concatenate([jnp.broadcast_to(t, (ck, dk)) for t in tots], axis=0)
            ref = 0.5 * tot
            qt = q * jnp.exp(bc - ref)
            kt = (k * jnp.exp(ref - bc)).astype(BF16)
            lhs = jnp.concatenate([jnp.where(head_k == h, qt, 0.0) for h in range(GLA_HEADS)], axis=0).astype(BF16)
            p = _dot_nt(lhs, kt)
            o = None
            for h in range(GLA_HEADS):
                att = jnp.where(tri, p[h * sup:(h + 1) * sup], 0.0).astype(BF16)
                oh = _dot(att, jnp.where(head_v == h, v, 0.0).astype(BF16))
                o = oh if o is None else o + oh
            o_ref[0, rows, :] = o
            qe_sc[j, pl.ds(pl.multiple_of(i * sup, sup), sup), :] = (q * jnp.exp(bc)).astype(BF16)
            kl = (k * jnp.exp(tot - bc)).astype(BF16)
            vb = v.astype(BF16)
            for c in range(cps):
                cr = slice(c * ck, (c + 1) * ck)
                upd_sc[j, i * cps + c] = jnp.where(blockdiag, _dot_tn(vb[cr], kl[cr]), 0.0)
                dec_sc[j, i * cps + c] = jnp.broadcast_to(jnp.exp(tots[c]), (8, dk))
            return carry

        lax.fori_loop(0, nsup, sup_body, 0, unroll=2 if nsup % 2 == 0 else 1)

        def state_body(c, st, j=j):
            ci = c + d * (nchunk - 1 - 2 * c)
            sall_sc[j, ci] = st.astype(BF16)
            return dec_sc[j, ci][0:1, :] * st + upd_sc[j, ci]

        st_fin = jnp.transpose(lax.fori_loop(0, nchunk, state_body, s0_ref[j, 0]))
        for h in range(GLA_HEADS):
            sfin_ref[j, 0, h] = st_fin[h * GLA_DK:(h + 1) * GLA_DK, h * GLA_DV:(h + 1) * GLA_DV]

        def inter_body(i, carry, j=j, base=base):
            rows = pl.ds(pl.multiple_of(base + i * sup, sup), sup)
            qe = qe_sc[j, pl.ds(pl.multiple_of(i * sup, sup), sup), :]
            zero = jnp.zeros_like(qe)
            lhs = jnp.concatenate([jnp.where(row_chunk == c, qe, zero) for c in range(cps)], axis=1)
            st = jnp.concatenate([sall_sc[j, i * cps + c] for c in range(cps)], axis=1)
            o_ref[0, rows, :] += _dot_nt(lhs, st)
            return carry

        lax.fori_loop(0, nsup, inter_body, 0, unroll=2 if nsup % 2 == 0 else 1)


def _gla(gla_in, wl, s0t, batch, seq_len):
    n = gla_in.shape[0]
    dk, dv = GLA_HEADS * GLA_DK, GLA_HEADS * GLA_DV
    nb = max(1, min(batch, 1024 // seq_len))
    rows = nb * seq_len
    nchunk = seq_len // GLA_CHUNK
    return pl.pallas_call(
        functools.partial(_gla_kernel, seq_len, nb),
        grid=(batch // nb, 2),
        in_specs=[pl.BlockSpec((rows, dk), lambda b, d: (b, 0)),
                  pl.BlockSpec((rows, dk), lambda b, d: (b, 1)),
                  pl.BlockSpec((rows, dv), lambda b, d: (b, 1)),
                  pl.BlockSpec((rows, dk), lambda b, d: (b, 4)),
                  pl.BlockSpec((1, dk, dk), lambda b, d: (d, 0, 0)),
                  pl.BlockSpec((1, 1, dk), lambda b, d: (d, 0, 0)),
                  pl.BlockSpec((nb, 1, dv, dk), lambda b, d: (b, d, 0, 0))],
        out_specs=[pl.BlockSpec((1, rows, dv), lambda b, d: (d, b, 0)),
                   pl.BlockSpec((nb, 1, GLA_HEADS, GLA_DK, GLA_DV), lambda b, d: (b, d, 0, 0, 0))],
        out_shape=[jax.ShapeDtypeStruct((2, n, dv), F32),
                   jax.ShapeDtypeStruct((batch, 2, GLA_HEADS, GLA_DK, GLA_DV), F32)],
        scratch_shapes=[pltpu.VMEM((nb, seq_len, dk), BF16),
                        pltpu.VMEM((nb, nchunk, dv, dk), F32),
                        pltpu.VMEM((nb, nchunk, 8, dk), F32),
                        pltpu.VMEM((nb, nchunk, dv, dk), BF16)],
        compiler_params=_cp("arbitrary", "arbitrary"),
        name="gla",
    )(gla_in, gla_in, gla_in, gla_in, wl["gla_gw"], wl["gla_gb"], s0t)


def _outproj_kernel(x_ref, mod_ref, g_ref, om_ref, oh_ref, os_ref, gf_ref, gb_ref, gn_ref, hm_ref, w_ref, y_ref):
    c = D_BRANCH
    g = g_ref[...]
    acc = _dot((om_ref[...] * _silu(g[:, 0:c])).astype(BF16), w_ref[0:c, :])
    acc += _dot((oh_ref[...] * _silu(g[:, c:2 * c])).astype(BF16), w_ref[c:2 * c, :])
    acc += _dot((os_ref[...] * _silu(g[:, 2 * c:3 * c])).astype(BF16), w_ref[2 * c:3 * c, :])
    og = gf_ref[0] + gb_ref[0]
    hi, lo = _split2(og * og)
    ms = _dot(hi, hm_ref[...]) + _dot(lo, hm_ref[...])
    ogn = og * lax.rsqrt(ms + EPS) * gn_ref[...]
    acc += _dot((ogn * _silu(g[:, 3 * c:4 * c])).astype(BF16), w_ref[3 * c:4 * c, :])
    y_ref[...] = x_ref[...] + mod_ref[0, 2:3, :] * acc


def _outproj(x, mod, mod_row, gates, o_mla, o_hy, o_s5, o_gla, wl, tm):
    n, d = x.shape
    c = D_BRANCH
    row = lambda w: pl.BlockSpec((tm, w), lambda i: (i, 0))
    return pl.pallas_call(
        _outproj_kernel,
        grid=(n // tm,),
        in_specs=[row(d),
                  pl.BlockSpec((1, 3, d), lambda i: (mod_row(i), 0, 0)),
                  row(d), row(c), row(c), row(c),
                  pl.BlockSpec((1, tm, c), lambda i: (0, i, 0)),
                  pl.BlockSpec((1, tm, c), lambda i: (1, i, 0)),
                  pl.BlockSpec((1, c), lambda i: (0, 0)),
                  pl.BlockSpec((c, c), lambda i: (0, 0)),
                  pl.BlockSpec((d, d), lambda i: (0, 0))],
        out_specs=row(d),
        out_shape=jax.ShapeDtypeStruct((n, d), F32),
        compiler_params=_cp("arbitrary"),
        name="outproj",
    )(x, mod, gates, o_mla, o_hy, o_s5, o_gla, o_gla, wl["gla_norm"], wl["head_mean"], wl["w_out"])


def _rope_tables(seq_len):
    pos = np.arange(seq_len)
    inv = ROPE_BASE ** (-np.arange(0, 16, 2, dtype=np.float64) / 16.0)
    cos = np.ones((seq_len, HEAD_PAD))
    sin_a = np.zeros((seq_len, HEAD_PAD))
    sin_b = np.zeros((seq_len, HEAD_PAD))
    for base, p in ((MLA_NOPE, pos // GRID_W), (MLA_NOPE + 16, pos % GRID_W)):
        ang = p[:, None].astype(np.float64) * inv[None, :]
        cos[:, base:base + 8] = np.cos(ang)
        cos[:, base + 8:base + 16] = np.cos(ang)
        sin_a[:, base:base + 8] = -np.sin(ang)
        sin_b[:, base + 8:base + 16] = np.sin(ang)
    return tuple(jnp.asarray(t, F32) for t in (cos, sin_a, sin_b))


def _odd_dft(seq_len):
    bk = min(seq_len, HY_BLOCK)
    k = np.arange(bk)[:, None]
    t = np.arange(bk)[None, :]

    def mat(shift):
        ang = (np.pi / (2 * bk)) * (((2 * k + 1) * (t + shift)) % (4 * bk))
        return np.concatenate([np.cos(ang), -np.sin(ang)], axis=0)

    fo = mat(0)
    fb = -mat(bk)
    fb[:, 0] = 0.0
    const = lambda a: jnp.asarray(a, F32).astype(BF16)
    return const(fo), const(fb), const(fo.T)


def _hyena_tables(seq_len):
    pos = np.arange(seq_len, dtype=np.float64)
    t = pos / seq_len
    w = 2.0 * np.pi * pos / seq_len
    bands = np.linspace(1e-4, HY_BANDS - 1, HY_BANDS)
    feat = np.zeros((seq_len, 128))
    feat[:, 0] = t
    feat[:, 1:1 + HY_BANDS] = np.cos(w[:, None] * bands)
    feat[:, 1 + HY_BANDS:HY_FEAT] = np.sin(w[:, None] * bands)
    deltas = np.linspace(math.log(1.0 / HY_TARGET) / HY_FAST_DECAY, math.log(1.0 / HY_TARGET) / HY_SLOW_DECAY,
                         D_BRANCH)
    win = np.exp(-t[:, None] * deltas[None, :]) + HY_SHIFT
    return jnp.asarray(feat, F32), jnp.asarray(win, F32)


def _pad_to(a, shape):
    return jnp.pad(a, [(0, s - d) for s, d in zip(shape, a.shape)])


def _layer_weights(l, p):
    z = lambda *s: jnp.zeros(s, F32)
    w_in = p["w_in"][l]
    col = lambda lo, hi: w_in[:, lo:hi]
    d = D_MODEL
    w_p = jnp.concatenate([
        col(0, 192), z(d, 64), col(192, 320), z(d, 64), col(320, 352), z(d, 32),
        col(352, 608), col(1376, 1632), col(1888, 2144), col(2688, 2944),
        col(608, 1376), col(1632, 1888),
        col(2144, 2272), col(2272, 2400), col(2400, 2656), col(2656, 2688), z(d, 96)], axis=1).astype(BF16)
    wl = {"w_in": w_p, "norm_w": p["norm_w"][l].reshape(1, d), "w_out": p["w_out"][l].astype(BF16)}
    wl["qa_norm"] = _pad_to(p["mla_qa_norm"][l].reshape(1, -1), (1, 256))
    w_uq = _pad_to(p["mla_w_uq"][l].reshape(MLA_Q_RANK, MLA_HEADS, MLA_QK), (256, MLA_HEADS, HEAD_PAD))
    wl["w_uq"] = w_uq.reshape(256, MLA_HEADS * HEAD_PAD).astype(BF16)
    wl["q_norm"] = _pad_to(p["mla_q_norm"][l].reshape(1, -1), (1, HEAD_PAD))
    wl["k_norm"] = _pad_to(p["mla_k_norm"][l].reshape(1, -1), (1, HEAD_PAD))
    wl["kva_norm"] = p["mla_kva_norm"][l].reshape(1, -1)
    w_ukv = p["mla_w_ukv"][l].reshape(MLA_KV_RANK, MLA_HEADS, MLA_NOPE + MLA_V)
    wl["w_uk"] = _pad_to(w_ukv[:, :, :MLA_NOPE], (MLA_KV_RANK, MLA_HEADS, HEAD_PAD)).reshape(MLA_KV_RANK, -1).astype(BF16)
    wl["w_uv"] = w_ukv[:, :, MLA_NOPE:].reshape(MLA_KV_RANK, MLA_HEADS * MLA_V).astype(BF16)
    wl["hy_conv_w"] = p["hy_conv_w"][l]
    wl["hy_conv_b"] = p["hy_conv_b"][l].reshape(1, -1)
    wl["hy_w1"] = _pad_to(p["hy_w1"][l], (128, 128))
    wl["hy_b1"] = _pad_to(p["hy_b1"][l].reshape(1, -1), (1, 128))
    wl["hy_f1"] = _pad_to(p["hy_freq1"][l].reshape(1, -1), (1, 128))
    wl["hy_w2"] = _pad_to(p["hy_w2"][l], (128, 128))
    wl["hy_b2"] = _pad_to(p["hy_b2"][l].reshape(1, -1), (1, 128))
    wl["hy_f2"] = _pad_to(p["hy_freq2"][l].reshape(1, -1), (1, 128))
    wl["hy_w3"] = _pad_to(p["hy_w3"][l], (128, 1024))
    wl["hy_bias"] = p["hy_bias"][l]
    flat = lambda a: a[l].reshape(2, 1, S5_N)
    wl["s5_are"], wl["s5_aim"] = flat(p["s5_a_re"]), flat(p["s5_a_im"])
    wl["s5_ldt"] = jnp.repeat(p["s5_log_dt"][l], S5_STATE, axis=-1).reshape(2, 1, S5_N)
    same_group = jnp.asarray((np.arange(D_BRANCH) // S5_GROUP)[:, None] == (np.arange(S5_N) // S5_STATE)[None, :])
    bd_b = lambda a: jnp.where(same_group, jnp.tile(
        a[l].transpose(0, 1, 3, 2).reshape(2, D_BRANCH, S5_STATE), (1, 1, S5_GROUPS)), 0.0)
    bd_c = lambda a: jnp.where(same_group.T, jnp.tile(
        a[l].transpose(0, 1, 3, 2).reshape(2, S5_N, S5_GROUP), (1, 1, S5_GROUPS)), 0.0)
    wl["s5_bre"], wl["s5_bim"] = bd_b(p["s5_b_re"]), bd_b(p["s5_b_im"])
    wl["s5_cre"], wl["s5_cim"] = bd_c(p["s5_c_re"]), bd_c(p["s5_c_im"])
    wl["s5_d"] = p["s5_d"][l].reshape(1, -1)
    wl["s5_glu_w"] = p["s5_glu_w"][l].astype(BF16)
    wl["s5_glu_b"] = p["s5_glu_b"][l].reshape(1, -1)
    gw = p["gla_gw"][l]
    dk = GLA_HEADS * GLA_DK
    wl["gla_gw"] = jnp.stack([_pad_to(jnp.pad(gw[i], ((GLA_RANK * i, 0), (0, 0))), (dk, dk)) for i in range(2)]).astype(BF16)
    wl["gla_gb"] = p["gla_gb"][l].reshape(2, 1, dk)
    wl["gla_norm"] = jnp.tile(p["gla_norm"][l], GLA_HEADS).reshape(1, -1)
    head = np.arange(D_BRANCH) // GLA_DV
    wl["head_mean"] = jnp.asarray((head[:, None] == head[None, :]) / GLA_DV, BF16)
    return wl


def _hyena_filters(wl, tabs):
    feat, win, fo, fb, _ = tabs
    filt, nrm = _hy_mlp(feat, wl, win)
    kern_lin = jnp.concatenate([jnp.zeros((1, 512), F32), filt[1, 1:][::-1], filt[0]], axis=0)
    return _hy_kspec(kern_lin, nrm, fo, fb)


def _trunk_layer(x, mod, mod_row, wl, batch, seq_len, hy_tabs, rope_tabs=None, ctx=None):
    n = batch * seq_len
    tm = 512
    mla_in, gates, hy_in, s5_in, gla_in = _inproj(x, mod, mod_row, wl["norm_w"], wl["w_in"], tm)

    q, k, v, ckv, krope = _mla_prep(mla_in, wl, rope_tabs, seq_len, tm, True)
    kv_parts = [(k, v, seq_len)]
    if ctx is not None:
        k_ctx, v_ctx = _mla_prep(ctx["mla"], wl, None, ctx["past"], 512, False)
        kv_parts = [(k_ctx, v_ctx, ctx["past"])] + kv_parts
    o_mla = _attention(q, kv_parts, batch, seq_len, 256)

    o_hy = _hyena(hy_in, wl, _hyena_filters(wl, hy_tabs), hy_tabs[2], hy_tabs[4], batch, seq_len)

    nseg = seq_len // S5_SEG
    nseq = batch * nseg
    u_seg = s5_in.reshape(nseq, S5_SEG, D_BRANCH)
    if ctx is None:
        hin = jnp.zeros((2, nseq, 2 * S5_N), F32)
    else:
        (fin,) = _s5_scan(u_seg, jnp.zeros((2, nseq, 2 * S5_N), F32), wl, False)
        hin = _s5_chain(fin, ctx["s5_h0"], wl, batch, nseg)
    y2, s5_fin = _s5_scan(u_seg, hin, wl, True)
    o_s5 = _s5_post(s5_in, y2.reshape(2, n, D_BRANCH), wl)

    s0 = jnp.zeros((batch, 2, GLA_HEADS * GLA_DV, GLA_HEADS * GLA_DK), F32) if ctx is None else ctx["gla_s0"]
    o_gla, gla_fin = _gla(gla_in, wl, s0, batch, seq_len)

    y = _outproj(x, mod, mod_row, gates, o_mla, o_hy, o_s5, o_gla, wl, tm)
    return y, (ckv, krope, s5_fin, gla_fin)


def kernel(x_prompt, x_sample, c, cache_mla_ckv, cache_mla_krope, state_s5, state_gla, c_ctx, norm_w, ada_w, ada_b, w_in, w_out, mla_qa_norm, mla_kva_norm, mla_w_uq, mla_w_ukv, mla_q_norm, mla_k_norm, hy_conv_w, hy_conv_b, hy_w1, hy_b1, hy_freq1, hy_w2, hy_b2, hy_freq2, hy_w3, hy_bias, s5_a_re, s5_a_im, s5_log_dt, s5_b_re, s5_b_im, s5_c_re, s5_c_im, s5_d, s5_glu_w, s5_glu_b, gla_gw, gla_gb, gla_norm):
    params = dict(norm_w=norm_w, w_in=w_in, w_out=w_out, mla_qa_norm=mla_qa_norm, mla_kva_norm=mla_kva_norm,
                  mla_w_uq=mla_w_uq, mla_w_ukv=mla_w_ukv, mla_q_norm=mla_q_norm, mla_k_norm=mla_k_norm,
                  hy_conv_w=hy_conv_w, hy_conv_b=hy_conv_b, hy_w1=hy_w1, hy_b1=hy_b1, hy_freq1=hy_freq1,
                  hy_w2=hy_w2, hy_b2=hy_b2, hy_freq2=hy_freq2, hy_w3=hy_w3, hy_bias=hy_bias,
                  s5_a_re=s5_a_re, s5_a_im=s5_a_im, s5_log_dt=s5_log_dt, s5_b_re=s5_b_re, s5_b_im=s5_b_im,
                  s5_c_re=s5_c_re, s5_c_im=s5_c_im, s5_d=s5_d, s5_glu_w=s5_glu_w, s5_glu_b=s5_glu_b,
                  gla_gw=gla_gw, gla_gb=gla_gb, gla_norm=gla_norm)
    bp, lp, d = x_prompt.shape
    bs, ls, _ = x_sample.shape
    past = cache_mla_ckv.shape[2]
    n_s5 = S5_N

    conds = jnp.concatenate([c_ctx[None, :], c, jnp.zeros((8 - 1 - bs, d), F32)], axis=0)
    mods = _modulation(conds, ada_w, ada_b).reshape(DEPTH, 8, 3, d)

    tabs_p = _hyena_tables(lp) + _odd_dft(lp)
    tabs_s = _hyena_tables(ls) + _odd_dft(ls)
    rope_tabs = _rope_tables(ls)
    tm_s = 512
    nseg = ls // S5_SEG

    y_p = x_prompt.reshape(bp * lp, d)
    y_s = x_sample.reshape(bs * ls, d)
    ckv_l, krope_l, s5_l, gla_l = [], [], [], []
    for l in range(DEPTH):
        wl = _layer_weights(l, params)
        y_p, (ckv, krope, s5_fin, gla_fin) = _trunk_layer(y_p, mods[l], lambda i: 0, wl, bp, lp, tabs_p)
        ckv_l.append(ckv.reshape(bp, lp, -1))
        krope_l.append(krope.reshape(bp, lp, -1))
        s5_l.append(jnp.stack([s5_fin[:, :, :n_s5], s5_fin[:, :, n_s5:]], axis=-1)
                    .reshape(2, bp, S5_GROUPS, S5_STATE, 2).transpose(1, 0, 2, 3, 4))
        gla_l.append(gla_fin)

        mla_ctx = jnp.concatenate([cache_mla_ckv[:, l], jnp.zeros((bs, past, 64), F32), cache_mla_krope[:, l],
                                   jnp.zeros((bs, past, 32), F32)], axis=-1).reshape(bs * past, 256)
        st = state_s5[:, l]
        h0 = jnp.concatenate([st[..., 0].reshape(bs, 2, n_s5), st[..., 1].reshape(bs, 2, n_s5)], axis=-1)
        h0 = h0.transpose(1, 0, 2)
        h0rows = jnp.zeros((2, nseg * bs, 2 * n_s5), F32)
        h0rows = h0rows.at[0, 0::nseg].set(h0[0]).at[1, nseg - 1::nseg].set(h0[1])
        eye_h = jnp.eye(GLA_HEADS, dtype=F32)
        gla_s0 = jnp.einsum("bdhke,hg->bdhegk", state_gla[:, l], eye_h).reshape(
            bs, 2, GLA_HEADS * GLA_DV, GLA_HEADS * GLA_DK)
        ctx = {"mla": mla_ctx, "past": past, "s5_h0": h0rows, "gla_s0": gla_s0}
        y_s, _ = _trunk_layer(y_s, mods[l], lambda i: 1 + (i * tm_s) // ls, wl, bs, ls, tabs_s, rope_tabs, ctx)

    return (y_p.reshape(bp, lp, d), y_s.reshape(bs, ls, d),
            jnp.stack(ckv_l, axis=1), jnp.stack(krope_l, axis=1), jnp.stack(s5_l, axis=1), jnp.stack(gla_l, axis=1))
```

```python
import functools
import math

import numpy as np
import jax
import jax.numpy as jnp
from jax import lax
from jax.experimental import pallas as pl
from jax.experimental.pallas import tpu as pltpu

F32 = jnp.float32
BF16 = jnp.bfloat16

D_MODEL = 1024
DEPTH = 2
GRID_W = 64
D_BRANCH = 256
EPS = 1e-6

MLA_HEADS = 4
MLA_Q_RANK = 192
MLA_KV_RANK = 128
MLA_NOPE = 64
MLA_ROPE = 32
MLA_QK = 96
MLA_V = 64
ROPE_BASE = 10000.0
HEAD_PAD = 128

HY_BANDS = 16
HY_FEAT = 33
HY_HIDDEN = 64
HY_SHIFT = 0.05
HY_FAST_DECAY = 0.3
HY_SLOW_DECAY = 1.5
HY_TARGET = 1e-2
HY_BLOCK = 512

S5_GROUP = 16
S5_GROUPS = 16
S5_STATE = 64
S5_N = S5_GROUPS * S5_STATE
S5_ROWS = 512
S5_SEG = 256

GLA_HEADS = 4
GLA_DK = 32
GLA_DV = 64
GLA_RANK = 16
GLA_TAU = 16.0
GLA_CHUNK = 64
GLA_SUPER = 256

SEG_MLA = (0, 512)
SEG_GATE = (512, 1536)
SEG_HY = (1536, 2304)
SEG_S5 = (2304, 2560)
SEG_GLA = (2560, 3200)
N_PROJ = 3200

VMEM_LIMIT = 48 * 1024 * 1024


def _cp(*sem):
    return pltpu.CompilerParams(dimension_semantics=sem, vmem_limit_bytes=VMEM_LIMIT)


def _dot(a, b):
    return jnp.dot(a, b, preferred_element_type=F32)


def _dot_nt(a, b):
    return lax.dot_general(a, b, (((1,), (1,)), ((), ())), preferred_element_type=F32)


def _dot_tn(a, b):
    return lax.dot_general(a, b, (((0,), (0,)), ((), ())), preferred_element_type=F32)


def _split2(x):
    hi = x.astype(BF16)
    lo = (x - hi.astype(F32)).astype(BF16)
    return hi, lo


def _split3(x):
    h1 = x.astype(BF16)
    r1 = x - h1.astype(F32)
    h2 = r1.astype(BF16)
    h3 = (r1 - h2.astype(F32)).astype(BF16)
    return h1, h2, h3


def _dot3(a, b):
    a1, a2 = _split2(a)
    b1, b2 = _split2(b)
    return _dot(a1, b1) + (_dot(a1, b2) + _dot(a2, b1))


def _silu(z):
    return z / (1.0 + jnp.exp(-z))


def _mod_kernel(c_ref, w_ref, b_ref, o_ref):
    s = _silu(c_ref[...])
    o_ref[0] = _dot(s.astype(BF16), w_ref[0].astype(BF16)) + b_ref[0]


def _modulation(conds, ada_w, ada_b):
    d = D_MODEL
    return pl.pallas_call(
        _mod_kernel,
        grid=(DEPTH, 3),
        in_specs=[pl.BlockSpec((8, d), lambda l, j: (0, 0)),
                  pl.BlockSpec((1, d, d), lambda l, j: (l, 0, j)),
                  pl.BlockSpec((1, 1, d), lambda l, j: (l, 0, j))],
        out_specs=pl.BlockSpec((1, 8, d), lambda l, j: (l, 0, j)),
        out_shape=jax.ShapeDtypeStruct((DEPTH, 8, 3 * d), F32),
        compiler_params=_cp("arbitrary", "arbitrary"),
        name="modulation",
    )(conds, ada_w, ada_b.reshape(DEPTH, 1, 3 * d))


def _inproj_kernel(x_ref, mod_ref, nw_ref, w_ref, o_mla, o_g, o_hy, o_s5, o_gla):
    x = x_ref[...]
    ms = jnp.mean(x * x, axis=-1, keepdims=True)
    y = x * lax.rsqrt(ms + EPS) * nw_ref[...]
    h = (y * (1.0 + mod_ref[0, 1:2, :]) + mod_ref[0, 0:1, :]).astype(BF16)
    for o, (lo, hi) in ((o_mla, SEG_MLA), (o_g, SEG_GATE), (o_hy, SEG_HY), (o_s5, SEG_S5), (o_gla, SEG_GLA)):
        o[...] = _dot(h, w_ref[:, lo:hi])


def _inproj(x, mod, mod_row, norm_w, w_p, tm):
    n, d = x.shape
    widths = [hi - lo for lo, hi in (SEG_MLA, SEG_GATE, SEG_HY, SEG_S5, SEG_GLA)]
    return pl.pallas_call(
        _inproj_kernel,
        grid=(n // tm,),
        in_specs=[pl.BlockSpec((tm, d), lambda i: (i, 0)),
                  pl.BlockSpec((1, 3, d), lambda i: (mod_row(i), 0, 0)),
                  pl.BlockSpec((1, d), lambda i: (0, 0)),
                  pl.BlockSpec((d, N_PROJ), lambda i: (0, 0))],
        out_specs=[pl.BlockSpec((tm, w), lambda i: (i, 0)) for w in widths],
        out_shape=[jax.ShapeDtypeStruct((n, w), F32) for w in widths],
        compiler_params=_cp("arbitrary"),
        name="inproj",
    )(x, mod, norm_w, w_p)


def _head_norm(xh, w):
    ms = jnp.sum(xh * xh, axis=-1, keepdims=True) * (1.0 / MLA_QK)
    return xh * lax.rsqrt(ms + EPS) * w


def _rope(xh, cos, sin_a, sin_b):
    return xh * cos + pltpu.roll(xh, HEAD_PAD - 8, 1) * sin_a + pltpu.roll(xh, 8, 1) * sin_b


def _mla_prep_kernel(has_q, rope, *refs):
    refs = list(refs)
    m_ref = refs.pop(0)
    if has_q:
        qan_ref, wuq_ref, qn_ref, kvn_ref = refs[:4]
        refs = refs[4:]
    wuk_ref, wuv_ref, kn_ref = refs[:3]
    refs = refs[3:]
    if rope:
        cos_ref, sa_ref, sb_ref = refs[:3]
        refs = refs[3:]
        cos, sa, sb = cos_ref[...], sa_ref[...], sb_ref[...]
    if has_q:
        q_ref, k_ref, v_ref, ckv_ref, kro_ref = refs
    else:
        k_ref, v_ref = refs
    m = m_ref[...]
    if has_q:
        cq = m[:, 0:256]
        ms = jnp.sum(cq * cq, axis=-1, keepdims=True) * (1.0 / MLA_Q_RANK)
        cqn = cq * lax.rsqrt(ms + EPS) * qan_ref[...]
        q = _dot(cqn.astype(BF16), wuq_ref[...])
        ckv = m[:, 256:384]
        ckvn = ckv * lax.rsqrt(jnp.mean(ckv * ckv, axis=-1, keepdims=True) + EPS) * kvn_ref[...]
        ckv_ref[...] = ckvn
        kr = m[:, 384:512]
        kro_ref[...] = kr[:, MLA_NOPE:MLA_NOPE + MLA_ROPE]
    else:
        ckvn = m[:, 0:128]
        kr = m[:, 128:256]
    cb = ckvn.astype(BF16)
    kup = _dot(cb, wuk_ref[...])
    v_ref[...] = _dot(cb, wuv_ref[...]).astype(BF16)
    for h in range(MLA_HEADS):
        sl = slice(HEAD_PAD * h, HEAD_PAD * (h + 1))
        kh = _head_norm(kup[:, sl] + kr, kn_ref[...])
        if rope:
            kh = _rope(kh, cos, sa, sb)
        k_ref[:, sl] = kh.astype(BF16)
        if has_q:
            qh = _head_norm(q[:, sl], qn_ref[...])
            if rope:
                qh = _rope(qh, cos, sa, sb)
            q_ref[:, sl] = (qh * (MLA_QK ** -0.5)).astype(BF16)


def _mla_prep(m, wl, rope_tabs, seq_len, tm, has_q):
    n, wm = m.shape
    rope = rope_tabs is not None
    full = lambda shape: pl.BlockSpec(shape, lambda i: (0,) * len(shape))
    args, specs = [m], [pl.BlockSpec((tm, wm), lambda i: (i, 0))]
    if has_q:
        args += [wl["qa_norm"], wl["w_uq"], wl["q_norm"], wl["kva_norm"]]
        specs += [full((1, 256)), full((256, 512)), full((1, 128)), full((1, 128))]
    args += [wl["w_uk"], wl["w_uv"], wl["k_norm"]]
    specs += [full((128, 512)), full((128, 256)), full((1, 128))]
    if rope:
        nt = seq_len // tm
        args += list(rope_tabs)
        specs += [pl.BlockSpec((tm, HEAD_PAD), lambda i: (i % nt, 0))] * 3
    row = lambda w: pl.BlockSpec((tm, w), lambda i: (i, 0))
    out_specs = [row(512), row(256)]
    out_shape = [jax.ShapeDtypeStruct((n, 512), BF16), jax.ShapeDtypeStruct((n, 256), BF16)]
    if has_q:
        out_specs = [row(512)] + out_specs + [row(128), row(MLA_ROPE)]
        out_shape = ([jax.ShapeDtypeStruct((n, 512), BF16)] + out_shape
                     + [jax.ShapeDtypeStruct((n, 128), F32), jax.ShapeDtypeStruct((n, MLA_ROPE), F32)])
    return pl.pallas_call(
        functools.partial(_mla_prep_kernel, has_q, rope),
        grid=(n // tm,),
        in_specs=specs, out_specs=out_specs, out_shape=out_shape,
        compiler_params=_cp("arbitrary"),
        name="mla_prep",
    )(*args)


def _attn_kernel(nparts, q_ref, *refs):
    kv = [(refs[2 * i], refs[2 * i + 1]) for i in range(nparts)]
    o_ref = refs[2 * nparts]
    low = lax.broadcasted_iota(jnp.int32, (1, HEAD_PAD), 1) < MLA_V

    def scores(h):
        sl = slice(HEAD_PAD * h, HEAD_PAD * (h + 1))
        return [_dot_nt(q_ref[:, sl], k_ref[:, sl]) for k_ref, _ in kv]

    s_next = scores(0)
    acc = None
    for h in range(MLA_HEADS):
        pair, j = divmod(h, 2)
        s = s_next
        if h + 1 < MLA_HEADS:
            s_next = scores(h + 1)
        if j == 0:
            v_half = []
            for _, v_ref in kv:
                vp = v_ref[:, HEAD_PAD * pair:HEAD_PAD * (pair + 1)]
                zero = jnp.zeros_like(vp)
                v_half.append((jnp.where(low, vp, zero), jnp.where(low, zero, vp)))
        m = functools.reduce(jnp.maximum, [jnp.max(x, axis=-1, keepdims=True) for x in s])
        p = [jnp.exp(x - m) for x in s]
        den = functools.reduce(jnp.add, [jnp.sum(x, axis=-1, keepdims=True) for x in p])
        num = functools.reduce(jnp.add, [_dot(x.astype(BF16), vh[j]) for x, vh in zip(p, v_half)])
        o = num / den
        acc = o if j == 0 else acc + o
        if j == 1:
            o_ref[:, HEAD_PAD * pair:HEAD_PAD * (pair + 1)] = acc


def _attention(q, kv_parts, batch, lq, tq):
    nq = lq // tq
    args, specs = [q], [pl.BlockSpec((tq, 512), lambda b, i: (b * nq + i, 0))]
    for k, v, lk in kv_parts:
        args += [k, v]
        specs += [pl.BlockSpec((lk, 512), lambda b, i: (b, 0)), pl.BlockSpec((lk, 256), lambda b, i: (b, 0))]
    return pl.pallas_call(
        functools.partial(_attn_kernel, len(kv_parts)),
        grid=(batch, nq),
        in_specs=specs,
        out_specs=pl.BlockSpec((tq, 256), lambda b, i: (b * nq + i, 0)),
        out_shape=jax.ShapeDtypeStruct((batch * lq, 256), F32),
        compiler_params=_cp("arbitrary", "arbitrary"),
        name="attention",
    )(*args)


def _hyena_kernel(seq_len, bk, nseq, x_ref, cw_ref, cb_ref, fo_ref, go_ref, k_ref, bias_ref, o_ref,
                  u_sc, y_sc, z_sc):
    c = D_BRANCH
    n = nseq * seq_len
    nblk = seq_len // bk
    pos = jnp.bitwise_and(lax.broadcasted_iota(jnp.int32, (n, 1), 0), seq_len - 1)
    first, last = pos == 0, pos == seq_len - 1

    def short_conv(g):
        cols = slice(g * c, (g + 1) * c)
        x = x_ref[:, cols]
        xm = jnp.where(first, 0.0, pltpu.roll(x, 1, 0))
        xp = jnp.where(last, 0.0, pltpu.roll(x, n - 1, 0))
        return cw_ref[0:1, cols] * xm + cw_ref[1:2, cols] * x + cw_ref[2:3, cols] * xp + cb_ref[:, cols]

    fo, go = fo_ref[...], go_ref[...]

    def long_conv(v, order, emit):
        cols = slice(order * c, (order + 1) * c)
        for j in range(nblk):
            u_sc[j] = _dot(fo, v[j * bk:(j + 1) * bk].astype(BF16))
        rc = 64
        for i in range(nblk):
            def mix(r, carry, i=i):
                top = pl.ds(pl.multiple_of(r * rc, rc), rc)
                bot = pl.ds(pl.multiple_of(bk + r * rc, rc), rc)
                at = ab = None
                for j in range(nblk):
                    q = i - j + nblk - 1
                    kt, kb = k_ref[q, top, cols], k_ref[q, bot, cols]
                    ut, ub = u_sc[j, top, :], u_sc[j, bot, :]
                    pt, pb = ut * kt - ub * kb, ut * kb + ub * kt
                    at, ab = (pt, pb) if at is None else (at + pt, ab + pb)
                z_sc[top, :] = at.astype(BF16)
                z_sc[bot, :] = ab.astype(BF16)
                return carry

            lax.fori_loop(0, bk // rc, mix, 0, unroll=True if nblk == 1 else 2)
            emit(i, _dot(go, z_sc[...]))

    v_all, x1_all, x2_all = short_conv(0), short_conv(1), short_conv(2)
    for s in range(nseq):
        base = s * seq_len
        v = v_all[base:base + seq_len]

        def emit1(i, y, base=base, v=v):
            r = slice(i * bk, (i + 1) * bk)
            y_sc[r, :] = x1_all[base + i * bk:base + (i + 1) * bk] * (y + bias_ref[0:1, :] * v[r])

        long_conv(v, 0, emit1)
        y1 = y_sc[...]

        def emit2(i, y, base=base, y1=y1):
            r = slice(i * bk, (i + 1) * bk)
            o_ref[base + i * bk:base + (i + 1) * bk, :] = (
                x2_all[base + i * bk:base + (i + 1) * bk] * (y + bias_ref[1:2, :] * y1[r]))

        long_conv(y1, 1, emit2)


def _hyena(x, wl, kspec, fo, go, batch, seq_len):
    c = D_BRANCH
    bk = fo.shape[1]
    nseq = max(1, min(batch, 1024 // seq_len))
    rows = nseq * seq_len
    nblk = seq_len // bk
    full = lambda a: pl.BlockSpec(a.shape, lambda i: (0,) * a.ndim)
    return pl.pallas_call(
        functools.partial(_hyena_kernel, seq_len, bk, nseq),
        grid=(batch // nseq,),
        in_specs=[pl.BlockSpec((rows, 3 * c), lambda i: (i, 0)),
                  full(wl["hy_conv_w"]), full(wl["hy_conv_b"]), full(fo), full(go),
                  pl.BlockSpec(kspec.shape, lambda i: (0, 0, 0), pipeline_mode=pl.Buffered(1)),
                  full(wl["hy_bias"])],
        out_specs=pl.BlockSpec((rows, c), lambda i: (i, 0)),
        out_shape=jax.ShapeDtypeStruct((batch * seq_len, c), F32),
        scratch_shapes=[pltpu.VMEM((nblk, 2 * bk, c), F32), pltpu.VMEM((seq_len, c), F32),
                        pltpu.VMEM((2 * bk, c), BF16)],
        compiler_params=pltpu.CompilerParams(dimension_semantics=("arbitrary",), vmem_limit_bytes=56 * 1024 * 1024),
        name="hyena",
    )(x, wl["hy_conv_w"], wl["hy_conv_b"], fo, go, kspec, wl["hy_bias"])


def _hy_mlp_kernel(feat_ref, w1_ref, b1_ref, f1_ref, w2_ref, b2_ref, f2_ref, w3_ref, win_ref,
                   kern_ref, nrm_ref):
    i = pl.program_id(0)
    tl = feat_ref.shape[0]
    h = jnp.sin(f1_ref[...] * (_dot3(feat_ref[...], w1_ref[...]) + b1_ref[...]))
    h = jnp.sin(f2_ref[...] * (_dot3(h, w2_ref[...]) + b2_ref[...]))
    filt = _dot3(h, w3_ref[...])
    win = win_ref[...]
    row0 = (lax.broadcasted_iota(jnp.int32, (tl, 1), 0) + i * tl) == 0
    filt = jnp.where(row0, 0.0, filt * jnp.concatenate([win, win], axis=1))
    kern_ref[...] = filt
    part = jnp.sum(jnp.abs(filt), axis=0, keepdims=True)

    @pl.when(i == 0)
    def _():
        nrm_ref[...] = jnp.zeros_like(nrm_ref)
    nrm_ref[...] += jnp.broadcast_to(part, nrm_ref.shape)


def _hy_mlp(feat, wl, win):
    rows = feat.shape[0]
    tl = 256
    nl = rows // (2 * tl)
    full = lambda shape: pl.BlockSpec(shape, lambda i: (0,) * len(shape))
    return pl.pallas_call(
        _hy_mlp_kernel,
        grid=(rows // tl,),
        in_specs=[pl.BlockSpec((tl, 128), lambda i: (i, 0)),
                  full((128, 128)), full((1, 128)), full((1, 128)),
                  full((128, 128)), full((1, 128)), full((1, 128)),
                  pl.BlockSpec((128, 512), lambda i: (0, jnp.where(i < nl, 1, 0))),
                  pl.BlockSpec((tl, 256), lambda i: (i, 0))],
        out_specs=[pl.BlockSpec((tl, 512), lambda i: (i, 0)), full((8, 512))],
        out_shape=[jax.ShapeDtypeStruct((rows, 512), F32), jax.ShapeDtypeStruct((8, 512), F32)],
        compiler_params=_cp("arbitrary"),
        name="hy_mlp",
    )(feat, wl["hy_w1"], wl["hy_b1"], wl["hy_f1"], wl["hy_w2"], wl["hy_b2"], wl["hy_f2"], wl["hy_w3"], win)


def _hy_kspec_kernel(lo_ref, hi_ref, fo_ref, fb_ref, n_ref, o_ref):
    bk = lo_ref.shape[0]
    k = _dot(fo_ref[...], hi_ref[...].astype(BF16)) + _dot(fb_ref[...], lo_ref[...].astype(BF16))
    o_ref[0] = k * ((1.0 / bk) / n_ref[0:1, :])


def _hy_kspec(kern_lin, nrm, fo, fb):
    n2, bk = fo.shape
    nq = kern_lin.shape[0] // bk - 1
    full = lambda a: pl.BlockSpec(a.shape, lambda q: (0,) * a.ndim)
    return pl.pallas_call(
        _hy_kspec_kernel,
        grid=(nq,),
        in_specs=[pl.BlockSpec((bk, 512), lambda q: (q, 0)),
                  pl.BlockSpec((bk, 512), lambda q: (q + 1, 0)),
                  full(fo), full(fb), full(nrm)],
        out_specs=pl.BlockSpec((1, n2, 512), lambda q: (q, 0, 0)),
        out_shape=jax.ShapeDtypeStruct((nq, n2, 512), F32),
        compiler_params=_cp("arbitrary"),
        name="hy_kspec",
    )(kern_lin, kern_lin, fo, fb, nrm)


def _s5_discretise(are_ref, aim_ref, ldt_ref):
    ar = jnp.minimum(are_ref[0], -1e-4)
    ai = aim_ref[0]
    dt = jnp.exp(ldt_ref[0])
    e = jnp.exp(ar * dt)
    return ar, ai, e * jnp.cos(ai * dt), e * jnp.sin(ai * dt)


def _s5_scan_kernel(nseq, emit_y, u_ref, hin_ref, are_ref, aim_ref, ldt_ref, bre_ref, bim_ref, *rest):
    if emit_y:
        cre_ref, cim_ref, y_ref, hfin_ref, wb_sc, ab_sc, s_sc, hc_sc, perm_sc, wc_sc = rest
    else:
        hfin_ref, wb_sc, ab_sc, s_sc, hc_sc, perm_sc = rest
    d = pl.program_id(0)
    c = pl.program_id(1)
    n = S5_N

    @pl.when(c == 0)
    def _():
        ar, ai, abr, abi = _s5_discretise(are_ref, aim_ref, ldt_ref)
        ab_sc[0:1, :] = abr
        ab_sc[1:2, :] = abi
        den = 1.0 / (ar * ar + ai * ai)
        cr = ((abr - 1.0) * ar + abi * ai) * den
        ci = (abi * ar - (abr - 1.0) * ai) * den
        bre, bim = bre_ref[0], bim_ref[0]
        wb_sc[:, 0:n] = (cr * bre - ci * bim).astype(BF16)
        wb_sc[:, n:2 * n] = (cr * bim + ci * bre).astype(BF16)
        if emit_y:
            wc_sc[0:n, :] = cre_ref[0].astype(BF16)
            wc_sc[n:2 * n, :] = (-cim_ref[0]).astype(BF16)
        hc_sc[...] = hin_ref[0]

    steps = u_ref.shape[1]
    rows_c = nseq * steps

    @pl.when(c == 0)
    def _():
        i = lax.broadcasted_iota(jnp.int32, (rows_c, rows_c), 0)
        j = lax.broadcasted_iota(jnp.int32, (rows_c, rows_c), 1)
        p = lax.shift_right_logical(i, int(math.log2(nseq)))
        step = p + d * (steps - 1 - 2 * p)
        src = jnp.bitwise_and(i, nseq - 1) * steps + step
        perm_sc[...] = jnp.where(j == src, 1.0, 0.0).astype(BF16)

    lhs = _dot(perm_sc[...], u_ref[...].reshape(rows_c, D_BRANCH).astype(BF16)).astype(BF16)
    lb = 256
    y = None
    for j in range(n // lb):
        lr = slice(lb * j, lb * (j + 1))
        li = slice(n + lb * j, n + lb * (j + 1))
        bur = _dot(lhs, wb_sc[:, lr])
        bui = _dot(lhs, wb_sc[:, li])
        abr = ab_sc[0:1, lr]
        abi = ab_sc[1:2, lr]
        hr, hi = hc_sc[:, lr], hc_sc[:, li]
        for p in range(steps):
            rows = slice(p * nseq, (p + 1) * nseq)
            hr, hi = abr * hr - abi * hi + bur[rows], abr * hi + abi * hr + bui[rows]
            if emit_y:
                s_sc[rows, lr] = hr.astype(BF16)
                s_sc[rows, li] = hi.astype(BF16)
        hc_sc[:, lr] = hr
        hc_sc[:, li] = hi
        if emit_y:
            yj = _dot(s_sc[:, lr], wc_sc[lr, :]) + _dot(s_sc[:, li], wc_sc[li, :])
            y = yj if y is None else y + yj

    if emit_y:
        @pl.when(d == 0)
        def _():
            for p in range(steps):
                y_ref[0, :, p, :] = y[p * nseq:(p + 1) * nseq]

        @pl.when(d == 1)
        def _():
            for p in range(steps):
                y_ref[0, :, steps - 1 - p, :] = y[p * nseq:(p + 1) * nseq]

    @pl.when(c == pl.num_programs(1) - 1)
    def _():
        hfin_ref[0] = hc_sc[...]


def _s5_scan(u, hin, wl, emit_y):
    nseq, nstep, _ = u.shape
    steps = S5_ROWS // nseq
    nc = nstep // steps
    n = S5_N
    chunk = lambda d, c: c + d * (nc - 1 - 2 * c)
    per_dir = lambda shape: pl.BlockSpec((1,) + shape, lambda d, c: (d,) + (0,) * len(shape))
    args = [u, hin, wl["s5_are"], wl["s5_aim"], wl["s5_ldt"], wl["s5_bre"], wl["s5_bim"]]
    specs = [pl.BlockSpec((nseq, steps, D_BRANCH), lambda d, c: (0, chunk(d, c), 0)),
             per_dir((nseq, 2 * n)), per_dir((1, n)), per_dir((1, n)), per_dir((1, n)),
             per_dir((D_BRANCH, n)), per_dir((D_BRANCH, n))]
    out_specs = [per_dir((nseq, 2 * n))]
    out_shape = [jax.ShapeDtypeStruct((2, nseq, 2 * n), F32)]
    scratch = [pltpu.VMEM((D_BRANCH, 2 * n), BF16), pltpu.VMEM((8, n), F32),
               pltpu.VMEM((S5_ROWS, 2 * n), BF16), pltpu.VMEM((nseq, 2 * n), F32),
               pltpu.VMEM((S5_ROWS, S5_ROWS), BF16)]
    if emit_y:
        args += [wl["s5_cre"], wl["s5_cim"]]
        specs += [per_dir((n, D_BRANCH)), per_dir((n, D_BRANCH))]
        out_specs = [pl.BlockSpec((1, nseq, steps, D_BRANCH), lambda d, c: (d, 0, chunk(d, c), 0))] + out_specs
        out_shape = [jax.ShapeDtypeStruct((2, nseq, nstep, D_BRANCH), F32)] + out_shape
        scratch += [pltpu.VMEM((2 * n, D_BRANCH), BF16)]
    return pl.pallas_call(
        functools.partial(_s5_scan_kernel, nseq, emit_y),
        grid=(2, nc),
        in_specs=specs, out_specs=out_specs, out_shape=out_shape, scratch_shapes=scratch,
        compiler_params=_cp("arbitrary", "arbitrary"),
        name="s5_scan" if emit_y else "s5_scan_finals",
    )(*args)


def _s5_chain_kernel(batch, nseg, f_ref, h0_ref, are_ref, aim_ref, ldt_ref, o_ref):
    d = pl.program_id(0)
    n = S5_N
    _, _, pr, pi = _s5_discretise(are_ref, aim_ref, ldt_ref)
    for _ in range(int(math.log2(S5_SEG))):
        pr, pi = pr * pr - pi * pi, 2.0 * pr * pi
    f = f_ref[0]
    fr, fi = f[:, 0:n], f[:, n:2 * n]
    h0 = h0_ref[0]
    h0r, h0i = h0[:, 0:n], h0[:, n:2 * n]
    nrow = batch * nseg
    seg = jnp.bitwise_and(lax.broadcasted_iota(jnp.int32, (nrow, 1), 0), nseg - 1)

    def run(shift, keep):
        xr, xi = h0r, h0i
        for _ in range(nseg - 1):
            zr = fr + pr * xr - pi * xi
            zi = fi + pr * xi + pi * xr
            xr = h0r + jnp.where(keep, pltpu.roll(zr, shift, 0), 0.0)
            xi = h0i + jnp.where(keep, pltpu.roll(zi, shift, 0), 0.0)
        o_ref[0, :, 0:n] = xr
        o_ref[0, :, n:2 * n] = xi

    @pl.when(d == 0)
    def _():
        run(1, seg != 0)

    @pl.when(d == 1)
    def _():
        run(nrow - 1, seg != nseg - 1)


def _s5_chain(fin, h0rows, wl, batch, nseg):
    nrow = batch * nseg
    n = S5_N
    per_dir = lambda shape: pl.BlockSpec((1,) + shape, lambda d: (d,) + (0,) * len(shape))
    return pl.pallas_call(
        functools.partial(_s5_chain_kernel, batch, nseg),
        grid=(2,),
        in_specs=[per_dir((nrow, 2 * n)), per_dir((nrow, 2 * n)), per_dir((1, n)), per_dir((1, n)), per_dir((1, n))],
        out_specs=per_dir((nrow, 2 * n)),
        out_shape=jax.ShapeDtypeStruct((2, nrow, 2 * n), F32),
        compiler_params=_cp("arbitrary"),
        name="s5_chain",
    )(fin, h0rows, wl["s5_are"], wl["s5_aim"], wl["s5_ldt"])


def _gla_kernel(seq_len, nb, q_ref, k_ref, v_ref, g_ref, gw_ref, gb_ref, s0_ref, o_ref, sfin_ref,
                qe_sc, upd_sc, dec_sc, sall_sc, lhs_sc, kt_sc, la_sc):
    d = pl.program_id(1)
    sign = 1 - 2 * d
    ck, sup = GLA_CHUNK, GLA_SUPER
    cps = sup // ck
    nsup, nchunk = seq_len // sup, seq_len // ck
    dk, dv = GLA_HEADS * GLA_DK, GLA_HEADS * GLA_DV
    r = lax.broadcasted_iota(jnp.int32, (sup, sup), 0)
    s = lax.broadcasted_iota(jnp.int32, (sup, sup), 1)
    same = lax.shift_right_logical(r, 6) == lax.shift_right_logical(s, 6)
    tri = jnp.logical_and(same, (s - r) * sign <= 0)
    cum_lhs = jnp.where(tri, 1.0, 0.0).astype(BF16)
    pos = jnp.bitwise_and(lax.broadcasted_iota(jnp.int32, (ck, 1), 0), ck - 1)
    is_last = pos == (ck - 1) * (1 - d)
    row_chunk = lax.shift_right_logical(lax.broadcasted_iota(jnp.int32, (sup, 1), 0), 6)
    head_k = lax.shift_right_logical(lax.broadcasted_iota(jnp.int32, (1, dk), 1), 5)
    head_v = lax.shift_right_logical(lax.broadcasted_iota(jnp.int32, (1, dv), 1), 6)
    blockdiag = lax.shift_right_logical(lax.broadcasted_iota(jnp.int32, (dv, 1), 0), 6) == head_k

    def stage_a(j, i, slot):
        rows = pl.ds(pl.multiple_of(j * seq_len + i * sup, sup), sup)
        q = q_ref[rows, :] * (GLA_DK ** -0.5)
        k = k_ref[rows, :]
        v = v_ref[rows, :]
        cs = _dot(cum_lhs, la_sc[rows, :])
        yield
        bc = cs[:, 0:dk] + cs[:, dk:2 * dk]
        tots = [jnp.sum(jnp.where(is_last, bc[c * ck:(c + 1) * ck], 0.0), axis=0, keepdims=True)
                for c in range(cps)]
        tot = jnp.concatenate([jnp.broadcast_to(t, (ck, dk)) for t in tots], axis=0)
        ref = 0.5 * tot
        qt = q * jnp.exp(bc - ref)
        kt_sc[slot] = (k * jnp.exp(ref - bc)).astype(BF16)
        lhs_sc[slot] = jnp.concatenate(
            [jnp.where(head_k == h, qt, 0.0) for h in range(GLA_HEADS)], axis=0).astype(BF16)
        qe_sc[j, pl.ds(pl.multiple_of(i * sup, sup), sup), :] = (q * jnp.exp(bc)).astype(BF16)
        kl = (k * jnp.exp(tot - bc)).astype(BF16)
        zero = jnp.zeros_like(kl)
        klx = jnp.concatenate([jnp.where(row_chunk == c, kl, zero) for c in range(cps)], axis=1)
        upd = _dot_tn(v.astype(BF16), klx)
        yield
        for c in range(cps):
            upd_sc[j, i * cps + c] = jnp.where(blockdiag, upd[:, c * dk:(c + 1) * dk], 0.0)
            dec_sc[j, i * cps + c] = jnp.broadcast_to(jnp.exp(tots[c]), (8, dk))

    def stage_b(j, i, slot):
        rows = pl.ds(pl.multiple_of(j * seq_len + i * sup, sup), sup)
        v = v_ref[rows, :]
        p = _dot_nt(lhs_sc[slot], kt_sc[slot])
        yield
        att = jnp.concatenate([jnp.where(tri, p[h * sup:(h + 1) * sup], 0.0).astype(BF16)
                               for h in range(GLA_HEADS)], axis=1)
        vexp = jnp.concatenate([jnp.where(head_v == h, v, 0.0).astype(BF16) for h in range(GLA_HEADS)], axis=0)
        o = _dot(att, vexp)
        yield
        o_ref[0, rows, :] = o

    def run(*stages):
        live = list(stages)
        while live:
            for g in list(live):
                if next(g, StopIteration) is StopIteration:
                    live.remove(g)

    x = _dot(g_ref[...].astype(BF16), gw_ref[0]) + gb_ref[0]
    la = (jnp.minimum(x, 0.0) - jnp.log(1.0 + jnp.exp(-jnp.abs(x)))) * (1.0 / GLA_TAU)
    la_sc[...] = jnp.concatenate(_split2(la), axis=1)

    if nb * nsup <= 4:
        units = [(j, i) for j in range(nb) for i in range(nsup)]
        run(stage_a(*units[0], 0))
        for n, (j, i) in enumerate(units):
            if n + 1 < len(units):
                run(stage_b(j, i, n % 2), stage_a(*units[n + 1], (n + 1) % 2))
            else:
                run(stage_b(j, i, n % 2))
    else:
        for j in range(nb):
            run(stage_a(j, 0, 0))

            def sup_body(i, carry, j=j):
                nxt = jnp.minimum(i + 1, nsup - 1)
                run(stage_b(j, i, jnp.bitwise_and(i, 1)), stage_a(j, nxt, jnp.bitwise_and(i + 1, 1)))
                return carry

            lax.fori_loop(0, nsup, sup_body, 0)

    for j in range(nb):
        base = j * seq_len

        def state_body(c, st, j=j):
            ci = c + d * (nchunk - 1 - 2 * c)
            sall_sc[j, ci] = st.astype(BF16)
            return dec_sc[j, ci][0:1, :] * st + upd_sc[j, ci]

        st_fin = jnp.transpose(lax.fori_loop(0, nchunk, state_body, s0_ref[j, 0]))
        for h in range(GLA_HEADS):
            sfin_ref[j, 0, h] = st_fin[h * GLA_DK:(h + 1) * GLA_DK, h * GLA_DV:(h + 1) * GLA_DV]

        def inter_body(i, carry, j=j, base=base):
            rows = pl.ds(pl.multiple_of(base + i * sup, sup), sup)
            qe = qe_sc[j, pl.ds(pl.multiple_of(i * sup, sup), sup), :]
            zero = jnp.zeros_like(qe)
            lhs = jnp.concatenate([jnp.where(row_chunk == c, qe, zero) for c in range(cps)], axis=1)
            st = jnp.concatenate([sall_sc[j, i * cps + c] for c in range(cps)], axis=1)
            o_ref[0, rows, :] += _dot_nt(lhs, st)
            return carry

        lax.fori_loop(0, nsup, inter_body, 0, unroll=2 if nsup % 2 == 0 else 1)


def _gla(gla_in, wl, s0t, batch, seq_len):
    n = gla_in.shape[0]
    dk, dv = GLA_HEADS * GLA_DK, GLA_HEADS * GLA_DV
    nb = max(1, min(batch, 1024 // seq_len))
    rows = nb * seq_len
    nchunk = seq_len // GLA_CHUNK
    return pl.pallas_call(
        functools.partial(_gla_kernel, seq_len, nb),
        grid=(batch // nb, 2),
        in_specs=[pl.BlockSpec((rows, dk), lambda b, d: (b, 0)),
                  pl.BlockSpec((rows, dk), lambda b, d: (b, 1)),
                  pl.BlockSpec((rows, dv), lambda b, d: (b, 1)),
                  pl.BlockSpec((rows, dk), lambda b, d: (b, 4)),
                  pl.BlockSpec((1, dk, dk), lambda b, d: (d, 0, 0)),
                  pl.BlockSpec((1, 1, dk), lambda b, d: (d, 0, 0)),
                  pl.BlockSpec((nb, 1, dv, dk), lambda b, d: (b, d, 0, 0))],
        out_specs=[pl.BlockSpec((1, rows, dv), lambda b, d: (d, b, 0)),
                   pl.BlockSpec((nb, 1, GLA_HEADS, GLA_DK, GLA_DV), lambda b, d: (b, d, 0, 0, 0))],
        out_shape=[jax.ShapeDtypeStruct((2, n, dv), F32),
                   jax.ShapeDtypeStruct((batch, 2, GLA_HEADS, GLA_DK, GLA_DV), F32)],
        scratch_shapes=[pltpu.VMEM((nb, seq_len, dk), BF16),
                        pltpu.VMEM((nb, nchunk, dv, dk), F32),
                        pltpu.VMEM((nb, nchunk, 8, dk), F32),
                        pltpu.VMEM((nb, nchunk, dv, dk), BF16),
                        pltpu.VMEM((2, GLA_HEADS * GLA_SUPER, dk), BF16),
                        pltpu.VMEM((2, GLA_SUPER, dk), BF16),
                        pltpu.VMEM((rows, 2 * dk), BF16)],
        compiler_params=_cp("arbitrary", "arbitrary"),
        name="gla",
    )(gla_in, gla_in, gla_in, gla_in, wl["gla_gw"], wl["gla_gb"], s0t)


def _outproj_kernel(x_ref, mod_ref, g_ref, om_ref, oh_ref, su_ref, sf_ref, sb_ref, sd_ref, sw_ref, sbias_ref,
                    gf_ref, gb_ref, gn_ref, hm_ref, w_ref, y_ref):
    c = D_BRANCH
    g = g_ref[...]
    acc = _dot((om_ref[...] * _silu(g[:, 0:c])).astype(BF16), w_ref[0:c, :])
    acc += _dot((oh_ref[...] * _silu(g[:, c:2 * c])).astype(BF16), w_ref[c:2 * c, :])
    ys = sd_ref[...] * su_ref[...] + sf_ref[0] + sb_ref[0]
    ge = 0.5 * ys * (1.0 + jnp.tanh(math.sqrt(2.0 / math.pi) * (ys + 0.044715 * (ys * ys * ys))))
    o_s5 = ge / (1.0 + jnp.exp(-(_dot(ge.astype(BF16), sw_ref[...]) + sbias_ref[...])))
    acc += _dot((o_s5 * _silu(g[:, 2 * c:3 * c])).astype(BF16), w_ref[2 * c:3 * c, :])
    og = gf_ref[0] + gb_ref[0]
    hi, lo = _split2(og * og)
    ms = _dot(hi, hm_ref[...]) + _dot(lo, hm_ref[...])
    ogn = og * lax.rsqrt(ms + EPS) * gn_ref[...]
    acc += _dot((ogn * _silu(g[:, 3 * c:4 * c])).astype(BF16), w_ref[3 * c:4 * c, :])
    y_ref[...] = x_ref[...] + mod_ref[0, 2:3, :] * acc


def _outproj(x, mod, mod_row, gates, o_mla, o_hy, s5_u, s5_y, o_gla, wl, tm):
    n, d = x.shape
    c = D_BRANCH
    row = lambda w: pl.BlockSpec((tm, w), lambda i: (i, 0))
    per_dir = lambda k: pl.BlockSpec((1, tm, c), lambda i: (k, i, 0))
    full = lambda *shape: pl.BlockSpec(shape, lambda i: (0,) * len(shape))
    return pl.pallas_call(
        _outproj_kernel,
        grid=(n // tm,),
        in_specs=[row(d),
                  pl.BlockSpec((1, 3, d), lambda i: (mod_row(i), 0, 0)),
                  row(d), row(c), row(c),
                  row(c), per_dir(0), per_dir(1), full(1, c), full(c, c), full(1, c),
                  per_dir(0), per_dir(1), full(1, c), full(c, c), full(d, d)],
        out_specs=row(d),
        out_shape=jax.ShapeDtypeStruct((n, d), F32),
        compiler_params=_cp("arbitrary"),
        name="outproj",
    )(x, mod, gates, o_mla, o_hy, s5_u, s5_y, s5_y, wl["s5_d"], wl["s5_glu_w"], wl["s5_glu_b"],
      o_gla, o_gla, wl["gla_norm"], wl["head_mean"], wl["w_out"])


def _rope_tables(seq_len):
    pos = np.arange(seq_len)
    inv = ROPE_BASE ** (-np.arange(0, 16, 2, dtype=np.float64) / 16.0)
    cos = np.ones((seq_len, HEAD_PAD))
    sin_a = np.zeros((seq_len, HEAD_PAD))
    sin_b = np.zeros((seq_len, HEAD_PAD))
    for base, p in ((MLA_NOPE, pos // GRID_W), (MLA_NOPE + 16, pos % GRID_W)):
        ang = p[:, None].astype(np.float64) * inv[None, :]
        cos[:, base:base + 8] = np.cos(ang)
        cos[:, base + 8:base + 16] = np.cos(ang)
        sin_a[:, base:base + 8] = -np.sin(ang)
        sin_b[:, base + 8:base + 16] = np.sin(ang)
    return tuple(jnp.asarray(t, F32) for t in (cos, sin_a, sin_b))


def _odd_dft(seq_len):
    bk = min(seq_len, HY_BLOCK)
    k = np.arange(bk)[:, None]
    t = np.arange(bk)[None, :]

    def mat(shift):
        ang = (np.pi / (2 * bk)) * (((2 * k + 1) * (t + shift)) % (4 * bk))
        return np.concatenate([np.cos(ang), -np.sin(ang)], axis=0)

    fo = mat(0)
    fb = -mat(bk)
    fb[:, 0] = 0.0
    const = lambda a: jnp.asarray(a, F32).astype(BF16)
    return const(fo), const(fb), const(fo.T)


def _hyena_tables(seq_len):
    lag = np.arange(-seq_len, seq_len)
    pos = np.where(lag == -seq_len, 0, np.abs(lag)).astype(np.float64)
    t = pos / seq_len
    w = 2.0 * np.pi * pos / seq_len
    bands = np.linspace(1e-4, HY_BANDS - 1, HY_BANDS)
    feat = np.zeros((2 * seq_len, 128))
    feat[:, 0] = t
    feat[:, 1:1 + HY_BANDS] = np.cos(w[:, None] * bands)
    feat[:, 1 + HY_BANDS:HY_FEAT] = np.sin(w[:, None] * bands)
    deltas = np.linspace(math.log(1.0 / HY_TARGET) / HY_FAST_DECAY, math.log(1.0 / HY_TARGET) / HY_SLOW_DECAY,
                         D_BRANCH)
    win = np.exp(-t[:, None] * deltas[None, :]) + HY_SHIFT
    return jnp.asarray(feat, F32), jnp.asarray(win, F32)


def _pad_to(a, shape):
    return jnp.pad(a, [(0, s - d) for s, d in zip(shape, a.shape)])


def _layer_weights(l, p):
    z = lambda *s: jnp.zeros(s, F32)
    w_in = p["w_in"][l]
    col = lambda lo, hi: w_in[:, lo:hi]
    d = D_MODEL
    w_p = jnp.concatenate([
        col(0, 192), z(d, 64), col(192, 320), z(d, 64), col(320, 352), z(d, 32),
        col(352, 608), col(1376, 1632), col(1888, 2144), col(2688, 2944),
        col(608, 1376), col(1632, 1888),
        col(2144, 2272), col(2272, 2400), col(2400, 2656), col(2656, 2688), z(d, 96)], axis=1).astype(BF16)
    wl = {"w_in": w_p, "norm_w": p["norm_w"][l].reshape(1, d), "w_out": p["w_out"][l].astype(BF16)}
    wl["qa_norm"] = _pad_to(p["mla_qa_norm"][l].reshape(1, -1), (1, 256))
    w_uq = _pad_to(p["mla_w_uq"][l].reshape(MLA_Q_RANK, MLA_HEADS, MLA_QK), (256, MLA_HEADS, HEAD_PAD))
    wl["w_uq"] = w_uq.reshape(256, MLA_HEADS * HEAD_PAD).astype(BF16)
    wl["q_norm"] = _pad_to(p["mla_q_norm"][l].reshape(1, -1), (1, HEAD_PAD))
    wl["k_norm"] = _pad_to(p["mla_k_norm"][l].reshape(1, -1), (1, HEAD_PAD))
    wl["kva_norm"] = p["mla_kva_norm"][l].reshape(1, -1)
    w_ukv = p["mla_w_ukv"][l].reshape(MLA_KV_RANK, MLA_HEADS, MLA_NOPE + MLA_V)
    wl["w_uk"] = _pad_to(w_ukv[:, :, :MLA_NOPE], (MLA_KV_RANK, MLA_HEADS, HEAD_PAD)).reshape(MLA_KV_RANK, -1).astype(BF16)
    wl["w_uv"] = w_ukv[:, :, MLA_NOPE:].reshape(MLA_KV_RANK, MLA_HEADS * MLA_V).astype(BF16)
    wl["hy_conv_w"] = p["hy_conv_w"][l]
    wl["hy_conv_b"] = p["hy_conv_b"][l].reshape(1, -1)
    wl["hy_w1"] = _pad_to(p["hy_w1"][l], (128, 128))
    wl["hy_b1"] = _pad_to(p["hy_b1"][l].reshape(1, -1), (1, 128))
    wl["hy_f1"] = _pad_to(p["hy_freq1"][l].reshape(1, -1), (1, 128))
    wl["hy_w2"] = _pad_to(p["hy_w2"][l], (128, 128))
    wl["hy_b2"] = _pad_to(p["hy_b2"][l].reshape(1, -1), (1, 128))
    wl["hy_f2"] = _pad_to(p["hy_freq2"][l].reshape(1, -1), (1, 128))
    wl["hy_w3"] = _pad_to(p["hy_w3"][l], (128, 1024))
    wl["hy_bias"] = p["hy_bias"][l]
    flat = lambda a: a[l].reshape(2, 1, S5_N)
    wl["s5_are"], wl["s5_aim"] = flat(p["s5_a_re"]), flat(p["s5_a_im"])
    wl["s5_ldt"] = jnp.repeat(p["s5_log_dt"][l], S5_STATE, axis=-1).reshape(2, 1, S5_N)
    same_group = jnp.asarray((np.arange(D_BRANCH) // S5_GROUP)[:, None] == (np.arange(S5_N) // S5_STATE)[None, :])
    bd_b = lambda a: jnp.where(same_group, jnp.tile(
        a[l].transpose(0, 1, 3, 2).reshape(2, D_BRANCH, S5_STATE), (1, 1, S5_GROUPS)), 0.0)
    bd_c = lambda a: jnp.where(same_group.T, jnp.tile(
        a[l].transpose(0, 1, 3, 2).reshape(2, S5_N, S5_GROUP), (1, 1, S5_GROUPS)), 0.0)
    wl["s5_bre"], wl["s5_bim"] = bd_b(p["s5_b_re"]), bd_b(p["s5_b_im"])
    wl["s5_cre"], wl["s5_cim"] = bd_c(p["s5_c_re"]), bd_c(p["s5_c_im"])
    wl["s5_d"] = p["s5_d"][l].reshape(1, -1)
    wl["s5_glu_w"] = p["s5_glu_w"][l].astype(BF16)
    wl["s5_glu_b"] = p["s5_glu_b"][l].reshape(1, -1)
    gw = p["gla_gw"][l]
    dk = GLA_HEADS * GLA_DK
    wl["gla_gw"] = jnp.stack([_pad_to(jnp.pad(gw[i], ((GLA_RANK * i, 0), (0, 0))), (dk, dk)) for i in range(2)]).astype(BF16)
    wl["gla_gb"] = p["gla_gb"][l].reshape(2, 1, dk)
    wl["gla_norm"] = jnp.tile(p["gla_norm"][l], GLA_HEADS).reshape(1, -1)
    head = np.arange(D_BRANCH) // GLA_DV
    wl["head_mean"] = jnp.asarray((head[:, None] == head[None, :]) / GLA_DV, BF16)
    return wl


def _hyena_filters(wl, tabs):
    feat, win, fo, fb, _ = tabs
    kern_lin, nrm = _hy_mlp(feat, wl, win)
    return _hy_kspec(kern_lin, nrm, fo, fb)


def _trunk_layer(x, mod, mod_row, wl, batch, seq_len, hy_tabs, rope_tabs=None, ctx=None):
    n = batch * seq_len
    tm = 512
    mla_in, gates, hy_in, s5_in, gla_in = _inproj(x, mod, mod_row, wl["norm_w"], wl["w_in"], tm)

    q, k, v, ckv, krope = _mla_prep(mla_in, wl, rope_tabs, seq_len, tm, True)
    kv_parts = [(k, v, seq_len)]
    if ctx is not None:
        k_ctx, v_ctx = _mla_prep(ctx["mla"], wl, None, ctx["past"], 512, False)
        kv_parts = [(k_ctx, v_ctx, ctx["past"])] + kv_parts
    o_mla = _attention(q, kv_parts, batch, seq_len, 256)

    o_hy = _hyena(hy_in, wl, _hyena_filters(wl, hy_tabs), hy_tabs[2], hy_tabs[4], batch, seq_len)

    nseg = seq_len // S5_SEG
    nseq = batch * nseg
    u_seg = s5_in.reshape(nseq, S5_SEG, D_BRANCH)
    if ctx is None:
        hin = jnp.zeros((2, nseq, 2 * S5_N), F32)
    else:
        (fin,) = _s5_scan(u_seg, jnp.zeros((2, nseq, 2 * S5_N), F32), wl, False)
        hin = _s5_chain(fin, ctx["s5_h0"], wl, batch, nseg)
    y2, s5_fin = _s5_scan(u_seg, hin, wl, True)

    s0 = jnp.zeros((batch, 2, GLA_HEADS * GLA_DV, GLA_HEADS * GLA_DK), F32) if ctx is None else ctx["gla_s0"]
    o_gla, gla_fin = _gla(gla_in, wl, s0, batch, seq_len)

    y = _outproj(x, mod, mod_row, gates, o_mla, o_hy, s5_in, y2.reshape(2, n, D_BRANCH), o_gla, wl, tm)
    return y, (ckv, krope, s5_fin, gla_fin)


def kernel(x_prompt, x_sample, c, cache_mla_ckv, cache_mla_krope, state_s5, state_gla, c_ctx, norm_w, ada_w, ada_b, w_in, w_out, mla_qa_norm, mla_kva_norm, mla_w_uq, mla_w_ukv, mla_q_norm, mla_k_norm, hy_conv_w, hy_conv_b, hy_w1, hy_b1, hy_freq1, hy_w2, hy_b2, hy_freq2, hy_w3, hy_bias, s5_a_re, s5_a_im, s5_log_dt, s5_b_re, s5_b_im, s5_c_re, s5_c_im, s5_d, s5_glu_w, s5_glu_b, gla_gw, gla_gb, gla_norm):
    params = dict(norm_w=norm_w, w_in=w_in, w_out=w_out, mla_qa_norm=mla_qa_norm, mla_kva_norm=mla_kva_norm,
                  mla_w_uq=mla_w_uq, mla_w_ukv=mla_w_ukv, mla_q_norm=mla_q_norm, mla_k_norm=mla_k_norm,
                  hy_conv_w=hy_conv_w, hy_conv_b=hy_conv_b, hy_w1=hy_w1, hy_b1=hy_b1, hy_freq1=hy_freq1,
                  hy_w2=hy_w2, hy_b2=hy_b2, hy_freq2=hy_freq2, hy_w3=hy_w3, hy_bias=hy_bias,
                  s5_a_re=s5_a_re, s5_a_im=s5_a_im, s5_log_dt=s5_log_dt, s5_b_re=s5_b_re, s5_b_im=s5_b_im,
                  s5_c_re=s5_c_re, s5_c_im=s5_c_im, s5_d=s5_d, s5_glu_w=s5_glu_w, s5_glu_b=s5_glu_b,
                  gla_gw=gla_gw, gla_gb=gla_gb, gla_norm=gla_norm)
    bp, lp, d = x_prompt.shape
    bs, ls, _ = x_sample.shape
    past = cache_mla_ckv.shape[2]
    n_s5 = S5_N

    conds = jnp.concatenate([c_ctx[None, :], c, jnp.zeros((8 - 1 - bs, d), F32)], axis=0)
    mods = _modulation(conds, ada_w, ada_b).reshape(DEPTH, 8, 3, d)

    tabs_p = _hyena_tables(lp) + _odd_dft(lp)
    tabs_s = _hyena_tables(ls) + _odd_dft(ls)
    rope_tabs = _rope_tables(ls)
    tm_s = 512
    nseg = ls // S5_SEG

    y_p = x_prompt.reshape(bp * lp, d)
    y_s = x_sample.reshape(bs * ls, d)
    ckv_l, krope_l, s5_l, gla_l = [], [], [], []
    for l in range(DEPTH):
        wl = _layer_weights(l, params)
        y_p, (ckv, krope, s5_fin, gla_fin) = _trunk_layer(y_p, mods[l], lambda i: 0, wl, bp, lp, tabs_p)
        ckv_l.append(ckv.reshape(bp, lp, -1))
        krope_l.append(krope.reshape(bp, lp, -1))
        s5_l.append(jnp.stack([s5_fin[:, :, :n_s5], s5_fin[:, :, n_s5:]], axis=-1)
                    .reshape(2, bp, S5_GROUPS, S5_STATE, 2).transpose(1, 0, 2, 3, 4))
        gla_l.append(gla_fin)

        mla_ctx = jnp.concatenate([cache_mla_ckv[:, l], jnp.zeros((bs, past, 64), F32), cache_mla_krope[:, l],
                                   jnp.zeros((bs, past, 32), F32)], axis=-1).reshape(bs * past, 256)
        st = state_s5[:, l]
        h0 = jnp.concatenate([st[..., 0].reshape(bs, 2, n_s5), st[..., 1].reshape(bs, 2, n_s5)], axis=-1)
        h0 = h0.transpose(1, 0, 2)
        h0rows = jnp.zeros((2, nseg * bs, 2 * n_s5), F32)
        h0rows = h0rows.at[0, 0::nseg].set(h0[0]).at[1, nseg - 1::nseg].set(h0[1])
        eye_h = jnp.eye(GLA_HEADS, dtype=F32)
        gla_s0 = jnp.einsum("bdhke,hg->bdhegk", state_gla[:, l], eye_h).reshape(
            bs, 2, GLA_HEADS * GLA_DV, GLA_HEADS * GLA_DK)
        ctx = {"mla": mla_ctx, "past": past, "s5_h0": h0rows, "gla_s0": gla_s0}
        y_s, _ = _trunk_layer(y_s, mods[l], lambda i: 1 + (i * tm_s) // ls, wl, bs, ls, tabs_s, rope_tabs, ctx)

    return (y_p.reshape(bp, lp, d), y_s.reshape(bs, ls, d),
            jnp.stack(ckv_l, axis=1), jnp.stack(krope_l, axis=1), jnp.stack(s5_l, axis=1), jnp.stack(gla_l, axis=1))
```

```python
import functools
import math

import numpy as np
import jax
import jax.numpy as jnp
from jax import lax
from jax.experimental import pallas as pl
from jax.experimental.pallas import tpu as pltpu

F32 = jnp.float32
BF16 = jnp.bfloat16

D_MODEL = 1024
DEPTH = 2
GRID_W = 64
D_BRANCH = 256
EPS = 1e-6

MLA_HEADS = 4
MLA_Q_RANK = 192
MLA_KV_RANK = 128
MLA_NOPE = 64
MLA_ROPE = 32
MLA_QK = 96
MLA_V = 64
ROPE_BASE = 10000.0
HEAD_PAD = 128

HY_BANDS = 16
HY_FEAT = 33
HY_HIDDEN = 64
HY_SHIFT = 0.05
HY_FAST_DECAY = 0.3
HY_SLOW_DECAY = 1.5
HY_TARGET = 1e-2
HY_BLOCK = 512

S5_GROUP = 16
S5_GROUPS = 16
S5_STATE = 64
S5_N = S5_GROUPS * S5_STATE
S5_ROWS = 512
S5_SEG = 256

GLA_HEADS = 4
GLA_DK = 32
GLA_DV = 64
GLA_RANK = 16
GLA_TAU = 16.0
GLA_CHUNK = 64
GLA_SUPER = 256

SEG_MLA = (0, 384)
SEG_GATE = (384, 1408)
SEG_HY = (1408, 2176)
SEG_S5 = (2176, 2432)
SEG_GLA = (2432, 2944)
N_PROJ = 2944
GLA_G_LANE = 96

VMEM_LIMIT = 48 * 1024 * 1024


def _cp(*sem):
    return pltpu.CompilerParams(dimension_semantics=sem, vmem_limit_bytes=VMEM_LIMIT)


def _dot(a, b):
    return jnp.dot(a, b, preferred_element_type=F32)


def _dot_nt(a, b):
    return lax.dot_general(a, b, (((1,), (1,)), ((), ())), preferred_element_type=F32)


def _dot_tn(a, b):
    return lax.dot_general(a, b, (((0,), (0,)), ((), ())), preferred_element_type=F32)


def _split2(x):
    hi = x.astype(BF16)
    lo = (x - hi.astype(F32)).astype(BF16)
    return hi, lo


def _split3(x):
    h1 = x.astype(BF16)
    r1 = x - h1.astype(F32)
    h2 = r1.astype(BF16)
    h3 = (r1 - h2.astype(F32)).astype(BF16)
    return h1, h2, h3


def _dot3(a, b):
    a1, a2 = _split2(a)
    b1, b2 = _split2(b)
    return _dot(a1, b1) + (_dot(a1, b2) + _dot(a2, b1))


def _silu(z):
    return z / (1.0 + jnp.exp(-z))


def _mod_kernel(c_ref, w_ref, b_ref, o_ref):
    s = _silu(c_ref[...])
    o_ref[0] = _dot(s.astype(BF16), w_ref[0].astype(BF16)) + b_ref[0]


def _modulation(conds, ada_w, ada_b):
    d = D_MODEL
    return pl.pallas_call(
        _mod_kernel,
        grid=(DEPTH, 3),
        in_specs=[pl.BlockSpec((8, d), lambda l, j: (0, 0)),
                  pl.BlockSpec((1, d, d), lambda l, j: (l, 0, j)),
                  pl.BlockSpec((1, 1, d), lambda l, j: (l, 0, j))],
        out_specs=pl.BlockSpec((1, 8, d), lambda l, j: (l, 0, j)),
        out_shape=jax.ShapeDtypeStruct((DEPTH, 8, 3 * d), F32),
        compiler_params=_cp("arbitrary", "arbitrary"),
        name="modulation",
    )(conds, ada_w, ada_b.reshape(DEPTH, 1, 3 * d))


def _inproj_kernel(x_ref, mod_ref, nw_ref, w_ref, o_mla, o_g, o_hy, o_s5, o_gla):
    x = x_ref[...]
    ms = jnp.mean(x * x, axis=-1, keepdims=True)
    y = x * lax.rsqrt(ms + EPS) * nw_ref[...]
    h = (y * (1.0 + mod_ref[0, 1:2, :]) + mod_ref[0, 0:1, :]).astype(BF16)
    for o, (lo, hi) in ((o_mla, SEG_MLA), (o_g, SEG_GATE), (o_hy, SEG_HY), (o_s5, SEG_S5), (o_gla, SEG_GLA)):
        o[...] = _dot(h, w_ref[:, lo:hi]).astype(o.dtype)


def _inproj(x, mod, mod_row, norm_w, w_p, tm):
    n, d = x.shape
    widths = [hi - lo for lo, hi in (SEG_MLA, SEG_GATE, SEG_HY, SEG_S5, SEG_GLA)]
    dtypes = [F32, BF16, F32, F32, F32]
    return pl.pallas_call(
        _inproj_kernel,
        grid=(n // tm,),
        in_specs=[pl.BlockSpec((tm, d), lambda i: (i, 0)),
                  pl.BlockSpec((1, 3, d), lambda i: (mod_row(i), 0, 0)),
                  pl.BlockSpec((1, d), lambda i: (0, 0)),
                  pl.BlockSpec((d, N_PROJ), lambda i: (0, 0))],
        out_specs=[pl.BlockSpec((tm, w), lambda i: (i, 0)) for w in widths],
        out_shape=[jax.ShapeDtypeStruct((n, w), t) for w, t in zip(widths, dtypes)],
        compiler_params=_cp("arbitrary"),
        name="inproj",
    )(x, mod, norm_w, w_p)


def _head_norm(xh, w):
    ms = jnp.sum(xh * xh, axis=-1, keepdims=True) * (1.0 / MLA_QK)
    return xh * lax.rsqrt(ms + EPS) * w


def _rope(xh, cos, sin_a, sin_b):
    return xh * cos + pltpu.roll(xh, HEAD_PAD - 8, 1) * sin_a + pltpu.roll(xh, 8, 1) * sin_b


def _mla_prep_kernel(has_q, rope, cache_seqs, n_aliased, *refs):
    refs = list(refs)
    m_ref = refs.pop(0)
    if has_q:
        qan_ref, wuq_ref, qn_ref, kvn_ref = refs[:4]
        refs = refs[4:]
    wuk_ref, wuv_ref, kn_ref = refs[:3]
    refs = refs[3:]
    if rope:
        cos_ref, sa_ref, sb_ref = refs[:3]
        refs = refs[3:]
        cos, sa, sb = cos_ref[...], sa_ref[...], sb_ref[...]
    refs = refs[n_aliased:]
    if has_q:
        q_ref = refs.pop(0)
    k_ref, v_ref = refs[:2]
    if cache_seqs:
        ckv_ref, kro_ref = refs[2:]
    m = m_ref[...]
    if has_q:
        lane = lax.broadcasted_iota(jnp.int32, (1, HEAD_PAD), 1)
        mixed = m[:, 128:256]
        cq = jnp.concatenate([m[:, 0:128], jnp.where(lane < MLA_NOPE, mixed, 0.0)], axis=1)
        ms = jnp.sum(cq * cq, axis=-1, keepdims=True) * (1.0 / MLA_Q_RANK)
        cqn = cq * lax.rsqrt(ms + EPS) * qan_ref[...]
        q = _dot(cqn.astype(BF16), wuq_ref[...])
        ckv = m[:, 256:384]
        ckvn = ckv * lax.rsqrt(jnp.mean(ckv * ckv, axis=-1, keepdims=True) + EPS) * kvn_ref[...]
        kr = jnp.where(jnp.logical_and(lane >= MLA_NOPE, lane < MLA_NOPE + MLA_ROPE), mixed, 0.0)
        if cache_seqs:
            seq_len = ckv_ref.shape[2]
            for s in range(cache_seqs):
                ckv_ref[s, 0] = ckvn[s * seq_len:(s + 1) * seq_len]
                kro_ref[s, 0] = kr[s * seq_len:(s + 1) * seq_len, MLA_NOPE:MLA_NOPE + MLA_ROPE]
    else:
        ckvn = m[:, 0:128]
        kr = m[:, 128:256]
    cb = ckvn.astype(BF16)
    kup = _dot(cb, wuk_ref[...])
    v_ref[...] = _dot(cb, wuv_ref[...]).astype(BF16)
    for h in range(MLA_HEADS):
        sl = slice(HEAD_PAD * h, HEAD_PAD * (h + 1))
        kh = _head_norm(kup[:, sl] + kr, kn_ref[...])
        if rope:
            kh = _rope(kh, cos, sa, sb)
        k_ref[:, sl] = kh.astype(BF16)
        if has_q:
            qh = _head_norm(q[:, sl], qn_ref[...])
            if rope:
                qh = _rope(qh, cos, sa, sb)
            q_ref[:, sl] = (qh * (MLA_QK ** -0.5)).astype(BF16)


def _mla_prep(m, wl, rope_tabs, seq_len, tm, has_q, cache=None):
    n, wm = m.shape
    rope = rope_tabs is not None
    full = lambda shape: pl.BlockSpec(shape, lambda i: (0,) * len(shape))
    args, specs = [m], [pl.BlockSpec((tm, wm), lambda i: (i, 0))]
    if has_q:
        args += [wl["qa_norm"], wl["w_uq"], wl["q_norm"], wl["kva_norm"]]
        specs += [full((1, 256)), full((256, 512)), full((1, 128)), full((1, 128))]
    args += [wl["w_uk"], wl["w_uv"], wl["k_norm"]]
    specs += [full((128, 512)), full((128, 256)), full((1, 128))]
    if rope:
        nt = seq_len // tm
        args += list(rope_tabs)
        specs += [pl.BlockSpec((tm, HEAD_PAD), lambda i: (i % nt, 0))] * 3
    row = lambda w: pl.BlockSpec((tm, w), lambda i: (i, 0))
    out_specs = [row(512), row(256)]
    out_shape = [jax.ShapeDtypeStruct((n, 512), BF16), jax.ShapeDtypeStruct((n, 256), BF16)]
    aliases = {}
    nseq = 0
    if has_q:
        out_specs = [row(512)] + out_specs
        out_shape = [jax.ShapeDtypeStruct((n, 512), BF16)] + out_shape
    if cache is not None:
        layer, prev = cache
        nseq = tm // seq_len
        for w in (MLA_KV_RANK, MLA_ROPE):
            out_specs.append(pl.BlockSpec((nseq, 1, seq_len, w), lambda i: (i, layer, 0, 0)))
            out_shape.append(jax.ShapeDtypeStruct((n // seq_len, DEPTH, seq_len, w), F32))
        if prev is not None:
            for k, buf in enumerate(prev):
                aliases[len(args)] = len(out_shape) - 2 + k
                args.append(buf)
                specs.append(pl.BlockSpec(memory_space=pl.ANY))
    return pl.pallas_call(
        functools.partial(_mla_prep_kernel, has_q, rope, nseq, len(aliases)),
        grid=(n // tm,),
        in_specs=specs, out_specs=out_specs, out_shape=out_shape,
        input_output_aliases=aliases,
        compiler_params=_cp("arbitrary"),
        name="mla_prep",
    )(*args)


def _attn_kernel(nparts, q_ref, *refs):
    kv = [(refs[2 * i], refs[2 * i + 1]) for i in range(nparts)]
    o_ref = refs[2 * nparts]
    low = lax.broadcasted_iota(jnp.int32, (1, HEAD_PAD), 1) < MLA_V

    def scores(h):
        sl = slice(HEAD_PAD * h, HEAD_PAD * (h + 1))
        return [_dot_nt(q_ref[:, sl], k_ref[:, sl]) for k_ref, _ in kv]

    s_next = scores(0)
    acc = None
    for h in range(MLA_HEADS):
        pair, j = divmod(h, 2)
        s = s_next
        if h + 1 < MLA_HEADS:
            s_next = scores(h + 1)
        if j == 0:
            v_half = []
            for _, v_ref in kv:
                vp = v_ref[:, HEAD_PAD * pair:HEAD_PAD * (pair + 1)]
                zero = jnp.zeros_like(vp)
                v_half.append((jnp.where(low, vp, zero), jnp.where(low, zero, vp)))
        m = functools.reduce(jnp.maximum, [jnp.max(x, axis=-1, keepdims=True) for x in s])
        p = [jnp.exp(x - m) for x in s]
        den = functools.reduce(jnp.add, [jnp.sum(x, axis=-1, keepdims=True) for x in p])
        num = functools.reduce(jnp.add, [_dot(x.astype(BF16), vh[j]) for x, vh in zip(p, v_half)])
        o = num / den
        acc = o if j == 0 else acc + o
        if j == 1:
            o_ref[:, HEAD_PAD * pair:HEAD_PAD * (pair + 1)] = acc.astype(BF16)


def _attention(q, kv_parts, batch, lq, tq):
    nq = lq // tq
    args, specs = [q], [pl.BlockSpec((tq, 512), lambda b, i: (b * nq + i, 0))]
    for k, v, lk in kv_parts:
        args += [k, v]
        specs += [pl.BlockSpec((lk, 512), lambda b, i: (b, 0)), pl.BlockSpec((lk, 256), lambda b, i: (b, 0))]
    return pl.pallas_call(
        functools.partial(_attn_kernel, len(kv_parts)),
        grid=(batch, nq),
        in_specs=specs,
        out_specs=pl.BlockSpec((tq, 256), lambda b, i: (b * nq + i, 0)),
        out_shape=jax.ShapeDtypeStruct((batch * lq, 256), BF16),
        compiler_params=_cp("arbitrary", "arbitrary"),
        name="attention",
    )(*args)


def _hyena_kernel(seq_len, bk, nseq, x_ref, cw_ref, cb_ref, fo_ref, go_ref, k_ref, bias_ref, o_ref,
                  u_sc, y_sc, z_sc):
    c = D_BRANCH
    n = nseq * seq_len
    nblk = seq_len // bk
    pos = jnp.bitwise_and(lax.broadcasted_iota(jnp.int32, (n, 1), 0), seq_len - 1)
    first, last = pos == 0, pos == seq_len - 1

    def short_conv(g):
        cols = slice(g * c, (g + 1) * c)
        x = x_ref[:, cols]
        xm = jnp.where(first, 0.0, pltpu.roll(x, 1, 0))
        xp = jnp.where(last, 0.0, pltpu.roll(x, n - 1, 0))
        return cw_ref[0:1, cols] * xm + cw_ref[1:2, cols] * x + cw_ref[2:3, cols] * xp + cb_ref[:, cols]

    fo, go = fo_ref[...], go_ref[...]

    def long_conv(v, order, emit):
        cols = slice(order * c, (order + 1) * c)
        for j in range(nblk):
            u_sc[j] = _dot(fo, v[j * bk:(j + 1) * bk].astype(BF16))
        rc = 64
        for i in range(nblk):
            def mix(r, carry, i=i):
                top = pl.ds(pl.multiple_of(r * rc, rc), rc)
                bot = pl.ds(pl.multiple_of(bk + r * rc, rc), rc)
                at = ab = None
                for j in range(nblk):
                    q = i - j + nblk - 1
                    kt, kb = k_ref[q, top, cols], k_ref[q, bot, cols]
                    ut, ub = u_sc[j, top, :], u_sc[j, bot, :]
                    pt, pb = ut * kt - ub * kb, ut * kb + ub * kt
                    at, ab = (pt, pb) if at is None else (at + pt, ab + pb)
                z_sc[top, :] = at.astype(BF16)
                z_sc[bot, :] = ab.astype(BF16)
                return carry

            lax.fori_loop(0, bk // rc, mix, 0, unroll=True if nblk == 1 else 2)
            emit(i, _dot(go, z_sc[...]))

    v_all, x1_all, x2_all = short_conv(0), short_conv(1), short_conv(2)
    for s in range(nseq):
        base = s * seq_len
        v = v_all[base:base + seq_len]

        def emit1(i, y, base=base, v=v):
            r = slice(i * bk, (i + 1) * bk)
            y_sc[r, :] = x1_all[base + i * bk:base + (i + 1) * bk] * (y + bias_ref[0:1, :] * v[r])

        long_conv(v, 0, emit1)
        y1 = y_sc[...]

        def emit2(i, y, base=base, y1=y1):
            r = slice(i * bk, (i + 1) * bk)
            o_ref[base + i * bk:base + (i + 1) * bk, :] = (
                x2_all[base + i * bk:base + (i + 1) * bk] * (y + bias_ref[1:2, :] * y1[r])).astype(BF16)

        long_conv(y1, 1, emit2)


def _hyena(x, wl, kspec, fo, go, batch, seq_len):
    c = D_BRANCH
    bk = fo.shape[1]
    nseq = max(1, min(batch, 1024 // seq_len))
    rows = nseq * seq_len
    nblk = seq_len // bk
    full = lambda a: pl.BlockSpec(a.shape, lambda i: (0,) * a.ndim)
    return pl.pallas_call(
        functools.partial(_hyena_kernel, seq_len, bk, nseq),
        grid=(batch // nseq,),
        in_specs=[pl.BlockSpec((rows, 3 * c), lambda i: (i, 0)),
                  full(wl["hy_conv_w"]), full(wl["hy_conv_b"]), full(fo), full(go),
                  pl.BlockSpec(kspec.shape, lambda i: (0, 0, 0), pipeline_mode=pl.Buffered(1)),
                  full(wl["hy_bias"])],
        out_specs=pl.BlockSpec((rows, c), lambda i: (i, 0)),
        out_shape=jax.ShapeDtypeStruct((batch * seq_len, c), BF16),
        scratch_shapes=[pltpu.VMEM((nblk, 2 * bk, c), F32), pltpu.VMEM((seq_len, c), F32),
                        pltpu.VMEM((2 * bk, c), BF16)],
        compiler_params=pltpu.CompilerParams(dimension_semantics=("arbitrary",), vmem_limit_bytes=56 * 1024 * 1024),
        name="hyena",
    )(x, wl["hy_conv_w"], wl["hy_conv_b"], fo, go, kspec, wl["hy_bias"])


def _hy_mlp_kernel(feat_ref, w1_ref, b1_ref, f1_ref, w2_ref, b2_ref, f2_ref, w3_ref, win_ref,
                   kern_ref, nrm_ref):
    i = pl.program_id(0)
    tl = feat_ref.shape[0]
    h = jnp.sin(f1_ref[...] * (_dot3(feat_ref[...], w1_ref[...]) + b1_ref[...]))
    h = jnp.sin(f2_ref[...] * (_dot3(h, w2_ref[...]) + b2_ref[...]))
    filt = _dot3(h, w3_ref[...])
    win = win_ref[...]
    row0 = (lax.broadcasted_iota(jnp.int32, (tl, 1), 0) + i * tl) == 0
    filt = jnp.where(row0, 0.0, filt * jnp.concatenate([win, win], axis=1))
    kern_ref[...] = filt
    part = jnp.sum(jnp.abs(filt), axis=0, keepdims=True)

    @pl.when(i == 0)
    def _():
        nrm_ref[...] = jnp.zeros_like(nrm_ref)
    nrm_ref[...] += jnp.broadcast_to(part, nrm_ref.shape)


def _hy_mlp(feat, wl, win):
    rows = feat.shape[0]
    tl = 256
    nl = rows // (2 * tl)
    full = lambda shape: pl.BlockSpec(shape, lambda i: (0,) * len(shape))
    return pl.pallas_call(
        _hy_mlp_kernel,
        grid=(rows // tl,),
        in_specs=[pl.BlockSpec((tl, 128), lambda i: (i, 0)),
                  full((128, 128)), full((1, 128)), full((1, 128)),
                  full((128, 128)), full((1, 128)), full((1, 128)),
                  pl.BlockSpec((128, 512), lambda i: (0, jnp.where(i < nl, 1, 0))),
                  pl.BlockSpec((tl, 256), lambda i: (i, 0))],
        out_specs=[pl.BlockSpec((tl, 512), lambda i: (i, 0)), full((8, 512))],
        out_shape=[jax.ShapeDtypeStruct((rows, 512), F32), jax.ShapeDtypeStruct((8, 512), F32)],
        compiler_params=_cp("arbitrary"),
        name="hy_mlp",
    )(feat, wl["hy_w1"], wl["hy_b1"], wl["hy_f1"], wl["hy_w2"], wl["hy_b2"], wl["hy_f2"], wl["hy_w3"], win)


def _hy_kspec_kernel(lo_ref, hi_ref, fo_ref, fb_ref, n_ref, o_ref):
    bk = lo_ref.shape[0]
    k = _dot(fo_ref[...], hi_ref[...].astype(BF16)) + _dot(fb_ref[...], lo_ref[...].astype(BF16))
    o_ref[0] = k * ((1.0 / bk) / n_ref[0:1, :])


def _hy_kspec(kern_lin, nrm, fo, fb):
    n2, bk = fo.shape
    nq = kern_lin.shape[0] // bk - 1
    full = lambda a: pl.BlockSpec(a.shape, lambda q: (0,) * a.ndim)
    return pl.pallas_call(
        _hy_kspec_kernel,
        grid=(nq,),
        in_specs=[pl.BlockSpec((bk, 512), lambda q: (q, 0)),
                  pl.BlockSpec((bk, 512), lambda q: (q + 1, 0)),
                  full(fo), full(fb), full(nrm)],
        out_specs=pl.BlockSpec((1, n2, 512), lambda q: (q, 0, 0)),
        out_shape=jax.ShapeDtypeStruct((nq, n2, 512), F32),
        compiler_params=_cp("arbitrary"),
        name="hy_kspec",
    )(kern_lin, kern_lin, fo, fb, nrm)


def _s5_discretise(are_ref, aim_ref, ldt_ref):
    ar = jnp.minimum(are_ref[0], -1e-4)
    ai = aim_ref[0]
    dt = jnp.exp(ldt_ref[0])
    e = jnp.exp(ar * dt)
    return ar, ai, e * jnp.cos(ai * dt), e * jnp.sin(ai * dt)


def _s5_scan_kernel(nseq, emit_y, u_ref, hin_ref, are_ref, aim_ref, ldt_ref, bre_ref, bim_ref, *rest):
    if emit_y:
        cre_ref, cim_ref, y_ref, hfin_ref, wb_sc, ab_sc, s_sc, hc_sc, perm_sc, wc_sc = rest
    else:
        hfin_ref, wb_sc, ab_sc, s_sc, hc_sc, perm_sc = rest
    d = pl.program_id(0)
    c = pl.program_id(1)
    n = S5_N

    @pl.when(c == 0)
    def _():
        ar, ai, abr, abi = _s5_discretise(are_ref, aim_ref, ldt_ref)
        ab_sc[0:1, :] = abr
        ab_sc[1:2, :] = abi
        den = 1.0 / (ar * ar + ai * ai)
        cr = ((abr - 1.0) * ar + abi * ai) * den
        ci = (abi * ar - (abr - 1.0) * ai) * den
        bre, bim = bre_ref[0], bim_ref[0]
        wb_sc[:, 0:n] = (cr * bre - ci * bim).astype(BF16)
        wb_sc[:, n:2 * n] = (cr * bim + ci * bre).astype(BF16)
        if emit_y:
            wc_sc[0:n, :] = cre_ref[0].astype(BF16)
            wc_sc[n:2 * n, :] = (-cim_ref[0]).astype(BF16)
        hc_sc[...] = hin_ref[0]

    steps = u_ref.shape[1]
    rows_c = nseq * steps

    @pl.when(c == 0)
    def _():
        i = lax.broadcasted_iota(jnp.int32, (rows_c, rows_c), 0)
        j = lax.broadcasted_iota(jnp.int32, (rows_c, rows_c), 1)
        p = lax.shift_right_logical(i, int(math.log2(nseq)))
        step = p + d * (steps - 1 - 2 * p)
        src = jnp.bitwise_and(i, nseq - 1) * steps + step
        perm_sc[...] = jnp.where(j == src, 1.0, 0.0).astype(BF16)

    lhs = _dot(perm_sc[...], u_ref[...].reshape(rows_c, D_BRANCH).astype(BF16)).astype(BF16)
    lb = 256
    y = None
    for j in range(n // lb):
        lr = slice(lb * j, lb * (j + 1))
        li = slice(n + lb * j, n + lb * (j + 1))
        bur = _dot(lhs, wb_sc[:, lr])
        bui = _dot(lhs, wb_sc[:, li])
        abr = ab_sc[0:1, lr]
        abi = ab_sc[1:2, lr]
        hr, hi = hc_sc[:, lr], hc_sc[:, li]
        for p in range(steps):
            rows = slice(p * nseq, (p + 1) * nseq)
            hr, hi = abr * hr - abi * hi + bur[rows], abr * hi + abi * hr + bui[rows]
            if emit_y:
                s_sc[rows, lr] = hr.astype(BF16)
                s_sc[rows, li] = hi.astype(BF16)
        hc_sc[:, lr] = hr
        hc_sc[:, li] = hi
        if emit_y:
            yj = _dot(s_sc[:, lr], wc_sc[lr, :]) + _dot(s_sc[:, li], wc_sc[li, :])
            y = yj if y is None else y + yj

    if emit_y:
        @pl.when(d == 0)
        def _():
            for p in range(steps):
                y_ref[0, :, p, :] = y[p * nseq:(p + 1) * nseq]

        @pl.when(d == 1)
        def _():
            for p in range(steps):
                y_ref[0, :, steps - 1 - p, :] = y[p * nseq:(p + 1) * nseq]

    @pl.when(c == pl.num_programs(1) - 1)
    def _():
        hfin_ref[0] = hc_sc[...]


def _s5_scan(u, hin, wl, emit_y):
    nseq, nstep, _ = u.shape
    steps = S5_ROWS // nseq
    nc = nstep // steps
    n = S5_N
    chunk = lambda d, c: c + d * (nc - 1 - 2 * c)
    per_dir = lambda shape: pl.BlockSpec((1,) + shape, lambda d, c: (d,) + (0,) * len(shape))
    args = [u, hin, wl["s5_are"], wl["s5_aim"], wl["s5_ldt"], wl["s5_bre"], wl["s5_bim"]]
    specs = [pl.BlockSpec((nseq, steps, D_BRANCH), lambda d, c: (0, chunk(d, c), 0)),
             per_dir((nseq, 2 * n)), per_dir((1, n)), per_dir((1, n)), per_dir((1, n)),
             per_dir((D_BRANCH, n)), per_dir((D_BRANCH, n))]
    out_specs = [per_dir((nseq, 2 * n))]
    out_shape = [jax.ShapeDtypeStruct((2, nseq, 2 * n), F32)]
    scratch = [pltpu.VMEM((D_BRANCH, 2 * n), BF16), pltpu.VMEM((8, n), F32),
               pltpu.VMEM((S5_ROWS, 2 * n), BF16), pltpu.VMEM((nseq, 2 * n), F32),
               pltpu.VMEM((S5_ROWS, S5_ROWS), BF16)]
    if emit_y:
        args += [wl["s5_cre"], wl["s5_cim"]]
        specs += [per_dir((n, D_BRANCH)), per_dir((n, D_BRANCH))]
        out_specs = [pl.BlockSpec((1, nseq, steps, D_BRANCH), lambda d, c: (d, 0, chunk(d, c), 0))] + out_specs
        out_shape = [jax.ShapeDtypeStruct((2, nseq, nstep, D_BRANCH), F32)] + out_shape
        scratch += [pltpu.VMEM((2 * n, D_BRANCH), BF16)]
    return pl.pallas_call(
        functools.partial(_s5_scan_kernel, nseq, emit_y),
        grid=(2, nc),
        in_specs=specs, out_specs=out_specs, out_shape=out_shape, scratch_shapes=scratch,
        compiler_params=_cp("arbitrary", "arbitrary"),
        name="s5_scan" if emit_y else "s5_scan_finals",
    )(*args)


def _s5_chain_kernel(batch, nseg, f_ref, h0_ref, are_ref, aim_ref, ldt_ref, o_ref):
    d = pl.program_id(0)
    n = S5_N
    _, _, pr, pi = _s5_discretise(are_ref, aim_ref, ldt_ref)
    for _ in range(int(math.log2(S5_SEG))):
        pr, pi = pr * pr - pi * pi, 2.0 * pr * pi
    f = f_ref[0]
    fr, fi = f[:, 0:n], f[:, n:2 * n]
    h0 = h0_ref[0]
    h0r, h0i = h0[:, 0:n], h0[:, n:2 * n]
    nrow = batch * nseg
    seg = jnp.bitwise_and(lax.broadcasted_iota(jnp.int32, (nrow, 1), 0), nseg - 1)

    def run(shift, keep):
        xr, xi = h0r, h0i
        for _ in range(nseg - 1):
            zr = fr + pr * xr - pi * xi
            zi = fi + pr * xi + pi * xr
            xr = h0r + jnp.where(keep, pltpu.roll(zr, shift, 0), 0.0)
            xi = h0i + jnp.where(keep, pltpu.roll(zi, shift, 0), 0.0)
        o_ref[0, :, 0:n] = xr
        o_ref[0, :, n:2 * n] = xi

    @pl.when(d == 0)
    def _():
        run(1, seg != 0)

    @pl.when(d == 1)
    def _():
        run(nrow - 1, seg != nseg - 1)


def _s5_chain(fin, h0rows, wl, batch, nseg):
    nrow = batch * nseg
    n = S5_N
    per_dir = lambda shape: pl.BlockSpec((1,) + shape, lambda d: (d,) + (0,) * len(shape))
    return pl.pallas_call(
        functools.partial(_s5_chain_kernel, batch, nseg),
        grid=(2,),
        in_specs=[per_dir((nrow, 2 * n)), per_dir((nrow, 2 * n)), per_dir((1, n)), per_dir((1, n)), per_dir((1, n))],
        out_specs=per_dir((nrow, 2 * n)),
        out_shape=jax.ShapeDtypeStruct((2, nrow, 2 * n), F32),
        compiler_params=_cp("arbitrary"),
        name="s5_chain",
    )(fin, h0rows, wl["s5_are"], wl["s5_aim"], wl["s5_ldt"])


def _gla_kernel(seq_len, nb, n_aliased, q_ref, k_ref, v_ref, g_ref, gw_ref, gb_ref, s0_ref, *rest):
    o_ref, sfin_ref, qe_sc, upd_sc, dec_sc, sall_sc, lhs_sc, kt_sc, la_sc, oi_sc = rest[n_aliased:]
    d = pl.program_id(1)
    sign = 1 - 2 * d
    ck, sup = GLA_CHUNK, GLA_SUPER
    cps = sup // ck
    nsup, nchunk = seq_len // sup, seq_len // ck
    dk, dv = GLA_HEADS * GLA_DK, GLA_HEADS * GLA_DV
    r = lax.broadcasted_iota(jnp.int32, (sup, sup), 0)
    s = lax.broadcasted_iota(jnp.int32, (sup, sup), 1)
    same = lax.shift_right_logical(r, 6) == lax.shift_right_logical(s, 6)
    tri = jnp.logical_and(same, (s - r) * sign <= 0)
    cum_lhs = jnp.where(tri, 1.0, 0.0).astype(BF16)
    pos = jnp.bitwise_and(lax.broadcasted_iota(jnp.int32, (ck, 1), 0), ck - 1)
    is_last = pos == (ck - 1) * (1 - d)
    row_chunk = lax.shift_right_logical(lax.broadcasted_iota(jnp.int32, (sup, 1), 0), 6)
    head_k = lax.shift_right_logical(lax.broadcasted_iota(jnp.int32, (1, dk), 1), 5)
    head_v = lax.shift_right_logical(lax.broadcasted_iota(jnp.int32, (1, dv), 1), 6)
    blockdiag = lax.shift_right_logical(lax.broadcasted_iota(jnp.int32, (dv, 1), 0), 6) == head_k

    def stage_a(j, i, slot):
        rows = pl.ds(pl.multiple_of(j * seq_len + i * sup, sup), sup)
        q = q_ref[rows, :] * (GLA_DK ** -0.5)
        k = k_ref[rows, :]
        v = v_ref[rows, :]
        cs = _dot(cum_lhs, la_sc[rows, :])
        yield
        bc = cs[:, 0:dk] + cs[:, dk:2 * dk]
        tots = [jnp.sum(jnp.where(is_last, bc[c * ck:(c + 1) * ck], 0.0), axis=0, keepdims=True)
                for c in range(cps)]
        tot = jnp.concatenate([jnp.broadcast_to(t, (ck, dk)) for t in tots], axis=0)
        ref = 0.5 * tot
        qt = q * jnp.exp(bc - ref)
        kt_sc[slot] = (k * jnp.exp(ref - bc)).astype(BF16)
        lhs_sc[slot] = jnp.concatenate(
            [jnp.where(head_k == h, qt, 0.0) for h in range(GLA_HEADS)], axis=0).astype(BF16)
        qe_sc[j, pl.ds(pl.multiple_of(i * sup, sup), sup), :] = (q * jnp.exp(bc)).astype(BF16)
        kl = (k * jnp.exp(tot - bc)).astype(BF16)
        zero = jnp.zeros_like(kl)
        klx = jnp.concatenate([jnp.where(row_chunk == c, kl, zero) for c in range(cps)], axis=1)
        upd = _dot_tn(v.astype(BF16), klx)
        yield
        for c in range(cps):
            upd_sc[j, i * cps + c] = jnp.where(blockdiag, upd[:, c * dk:(c + 1) * dk], 0.0)
            dec_sc[j, i * cps + c] = jnp.broadcast_to(jnp.exp(tots[c]), (8, dk))

    def stage_b(j, i, slot):
        rows = pl.ds(pl.multiple_of(j * seq_len + i * sup, sup), sup)
        v = v_ref[rows, :]
        p = _dot_nt(lhs_sc[slot], kt_sc[slot])
        yield
        att = jnp.concatenate([jnp.where(tri, p[h * sup:(h + 1) * sup], 0.0).astype(BF16)
                               for h in range(GLA_HEADS)], axis=1)
        vexp = jnp.concatenate([jnp.where(head_v == h, v, 0.0).astype(BF16) for h in range(GLA_HEADS)], axis=0)
        o = _dot(att, vexp)
        yield
        oi_sc[rows, :] = o

    def run(*stages):
        live = list(stages)
        while live:
            for g in list(live):
                if next(g, StopIteration) is StopIteration:
                    live.remove(g)

    x = _dot(g_ref[...].astype(BF16), gw_ref[0]) + gb_ref[0]
    la = (jnp.minimum(x, 0.0) - jnp.log(1.0 + jnp.exp(-jnp.abs(x)))) * (1.0 / GLA_TAU)
    la_sc[...] = jnp.concatenate(_split2(la), axis=1)

    if nb * nsup <= 4:
        units = [(j, i) for j in range(nb) for i in range(nsup)]
        run(stage_a(*units[0], 0))
        for n, (j, i) in enumerate(units):
            if n + 1 < len(units):
                run(stage_b(j, i, n % 2), stage_a(*units[n + 1], (n + 1) % 2))
            else:
                run(stage_b(j, i, n % 2))
    else:
        for j in range(nb):
            run(stage_a(j, 0, 0))

            def sup_body(i, carry, j=j):
                nxt = jnp.minimum(i + 1, nsup - 1)
                run(stage_b(j, i, jnp.bitwise_and(i, 1)), stage_a(j, nxt, jnp.bitwise_and(i + 1, 1)))
                return carry

            lax.fori_loop(0, nsup, sup_body, 0)

    for j in range(nb):
        base = j * seq_len

        def state_body(c, st, j=j):
            ci = c + d * (nchunk - 1 - 2 * c)
            sall_sc[j, ci] = st.astype(BF16)
            return dec_sc[j, ci][0:1, :] * st + upd_sc[j, ci]

        st_fin = jnp.transpose(lax.fori_loop(0, nchunk, state_body, s0_ref[j, 0]))
        for h in range(GLA_HEADS):
            sfin_ref[j, 0, 0, h] = st_fin[h * GLA_DK:(h + 1) * GLA_DK, h * GLA_DV:(h + 1) * GLA_DV]

        def inter_body(i, carry, j=j, base=base):
            rows = pl.ds(pl.multiple_of(base + i * sup, sup), sup)
            qe = qe_sc[j, pl.ds(pl.multiple_of(i * sup, sup), sup), :]
            zero = jnp.zeros_like(qe)
            lhs = jnp.concatenate([jnp.where(row_chunk == c, qe, zero) for c in range(cps)], axis=1)
            st = jnp.concatenate([sall_sc[j, i * cps + c] for c in range(cps)], axis=1)
            o_ref[0, rows, :] = (oi_sc[rows, :] + _dot_nt(lhs, st)).astype(BF16)
            return carry

        lax.fori_loop(0, nsup, inter_body, 0, unroll=2 if nsup % 2 == 0 else 1)


def _gla(gla_in, mla_in, wl, s0t, batch, seq_len, fin=(0, None, 1)):
    layer, prev_fin, fin_layers = fin
    aliases = {} if prev_fin is None else {7: 1}
    extra = [] if prev_fin is None else [prev_fin]
    n = gla_in.shape[0]
    dk, dv = GLA_HEADS * GLA_DK, GLA_HEADS * GLA_DV
    nb = max(1, min(batch, 1024 // seq_len))
    rows = nb * seq_len
    nchunk = seq_len // GLA_CHUNK
    return pl.pallas_call(
        functools.partial(_gla_kernel, seq_len, nb, len(extra)),
        grid=(batch // nb, 2),
        input_output_aliases=aliases,
        in_specs=[pl.BlockSpec((rows, dk), lambda b, d: (b, 0)),
                  pl.BlockSpec((rows, dk), lambda b, d: (b, 1)),
                  pl.BlockSpec((rows, dv), lambda b, d: (b, 1)),
                  pl.BlockSpec((rows, dk), lambda b, d: (b, 1)),
                  pl.BlockSpec((1, dk, dk), lambda b, d: (d, 0, 0)),
                  pl.BlockSpec((1, 1, dk), lambda b, d: (d, 0, 0)),
                  pl.BlockSpec((nb, 1, dv, dk), lambda b, d: (b, d, 0, 0))]
                 + [pl.BlockSpec(memory_space=pl.ANY)] * len(extra),
        out_specs=[pl.BlockSpec((1, rows, dv), lambda b, d: (d, b, 0)),
                   pl.BlockSpec((nb, 1, 1, GLA_HEADS, GLA_DK, GLA_DV), lambda b, d: (b, layer, d, 0, 0, 0))],
        out_shape=[jax.ShapeDtypeStruct((2, n, dv), BF16),
                   jax.ShapeDtypeStruct((batch, fin_layers, 2, GLA_HEADS, GLA_DK, GLA_DV), F32)],
        scratch_shapes=[pltpu.VMEM((nb, seq_len, dk), BF16),
                        pltpu.VMEM((nb, nchunk, dv, dk), F32),
                        pltpu.VMEM((nb, nchunk, 8, dk), F32),
                        pltpu.VMEM((nb, nchunk, dv, dk), BF16),
                        pltpu.VMEM((2, GLA_HEADS * GLA_SUPER, dk), BF16),
                        pltpu.VMEM((2, GLA_SUPER, dk), BF16),
                        pltpu.VMEM((rows, 2 * dk), BF16),
                        pltpu.VMEM((rows, dv), F32)],
        compiler_params=_cp("arbitrary", "arbitrary"),
        name="gla",
    )(gla_in, gla_in, gla_in, mla_in, wl["gla_gw"], wl["gla_gb"], s0t, *extra)


def _outproj_kernel(x_ref, mod_ref, g_ref, om_ref, oh_ref, su_ref, sf_ref, sb_ref, sd_ref, sw_ref, sbias_ref,
                    gf_ref, gb_ref, gn_ref, hm_ref, w_ref, y_ref):
    c = D_BRANCH
    g = g_ref[...].astype(F32)
    acc = _dot((om_ref[...].astype(F32) * _silu(g[:, 0:c])).astype(BF16), w_ref[0:c, :])
    acc += _dot((oh_ref[...].astype(F32) * _silu(g[:, c:2 * c])).astype(BF16), w_ref[c:2 * c, :])
    ys = sd_ref[...] * su_ref[...] + sf_ref[0] + sb_ref[0]
    ge = 0.5 * ys * (1.0 + jnp.tanh(math.sqrt(2.0 / math.pi) * (ys + 0.044715 * (ys * ys * ys))))
    o_s5 = ge / (1.0 + jnp.exp(-(_dot(ge.astype(BF16), sw_ref[...]) + sbias_ref[...])))
    acc += _dot((o_s5 * _silu(g[:, 2 * c:3 * c])).astype(BF16), w_ref[2 * c:3 * c, :])
    og = gf_ref[0].astype(F32) + gb_ref[0].astype(F32)
    hi, lo = _split2(og * og)
    ms = _dot(hi, hm_ref[...]) + _dot(lo, hm_ref[...])
    ogn = og * lax.rsqrt(ms + EPS) * gn_ref[...]
    acc += _dot((ogn * _silu(g[:, 3 * c:4 * c])).astype(BF16), w_ref[3 * c:4 * c, :])
    y_ref[...] = x_ref[...] + mod_ref[0, 2:3, :] * acc


def _outproj(x, mod, mod_row, gates, o_mla, o_hy, s5_u, s5_y, o_gla, wl, tm):
    n, d = x.shape
    c = D_BRANCH
    row = lambda w: pl.BlockSpec((tm, w), lambda i: (i, 0))
    per_dir = lambda k: pl.BlockSpec((1, tm, c), lambda i: (k, i, 0))
    full = lambda *shape: pl.BlockSpec(shape, lambda i: (0,) * len(shape))
    return pl.pallas_call(
        _outproj_kernel,
        grid=(n // tm,),
        in_specs=[row(d),
                  pl.BlockSpec((1, 3, d), lambda i: (mod_row(i), 0, 0)),
                  row(d), row(c), row(c),
                  row(c), per_dir(0), per_dir(1), full(1, c), full(c, c), full(1, c),
                  per_dir(0), per_dir(1), full(1, c), full(c, c), full(d, d)],
        out_specs=row(d),
        out_shape=jax.ShapeDtypeStruct((n, d), F32),
        compiler_params=_cp("arbitrary"),
        name="outproj",
    )(x, mod, gates, o_mla, o_hy, s5_u, s5_y, s5_y, wl["s5_d"], wl["s5_glu_w"], wl["s5_glu_b"],
      o_gla, o_gla, wl["gla_norm"], wl["head_mean"], wl["w_out"])


def _rope_tables(seq_len):
    pos = np.arange(seq_len)
    inv = ROPE_BASE ** (-np.arange(0, 16, 2, dtype=np.float64) / 16.0)
    cos = np.ones((seq_len, HEAD_PAD))
    sin_a = np.zeros((seq_len, HEAD_PAD))
    sin_b = np.zeros((seq_len, HEAD_PAD))
    for base, p in ((MLA_NOPE, pos // GRID_W), (MLA_NOPE + 16, pos % GRID_W)):
        ang = p[:, None].astype(np.float64) * inv[None, :]
        cos[:, base:base + 8] = np.cos(ang)
        cos[:, base + 8:base + 16] = np.cos(ang)
        sin_a[:, base:base + 8] = -np.sin(ang)
        sin_b[:, base + 8:base + 16] = np.sin(ang)
    return tuple(jnp.asarray(t, F32) for t in (cos, sin_a, sin_b))


def _odd_dft(seq_len):
    bk = min(seq_len, HY_BLOCK)
    k = np.arange(bk)[:, None]
    t = np.arange(bk)[None, :]

    def mat(shift):
        ang = (np.pi / (2 * bk)) * (((2 * k + 1) * (t + shift)) % (4 * bk))
        return np.concatenate([np.cos(ang), -np.sin(ang)], axis=0)

    fo = mat(0)
    fb = -mat(bk)
    fb[:, 0] = 0.0
    const = lambda a: jnp.asarray(a, F32).astype(BF16)
    return const(fo), const(fb), const(fo.T)


def _hyena_tables(seq_len):
    lag = np.arange(-seq_len, seq_len)
    pos = np.where(lag == -seq_len, 0, np.abs(lag)).astype(np.float64)
    t = pos / seq_len
    w = 2.0 * np.pi * pos / seq_len
    bands = np.linspace(1e-4, HY_BANDS - 1, HY_BANDS)
    feat = np.zeros((2 * seq_len, 128))
    feat[:, 0] = t
    feat[:, 1:1 + HY_BANDS] = np.cos(w[:, None] * bands)
    feat[:, 1 + HY_BANDS:HY_FEAT] = np.sin(w[:, None] * bands)
    deltas = np.linspace(math.log(1.0 / HY_TARGET) / HY_FAST_DECAY, math.log(1.0 / HY_TARGET) / HY_SLOW_DECAY,
                         D_BRANCH)
    win = np.exp(-t[:, None] * deltas[None, :]) + HY_SHIFT
    return jnp.asarray(feat, F32), jnp.asarray(win, F32)


def _pad_to(a, shape):
    return jnp.pad(a, [(0, s - d) for s, d in zip(shape, a.shape)])


def _layer_weights(l, p):
    z = lambda *s: jnp.zeros(s, F32)
    w_in = p["w_in"][l]
    col = lambda lo, hi: w_in[:, lo:hi]
    d = D_MODEL
    w_p = jnp.concatenate([
        col(0, 192), col(320, 352), col(2656, 2688), col(192, 320),
        col(352, 608), col(1376, 1632), col(1888, 2144), col(2688, 2944),
        col(608, 1376), col(1632, 1888),
        col(2144, 2272), col(2272, 2400), col(2400, 2656)], axis=1).astype(BF16)
    wl = {"w_in": w_p, "norm_w": p["norm_w"][l].reshape(1, d), "w_out": p["w_out"][l].astype(BF16)}
    wl["qa_norm"] = _pad_to(p["mla_qa_norm"][l].reshape(1, -1), (1, 256))
    w_uq = _pad_to(p["mla_w_uq"][l].reshape(MLA_Q_RANK, MLA_HEADS, MLA_QK), (256, MLA_HEADS, HEAD_PAD))
    wl["w_uq"] = w_uq.reshape(256, MLA_HEADS * HEAD_PAD).astype(BF16)
    wl["q_norm"] = _pad_to(p["mla_q_norm"][l].reshape(1, -1), (1, HEAD_PAD))
    wl["k_norm"] = _pad_to(p["mla_k_norm"][l].reshape(1, -1), (1, HEAD_PAD))
    wl["kva_norm"] = p["mla_kva_norm"][l].reshape(1, -1)
    w_ukv = p["mla_w_ukv"][l].reshape(MLA_KV_RANK, MLA_HEADS, MLA_NOPE + MLA_V)
    wl["w_uk"] = _pad_to(w_ukv[:, :, :MLA_NOPE], (MLA_KV_RANK, MLA_HEADS, HEAD_PAD)).reshape(MLA_KV_RANK, -1).astype(BF16)
    wl["w_uv"] = w_ukv[:, :, MLA_NOPE:].reshape(MLA_KV_RANK, MLA_HEADS * MLA_V).astype(BF16)
    wl["hy_conv_w"] = p["hy_conv_w"][l]
    wl["hy_conv_b"] = p["hy_conv_b"][l].reshape(1, -1)
    wl["hy_w1"] = _pad_to(p["hy_w1"][l], (128, 128))
    wl["hy_b1"] = _pad_to(p["hy_b1"][l].reshape(1, -1), (1, 128))
    wl["hy_f1"] = _pad_to(p["hy_freq1"][l].reshape(1, -1), (1, 128))
    wl["hy_w2"] = _pad_to(p["hy_w2"][l], (128, 128))
    wl["hy_b2"] = _pad_to(p["hy_b2"][l].reshape(1, -1), (1, 128))
    wl["hy_f2"] = _pad_to(p["hy_freq2"][l].reshape(1, -1), (1, 128))
    wl["hy_w3"] = _pad_to(p["hy_w3"][l], (128, 1024))
    wl["hy_bias"] = p["hy_bias"][l]
    flat = lambda a: a[l].reshape(2, 1, S5_N)
    wl["s5_are"], wl["s5_aim"] = flat(p["s5_a_re"]), flat(p["s5_a_im"])
    wl["s5_ldt"] = jnp.repeat(p["s5_log_dt"][l], S5_STATE, axis=-1).reshape(2, 1, S5_N)
    same_group = jnp.asarray((np.arange(D_BRANCH) // S5_GROUP)[:, None] == (np.arange(S5_N) // S5_STATE)[None, :])
    bd_b = lambda a: jnp.where(same_group, jnp.tile(
        a[l].transpose(0, 1, 3, 2).reshape(2, D_BRANCH, S5_STATE), (1, 1, S5_GROUPS)), 0.0)
    bd_c = lambda a: jnp.where(same_group.T, jnp.tile(
        a[l].transpose(0, 1, 3, 2).reshape(2, S5_N, S5_GROUP), (1, 1, S5_GROUPS)), 0.0)
    wl["s5_bre"], wl["s5_bim"] = bd_b(p["s5_b_re"]), bd_b(p["s5_b_im"])
    wl["s5_cre"], wl["s5_cim"] = bd_c(p["s5_c_re"]), bd_c(p["s5_c_im"])
    wl["s5_d"] = p["s5_d"][l].reshape(1, -1)
    wl["s5_glu_w"] = p["s5_glu_w"][l].astype(BF16)
    wl["s5_glu_b"] = p["s5_glu_b"][l].reshape(1, -1)
    gw = p["gla_gw"][l]
    dk = GLA_HEADS * GLA_DK
    wl["gla_gw"] = jnp.stack([_pad_to(jnp.pad(gw[i], ((GLA_G_LANE + GLA_RANK * i, 0), (0, 0))), (dk, dk))
                              for i in range(2)]).astype(BF16)
    wl["gla_gb"] = p["gla_gb"][l].reshape(2, 1, dk)
    wl["gla_norm"] = jnp.tile(p["gla_norm"][l], GLA_HEADS).reshape(1, -1)
    head = np.arange(D_BRANCH) // GLA_DV
    wl["head_mean"] = jnp.asarray((head[:, None] == head[None, :]) / GLA_DV, BF16)
    return wl


def _hyena_filters(wl, tabs):
    feat, win, fo, fb, _ = tabs
    kern_lin, nrm = _hy_mlp(feat, wl, win)
    return _hy_kspec(kern_lin, nrm, fo, fb)


def _trunk_layer(x, mod, mod_row, wl, batch, seq_len, hy_tabs, rope_tabs=None, ctx=None, layer=0, cache_bufs=None):
    n = batch * seq_len
    tm = 512
    mla_in, gates, hy_in, s5_in, gla_in = _inproj(x, mod, mod_row, wl["norm_w"], wl["w_in"], tm)

    if ctx is None:
        prev = None if cache_bufs is None else cache_bufs[:2]
        q, k, v, ckv, krope = _mla_prep(mla_in, wl, rope_tabs, seq_len, tm, True, (layer, prev))
    else:
        q, k, v = _mla_prep(mla_in, wl, rope_tabs, seq_len, tm, True)
        ckv = krope = None
    kv_parts = [(k, v, seq_len)]
    if ctx is not None:
        k_ctx, v_ctx = _mla_prep(ctx["mla"], wl, None, ctx["past"], 512, False)
        kv_parts = [(k_ctx, v_ctx, ctx["past"])] + kv_parts
    o_mla = _attention(q, kv_parts, batch, seq_len, 256)

    o_hy = _hyena(hy_in, wl, _hyena_filters(wl, hy_tabs), hy_tabs[2], hy_tabs[4], batch, seq_len)

    nseg = seq_len // S5_SEG
    nseq = batch * nseg
    u_seg = s5_in.reshape(nseq, S5_SEG, D_BRANCH)
    if ctx is None:
        hin = jnp.zeros((2, nseq, 2 * S5_N), F32)
    else:
        (fin,) = _s5_scan(u_seg, jnp.zeros((2, nseq, 2 * S5_N), F32), wl, False)
        hin = _s5_chain(fin, ctx["s5_h0"], wl, batch, nseg)
    y2, s5_fin = _s5_scan(u_seg, hin, wl, True)

    s0 = jnp.zeros((batch, 2, GLA_HEADS * GLA_DV, GLA_HEADS * GLA_DK), F32) if ctx is None else ctx["gla_s0"]
    fin = (0, None, 1) if ctx is not None else (layer, None if cache_bufs is None else cache_bufs[2], DEPTH)
    o_gla, gla_fin = _gla(gla_in, mla_in, wl, s0, batch, seq_len, fin)

    y = _outproj(x, mod, mod_row, gates, o_mla, o_hy, s5_in, y2.reshape(2, n, D_BRANCH), o_gla, wl, tm)
    return y, (ckv, krope, s5_fin, gla_fin)


def kernel(x_prompt, x_sample, c, cache_mla_ckv, cache_mla_krope, state_s5, state_gla, c_ctx, norm_w, ada_w, ada_b, w_in, w_out, mla_qa_norm, mla_kva_norm, mla_w_uq, mla_w_ukv, mla_q_norm, mla_k_norm, hy_conv_w, hy_conv_b, hy_w1, hy_b1, hy_freq1, hy_w2, hy_b2, hy_freq2, hy_w3, hy_bias, s5_a_re, s5_a_im, s5_log_dt, s5_b_re, s5_b_im, s5_c_re, s5_c_im, s5_d, s5_glu_w, s5_glu_b, gla_gw, gla_gb, gla_norm):
    params = dict(norm_w=norm_w, w_in=w_in, w_out=w_out, mla_qa_norm=mla_qa_norm, mla_kva_norm=mla_kva_norm,
                  mla_w_uq=mla_w_uq, mla_w_ukv=mla_w_ukv, mla_q_norm=mla_q_norm, mla_k_norm=mla_k_norm,
                  hy_conv_w=hy_conv_w, hy_conv_b=hy_conv_b, hy_w1=hy_w1, hy_b1=hy_b1, hy_freq1=hy_freq1,
                  hy_w2=hy_w2, hy_b2=hy_b2, hy_freq2=hy_freq2, hy_w3=hy_w3, hy_bias=hy_bias,
                  s5_a_re=s5_a_re, s5_a_im=s5_a_im, s5_log_dt=s5_log_dt, s5_b_re=s5_b_re, s5_b_im=s5_b_im,
                  s5_c_re=s5_c_re, s5_c_im=s5_c_im, s5_d=s5_d, s5_glu_w=s5_glu_w, s5_glu_b=s5_glu_b,
                  gla_gw=gla_gw, gla_gb=gla_gb, gla_norm=gla_norm)
    bp, lp, d = x_prompt.shape
    bs, ls, _ = x_sample.shape
    past = cache_mla_ckv.shape[2]
    n_s5 = S5_N

    conds = jnp.concatenate([c_ctx[None, :], c, jnp.zeros((8 - 1 - bs, d), F32)], axis=0)
    mods = _modulation(conds, ada_w, ada_b).reshape(DEPTH, 8, 3, d)

    tabs_p = _hyena_tables(lp) + _odd_dft(lp)
    tabs_s = _hyena_tables(ls) + _odd_dft(ls)
    rope_tabs = _rope_tables(ls)
    tm_s = 512
    nseg = ls // S5_SEG

    y_p = x_prompt.reshape(bp * lp, d)
    y_s = x_sample.reshape(bs * ls, d)
    s5_l = []
    cache_bufs = None
    for l in range(DEPTH):
        wl = _layer_weights(l, params)
        y_p, (ckv, krope, s5_fin, gla_fin) = _trunk_layer(y_p, mods[l], lambda i: 0, wl, bp, lp, tabs_p,
                                                          layer=l, cache_bufs=cache_bufs)
        cache_bufs = (ckv, krope, gla_fin)
        s5_l.append(jnp.stack([s5_fin[:, :, :n_s5], s5_fin[:, :, n_s5:]], axis=-1)
                    .reshape(2, bp, S5_GROUPS, S5_STATE, 2).transpose(1, 0, 2, 3, 4))

        mla_ctx = jnp.concatenate([cache_mla_ckv[:, l], jnp.zeros((bs, past, 64), F32), cache_mla_krope[:, l],
                                   jnp.zeros((bs, past, 32), F32)], axis=-1).reshape(bs * past, 256)
        st = state_s5[:, l]
        h0 = jnp.concatenate([st[..., 0].reshape(bs, 2, n_s5), st[..., 1].reshape(bs, 2, n_s5)], axis=-1)
        h0 = h0.transpose(1, 0, 2)
        h0rows = jnp.zeros((2, nseg * bs, 2 * n_s5), F32)
        h0rows = h0rows.at[0, 0::nseg].set(h0[0]).at[1, nseg - 1::nseg].set(h0[1])
        eye_h = jnp.eye(GLA_HEADS, dtype=F32)
        gla_s0 = jnp.einsum("bdhke,hg->bdhegk", state_gla[:, l], eye_h).reshape(
            bs, 2, GLA_HEADS * GLA_DV, GLA_HEADS * GLA_DK)
        ctx = {"mla": mla_ctx, "past": past, "s5_h0": h0rows, "gla_s0": gla_s0}
        y_s, _ = _trunk_layer(y_s, mods[l], lambda i: 1 + (i * tm_s) // ls, wl, bs, ls, tabs_s, rope_tabs, ctx)

    return (y_p.reshape(bp, lp, d), y_s.reshape(bs, ls, d),
            cache_bufs[0], cache_bufs[1], jnp.stack(s5_l, axis=1), cache_bufs[2])
```

```python
import functools
import math

import numpy as np
import jax
import jax.numpy as jnp
from jax import lax
from jax.experimental import pallas as pl
from jax.experimental.pallas import tpu as pltpu

F32 = jnp.float32
BF16 = jnp.bfloat16

D_MODEL = 1024
DEPTH = 2
GRID_W = 64
D_BRANCH = 256
EPS = 1e-6

MLA_HEADS = 4
MLA_Q_RANK = 192
MLA_KV_RANK = 128
MLA_NOPE = 64
MLA_ROPE = 32
MLA_QK = 96
MLA_V = 64
ROPE_BASE = 10000.0
HEAD_PAD = 128

HY_BANDS = 16
HY_FEAT = 33
HY_HIDDEN = 64
HY_SHIFT = 0.05
HY_FAST_DECAY = 0.3
HY_SLOW_DECAY = 1.5
HY_TARGET = 1e-2
HY_BLOCK = 512

S5_GROUP = 16
S5_GROUPS = 16
S5_STATE = 64
S5_N = S5_GROUPS * S5_STATE
S5_ROWS = 512
S5_SEG = 256

GLA_HEADS = 4
GLA_DK = 32
GLA_DV = 64
GLA_RANK = 16
GLA_TAU = 16.0
GLA_CHUNK = 64
GLA_SUPER = 256

SEG_MLA = (0, 384)
SEG_GATE = (384, 1408)
SEG_HY = (1408, 2176)
SEG_S5 = (2176, 2432)
SEG_GLA = (2432, 2944)
N_PROJ = 2944
GLA_G_LANE = 96

VMEM_LIMIT = 48 * 1024 * 1024


def _cp(*sem):
    return pltpu.CompilerParams(dimension_semantics=sem, vmem_limit_bytes=VMEM_LIMIT)


def _dot(a, b):
    return jnp.dot(a, b, preferred_element_type=F32)


def _dot_nt(a, b):
    return lax.dot_general(a, b, (((1,), (1,)), ((), ())), preferred_element_type=F32)


def _dot_tn(a, b):
    return lax.dot_general(a, b, (((0,), (0,)), ((), ())), preferred_element_type=F32)


def _split2(x):
    hi = x.astype(BF16)
    lo = (x - hi.astype(F32)).astype(BF16)
    return hi, lo


def _split3(x):
    h1 = x.astype(BF16)
    r1 = x - h1.astype(F32)
    h2 = r1.astype(BF16)
    h3 = (r1 - h2.astype(F32)).astype(BF16)
    return h1, h2, h3


def _dot3(a, b):
    a1, a2 = _split2(a)
    b1, b2 = _split2(b)
    return _dot(a1, b1) + (_dot(a1, b2) + _dot(a2, b1))


def _silu(z):
    return z / (1.0 + jnp.exp(-z))


def _mod_kernel(c_ref, w_ref, b_ref, o_ref):
    s = _silu(c_ref[...])
    o_ref[0] = _dot(s.astype(BF16), w_ref[0].astype(BF16)) + b_ref[0]


def _modulation(conds, ada_w, ada_b):
    d = D_MODEL
    return pl.pallas_call(
        _mod_kernel,
        grid=(DEPTH, 3),
        in_specs=[pl.BlockSpec((8, d), lambda l, j: (0, 0)),
                  pl.BlockSpec((1, d, d), lambda l, j: (l, 0, j)),
                  pl.BlockSpec((1, 1, d), lambda l, j: (l, 0, j))],
        out_specs=pl.BlockSpec((1, 8, d), lambda l, j: (l, 0, j)),
        out_shape=jax.ShapeDtypeStruct((DEPTH, 8, 3 * d), F32),
        compiler_params=_cp("arbitrary", "arbitrary"),
        name="modulation",
    )(conds, ada_w, ada_b.reshape(DEPTH, 1, 3 * d))


def _inproj_kernel(x_ref, mod_ref, nw_ref, w_ref, o_mla, o_g, o_hy, o_s5, o_gla):
    x = x_ref[...]
    ms = jnp.mean(x * x, axis=-1, keepdims=True)
    y = x * lax.rsqrt(ms + EPS) * nw_ref[...]
    h = (y * (1.0 + mod_ref[0, 1:2, :]) + mod_ref[0, 0:1, :]).astype(BF16)
    for o, (lo, hi) in ((o_mla, SEG_MLA), (o_g, SEG_GATE), (o_hy, SEG_HY), (o_s5, SEG_S5), (o_gla, SEG_GLA)):
        o[...] = _dot(h, w_ref[:, lo:hi]).astype(o.dtype)


def _inproj(x, mod, mod_row, norm_w, w_p, tm):
    n, d = x.shape
    widths = [hi - lo for lo, hi in (SEG_MLA, SEG_GATE, SEG_HY, SEG_S5, SEG_GLA)]
    dtypes = [F32, BF16, F32, F32, F32]
    return pl.pallas_call(
        _inproj_kernel,
        grid=(n // tm,),
        in_specs=[pl.BlockSpec((tm, d), lambda i: (i, 0)),
                  pl.BlockSpec((1, 3, d), lambda i: (mod_row(i), 0, 0)),
                  pl.BlockSpec((1, d), lambda i: (0, 0)),
                  pl.BlockSpec((d, N_PROJ), lambda i: (0, 0))],
        out_specs=[pl.BlockSpec((tm, w), lambda i: (i, 0)) for w in widths],
        out_shape=[jax.ShapeDtypeStruct((n, w), t) for w, t in zip(widths, dtypes)],
        compiler_params=_cp("arbitrary"),
        name="inproj",
    )(x, mod, norm_w, w_p)


def _head_norm(xh, w):
    ms = jnp.sum(xh * xh, axis=-1, keepdims=True) * (1.0 / MLA_QK)
    return xh * lax.rsqrt(ms + EPS) * w


def _rope(xh, cos, sin_a, sin_b):
    return xh * cos + pltpu.roll(xh, HEAD_PAD - 8, 1) * sin_a + pltpu.roll(xh, 8, 1) * sin_b


def _mla_prep_kernel(has_q, rope, cache_seqs, n_aliased, *refs):
    refs = list(refs)
    m_ref = refs.pop(0)
    if has_q:
        qan_ref, wuq_ref, qn_ref, kvn_ref = refs[:4]
        refs = refs[4:]
    wuk_ref, wuv_ref, kn_ref = refs[:3]
    refs = refs[3:]
    if rope:
        cos_ref, sa_ref, sb_ref = refs[:3]
        refs = refs[3:]
        cos, sa, sb = cos_ref[...], sa_ref[...], sb_ref[...]
    refs = refs[n_aliased:]
    if has_q:
        q_ref = refs.pop(0)
    k_ref, v_ref = refs[:2]
    if cache_seqs:
        ckv_ref, kro_ref = refs[2:]
    m = m_ref[...]
    if has_q:
        lane = lax.broadcasted_iota(jnp.int32, (1, HEAD_PAD), 1)
        mixed = m[:, 128:256]
        cq = jnp.concatenate([m[:, 0:128], jnp.where(lane < MLA_NOPE, mixed, 0.0)], axis=1)
        ms = jnp.sum(cq * cq, axis=-1, keepdims=True) * (1.0 / MLA_Q_RANK)
        cqn = cq * lax.rsqrt(ms + EPS) * qan_ref[...]
        q = _dot(cqn.astype(BF16), wuq_ref[...])
        ckv = m[:, 256:384]
        ckvn = ckv * lax.rsqrt(jnp.mean(ckv * ckv, axis=-1, keepdims=True) + EPS) * kvn_ref[...]
        kr = jnp.where(jnp.logical_and(lane >= MLA_NOPE, lane < MLA_NOPE + MLA_ROPE), mixed, 0.0)
        if cache_seqs:
            seq_len = ckv_ref.shape[2]
            for s in range(cache_seqs):
                ckv_ref[s, 0] = ckvn[s * seq_len:(s + 1) * seq_len]
                kro_ref[s, 0] = kr[s * seq_len:(s + 1) * seq_len, MLA_NOPE:MLA_NOPE + MLA_ROPE]
    else:
        ckvn = m[:, 0:128]
        kr = m[:, 128:256]
    cb = ckvn.astype(BF16)
    kup = _dot(cb, wuk_ref[...])
    v_ref[...] = _dot(cb, wuv_ref[...]).astype(BF16)
    for h in range(MLA_HEADS):
        sl = slice(HEAD_PAD * h, HEAD_PAD * (h + 1))
        kh = _head_norm(kup[:, sl] + kr, kn_ref[...])
        if rope:
            kh = _rope(kh, cos, sa, sb)
        k_ref[:, sl] = kh.astype(BF16)
        if has_q:
            qh = _head_norm(q[:, sl], qn_ref[...])
            if rope:
                qh = _rope(qh, cos, sa, sb)
            q_ref[:, sl] = (qh * (MLA_QK ** -0.5)).astype(BF16)


def _mla_prep(m, wl, rope_tabs, seq_len, tm, has_q, cache=None):
    n, wm = m.shape
    rope = rope_tabs is not None
    full = lambda shape: pl.BlockSpec(shape, lambda i: (0,) * len(shape))
    args, specs = [m], [pl.BlockSpec((tm, wm), lambda i: (i, 0))]
    if has_q:
        args += [wl["qa_norm"], wl["w_uq"], wl["q_norm"], wl["kva_norm"]]
        specs += [full((1, 256)), full((256, 512)), full((1, 128)), full((1, 128))]
    args += [wl["w_uk"], wl["w_uv"], wl["k_norm"]]
    specs += [full((128, 512)), full((128, 256)), full((1, 128))]
    if rope:
        nt = seq_len // tm
        args += list(rope_tabs)
        specs += [pl.BlockSpec((tm, HEAD_PAD), lambda i: (i % nt, 0))] * 3
    row = lambda w: pl.BlockSpec((tm, w), lambda i: (i, 0))
    out_specs = [row(512), row(256)]
    out_shape = [jax.ShapeDtypeStruct((n, 512), BF16), jax.ShapeDtypeStruct((n, 256), BF16)]
    aliases = {}
    nseq = 0
    if has_q:
        out_specs = [row(512)] + out_specs
        out_shape = [jax.ShapeDtypeStruct((n, 512), BF16)] + out_shape
    if cache is not None:
        layer, prev = cache
        nseq = tm // seq_len
        for w in (MLA_KV_RANK, MLA_ROPE):
            out_specs.append(pl.BlockSpec((nseq, 1, seq_len, w), lambda i: (i, layer, 0, 0)))
            out_shape.append(jax.ShapeDtypeStruct((n // seq_len, DEPTH, seq_len, w), F32))
        if prev is not None:
            for k, buf in enumerate(prev):
                aliases[len(args)] = len(out_shape) - 2 + k
                args.append(buf)
                specs.append(pl.BlockSpec(memory_space=pl.ANY))
    return pl.pallas_call(
        functools.partial(_mla_prep_kernel, has_q, rope, nseq, len(aliases)),
        grid=(n // tm,),
        in_specs=specs, out_specs=out_specs, out_shape=out_shape,
        input_output_aliases=aliases,
        compiler_params=_cp("arbitrary"),
        name="mla_prep",
    )(*args)


def _attn_kernel(nparts, q_ref, *refs):
    kv = [(refs[2 * i], refs[2 * i + 1]) for i in range(nparts)]
    o_ref = refs[2 * nparts]
    low = lax.broadcasted_iota(jnp.int32, (1, HEAD_PAD), 1) < MLA_V

    def scores(h):
        sl = slice(HEAD_PAD * h, HEAD_PAD * (h + 1))
        return [_dot_nt(q_ref[:, sl], k_ref[:, sl]) for k_ref, _ in kv]

    s_next = scores(0)
    acc = None
    for h in range(MLA_HEADS):
        pair, j = divmod(h, 2)
        s = s_next
        if h + 1 < MLA_HEADS:
            s_next = scores(h + 1)
        if j == 0:
            v_half = []
            for _, v_ref in kv:
                vp = v_ref[:, HEAD_PAD * pair:HEAD_PAD * (pair + 1)]
                zero = jnp.zeros_like(vp)
                v_half.append((jnp.where(low, vp, zero), jnp.where(low, zero, vp)))
        m = functools.reduce(jnp.maximum, [jnp.max(x, axis=-1, keepdims=True) for x in s])
        p = [jnp.exp(x - m) for x in s]
        den = functools.reduce(jnp.add, [jnp.sum(x, axis=-1, keepdims=True) for x in p])
        num = functools.reduce(jnp.add, [_dot(x.astype(BF16), vh[j]) for x, vh in zip(p, v_half)])
        o = num / den
        acc = o if j == 0 else acc + o
        if j == 1:
            o_ref[:, HEAD_PAD * pair:HEAD_PAD * (pair + 1)] = acc.astype(BF16)


def _attention(q, kv_parts, batch, lq, tq):
    nq = lq // tq
    args, specs = [q], [pl.BlockSpec((tq, 512), lambda b, i: (b * nq + i, 0))]
    for k, v, lk in kv_parts:
        args += [k, v]
        specs += [pl.BlockSpec((lk, 512), lambda b, i: (b, 0)), pl.BlockSpec((lk, 256), lambda b, i: (b, 0))]
    return pl.pallas_call(
        functools.partial(_attn_kernel, len(kv_parts)),
        grid=(batch, nq),
        in_specs=specs,
        out_specs=pl.BlockSpec((tq, 256), lambda b, i: (b * nq + i, 0)),
        out_shape=jax.ShapeDtypeStruct((batch * lq, 256), BF16),
        compiler_params=_cp("arbitrary", "arbitrary"),
        name="attention",
    )(*args)


def _hyena_kernel(seq_len, bk, nseq, x_ref, cw_ref, cb_ref, fo_ref, go_ref, k_ref, bias_ref, o_ref,
                  u_sc, y_sc, z_sc):
    c = D_BRANCH
    n = nseq * seq_len
    nblk = seq_len // bk
    pos = jnp.bitwise_and(lax.broadcasted_iota(jnp.int32, (n, 1), 0), seq_len - 1)
    first, last = pos == 0, pos == seq_len - 1

    def short_conv(g):
        cols = slice(g * c, (g + 1) * c)
        x = x_ref[:, cols]
        xm = jnp.where(first, 0.0, pltpu.roll(x, 1, 0))
        xp = jnp.where(last, 0.0, pltpu.roll(x, n - 1, 0))
        return cw_ref[0:1, cols] * xm + cw_ref[1:2, cols] * x + cw_ref[2:3, cols] * xp + cb_ref[:, cols]

    fo, go = fo_ref[...], go_ref[...]

    def long_conv(v, order, emit):
        cols = slice(order * c, (order + 1) * c)
        for j in range(nblk):
            u_sc[j] = _dot(fo, v[j * bk:(j + 1) * bk].astype(BF16))
        rc = 32
        for i in range(nblk):
            def mix(r, carry, i=i):
                top = pl.ds(pl.multiple_of(r * rc, rc), rc)
                bot = pl.ds(pl.multiple_of(bk + r * rc, rc), rc)
                at = ab = None
                for j in range(nblk):
                    q = i - j + nblk - 1
                    kt, kb = k_ref[q, top, cols], k_ref[q, bot, cols]
                    ut, ub = u_sc[j, top, :], u_sc[j, bot, :]
                    pt, pb = ut * kt - ub * kb, ut * kb + ub * kt
                    at, ab = (pt, pb) if at is None else (at + pt, ab + pb)
                z_sc[top, :] = at.astype(BF16)
                z_sc[bot, :] = ab.astype(BF16)
                return carry

            lax.fori_loop(0, bk // rc, mix, 0, unroll=True if nblk == 1 else 2)
            emit(i, _dot(go, z_sc[...]))

    v_all, x1_all, x2_all = short_conv(0), short_conv(1), short_conv(2)
    for s in range(nseq):
        base = s * seq_len
        v = v_all[base:base + seq_len]

        def emit1(i, y, base=base, v=v):
            r = slice(i * bk, (i + 1) * bk)
            y_sc[r, :] = x1_all[base + i * bk:base + (i + 1) * bk] * (y + bias_ref[0:1, :] * v[r])

        long_conv(v, 0, emit1)
        y1 = y_sc[...]

        def emit2(i, y, base=base, y1=y1):
            r = slice(i * bk, (i + 1) * bk)
            o_ref[base + i * bk:base + (i + 1) * bk, :] = (
                x2_all[base + i * bk:base + (i + 1) * bk] * (y + bias_ref[1:2, :] * y1[r])).astype(BF16)

        long_conv(y1, 1, emit2)


def _hyena(x, wl, kspec, fo, go, batch, seq_len):
    c = D_BRANCH
    bk = fo.shape[1]
    nseq = max(1, min(batch, 1024 // seq_len))
    rows = nseq * seq_len
    nblk = seq_len // bk
    full = lambda a: pl.BlockSpec(a.shape, lambda i: (0,) * a.ndim)
    return pl.pallas_call(
        functools.partial(_hyena_kernel, seq_len, bk, nseq),
        grid=(batch // nseq,),
        in_specs=[pl.BlockSpec((rows, 3 * c), lambda i: (i, 0)),
                  full(wl["hy_conv_w"]), full(wl["hy_conv_b"]), full(fo), full(go),
                  pl.BlockSpec(kspec.shape, lambda i: (0, 0, 0), pipeline_mode=pl.Buffered(1)),
                  full(wl["hy_bias"])],
        out_specs=pl.BlockSpec((rows, c), lambda i: (i, 0)),
        out_shape=jax.ShapeDtypeStruct((batch * seq_len, c), BF16),
        scratch_shapes=[pltpu.VMEM((nblk, 2 * bk, c), F32), pltpu.VMEM((seq_len, c), F32),
                        pltpu.VMEM((2 * bk, c), BF16)],
        compiler_params=pltpu.CompilerParams(dimension_semantics=("arbitrary",), vmem_limit_bytes=56 * 1024 * 1024),
        name="hyena",
    )(x, wl["hy_conv_w"], wl["hy_conv_b"], fo, go, kspec, wl["hy_bias"])


def _hy_mlp_kernel(feat_ref, w1_ref, b1_ref, f1_ref, w2_ref, b2_ref, f2_ref, w3_ref, win_ref,
                   kern_ref, nrm_ref):
    i = pl.program_id(0)
    tl = feat_ref.shape[0]
    h = jnp.sin(f1_ref[...] * (_dot3(feat_ref[...], w1_ref[...]) + b1_ref[...]))
    h = jnp.sin(f2_ref[...] * (_dot3(h, w2_ref[...]) + b2_ref[...]))
    filt = _dot3(h, w3_ref[...])
    win = win_ref[...]
    row0 = (lax.broadcasted_iota(jnp.int32, (tl, 1), 0) + i * tl) == 0
    filt = jnp.where(row0, 0.0, filt * jnp.concatenate([win, win], axis=1))
    kern_ref[...] = filt
    part = jnp.sum(jnp.abs(filt), axis=0, keepdims=True)

    @pl.when(i == 0)
    def _():
        nrm_ref[...] = jnp.zeros_like(nrm_ref)
    nrm_ref[...] += jnp.broadcast_to(part, nrm_ref.shape)


def _hy_mlp(feat, wl, win):
    rows = feat.shape[0]
    tl = 256
    nl = rows // (2 * tl)
    full = lambda shape: pl.BlockSpec(shape, lambda i: (0,) * len(shape))
    return pl.pallas_call(
        _hy_mlp_kernel,
        grid=(rows // tl,),
        in_specs=[pl.BlockSpec((tl, 128), lambda i: (i, 0)),
                  full((128, 128)), full((1, 128)), full((1, 128)),
                  full((128, 128)), full((1, 128)), full((1, 128)),
                  pl.BlockSpec((128, 512), lambda i: (0, jnp.where(i < nl, 1, 0))),
                  pl.BlockSpec((tl, 256), lambda i: (i, 0))],
        out_specs=[pl.BlockSpec((tl, 512), lambda i: (i, 0)), full((8, 512))],
        out_shape=[jax.ShapeDtypeStruct((rows, 512), F32), jax.ShapeDtypeStruct((8, 512), F32)],
        compiler_params=_cp("arbitrary"),
        name="hy_mlp",
    )(feat, wl["hy_w1"], wl["hy_b1"], wl["hy_f1"], wl["hy_w2"], wl["hy_b2"], wl["hy_f2"], wl["hy_w3"], win)


def _hy_kspec_kernel(lo_ref, hi_ref, fo_ref, fb_ref, n_ref, o_ref):
    bk = lo_ref.shape[0]
    k = _dot(fo_ref[...], hi_ref[...].astype(BF16)) + _dot(fb_ref[...], lo_ref[...].astype(BF16))
    o_ref[0] = k * ((1.0 / bk) / n_ref[0:1, :])


def _hy_kspec(kern_lin, nrm, fo, fb):
    n2, bk = fo.shape
    nq = kern_lin.shape[0] // bk - 1
    full = lambda a: pl.BlockSpec(a.shape, lambda q: (0,) * a.ndim)
    return pl.pallas_call(
        _hy_kspec_kernel,
        grid=(nq,),
        in_specs=[pl.BlockSpec((bk, 512), lambda q: (q, 0)),
                  pl.BlockSpec((bk, 512), lambda q: (q + 1, 0)),
                  full(fo), full(fb), full(nrm)],
        out_specs=pl.BlockSpec((1, n2, 512), lambda q: (q, 0, 0)),
        out_shape=jax.ShapeDtypeStruct((nq, n2, 512), F32),
        compiler_params=_cp("arbitrary"),
        name="hy_kspec",
    )(kern_lin, kern_lin, fo, fb, nrm)


def _s5_discretise(are_ref, aim_ref, ldt_ref):
    ar = jnp.minimum(are_ref[0], -1e-4)
    ai = aim_ref[0]
    dt = jnp.exp(ldt_ref[0])
    e = jnp.exp(ar * dt)
    return ar, ai, e * jnp.cos(ai * dt), e * jnp.sin(ai * dt)


def _s5_scan_kernel(nseq, emit_y, u_ref, hin_ref, are_ref, aim_ref, ldt_ref, bre_ref, bim_ref, *rest):
    if emit_y:
        cre_ref, cim_ref, y_ref, hfin_ref, wb_sc, ab_sc, s_sc, hc_sc, perm_sc, wc_sc = rest
    else:
        hfin_ref, wb_sc, ab_sc, s_sc, hc_sc, perm_sc = rest
    d = pl.program_id(0)
    c = pl.program_id(1)
    n = S5_N

    @pl.when(c == 0)
    def _():
        ar, ai, abr, abi = _s5_discretise(are_ref, aim_ref, ldt_ref)
        ab_sc[0:1, :] = abr
        ab_sc[1:2, :] = abi
        den = 1.0 / (ar * ar + ai * ai)
        cr = ((abr - 1.0) * ar + abi * ai) * den
        ci = (abi * ar - (abr - 1.0) * ai) * den
        bre, bim = bre_ref[0], bim_ref[0]
        wb_sc[:, 0:n] = (cr * bre - ci * bim).astype(BF16)
        wb_sc[:, n:2 * n] = (cr * bim + ci * bre).astype(BF16)
        if emit_y:
            wc_sc[0:n, :] = cre_ref[0].astype(BF16)
            wc_sc[n:2 * n, :] = (-cim_ref[0]).astype(BF16)
        hc_sc[...] = hin_ref[0]

    steps = u_ref.shape[1]
    rows_c = nseq * steps

    @pl.when(c == 0)
    def _():
        i = lax.broadcasted_iota(jnp.int32, (rows_c, rows_c), 0)
        j = lax.broadcasted_iota(jnp.int32, (rows_c, rows_c), 1)
        p = lax.shift_right_logical(i, int(math.log2(nseq)))
        step = p + d * (steps - 1 - 2 * p)
        src = jnp.bitwise_and(i, nseq - 1) * steps + step
        perm_sc[...] = jnp.where(j == src, 1.0, 0.0).astype(BF16)

    lhs = _dot(perm_sc[...], u_ref[...].reshape(rows_c, D_BRANCH).astype(BF16)).astype(BF16)
    lb = 256
    y = None
    for j in range(n // lb):
        lr = slice(lb * j, lb * (j + 1))
        li = slice(n + lb * j, n + lb * (j + 1))
        bur = _dot(lhs, wb_sc[:, lr])
        bui = _dot(lhs, wb_sc[:, li])
        abr = ab_sc[0:1, lr]
        abi = ab_sc[1:2, lr]
        hr, hi = hc_sc[:, lr], hc_sc[:, li]
        for p in range(steps):
            rows = slice(p * nseq, (p + 1) * nseq)
            hr, hi = abr * hr - abi * hi + bur[rows], abr * hi + abi * hr + bui[rows]
            if emit_y:
                s_sc[rows, lr] = hr.astype(BF16)
                s_sc[rows, li] = hi.astype(BF16)
        hc_sc[:, lr] = hr
        hc_sc[:, li] = hi
        if emit_y:
            yj = _dot(s_sc[:, lr], wc_sc[lr, :]) + _dot(s_sc[:, li], wc_sc[li, :])
            y = yj if y is None else y + yj

    if emit_y:
        @pl.when(d == 0)
        def _():
            for p in range(steps):
                y_ref[0, :, p, :] = y[p * nseq:(p + 1) * nseq]

        @pl.when(d == 1)
        def _():
            for p in range(steps):
                y_ref[0, :, steps - 1 - p, :] = y[p * nseq:(p + 1) * nseq]

    @pl.when(c == pl.num_programs(1) - 1)
    def _():
        hfin_ref[0] = hc_sc[...]


def _s5_scan(u, hin, wl, emit_y):
    nseq, nstep, _ = u.shape
    steps = S5_ROWS // nseq
    nc = nstep // steps
    n = S5_N
    chunk = lambda d, c: c + d * (nc - 1 - 2 * c)
    per_dir = lambda shape: pl.BlockSpec((1,) + shape, lambda d, c: (d,) + (0,) * len(shape))
    args = [u, hin, wl["s5_are"], wl["s5_aim"], wl["s5_ldt"], wl["s5_bre"], wl["s5_bim"]]
    specs = [pl.BlockSpec((nseq, steps, D_BRANCH), lambda d, c: (0, chunk(d, c), 0)),
             per_dir((nseq, 2 * n)), per_dir((1, n)), per_dir((1, n)), per_dir((1, n)),
             per_dir((D_BRANCH, n)), per_dir((D_BRANCH, n))]
    out_specs = [per_dir((nseq, 2 * n))]
    out_shape = [jax.ShapeDtypeStruct((2, nseq, 2 * n), F32)]
    scratch = [pltpu.VMEM((D_BRANCH, 2 * n), BF16), pltpu.VMEM((8, n), F32),
               pltpu.VMEM((S5_ROWS, 2 * n), BF16), pltpu.VMEM((nseq, 2 * n), F32),
               pltpu.VMEM((S5_ROWS, S5_ROWS), BF16)]
    if emit_y:
        args += [wl["s5_cre"], wl["s5_cim"]]
        specs += [per_dir((n, D_BRANCH)), per_dir((n, D_BRANCH))]
        out_specs = [pl.BlockSpec((1, nseq, steps, D_BRANCH), lambda d, c: (d, 0, chunk(d, c), 0))] + out_specs
        out_shape = [jax.ShapeDtypeStruct((2, nseq, nstep, D_BRANCH), F32)] + out_shape
        scratch += [pltpu.VMEM((2 * n, D_BRANCH), BF16)]
    return pl.pallas_call(
        functools.partial(_s5_scan_kernel, nseq, emit_y),
        grid=(2, nc),
        in_specs=specs, out_specs=out_specs, out_shape=out_shape, scratch_shapes=scratch,
        compiler_params=_cp("arbitrary", "arbitrary"),
        name="s5_scan" if emit_y else "s5_scan_finals",
    )(*args)


def _s5_chain_kernel(batch, nseg, f_ref, h0_ref, are_ref, aim_ref, ldt_ref, o_ref):
    d = pl.program_id(0)
    n = S5_N
    _, _, pr, pi = _s5_discretise(are_ref, aim_ref, ldt_ref)
    for _ in range(int(math.log2(S5_SEG))):
        pr, pi = pr * pr - pi * pi, 2.0 * pr * pi
    f = f_ref[0]
    fr, fi = f[:, 0:n], f[:, n:2 * n]
    h0 = h0_ref[0]
    h0r, h0i = h0[:, 0:n], h0[:, n:2 * n]
    nrow = batch * nseg
    seg = jnp.bitwise_and(lax.broadcasted_iota(jnp.int32, (nrow, 1), 0), nseg - 1)

    def run(shift, keep):
        xr, xi = h0r, h0i
        for _ in range(nseg - 1):
            zr = fr + pr * xr - pi * xi
            zi = fi + pr * xi + pi * xr
            xr = h0r + jnp.where(keep, pltpu.roll(zr, shift, 0), 0.0)
            xi = h0i + jnp.where(keep, pltpu.roll(zi, shift, 0), 0.0)
        o_ref[0, :, 0:n] = xr
        o_ref[0, :, n:2 * n] = xi

    @pl.when(d == 0)
    def _():
        run(1, seg != 0)

    @pl.when(d == 1)
    def _():
        run(nrow - 1, seg != nseg - 1)


def _s5_chain(fin, h0rows, wl, batch, nseg):
    nrow = batch * nseg
    n = S5_N
    per_dir = lambda shape: pl.BlockSpec((1,) + shape, lambda d: (d,) + (0,) * len(shape))
    return pl.pallas_call(
        functools.partial(_s5_chain_kernel, batch, nseg),
        grid=(2,),
        in_specs=[per_dir((nrow, 2 * n)), per_dir((nrow, 2 * n)), per_dir((1, n)), per_dir((1, n)), per_dir((1, n))],
        out_specs=per_dir((nrow, 2 * n)),
        out_shape=jax.ShapeDtypeStruct((2, nrow, 2 * n), F32),
        compiler_params=_cp("arbitrary"),
        name="s5_chain",
    )(fin, h0rows, wl["s5_are"], wl["s5_aim"], wl["s5_ldt"])


def _gla_kernel(seq_len, nb, n_aliased, q_ref, k_ref, v_ref, g_ref, gw_ref, gb_ref, s0_ref, *rest):
    o_ref, sfin_ref, qe_sc, upd_sc, dec_sc, sall_sc, lhs_sc, kt_sc, la_sc, oi_sc = rest[n_aliased:]
    d = pl.program_id(1)
    sign = 1 - 2 * d
    ck, sup = GLA_CHUNK, GLA_SUPER
    cps = sup // ck
    nsup, nchunk = seq_len // sup, seq_len // ck
    dk, dv = GLA_HEADS * GLA_DK, GLA_HEADS * GLA_DV
    r = lax.broadcasted_iota(jnp.int32, (sup, sup), 0)
    s = lax.broadcasted_iota(jnp.int32, (sup, sup), 1)
    same = lax.shift_right_logical(r, 6) == lax.shift_right_logical(s, 6)
    tri = jnp.logical_and(same, (s - r) * sign <= 0)
    cum_lhs = jnp.where(tri, 1.0, 0.0).astype(BF16)
    pos = jnp.bitwise_and(lax.broadcasted_iota(jnp.int32, (ck, 1), 0), ck - 1)
    is_last = pos == (ck - 1) * (1 - d)
    row_chunk = lax.shift_right_logical(lax.broadcasted_iota(jnp.int32, (sup, 1), 0), 6)
    head_k = lax.shift_right_logical(lax.broadcasted_iota(jnp.int32, (1, dk), 1), 5)
    head_v = lax.shift_right_logical(lax.broadcasted_iota(jnp.int32, (1, dv), 1), 6)
    blockdiag = lax.shift_right_logical(lax.broadcasted_iota(jnp.int32, (dv, 1), 0), 6) == head_k

    def group_rows(u):
        return pl.ds(u * sup, sup) if isinstance(u, int) else pl.ds(pl.multiple_of(u * sup, sup), sup)

    def stage_a(u, slot):
        rows = group_rows(u)
        q = q_ref[rows, :] * (GLA_DK ** -0.5)
        k = k_ref[rows, :]
        v = v_ref[rows, :]
        cs = _dot(cum_lhs, la_sc[rows, :])
        yield
        bc = cs[:, 0:dk] + cs[:, dk:2 * dk]
        tots = [jnp.sum(jnp.where(is_last, bc[c * ck:(c + 1) * ck], 0.0), axis=0, keepdims=True)
                for c in range(cps)]
        tot = jnp.concatenate([jnp.broadcast_to(t, (ck, dk)) for t in tots], axis=0)
        ref = 0.5 * tot
        qt = q * jnp.exp(bc - ref)
        kt_sc[slot] = (k * jnp.exp(ref - bc)).astype(BF16)
        lhs_sc[slot] = jnp.concatenate(
            [jnp.where(head_k == h, qt, 0.0) for h in range(GLA_HEADS)], axis=0).astype(BF16)
        qe_sc[rows, :] = (q * jnp.exp(bc)).astype(BF16)
        kl = (k * jnp.exp(tot - bc)).astype(BF16)
        zero = jnp.zeros_like(kl)
        klx = jnp.concatenate([jnp.where(row_chunk == c, kl, zero) for c in range(cps)], axis=1)
        upd = _dot_tn(v.astype(BF16), klx)
        yield
        for c in range(cps):
            upd_sc[u * cps + c] = jnp.where(blockdiag, upd[:, c * dk:(c + 1) * dk], 0.0)
            dec_sc[u * cps + c] = jnp.broadcast_to(jnp.exp(tots[c]), (8, dk))

    def stage_b(u, slot):
        rows = group_rows(u)
        v = v_ref[rows, :]
        p = _dot_nt(lhs_sc[slot], kt_sc[slot])
        yield
        att = jnp.concatenate([jnp.where(tri, p[h * sup:(h + 1) * sup], 0.0).astype(BF16)
                               for h in range(GLA_HEADS)], axis=1)
        vexp = jnp.concatenate([jnp.where(head_v == h, v, 0.0).astype(BF16) for h in range(GLA_HEADS)], axis=0)
        o = _dot(att, vexp)
        yield
        oi_sc[rows, :] = o

    def run(*stages):
        live = list(stages)
        while live:
            for g in list(live):
                if next(g, StopIteration) is StopIteration:
                    live.remove(g)

    x = _dot(g_ref[...].astype(BF16), gw_ref[0]) + gb_ref[0]
    la = (jnp.minimum(x, 0.0) - jnp.log(1.0 + jnp.exp(-jnp.abs(x)))) * (1.0 / GLA_TAU)
    la_sc[...] = jnp.concatenate(_split2(la), axis=1)

    nu = nb * nsup
    if nu <= 4:
        run(stage_a(0, 0))
        for u in range(nu):
            if u + 1 < nu:
                run(stage_b(u, u % 2), stage_a(u + 1, (u + 1) % 2))
            else:
                run(stage_b(u, u % 2))
    else:
        run(stage_a(0, 0))

        def sup_body(u, carry):
            nxt = jnp.minimum(u + 1, nu - 1)
            run(stage_b(u, jnp.bitwise_and(u, 1)), stage_a(nxt, jnp.bitwise_and(u + 1, 1)))
            return carry

        lax.fori_loop(0, nu, sup_body, 0)

    for j in range(nb):
        def state_body(c, st, j=j):
            ci = j * nchunk + c + d * (nchunk - 1 - 2 * c)
            sall_sc[ci] = st.astype(BF16)
            return dec_sc[ci][0:1, :] * st + upd_sc[ci]

        st_fin = jnp.transpose(lax.fori_loop(0, nchunk, state_body, s0_ref[j, 0]))
        for h in range(GLA_HEADS):
            sfin_ref[j, 0, 0, h] = st_fin[h * GLA_DK:(h + 1) * GLA_DK, h * GLA_DV:(h + 1) * GLA_DV]

    def inter(u):
        rows = group_rows(u)
        qe = qe_sc[rows, :]
        zero = jnp.zeros_like(qe)
        lhs = jnp.concatenate([jnp.where(row_chunk == c, qe, zero) for c in range(cps)], axis=1)
        st = jnp.concatenate([sall_sc[u * cps + c] for c in range(cps)], axis=1)
        o_ref[0, rows, :] = (oi_sc[rows, :] + _dot_nt(lhs, st)).astype(BF16)

    if nu <= 4:
        for u in range(nu):
            inter(u)
    else:
        lax.fori_loop(0, nu, lambda u, carry: (inter(u), carry)[1], 0, unroll=2 if nu % 2 == 0 else 1)


def _gla(gla_in, mla_in, wl, s0t, batch, seq_len, fin=(0, None, 1)):
    layer, prev_fin, fin_layers = fin
    aliases = {} if prev_fin is None else {7: 1}
    extra = [] if prev_fin is None else [prev_fin]
    n = gla_in.shape[0]
    dk, dv = GLA_HEADS * GLA_DK, GLA_HEADS * GLA_DV
    nb = max(1, min(batch, 1024 // seq_len))
    rows = nb * seq_len
    nchunk = nb * (seq_len // GLA_CHUNK)
    return pl.pallas_call(
        functools.partial(_gla_kernel, seq_len, nb, len(extra)),
        grid=(batch // nb, 2),
        input_output_aliases=aliases,
        in_specs=[pl.BlockSpec((rows, dk), lambda b, d: (b, 0)),
                  pl.BlockSpec((rows, dk), lambda b, d: (b, 1)),
                  pl.BlockSpec((rows, dv), lambda b, d: (b, 1)),
                  pl.BlockSpec((rows, dk), lambda b, d: (b, 1)),
                  pl.BlockSpec((1, dk, dk), lambda b, d: (d, 0, 0)),
                  pl.BlockSpec((1, 1, dk), lambda b, d: (d, 0, 0)),
                  pl.BlockSpec((nb, 1, dv, dk), lambda b, d: (b, d, 0, 0))]
                 + [pl.BlockSpec(memory_space=pl.ANY)] * len(extra),
        out_specs=[pl.BlockSpec((1, rows, dv), lambda b, d: (d, b, 0)),
                   pl.BlockSpec((nb, 1, 1, GLA_HEADS, GLA_DK, GLA_DV), lambda b, d: (b, layer, d, 0, 0, 0))],
        out_shape=[jax.ShapeDtypeStruct((2, n, dv), BF16),
                   jax.ShapeDtypeStruct((batch, fin_layers, 2, GLA_HEADS, GLA_DK, GLA_DV), F32)],
        scratch_shapes=[pltpu.VMEM((rows, dk), BF16),
                        pltpu.VMEM((nchunk, dv, dk), F32),
                        pltpu.VMEM((nchunk, 8, dk), F32),
                        pltpu.VMEM((nchunk, dv, dk), BF16),
                        pltpu.VMEM((2, GLA_HEADS * GLA_SUPER, dk), BF16),
                        pltpu.VMEM((2, GLA_SUPER, dk), BF16),
                        pltpu.VMEM((rows, 2 * dk), BF16),
                        pltpu.VMEM((rows, dv), F32)],
        compiler_params=_cp("arbitrary", "arbitrary"),
        name="gla",
    )(gla_in, gla_in, gla_in, mla_in, wl["gla_gw"], wl["gla_gb"], s0t, *extra)


def _outproj_kernel(x_ref, mod_ref, g_ref, om_ref, oh_ref, su_ref, sf_ref, sb_ref, sd_ref, sw_ref, sbias_ref,
                    gf_ref, gb_ref, gn_ref, hm_ref, w_ref, y_ref):
    c = D_BRANCH
    g = g_ref[...].astype(F32)
    acc = _dot((om_ref[...].astype(F32) * _silu(g[:, 0:c])).astype(BF16), w_ref[0:c, :])
    acc += _dot((oh_ref[...].astype(F32) * _silu(g[:, c:2 * c])).astype(BF16), w_ref[c:2 * c, :])
    ys = sd_ref[...] * su_ref[...] + sf_ref[0] + sb_ref[0]
    ge = 0.5 * ys * (1.0 + jnp.tanh(math.sqrt(2.0 / math.pi) * (ys + 0.044715 * (ys * ys * ys))))
    o_s5 = ge / (1.0 + jnp.exp(-(_dot(ge.astype(BF16), sw_ref[...]) + sbias_ref[...])))
    acc += _dot((o_s5 * _silu(g[:, 2 * c:3 * c])).astype(BF16), w_ref[2 * c:3 * c, :])
    og = gf_ref[0].astype(F32) + gb_ref[0].astype(F32)
    hi, lo = _split2(og * og)
    ms = _dot(hi, hm_ref[...]) + _dot(lo, hm_ref[...])
    ogn = og * lax.rsqrt(ms + EPS) * gn_ref[...]
    acc += _dot((ogn * _silu(g[:, 3 * c:4 * c])).astype(BF16), w_ref[3 * c:4 * c, :])
    y_ref[...] = x_ref[...] + mod_ref[0, 2:3, :] * acc


def _outproj(x, mod, mod_row, gates, o_mla, o_hy, s5_u, s5_y, o_gla, wl, tm):
    n, d = x.shape
    c = D_BRANCH
    row = lambda w: pl.BlockSpec((tm, w), lambda i: (i, 0))
    per_dir = lambda k: pl.BlockSpec((1, tm, c), lambda i: (k, i, 0))
    full = lambda *shape: pl.BlockSpec(shape, lambda i: (0,) * len(shape))
    return pl.pallas_call(
        _outproj_kernel,
        grid=(n // tm,),
        in_specs=[row(d),
                  pl.BlockSpec((1, 3, d), lambda i: (mod_row(i), 0, 0)),
                  row(d), row(c), row(c),
                  row(c), per_dir(0), per_dir(1), full(1, c), full(c, c), full(1, c),
                  per_dir(0), per_dir(1), full(1, c), full(c, c), full(d, d)],
        out_specs=row(d),
        out_shape=jax.ShapeDtypeStruct((n, d), F32),
        compiler_params=_cp("arbitrary"),
        name="outproj",
    )(x, mod, gates, o_mla, o_hy, s5_u, s5_y, s5_y, wl["s5_d"], wl["s5_glu_w"], wl["s5_glu_b"],
      o_gla, o_gla, wl["gla_norm"], wl["head_mean"], wl["w_out"])


def _rope_tables(seq_len):
    pos = np.arange(seq_len)
    inv = ROPE_BASE ** (-np.arange(0, 16, 2, dtype=np.float64) / 16.0)
    cos = np.ones((seq_len, HEAD_PAD))
    sin_a = np.zeros((seq_len, HEAD_PAD))
    sin_b = np.zeros((seq_len, HEAD_PAD))
    for base, p in ((MLA_NOPE, pos // GRID_W), (MLA_NOPE + 16, pos % GRID_W)):
        ang = p[:, None].astype(np.float64) * inv[None, :]
        cos[:, base:base + 8] = np.cos(ang)
        cos[:, base + 8:base + 16] = np.cos(ang)
        sin_a[:, base:base + 8] = -np.sin(ang)
        sin_b[:, base + 8:base + 16] = np.sin(ang)
    return tuple(jnp.asarray(t, F32) for t in (cos, sin_a, sin_b))


def _odd_dft(seq_len):
    bk = min(seq_len, HY_BLOCK)
    k = np.arange(bk)[:, None]
    t = np.arange(bk)[None, :]

    def mat(shift):
        ang = (np.pi / (2 * bk)) * (((2 * k + 1) * (t + shift)) % (4 * bk))
        return np.concatenate([np.cos(ang), -np.sin(ang)], axis=0)

    fo = mat(0)
    fb = -mat(bk)
    fb[:, 0] = 0.0
    const = lambda a: jnp.asarray(a, F32).astype(BF16)
    return const(fo), const(fb), const(fo.T)


def _hyena_tables(seq_len):
    lag = np.arange(-seq_len, seq_len)
    pos = np.where(lag == -seq_len, 0, np.abs(lag)).astype(np.float64)
    t = pos / seq_len
    w = 2.0 * np.pi * pos / seq_len
    bands = np.linspace(1e-4, HY_BANDS - 1, HY_BANDS)
    feat = np.zeros((2 * seq_len, 128))
    feat[:, 0] = t
    feat[:, 1:1 + HY_BANDS] = np.cos(w[:, None] * bands)
    feat[:, 1 + HY_BANDS:HY_FEAT] = np.sin(w[:, None] * bands)
    deltas = np.linspace(math.log(1.0 / HY_TARGET) / HY_FAST_DECAY, math.log(1.0 / HY_TARGET) / HY_SLOW_DECAY,
                         D_BRANCH)
    win = np.exp(-t[:, None] * deltas[None, :]) + HY_SHIFT
    return jnp.asarray(feat, F32), jnp.asarray(win, F32)


def _pad_to(a, shape):
    return jnp.pad(a, [(0, s - d) for s, d in zip(shape, a.shape)])


def _layer_weights(l, p):
    z = lambda *s: jnp.zeros(s, F32)
    w_in = p["w_in"][l]
    col = lambda lo, hi: w_in[:, lo:hi]
    d = D_MODEL
    w_p = jnp.concatenate([
        col(0, 192), col(320, 352), col(2656, 2688), col(192, 320),
        col(352, 608), col(1376, 1632), col(1888, 2144), col(2688, 2944),
        col(608, 1376), col(1632, 1888),
        col(2144, 2272), col(2272, 2400), col(2400, 2656)], axis=1).astype(BF16)
    wl = {"w_in": w_p, "norm_w": p["norm_w"][l].reshape(1, d), "w_out": p["w_out"][l].astype(BF16)}
    wl["qa_norm"] = _pad_to(p["mla_qa_norm"][l].reshape(1, -1), (1, 256))
    w_uq = _pad_to(p["mla_w_uq"][l].reshape(MLA_Q_RANK, MLA_HEADS, MLA_QK), (256, MLA_HEADS, HEAD_PAD))
    wl["w_uq"] = w_uq.reshape(256, MLA_HEADS * HEAD_PAD).astype(BF16)
    wl["q_norm"] = _pad_to(p["mla_q_norm"][l].reshape(1, -1), (1, HEAD_PAD))
    wl["k_norm"] = _pad_to(p["mla_k_norm"][l].reshape(1, -1), (1, HEAD_PAD))
    wl["kva_norm"] = p["mla_kva_norm"][l].reshape(1, -1)
    w_ukv = p["mla_w_ukv"][l].reshape(MLA_KV_RANK, MLA_HEADS, MLA_NOPE + MLA_V)
    wl["w_uk"] = _pad_to(w_ukv[:, :, :MLA_NOPE], (MLA_KV_RANK, MLA_HEADS, HEAD_PAD)).reshape(MLA_KV_RANK, -1).astype(BF16)
    wl["w_uv"] = w_ukv[:, :, MLA_NOPE:].reshape(MLA_KV_RANK, MLA_HEADS * MLA_V).astype(BF16)
    wl["hy_conv_w"] = p["hy_conv_w"][l]
    wl["hy_conv_b"] = p["hy_conv_b"][l].reshape(1, -1)
    wl["hy_w1"] = _pad_to(p["hy_w1"][l], (128, 128))
    wl["hy_b1"] = _pad_to(p["hy_b1"][l].reshape(1, -1), (1, 128))
    wl["hy_f1"] = _pad_to(p["hy_freq1"][l].reshape(1, -1), (1, 128))
    wl["hy_w2"] = _pad_to(p["hy_w2"][l], (128, 128))
    wl["hy_b2"] = _pad_to(p["hy_b2"][l].reshape(1, -1), (1, 128))
    wl["hy_f2"] = _pad_to(p["hy_freq2"][l].reshape(1, -1), (1, 128))
    wl["hy_w3"] = _pad_to(p["hy_w3"][l], (128, 1024))
    wl["hy_bias"] = p["hy_bias"][l]
    flat = lambda a: a[l].reshape(2, 1, S5_N)
    wl["s5_are"], wl["s5_aim"] = flat(p["s5_a_re"]), flat(p["s5_a_im"])
    wl["s5_ldt"] = jnp.repeat(p["s5_log_dt"][l], S5_STATE, axis=-1).reshape(2, 1, S5_N)
    same_group = jnp.asarray((np.arange(D_BRANCH) // S5_GROUP)[:, None] == (np.arange(S5_N) // S5_STATE)[None, :])
    bd_b = lambda a: jnp.where(same_group, jnp.tile(
        a[l].transpose(0, 1, 3, 2).reshape(2, D_BRANCH, S5_STATE), (1, 1, S5_GROUPS)), 0.0)
    bd_c = lambda a: jnp.where(same_group.T, jnp.tile(
        a[l].transpose(0, 1, 3, 2).reshape(2, S5_N, S5_GROUP), (1, 1, S5_GROUPS)), 0.0)
    wl["s5_bre"], wl["s5_bim"] = bd_b(p["s5_b_re"]), bd_b(p["s5_b_im"])
    wl["s5_cre"], wl["s5_cim"] = bd_c(p["s5_c_re"]), bd_c(p["s5_c_im"])
    wl["s5_d"] = p["s5_d"][l].reshape(1, -1)
    wl["s5_glu_w"] = p["s5_glu_w"][l].astype(BF16)
    wl["s5_glu_b"] = p["s5_glu_b"][l].reshape(1, -1)
    gw = p["gla_gw"][l]
    dk = GLA_HEADS * GLA_DK
    wl["gla_gw"] = jnp.stack([_pad_to(jnp.pad(gw[i], ((GLA_G_LANE + GLA_RANK * i, 0), (0, 0))), (dk, dk))
                              for i in range(2)]).astype(BF16)
    wl["gla_gb"] = p["gla_gb"][l].reshape(2, 1, dk)
    wl["gla_norm"] = jnp.tile(p["gla_norm"][l], GLA_HEADS).reshape(1, -1)
    head = np.arange(D_BRANCH) // GLA_DV
    wl["head_mean"] = jnp.asarray((head[:, None] == head[None, :]) / GLA_DV, BF16)
    return wl


def _hyena_filters(wl, tabs):
    feat, win, fo, fb, _ = tabs
    kern_lin, nrm = _hy_mlp(feat, wl, win)
    return _hy_kspec(kern_lin, nrm, fo, fb)


def _trunk_layer(x, mod, mod_row, wl, batch, seq_len, hy_tabs, rope_tabs=None, ctx=None, layer=0, cache_bufs=None):
    n = batch * seq_len
    tm = 512
    mla_in, gates, hy_in, s5_in, gla_in = _inproj(x, mod, mod_row, wl["norm_w"], wl["w_in"], tm)

    if ctx is None:
        prev = None if cache_bufs is None else cache_bufs[:2]
        q, k, v, ckv, krope = _mla_prep(mla_in, wl, rope_tabs, seq_len, tm, True, (layer, prev))
    else:
        q, k, v = _mla_prep(mla_in, wl, rope_tabs, seq_len, tm, True)
        ckv = krope = None
    kv_parts = [(k, v, seq_len)]
    if ctx is not None:
        k_ctx, v_ctx = _mla_prep(ctx["mla"], wl, None, ctx["past"], 512, False)
        kv_parts = [(k_ctx, v_ctx, ctx["past"])] + kv_parts
    o_mla = _attention(q, kv_parts, batch, seq_len, 256)

    o_hy = _hyena(hy_in, wl, _hyena_filters(wl, hy_tabs), hy_tabs[2], hy_tabs[4], batch, seq_len)

    nseg = seq_len // S5_SEG
    nseq = batch * nseg
    u_seg = s5_in.reshape(nseq, S5_SEG, D_BRANCH)
    if ctx is None:
        hin = jnp.zeros((2, nseq, 2 * S5_N), F32)
    else:
        (fin,) = _s5_scan(u_seg, jnp.zeros((2, nseq, 2 * S5_N), F32), wl, False)
        hin = _s5_chain(fin, ctx["s5_h0"], wl, batch, nseg)
    y2, s5_fin = _s5_scan(u_seg, hin, wl, True)

    s0 = jnp.zeros((batch, 2, GLA_HEADS * GLA_DV, GLA_HEADS * GLA_DK), F32) if ctx is None else ctx["gla_s0"]
    fin = (0, None, 1) if ctx is not None else (layer, None if cache_bufs is None else cache_bufs[2], DEPTH)
    o_gla, gla_fin = _gla(gla_in, mla_in, wl, s0, batch, seq_len, fin)

    y = _outproj(x, mod, mod_row, gates, o_mla, o_hy, s5_in, y2.reshape(2, n, D_BRANCH), o_gla, wl, tm)
    return y, (ckv, krope, s5_fin, gla_fin)


def kernel(x_prompt, x_sample, c, cache_mla_ckv, cache_mla_krope, state_s5, state_gla, c_ctx, norm_w, ada_w, ada_b, w_in, w_out, mla_qa_norm, mla_kva_norm, mla_w_uq, mla_w_ukv, mla_q_norm, mla_k_norm, hy_conv_w, hy_conv_b, hy_w1, hy_b1, hy_freq1, hy_w2, hy_b2, hy_freq2, hy_w3, hy_bias, s5_a_re, s5_a_im, s5_log_dt, s5_b_re, s5_b_im, s5_c_re, s5_c_im, s5_d, s5_glu_w, s5_glu_b, gla_gw, gla_gb, gla_norm):
    params = dict(norm_w=norm_w, w_in=w_in, w_out=w_out, mla_qa_norm=mla_qa_norm, mla_kva_norm=mla_kva_norm,
                  mla_w_uq=mla_w_uq, mla_w_ukv=mla_w_ukv, mla_q_norm=mla_q_norm, mla_k_norm=mla_k_norm,
                  hy_conv_w=hy_conv_w, hy_conv_b=hy_conv_b, hy_w1=hy_w1, hy_b1=hy_b1, hy_freq1=hy_freq1,
                  hy_w2=hy_w2, hy_b2=hy_b2, hy_freq2=hy_freq2, hy_w3=hy_w3, hy_bias=hy_bias,
                  s5_a_re=s5_a_re, s5_a_im=s5_a_im, s5_log_dt=s5_log_dt, s5_b_re=s5_b_re, s5_b_im=s5_b_im,
                  s5_c_re=s5_c_re, s5_c_im=s5_c_im, s5_d=s5_d, s5_glu_w=s5_glu_w, s5_glu_b=s5_glu_b,
                  gla_gw=gla_gw, gla_gb=gla_gb, gla_norm=gla_norm)
    bp, lp, d = x_prompt.shape
    bs, ls, _ = x_sample.shape
    past = cache_mla_ckv.shape[2]
    n_s5 = S5_N

    conds = jnp.concatenate([c_ctx[None, :], c, jnp.zeros((8 - 1 - bs, d), F32)], axis=0)
    mods = _modulation(conds, ada_w, ada_b).reshape(DEPTH, 8, 3, d)

    tabs_p = _hyena_tables(lp) + _odd_dft(lp)
    tabs_s = _hyena_tables(ls) + _odd_dft(ls)
    rope_tabs = _rope_tables(ls)
    tm_s = 512
    nseg = ls // S5_SEG

    y_p = x_prompt.reshape(bp * lp, d)
    y_s = x_sample.reshape(bs * ls, d)
    s5_l = []
    cache_bufs = None
    for l in range(DEPTH):
        wl = _layer_weights(l, params)
        y_p, (ckv, krope, s5_fin, gla_fin) = _trunk_layer(y_p, mods[l], lambda i: 0, wl, bp, lp, tabs_p,
                                                          layer=l, cache_bufs=cache_bufs)
        cache_bufs = (ckv, krope, gla_fin)
        s5_l.append(jnp.stack([s5_fin[:, :, :n_s5], s5_fin[:, :, n_s5:]], axis=-1)
                    .reshape(2, bp, S5_GROUPS, S5_STATE, 2).transpose(1, 0, 2, 3, 4))

        mla_ctx = jnp.concatenate([cache_mla_ckv[:, l], jnp.zeros((bs, past, 64), F32), cache_mla_krope[:, l],
                                   jnp.zeros((bs, past, 32), F32)], axis=-1).reshape(bs * past, 256)
        st = state_s5[:, l]
        h0 = jnp.concatenate([st[..., 0].reshape(bs, 2, n_s5), st[..., 1].reshape(bs, 2, n_s5)], axis=-1)
        h0 = h0.transpose(1, 0, 2)
        h0rows = jnp.zeros((2, nseg * bs, 2 * n_s5), F32)
        h0rows = h0rows.at[0, 0::nseg].set(h0[0]).at[1, nseg - 1::nseg].set(h0[1])
        eye_h = jnp.eye(GLA_HEADS, dtype=F32)
        gla_s0 = jnp.einsum("bdhke,hg->bdhegk", state_gla[:, l], eye_h).reshape(
            bs, 2, GLA_HEADS * GLA_DV, GLA_HEADS * GLA_DK)
        ctx = {"mla": mla_ctx, "past": past, "s5_h0": h0rows, "gla_s0": gla_s0}
        y_s, _ = _trunk_layer(y_s, mods[l], lambda i: 1 + (i * tm_s) // ls, wl, bs, ls, tabs_s, rope_tabs, ctx)

    return (y_p.reshape(bp, lp, d), y_s.reshape(bs, ls, d),
            cache_bufs[0], cache_bufs[1], jnp.stack(s5_l, axis=1), cache_bufs[2])
```

```python
import functools
import math

import numpy as np
import jax
import jax.numpy as jnp
from jax import lax
from jax.experimental import pallas as pl
from jax.experimental.pallas import tpu as pltpu

F32 = jnp.float32
BF16 = jnp.bfloat16

D_MODEL = 1024
DEPTH = 2
GRID_W = 64
D_BRANCH = 256
EPS = 1e-6

MLA_HEADS = 4
MLA_Q_RANK = 192
MLA_KV_RANK = 128
MLA_NOPE = 64
MLA_ROPE = 32
MLA_QK = 96
MLA_V = 64
ROPE_BASE = 10000.0
HEAD_PAD = 128

HY_BANDS = 16
HY_FEAT = 33
HY_HIDDEN = 64
HY_SHIFT = 0.05
HY_FAST_DECAY = 0.3
HY_SLOW_DECAY = 1.5
HY_TARGET = 1e-2
HY_BLOCK = 512

S5_GROUP = 16
S5_GROUPS = 16
S5_STATE = 64
S5_N = S5_GROUPS * S5_STATE
S5_ROWS = 512
S5_SEG = 256

GLA_HEADS = 4
GLA_DK = 32
GLA_DV = 64
GLA_RANK = 16
GLA_TAU = 16.0
GLA_CHUNK = 64
GLA_SUPER = 256

SEG_MLA = (0, 384)
SEG_GATE = (384, 1408)
SEG_HY = (1408, 2176)
SEG_S5 = (2176, 2432)
SEG_GLA = (2432, 2944)
N_PROJ = 2944
GLA_G_LANE = 96

VMEM_LIMIT = 48 * 1024 * 1024


def _cp(*sem):
    return pltpu.CompilerParams(dimension_semantics=sem, vmem_limit_bytes=VMEM_LIMIT)


def _dot(a, b):
    return jnp.dot(a, b, preferred_element_type=F32)


def _dot_nt(a, b):
    return lax.dot_general(a, b, (((1,), (1,)), ((), ())), preferred_element_type=F32)


def _dot_tn(a, b):
    return lax.dot_general(a, b, (((0,), (0,)), ((), ())), preferred_element_type=F32)


def _split2(x):
    hi = x.astype(BF16)
    lo = (x - hi.astype(F32)).astype(BF16)
    return hi, lo


def _split3(x):
    h1 = x.astype(BF16)
    r1 = x - h1.astype(F32)
    h2 = r1.astype(BF16)
    h3 = (r1 - h2.astype(F32)).astype(BF16)
    return h1, h2, h3


def _dot3(a, b):
    a1, a2 = _split2(a)
    b1, b2 = _split2(b)
    return _dot(a1, b1) + (_dot(a1, b2) + _dot(a2, b1))


def _silu(z):
    return z / (1.0 + jnp.exp(-z))


def _mod_kernel(c_ref, w_ref, b_ref, o_ref):
    s = _silu(c_ref[...])
    o_ref[0] = _dot(s.astype(BF16), w_ref[0].astype(BF16)) + b_ref[0]


def _modulation(conds, ada_w, ada_b):
    d = D_MODEL
    return pl.pallas_call(
        _mod_kernel,
        grid=(DEPTH, 3),
        in_specs=[pl.BlockSpec((8, d), lambda l, j: (0, 0)),
                  pl.BlockSpec((1, d, d), lambda l, j: (l, 0, j)),
                  pl.BlockSpec((1, 1, d), lambda l, j: (l, 0, j))],
        out_specs=pl.BlockSpec((1, 8, d), lambda l, j: (l, 0, j)),
        out_shape=jax.ShapeDtypeStruct((DEPTH, 8, 3 * d), F32),
        compiler_params=_cp("arbitrary", "arbitrary"),
        name="modulation",
    )(conds, ada_w, ada_b.reshape(DEPTH, 1, 3 * d))


def _inproj_kernel(x_ref, mod_ref, nw_ref, w_ref, o_mla, o_g, o_hy, o_s5, o_gla):
    x = x_ref[...]
    ms = jnp.mean(x * x, axis=-1, keepdims=True)
    y = x * lax.rsqrt(ms + EPS) * nw_ref[...]
    h = (y * (1.0 + mod_ref[0, 1:2, :]) + mod_ref[0, 0:1, :]).astype(BF16)
    for o, (lo, hi) in ((o_mla, SEG_MLA), (o_g, SEG_GATE), (o_hy, SEG_HY), (o_s5, SEG_S5), (o_gla, SEG_GLA)):
        o[...] = _dot(h, w_ref[:, lo:hi]).astype(o.dtype)


def _inproj(x, mod, mod_row, norm_w, w_p, tm):
    n, d = x.shape
    widths = [hi - lo for lo, hi in (SEG_MLA, SEG_GATE, SEG_HY, SEG_S5, SEG_GLA)]
    dtypes = [F32, BF16, F32, F32, F32]
    return pl.pallas_call(
        _inproj_kernel,
        grid=(n // tm,),
        in_specs=[pl.BlockSpec((tm, d), lambda i: (i, 0)),
                  pl.BlockSpec((1, 3, d), lambda i: (mod_row(i), 0, 0)),
                  pl.BlockSpec((1, d), lambda i: (0, 0)),
                  pl.BlockSpec((d, N_PROJ), lambda i: (0, 0))],
        out_specs=[pl.BlockSpec((tm, w), lambda i: (i, 0)) for w in widths],
        out_shape=[jax.ShapeDtypeStruct((n, w), t) for w, t in zip(widths, dtypes)],
        compiler_params=_cp("arbitrary"),
        name="inproj",
    )(x, mod, norm_w, w_p)


def _head_norm(xh, w):
    ms = jnp.sum(xh * xh, axis=-1, keepdims=True) * (1.0 / MLA_QK)
    return xh * lax.rsqrt(ms + EPS) * w


def _rope(xh, cos, sin_a, sin_b):
    return xh * cos + pltpu.roll(xh, HEAD_PAD - 8, 1) * sin_a + pltpu.roll(xh, 8, 1) * sin_b


def _mla_prep_kernel(has_q, rope, cache_seqs, n_aliased, *refs):
    refs = list(refs)
    m_ref = refs.pop(0)
    if has_q:
        qan_ref, wuq_ref, qn_ref, kvn_ref = refs[:4]
        refs = refs[4:]
    wuk_ref, wuv_ref, kn_ref = refs[:3]
    refs = refs[3:]
    if rope:
        cos_ref, sa_ref, sb_ref = refs[:3]
        refs = refs[3:]
        cos, sa, sb = cos_ref[...], sa_ref[...], sb_ref[...]
    refs = refs[n_aliased:]
    if has_q:
        q_ref = refs.pop(0)
    k_ref, v_ref = refs[:2]
    if cache_seqs:
        ckv_ref, kro_ref = refs[2:]
    m = m_ref[...]
    if has_q:
        lane = lax.broadcasted_iota(jnp.int32, (1, HEAD_PAD), 1)
        mixed = m[:, 128:256]
        cq = jnp.concatenate([m[:, 0:128], jnp.where(lane < MLA_NOPE, mixed, 0.0)], axis=1)
        ms = jnp.sum(cq * cq, axis=-1, keepdims=True) * (1.0 / MLA_Q_RANK)
        cqn = cq * lax.rsqrt(ms + EPS) * qan_ref[...]
        q = _dot(cqn.astype(BF16), wuq_ref[...])
        ckv = m[:, 256:384]
        ckvn = ckv * lax.rsqrt(jnp.mean(ckv * ckv, axis=-1, keepdims=True) + EPS) * kvn_ref[...]
        kr = jnp.where(jnp.logical_and(lane >= MLA_NOPE, lane < MLA_NOPE + MLA_ROPE), mixed, 0.0)
        if cache_seqs:
            seq_len = ckv_ref.shape[2]
            for s in range(cache_seqs):
                ckv_ref[s, 0] = ckvn[s * seq_len:(s + 1) * seq_len]
                kro_ref[s, 0] = kr[s * seq_len:(s + 1) * seq_len, MLA_NOPE:MLA_NOPE + MLA_ROPE]
    else:
        ckvn = m[:, 0:128]
        kr = m[:, 128:256]
    cb = ckvn.astype(BF16)
    kup = _dot(cb, wuk_ref[...])
    v_ref[...] = _dot(cb, wuv_ref[...]).astype(BF16)
    for h in range(MLA_HEADS):
        sl = slice(HEAD_PAD * h, HEAD_PAD * (h + 1))
        kh = _head_norm(kup[:, sl] + kr, kn_ref[...])
        if rope:
            kh = _rope(kh, cos, sa, sb)
        k_ref[:, sl] = kh.astype(BF16)
        if has_q:
            qh = _head_norm(q[:, sl], qn_ref[...])
            if rope:
                qh = _rope(qh, cos, sa, sb)
            q_ref[:, sl] = (qh * (MLA_QK ** -0.5)).astype(BF16)


def _mla_prep(m, wl, rope_tabs, seq_len, tm, has_q, cache=None):
    n, wm = m.shape
    rope = rope_tabs is not None
    full = lambda shape: pl.BlockSpec(shape, lambda i: (0,) * len(shape))
    args, specs = [m], [pl.BlockSpec((tm, wm), lambda i: (i, 0))]
    if has_q:
        args += [wl["qa_norm"], wl["w_uq"], wl["q_norm"], wl["kva_norm"]]
        specs += [full((1, 256)), full((256, 512)), full((1, 128)), full((1, 128))]
    args += [wl["w_uk"], wl["w_uv"], wl["k_norm"]]
    specs += [full((128, 512)), full((128, 256)), full((1, 128))]
    if rope:
        nt = seq_len // tm
        args += list(rope_tabs)
        specs += [pl.BlockSpec((tm, HEAD_PAD), lambda i: (i % nt, 0))] * 3
    row = lambda w: pl.BlockSpec((tm, w), lambda i: (i, 0))
    out_specs = [row(512), row(256)]
    out_shape = [jax.ShapeDtypeStruct((n, 512), BF16), jax.ShapeDtypeStruct((n, 256), BF16)]
    aliases = {}
    nseq = 0
    if has_q:
        out_specs = [row(512)] + out_specs
        out_shape = [jax.ShapeDtypeStruct((n, 512), BF16)] + out_shape
    if cache is not None:
        layer, prev = cache
        nseq = tm // seq_len
        for w in (MLA_KV_RANK, MLA_ROPE):
            out_specs.append(pl.BlockSpec((nseq, 1, seq_len, w), lambda i: (i, layer, 0, 0)))
            out_shape.append(jax.ShapeDtypeStruct((n // seq_len, DEPTH, seq_len, w), F32))
        if prev is not None:
            for k, buf in enumerate(prev):
                aliases[len(args)] = len(out_shape) - 2 + k
                args.append(buf)
                specs.append(pl.BlockSpec(memory_space=pl.ANY))
    return pl.pallas_call(
        functools.partial(_mla_prep_kernel, has_q, rope, nseq, len(aliases)),
        grid=(n // tm,),
        in_specs=specs, out_specs=out_specs, out_shape=out_shape,
        input_output_aliases=aliases,
        compiler_params=_cp("arbitrary"),
        name="mla_prep",
    )(*args)


def _attn_kernel(nparts, nseq, q_ref, *refs):
    kv = [(refs[2 * i], refs[2 * i + 1]) for i in range(nparts)]
    o_ref = refs[2 * nparts]
    tq = q_ref.shape[0] // nseq
    low = lax.broadcasted_iota(jnp.int32, (1, HEAD_PAD), 1) < MLA_V
    units = [(s, h) for s in range(nseq) for h in range(MLA_HEADS)]

    def keys(ref, s):
        lk = ref.shape[0] // nseq
        return slice(s * lk, (s + 1) * lk)

    def scores(s, h):
        sl = slice(HEAD_PAD * h, HEAD_PAD * (h + 1))
        return [_dot_nt(q_ref[s * tq:(s + 1) * tq, sl], k_ref[keys(k_ref, s), sl]) for k_ref, _ in kv]

    s_next = scores(*units[0])
    acc = None
    for n, (s, h) in enumerate(units):
        pair, j = divmod(h, 2)
        sc = s_next
        if n + 1 < len(units):
            s_next = scores(*units[n + 1])
        if j == 0:
            v_half = []
            for _, v_ref in kv:
                vp = v_ref[keys(v_ref, s), HEAD_PAD * pair:HEAD_PAD * (pair + 1)]
                zero = jnp.zeros_like(vp)
                v_half.append((jnp.where(low, vp, zero), jnp.where(low, zero, vp)))
        m = functools.reduce(jnp.maximum, [jnp.max(x, axis=-1, keepdims=True) for x in sc])
        p = [jnp.exp(x - m) for x in sc]
        den = functools.reduce(jnp.add, [jnp.sum(x, axis=-1, keepdims=True) for x in p])
        num = functools.reduce(jnp.add, [_dot(x.astype(BF16), vh[j]) for x, vh in zip(p, v_half)])
        o = num / den
        acc = o if j == 0 else acc + o
        if j == 1:
            o_ref[s * tq:(s + 1) * tq, HEAD_PAD * pair:HEAD_PAD * (pair + 1)] = acc.astype(BF16)


def _attention(q, kv_parts, batch, lq, tq):
    nq = lq // tq
    nseq = max(1, min(batch, 1024 // lq)) if nq == 1 else 1
    args, specs = [q], [pl.BlockSpec((nseq * tq, 512), lambda b, i: (b * nq + i, 0))]
    for k, v, lk in kv_parts:
        args += [k, v]
        specs += [pl.BlockSpec((nseq * lk, 512), lambda b, i: (b, 0)),
                  pl.BlockSpec((nseq * lk, 256), lambda b, i: (b, 0))]
    return pl.pallas_call(
        functools.partial(_attn_kernel, len(kv_parts), nseq),
        grid=(batch // nseq, nq),
        in_specs=specs,
        out_specs=pl.BlockSpec((nseq * tq, 256), lambda b, i: (b * nq + i, 0)),
        out_shape=jax.ShapeDtypeStruct((batch * lq, 256), BF16),
        compiler_params=_cp("arbitrary", "arbitrary"),
        name="attention",
    )(*args)


def _hyena_kernel(seq_len, bk, nseq, x_ref, cw_ref, cb_ref, fo_ref, go_ref, k_ref, bias_ref, o_ref,
                  u_sc, y_sc, z_sc):
    c = D_BRANCH
    n = nseq * seq_len
    nblk = seq_len // bk
    pos = jnp.bitwise_and(lax.broadcasted_iota(jnp.int32, (n, 1), 0), seq_len - 1)
    first, last = pos == 0, pos == seq_len - 1

    def short_conv(g):
        cols = slice(g * c, (g + 1) * c)
        x = x_ref[:, cols]
        xm = jnp.where(first, 0.0, pltpu.roll(x, 1, 0))
        xp = jnp.where(last, 0.0, pltpu.roll(x, n - 1, 0))
        return cw_ref[0:1, cols] * xm + cw_ref[1:2, cols] * x + cw_ref[2:3, cols] * xp + cb_ref[:, cols]

    fo, go = fo_ref[...], go_ref[...]

    def long_conv(s, v, order, emit):
        cols = slice(order * c, (order + 1) * c)
        for j in range(nblk):
            u_sc[s, j] = _dot(fo, v[j * bk:(j + 1) * bk].astype(BF16))
        yield
        rc = 32
        for i in range(nblk):
            def mix(r, carry, i=i):
                top = pl.ds(pl.multiple_of(r * rc, rc), rc)
                bot = pl.ds(pl.multiple_of(bk + r * rc, rc), rc)
                at = ab = None
                for j in range(nblk):
                    q = i - j + nblk - 1
                    kt, kb = k_ref[q, top, cols], k_ref[q, bot, cols]
                    ut, ub = u_sc[s, j, top, :], u_sc[s, j, bot, :]
                    pt, pb = ut * kt - ub * kb, ut * kb + ub * kt
                    at, ab = (pt, pb) if at is None else (at + pt, ab + pb)
                z_sc[s, top, :] = at.astype(BF16)
                z_sc[s, bot, :] = ab.astype(BF16)
                return carry

            lax.fori_loop(0, bk // rc, mix, 0, unroll=True if nblk == 1 else 2)
            y = _dot(go, z_sc[s])
            yield
            emit(i, y)

    v_all, x1_all, x2_all = short_conv(0), short_conv(1), short_conv(2)

    def sequence(s):
        base = s * seq_len
        v = v_all[base:base + seq_len]

        def emit1(i, y):
            r = slice(i * bk, (i + 1) * bk)
            y_sc[s, r, :] = x1_all[base + i * bk:base + (i + 1) * bk] * (y + bias_ref[0:1, :] * v[r])

        yield from long_conv(s, v, 0, emit1)
        y1 = y_sc[s]

        def emit2(i, y):
            r = slice(i * bk, (i + 1) * bk)
            o_ref[base + i * bk:base + (i + 1) * bk, :] = (
                x2_all[base + i * bk:base + (i + 1) * bk] * (y + bias_ref[1:2, :] * y1[r])).astype(BF16)

        yield from long_conv(s, y1, 1, emit2)

    live = [sequence(s) for s in range(nseq)]
    while live:
        for g in list(live):
            if next(g, StopIteration) is StopIteration:
                live.remove(g)


def _hyena(x, wl, kspec, fo, go, batch, seq_len):
    c = D_BRANCH
    bk = fo.shape[1]
    nseq = max(1, min(batch, 1024 // seq_len))
    rows = nseq * seq_len
    nblk = seq_len // bk
    full = lambda a: pl.BlockSpec(a.shape, lambda i: (0,) * a.ndim)
    return pl.pallas_call(
        functools.partial(_hyena_kernel, seq_len, bk, nseq),
        grid=(batch // nseq,),
        in_specs=[pl.BlockSpec((rows, 3 * c), lambda i: (i, 0)),
                  full(wl["hy_conv_w"]), full(wl["hy_conv_b"]), full(fo), full(go),
                  pl.BlockSpec(kspec.shape, lambda i: (0, 0, 0), pipeline_mode=pl.Buffered(1)),
                  full(wl["hy_bias"])],
        out_specs=pl.BlockSpec((rows, c), lambda i: (i, 0)),
        out_shape=jax.ShapeDtypeStruct((batch * seq_len, c), BF16),
        scratch_shapes=[pltpu.VMEM((nseq, nblk, 2 * bk, c), F32), pltpu.VMEM((nseq, seq_len, c), F32),
                        pltpu.VMEM((nseq, 2 * bk, c), BF16)],
        compiler_params=pltpu.CompilerParams(dimension_semantics=("arbitrary",), vmem_limit_bytes=56 * 1024 * 1024),
        name="hyena",
    )(x, wl["hy_conv_w"], wl["hy_conv_b"], fo, go, kspec, wl["hy_bias"])


def _hy_mlp_kernel(feat_ref, w1_ref, b1_ref, f1_ref, w2_ref, b2_ref, f2_ref, w3_ref, win_ref,
                   kern_ref, nrm_ref):
    i = pl.program_id(0)
    tl = win_ref.shape[0]
    h = jnp.sin(f1_ref[...] * (_dot3(feat_ref[...], w1_ref[...]) + b1_ref[...]))
    h = jnp.sin(f2_ref[...] * (_dot3(h, w2_ref[...]) + b2_ref[...]))
    filt = _dot3(h, w3_ref[...])
    filt = jnp.concatenate([filt[:, 0:512], filt[:, 512:1024]], axis=0)
    win = win_ref[...]
    row0 = (lax.broadcasted_iota(jnp.int32, (tl, 1), 0) + i * tl) == 0
    filt = jnp.where(row0, 0.0, filt * jnp.concatenate([win, win], axis=1))
    kern_ref[...] = filt
    part = jnp.sum(jnp.abs(filt), axis=0, keepdims=True)

    @pl.when(i == 0)
    def _():
        nrm_ref[...] = jnp.zeros_like(nrm_ref)
    nrm_ref[...] += jnp.broadcast_to(part, nrm_ref.shape)


def _hy_mlp(feat, wl, win):
    rows = win.shape[0]
    tl = 256
    nl = rows // (2 * tl)
    full = lambda shape: pl.BlockSpec(shape, lambda i: (0,) * len(shape))
    return pl.pallas_call(
        _hy_mlp_kernel,
        grid=(rows // tl,),
        in_specs=[pl.BlockSpec((tl // 2, 128), lambda i: (i, 0)),
                  full((128, 128)), full((1, 128)), full((1, 128)),
                  full((128, 128)), full((1, 128)), full((1, 128)),
                  pl.BlockSpec((128, 1024), lambda i: (0, jnp.where(i < nl, 1, 0))),
                  pl.BlockSpec((tl, 256), lambda i: (i, 0))],
        out_specs=[pl.BlockSpec((tl, 512), lambda i: (i, 0)), full((8, 512))],
        out_shape=[jax.ShapeDtypeStruct((rows, 512), F32), jax.ShapeDtypeStruct((8, 512), F32)],
        compiler_params=_cp("arbitrary"),
        name="hy_mlp",
    )(feat, wl["hy_w1"], wl["hy_b1"], wl["hy_f1"], wl["hy_w2"], wl["hy_b2"], wl["hy_f2"], wl["hy_w3"], win)


def _hy_kspec_kernel(lo_ref, hi_ref, fo_ref, fb_ref, n_ref, o_ref):
    bk = lo_ref.shape[0]
    k = _dot(fo_ref[...], hi_ref[...].astype(BF16)) + _dot(fb_ref[...], lo_ref[...].astype(BF16))
    o_ref[0] = k * ((1.0 / bk) / n_ref[0:1, :])


def _hy_kspec(kern_lin, nrm, fo, fb):
    n2, bk = fo.shape
    nq = kern_lin.shape[0] // bk - 1
    full = lambda a: pl.BlockSpec(a.shape, lambda q: (0,) * a.ndim)
    return pl.pallas_call(
        _hy_kspec_kernel,
        grid=(nq,),
        in_specs=[pl.BlockSpec((bk, 512), lambda q: (q, 0)),
                  pl.BlockSpec((bk, 512), lambda q: (q + 1, 0)),
                  full(fo), full(fb), full(nrm)],
        out_specs=pl.BlockSpec((1, n2, 512), lambda q: (q, 0, 0)),
        out_shape=jax.ShapeDtypeStruct((nq, n2, 512), F32),
        compiler_params=_cp("arbitrary"),
        name="hy_kspec",
    )(kern_lin, kern_lin, fo, fb, nrm)


def _s5_discretise(are_ref, aim_ref, ldt_ref):
    ar = jnp.minimum(are_ref[0], -1e-4)
    ai = aim_ref[0]
    dt = jnp.exp(ldt_ref[0])
    e = jnp.exp(ar * dt)
    return ar, ai, e * jnp.cos(ai * dt), e * jnp.sin(ai * dt)


def _s5_scan_kernel(nseq, emit_y, u_ref, hin_ref, are_ref, aim_ref, ldt_ref, bre_ref, bim_ref, *rest):
    if emit_y:
        cre_ref, cim_ref, y_ref, hfin_ref, wb_sc, ab_sc, s_sc, hc_sc, perm_sc, wc_sc = rest
    else:
        hfin_ref, wb_sc, ab_sc, s_sc, hc_sc, perm_sc = rest
    d = pl.program_id(0)
    c = pl.program_id(1)
    n = S5_N

    @pl.when(c == 0)
    def _():
        ar, ai, abr, abi = _s5_discretise(are_ref, aim_ref, ldt_ref)
        ab_sc[0:1, :] = abr
        ab_sc[1:2, :] = abi
        den = 1.0 / (ar * ar + ai * ai)
        cr = ((abr - 1.0) * ar + abi * ai) * den
        ci = (abi * ar - (abr - 1.0) * ai) * den
        bre, bim = bre_ref[0], bim_ref[0]
        wb_sc[:, 0:n] = (cr * bre - ci * bim).astype(BF16)
        wb_sc[:, n:2 * n] = (cr * bim + ci * bre).astype(BF16)
        if emit_y:
            wc_sc[0:n, :] = cre_ref[0].astype(BF16)
            wc_sc[n:2 * n, :] = (-cim_ref[0]).astype(BF16)
        hc_sc[...] = hin_ref[0]

    steps = u_ref.shape[1]
    rows_c = nseq * steps

    @pl.when(c == 0)
    def _():
        i = lax.broadcasted_iota(jnp.int32, (rows_c, rows_c), 0)
        j = lax.broadcasted_iota(jnp.int32, (rows_c, rows_c), 1)
        p = lax.shift_right_logical(i, int(math.log2(nseq)))
        step = p + d * (steps - 1 - 2 * p)
        src = jnp.bitwise_and(i, nseq - 1) * steps + step
        perm_sc[...] = jnp.where(j == src, 1.0, 0.0).astype(BF16)

    lhs = _dot(perm_sc[...], u_ref[...].reshape(rows_c, D_BRANCH).astype(BF16)).astype(BF16)
    lb = 256
    y = None
    for j in range(n // lb):
        lr = slice(lb * j, lb * (j + 1))
        li = slice(n + lb * j, n + lb * (j + 1))
        bur = _dot(lhs, wb_sc[:, lr])
        bui = _dot(lhs, wb_sc[:, li])
        abr = ab_sc[0:1, lr]
        abi = ab_sc[1:2, lr]
        hr, hi = hc_sc[:, lr], hc_sc[:, li]
        for p in range(steps):
            rows = slice(p * nseq, (p + 1) * nseq)
            hr, hi = abr * hr - abi * hi + bur[rows], abr * hi + abi * hr + bui[rows]
            if emit_y:
                s_sc[rows, lr] = hr.astype(BF16)
                s_sc[rows, li] = hi.astype(BF16)
        hc_sc[:, lr] = hr
        hc_sc[:, li] = hi
        if emit_y:
            yj = _dot(s_sc[:, lr], wc_sc[lr, :]) + _dot(s_sc[:, li], wc_sc[li, :])
            y = yj if y is None else y + yj

    if emit_y:
        @pl.when(d == 0)
        def _():
            for p in range(steps):
                y_ref[0, :, p, :] = y[p * nseq:(p + 1) * nseq]

        @pl.when(d == 1)
        def _():
            for p in range(steps):
                y_ref[0, :, steps - 1 - p, :] = y[p * nseq:(p + 1) * nseq]

    @pl.when(c == pl.num_programs(1) - 1)
    def _():
        hfin_ref[0] = hc_sc[...]


def _s5_scan(u, hin, wl, emit_y):
    nseq, nstep, _ = u.shape
    steps = S5_ROWS // nseq
    nc = nstep // steps
    n = S5_N
    chunk = lambda d, c: c + d * (nc - 1 - 2 * c)
    per_dir = lambda shape: pl.BlockSpec((1,) + shape, lambda d, c: (d,) + (0,) * len(shape))
    args = [u, hin, wl["s5_are"], wl["s5_aim"], wl["s5_ldt"], wl["s5_bre"], wl["s5_bim"]]
    specs = [pl.BlockSpec((nseq, steps, D_BRANCH), lambda d, c: (0, chunk(d, c), 0)),
             per_dir((nseq, 2 * n)), per_dir((1, n)), per_dir((1, n)), per_dir((1, n)),
             per_dir((D_BRANCH, n)), per_dir((D_BRANCH, n))]
    out_specs = [per_dir((nseq, 2 * n))]
    out_shape = [jax.ShapeDtypeStruct((2, nseq, 2 * n), F32)]
    scratch = [pltpu.VMEM((D_BRANCH, 2 * n), BF16), pltpu.VMEM((8, n), F32),
               pltpu.VMEM((S5_ROWS, 2 * n), BF16), pltpu.VMEM((nseq, 2 * n), F32),
               pltpu.VMEM((S5_ROWS, S5_ROWS), BF16)]
    if emit_y:
        args += [wl["s5_cre"], wl["s5_cim"]]
        specs += [per_dir((n, D_BRANCH)), per_dir((n, D_BRANCH))]
        out_specs = [pl.BlockSpec((1, nseq, steps, D_BRANCH), lambda d, c: (d, 0, chunk(d, c), 0))] + out_specs
        out_shape = [jax.ShapeDtypeStruct((2, nseq, nstep, D_BRANCH), F32)] + out_shape
        scratch += [pltpu.VMEM((2 * n, D_BRANCH), BF16)]
    return pl.pallas_call(
        functools.partial(_s5_scan_kernel, nseq, emit_y),
        grid=(2, nc),
        in_specs=specs, out_specs=out_specs, out_shape=out_shape, scratch_shapes=scratch,
        compiler_params=_cp("arbitrary", "arbitrary"),
        name="s5_scan" if emit_y else "s5_scan_finals",
    )(*args)


def _s5_chain_kernel(batch, nseg, f_ref, h0_ref, are_ref, aim_ref, ldt_ref, o_ref):
    d = pl.program_id(0)
    n = S5_N
    _, _, pr, pi = _s5_discretise(are_ref, aim_ref, ldt_ref)
    for _ in range(int(math.log2(S5_SEG))):
        pr, pi = pr * pr - pi * pi, 2.0 * pr * pi
    f = f_ref[0]
    fr, fi = f[:, 0:n], f[:, n:2 * n]
    h0 = h0_ref[0]
    h0r, h0i = h0[:, 0:n], h0[:, n:2 * n]
    nrow = batch * nseg
    seg = jnp.bitwise_and(lax.broadcasted_iota(jnp.int32, (nrow, 1), 0), nseg - 1)

    def run(shift, keep):
        xr, xi = h0r, h0i
        for _ in range(nseg - 1):
            zr = fr + pr * xr - pi * xi
            zi = fi + pr * xi + pi * xr
            xr = h0r + jnp.where(keep, pltpu.roll(zr, shift, 0), 0.0)
            xi = h0i + jnp.where(keep, pltpu.roll(zi, shift, 0), 0.0)
        o_ref[0, :, 0:n] = xr
        o_ref[0, :, n:2 * n] = xi

    @pl.when(d == 0)
    def _():
        run(1, seg != 0)

    @pl.when(d == 1)
    def _():
        run(nrow - 1, seg != nseg - 1)


def _s5_chain(fin, h0rows, wl, batch, nseg):
    nrow = batch * nseg
    n = S5_N
    per_dir = lambda shape: pl.BlockSpec((1,) + shape, lambda d: (d,) + (0,) * len(shape))
    return pl.pallas_call(
        functools.partial(_s5_chain_kernel, batch, nseg),
        grid=(2,),
        in_specs=[per_dir((nrow, 2 * n)), per_dir((nrow, 2 * n)), per_dir((1, n)), per_dir((1, n)), per_dir((1, n))],
        out_specs=per_dir((nrow, 2 * n)),
        out_shape=jax.ShapeDtypeStruct((2, nrow, 2 * n), F32),
        compiler_params=_cp("arbitrary"),
        name="s5_chain",
    )(fin, h0rows, wl["s5_are"], wl["s5_aim"], wl["s5_ldt"])


def _gla_kernel(seq_len, nb, n_aliased, q_ref, k_ref, v_ref, g_ref, gw_ref, gb_ref, s0_ref, *rest):
    o_ref, sfin_ref, qe_sc, upd_sc, dec_sc, sall_sc, lhs_sc, kt_sc, la_sc, oi_sc = rest[n_aliased:]
    d = pl.program_id(1)
    sign = 1 - 2 * d
    ck, sup = GLA_CHUNK, GLA_SUPER
    cps = sup // ck
    nsup, nchunk = seq_len // sup, seq_len // ck
    dk, dv = GLA_HEADS * GLA_DK, GLA_HEADS * GLA_DV
    r = lax.broadcasted_iota(jnp.int32, (sup, sup), 0)
    s = lax.broadcasted_iota(jnp.int32, (sup, sup), 1)
    same = lax.shift_right_logical(r, 6) == lax.shift_right_logical(s, 6)
    tri = jnp.logical_and(same, (s - r) * sign <= 0)
    cum_lhs = jnp.where(tri, 1.0, 0.0).astype(BF16)
    pos = jnp.bitwise_and(lax.broadcasted_iota(jnp.int32, (ck, 1), 0), ck - 1)
    is_last = pos == (ck - 1) * (1 - d)
    row_chunk = lax.shift_right_logical(lax.broadcasted_iota(jnp.int32, (sup, 1), 0), 6)
    head_k = lax.shift_right_logical(lax.broadcasted_iota(jnp.int32, (1, dk), 1), 5)
    head_v = lax.shift_right_logical(lax.broadcasted_iota(jnp.int32, (1, dv), 1), 6)
    blockdiag = lax.shift_right_logical(lax.broadcasted_iota(jnp.int32, (dv, 1), 0), 6) == head_k

    def group_rows(u):
        return pl.ds(u * sup, sup) if isinstance(u, int) else pl.ds(pl.multiple_of(u * sup, sup), sup)

    def stage_a(u, slot):
        rows = group_rows(u)
        q = q_ref[rows, :] * (GLA_DK ** -0.5)
        k = k_ref[rows, :]
        v = v_ref[rows, :]
        cs = _dot(cum_lhs, la_sc[rows, :])
        yield
        bc = cs[:, 0:dk] + cs[:, dk:2 * dk]
        tots = [jnp.sum(jnp.where(is_last, bc[c * ck:(c + 1) * ck], 0.0), axis=0, keepdims=True)
                for c in range(cps)]
        tot = jnp.concatenate([jnp.broadcast_to(t, (ck, dk)) for t in tots], axis=0)
        ref = 0.5 * tot
        qt = q * jnp.exp(bc - ref)
        kt_sc[slot] = (k * jnp.exp(ref - bc)).astype(BF16)
        lhs_sc[slot] = jnp.concatenate(
            [jnp.where(head_k == h, qt, 0.0) for h in range(GLA_HEADS)], axis=0).astype(BF16)
        qe_sc[rows, :] = (q * jnp.exp(bc)).astype(BF16)
        kl = (k * jnp.exp(tot - bc)).astype(BF16)
        zero = jnp.zeros_like(kl)
        klx = jnp.concatenate([jnp.where(row_chunk == c, kl, zero) for c in range(cps)], axis=1)
        upd = _dot_tn(v.astype(BF16), klx)
        yield
        for c in range(cps):
            upd_sc[u * cps + c] = jnp.where(blockdiag, upd[:, c * dk:(c + 1) * dk], 0.0)
            dec_sc[u * cps + c] = jnp.broadcast_to(jnp.exp(tots[c]), (8, dk))

    def stage_b(u, slot):
        rows = group_rows(u)
        v = v_ref[rows, :]
        p = _dot_nt(lhs_sc[slot], kt_sc[slot])
        yield
        att = jnp.concatenate([jnp.where(tri, p[h * sup:(h + 1) * sup], 0.0).astype(BF16)
                               for h in range(GLA_HEADS)], axis=1)
        vexp = jnp.concatenate([jnp.where(head_v == h, v, 0.0).astype(BF16) for h in range(GLA_HEADS)], axis=0)
        o = _dot(att, vexp)
        yield
        oi_sc[rows, :] = o

    def run(*stages):
        live = list(stages)
        while live:
            for g in list(live):
                if next(g, StopIteration) is StopIteration:
                    live.remove(g)

    x = _dot(g_ref[...].astype(BF16), gw_ref[0]) + gb_ref[0]
    la = (jnp.minimum(x, 0.0) - jnp.log(1.0 + jnp.exp(-jnp.abs(x)))) * (1.0 / GLA_TAU)
    la_sc[...] = jnp.concatenate(_split2(la), axis=1)

    nu = nb * nsup
    if nu <= 4:
        run(stage_a(0, 0))
        for u in range(nu):
            if u + 1 < nu:
                run(stage_b(u, u % 2), stage_a(u + 1, (u + 1) % 2))
            else:
                run(stage_b(u, u % 2))
    else:
        run(stage_a(0, 0))

        def sup_body(u, carry):
            nxt = jnp.minimum(u + 1, nu - 1)
            run(stage_b(u, jnp.bitwise_and(u, 1)), stage_a(nxt, jnp.bitwise_and(u + 1, 1)))
            return carry

        lax.fori_loop(0, nu, sup_body, 0)

    for j in range(nb):
        def state_body(c, st, j=j):
            ci = j * nchunk + c + d * (nchunk - 1 - 2 * c)
            sall_sc[ci] = st.astype(BF16)
            return dec_sc[ci][0:1, :] * st + upd_sc[ci]

        st_fin = jnp.transpose(lax.fori_loop(0, nchunk, state_body, s0_ref[j, 0]))
        for h in range(GLA_HEADS):
            sfin_ref[j, 0, 0, h] = st_fin[h * GLA_DK:(h + 1) * GLA_DK, h * GLA_DV:(h + 1) * GLA_DV]

    def inter(u):
        rows = group_rows(u)
        qe = qe_sc[rows, :]
        zero = jnp.zeros_like(qe)
        lhs = jnp.concatenate([jnp.where(row_chunk == c, qe, zero) for c in range(cps)], axis=1)
        st = jnp.concatenate([sall_sc[u * cps + c] for c in range(cps)], axis=1)
        o_ref[0, rows, :] = (oi_sc[rows, :] + _dot_nt(lhs, st)).astype(BF16)

    if nu <= 4:
        for u in range(nu):
            inter(u)
    else:
        lax.fori_loop(0, nu, lambda u, carry: (inter(u), carry)[1], 0, unroll=2 if nu % 2 == 0 else 1)


def _gla(gla_in, mla_in, wl, s0t, batch, seq_len, fin=(0, None, 1)):
    layer, prev_fin, fin_layers = fin
    aliases = {} if prev_fin is None else {7: 1}
    extra = [] if prev_fin is None else [prev_fin]
    n = gla_in.shape[0]
    dk, dv = GLA_HEADS * GLA_DK, GLA_HEADS * GLA_DV
    nb = max(1, min(batch, 1024 // seq_len))
    rows = nb * seq_len
    nchunk = nb * (seq_len // GLA_CHUNK)
    return pl.pallas_call(
        functools.partial(_gla_kernel, seq_len, nb, len(extra)),
        grid=(batch // nb, 2),
        input_output_aliases=aliases,
        in_specs=[pl.BlockSpec((rows, dk), lambda b, d: (b, 0)),
                  pl.BlockSpec((rows, dk), lambda b, d: (b, 1)),
                  pl.BlockSpec((rows, dv), lambda b, d: (b, 1)),
                  pl.BlockSpec((rows, dk), lambda b, d: (b, 1)),
                  pl.BlockSpec((1, dk, dk), lambda b, d: (d, 0, 0)),
                  pl.BlockSpec((1, 1, dk), lambda b, d: (d, 0, 0)),
                  pl.BlockSpec((nb, 1, dv, dk), lambda b, d: (b, d, 0, 0))]
                 + [pl.BlockSpec(memory_space=pl.ANY)] * len(extra),
        out_specs=[pl.BlockSpec((1, rows, dv), lambda b, d: (d, b, 0)),
                   pl.BlockSpec((nb, 1, 1, GLA_HEADS, GLA_DK, GLA_DV), lambda b, d: (b, layer, d, 0, 0, 0))],
        out_shape=[jax.ShapeDtypeStruct((2, n, dv), BF16),
                   jax.ShapeDtypeStruct((batch, fin_layers, 2, GLA_HEADS, GLA_DK, GLA_DV), F32)],
        scratch_shapes=[pltpu.VMEM((rows, dk), BF16),
                        pltpu.VMEM((nchunk, dv, dk), F32),
                        pltpu.VMEM((nchunk, 8, dk), F32),
                        pltpu.VMEM((nchunk, dv, dk), BF16),
                        pltpu.VMEM((2, GLA_HEADS * GLA_SUPER, dk), BF16),
                        pltpu.VMEM((2, GLA_SUPER, dk), BF16),
                        pltpu.VMEM((rows, 2 * dk), BF16),
                        pltpu.VMEM((rows, dv), F32)],
        compiler_params=_cp("arbitrary", "arbitrary"),
        name="gla",
    )(gla_in, gla_in, gla_in, mla_in, wl["gla_gw"], wl["gla_gb"], s0t, *extra)


def _outproj_kernel(x_ref, mod_ref, g_ref, om_ref, oh_ref, su_ref, sf_ref, sb_ref, sd_ref, sw_ref, sbias_ref,
                    gf_ref, gb_ref, gn_ref, hm_ref, w_ref, y_ref):
    c = D_BRANCH
    g = g_ref[...].astype(F32)
    acc = _dot((om_ref[...].astype(F32) * _silu(g[:, 0:c])).astype(BF16), w_ref[0:c, :])
    acc += _dot((oh_ref[...].astype(F32) * _silu(g[:, c:2 * c])).astype(BF16), w_ref[c:2 * c, :])
    ys = sd_ref[...] * su_ref[...] + sf_ref[0] + sb_ref[0]
    ge = 0.5 * ys * (1.0 + jnp.tanh(math.sqrt(2.0 / math.pi) * (ys + 0.044715 * (ys * ys * ys))))
    o_s5 = ge / (1.0 + jnp.exp(-(_dot(ge.astype(BF16), sw_ref[...]) + sbias_ref[...])))
    acc += _dot((o_s5 * _silu(g[:, 2 * c:3 * c])).astype(BF16), w_ref[2 * c:3 * c, :])
    og = gf_ref[0].astype(F32) + gb_ref[0].astype(F32)
    hi, lo = _split2(og * og)
    ms = _dot(hi, hm_ref[...]) + _dot(lo, hm_ref[...])
    ogn = og * lax.rsqrt(ms + EPS) * gn_ref[...]
    acc += _dot((ogn * _silu(g[:, 3 * c:4 * c])).astype(BF16), w_ref[3 * c:4 * c, :])
    y_ref[...] = x_ref[...] + mod_ref[0, 2:3, :] * acc


def _outproj(x, mod, mod_row, gates, o_mla, o_hy, s5_u, s5_y, o_gla, wl, tm):
    n, d = x.shape
    c = D_BRANCH
    row = lambda w: pl.BlockSpec((tm, w), lambda i: (i, 0))
    per_dir = lambda k: pl.BlockSpec((1, tm, c), lambda i: (k, i, 0))
    full = lambda *shape: pl.BlockSpec(shape, lambda i: (0,) * len(shape))
    return pl.pallas_call(
        _outproj_kernel,
        grid=(n // tm,),
        in_specs=[row(d),
                  pl.BlockSpec((1, 3, d), lambda i: (mod_row(i), 0, 0)),
                  row(d), row(c), row(c),
                  row(c), per_dir(0), per_dir(1), full(1, c), full(c, c), full(1, c),
                  per_dir(0), per_dir(1), full(1, c), full(c, c), full(d, d)],
        out_specs=row(d),
        out_shape=jax.ShapeDtypeStruct((n, d), F32),
        compiler_params=_cp("arbitrary"),
        name="outproj",
    )(x, mod, gates, o_mla, o_hy, s5_u, s5_y, s5_y, wl["s5_d"], wl["s5_glu_w"], wl["s5_glu_b"],
      o_gla, o_gla, wl["gla_norm"], wl["head_mean"], wl["w_out"])


def _rope_tables(seq_len):
    pos = np.arange(seq_len)
    inv = ROPE_BASE ** (-np.arange(0, 16, 2, dtype=np.float64) / 16.0)
    cos = np.ones((seq_len, HEAD_PAD))
    sin_a = np.zeros((seq_len, HEAD_PAD))
    sin_b = np.zeros((seq_len, HEAD_PAD))
    for base, p in ((MLA_NOPE, pos // GRID_W), (MLA_NOPE + 16, pos % GRID_W)):
        ang = p[:, None].astype(np.float64) * inv[None, :]
        cos[:, base:base + 8] = np.cos(ang)
        cos[:, base + 8:base + 16] = np.cos(ang)
        sin_a[:, base:base + 8] = -np.sin(ang)
        sin_b[:, base + 8:base + 16] = np.sin(ang)
    return tuple(jnp.asarray(t, F32) for t in (cos, sin_a, sin_b))


def _odd_dft(seq_len):
    bk = min(seq_len, HY_BLOCK)
    k = np.arange(bk)[:, None]
    t = np.arange(bk)[None, :]

    def mat(shift):
        ang = (np.pi / (2 * bk)) * (((2 * k + 1) * (t + shift)) % (4 * bk))
        return np.concatenate([np.cos(ang), -np.sin(ang)], axis=0)

    fo = mat(0)
    fb = -mat(bk)
    fb[:, 0] = 0.0
    const = lambda a: jnp.asarray(a, F32).astype(BF16)
    return const(fo), const(fb), const(fo.T)


def _hyena_tables(seq_len):
    lag = np.arange(-seq_len, seq_len)
    pos = np.where(lag == -seq_len, 0, np.abs(lag)).astype(np.float64)
    t = pos / seq_len
    w = 2.0 * np.pi * pos / seq_len
    bands = np.linspace(1e-4, HY_BANDS - 1, HY_BANDS)
    feat = np.zeros((2 * seq_len, HY_HIDDEN))
    feat[:, 0] = t
    feat[:, 1:1 + HY_BANDS] = np.cos(w[:, None] * bands)
    feat[:, 1 + HY_BANDS:HY_FEAT] = np.sin(w[:, None] * bands)
    feat = feat.reshape(-1, 2, 128, HY_HIDDEN).transpose(0, 2, 1, 3).reshape(seq_len, 2 * HY_HIDDEN)
    deltas = np.linspace(math.log(1.0 / HY_TARGET) / HY_FAST_DECAY, math.log(1.0 / HY_TARGET) / HY_SLOW_DECAY,
                         D_BRANCH)
    win = np.exp(-t[:, None] * deltas[None, :]) + HY_SHIFT
    return jnp.asarray(feat, F32), jnp.asarray(win, F32)


def _pad_to(a, shape):
    return jnp.pad(a, [(0, s - d) for s, d in zip(shape, a.shape)])


def _layer_weights(l, p):
    z = lambda *s: jnp.zeros(s, F32)
    w_in = p["w_in"][l]
    col = lambda lo, hi: w_in[:, lo:hi]
    d = D_MODEL
    w_p = jnp.concatenate([
        col(0, 192), col(320, 352), col(2656, 2688), col(192, 320),
        col(352, 608), col(1376, 1632), col(1888, 2144), col(2688, 2944),
        col(608, 1376), col(1632, 1888),
        col(2144, 2272), col(2272, 2400), col(2400, 2656)], axis=1).astype(BF16)
    wl = {"w_in": w_p, "norm_w": p["norm_w"][l].reshape(1, d), "w_out": p["w_out"][l].astype(BF16)}
    wl["qa_norm"] = _pad_to(p["mla_qa_norm"][l].reshape(1, -1), (1, 256))
    w_uq = _pad_to(p["mla_w_uq"][l].reshape(MLA_Q_RANK, MLA_HEADS, MLA_QK), (256, MLA_HEADS, HEAD_PAD))
    wl["w_uq"] = w_uq.reshape(256, MLA_HEADS * HEAD_PAD).astype(BF16)
    wl["q_norm"] = _pad_to(p["mla_q_norm"][l].reshape(1, -1), (1, HEAD_PAD))
    wl["k_norm"] = _pad_to(p["mla_k_norm"][l].reshape(1, -1), (1, HEAD_PAD))
    wl["kva_norm"] = p["mla_kva_norm"][l].reshape(1, -1)
    w_ukv = p["mla_w_ukv"][l].reshape(MLA_KV_RANK, MLA_HEADS, MLA_NOPE + MLA_V)
    wl["w_uk"] = _pad_to(w_ukv[:, :, :MLA_NOPE], (MLA_KV_RANK, MLA_HEADS, HEAD_PAD)).reshape(MLA_KV_RANK, -1).astype(BF16)
    wl["w_uv"] = w_ukv[:, :, MLA_NOPE:].reshape(MLA_KV_RANK, MLA_HEADS * MLA_V).astype(BF16)
    wl["hy_conv_w"] = p["hy_conv_w"][l]
    wl["hy_conv_b"] = p["hy_conv_b"][l].reshape(1, -1)
    hh = HY_HIDDEN
    twice = lambda a: jnp.tile(a.reshape(1, -1), (1, 2))
    bdiag = lambda a: jnp.concatenate([jnp.pad(a, ((0, 0), (0, a.shape[1]))), jnp.pad(a, ((0, 0), (a.shape[1], 0)))], axis=0)
    wl["hy_w1"] = bdiag(_pad_to(p["hy_w1"][l], (hh, hh)))
    wl["hy_b1"], wl["hy_f1"] = twice(p["hy_b1"][l]), twice(p["hy_freq1"][l])
    wl["hy_w2"] = bdiag(p["hy_w2"][l])
    wl["hy_b2"], wl["hy_f2"] = twice(p["hy_b2"][l]), twice(p["hy_freq2"][l])
    w3 = p["hy_w3"][l]
    wl["hy_w3"] = jnp.concatenate([bdiag(w3[:, 0:512]), bdiag(w3[:, 512:1024])], axis=1)
    wl["hy_bias"] = p["hy_bias"][l]
    flat = lambda a: a[l].reshape(2, 1, S5_N)
    wl["s5_are"], wl["s5_aim"] = flat(p["s5_a_re"]), flat(p["s5_a_im"])
    wl["s5_ldt"] = jnp.repeat(p["s5_log_dt"][l], S5_STATE, axis=-1).reshape(2, 1, S5_N)
    same_group = jnp.asarray((np.arange(D_BRANCH) // S5_GROUP)[:, None] == (np.arange(S5_N) // S5_STATE)[None, :])
    bd_b = lambda a: jnp.where(same_group, jnp.tile(
        a[l].transpose(0, 1, 3, 2).reshape(2, D_BRANCH, S5_STATE), (1, 1, S5_GROUPS)), 0.0)
    bd_c = lambda a: jnp.where(same_group.T, jnp.tile(
        a[l].transpose(0, 1, 3, 2).reshape(2, S5_N, S5_GROUP), (1, 1, S5_GROUPS)), 0.0)
    wl["s5_bre"], wl["s5_bim"] = bd_b(p["s5_b_re"]), bd_b(p["s5_b_im"])
    wl["s5_cre"], wl["s5_cim"] = bd_c(p["s5_c_re"]), bd_c(p["s5_c_im"])
    wl["s5_d"] = p["s5_d"][l].reshape(1, -1)
    wl["s5_glu_w"] = p["s5_glu_w"][l].astype(BF16)
    wl["s5_glu_b"] = p["s5_glu_b"][l].reshape(1, -1)
    gw = p["gla_gw"][l]
    dk = GLA_HEADS * GLA_DK
    wl["gla_gw"] = jnp.stack([_pad_to(jnp.pad(gw[i], ((GLA_G_LANE + GLA_RANK * i, 0), (0, 0))), (dk, dk))
                              for i in range(2)]).astype(BF16)
    wl["gla_gb"] = p["gla_gb"][l].reshape(2, 1, dk)
    wl["gla_norm"] = jnp.tile(p["gla_norm"][l], GLA_HEADS).reshape(1, -1)
    head = np.arange(D_BRANCH) // GLA_DV
    wl["head_mean"] = jnp.asarray((head[:, None] == head[None, :]) / GLA_DV, BF16)
    return wl


def _hyena_filters(wl, tabs):
    feat, win, fo, fb, _ = tabs
    kern_lin, nrm = _hy_mlp(feat, wl, win)
    return _hy_kspec(kern_lin, nrm, fo, fb)


def _trunk_layer(x, mod, mod_row, wl, batch, seq_len, hy_tabs, rope_tabs=None, ctx=None, layer=0, cache_bufs=None):
    n = batch * seq_len
    tm = 512
    mla_in, gates, hy_in, s5_in, gla_in = _inproj(x, mod, mod_row, wl["norm_w"], wl["w_in"], tm)

    if ctx is None:
        prev = None if cache_bufs is None else cache_bufs[:2]
        q, k, v, ckv, krope = _mla_prep(mla_in, wl, rope_tabs, seq_len, tm, True, (layer, prev))
    else:
        q, k, v = _mla_prep(mla_in, wl, rope_tabs, seq_len, tm, True)
        ckv = krope = None
    kv_parts = [(k, v, seq_len)]
    if ctx is not None:
        k_ctx, v_ctx = _mla_prep(ctx["mla"], wl, None, ctx["past"], 512, False)
        kv_parts = [(k_ctx, v_ctx, ctx["past"])] + kv_parts
    o_mla = _attention(q, kv_parts, batch, seq_len, 256)

    o_hy = _hyena(hy_in, wl, _hyena_filters(wl, hy_tabs), hy_tabs[2], hy_tabs[4], batch, seq_len)

    nseg = seq_len // S5_SEG
    nseq = batch * nseg
    u_seg = s5_in.reshape(nseq, S5_SEG, D_BRANCH)
    if ctx is None:
        hin = jnp.zeros((2, nseq, 2 * S5_N), F32)
    else:
        (fin,) = _s5_scan(u_seg, jnp.zeros((2, nseq, 2 * S5_N), F32), wl, False)
        hin = _s5_chain(fin, ctx["s5_h0"], wl, batch, nseg)
    y2, s5_fin = _s5_scan(u_seg, hin, wl, True)

    s0 = jnp.zeros((batch, 2, GLA_HEADS * GLA_DV, GLA_HEADS * GLA_DK), F32) if ctx is None else ctx["gla_s0"]
    fin = (0, None, 1) if ctx is not None else (layer, None if cache_bufs is None else cache_bufs[2], DEPTH)
    o_gla, gla_fin = _gla(gla_in, mla_in, wl, s0, batch, seq_len, fin)

    y = _outproj(x, mod, mod_row, gates, o_mla, o_hy, s5_in, y2.reshape(2, n, D_BRANCH), o_gla, wl, tm)
    return y, (ckv, krope, s5_fin, gla_fin)


def kernel(x_prompt, x_sample, c, cache_mla_ckv, cache_mla_krope, state_s5, state_gla, c_ctx, norm_w, ada_w, ada_b, w_in, w_out, mla_qa_norm, mla_kva_norm, mla_w_uq, mla_w_ukv, mla_q_norm, mla_k_norm, hy_conv_w, hy_conv_b, hy_w1, hy_b1, hy_freq1, hy_w2, hy_b2, hy_freq2, hy_w3, hy_bias, s5_a_re, s5_a_im, s5_log_dt, s5_b_re, s5_b_im, s5_c_re, s5_c_im, s5_d, s5_glu_w, s5_glu_b, gla_gw, gla_gb, gla_norm):
    params = dict(norm_w=norm_w, w_in=w_in, w_out=w_out, mla_qa_norm=mla_qa_norm, mla_kva_norm=mla_kva_norm,
                  mla_w_uq=mla_w_uq, mla_w_ukv=mla_w_ukv, mla_q_norm=mla_q_norm, mla_k_norm=mla_k_norm,
                  hy_conv_w=hy_conv_w, hy_conv_b=hy_conv_b, hy_w1=hy_w1, hy_b1=hy_b1, hy_freq1=hy_freq1,
                  hy_w2=hy_w2, hy_b2=hy_b2, hy_freq2=hy_freq2, hy_w3=hy_w3, hy_bias=hy_bias,
                  s5_a_re=s5_a_re, s5_a_im=s5_a_im, s5_log_dt=s5_log_dt, s5_b_re=s5_b_re, s5_b_im=s5_b_im,
                  s5_c_re=s5_c_re, s5_c_im=s5_c_im, s5_d=s5_d, s5_glu_w=s5_glu_w, s5_glu_b=s5_glu_b,
                  gla_gw=gla_gw, gla_gb=gla_gb, gla_norm=gla_norm)
    bp, lp, d = x_prompt.shape
    bs, ls, _ = x_sample.shape
    past = cache_mla_ckv.shape[2]
    n_s5 = S5_N

    conds = jnp.concatenate([c_ctx[None, :], c, jnp.zeros((8 - 1 - bs, d), F32)], axis=0)
    mods = _modulation(conds, ada_w, ada_b).reshape(DEPTH, 8, 3, d)

    tabs_p = _hyena_tables(lp) + _odd_dft(lp)
    tabs_s = _hyena_tables(ls) + _odd_dft(ls)
    rope_tabs = _rope_tables(ls)
    tm_s = 512
    nseg = ls // S5_SEG

    y_p = x_prompt.reshape(bp * lp, d)
    y_s = x_sample.reshape(bs * ls, d)
    s5_l = []
    cache_bufs = (jnp.zeros((bp, DEPTH, lp, MLA_KV_RANK), F32), jnp.zeros((bp, DEPTH, lp, MLA_ROPE), F32),
                  jnp.zeros((bp, DEPTH, 2, GLA_HEADS, GLA_DK, GLA_DV), F32))
    for l in range(DEPTH):
        wl = _layer_weights(l, params)
        y_p, (ckv, krope, s5_fin, gla_fin) = _trunk_layer(y_p, mods[l], lambda i: 0, wl, bp, lp, tabs_p,
                                                          layer=l, cache_bufs=cache_bufs)
        cache_bufs = (ckv, krope, gla_fin)
        s5_l.append(jnp.stack([s5_fin[:, :, :n_s5], s5_fin[:, :, n_s5:]], axis=-1)
                    .reshape(2, bp, S5_GROUPS, S5_STATE, 2).transpose(1, 0, 2, 3, 4))

        mla_ctx = jnp.concatenate([cache_mla_ckv[:, l], jnp.zeros((bs, past, 64), F32), cache_mla_krope[:, l],
                                   jnp.zeros((bs, past, 32), F32)], axis=-1).reshape(bs * past, 256)
        st = state_s5[:, l]
        h0 = jnp.concatenate([st[..., 0].reshape(bs, 2, n_s5), st[..., 1].reshape(bs, 2, n_s5)], axis=-1)
        h0 = h0.transpose(1, 0, 2)
        h0rows = jnp.zeros((2, nseg * bs, 2 * n_s5), F32)
        h0rows = h0rows.at[0, 0::nseg].set(h0[0]).at[1, nseg - 1::nseg].set(h0[1])
        eye_h = jnp.eye(GLA_HEADS, dtype=F32)
        gla_s0 = jnp.einsum("bdhke,hg->bdhegk", state_gla[:, l], eye_h).reshape(
            bs, 2, GLA_HEADS * GLA_DV, GLA_HEADS * GLA_DK)
        ctx = {"mla": mla_ctx, "past": past, "s5_h0": h0rows, "gla_s0": gla_s0}
        y_s, _ = _trunk_layer(y_s, mods[l], lambda i: 1 + (i * tm_s) // ls, wl, bs, ls, tabs_s, rope_tabs, ctx)

    return (y_p.reshape(bp, lp, d), y_s.reshape(bs, ls, d),
            cache_bufs[0], cache_bufs[1], jnp.stack(s5_l, axis=1), cache_bufs[2])
```

```python
import functools
import math

import numpy as np
import jax
import jax.numpy as jnp
from jax import lax
from jax.experimental import pallas as pl
from jax.experimental.pallas import tpu as pltpu

F32 = jnp.float32
BF16 = jnp.bfloat16

D_MODEL = 1024
DEPTH = 2
GRID_W = 64
D_BRANCH = 256
EPS = 1e-6

MLA_HEADS = 4
MLA_Q_RANK = 192
MLA_KV_RANK = 128
MLA_NOPE = 64
MLA_ROPE = 32
MLA_QK = 96
MLA_V = 64
ROPE_BASE = 10000.0
HEAD_PAD = 128

HY_BANDS = 16
HY_FEAT = 33
HY_HIDDEN = 64
HY_SHIFT = 0.05
HY_FAST_DECAY = 0.3
HY_SLOW_DECAY = 1.5
HY_TARGET = 1e-2
HY_BLOCK = 512

S5_GROUP = 16
S5_GROUPS = 16
S5_STATE = 64
S5_N = S5_GROUPS * S5_STATE
S5_ROWS = 512
S5_SEG = 256

GLA_HEADS = 4
GLA_DK = 32
GLA_DV = 64
GLA_RANK = 16
GLA_TAU = 16.0
GLA_CHUNK = 64
GLA_SUPER = 256

SEG_MLA = (0, 384)
SEG_GATE = (384, 1408)
SEG_HY = (1408, 2176)
SEG_S5 = (2176, 2432)
SEG_GLA = (2432, 2944)
N_PROJ = 2944
GLA_G_LANE = 96

VMEM_LIMIT = 48 * 1024 * 1024


def _cp(*sem):
    return pltpu.CompilerParams(dimension_semantics=sem, vmem_limit_bytes=VMEM_LIMIT)


def _dot(a, b):
    return jnp.dot(a, b, preferred_element_type=F32)


def _dot_nt(a, b):
    return lax.dot_general(a, b, (((1,), (1,)), ((), ())), preferred_element_type=F32)


def _dot_tn(a, b):
    return lax.dot_general(a, b, (((0,), (0,)), ((), ())), preferred_element_type=F32)


def _split2(x):
    hi = x.astype(BF16)
    lo = (x - hi.astype(F32)).astype(BF16)
    return hi, lo


def _split3(x):
    h1 = x.astype(BF16)
    r1 = x - h1.astype(F32)
    h2 = r1.astype(BF16)
    h3 = (r1 - h2.astype(F32)).astype(BF16)
    return h1, h2, h3


def _dot3(a, b):
    a1, a2 = _split2(a)
    b1, b2 = _split2(b)
    return _dot(a1, b1) + (_dot(a1, b2) + _dot(a2, b1))


def _silu(z):
    return z / (1.0 + jnp.exp(-z))


def _mod_kernel(c_ref, w_ref, b_ref, o_ref):
    s = _silu(c_ref[...])
    o_ref[0] = _dot(s.astype(BF16), w_ref[0].astype(BF16)) + b_ref[0]


def _modulation(conds, ada_w, ada_b):
    d = D_MODEL
    return pl.pallas_call(
        _mod_kernel,
        grid=(DEPTH, 3),
        in_specs=[pl.BlockSpec((8, d), lambda l, j: (0, 0)),
                  pl.BlockSpec((1, d, d), lambda l, j: (l, 0, j)),
                  pl.BlockSpec((1, 1, d), lambda l, j: (l, 0, j))],
        out_specs=pl.BlockSpec((1, 8, d), lambda l, j: (l, 0, j)),
        out_shape=jax.ShapeDtypeStruct((DEPTH, 8, 3 * d), F32),
        compiler_params=_cp("arbitrary", "arbitrary"),
        name="modulation",
    )(conds, ada_w, ada_b.reshape(DEPTH, 1, 3 * d))


def _inproj_kernel(x_ref, mod_ref, nw_ref, w_ref, o_mla, o_g, o_hy, o_s5, o_gla):
    x = x_ref[...]
    ms = jnp.mean(x * x, axis=-1, keepdims=True)
    y = x * lax.rsqrt(ms + EPS) * nw_ref[...]
    h = (y * (1.0 + mod_ref[0, 1:2, :]) + mod_ref[0, 0:1, :]).astype(BF16)
    for o, (lo, hi) in ((o_mla, SEG_MLA), (o_g, SEG_GATE), (o_hy, SEG_HY), (o_s5, SEG_S5), (o_gla, SEG_GLA)):
        o[...] = _dot(h, w_ref[:, lo:hi]).astype(o.dtype)


def _inproj(x, mod, mod_row, norm_w, w_p, tm):
    n, d = x.shape
    widths = [hi - lo for lo, hi in (SEG_MLA, SEG_GATE, SEG_HY, SEG_S5, SEG_GLA)]
    dtypes = [F32, BF16, F32, F32, F32]
    return pl.pallas_call(
        _inproj_kernel,
        grid=(n // tm,),
        in_specs=[pl.BlockSpec((tm, d), lambda i: (i, 0)),
                  pl.BlockSpec((1, 3, d), lambda i: (mod_row(i), 0, 0)),
                  pl.BlockSpec((1, d), lambda i: (0, 0)),
                  pl.BlockSpec((d, N_PROJ), lambda i: (0, 0))],
        out_specs=[pl.BlockSpec((tm, w), lambda i: (i, 0)) for w in widths],
        out_shape=[jax.ShapeDtypeStruct((n, w), t) for w, t in zip(widths, dtypes)],
        compiler_params=_cp("arbitrary"),
        name="inproj",
    )(x, mod, norm_w, w_p)


def _head_norm(xh, w):
    ms = jnp.sum(xh * xh, axis=-1, keepdims=True) * (1.0 / MLA_QK)
    return xh * lax.rsqrt(ms + EPS) * w


def _rope(xh, cos, sin_a, sin_b):
    return xh * cos + pltpu.roll(xh, HEAD_PAD - 8, 1) * sin_a + pltpu.roll(xh, 8, 1) * sin_b


def _mla_prep_kernel(has_q, rope, cache_seqs, n_aliased, *refs):
    refs = list(refs)
    m_ref = refs.pop(0)
    if has_q:
        qan_ref, wuq_ref, qn_ref, kvn_ref = refs[:4]
        refs = refs[4:]
    wuk_ref, wuv_ref, kn_ref = refs[:3]
    refs = refs[3:]
    if rope:
        cos_ref, sa_ref, sb_ref = refs[:3]
        refs = refs[3:]
        cos, sa, sb = cos_ref[...], sa_ref[...], sb_ref[...]
    refs = refs[n_aliased:]
    if has_q:
        q_ref = refs.pop(0)
    k_ref, v_ref = refs[:2]
    if cache_seqs:
        ckv_ref, kro_ref = refs[2:]
    m = m_ref[...]
    if has_q:
        lane = lax.broadcasted_iota(jnp.int32, (1, HEAD_PAD), 1)
        mixed = m[:, 128:256]
        cq = jnp.concatenate([m[:, 0:128], jnp.where(lane < MLA_NOPE, mixed, 0.0)], axis=1)
        ms = jnp.sum(cq * cq, axis=-1, keepdims=True) * (1.0 / MLA_Q_RANK)
        cqn = cq * lax.rsqrt(ms + EPS) * qan_ref[...]
        q = _dot(cqn.astype(BF16), wuq_ref[...])
        ckv = m[:, 256:384]
        ckvn = ckv * lax.rsqrt(jnp.mean(ckv * ckv, axis=-1, keepdims=True) + EPS) * kvn_ref[...]
        kr = jnp.where(jnp.logical_and(lane >= MLA_NOPE, lane < MLA_NOPE + MLA_ROPE), mixed, 0.0)
        if cache_seqs:
            seq_len = ckv_ref.shape[2]
            for s in range(cache_seqs):
                ckv_ref[s, 0] = ckvn[s * seq_len:(s + 1) * seq_len]
                kro_ref[s, 0] = kr[s * seq_len:(s + 1) * seq_len, MLA_NOPE:MLA_NOPE + MLA_ROPE]
    else:
        ckvn = m[:, 0:128]
        kr = m[:, 128:256]
    cb = ckvn.astype(BF16)
    kup = _dot(cb, wuk_ref[...])
    v_ref[...] = _dot(cb, wuv_ref[...]).astype(BF16)
    for h in range(MLA_HEADS):
        sl = slice(HEAD_PAD * h, HEAD_PAD * (h + 1))
        kh = _head_norm(kup[:, sl] + kr, kn_ref[...])
        if rope:
            kh = _rope(kh, cos, sa, sb)
        k_ref[:, sl] = kh.astype(BF16)
        if has_q:
            qh = _head_norm(q[:, sl], qn_ref[...])
            if rope:
                qh = _rope(qh, cos, sa, sb)
            q_ref[:, sl] = (qh * (MLA_QK ** -0.5)).astype(BF16)


def _mla_prep(m, wl, rope_tabs, seq_len, tm, has_q, cache=None):
    n, wm = m.shape
    rope = rope_tabs is not None
    full = lambda shape: pl.BlockSpec(shape, lambda i: (0,) * len(shape))
    args, specs = [m], [pl.BlockSpec((tm, wm), lambda i: (i, 0))]
    if has_q:
        args += [wl["qa_norm"], wl["w_uq"], wl["q_norm"], wl["kva_norm"]]
        specs += [full((1, 256)), full((256, 512)), full((1, 128)), full((1, 128))]
    args += [wl["w_uk"], wl["w_uv"], wl["k_norm"]]
    specs += [full((128, 512)), full((128, 256)), full((1, 128))]
    if rope:
        nt = seq_len // tm
        args += list(rope_tabs)
        specs += [pl.BlockSpec((tm, HEAD_PAD), lambda i: (i % nt, 0))] * 3
    row = lambda w: pl.BlockSpec((tm, w), lambda i: (i, 0))
    out_specs = [row(512), row(256)]
    out_shape = [jax.ShapeDtypeStruct((n, 512), BF16), jax.ShapeDtypeStruct((n, 256), BF16)]
    aliases = {}
    nseq = 0
    if has_q:
        out_specs = [row(512)] + out_specs
        out_shape = [jax.ShapeDtypeStruct((n, 512), BF16)] + out_shape
    if cache is not None:
        layer, prev = cache
        nseq = tm // seq_len
        for w in (MLA_KV_RANK, MLA_ROPE):
            out_specs.append(pl.BlockSpec((nseq, 1, seq_len, w), lambda i: (i, layer, 0, 0)))
            out_shape.append(jax.ShapeDtypeStruct((n // seq_len, DEPTH, seq_len, w), F32))
        if prev is not None:
            for k, buf in enumerate(prev):
                aliases[len(args)] = len(out_shape) - 2 + k
                args.append(buf)
                specs.append(pl.BlockSpec(memory_space=pl.ANY))
    return pl.pallas_call(
        functools.partial(_mla_prep_kernel, has_q, rope, nseq, len(aliases)),
        grid=(n // tm,),
        in_specs=specs, out_specs=out_specs, out_shape=out_shape,
        input_output_aliases=aliases,
        compiler_params=_cp("arbitrary"),
        name="mla_prep",
    )(*args)


def _attn_kernel(nparts, nseq, q_ref, *refs):
    kv = [(refs[2 * i], refs[2 * i + 1]) for i in range(nparts)]
    o_ref = refs[2 * nparts]
    tq = q_ref.shape[0] // nseq
    low = lax.broadcasted_iota(jnp.int32, (1, HEAD_PAD), 1) < MLA_V
    units = [(s, h) for s in range(nseq) for h in range(MLA_HEADS)]

    def keys(ref, s):
        lk = ref.shape[0] // nseq
        return slice(s * lk, (s + 1) * lk)

    def scores(s, h):
        sl = slice(HEAD_PAD * h, HEAD_PAD * (h + 1))
        return [_dot_nt(q_ref[s * tq:(s + 1) * tq, sl], k_ref[keys(k_ref, s), sl]) for k_ref, _ in kv]

    s_next = scores(*units[0])
    acc = None
    for n, (s, h) in enumerate(units):
        pair, j = divmod(h, 2)
        sc = s_next
        if n + 1 < len(units):
            s_next = scores(*units[n + 1])
        if j == 0:
            v_half = []
            for _, v_ref in kv:
                vp = v_ref[keys(v_ref, s), HEAD_PAD * pair:HEAD_PAD * (pair + 1)]
                zero = jnp.zeros_like(vp)
                v_half.append((jnp.where(low, vp, zero), jnp.where(low, zero, vp)))
        m = functools.reduce(jnp.maximum, [jnp.max(x, axis=-1, keepdims=True) for x in sc])
        p = [jnp.exp(x - m) for x in sc]
        den = functools.reduce(jnp.add, [jnp.sum(x, axis=-1, keepdims=True) for x in p])
        num = functools.reduce(jnp.add, [_dot(x.astype(BF16), vh[j]) for x, vh in zip(p, v_half)])
        o = num / den
        acc = o if j == 0 else acc + o
        if j == 1:
            o_ref[s * tq:(s + 1) * tq, HEAD_PAD * pair:HEAD_PAD * (pair + 1)] = acc.astype(BF16)


def _attention(q, kv_parts, batch, lq, tq):
    nq = lq // tq
    nseq = max(1, min(batch, 1024 // lq)) if nq == 1 else 1
    args, specs = [q], [pl.BlockSpec((nseq * tq, 512), lambda b, i: (b * nq + i, 0))]
    for k, v, lk in kv_parts:
        args += [k, v]
        specs += [pl.BlockSpec((nseq * lk, 512), lambda b, i: (b, 0)),
                  pl.BlockSpec((nseq * lk, 256), lambda b, i: (b, 0))]
    return pl.pallas_call(
        functools.partial(_attn_kernel, len(kv_parts), nseq),
        grid=(batch // nseq, nq),
        in_specs=specs,
        out_specs=pl.BlockSpec((nseq * tq, 256), lambda b, i: (b * nq + i, 0)),
        out_shape=jax.ShapeDtypeStruct((batch * lq, 256), BF16),
        compiler_params=_cp("arbitrary", "arbitrary"),
        name="attention",
    )(*args)


def _hyena_kernel(seq_len, bk, nseq, x_ref, cw_ref, cb_ref, fo_ref, go_ref, k_ref, bias_ref, o_ref,
                  u_sc, y_sc, z_sc):
    c = D_BRANCH
    n = nseq * seq_len
    nblk = seq_len // bk
    pos = jnp.bitwise_and(lax.broadcasted_iota(jnp.int32, (n, 1), 0), seq_len - 1)
    first, last = pos == 0, pos == seq_len - 1

    def short_conv(g):
        cols = slice(g * c, (g + 1) * c)
        x = x_ref[:, cols]
        xm = jnp.where(first, 0.0, pltpu.roll(x, 1, 0))
        xp = jnp.where(last, 0.0, pltpu.roll(x, n - 1, 0))
        return cw_ref[0:1, cols] * xm + cw_ref[1:2, cols] * x + cw_ref[2:3, cols] * xp + cb_ref[:, cols]

    fo, go = fo_ref[...], go_ref[...]

    def long_conv(s, v, order, emit):
        cols = slice(order * c, (order + 1) * c)
        for j in range(nblk):
            u_sc[s, j] = _dot(fo, v[j * bk:(j + 1) * bk].astype(BF16))
        yield
        rc = 32
        for i in range(nblk):
            def mix(r, carry, i=i):
                top = pl.ds(pl.multiple_of(r * rc, rc), rc)
                bot = pl.ds(pl.multiple_of(bk + r * rc, rc), rc)
                at = ab = None
                for j in range(nblk):
                    q = i - j + nblk - 1
                    kt, kb = k_ref[q, top, cols], k_ref[q, bot, cols]
                    ut, ub = u_sc[s, j, top, :], u_sc[s, j, bot, :]
                    pt, pb = ut * kt - ub * kb, ut * kb + ub * kt
                    at, ab = (pt, pb) if at is None else (at + pt, ab + pb)
                z_sc[s, top, :] = at.astype(BF16)
                z_sc[s, bot, :] = ab.astype(BF16)
                return carry

            lax.fori_loop(0, bk // rc, mix, 0, unroll=True if nblk == 1 else 2)
            y = _dot(go, z_sc[s])
            yield
            emit(i, y)

    v_all, x1_all, x2_all = short_conv(0), short_conv(1), short_conv(2)

    def sequence(s):
        base = s * seq_len
        v = v_all[base:base + seq_len]

        def emit1(i, y):
            r = slice(i * bk, (i + 1) * bk)
            y_sc[s, r, :] = x1_all[base + i * bk:base + (i + 1) * bk] * (y + bias_ref[0:1, :] * v[r])

        yield from long_conv(s, v, 0, emit1)
        y1 = y_sc[s]

        def emit2(i, y):
            r = slice(i * bk, (i + 1) * bk)
            o_ref[base + i * bk:base + (i + 1) * bk, :] = (
                x2_all[base + i * bk:base + (i + 1) * bk] * (y + bias_ref[1:2, :] * y1[r])).astype(BF16)

        yield from long_conv(s, y1, 1, emit2)

    live = [sequence(s) for s in range(nseq)]
    while live:
        for g in list(live):
            if next(g, StopIteration) is StopIteration:
                live.remove(g)


def _hyena(x, wl, kspec, fo, go, batch, seq_len):
    c = D_BRANCH
    bk = fo.shape[1]
    nseq = max(1, min(batch, 1024 // seq_len))
    rows = nseq * seq_len
    nblk = seq_len // bk
    full = lambda a: pl.BlockSpec(a.shape, lambda i: (0,) * a.ndim)
    return pl.pallas_call(
        functools.partial(_hyena_kernel, seq_len, bk, nseq),
        grid=(batch // nseq,),
        in_specs=[pl.BlockSpec((rows, 3 * c), lambda i: (i, 0)),
                  full(wl["hy_conv_w"]), full(wl["hy_conv_b"]), full(fo), full(go),
                  pl.BlockSpec(kspec.shape, lambda i: (0, 0, 0), pipeline_mode=pl.Buffered(1)),
                  full(wl["hy_bias"])],
        out_specs=pl.BlockSpec((rows, c), lambda i: (i, 0)),
        out_shape=jax.ShapeDtypeStruct((batch * seq_len, c), BF16),
        scratch_shapes=[pltpu.VMEM((nseq, nblk, 2 * bk, c), F32), pltpu.VMEM((nseq, seq_len, c), F32),
                        pltpu.VMEM((nseq, 2 * bk, c), BF16)],
        compiler_params=pltpu.CompilerParams(dimension_semantics=("arbitrary",), vmem_limit_bytes=56 * 1024 * 1024),
        name="hyena",
    )(x, wl["hy_conv_w"], wl["hy_conv_b"], fo, go, kspec, wl["hy_bias"])


def _hy_mlp_kernel(feat_ref, w1_ref, b1_ref, f1_ref, w2_ref, b2_ref, f2_ref, w3_ref, win_ref,
                   kern_ref, nrm_ref):
    i = pl.program_id(0)
    tl = win_ref.shape[0]
    h = jnp.sin(f1_ref[...] * (_dot3(feat_ref[...], w1_ref[...]) + b1_ref[...]))
    h = jnp.sin(f2_ref[...] * (_dot3(h, w2_ref[...]) + b2_ref[...]))
    filt = _dot3(h, w3_ref[...])
    filt = jnp.concatenate([filt[:, 0:512], filt[:, 512:1024]], axis=0)
    win = win_ref[...]
    row0 = (lax.broadcasted_iota(jnp.int32, (tl, 1), 0) + i * tl) == 0
    filt = jnp.where(row0, 0.0, filt * jnp.concatenate([win, win], axis=1))
    kern_ref[...] = filt
    part = jnp.sum(jnp.abs(filt), axis=0, keepdims=True)

    @pl.when(i == 0)
    def _():
        nrm_ref[...] = jnp.zeros_like(nrm_ref)
    nrm_ref[...] += jnp.broadcast_to(part, nrm_ref.shape)


def _hy_mlp(feat, wl, win):
    rows = win.shape[0]
    tl = 256
    nl = rows // (2 * tl)
    full = lambda shape: pl.BlockSpec(shape, lambda i: (0,) * len(shape))
    return pl.pallas_call(
        _hy_mlp_kernel,
        grid=(rows // tl,),
        in_specs=[pl.BlockSpec((tl // 2, 128), lambda i: (i, 0)),
                  full((128, 128)), full((1, 128)), full((1, 128)),
                  full((128, 128)), full((1, 128)), full((1, 128)),
                  pl.BlockSpec((128, 1024), lambda i: (0, jnp.where(i < nl, 1, 0))),
                  pl.BlockSpec((tl, 256), lambda i: (i, 0))],
        out_specs=[pl.BlockSpec((tl, 512), lambda i: (i, 0)), full((8, 512))],
        out_shape=[jax.ShapeDtypeStruct((rows, 512), F32), jax.ShapeDtypeStruct((8, 512), F32)],
        compiler_params=_cp("arbitrary"),
        name="hy_mlp",
    )(feat, wl["hy_w1"], wl["hy_b1"], wl["hy_f1"], wl["hy_w2"], wl["hy_b2"], wl["hy_f2"], wl["hy_w3"], win)


def _hy_kspec_kernel(lo_ref, hi_ref, fo_ref, fb_ref, n_ref, o_ref):
    bk = lo_ref.shape[0]
    k = _dot(fo_ref[...], hi_ref[...].astype(BF16)) + _dot(fb_ref[...], lo_ref[...].astype(BF16))
    o_ref[0] = k * ((1.0 / bk) / n_ref[0:1, :])


def _hy_kspec(kern_lin, nrm, fo, fb):
    n2, bk = fo.shape
    nq = kern_lin.shape[0] // bk - 1
    full = lambda a: pl.BlockSpec(a.shape, lambda q: (0,) * a.ndim)
    return pl.pallas_call(
        _hy_kspec_kernel,
        grid=(nq,),
        in_specs=[pl.BlockSpec((bk, 512), lambda q: (q, 0)),
                  pl.BlockSpec((bk, 512), lambda q: (q + 1, 0)),
                  full(fo), full(fb), full(nrm)],
        out_specs=pl.BlockSpec((1, n2, 512), lambda q: (q, 0, 0)),
        out_shape=jax.ShapeDtypeStruct((nq, n2, 512), F32),
        compiler_params=_cp("arbitrary"),
        name="hy_kspec",
    )(kern_lin, kern_lin, fo, fb, nrm)


def _s5_discretise(are_ref, aim_ref, ldt_ref):
    ar = jnp.minimum(are_ref[0], -1e-4)
    ai = aim_ref[0]
    dt = jnp.exp(ldt_ref[0])
    e = jnp.exp(ar * dt)
    return ar, ai, e * jnp.cos(ai * dt), e * jnp.sin(ai * dt)


def _s5_scan_kernel(nseq, emit_y, u_ref, hin_ref, are_ref, aim_ref, ldt_ref, bre_ref, bim_ref, *rest):
    if emit_y:
        cre_ref, cim_ref, y_ref, hfin_ref, wb_sc, ab_sc, s_sc, hc_sc, perm_sc, wc_sc = rest
    else:
        hfin_ref, wb_sc, ab_sc, s_sc, hc_sc, perm_sc = rest
    d = pl.program_id(0)
    c = pl.program_id(1)
    n = S5_N

    @pl.when(c == 0)
    def _():
        ar, ai, abr, abi = _s5_discretise(are_ref, aim_ref, ldt_ref)
        ab_sc[0:1, :] = abr
        ab_sc[1:2, :] = abi
        den = 1.0 / (ar * ar + ai * ai)
        cr = ((abr - 1.0) * ar + abi * ai) * den
        ci = (abi * ar - (abr - 1.0) * ai) * den
        bre, bim = bre_ref[0], bim_ref[0]
        wb_sc[:, 0:n] = (cr * bre - ci * bim).astype(BF16)
        wb_sc[:, n:2 * n] = (cr * bim + ci * bre).astype(BF16)
        if emit_y:
            wc_sc[0:n, :] = cre_ref[0].astype(BF16)
            wc_sc[n:2 * n, :] = (-cim_ref[0]).astype(BF16)
        hc_sc[...] = hin_ref[0]

    steps = u_ref.shape[1]
    rows_c = nseq * steps

    @pl.when(c == 0)
    def _():
        i = lax.broadcasted_iota(jnp.int32, (rows_c, rows_c), 0)
        j = lax.broadcasted_iota(jnp.int32, (rows_c, rows_c), 1)
        p = lax.shift_right_logical(i, int(math.log2(nseq)))
        step = p + d * (steps - 1 - 2 * p)
        src = jnp.bitwise_and(i, nseq - 1) * steps + step
        perm_sc[...] = jnp.where(j == src, 1.0, 0.0).astype(BF16)

    lhs = _dot(perm_sc[...], u_ref[...].reshape(rows_c, D_BRANCH).astype(BF16)).astype(BF16)
    lb = 256
    y = None
    for j in range(n // lb):
        lr = slice(lb * j, lb * (j + 1))
        li = slice(n + lb * j, n + lb * (j + 1))
        bur = _dot(lhs, wb_sc[:, lr])
        bui = _dot(lhs, wb_sc[:, li])
        abr = ab_sc[0:1, lr]
        abi = ab_sc[1:2, lr]
        hr, hi = hc_sc[:, lr], hc_sc[:, li]
        for p in range(steps):
            rows = slice(p * nseq, (p + 1) * nseq)
            hr, hi = abr * hr - abi * hi + bur[rows], abr * hi + abi * hr + bui[rows]
            if emit_y:
                s_sc[rows, lr] = hr.astype(BF16)
                s_sc[rows, li] = hi.astype(BF16)
        hc_sc[:, lr] = hr
        hc_sc[:, li] = hi
        if emit_y:
            yj = _dot(s_sc[:, lr], wc_sc[lr, :]) + _dot(s_sc[:, li], wc_sc[li, :])
            y = yj if y is None else y + yj

    if emit_y:
        @pl.when(d == 0)
        def _():
            for p in range(steps):
                y_ref[0, :, p, :] = y[p * nseq:(p + 1) * nseq]

        @pl.when(d == 1)
        def _():
            for p in range(steps):
                y_ref[0, :, steps - 1 - p, :] = y[p * nseq:(p + 1) * nseq]

    @pl.when(c == pl.num_programs(1) - 1)
    def _():
        hfin_ref[0] = hc_sc[...]


def _s5_scan(u, hin, wl, emit_y):
    nseq, nstep, _ = u.shape
    steps = S5_ROWS // nseq
    nc = nstep // steps
    n = S5_N
    chunk = lambda d, c: c + d * (nc - 1 - 2 * c)
    per_dir = lambda shape: pl.BlockSpec((1,) + shape, lambda d, c: (d,) + (0,) * len(shape))
    args = [u, hin, wl["s5_are"], wl["s5_aim"], wl["s5_ldt"], wl["s5_bre"], wl["s5_bim"]]
    specs = [pl.BlockSpec((nseq, steps, D_BRANCH), lambda d, c: (0, chunk(d, c), 0)),
             per_dir((nseq, 2 * n)), per_dir((1, n)), per_dir((1, n)), per_dir((1, n)),
             per_dir((D_BRANCH, n)), per_dir((D_BRANCH, n))]
    out_specs = [per_dir((nseq, 2 * n))]
    out_shape = [jax.ShapeDtypeStruct((2, nseq, 2 * n), F32)]
    scratch = [pltpu.VMEM((D_BRANCH, 2 * n), BF16), pltpu.VMEM((8, n), F32),
               pltpu.VMEM((S5_ROWS, 2 * n), BF16), pltpu.VMEM((nseq, 2 * n), F32),
               pltpu.VMEM((S5_ROWS, S5_ROWS), BF16)]
    if emit_y:
        args += [wl["s5_cre"], wl["s5_cim"]]
        specs += [per_dir((n, D_BRANCH)), per_dir((n, D_BRANCH))]
        out_specs = [pl.BlockSpec((1, nseq, steps, D_BRANCH), lambda d, c: (d, 0, chunk(d, c), 0))] + out_specs
        out_shape = [jax.ShapeDtypeStruct((2, nseq, nstep, D_BRANCH), F32)] + out_shape
        scratch += [pltpu.VMEM((2 * n, D_BRANCH), BF16)]
    return pl.pallas_call(
        functools.partial(_s5_scan_kernel, nseq, emit_y),
        grid=(2, nc),
        in_specs=specs, out_specs=out_specs, out_shape=out_shape, scratch_shapes=scratch,
        compiler_params=_cp("arbitrary", "arbitrary"),
        name="s5_scan" if emit_y else "s5_scan_finals",
    )(*args)


def _s5_chain_kernel(batch, nseg, f_ref, h0_ref, are_ref, aim_ref, ldt_ref, o_ref):
    d = pl.program_id(0)
    n = S5_N
    _, _, pr, pi = _s5_discretise(are_ref, aim_ref, ldt_ref)
    for _ in range(int(math.log2(S5_SEG))):
        pr, pi = pr * pr - pi * pi, 2.0 * pr * pi
    f = f_ref[0]
    fr, fi = f[:, 0:n], f[:, n:2 * n]
    h0 = h0_ref[0]
    h0r, h0i = h0[:, 0:n], h0[:, n:2 * n]
    nrow = batch * nseg
    seg = jnp.bitwise_and(lax.broadcasted_iota(jnp.int32, (nrow, 1), 0), nseg - 1)

    def run(shift, keep):
        xr, xi = h0r, h0i
        for _ in range(nseg - 1):
            zr = fr + pr * xr - pi * xi
            zi = fi + pr * xi + pi * xr
            xr = h0r + jnp.where(keep, pltpu.roll(zr, shift, 0), 0.0)
            xi = h0i + jnp.where(keep, pltpu.roll(zi, shift, 0), 0.0)
        o_ref[0, :, 0:n] = xr
        o_ref[0, :, n:2 * n] = xi

    @pl.when(d == 0)
    def _():
        run(1, seg != 0)

    @pl.when(d == 1)
    def _():
        run(nrow - 1, seg != nseg - 1)


def _s5_chain(fin, h0rows, wl, batch, nseg):
    nrow = batch * nseg
    n = S5_N
    per_dir = lambda shape: pl.BlockSpec((1,) + shape, lambda d: (d,) + (0,) * len(shape))
    return pl.pallas_call(
        functools.partial(_s5_chain_kernel, batch, nseg),
        grid=(2,),
        in_specs=[per_dir((nrow, 2 * n)), per_dir((nrow, 2 * n)), per_dir((1, n)), per_dir((1, n)), per_dir((1, n))],
        out_specs=per_dir((nrow, 2 * n)),
        out_shape=jax.ShapeDtypeStruct((2, nrow, 2 * n), F32),
        compiler_params=_cp("arbitrary"),
        name="s5_chain",
    )(fin, h0rows, wl["s5_are"], wl["s5_aim"], wl["s5_ldt"])


def _gla_kernel(seq_len, nb, n_aliased, q_ref, k_ref, v_ref, g_ref, gw_ref, gb_ref, s0_ref, *rest):
    o_ref, sfin_ref, qe_sc, upd_sc, dec_sc, sall_sc, lhs_sc, kt_sc, la_sc, oi_sc = rest[n_aliased:]
    d = pl.program_id(1)
    sign = 1 - 2 * d
    ck, sup = GLA_CHUNK, GLA_SUPER
    cps = sup // ck
    nsup, nchunk = seq_len // sup, seq_len // ck
    dk, dv = GLA_HEADS * GLA_DK, GLA_HEADS * GLA_DV
    r = lax.broadcasted_iota(jnp.int32, (sup, sup), 0)
    s = lax.broadcasted_iota(jnp.int32, (sup, sup), 1)
    same = lax.shift_right_logical(r, 6) == lax.shift_right_logical(s, 6)
    tri = jnp.logical_and(same, (s - r) * sign <= 0)
    cum_lhs = jnp.where(tri, 1.0, 0.0).astype(BF16)
    t4 = lax.broadcasted_iota(jnp.int32, (ck, GLA_HEADS * ck), 0)
    s4 = jnp.bitwise_and(lax.broadcasted_iota(jnp.int32, (ck, GLA_HEADS * ck), 1), ck - 1)
    tri4 = (s4 - t4) * sign <= 0
    pos = jnp.bitwise_and(lax.broadcasted_iota(jnp.int32, (ck, 1), 0), ck - 1)
    is_last = pos == (ck - 1) * (1 - d)
    row_chunk = lax.shift_right_logical(lax.broadcasted_iota(jnp.int32, (sup, 1), 0), 6)
    head_k = lax.shift_right_logical(lax.broadcasted_iota(jnp.int32, (1, dk), 1), 5)
    head_v = lax.shift_right_logical(lax.broadcasted_iota(jnp.int32, (1, dv), 1), 6)
    blockdiag = lax.shift_right_logical(lax.broadcasted_iota(jnp.int32, (dv, 1), 0), 6) == head_k

    def group_rows(u):
        return pl.ds(u * sup, sup) if isinstance(u, int) else pl.ds(pl.multiple_of(u * sup, sup), sup)

    def stage_a(u, slot):
        rows = group_rows(u)
        q = q_ref[rows, :] * (GLA_DK ** -0.5)
        k = k_ref[rows, :]
        v = v_ref[rows, :]
        cs = _dot(cum_lhs, la_sc[rows, :])
        yield
        bc = cs[:, 0:dk] + cs[:, dk:2 * dk]
        tots = [jnp.sum(jnp.where(is_last, bc[c * ck:(c + 1) * ck], 0.0), axis=0, keepdims=True)
                for c in range(cps)]
        tot = jnp.concatenate([jnp.broadcast_to(t, (ck, dk)) for t in tots], axis=0)
        ref = 0.5 * tot
        kt_sc[slot] = (k * jnp.exp(ref - bc)).astype(BF16)
        lhs_sc[slot] = (q * jnp.exp(bc - ref)).astype(BF16)
        qe_sc[rows, :] = (q * jnp.exp(bc)).astype(BF16)
        kl = (k * jnp.exp(tot - bc)).astype(BF16)
        zero = jnp.zeros_like(kl)
        klx = jnp.concatenate([jnp.where(row_chunk == c, kl, zero) for c in range(cps)], axis=1)
        upd = _dot_tn(v.astype(BF16), klx)
        yield
        for c in range(cps):
            upd_sc[u * cps + c] = jnp.where(blockdiag, upd[:, c * dk:(c + 1) * dk], 0.0)
            dec_sc[u * cps + c] = jnp.broadcast_to(jnp.exp(tots[c]), (8, dk))

    def stage_b(u, slot):
        rows = group_rows(u)
        v = v_ref[rows, :].astype(BF16)
        qt, kt = lhs_sc[slot], kt_sc[slot]
        zk, zv = jnp.zeros_like(kt[0:ck]), jnp.zeros_like(v[0:ck])
        p = [_dot_nt(qt[c * ck:(c + 1) * ck],
                     jnp.concatenate([jnp.where(head_k == h, kt[c * ck:(c + 1) * ck], zk)
                                      for h in range(GLA_HEADS)], axis=0)) for c in range(cps)]
        yield
        o = [_dot(jnp.where(tri4, p[c], 0.0).astype(BF16),
                  jnp.concatenate([jnp.where(head_v == h, v[c * ck:(c + 1) * ck], zv)
                                   for h in range(GLA_HEADS)], axis=0)) for c in range(cps)]
        yield
        oi_sc[rows, :] = jnp.concatenate(o, axis=0)

    def run(*stages):
        live = list(stages)
        while live:
            for g in list(live):
                if next(g, StopIteration) is StopIteration:
                    live.remove(g)

    x = _dot(g_ref[...].astype(BF16), gw_ref[0]) + gb_ref[0]
    la = (jnp.minimum(x, 0.0) - jnp.log(1.0 + jnp.exp(-jnp.abs(x)))) * (1.0 / GLA_TAU)
    la_sc[...] = jnp.concatenate(_split2(la), axis=1)

    nu = nb * nsup
    if nu <= 4:
        run(stage_a(0, 0))
        for u in range(nu):
            if u + 1 < nu:
                run(stage_b(u, u % 2), stage_a(u + 1, (u + 1) % 2))
            else:
                run(stage_b(u, u % 2))
    else:
        run(stage_a(0, 0))

        def sup_body(u, carry):
            nxt = jnp.minimum(u + 1, nu - 1)
            run(stage_b(u, jnp.bitwise_and(u, 1)), stage_a(nxt, jnp.bitwise_and(u + 1, 1)))
            return carry

        lax.fori_loop(0, nu, sup_body, 0)

    for j in range(nb):
        def state_body(c, st, j=j):
            ci = j * nchunk + c + d * (nchunk - 1 - 2 * c)
            sall_sc[ci] = st.astype(BF16)
            return dec_sc[ci][0:1, :] * st + upd_sc[ci]

        st_fin = jnp.transpose(lax.fori_loop(0, nchunk, state_body, s0_ref[j, 0]))
        for h in range(GLA_HEADS):
            sfin_ref[j, 0, 0, h] = st_fin[h * GLA_DK:(h + 1) * GLA_DK, h * GLA_DV:(h + 1) * GLA_DV]

    def inter(u):
        rows = group_rows(u)
        qe = qe_sc[rows, :]
        oi = jnp.concatenate([_dot_nt(qe[c * ck:(c + 1) * ck], sall_sc[u * cps + c]) for c in range(cps)], axis=0)
        o_ref[0, rows, :] = (oi_sc[rows, :] + oi).astype(BF16)

    if nu <= 4:
        for u in range(nu):
            inter(u)
    else:
        lax.fori_loop(0, nu, lambda u, carry: (inter(u), carry)[1], 0, unroll=2 if nu % 2 == 0 else 1)


def _gla(gla_in, mla_in, wl, s0t, batch, seq_len, fin=(0, None, 1)):
    layer, prev_fin, fin_layers = fin
    aliases = {} if prev_fin is None else {7: 1}
    extra = [] if prev_fin is None else [prev_fin]
    n = gla_in.shape[0]
    dk, dv = GLA_HEADS * GLA_DK, GLA_HEADS * GLA_DV
    nb = max(1, min(batch, 1024 // seq_len))
    rows = nb * seq_len
    nchunk = nb * (seq_len // GLA_CHUNK)
    return pl.pallas_call(
        functools.partial(_gla_kernel, seq_len, nb, len(extra)),
        grid=(batch // nb, 2),
        input_output_aliases=aliases,
        in_specs=[pl.BlockSpec((rows, dk), lambda b, d: (b, 0)),
                  pl.BlockSpec((rows, dk), lambda b, d: (b, 1)),
                  pl.BlockSpec((rows, dv), lambda b, d: (b, 1)),
                  pl.BlockSpec((rows, dk), lambda b, d: (b, 1)),
                  pl.BlockSpec((1, dk, dk), lambda b, d: (d, 0, 0)),
                  pl.BlockSpec((1, 1, dk), lambda b, d: (d, 0, 0)),
                  pl.BlockSpec((nb, 1, dv, dk), lambda b, d: (b, d, 0, 0))]
                 + [pl.BlockSpec(memory_space=pl.ANY)] * len(extra),
        out_specs=[pl.BlockSpec((1, rows, dv), lambda b, d: (d, b, 0)),
                   pl.BlockSpec((nb, 1, 1, GLA_HEADS, GLA_DK, GLA_DV), lambda b, d: (b, layer, d, 0, 0, 0))],
        out_shape=[jax.ShapeDtypeStruct((2, n, dv), BF16),
                   jax.ShapeDtypeStruct((batch, fin_layers, 2, GLA_HEADS, GLA_DK, GLA_DV), F32)],
        scratch_shapes=[pltpu.VMEM((rows, dk), BF16),
                        pltpu.VMEM((nchunk, dv, dk), F32),
                        pltpu.VMEM((nchunk, 8, dk), F32),
                        pltpu.VMEM((nchunk, dv, dk), BF16),
                        pltpu.VMEM((2, GLA_SUPER, dk), BF16),
                        pltpu.VMEM((2, GLA_SUPER, dk), BF16),
                        pltpu.VMEM((rows, 2 * dk), BF16),
                        pltpu.VMEM((rows, dv), F32)],
        compiler_params=_cp("arbitrary", "arbitrary"),
        name="gla",
    )(gla_in, gla_in, gla_in, mla_in, wl["gla_gw"], wl["gla_gb"], s0t, *extra)


def _outproj_kernel(x_ref, mod_ref, g_ref, om_ref, oh_ref, su_ref, sf_ref, sb_ref, sd_ref, sw_ref, sbias_ref,
                    gf_ref, gb_ref, gn_ref, hm_ref, w_ref, y_ref):
    c = D_BRANCH
    g = g_ref[...].astype(F32)
    acc = _dot((om_ref[...].astype(F32) * _silu(g[:, 0:c])).astype(BF16), w_ref[0:c, :])
    acc += _dot((oh_ref[...].astype(F32) * _silu(g[:, c:2 * c])).astype(BF16), w_ref[c:2 * c, :])
    ys = sd_ref[...] * su_ref[...] + sf_ref[0] + sb_ref[0]
    ge = 0.5 * ys * (1.0 + jnp.tanh(math.sqrt(2.0 / math.pi) * (ys + 0.044715 * (ys * ys * ys))))
    o_s5 = ge / (1.0 + jnp.exp(-(_dot(ge.astype(BF16), sw_ref[...]) + sbias_ref[...])))
    acc += _dot((o_s5 * _silu(g[:, 2 * c:3 * c])).astype(BF16), w_ref[2 * c:3 * c, :])
    og = gf_ref[0].astype(F32) + gb_ref[0].astype(F32)
    hi, lo = _split2(og * og)
    ms = _dot(hi, hm_ref[...]) + _dot(lo, hm_ref[...])
    ogn = og * lax.rsqrt(ms + EPS) * gn_ref[...]
    acc += _dot((ogn * _silu(g[:, 3 * c:4 * c])).astype(BF16), w_ref[3 * c:4 * c, :])
    y_ref[...] = x_ref[...] + mod_ref[0, 2:3, :] * acc


def _outproj(x, mod, mod_row, gates, o_mla, o_hy, s5_u, s5_y, o_gla, wl, tm):
    n, d = x.shape
    c = D_BRANCH
    row = lambda w: pl.BlockSpec((tm, w), lambda i: (i, 0))
    per_dir = lambda k: pl.BlockSpec((1, tm, c), lambda i: (k, i, 0))
    full = lambda *shape: pl.BlockSpec(shape, lambda i: (0,) * len(shape))
    return pl.pallas_call(
        _outproj_kernel,
        grid=(n // tm,),
        in_specs=[row(d),
                  pl.BlockSpec((1, 3, d), lambda i: (mod_row(i), 0, 0)),
                  row(d), row(c), row(c),
                  row(c), per_dir(0), per_dir(1), full(1, c), full(c, c), full(1, c),
                  per_dir(0), per_dir(1), full(1, c), full(c, c), full(d, d)],
        out_specs=row(d),
        out_shape=jax.ShapeDtypeStruct((n, d), F32),
        compiler_params=_cp("arbitrary"),
        name="outproj",
    )(x, mod, gates, o_mla, o_hy, s5_u, s5_y, s5_y, wl["s5_d"], wl["s5_glu_w"], wl["s5_glu_b"],
      o_gla, o_gla, wl["gla_norm"], wl["head_mean"], wl["w_out"])


def _rope_tables(seq_len):
    pos = np.arange(seq_len)
    inv = ROPE_BASE ** (-np.arange(0, 16, 2, dtype=np.float64) / 16.0)
    cos = np.ones((seq_len, HEAD_PAD))
    sin_a = np.zeros((seq_len, HEAD_PAD))
    sin_b = np.zeros((seq_len, HEAD_PAD))
    for base, p in ((MLA_NOPE, pos // GRID_W), (MLA_NOPE + 16, pos % GRID_W)):
        ang = p[:, None].astype(np.float64) * inv[None, :]
        cos[:, base:base + 8] = np.cos(ang)
        cos[:, base + 8:base + 16] = np.cos(ang)
        sin_a[:, base:base + 8] = -np.sin(ang)
        sin_b[:, base + 8:base + 16] = np.sin(ang)
    return tuple(jnp.asarray(t, F32) for t in (cos, sin_a, sin_b))


def _odd_dft(seq_len):
    bk = min(seq_len, HY_BLOCK)
    k = np.arange(bk)[:, None]
    t = np.arange(bk)[None, :]

    def mat(shift):
        ang = (np.pi / (2 * bk)) * (((2 * k + 1) * (t + shift)) % (4 * bk))
        return np.concatenate([np.cos(ang), -np.sin(ang)], axis=0)

    fo = mat(0)
    fb = -mat(bk)
    fb[:, 0] = 0.0
    const = lambda a: jnp.asarray(a, F32).astype(BF16)
    return const(fo), const(fb), const(fo.T)


def _hyena_tables(seq_len):
    lag = np.arange(-seq_len, seq_len)
    pos = np.where(lag == -seq_len, 0, np.abs(lag)).astype(np.float64)
    t = pos / seq_len
    w = 2.0 * np.pi * pos / seq_len
    bands = np.linspace(1e-4, HY_BANDS - 1, HY_BANDS)
    feat = np.zeros((2 * seq_len, HY_HIDDEN))
    feat[:, 0] = t
    feat[:, 1:1 + HY_BANDS] = np.cos(w[:, None] * bands)
    feat[:, 1 + HY_BANDS:HY_FEAT] = np.sin(w[:, None] * bands)
    feat = feat.reshape(-1, 2, 128, HY_HIDDEN).transpose(0, 2, 1, 3).reshape(seq_len, 2 * HY_HIDDEN)
    deltas = np.linspace(math.log(1.0 / HY_TARGET) / HY_FAST_DECAY, math.log(1.0 / HY_TARGET) / HY_SLOW_DECAY,
                         D_BRANCH)
    win = np.exp(-t[:, None] * deltas[None, :]) + HY_SHIFT
    return jnp.asarray(feat, F32), jnp.asarray(win, F32)


def _pad_to(a, shape):
    return jnp.pad(a, [(0, s - d) for s, d in zip(shape, a.shape)])


def _layer_weights(l, p):
    z = lambda *s: jnp.zeros(s, F32)
    w_in = p["w_in"][l]
    col = lambda lo, hi: w_in[:, lo:hi]
    d = D_MODEL
    w_p = jnp.concatenate([
        col(0, 192), col(320, 352), col(2656, 2688), col(192, 320),
        col(352, 608), col(1376, 1632), col(1888, 2144), col(2688, 2944),
        col(608, 1376), col(1632, 1888),
        col(2144, 2272), col(2272, 2400), col(2400, 2656)], axis=1).astype(BF16)
    wl = {"w_in": w_p, "norm_w": p["norm_w"][l].reshape(1, d), "w_out": p["w_out"][l].astype(BF16)}
    wl["qa_norm"] = _pad_to(p["mla_qa_norm"][l].reshape(1, -1), (1, 256))
    w_uq = _pad_to(p["mla_w_uq"][l].reshape(MLA_Q_RANK, MLA_HEADS, MLA_QK), (256, MLA_HEADS, HEAD_PAD))
    wl["w_uq"] = w_uq.reshape(256, MLA_HEADS * HEAD_PAD).astype(BF16)
    wl["q_norm"] = _pad_to(p["mla_q_norm"][l].reshape(1, -1), (1, HEAD_PAD))
    wl["k_norm"] = _pad_to(p["mla_k_norm"][l].reshape(1, -1), (1, HEAD_PAD))
    wl["kva_norm"] = p["mla_kva_norm"][l].reshape(1, -1)
    w_ukv = p["mla_w_ukv"][l].reshape(MLA_KV_RANK, MLA_HEADS, MLA_NOPE + MLA_V)
    wl["w_uk"] = _pad_to(w_ukv[:, :, :MLA_NOPE], (MLA_KV_RANK, MLA_HEADS, HEAD_PAD)).reshape(MLA_KV_RANK, -1).astype(BF16)
    wl["w_uv"] = w_ukv[:, :, MLA_NOPE:].reshape(MLA_KV_RANK, MLA_HEADS * MLA_V).astype(BF16)
    wl["hy_conv_w"] = p["hy_conv_w"][l]
    wl["hy_conv_b"] = p["hy_conv_b"][l].reshape(1, -1)
    hh = HY_HIDDEN
    twice = lambda a: jnp.tile(a.reshape(1, -1), (1, 2))
    bdiag = lambda a: jnp.concatenate([jnp.pad(a, ((0, 0), (0, a.shape[1]))), jnp.pad(a, ((0, 0), (a.shape[1], 0)))], axis=0)
    wl["hy_w1"] = bdiag(_pad_to(p["hy_w1"][l], (hh, hh)))
    wl["hy_b1"], wl["hy_f1"] = twice(p["hy_b1"][l]), twice(p["hy_freq1"][l])
    wl["hy_w2"] = bdiag(p["hy_w2"][l])
    wl["hy_b2"], wl["hy_f2"] = twice(p["hy_b2"][l]), twice(p["hy_freq2"][l])
    w3 = p["hy_w3"][l]
    wl["hy_w3"] = jnp.concatenate([bdiag(w3[:, 0:512]), bdiag(w3[:, 512:1024])], axis=1)
    wl["hy_bias"] = p["hy_bias"][l]
    flat = lambda a: a[l].reshape(2, 1, S5_N)
    wl["s5_are"], wl["s5_aim"] = flat(p["s5_a_re"]), flat(p["s5_a_im"])
    wl["s5_ldt"] = jnp.repeat(p["s5_log_dt"][l], S5_STATE, axis=-1).reshape(2, 1, S5_N)
    same_group = jnp.asarray((np.arange(D_BRANCH) // S5_GROUP)[:, None] == (np.arange(S5_N) // S5_STATE)[None, :])
    bd_b = lambda a: jnp.where(same_group, jnp.tile(
        a[l].transpose(0, 1, 3, 2).reshape(2, D_BRANCH, S5_STATE), (1, 1, S5_GROUPS)), 0.0)
    bd_c = lambda a: jnp.where(same_group.T, jnp.tile(
        a[l].transpose(0, 1, 3, 2).reshape(2, S5_N, S5_GROUP), (1, 1, S5_GROUPS)), 0.0)
    wl["s5_bre"], wl["s5_bim"] = bd_b(p["s5_b_re"]), bd_b(p["s5_b_im"])
    wl["s5_cre"], wl["s5_cim"] = bd_c(p["s5_c_re"]), bd_c(p["s5_c_im"])
    wl["s5_d"] = p["s5_d"][l].reshape(1, -1)
    wl["s5_glu_w"] = p["s5_glu_w"][l].astype(BF16)
    wl["s5_glu_b"] = p["s5_glu_b"][l].reshape(1, -1)
    gw = p["gla_gw"][l]
    dk = GLA_HEADS * GLA_DK
    wl["gla_gw"] = jnp.stack([_pad_to(jnp.pad(gw[i], ((GLA_G_LANE + GLA_RANK * i, 0), (0, 0))), (dk, dk))
                              for i in range(2)]).astype(BF16)
    wl["gla_gb"] = p["gla_gb"][l].reshape(2, 1, dk)
    wl["gla_norm"] = jnp.tile(p["gla_norm"][l], GLA_HEADS).reshape(1, -1)
    head = np.arange(D_BRANCH) // GLA_DV
    wl["head_mean"] = jnp.asarray((head[:, None] == head[None, :]) / GLA_DV, BF16)
    return wl


def _hyena_filters(wl, tabs):
    feat, win, fo, fb, _ = tabs
    kern_lin, nrm = _hy_mlp(feat, wl, win)
    return _hy_kspec(kern_lin, nrm, fo, fb)


def _trunk_layer(x, mod, mod_row, wl, batch, seq_len, hy_tabs, rope_tabs=None, ctx=None, layer=0, cache_bufs=None):
    n = batch * seq_len
    tm = 512
    mla_in, gates, hy_in, s5_in, gla_in = _inproj(x, mod, mod_row, wl["norm_w"], wl["w_in"], tm)

    if ctx is None:
        prev = None if cache_bufs is None else cache_bufs[:2]
        q, k, v, ckv, krope = _mla_prep(mla_in, wl, rope_tabs, seq_len, tm, True, (layer, prev))
    else:
        q, k, v = _mla_prep(mla_in, wl, rope_tabs, seq_len, tm, True)
        ckv = krope = None
    kv_parts = [(k, v, seq_len)]
    if ctx is not None:
        k_ctx, v_ctx = _mla_prep(ctx["mla"], wl, None, ctx["past"], 512, False)
        kv_parts = [(k_ctx, v_ctx, ctx["past"])] + kv_parts
    o_mla = _attention(q, kv_parts, batch, seq_len, 256)

    o_hy = _hyena(hy_in, wl, _hyena_filters(wl, hy_tabs), hy_tabs[2], hy_tabs[4], batch, seq_len)

    nseg = seq_len // S5_SEG
    nseq = batch * nseg
    u_seg = s5_in.reshape(nseq, S5_SEG, D_BRANCH)
    if ctx is None:
        hin = jnp.zeros((2, nseq, 2 * S5_N), F32)
    else:
        (fin,) = _s5_scan(u_seg, jnp.zeros((2, nseq, 2 * S5_N), F32), wl, False)
        hin = _s5_chain(fin, ctx["s5_h0"], wl, batch, nseg)
    y2, s5_fin = _s5_scan(u_seg, hin, wl, True)

    s0 = jnp.zeros((batch, 2, GLA_HEADS * GLA_DV, GLA_HEADS * GLA_DK), F32) if ctx is None else ctx["gla_s0"]
    fin = (0, None, 1) if ctx is not None else (layer, None if cache_bufs is None else cache_bufs[2], DEPTH)
    o_gla, gla_fin = _gla(gla_in, mla_in, wl, s0, batch, seq_len, fin)

    y = _outproj(x, mod, mod_row, gates, o_mla, o_hy, s5_in, y2.reshape(2, n, D_BRANCH), o_gla, wl, tm)
    return y, (ckv, krope, s5_fin, gla_fin)


def kernel(x_prompt, x_sample, c, cache_mla_ckv, cache_mla_krope, state_s5, state_gla, c_ctx, norm_w, ada_w, ada_b, w_in, w_out, mla_qa_norm, mla_kva_norm, mla_w_uq, mla_w_ukv, mla_q_norm, mla_k_norm, hy_conv_w, hy_conv_b, hy_w1, hy_b1, hy_freq1, hy_w2, hy_b2, hy_freq2, hy_w3, hy_bias, s5_a_re, s5_a_im, s5_log_dt, s5_b_re, s5_b_im, s5_c_re, s5_c_im, s5_d, s5_glu_w, s5_glu_b, gla_gw, gla_gb, gla_norm):
    params = dict(norm_w=norm_w, w_in=w_in, w_out=w_out, mla_qa_norm=mla_qa_norm, mla_kva_norm=mla_kva_norm,
                  mla_w_uq=mla_w_uq, mla_w_ukv=mla_w_ukv, mla_q_norm=mla_q_norm, mla_k_norm=mla_k_norm,
                  hy_conv_w=hy_conv_w, hy_conv_b=hy_conv_b, hy_w1=hy_w1, hy_b1=hy_b1, hy_freq1=hy_freq1,
                  hy_w2=hy_w2, hy_b2=hy_b2, hy_freq2=hy_freq2, hy_w3=hy_w3, hy_bias=hy_bias,
                  s5_a_re=s5_a_re, s5_a_im=s5_a_im, s5_log_dt=s5_log_dt, s5_b_re=s5_b_re, s5_b_im=s5_b_im,
                  s5_c_re=s5_c_re, s5_c_im=s5_c_im, s5_d=s5_d, s5_glu_w=s5_glu_w, s5_glu_b=s5_glu_b,
                  gla_gw=gla_gw, gla_gb=gla_gb, gla_norm=gla_norm)
    bp, lp, d = x_prompt.shape
    bs, ls, _ = x_sample.shape
    past = cache_mla_ckv.shape[2]
    n_s5 = S5_N

    conds = jnp.concatenate([c_ctx[None, :], c, jnp.zeros((8 - 1 - bs, d), F32)], axis=0)
    mods = _modulation(conds, ada_w, ada_b).reshape(DEPTH, 8, 3, d)

    tabs_p = _hyena_tables(lp) + _odd_dft(lp)
    tabs_s = _hyena_tables(ls) + _odd_dft(ls)
    rope_tabs = _rope_tables(ls)
    tm_s = 512
    nseg = ls // S5_SEG

    y_p = x_prompt.reshape(bp * lp, d)
    y_s = x_sample.reshape(bs * ls, d)
    s5_l = []
    cache_bufs = (jnp.zeros((bp, DEPTH, lp, MLA_KV_RANK), F32), jnp.zeros((bp, DEPTH, lp, MLA_ROPE), F32),
                  jnp.zeros((bp, DEPTH, 2, GLA_HEADS, GLA_DK, GLA_DV), F32))
    for l in range(DEPTH):
        wl = _layer_weights(l, params)
        y_p, (ckv, krope, s5_fin, gla_fin) = _trunk_layer(y_p, mods[l], lambda i: 0, wl, bp, lp, tabs_p,
                                                          layer=l, cache_bufs=cache_bufs)
        cache_bufs = (ckv, krope, gla_fin)
        s5_l.append(jnp.stack([s5_fin[:, :, :n_s5], s5_fin[:, :, n_s5:]], axis=-1)
                    .reshape(2, bp, S5_GROUPS, S5_STATE, 2).transpose(1, 0, 2, 3, 4))

        mla_ctx = jnp.concatenate([cache_mla_ckv[:, l], jnp.zeros((bs, past, 64), F32), cache_mla_krope[:, l],
                                   jnp.zeros((bs, past, 32), F32)], axis=-1).reshape(bs * past, 256)
        st = state_s5[:, l]
        h0 = jnp.concatenate([st[..., 0].reshape(bs, 2, n_s5), st[..., 1].reshape(bs, 2, n_s5)], axis=-1)
        h0 = h0.transpose(1, 0, 2)
        h0rows = jnp.zeros((2, nseg * bs, 2 * n_s5), F32)
        h0rows = h0rows.at[0, 0::nseg].set(h0[0]).at[1, nseg - 1::nseg].set(h0[1])
        eye_h = jnp.eye(GLA_HEADS, dtype=F32)
        gla_s0 = jnp.einsum("bdhke,hg->bdhegk", state_gla[:, l], eye_h).reshape(
            bs, 2, GLA_HEADS * GLA_DV, GLA_HEADS * GLA_DK)
        ctx = {"mla": mla_ctx, "past": past, "s5_h0": h0rows, "gla_s0": gla_s0}
        y_s, _ = _trunk_layer(y_s, mods[l], lambda i: 1 + (i * tm_s) // ls, wl, bs, ls, tabs_s, rope_tabs, ctx)

    return (y_p.reshape(bp, lp, d), y_s.reshape(bs, ls, d),
            cache_bufs[0], cache_bufs[1], jnp.stack(s5_l, axis=1), cache_bufs[2])
```

```python
import functools
import math

import numpy as np
import jax
import jax.numpy as jnp
from jax import lax
from jax.experimental import pallas as pl
from jax.experimental.pallas import tpu as pltpu

F32 = jnp.float32
BF16 = jnp.bfloat16

D_MODEL = 1024
DEPTH = 2
GRID_W = 64
D_BRANCH = 256
EPS = 1e-6

MLA_HEADS = 4
MLA_Q_RANK = 192
MLA_KV_RANK = 128
MLA_NOPE = 64
MLA_ROPE = 32
MLA_QK = 96
MLA_V = 64
ROPE_BASE = 10000.0
HEAD_PAD = 128

HY_BANDS = 16
HY_FEAT = 33
HY_HIDDEN = 64
HY_SHIFT = 0.05
HY_FAST_DECAY = 0.3
HY_SLOW_DECAY = 1.5
HY_TARGET = 1e-2
HY_BLOCK = 512

S5_GROUP = 16
S5_GROUPS = 16
S5_STATE = 64
S5_N = S5_GROUPS * S5_STATE
S5_ROWS = 512
S5_SEG = 256

GLA_HEADS = 4
GLA_DK = 32
GLA_DV = 64
GLA_RANK = 16
GLA_TAU = 16.0
GLA_CHUNK = 64
GLA_SUPER = 256

SEG_MLA = (0, 384)
SEG_GATE = (384, 1408)
SEG_HY = (1408, 2176)
SEG_S5 = (2176, 2432)
SEG_GLA = (2432, 2944)
N_PROJ = 2944
GLA_G_LANE = 96

VMEM_LIMIT = 48 * 1024 * 1024


def _cp(*sem):
    return pltpu.CompilerParams(dimension_semantics=sem, vmem_limit_bytes=VMEM_LIMIT)


def _dot(a, b):
    return jnp.dot(a, b, preferred_element_type=F32)


def _dot_nt(a, b):
    return lax.dot_general(a, b, (((1,), (1,)), ((), ())), preferred_element_type=F32)


def _dot_tn(a, b):
    return lax.dot_general(a, b, (((0,), (0,)), ((), ())), preferred_element_type=F32)


def _split2(x):
    hi = x.astype(BF16)
    lo = (x - hi.astype(F32)).astype(BF16)
    return hi, lo


def _split3(x):
    h1 = x.astype(BF16)
    r1 = x - h1.astype(F32)
    h2 = r1.astype(BF16)
    h3 = (r1 - h2.astype(F32)).astype(BF16)
    return h1, h2, h3


def _dot3(a, b):
    a1, a2 = _split2(a)
    b1, b2 = _split2(b)
    return _dot(a1, b1) + (_dot(a1, b2) + _dot(a2, b1))


def _silu(z):
    return z / (1.0 + jnp.exp(-z))


def _mod_kernel(c_ref, w_ref, b_ref, o_ref):
    s = _silu(c_ref[...])
    o_ref[0] = _dot(s.astype(BF16), w_ref[0].astype(BF16)) + b_ref[0]


def _modulation(conds, ada_w, ada_b):
    d = D_MODEL
    return pl.pallas_call(
        _mod_kernel,
        grid=(DEPTH, 3),
        in_specs=[pl.BlockSpec((8, d), lambda l, j: (0, 0)),
                  pl.BlockSpec((1, d, d), lambda l, j: (l, 0, j)),
                  pl.BlockSpec((1, 1, d), lambda l, j: (l, 0, j))],
        out_specs=pl.BlockSpec((1, 8, d), lambda l, j: (l, 0, j)),
        out_shape=jax.ShapeDtypeStruct((DEPTH, 8, 3 * d), F32),
        compiler_params=_cp("arbitrary", "arbitrary"),
        name="modulation",
    )(conds, ada_w, ada_b.reshape(DEPTH, 1, 3 * d))


def _inproj_kernel(x_ref, mod_ref, nw_ref, w_ref, o_mla, o_g, o_hy, o_s5, o_gla):
    x = x_ref[...]
    ms = jnp.mean(x * x, axis=-1, keepdims=True)
    y = x * lax.rsqrt(ms + EPS) * nw_ref[...]
    h = (y * (1.0 + mod_ref[0, 1:2, :]) + mod_ref[0, 0:1, :]).astype(BF16)
    for o, (lo, hi) in ((o_mla, SEG_MLA), (o_g, SEG_GATE), (o_hy, SEG_HY), (o_s5, SEG_S5), (o_gla, SEG_GLA)):
        o[...] = _dot(h, w_ref[:, lo:hi]).astype(o.dtype)


def _inproj(x, mod, mod_row, norm_w, w_p, tm):
    n, d = x.shape
    widths = [hi - lo for lo, hi in (SEG_MLA, SEG_GATE, SEG_HY, SEG_S5, SEG_GLA)]
    dtypes = [F32, BF16, F32, F32, F32]
    return pl.pallas_call(
        _inproj_kernel,
        grid=(n // tm,),
        in_specs=[pl.BlockSpec((tm, d), lambda i: (i, 0)),
                  pl.BlockSpec((1, 3, d), lambda i: (mod_row(i * tm), 0, 0)),
                  pl.BlockSpec((1, d), lambda i: (0, 0)),
                  pl.BlockSpec((d, N_PROJ), lambda i: (0, 0))],
        out_specs=[pl.BlockSpec((tm, w), lambda i: (i, 0)) for w in widths],
        out_shape=[jax.ShapeDtypeStruct((n, w), t) for w, t in zip(widths, dtypes)],
        compiler_params=_cp("arbitrary"),
        name="inproj",
    )(x, mod, norm_w, w_p)


def _head_norm(xh, w):
    ms = jnp.sum(xh * xh, axis=-1, keepdims=True) * (1.0 / MLA_QK)
    return xh * lax.rsqrt(ms + EPS) * w


def _rope(xh, cos, sin_a, sin_b):
    return xh * cos + pltpu.roll(xh, HEAD_PAD - 8, 1) * sin_a + pltpu.roll(xh, 8, 1) * sin_b


def _mla_prep_kernel(has_q, rope, cache_seqs, n_aliased, *refs):
    refs = list(refs)
    m_ref = refs.pop(0)
    if has_q:
        qan_ref, wuq_ref, qn_ref, kvn_ref = refs[:4]
        refs = refs[4:]
    wuk_ref, wuv_ref, kn_ref = refs[:3]
    refs = refs[3:]
    if rope:
        cos_ref, sa_ref, sb_ref = refs[:3]
        refs = refs[3:]
        cos, sa, sb = cos_ref[...], sa_ref[...], sb_ref[...]
    refs = refs[n_aliased:]
    if has_q:
        q_ref = refs.pop(0)
    k_ref, v_ref = refs[:2]
    if cache_seqs:
        ckv_ref, kro_ref = refs[2:]
    m = m_ref[...]
    if has_q:
        lane = lax.broadcasted_iota(jnp.int32, (1, HEAD_PAD), 1)
        mixed = m[:, 128:256]
        cq = jnp.concatenate([m[:, 0:128], jnp.where(lane < MLA_NOPE, mixed, 0.0)], axis=1)
        ms = jnp.sum(cq * cq, axis=-1, keepdims=True) * (1.0 / MLA_Q_RANK)
        cqn = cq * lax.rsqrt(ms + EPS) * qan_ref[...]
        q = _dot(cqn.astype(BF16), wuq_ref[...])
        ckv = m[:, 256:384]
        ckvn = ckv * lax.rsqrt(jnp.mean(ckv * ckv, axis=-1, keepdims=True) + EPS) * kvn_ref[...]
        kr = jnp.where(jnp.logical_and(lane >= MLA_NOPE, lane < MLA_NOPE + MLA_ROPE), mixed, 0.0)
        if cache_seqs:
            seq_len = ckv_ref.shape[2]
            for s in range(cache_seqs):
                ckv_ref[s, 0] = ckvn[s * seq_len:(s + 1) * seq_len]
                kro_ref[s, 0] = kr[s * seq_len:(s + 1) * seq_len, MLA_NOPE:MLA_NOPE + MLA_ROPE]
    else:
        ckvn = m[:, 0:128]
        kr = m[:, 128:256]
    cb = ckvn.astype(BF16)
    kup = _dot(cb, wuk_ref[...])
    v_ref[...] = _dot(cb, wuv_ref[...]).astype(BF16)
    for h in range(MLA_HEADS):
        sl = slice(HEAD_PAD * h, HEAD_PAD * (h + 1))
        kh = _head_norm(kup[:, sl] + kr, kn_ref[...])
        if rope:
            kh = _rope(kh, cos, sa, sb)
        k_ref[:, sl] = kh.astype(BF16)
        if has_q:
            qh = _head_norm(q[:, sl], qn_ref[...])
            if rope:
                qh = _rope(qh, cos, sa, sb)
            q_ref[:, sl] = (qh * (MLA_QK ** -0.5)).astype(BF16)


def _mla_prep(m, wl, rope_tabs, seq_len, tm, has_q, cache=None):
    n, wm = m.shape
    rope = rope_tabs is not None
    full = lambda shape: pl.BlockSpec(shape, lambda i: (0,) * len(shape))
    args, specs = [m], [pl.BlockSpec((tm, wm), lambda i: (i, 0))]
    if has_q:
        args += [wl["qa_norm"], wl["w_uq"], wl["q_norm"], wl["kva_norm"]]
        specs += [full((1, 256)), full((256, 512)), full((1, 128)), full((1, 128))]
    args += [wl["w_uk"], wl["w_uv"], wl["k_norm"]]
    specs += [full((128, 512)), full((128, 256)), full((1, 128))]
    if rope:
        nt = seq_len // tm
        args += list(rope_tabs)
        specs += [pl.BlockSpec((tm, HEAD_PAD), lambda i: (i % nt, 0))] * 3
    row = lambda w: pl.BlockSpec((tm, w), lambda i: (i, 0))
    out_specs = [row(512), row(256)]
    out_shape = [jax.ShapeDtypeStruct((n, 512), BF16), jax.ShapeDtypeStruct((n, 256), BF16)]
    aliases = {}
    nseq = 0
    if has_q:
        out_specs = [row(512)] + out_specs
        out_shape = [jax.ShapeDtypeStruct((n, 512), BF16)] + out_shape
    if cache is not None:
        layer, prev = cache
        nseq = tm // seq_len
        for w in (MLA_KV_RANK, MLA_ROPE):
            out_specs.append(pl.BlockSpec((nseq, 1, seq_len, w), lambda i: (i, layer, 0, 0)))
            out_shape.append(jax.ShapeDtypeStruct((n // seq_len, DEPTH, seq_len, w), F32))
        if prev is not None:
            for k, buf in enumerate(prev):
                aliases[len(args)] = len(out_shape) - 2 + k
                args.append(buf)
                specs.append(pl.BlockSpec(memory_space=pl.ANY))
    return pl.pallas_call(
        functools.partial(_mla_prep_kernel, has_q, rope, nseq, len(aliases)),
        grid=(n // tm,),
        in_specs=specs, out_specs=out_specs, out_shape=out_shape,
        input_output_aliases=aliases,
        compiler_params=_cp("arbitrary"),
        name="mla_prep",
    )(*args)


def _attn_kernel(nparts, nseq, q_ref, *refs):
    kv = [(refs[2 * i], refs[2 * i + 1]) for i in range(nparts)]
    o_ref = refs[2 * nparts]
    tq = q_ref.shape[0] // nseq
    low = lax.broadcasted_iota(jnp.int32, (1, HEAD_PAD), 1) < MLA_V
    units = [(s, h) for s in range(nseq) for h in range(MLA_HEADS)]

    def keys(ref, s):
        lk = ref.shape[0] // nseq
        return slice(s * lk, (s + 1) * lk)

    def scores(s, h):
        sl = slice(HEAD_PAD * h, HEAD_PAD * (h + 1))
        return [_dot_nt(q_ref[s * tq:(s + 1) * tq, sl], k_ref[keys(k_ref, s), sl]) for k_ref, _ in kv]

    s_next = scores(*units[0])
    acc = None
    for n, (s, h) in enumerate(units):
        pair, j = divmod(h, 2)
        sc = s_next
        if n + 1 < len(units):
            s_next = scores(*units[n + 1])
        if j == 0:
            v_half = []
            for _, v_ref in kv:
                vp = v_ref[keys(v_ref, s), HEAD_PAD * pair:HEAD_PAD * (pair + 1)]
                zero = jnp.zeros_like(vp)
                v_half.append((jnp.where(low, vp, zero), jnp.where(low, zero, vp)))
        m = functools.reduce(jnp.maximum, [jnp.max(x, axis=-1, keepdims=True) for x in sc])
        p = [jnp.exp(x - m) for x in sc]
        den = functools.reduce(jnp.add, [jnp.sum(x, axis=-1, keepdims=True) for x in p])
        num = functools.reduce(jnp.add, [_dot(x.astype(BF16), vh[j]) for x, vh in zip(p, v_half)])
        o = num / den
        acc = o if j == 0 else acc + o
        if j == 1:
            o_ref[s * tq:(s + 1) * tq, HEAD_PAD * pair:HEAD_PAD * (pair + 1)] = acc.astype(BF16)


def _attention(q, kv_parts, batch, lq, tq):
    nq = lq // tq
    nseq = max(1, min(batch, 1024 // lq)) if nq == 1 else 1
    args, specs = [q], [pl.BlockSpec((nseq * tq, 512), lambda b, i: (b * nq + i, 0))]
    for k, v, lk in kv_parts:
        args += [k, v]
        specs += [pl.BlockSpec((nseq * lk, 512), lambda b, i: (b, 0)),
                  pl.BlockSpec((nseq * lk, 256), lambda b, i: (b, 0))]
    return pl.pallas_call(
        functools.partial(_attn_kernel, len(kv_parts), nseq),
        grid=(batch // nseq, nq),
        in_specs=specs,
        out_specs=pl.BlockSpec((nseq * tq, 256), lambda b, i: (b * nq + i, 0)),
        out_shape=jax.ShapeDtypeStruct((batch * lq, 256), BF16),
        compiler_params=_cp("arbitrary", "arbitrary"),
        name="attention",
    )(*args)


def _hyena_kernel(seq_len, bk, nseq, x_ref, cw_ref, cb_ref, fo_ref, go_ref, k_ref, bias_ref, o_ref,
                  u_sc, y_sc, z_sc):
    c = D_BRANCH
    n = nseq * seq_len
    nblk = seq_len // bk
    pos = jnp.bitwise_and(lax.broadcasted_iota(jnp.int32, (n, 1), 0), seq_len - 1)
    first, last = pos == 0, pos == seq_len - 1

    def short_conv(g):
        cols = slice(g * c, (g + 1) * c)
        x = x_ref[:, cols]
        xm = jnp.where(first, 0.0, pltpu.roll(x, 1, 0))
        xp = jnp.where(last, 0.0, pltpu.roll(x, n - 1, 0))
        return cw_ref[0:1, cols] * xm + cw_ref[1:2, cols] * x + cw_ref[2:3, cols] * xp + cb_ref[:, cols]

    fo, go = fo_ref[...], go_ref[...]

    def long_conv(s, v, order, emit):
        cols = slice(order * c, (order + 1) * c)
        for j in range(nblk):
            u_sc[s, j] = _dot(fo, v[j * bk:(j + 1) * bk].astype(BF16))
        yield
        rc = 32
        for i in range(nblk):
            def mix(r, carry, i=i):
                top = pl.ds(pl.multiple_of(r * rc, rc), rc)
                bot = pl.ds(pl.multiple_of(bk + r * rc, rc), rc)
                at = ab = None
                for j in range(nblk):
                    q = i - j + nblk - 1
                    kt, kb = k_ref[q, top, cols], k_ref[q, bot, cols]
                    ut, ub = u_sc[s, j, top, :], u_sc[s, j, bot, :]
                    pt, pb = ut * kt - ub * kb, ut * kb + ub * kt
                    at, ab = (pt, pb) if at is None else (at + pt, ab + pb)
                z_sc[s, top, :] = at.astype(BF16)
                z_sc[s, bot, :] = ab.astype(BF16)
                return carry

            lax.fori_loop(0, bk // rc, mix, 0, unroll=True if nblk == 1 else 2)
            y = _dot(go, z_sc[s])
            yield
            emit(i, y)

    v_all, x1_all, x2_all = short_conv(0), short_conv(1), short_conv(2)

    def sequence(s):
        base = s * seq_len
        v = v_all[base:base + seq_len]

        def emit1(i, y):
            r = slice(i * bk, (i + 1) * bk)
            y_sc[s, r, :] = x1_all[base + i * bk:base + (i + 1) * bk] * (y + bias_ref[0:1, :] * v[r])

        yield from long_conv(s, v, 0, emit1)
        y1 = y_sc[s]

        def emit2(i, y):
            r = slice(i * bk, (i + 1) * bk)
            o_ref[base + i * bk:base + (i + 1) * bk, :] = (
                x2_all[base + i * bk:base + (i + 1) * bk] * (y + bias_ref[1:2, :] * y1[r])).astype(BF16)

        yield from long_conv(s, y1, 1, emit2)

    live = [sequence(s) for s in range(nseq)]
    while live:
        for g in list(live):
            if next(g, StopIteration) is StopIteration:
                live.remove(g)


def _hyena(x, wl, kspec, fo, go, batch, seq_len):
    c = D_BRANCH
    bk = fo.shape[1]
    nseq = max(1, min(batch, 1024 // seq_len))
    rows = nseq * seq_len
    nblk = seq_len // bk
    full = lambda a: pl.BlockSpec(a.shape, lambda i: (0,) * a.ndim)
    return pl.pallas_call(
        functools.partial(_hyena_kernel, seq_len, bk, nseq),
        grid=(batch // nseq,),
        in_specs=[pl.BlockSpec((rows, 3 * c), lambda i: (i, 0)),
                  full(wl["hy_conv_w"]), full(wl["hy_conv_b"]), full(fo), full(go),
                  pl.BlockSpec(kspec.shape, lambda i: (0, 0, 0), pipeline_mode=pl.Buffered(1)),
                  full(wl["hy_bias"])],
        out_specs=pl.BlockSpec((rows, c), lambda i: (i, 0)),
        out_shape=jax.ShapeDtypeStruct((batch * seq_len, c), BF16),
        scratch_shapes=[pltpu.VMEM((nseq, nblk, 2 * bk, c), F32), pltpu.VMEM((nseq, seq_len, c), F32),
                        pltpu.VMEM((nseq, 2 * bk, c), BF16)],
        compiler_params=pltpu.CompilerParams(dimension_semantics=("arbitrary",), vmem_limit_bytes=56 * 1024 * 1024),
        name="hyena",
    )(x, wl["hy_conv_w"], wl["hy_conv_b"], fo, go, kspec, wl["hy_bias"])


def _hy_mlp_kernel(feat_ref, w1_ref, b1_ref, f1_ref, w2_ref, b2_ref, f2_ref, w3_ref, win_ref,
                   kern_ref, nrm_ref):
    i = pl.program_id(0)
    tl = win_ref.shape[0]
    h = jnp.sin(f1_ref[...] * (_dot3(feat_ref[...], w1_ref[...]) + b1_ref[...]))
    h = jnp.sin(f2_ref[...] * (_dot3(h, w2_ref[...]) + b2_ref[...]))
    filt = _dot3(h, w3_ref[...])
    filt = jnp.concatenate([filt[:, 0:512], filt[:, 512:1024]], axis=0)
    win = win_ref[...]
    row0 = (lax.broadcasted_iota(jnp.int32, (tl, 1), 0) + i * tl) == 0
    filt = jnp.where(row0, 0.0, filt * jnp.concatenate([win, win], axis=1))
    kern_ref[...] = filt
    part = jnp.sum(jnp.abs(filt), axis=0, keepdims=True)

    @pl.when(i == 0)
    def _():
        nrm_ref[...] = jnp.zeros_like(nrm_ref)
    nrm_ref[...] += jnp.broadcast_to(part, nrm_ref.shape)


def _hy_mlp(feat, wl, win):
    rows = win.shape[0]
    tl = 256
    nl = rows // (2 * tl)
    full = lambda shape: pl.BlockSpec(shape, lambda i: (0,) * len(shape))
    return pl.pallas_call(
        _hy_mlp_kernel,
        grid=(rows // tl,),
        in_specs=[pl.BlockSpec((tl // 2, 128), lambda i: (i, 0)),
                  full((128, 128)), full((1, 128)), full((1, 128)),
                  full((128, 128)), full((1, 128)), full((1, 128)),
                  pl.BlockSpec((128, 1024), lambda i: (0, jnp.where(i < nl, 1, 0))),
                  pl.BlockSpec((tl, 256), lambda i: (i, 0))],
        out_specs=[pl.BlockSpec((tl, 512), lambda i: (i, 0)), full((8, 512))],
        out_shape=[jax.ShapeDtypeStruct((rows, 512), F32), jax.ShapeDtypeStruct((8, 512), F32)],
        compiler_params=_cp("arbitrary"),
        name="hy_mlp",
    )(feat, wl["hy_w1"], wl["hy_b1"], wl["hy_f1"], wl["hy_w2"], wl["hy_b2"], wl["hy_f2"], wl["hy_w3"], win)


def _hy_kspec_kernel(lo_ref, hi_ref, fo_ref, fb_ref, n_ref, o_ref):
    bk = lo_ref.shape[0]
    k = _dot(fo_ref[...], hi_ref[...].astype(BF16)) + _dot(fb_ref[...], lo_ref[...].astype(BF16))
    o_ref[0] = k * ((1.0 / bk) / n_ref[0:1, :])


def _hy_kspec(kern_lin, nrm, fo, fb):
    n2, bk = fo.shape
    nq = kern_lin.shape[0] // bk - 1
    full = lambda a: pl.BlockSpec(a.shape, lambda q: (0,) * a.ndim)
    return pl.pallas_call(
        _hy_kspec_kernel,
        grid=(nq,),
        in_specs=[pl.BlockSpec((bk, 512), lambda q: (q, 0)),
                  pl.BlockSpec((bk, 512), lambda q: (q + 1, 0)),
                  full(fo), full(fb), full(nrm)],
        out_specs=pl.BlockSpec((1, n2, 512), lambda q: (q, 0, 0)),
        out_shape=jax.ShapeDtypeStruct((nq, n2, 512), F32),
        compiler_params=_cp("arbitrary"),
        name="hy_kspec",
    )(kern_lin, kern_lin, fo, fb, nrm)


def _s5_discretise(are_ref, aim_ref, ldt_ref):
    ar = jnp.minimum(are_ref[0], -1e-4)
    ai = aim_ref[0]
    dt = jnp.exp(ldt_ref[0])
    e = jnp.exp(ar * dt)
    return ar, ai, e * jnp.cos(ai * dt), e * jnp.sin(ai * dt)


def _s5_scan_kernel(nseq, emit_y, u_ref, hin_ref, are_ref, aim_ref, ldt_ref, bre_ref, bim_ref, *rest):
    if emit_y:
        cre_ref, cim_ref, y_ref, hfin_ref, wb_sc, ab_sc, s_sc, hc_sc, perm_sc, wc_sc = rest
    else:
        hfin_ref, wb_sc, ab_sc, s_sc, hc_sc, perm_sc = rest
    d = pl.program_id(0)
    c = pl.program_id(1)
    n = S5_N

    @pl.when(c == 0)
    def _():
        ar, ai, abr, abi = _s5_discretise(are_ref, aim_ref, ldt_ref)
        ab_sc[0:1, :] = abr
        ab_sc[1:2, :] = abi
        den = 1.0 / (ar * ar + ai * ai)
        cr = ((abr - 1.0) * ar + abi * ai) * den
        ci = (abi * ar - (abr - 1.0) * ai) * den
        grp_rows = lax.shift_right_logical(lax.broadcasted_iota(jnp.int32, (D_BRANCH, 1), 0), 4)
        grp_cols = lax.shift_right_logical(lax.broadcasted_iota(jnp.int32, (1, n), 1), 6)
        expand_b = lambda ref: jnp.where(grp_rows == grp_cols, jnp.concatenate([ref[0]] * (n // 128), axis=1), 0.0)
        bre, bim = expand_b(bre_ref), expand_b(bim_ref)
        wb_sc[:, 0:n] = (cr * bre - ci * bim).astype(BF16)
        wb_sc[:, n:2 * n] = (cr * bim + ci * bre).astype(BF16)
        if emit_y:
            st_rows = lax.shift_right_logical(lax.broadcasted_iota(jnp.int32, (n, 1), 0), 6)
            ch_cols = lax.shift_right_logical(lax.broadcasted_iota(jnp.int32, (1, D_BRANCH), 1), 4)
            expand_c = lambda ref: jnp.where(st_rows == ch_cols, jnp.concatenate([ref[0]] * 2, axis=1), 0.0)
            wc_sc[0:n, :] = expand_c(cre_ref).astype(BF16)
            wc_sc[n:2 * n, :] = (-expand_c(cim_ref)).astype(BF16)
        hc_sc[...] = hin_ref[0]

    steps = u_ref.shape[1]
    rows_c = nseq * steps

    @pl.when(c == 0)
    def _():
        i = lax.broadcasted_iota(jnp.int32, (rows_c, rows_c), 0)
        j = lax.broadcasted_iota(jnp.int32, (rows_c, rows_c), 1)
        p = lax.shift_right_logical(i, int(math.log2(nseq)))
        step = p + d * (steps - 1 - 2 * p)
        src = jnp.bitwise_and(i, nseq - 1) * steps + step
        perm_sc[...] = jnp.where(j == src, 1.0, 0.0).astype(BF16)

    lhs = _dot(perm_sc[...], u_ref[...].reshape(rows_c, D_BRANCH).astype(BF16)).astype(BF16)
    lb = 256
    y = None
    for j in range(n // lb):
        lr = slice(lb * j, lb * (j + 1))
        li = slice(n + lb * j, n + lb * (j + 1))
        bur = _dot(lhs, wb_sc[:, lr])
        bui = _dot(lhs, wb_sc[:, li])
        abr = ab_sc[0:1, lr]
        abi = ab_sc[1:2, lr]
        hr, hi = hc_sc[:, lr], hc_sc[:, li]
        for p in range(steps):
            rows = slice(p * nseq, (p + 1) * nseq)
            hr, hi = abr * hr - abi * hi + bur[rows], abr * hi + abi * hr + bui[rows]
            if emit_y:
                s_sc[rows, lr] = hr.astype(BF16)
                s_sc[rows, li] = hi.astype(BF16)
        hc_sc[:, lr] = hr
        hc_sc[:, li] = hi
        if emit_y:
            yj = _dot(s_sc[:, lr], wc_sc[lr, :]) + _dot(s_sc[:, li], wc_sc[li, :])
            y = yj if y is None else y + yj

    if emit_y:
        @pl.when(d == 0)
        def _():
            for p in range(steps):
                y_ref[0, :, p, :] = y[p * nseq:(p + 1) * nseq]

        @pl.when(d == 1)
        def _():
            for p in range(steps):
                y_ref[0, :, steps - 1 - p, :] = y[p * nseq:(p + 1) * nseq]

    @pl.when(c == pl.num_programs(1) - 1)
    def _():
        hfin_ref[0] = hc_sc[...]


def _s5_scan(u, hin, wl, emit_y):
    nseq, nstep, _ = u.shape
    steps = S5_ROWS // nseq
    nc = nstep // steps
    n = S5_N
    chunk = lambda d, c: c + d * (nc - 1 - 2 * c)
    per_dir = lambda shape: pl.BlockSpec((1,) + shape, lambda d, c: (d,) + (0,) * len(shape))
    args = [u, hin, wl["s5_are"], wl["s5_aim"], wl["s5_ldt"], wl["s5_bre"], wl["s5_bim"]]
    specs = [pl.BlockSpec((nseq, steps, D_BRANCH), lambda d, c: (0, chunk(d, c), 0)),
             per_dir((nseq, 2 * n)), per_dir((1, n)), per_dir((1, n)), per_dir((1, n)),
             per_dir((D_BRANCH, 128)), per_dir((D_BRANCH, 128))]
    out_specs = [per_dir((nseq, 2 * n))]
    out_shape = [jax.ShapeDtypeStruct((2, nseq, 2 * n), F32)]
    scratch = [pltpu.VMEM((D_BRANCH, 2 * n), BF16), pltpu.VMEM((8, n), F32),
               pltpu.VMEM((S5_ROWS, 2 * n), BF16), pltpu.VMEM((nseq, 2 * n), F32),
               pltpu.VMEM((S5_ROWS, S5_ROWS), BF16)]
    if emit_y:
        args += [wl["s5_cre"], wl["s5_cim"]]
        specs += [per_dir((n, 128)), per_dir((n, 128))]
        out_specs = [pl.BlockSpec((1, nseq, steps, D_BRANCH), lambda d, c: (d, 0, chunk(d, c), 0))] + out_specs
        out_shape = [jax.ShapeDtypeStruct((2, nseq, nstep, D_BRANCH), F32)] + out_shape
        scratch += [pltpu.VMEM((2 * n, D_BRANCH), BF16)]
    return pl.pallas_call(
        functools.partial(_s5_scan_kernel, nseq, emit_y),
        grid=(2, nc),
        in_specs=specs, out_specs=out_specs, out_shape=out_shape, scratch_shapes=scratch,
        compiler_params=_cp("arbitrary", "arbitrary"),
        name="s5_scan" if emit_y else "s5_scan_finals",
    )(*args)


def _s5_chain_kernel(batch, nseg, f_ref, h0_ref, are_ref, aim_ref, ldt_ref, o_ref):
    d = pl.program_id(0)
    n = S5_N
    _, _, pr, pi = _s5_discretise(are_ref, aim_ref, ldt_ref)
    for _ in range(int(math.log2(S5_SEG))):
        pr, pi = pr * pr - pi * pi, 2.0 * pr * pi
    f = f_ref[0]
    fr, fi = f[:, 0:n], f[:, n:2 * n]
    h0 = h0_ref[0]
    h0r, h0i = h0[:, 0:n], h0[:, n:2 * n]
    nrow = batch * nseg
    seg = jnp.bitwise_and(lax.broadcasted_iota(jnp.int32, (nrow, 1), 0), nseg - 1)

    def run(shift, keep):
        xr, xi = h0r, h0i
        for _ in range(nseg - 1):
            zr = fr + pr * xr - pi * xi
            zi = fi + pr * xi + pi * xr
            xr = h0r + jnp.where(keep, pltpu.roll(zr, shift, 0), 0.0)
            xi = h0i + jnp.where(keep, pltpu.roll(zi, shift, 0), 0.0)
        o_ref[0, :, 0:n] = xr
        o_ref[0, :, n:2 * n] = xi

    @pl.when(d == 0)
    def _():
        run(1, seg != 0)

    @pl.when(d == 1)
    def _():
        run(nrow - 1, seg != nseg - 1)


def _s5_chain(fin, h0rows, wl, batch, nseg):
    nrow = batch * nseg
    n = S5_N
    per_dir = lambda shape: pl.BlockSpec((1,) + shape, lambda d: (d,) + (0,) * len(shape))
    return pl.pallas_call(
        functools.partial(_s5_chain_kernel, batch, nseg),
        grid=(2,),
        in_specs=[per_dir((nrow, 2 * n)), per_dir((nrow, 2 * n)), per_dir((1, n)), per_dir((1, n)), per_dir((1, n))],
        out_specs=per_dir((nrow, 2 * n)),
        out_shape=jax.ShapeDtypeStruct((2, nrow, 2 * n), F32),
        compiler_params=_cp("arbitrary"),
        name="s5_chain",
    )(fin, h0rows, wl["s5_are"], wl["s5_aim"], wl["s5_ldt"])


def _gla_kernel(seq_len, nb, n_aliased, q_ref, k_ref, v_ref, g_ref, gw_ref, gb_ref, s0_ref, *rest):
    o_ref, sfin_ref, qe_sc, upd_sc, dec_sc, sall_sc, lhs_sc, kt_sc, la_sc, oi_sc = rest[n_aliased:]
    d = pl.program_id(1)
    sign = 1 - 2 * d
    ck, sup = GLA_CHUNK, GLA_SUPER
    cps = sup // ck
    nsup, nchunk = seq_len // sup, seq_len // ck
    dk, dv = GLA_HEADS * GLA_DK, GLA_HEADS * GLA_DV
    r = lax.broadcasted_iota(jnp.int32, (sup, sup), 0)
    s = lax.broadcasted_iota(jnp.int32, (sup, sup), 1)
    same = lax.shift_right_logical(r, 6) == lax.shift_right_logical(s, 6)
    tri = jnp.logical_and(same, (s - r) * sign <= 0)
    cum_lhs = jnp.where(tri, 1.0, 0.0).astype(BF16)
    t4 = lax.broadcasted_iota(jnp.int32, (ck, GLA_HEADS * ck), 0)
    s4 = jnp.bitwise_and(lax.broadcasted_iota(jnp.int32, (ck, GLA_HEADS * ck), 1), ck - 1)
    tri4 = (s4 - t4) * sign <= 0
    pos = jnp.bitwise_and(lax.broadcasted_iota(jnp.int32, (ck, 1), 0), ck - 1)
    is_last = pos == (ck - 1) * (1 - d)
    row_chunk = lax.shift_right_logical(lax.broadcasted_iota(jnp.int32, (sup, 1), 0), 6)
    head_k = lax.shift_right_logical(lax.broadcasted_iota(jnp.int32, (1, dk), 1), 5)
    head_v = lax.shift_right_logical(lax.broadcasted_iota(jnp.int32, (1, dv), 1), 6)
    blockdiag = lax.shift_right_logical(lax.broadcasted_iota(jnp.int32, (dv, 1), 0), 6) == head_k

    def group_rows(u):
        return pl.ds(u * sup, sup) if isinstance(u, int) else pl.ds(pl.multiple_of(u * sup, sup), sup)

    def stage_a(u, slot):
        rows = group_rows(u)
        q = q_ref[rows, :] * (GLA_DK ** -0.5)
        k = k_ref[rows, :]
        v = v_ref[rows, :]
        cs = _dot(cum_lhs, la_sc[rows, :])
        yield
        bc = cs[:, 0:dk] + cs[:, dk:2 * dk]
        tots = [jnp.sum(jnp.where(is_last, bc[c * ck:(c + 1) * ck], 0.0), axis=0, keepdims=True)
                for c in range(cps)]
        tot = jnp.concatenate([jnp.broadcast_to(t, (ck, dk)) for t in tots], axis=0)
        ref = 0.5 * tot
        kt_sc[slot] = (k * jnp.exp(ref - bc)).astype(BF16)
        lhs_sc[slot] = (q * jnp.exp(bc - ref)).astype(BF16)
        qe_sc[rows, :] = (q * jnp.exp(bc)).astype(BF16)
        kl = (k * jnp.exp(tot - bc)).astype(BF16)
        zero = jnp.zeros_like(kl)
        klx = jnp.concatenate([jnp.where(row_chunk == c, kl, zero) for c in range(cps)], axis=1)
        upd = _dot_tn(v.astype(BF16), klx)
        yield
        for c in range(cps):
            upd_sc[u * cps + c] = jnp.where(blockdiag, upd[:, c * dk:(c + 1) * dk], 0.0)
            dec_sc[u * cps + c] = jnp.broadcast_to(jnp.exp(tots[c]), (8, dk))

    def stage_b(u, slot):
        rows = group_rows(u)
        v = v_ref[rows, :].astype(BF16)
        qt, kt = lhs_sc[slot], kt_sc[slot]
        zk, zv = jnp.zeros_like(kt[0:ck]), jnp.zeros_like(v[0:ck])
        p = [_dot_nt(qt[c * ck:(c + 1) * ck],
                     jnp.concatenate([jnp.where(head_k == h, kt[c * ck:(c + 1) * ck], zk)
                                      for h in range(GLA_HEADS)], axis=0)) for c in range(cps)]
        yield
        o = [_dot(jnp.where(tri4, p[c], 0.0).astype(BF16),
                  jnp.concatenate([jnp.where(head_v == h, v[c * ck:(c + 1) * ck], zv)
                                   for h in range(GLA_HEADS)], axis=0)) for c in range(cps)]
        yield
        oi_sc[rows, :] = jnp.concatenate(o, axis=0)

    def run(*stages):
        live = list(stages)
        while live:
            for g in list(live):
                if next(g, StopIteration) is StopIteration:
                    live.remove(g)

    x = _dot(g_ref[...].astype(BF16), gw_ref[0]) + gb_ref[0]
    la = (jnp.minimum(x, 0.0) - jnp.log(1.0 + jnp.exp(-jnp.abs(x)))) * (1.0 / GLA_TAU)
    la_sc[...] = jnp.concatenate(_split2(la), axis=1)

    nu = nb * nsup
    if nu <= 4:
        run(stage_a(0, 0))
        for u in range(nu):
            if u + 1 < nu:
                run(stage_b(u, u % 2), stage_a(u + 1, (u + 1) % 2))
            else:
                run(stage_b(u, u % 2))
    else:
        run(stage_a(0, 0))

        def sup_body(u, carry):
            nxt = jnp.minimum(u + 1, nu - 1)
            run(stage_b(u, jnp.bitwise_and(u, 1)), stage_a(nxt, jnp.bitwise_and(u + 1, 1)))
            return carry

        lax.fori_loop(0, nu, sup_body, 0)

    for j in range(nb):
        def state_body(c, st, j=j):
            ci = j * nchunk + c + d * (nchunk - 1 - 2 * c)
            sall_sc[ci] = st.astype(BF16)
            return dec_sc[ci][0:1, :] * st + upd_sc[ci]

        st_fin = jnp.transpose(lax.fori_loop(0, nchunk, state_body, s0_ref[j, 0]))
        for h in range(GLA_HEADS):
            sfin_ref[j, 0, 0, h] = st_fin[h * GLA_DK:(h + 1) * GLA_DK, h * GLA_DV:(h + 1) * GLA_DV]

    def inter(u):
        rows = group_rows(u)
        qe = qe_sc[rows, :]
        oi = jnp.concatenate([_dot_nt(qe[c * ck:(c + 1) * ck], sall_sc[u * cps + c]) for c in range(cps)], axis=0)
        o_ref[0, rows, :] = (oi_sc[rows, :] + oi).astype(BF16)

    if nu <= 4:
        for u in range(nu):
            inter(u)
    else:
        lax.fori_loop(0, nu, lambda u, carry: (inter(u), carry)[1], 0, unroll=2 if nu % 2 == 0 else 1)


def _gla(gla_in, mla_in, wl, s0t, batch, seq_len, fin=(0, None, 1)):
    layer, prev_fin, fin_layers = fin
    aliases = {} if prev_fin is None else {7: 1}
    extra = [] if prev_fin is None else [prev_fin]
    n = gla_in.shape[0]
    dk, dv = GLA_HEADS * GLA_DK, GLA_HEADS * GLA_DV
    nb = max(1, min(batch, 1024 // seq_len))
    rows = nb * seq_len
    nchunk = nb * (seq_len // GLA_CHUNK)
    return pl.pallas_call(
        functools.partial(_gla_kernel, seq_len, nb, len(extra)),
        grid=(batch // nb, 2),
        input_output_aliases=aliases,
        in_specs=[pl.BlockSpec((rows, dk), lambda b, d: (b, 0)),
                  pl.BlockSpec((rows, dk), lambda b, d: (b, 1)),
                  pl.BlockSpec((rows, dv), lambda b, d: (b, 1)),
                  pl.BlockSpec((rows, dk), lambda b, d: (b, 1)),
                  pl.BlockSpec((1, dk, dk), lambda b, d: (d, 0, 0)),
                  pl.BlockSpec((1, 1, dk), lambda b, d: (d, 0, 0)),
                  pl.BlockSpec((nb, 1, dv, dk), lambda b, d: (b, d, 0, 0))]
                 + [pl.BlockSpec(memory_space=pl.ANY)] * len(extra),
        out_specs=[pl.BlockSpec((1, rows, dv), lambda b, d: (d, b, 0)),
                   pl.BlockSpec((nb, 1, 1, GLA_HEADS, GLA_DK, GLA_DV), lambda b, d: (b, layer, d, 0, 0, 0))],
        out_shape=[jax.ShapeDtypeStruct((2, n, dv), BF16),
                   jax.ShapeDtypeStruct((batch, fin_layers, 2, GLA_HEADS, GLA_DK, GLA_DV), F32)],
        scratch_shapes=[pltpu.VMEM((rows, dk), BF16),
                        pltpu.VMEM((nchunk, dv, dk), F32),
                        pltpu.VMEM((nchunk, 8, dk), F32),
                        pltpu.VMEM((nchunk, dv, dk), BF16),
                        pltpu.VMEM((2, GLA_SUPER, dk), BF16),
                        pltpu.VMEM((2, GLA_SUPER, dk), BF16),
                        pltpu.VMEM((rows, 2 * dk), BF16),
                        pltpu.VMEM((rows, dv), F32)],
        compiler_params=_cp("arbitrary", "arbitrary"),
        name="gla",
    )(gla_in, gla_in, gla_in, mla_in, wl["gla_gw"], wl["gla_gb"], s0t, *extra)


def _outproj_kernel(x_ref, mod_ref, g_ref, om_ref, oh_ref, su_ref, sf_ref, sb_ref, sd_ref, sw_ref, sbias_ref,
                    gf_ref, gb_ref, gn_ref, hm_ref, w_ref, y_ref):
    c = D_BRANCH
    g = g_ref[...].astype(F32)
    acc = _dot((om_ref[...].astype(F32) * _silu(g[:, 0:c])).astype(BF16), w_ref[0:c, :])
    acc += _dot((oh_ref[...].astype(F32) * _silu(g[:, c:2 * c])).astype(BF16), w_ref[c:2 * c, :])
    ys = sd_ref[...] * su_ref[...] + sf_ref[0] + sb_ref[0]
    ge = 0.5 * ys * (1.0 + jnp.tanh(math.sqrt(2.0 / math.pi) * (ys + 0.044715 * (ys * ys * ys))))
    o_s5 = ge / (1.0 + jnp.exp(-(_dot(ge.astype(BF16), sw_ref[...]) + sbias_ref[...])))
    acc += _dot((o_s5 * _silu(g[:, 2 * c:3 * c])).astype(BF16), w_ref[2 * c:3 * c, :])
    og = gf_ref[0].astype(F32) + gb_ref[0].astype(F32)
    hi, lo = _split2(og * og)
    ms = _dot(hi, hm_ref[...]) + _dot(lo, hm_ref[...])
    ogn = og * lax.rsqrt(ms + EPS) * gn_ref[...]
    acc += _dot((ogn * _silu(g[:, 3 * c:4 * c])).astype(BF16), w_ref[3 * c:4 * c, :])
    y_ref[...] = x_ref[...] + mod_ref[0, 2:3, :] * acc


def _outproj(x, mod, mod_row, gates, o_mla, o_hy, s5_u, s5_y, o_gla, wl, tm):
    n, d = x.shape
    c = D_BRANCH
    row = lambda w: pl.BlockSpec((tm, w), lambda i: (i, 0))
    per_dir = lambda k: pl.BlockSpec((1, tm, c), lambda i: (k, i, 0))
    full = lambda *shape: pl.BlockSpec(shape, lambda i: (0,) * len(shape))
    return pl.pallas_call(
        _outproj_kernel,
        grid=(n // tm,),
        in_specs=[row(d),
                  pl.BlockSpec((1, 3, d), lambda i: (mod_row(i * tm), 0, 0)),
                  row(d), row(c), row(c),
                  row(c), per_dir(0), per_dir(1), full(1, c), full(c, c), full(1, c),
                  per_dir(0), per_dir(1), full(1, c), full(c, c), full(d, d)],
        out_specs=row(d),
        out_shape=jax.ShapeDtypeStruct((n, d), F32),
        compiler_params=_cp("arbitrary"),
        name="outproj",
    )(x, mod, gates, o_mla, o_hy, s5_u, s5_y, s5_y, wl["s5_d"], wl["s5_glu_w"], wl["s5_glu_b"],
      o_gla, o_gla, wl["gla_norm"], wl["head_mean"], wl["w_out"])


def _rope_tables(seq_len):
    pos = np.arange(seq_len)
    inv = ROPE_BASE ** (-np.arange(0, 16, 2, dtype=np.float64) / 16.0)
    cos = np.ones((seq_len, HEAD_PAD))
    sin_a = np.zeros((seq_len, HEAD_PAD))
    sin_b = np.zeros((seq_len, HEAD_PAD))
    for base, p in ((MLA_NOPE, pos // GRID_W), (MLA_NOPE + 16, pos % GRID_W)):
        ang = p[:, None].astype(np.float64) * inv[None, :]
        cos[:, base:base + 8] = np.cos(ang)
        cos[:, base + 8:base + 16] = np.cos(ang)
        sin_a[:, base:base + 8] = -np.sin(ang)
        sin_b[:, base + 8:base + 16] = np.sin(ang)
    return tuple(jnp.asarray(t, F32) for t in (cos, sin_a, sin_b))


def _odd_dft(seq_len):
    bk = min(seq_len, HY_BLOCK)
    k = np.arange(bk)[:, None]
    t = np.arange(bk)[None, :]

    def mat(shift):
        ang = (np.pi / (2 * bk)) * (((2 * k + 1) * (t + shift)) % (4 * bk))
        return np.concatenate([np.cos(ang), -np.sin(ang)], axis=0)

    fo = mat(0)
    fb = -mat(bk)
    fb[:, 0] = 0.0
    const = lambda a: jnp.asarray(a, F32).astype(BF16)
    return const(fo), const(fb), const(fo.T)


def _hyena_tables(seq_len):
    lag = np.arange(-seq_len, seq_len)
    pos = np.where(lag == -seq_len, 0, np.abs(lag)).astype(np.float64)
    t = pos / seq_len
    w = 2.0 * np.pi * pos / seq_len
    bands = np.linspace(1e-4, HY_BANDS - 1, HY_BANDS)
    feat = np.zeros((2 * seq_len, HY_HIDDEN))
    feat[:, 0] = t
    feat[:, 1:1 + HY_BANDS] = np.cos(w[:, None] * bands)
    feat[:, 1 + HY_BANDS:HY_FEAT] = np.sin(w[:, None] * bands)
    feat = feat.reshape(-1, 2, 128, HY_HIDDEN).transpose(0, 2, 1, 3).reshape(seq_len, 2 * HY_HIDDEN)
    deltas = np.linspace(math.log(1.0 / HY_TARGET) / HY_FAST_DECAY, math.log(1.0 / HY_TARGET) / HY_SLOW_DECAY,
                         D_BRANCH)
    win = np.exp(-t[:, None] * deltas[None, :]) + HY_SHIFT
    return jnp.asarray(feat, F32), jnp.asarray(win, F32)


def _pad_to(a, shape):
    return jnp.pad(a, [(0, s - d) for s, d in zip(shape, a.shape)])


def _layer_weights(l, p):
    z = lambda *s: jnp.zeros(s, F32)
    w_in = p["w_in"][l]
    col = lambda lo, hi: w_in[:, lo:hi]
    d = D_MODEL
    w_p = jnp.concatenate([
        col(0, 192), col(320, 352), col(2656, 2688), col(192, 320),
        col(352, 608), col(1376, 1632), col(1888, 2144), col(2688, 2944),
        col(608, 1376), col(1632, 1888),
        col(2144, 2272), col(2272, 2400), col(2400, 2656)], axis=1).astype(BF16)
    wl = {"w_in": w_p, "norm_w": p["norm_w"][l].reshape(1, d), "w_out": p["w_out"][l].astype(BF16)}
    wl["qa_norm"] = _pad_to(p["mla_qa_norm"][l].reshape(1, -1), (1, 256))
    w_uq = _pad_to(p["mla_w_uq"][l].reshape(MLA_Q_RANK, MLA_HEADS, MLA_QK), (256, MLA_HEADS, HEAD_PAD))
    wl["w_uq"] = w_uq.reshape(256, MLA_HEADS * HEAD_PAD).astype(BF16)
    wl["q_norm"] = _pad_to(p["mla_q_norm"][l].reshape(1, -1), (1, HEAD_PAD))
    wl["k_norm"] = _pad_to(p["mla_k_norm"][l].reshape(1, -1), (1, HEAD_PAD))
    wl["kva_norm"] = p["mla_kva_norm"][l].reshape(1, -1)
    w_ukv = p["mla_w_ukv"][l].reshape(MLA_KV_RANK, MLA_HEADS, MLA_NOPE + MLA_V)
    wl["w_uk"] = _pad_to(w_ukv[:, :, :MLA_NOPE], (MLA_KV_RANK, MLA_HEADS, HEAD_PAD)).reshape(MLA_KV_RANK, -1).astype(BF16)
    wl["w_uv"] = w_ukv[:, :, MLA_NOPE:].reshape(MLA_KV_RANK, MLA_HEADS * MLA_V).astype(BF16)
    wl["hy_conv_w"] = p["hy_conv_w"][l]
    wl["hy_conv_b"] = p["hy_conv_b"][l].reshape(1, -1)
    hh = HY_HIDDEN
    twice = lambda a: jnp.tile(a.reshape(1, -1), (1, 2))
    bdiag = lambda a: jnp.concatenate([jnp.pad(a, ((0, 0), (0, a.shape[1]))), jnp.pad(a, ((0, 0), (a.shape[1], 0)))], axis=0)
    wl["hy_w1"] = bdiag(_pad_to(p["hy_w1"][l], (hh, hh)))
    wl["hy_b1"], wl["hy_f1"] = twice(p["hy_b1"][l]), twice(p["hy_freq1"][l])
    wl["hy_w2"] = bdiag(p["hy_w2"][l])
    wl["hy_b2"], wl["hy_f2"] = twice(p["hy_b2"][l]), twice(p["hy_freq2"][l])
    w3 = p["hy_w3"][l]
    wl["hy_w3"] = jnp.concatenate([bdiag(w3[:, 0:512]), bdiag(w3[:, 512:1024])], axis=1)
    wl["hy_bias"] = p["hy_bias"][l]
    flat = lambda a: a[l].reshape(2, 1, S5_N)
    wl["s5_are"], wl["s5_aim"] = flat(p["s5_a_re"]), flat(p["s5_a_im"])
    wl["s5_ldt"] = jnp.repeat(p["s5_log_dt"][l], S5_STATE, axis=-1).reshape(2, 1, S5_N)
    lanes_b = lambda a: jnp.tile(a[l].transpose(0, 1, 3, 2).reshape(2, D_BRANCH, S5_STATE), (1, 1, 128 // S5_STATE))
    lanes_c = lambda a: jnp.tile(a[l].transpose(0, 1, 3, 2).reshape(2, S5_N, S5_GROUP), (1, 1, 128 // S5_GROUP))
    wl["s5_bre"], wl["s5_bim"] = lanes_b(p["s5_b_re"]), lanes_b(p["s5_b_im"])
    wl["s5_cre"], wl["s5_cim"] = lanes_c(p["s5_c_re"]), lanes_c(p["s5_c_im"])
    wl["s5_d"] = p["s5_d"][l].reshape(1, -1)
    wl["s5_glu_w"] = p["s5_glu_w"][l].astype(BF16)
    wl["s5_glu_b"] = p["s5_glu_b"][l].reshape(1, -1)
    gw = p["gla_gw"][l]
    dk = GLA_HEADS * GLA_DK
    wl["gla_gw"] = jnp.stack([_pad_to(jnp.pad(gw[i], ((GLA_G_LANE + GLA_RANK * i, 0), (0, 0))), (dk, dk))
                              for i in range(2)]).astype(BF16)
    wl["gla_gb"] = p["gla_gb"][l].reshape(2, 1, dk)
    wl["gla_norm"] = jnp.tile(p["gla_norm"][l], GLA_HEADS).reshape(1, -1)
    head = np.arange(D_BRANCH) // GLA_DV
    wl["head_mean"] = jnp.asarray((head[:, None] == head[None, :]) / GLA_DV, BF16)
    return wl


def _hyena_filters(wl, tabs):
    feat, win, fo, fb, _ = tabs
    kern_lin, nrm = _hy_mlp(feat, wl, win)
    return _hy_kspec(kern_lin, nrm, fo, fb)


def _trunk_layer(x, mod, mod_row, wl, batch, seq_len, hy_tabs, rope_tabs=None, ctx=None, layer=0, cache_bufs=None):
    n = batch * seq_len
    tm = 512
    mla_in, gates, hy_in, s5_in, gla_in = _inproj(x, mod, mod_row, wl["norm_w"], wl["w_in"], tm)

    if ctx is None:
        prev = None if cache_bufs is None else cache_bufs[:2]
        q, k, v, ckv, krope = _mla_prep(mla_in, wl, rope_tabs, seq_len, tm, True, (layer, prev))
    else:
        q, k, v = _mla_prep(mla_in, wl, rope_tabs, seq_len, tm, True)
        ckv = krope = None
    kv_parts = [(k, v, seq_len)]
    if ctx is not None:
        k_ctx, v_ctx = _mla_prep(ctx["mla"], wl, None, ctx["past"], 512, False)
        kv_parts = [(k_ctx, v_ctx, ctx["past"])] + kv_parts
    o_mla = _attention(q, kv_parts, batch, seq_len, 256)

    o_hy = _hyena(hy_in, wl, _hyena_filters(wl, hy_tabs), hy_tabs[2], hy_tabs[4], batch, seq_len)

    nseg = seq_len // S5_SEG
    nseq = batch * nseg
    u_seg = s5_in.reshape(nseq, S5_SEG, D_BRANCH)
    if ctx is None:
        hin = jnp.zeros((2, nseq, 2 * S5_N), F32)
    else:
        (fin,) = _s5_scan(u_seg, jnp.zeros((2, nseq, 2 * S5_N), F32), wl, False)
        hin = _s5_chain(fin, ctx["s5_h0"], wl, batch, nseg)
    y2, s5_fin = _s5_scan(u_seg, hin, wl, True)

    s0 = jnp.zeros((batch, 2, GLA_HEADS * GLA_DV, GLA_HEADS * GLA_DK), F32) if ctx is None else ctx["gla_s0"]
    fin = (0, None, 1) if ctx is not None else (layer, None if cache_bufs is None else cache_bufs[2], DEPTH)
    o_gla, gla_fin = _gla(gla_in, mla_in, wl, s0, batch, seq_len, fin)

    y = _outproj(x, mod, mod_row, gates, o_mla, o_hy, s5_in, y2.reshape(2, n, D_BRANCH), o_gla, wl, 2 * tm)
    return y, (ckv, krope, s5_fin, gla_fin)


def kernel(x_prompt, x_sample, c, cache_mla_ckv, cache_mla_krope, state_s5, state_gla, c_ctx, norm_w, ada_w, ada_b, w_in, w_out, mla_qa_norm, mla_kva_norm, mla_w_uq, mla_w_ukv, mla_q_norm, mla_k_norm, hy_conv_w, hy_conv_b, hy_w1, hy_b1, hy_freq1, hy_w2, hy_b2, hy_freq2, hy_w3, hy_bias, s5_a_re, s5_a_im, s5_log_dt, s5_b_re, s5_b_im, s5_c_re, s5_c_im, s5_d, s5_glu_w, s5_glu_b, gla_gw, gla_gb, gla_norm):
    params = dict(norm_w=norm_w, w_in=w_in, w_out=w_out, mla_qa_norm=mla_qa_norm, mla_kva_norm=mla_kva_norm,
                  mla_w_uq=mla_w_uq, mla_w_ukv=mla_w_ukv, mla_q_norm=mla_q_norm, mla_k_norm=mla_k_norm,
                  hy_conv_w=hy_conv_w, hy_conv_b=hy_conv_b, hy_w1=hy_w1, hy_b1=hy_b1, hy_freq1=hy_freq1,
                  hy_w2=hy_w2, hy_b2=hy_b2, hy_freq2=hy_freq2, hy_w3=hy_w3, hy_bias=hy_bias,
                  s5_a_re=s5_a_re, s5_a_im=s5_a_im, s5_log_dt=s5_log_dt, s5_b_re=s5_b_re, s5_b_im=s5_b_im,
                  s5_c_re=s5_c_re, s5_c_im=s5_c_im, s5_d=s5_d, s5_glu_w=s5_glu_w, s5_glu_b=s5_glu_b,
                  gla_gw=gla_gw, gla_gb=gla_gb, gla_norm=gla_norm)
    bp, lp, d = x_prompt.shape
    bs, ls, _ = x_sample.shape
    past = cache_mla_ckv.shape[2]
    n_s5 = S5_N

    conds = jnp.concatenate([c_ctx[None, :], c, jnp.zeros((8 - 1 - bs, d), F32)], axis=0)
    mods = _modulation(conds, ada_w, ada_b).reshape(DEPTH, 8, 3, d)

    tabs_p = _hyena_tables(lp) + _odd_dft(lp)
    tabs_s = _hyena_tables(ls) + _odd_dft(ls)
    rope_tabs = _rope_tables(ls)
    nseg = ls // S5_SEG

    y_p = x_prompt.reshape(bp * lp, d)
    y_s = x_sample.reshape(bs * ls, d)
    s5_l = []
    cache_bufs = (jnp.zeros((bp, DEPTH, lp, MLA_KV_RANK), F32), jnp.zeros((bp, DEPTH, lp, MLA_ROPE), F32),
                  jnp.zeros((bp, DEPTH, 2, GLA_HEADS, GLA_DK, GLA_DV), F32))
    for l in range(DEPTH):
        wl = _layer_weights(l, params)
        y_p, (ckv, krope, s5_fin, gla_fin) = _trunk_layer(y_p, mods[l], lambda row: 0, wl, bp, lp, tabs_p,
                                                          layer=l, cache_bufs=cache_bufs)
        cache_bufs = (ckv, krope, gla_fin)
        s5_l.append(jnp.stack([s5_fin[:, :, :n_s5], s5_fin[:, :, n_s5:]], axis=-1)
                    .reshape(2, bp, S5_GROUPS, S5_STATE, 2).transpose(1, 0, 2, 3, 4))

        mla_ctx = jnp.concatenate([cache_mla_ckv[:, l], jnp.zeros((bs, past, 64), F32), cache_mla_krope[:, l],
                                   jnp.zeros((bs, past, 32), F32)], axis=-1).reshape(bs * past, 256)
        st = state_s5[:, l]
        h0 = jnp.concatenate([st[..., 0].reshape(bs, 2, n_s5), st[..., 1].reshape(bs, 2, n_s5)], axis=-1)
        h0 = h0.transpose(1, 0, 2)
        h0rows = jnp.zeros((2, nseg * bs, 2 * n_s5), F32)
        h0rows = h0rows.at[0, 0::nseg].set(h0[0]).at[1, nseg - 1::nseg].set(h0[1])
        eye_h = jnp.eye(GLA_HEADS, dtype=F32)
        gla_s0 = jnp.einsum("bdhke,hg->bdhegk", state_gla[:, l], eye_h).reshape(
            bs, 2, GLA_HEADS * GLA_DV, GLA_HEADS * GLA_DK)
        ctx = {"mla": mla_ctx, "past": past, "s5_h0": h0rows, "gla_s0": gla_s0}
        y_s, _ = _trunk_layer(y_s, mods[l], lambda row: 1 + row // ls, wl, bs, ls, tabs_s, rope_tabs, ctx)

    return (y_p.reshape(bp, lp, d), y_s.reshape(bs, ls, d),
            cache_bufs[0], cache_bufs[1], jnp.stack(s5_l, axis=1), cache_bufs[2])
```

```python
import functools
import math

import numpy as np
import jax
import jax.numpy as jnp
from jax import lax
from jax.experimental import pallas as pl
from jax.experimental.pallas import tpu as pltpu

F32 = jnp.float32
BF16 = jnp.bfloat16

D_MODEL = 1024
DEPTH = 2
GRID_W = 64
D_BRANCH = 256
EPS = 1e-6

MLA_HEADS = 4
MLA_Q_RANK = 192
MLA_KV_RANK = 128
MLA_NOPE = 64
MLA_ROPE = 32
MLA_QK = 96
MLA_V = 64
ROPE_BASE = 10000.0
HEAD_PAD = 128

HY_BANDS = 16
HY_FEAT = 33
HY_HIDDEN = 64
HY_SHIFT = 0.05
HY_FAST_DECAY = 0.3
HY_SLOW_DECAY = 1.5
HY_TARGET = 1e-2
HY_BLOCK = 512

S5_GROUP = 16
S5_GROUPS = 16
S5_STATE = 64
S5_N = S5_GROUPS * S5_STATE
S5_ROWS = 512
S5_SEG = 256

GLA_HEADS = 4
GLA_DK = 32
GLA_DV = 64
GLA_RANK = 16
GLA_TAU = 16.0
GLA_CHUNK = 64
GLA_SUPER = 256

SEG_MLA = (0, 384)
SEG_GATE = (384, 1408)
SEG_HY = (1408, 2176)
SEG_S5 = (2176, 2432)
SEG_GLA = (2432, 2944)
N_PROJ = 2944
GLA_G_LANE = 96

VMEM_LIMIT = 48 * 1024 * 1024


def _cp(*sem):
    return pltpu.CompilerParams(dimension_semantics=sem, vmem_limit_bytes=VMEM_LIMIT)


def _dot(a, b):
    return jnp.dot(a, b, preferred_element_type=F32)


def _dot_nt(a, b):
    return lax.dot_general(a, b, (((1,), (1,)), ((), ())), preferred_element_type=F32)


def _dot_tn(a, b):
    return lax.dot_general(a, b, (((0,), (0,)), ((), ())), preferred_element_type=F32)


def _split2(x):
    hi = x.astype(BF16)
    lo = (x - hi.astype(F32)).astype(BF16)
    return hi, lo


def _split3(x):
    h1 = x.astype(BF16)
    r1 = x - h1.astype(F32)
    h2 = r1.astype(BF16)
    h3 = (r1 - h2.astype(F32)).astype(BF16)
    return h1, h2, h3


def _dot3(a, b):
    a1, a2 = _split2(a)
    b1, b2 = _split2(b)
    return _dot(a1, b1) + (_dot(a1, b2) + _dot(a2, b1))


def _silu(z):
    return z / (1.0 + jnp.exp(-z))


def _mod_kernel(c_ref, w_ref, b_ref, o_ref):
    s = _silu(c_ref[...])
    o_ref[0] = _dot(s.astype(BF16), w_ref[0].astype(BF16)) + b_ref[0]


def _modulation(conds, ada_w, ada_b):
    d = D_MODEL
    return pl.pallas_call(
        _mod_kernel,
        grid=(DEPTH, 3),
        in_specs=[pl.BlockSpec((8, d), lambda l, j: (0, 0)),
                  pl.BlockSpec((1, d, d), lambda l, j: (l, 0, j)),
                  pl.BlockSpec((1, 1, d), lambda l, j: (l, 0, j))],
        out_specs=pl.BlockSpec((1, 8, d), lambda l, j: (l, 0, j)),
        out_shape=jax.ShapeDtypeStruct((DEPTH, 8, 3 * d), F32),
        compiler_params=_cp("arbitrary", "arbitrary"),
        name="modulation",
    )(conds, ada_w, ada_b.reshape(DEPTH, 1, 3 * d))


def _head_norm(xh, w):
    ms = jnp.sum(xh * xh, axis=-1, keepdims=True) * (1.0 / MLA_QK)
    return xh * lax.rsqrt(ms + EPS) * w


def _rope(xh, cos, sin_a, sin_b):
    return xh * cos + pltpu.roll(xh, HEAD_PAD - 8, 1) * sin_a + pltpu.roll(xh, 8, 1) * sin_b


def _mla_steps(m, has_q, rope, cache_seqs, n_aliased, refs):
    refs = list(refs)
    if has_q:
        qan_ref, wuq_ref, qn_ref, kvn_ref = refs[:4]
        refs = refs[4:]
    wuk_ref, wuv_ref, kn_ref = refs[:3]
    refs = refs[3:]
    if rope:
        cos_ref, sa_ref, sb_ref = refs[:3]
        refs = refs[3:]
        cos, sa, sb = cos_ref[...], sa_ref[...], sb_ref[...]
    refs = refs[n_aliased:]
    if has_q:
        q_ref = refs.pop(0)
    k_ref, v_ref = refs[:2]
    if cache_seqs:
        ckv_ref, kro_ref = refs[2:]
    if has_q:
        lane = lax.broadcasted_iota(jnp.int32, (1, HEAD_PAD), 1)
        mixed = m[:, 128:256]
        cq = jnp.concatenate([m[:, 0:128], jnp.where(lane < MLA_NOPE, mixed, 0.0)], axis=1)
        ms = jnp.sum(cq * cq, axis=-1, keepdims=True) * (1.0 / MLA_Q_RANK)
        cqn = cq * lax.rsqrt(ms + EPS) * qan_ref[...]
        ckv = m[:, 256:384]
        ckvn = ckv * lax.rsqrt(jnp.mean(ckv * ckv, axis=-1, keepdims=True) + EPS) * kvn_ref[...]
        yield
        q = _dot(cqn.astype(BF16), wuq_ref[...])
        kr = jnp.where(jnp.logical_and(lane >= MLA_NOPE, lane < MLA_NOPE + MLA_ROPE), mixed, 0.0)
        if cache_seqs:
            seq_len = ckv_ref.shape[2]
            for s in range(cache_seqs):
                ckv_ref[s, 0] = ckvn[s * seq_len:(s + 1) * seq_len]
                kro_ref[s, 0] = kr[s * seq_len:(s + 1) * seq_len, MLA_NOPE:MLA_NOPE + MLA_ROPE]
    else:
        ckvn = m[:, 0:128]
        kr = m[:, 128:256]
    cb = ckvn.astype(BF16)
    kup = _dot(cb, wuk_ref[...])
    v_ref[...] = _dot(cb, wuv_ref[...]).astype(BF16)
    yield
    for h in range(MLA_HEADS):
        sl = slice(HEAD_PAD * h, HEAD_PAD * (h + 1))
        kh = _head_norm(kup[:, sl] + kr, kn_ref[...])
        if rope:
            kh = _rope(kh, cos, sa, sb)
        k_ref[:, sl] = kh.astype(BF16)
        if has_q:
            qh = _head_norm(q[:, sl], qn_ref[...])
            if rope:
                qh = _rope(qh, cos, sa, sb)
            q_ref[:, sl] = (qh * (MLA_QK ** -0.5)).astype(BF16)
        if h % 2 == 1:
            yield


def _mla_prep_kernel(has_q, rope, m_ref, *refs):
    for _ in _mla_steps(m_ref[...], has_q, rope, 0, 0, refs):
        pass


def _inproj_kernel(fuse_mla, cache_seqs, n_aliased, x_ref, mod_ref, nw_ref, w_ref, *refs):
    n_mla_in = 7 + n_aliased if fuse_mla else 0
    mla_refs, (o_mla, o_g, o_hy, o_s5, o_gla) = refs[:n_mla_in], refs[n_mla_in:n_mla_in + 5]
    x = x_ref[...]
    ms = jnp.mean(x * x, axis=-1, keepdims=True)
    y = x * lax.rsqrt(ms + EPS) * nw_ref[...]
    h = (y * (1.0 + mod_ref[0, 1:2, :]) + mod_ref[0, 0:1, :]).astype(BF16)
    project = lambda o, seg: o.__setitem__(Ellipsis, _dot(h, w_ref[:, seg[0]:seg[1]]).astype(o.dtype))
    m = _dot(h, w_ref[:, SEG_MLA[0]:SEG_MLA[1]])
    o_mla[...] = m
    steps = iter(()) if not fuse_mla else _mla_steps(
        m, True, False, cache_seqs, n_aliased, list(mla_refs) + list(refs[n_mla_in + 5:]))
    project(o_g, SEG_GATE)
    next(steps, None)
    next(steps, None)
    project(o_hy, SEG_HY)
    next(steps, None)
    project(o_s5, SEG_S5)
    project(o_gla, SEG_GLA)
    for _ in steps:
        pass


def _inproj(x, mod, mod_row, wl, tm, seq_len, cache=None):
    n, d = x.shape
    fuse_mla = cache is not None
    full = lambda shape: pl.BlockSpec(shape, lambda i: (0,) * len(shape))
    row = lambda w: pl.BlockSpec((tm, w), lambda i: (i, 0))
    args = [x, mod, wl["norm_w"], wl["w_in"]]
    specs = [row(d), pl.BlockSpec((1, 3, d), lambda i: (mod_row(i * tm), 0, 0)), full((1, d)), full((d, N_PROJ))]
    widths = [hi - lo for lo, hi in (SEG_MLA, SEG_GATE, SEG_HY, SEG_S5, SEG_GLA)]
    dtypes = [F32, BF16, F32, F32, F32]
    aliases = {}
    nseq = 0
    if fuse_mla:
        args += [wl["qa_norm"], wl["w_uq"], wl["q_norm"], wl["kva_norm"], wl["w_uk"], wl["w_uv"], wl["k_norm"]]
        specs += [full((1, 256)), full((256, 512)), full((1, 128)), full((1, 128)),
                  full((128, 512)), full((128, 256)), full((1, 128))]
        widths += [512, 512, 256]
        dtypes += [BF16, BF16, BF16]
    out_specs = [row(w) for w in widths]
    out_shape = [jax.ShapeDtypeStruct((n, w), t) for w, t in zip(widths, dtypes)]
    if fuse_mla:
        layer, prev = cache
        nseq = tm // seq_len
        for w in (MLA_KV_RANK, MLA_ROPE):
            out_specs.append(pl.BlockSpec((nseq, 1, seq_len, w), lambda i: (i, layer, 0, 0)))
            out_shape.append(jax.ShapeDtypeStruct((n // seq_len, DEPTH, seq_len, w), F32))
        for k, buf in enumerate(prev):
            aliases[len(args)] = len(out_shape) - 2 + k
            args.append(buf)
            specs.append(pl.BlockSpec(memory_space=pl.ANY))
    return pl.pallas_call(
        functools.partial(_inproj_kernel, fuse_mla, nseq, len(aliases)),
        grid=(n // tm,),
        in_specs=specs, out_specs=out_specs, out_shape=out_shape,
        input_output_aliases=aliases,
        compiler_params=_cp("arbitrary"),
        name="inproj",
    )(*args)


def _mla_prep(m, wl, rope_tabs, seq_len, tm, has_q):
    n, wm = m.shape
    rope = rope_tabs is not None
    full = lambda shape: pl.BlockSpec(shape, lambda i: (0,) * len(shape))
    row = lambda w: pl.BlockSpec((tm, w), lambda i: (i, 0))
    args, specs = [m], [row(wm)]
    if has_q:
        args += [wl["qa_norm"], wl["w_uq"], wl["q_norm"], wl["kva_norm"]]
        specs += [full((1, 256)), full((256, 512)), full((1, 128)), full((1, 128))]
    args += [wl["w_uk"], wl["w_uv"], wl["k_norm"]]
    specs += [full((128, 512)), full((128, 256)), full((1, 128))]
    if rope:
        nt = seq_len // tm
        args += list(rope_tabs)
        specs += [pl.BlockSpec((tm, HEAD_PAD), lambda i: (i % nt, 0))] * 3
    widths = ([512] if has_q else []) + [512, 256]
    return pl.pallas_call(
        functools.partial(_mla_prep_kernel, has_q, rope),
        grid=(n // tm,),
        in_specs=specs, out_specs=[row(w) for w in widths],
        out_shape=[jax.ShapeDtypeStruct((n, w), BF16) for w in widths],
        compiler_params=_cp("arbitrary"),
        name="mla_prep",
    )(*args)


def _attn_kernel(nparts, nseq, q_ref, *refs):
    kv = [(refs[2 * i], refs[2 * i + 1]) for i in range(nparts)]
    o_ref = refs[2 * nparts]
    tq = q_ref.shape[0] // nseq
    low = lax.broadcasted_iota(jnp.int32, (1, HEAD_PAD), 1) < MLA_V
    units = [(s, h) for s in range(nseq) for h in range(MLA_HEADS)]

    def keys(ref, s):
        lk = ref.shape[0] // nseq
        return slice(s * lk, (s + 1) * lk)

    def scores(s, h):
        sl = slice(HEAD_PAD * h, HEAD_PAD * (h + 1))
        return [_dot_nt(q_ref[s * tq:(s + 1) * tq, sl], k_ref[keys(k_ref, s), sl]) for k_ref, _ in kv]

    s_next = scores(*units[0])
    acc = None
    for n, (s, h) in enumerate(units):
        pair, j = divmod(h, 2)
        sc = s_next
        if n + 1 < len(units):
            s_next = scores(*units[n + 1])
        if j == 0:
            v_half = []
            for _, v_ref in kv:
                vp = v_ref[keys(v_ref, s), HEAD_PAD * pair:HEAD_PAD * (pair + 1)]
                zero = jnp.zeros_like(vp)
                v_half.append((jnp.where(low, vp, zero), jnp.where(low, zero, vp)))
        m = functools.reduce(jnp.maximum, [jnp.max(x, axis=-1, keepdims=True) for x in sc])
        p = [jnp.exp(x - m) for x in sc]
        den = functools.reduce(jnp.add, [jnp.sum(x, axis=-1, keepdims=True) for x in p])
        num = functools.reduce(jnp.add, [_dot(x.astype(BF16), vh[j]) for x, vh in zip(p, v_half)])
        o = num / den
        acc = o if j == 0 else acc + o
        if j == 1:
            o_ref[s * tq:(s + 1) * tq, HEAD_PAD * pair:HEAD_PAD * (pair + 1)] = acc.astype(BF16)


def _attention(q, kv_parts, batch, lq, tq):
    nq = lq // tq
    nseq = max(1, min(batch, 1024 // lq)) if nq == 1 else 1
    args, specs = [q], [pl.BlockSpec((nseq * tq, 512), lambda b, i: (b * nq + i, 0))]
    for k, v, lk in kv_parts:
        args += [k, v]
        specs += [pl.BlockSpec((nseq * lk, 512), lambda b, i: (b, 0)),
                  pl.BlockSpec((nseq * lk, 256), lambda b, i: (b, 0))]
    return pl.pallas_call(
        functools.partial(_attn_kernel, len(kv_parts), nseq),
        grid=(batch // nseq, nq),
        in_specs=specs,
        out_specs=pl.BlockSpec((nseq * tq, 256), lambda b, i: (b * nq + i, 0)),
        out_shape=jax.ShapeDtypeStruct((batch * lq, 256), BF16),
        compiler_params=_cp("arbitrary", "arbitrary"),
        name="attention",
    )(*args)


def _hyena_kernel(seq_len, bk, nseq, x_ref, cw_ref, cb_ref, fo_ref, go_ref, k_ref, bias_ref, o_ref,
                  u_sc, y_sc, z_sc):
    c = D_BRANCH
    n = nseq * seq_len
    nblk = seq_len // bk
    pos = jnp.bitwise_and(lax.broadcasted_iota(jnp.int32, (n, 1), 0), seq_len - 1)
    first, last = pos == 0, pos == seq_len - 1

    def short_conv(g):
        cols = slice(g * c, (g + 1) * c)
        x = x_ref[:, cols]
        xm = jnp.where(first, 0.0, pltpu.roll(x, 1, 0))
        xp = jnp.where(last, 0.0, pltpu.roll(x, n - 1, 0))
        return cw_ref[0:1, cols] * xm + cw_ref[1:2, cols] * x + cw_ref[2:3, cols] * xp + cb_ref[:, cols]

    fo, go = fo_ref[...], go_ref[...]

    def long_conv(s, v, order, emit):
        cols = slice(order * c, (order + 1) * c)
        for j in range(nblk):
            u_sc[s, j] = _dot(fo, v[j * bk:(j + 1) * bk].astype(BF16))
        yield
        rc = 32
        for i in range(nblk):
            def mix(r, carry, i=i):
                top = pl.ds(pl.multiple_of(r * rc, rc), rc)
                bot = pl.ds(pl.multiple_of(bk + r * rc, rc), rc)
                at = ab = None
                for j in range(nblk):
                    q = i - j + nblk - 1
                    kt, kb = k_ref[q, top, cols], k_ref[q, bot, cols]
                    ut, ub = u_sc[s, j, top, :], u_sc[s, j, bot, :]
                    pt, pb = ut * kt - ub * kb, ut * kb + ub * kt
                    at, ab = (pt, pb) if at is None else (at + pt, ab + pb)
                z_sc[s, top, :] = at.astype(BF16)
                z_sc[s, bot, :] = ab.astype(BF16)
                return carry

            lax.fori_loop(0, bk // rc, mix, 0, unroll=True if nblk == 1 else 2)
            y = _dot(go, z_sc[s])
            yield
            emit(i, y)

    v_all, x1_all, x2_all = short_conv(0), short_conv(1), short_conv(2)

    def sequence(s):
        base = s * seq_len
        v = v_all[base:base + seq_len]

        def emit1(i, y):
            r = slice(i * bk, (i + 1) * bk)
            y_sc[s, r, :] = x1_all[base + i * bk:base + (i + 1) * bk] * (y + bias_ref[0:1, :] * v[r])

        yield from long_conv(s, v, 0, emit1)
        y1 = y_sc[s]

        def emit2(i, y):
            r = slice(i * bk, (i + 1) * bk)
            o_ref[base + i * bk:base + (i + 1) * bk, :] = (
                x2_all[base + i * bk:base + (i + 1) * bk] * (y + bias_ref[1:2, :] * y1[r])).astype(BF16)

        yield from long_conv(s, y1, 1, emit2)

    live = [sequence(s) for s in range(nseq)]
    while live:
        for g in list(live):
            if next(g, StopIteration) is StopIteration:
                live.remove(g)


def _hyena(x, wl, kspec, fo, go, batch, seq_len):
    c = D_BRANCH
    bk = fo.shape[1]
    nseq = max(1, min(batch, 1024 // seq_len))
    rows = nseq * seq_len
    nblk = seq_len // bk
    full = lambda a: pl.BlockSpec(a.shape, lambda i: (0,) * a.ndim)
    return pl.pallas_call(
        functools.partial(_hyena_kernel, seq_len, bk, nseq),
        grid=(batch // nseq,),
        in_specs=[pl.BlockSpec((rows, 3 * c), lambda i: (i, 0)),
                  full(wl["hy_conv_w"]), full(wl["hy_conv_b"]), full(fo), full(go),
                  pl.BlockSpec(kspec.shape, lambda i: (0, 0, 0), pipeline_mode=pl.Buffered(1)),
                  full(wl["hy_bias"])],
        out_specs=pl.BlockSpec((rows, c), lambda i: (i, 0)),
        out_shape=jax.ShapeDtypeStruct((batch * seq_len, c), BF16),
        scratch_shapes=[pltpu.VMEM((nseq, nblk, 2 * bk, c), F32), pltpu.VMEM((nseq, seq_len, c), F32),
                        pltpu.VMEM((nseq, 2 * bk, c), BF16)],
        compiler_params=pltpu.CompilerParams(dimension_semantics=("arbitrary",), vmem_limit_bytes=56 * 1024 * 1024),
        name="hyena",
    )(x, wl["hy_conv_w"], wl["hy_conv_b"], fo, go, kspec, wl["hy_bias"])


def _hy_mlp_kernel(feat_ref, w1_ref, b1_ref, f1_ref, w2_ref, b2_ref, f2_ref, w3_ref, win_ref,
                   kern_ref, nrm_ref):
    i = pl.program_id(0)
    tl = win_ref.shape[0]
    h = jnp.sin(f1_ref[...] * (_dot3(feat_ref[...], w1_ref[...]) + b1_ref[...]))
    h = jnp.sin(f2_ref[...] * (_dot3(h, w2_ref[...]) + b2_ref[...]))
    filt = _dot3(h, w3_ref[...])
    filt = jnp.concatenate([filt[:, 0:512], filt[:, 512:1024]], axis=0)
    win = win_ref[...]
    row0 = (lax.broadcasted_iota(jnp.int32, (tl, 1), 0) + i * tl) == 0
    filt = jnp.where(row0, 0.0, filt * jnp.concatenate([win, win], axis=1))
    kern_ref[...] = filt
    part = jnp.sum(jnp.abs(filt), axis=0, keepdims=True)

    @pl.when(i == 0)
    def _():
        nrm_ref[...] = jnp.zeros_like(nrm_ref)
    nrm_ref[...] += jnp.broadcast_to(part, nrm_ref.shape)


def _hy_mlp(feat, wl, win):
    rows = win.shape[0]
    tl = 256
    nl = rows // (2 * tl)
    full = lambda shape: pl.BlockSpec(shape, lambda i: (0,) * len(shape))
    return pl.pallas_call(
        _hy_mlp_kernel,
        grid=(rows // tl,),
        in_specs=[pl.BlockSpec((tl // 2, 128), lambda i: (i, 0)),
                  full((128, 128)), full((1, 128)), full((1, 128)),
                  full((128, 128)), full((1, 128)), full((1, 128)),
                  pl.BlockSpec((128, 1024), lambda i: (0, jnp.where(i < nl, 1, 0))),
                  pl.BlockSpec((tl, 256), lambda i: (i, 0))],
        out_specs=[pl.BlockSpec((tl, 512), lambda i: (i, 0)), full((8, 512))],
        out_shape=[jax.ShapeDtypeStruct((rows, 512), F32), jax.ShapeDtypeStruct((8, 512), F32)],
        compiler_params=_cp("arbitrary"),
        name="hy_mlp",
    )(feat, wl["hy_w1"], wl["hy_b1"], wl["hy_f1"], wl["hy_w2"], wl["hy_b2"], wl["hy_f2"], wl["hy_w3"], win)


def _hy_kspec_kernel(lo_ref, hi_ref, fo_ref, fb_ref, n_ref, o_ref):
    bk = lo_ref.shape[0]
    k = _dot(fo_ref[...], hi_ref[...].astype(BF16)) + _dot(fb_ref[...], lo_ref[...].astype(BF16))
    o_ref[0] = k * ((1.0 / bk) / n_ref[0:1, :])


def _hy_kspec(kern_lin, nrm, fo, fb):
    n2, bk = fo.shape
    nq = kern_lin.shape[0] // bk - 1
    full = lambda a: pl.BlockSpec(a.shape, lambda q: (0,) * a.ndim)
    return pl.pallas_call(
        _hy_kspec_kernel,
        grid=(nq,),
        in_specs=[pl.BlockSpec((bk, 512), lambda q: (q, 0)),
                  pl.BlockSpec((bk, 512), lambda q: (q + 1, 0)),
                  full(fo), full(fb), full(nrm)],
        out_specs=pl.BlockSpec((1, n2, 512), lambda q: (q, 0, 0)),
        out_shape=jax.ShapeDtypeStruct((nq, n2, 512), F32),
        compiler_params=_cp("arbitrary"),
        name="hy_kspec",
    )(kern_lin, kern_lin, fo, fb, nrm)


def _s5_discretise(are_ref, aim_ref, ldt_ref):
    ar = jnp.minimum(are_ref[0], -1e-4)
    ai = aim_ref[0]
    dt = jnp.exp(ldt_ref[0])
    e = jnp.exp(ar * dt)
    return ar, ai, e * jnp.cos(ai * dt), e * jnp.sin(ai * dt)


def _s5_scan_kernel(nseq, emit_y, u_ref, hin_ref, are_ref, aim_ref, ldt_ref, bre_ref, bim_ref, *rest):
    if emit_y:
        cre_ref, cim_ref, y_ref, hfin_ref, wb_sc, ab_sc, s_sc, hc_sc, perm_sc, wc_sc = rest
    else:
        hfin_ref, wb_sc, ab_sc, s_sc, hc_sc, perm_sc = rest
    d = pl.program_id(0)
    c = pl.program_id(1)
    n = S5_N

    @pl.when(c == 0)
    def _():
        ar, ai, abr, abi = _s5_discretise(are_ref, aim_ref, ldt_ref)
        ab_sc[0:1, :] = abr
        ab_sc[1:2, :] = abi
        den = 1.0 / (ar * ar + ai * ai)
        cr = ((abr - 1.0) * ar + abi * ai) * den
        ci = (abi * ar - (abr - 1.0) * ai) * den
        grp_rows = lax.shift_right_logical(lax.broadcasted_iota(jnp.int32, (D_BRANCH, 1), 0), 4)
        grp_cols = lax.shift_right_logical(lax.broadcasted_iota(jnp.int32, (1, n), 1), 6)
        expand_b = lambda ref: jnp.where(grp_rows == grp_cols, jnp.concatenate([ref[0]] * (n // 128), axis=1), 0.0)
        bre, bim = expand_b(bre_ref), expand_b(bim_ref)
        wb_sc[:, 0:n] = (cr * bre - ci * bim).astype(BF16)
        wb_sc[:, n:2 * n] = (cr * bim + ci * bre).astype(BF16)
        if emit_y:
            st_rows = lax.shift_right_logical(lax.broadcasted_iota(jnp.int32, (n, 1), 0), 6)
            ch_cols = lax.shift_right_logical(lax.broadcasted_iota(jnp.int32, (1, D_BRANCH), 1), 4)
            expand_c = lambda ref: jnp.where(st_rows == ch_cols, jnp.concatenate([ref[0]] * 2, axis=1), 0.0)
            wc_sc[0:n, :] = expand_c(cre_ref).astype(BF16)
            wc_sc[n:2 * n, :] = (-expand_c(cim_ref)).astype(BF16)
        hc_sc[...] = hin_ref[0]

    steps = u_ref.shape[1]
    rows_c = nseq * steps

    @pl.when(c == 0)
    def _():
        i = lax.broadcasted_iota(jnp.int32, (rows_c, rows_c), 0)
        j = lax.broadcasted_iota(jnp.int32, (rows_c, rows_c), 1)
        p = lax.shift_right_logical(i, int(math.log2(nseq)))
        step = p + d * (steps - 1 - 2 * p)
        src = jnp.bitwise_and(i, nseq - 1) * steps + step
        perm_sc[...] = jnp.where(j == src, 1.0, 0.0).astype(BF16)

    lhs = _dot(perm_sc[...], u_ref[...].reshape(rows_c, D_BRANCH).astype(BF16)).astype(BF16)
    lb = 256
    y = None
    for j in range(n // lb):
        lr = slice(lb * j, lb * (j + 1))
        li = slice(n + lb * j, n + lb * (j + 1))
        bur = _dot(lhs, wb_sc[:, lr])
        bui = _dot(lhs, wb_sc[:, li])
        abr = ab_sc[0:1, lr]
        abi = ab_sc[1:2, lr]
        hr, hi = hc_sc[:, lr], hc_sc[:, li]
        for p in range(steps):
            rows = slice(p * nseq, (p + 1) * nseq)
            hr, hi = abr * hr - abi * hi + bur[rows], abr * hi + abi * hr + bui[rows]
            if emit_y:
                s_sc[rows, lr] = hr.astype(BF16)
                s_sc[rows, li] = hi.astype(BF16)
        hc_sc[:, lr] = hr
        hc_sc[:, li] = hi
        if emit_y:
            yj = _dot(s_sc[:, lr], wc_sc[lr, :]) + _dot(s_sc[:, li], wc_sc[li, :])
            y = yj if y is None else y + yj

    if emit_y:
        @pl.when(d == 0)
        def _():
            for p in range(steps):
                y_ref[0, :, p, :] = y[p * nseq:(p + 1) * nseq]

        @pl.when(d == 1)
        def _():
            for p in range(steps):
                y_ref[0, :, steps - 1 - p, :] = y[p * nseq:(p + 1) * nseq]

    @pl.when(c == pl.num_programs(1) - 1)
    def _():
        hfin_ref[0] = hc_sc[...]


def _s5_scan(u, hin, wl, emit_y):
    nseq, nstep, _ = u.shape
    steps = S5_ROWS // nseq
    nc = nstep // steps
    n = S5_N
    chunk = lambda d, c: c + d * (nc - 1 - 2 * c)
    per_dir = lambda shape: pl.BlockSpec((1,) + shape, lambda d, c: (d,) + (0,) * len(shape))
    args = [u, hin, wl["s5_are"], wl["s5_aim"], wl["s5_ldt"], wl["s5_bre"], wl["s5_bim"]]
    specs = [pl.BlockSpec((nseq, steps, D_BRANCH), lambda d, c: (0, chunk(d, c), 0)),
             per_dir((nseq, 2 * n)), per_dir((1, n)), per_dir((1, n)), per_dir((1, n)),
             per_dir((D_BRANCH, 128)), per_dir((D_BRANCH, 128))]
    out_specs = [per_dir((nseq, 2 * n))]
    out_shape = [jax.ShapeDtypeStruct((2, nseq, 2 * n), F32)]
    scratch = [pltpu.VMEM((D_BRANCH, 2 * n), BF16), pltpu.VMEM((8, n), F32),
               pltpu.VMEM((S5_ROWS, 2 * n), BF16), pltpu.VMEM((nseq, 2 * n), F32),
               pltpu.VMEM((S5_ROWS, S5_ROWS), BF16)]
    if emit_y:
        args += [wl["s5_cre"], wl["s5_cim"]]
        specs += [per_dir((n, 128)), per_dir((n, 128))]
        out_specs = [pl.BlockSpec((1, nseq, steps, D_BRANCH), lambda d, c: (d, 0, chunk(d, c), 0))] + out_specs
        out_shape = [jax.ShapeDtypeStruct((2, nseq, nstep, D_BRANCH), F32)] + out_shape
        scratch += [pltpu.VMEM((2 * n, D_BRANCH), BF16)]
    return pl.pallas_call(
        functools.partial(_s5_scan_kernel, nseq, emit_y),
        grid=(2, nc),
        in_specs=specs, out_specs=out_specs, out_shape=out_shape, scratch_shapes=scratch,
        compiler_params=_cp("arbitrary", "arbitrary"),
        name="s5_scan" if emit_y else "s5_scan_finals",
    )(*args)


def _s5_chain_kernel(batch, nseg, f_ref, h0_ref, are_ref, aim_ref, ldt_ref, o_ref):
    d = pl.program_id(0)
    n = S5_N
    _, _, pr, pi = _s5_discretise(are_ref, aim_ref, ldt_ref)
    for _ in range(int(math.log2(S5_SEG))):
        pr, pi = pr * pr - pi * pi, 2.0 * pr * pi
    f = f_ref[0]
    fr, fi = f[:, 0:n], f[:, n:2 * n]
    h0 = h0_ref[0]
    h0r, h0i = h0[:, 0:n], h0[:, n:2 * n]
    nrow = batch * nseg
    seg = jnp.bitwise_and(lax.broadcasted_iota(jnp.int32, (nrow, 1), 0), nseg - 1)

    def run(shift, keep):
        xr, xi = h0r, h0i
        for _ in range(nseg - 1):
            zr = fr + pr * xr - pi * xi
            zi = fi + pr * xi + pi * xr
            xr = h0r + jnp.where(keep, pltpu.roll(zr, shift, 0), 0.0)
            xi = h0i + jnp.where(keep, pltpu.roll(zi, shift, 0), 0.0)
        o_ref[0, :, 0:n] = xr
        o_ref[0, :, n:2 * n] = xi

    @pl.when(d == 0)
    def _():
        run(1, seg != 0)

    @pl.when(d == 1)
    def _():
        run(nrow - 1, seg != nseg - 1)


def _s5_chain(fin, h0rows, wl, batch, nseg):
    nrow = batch * nseg
    n = S5_N
    per_dir = lambda shape: pl.BlockSpec((1,) + shape, lambda d: (d,) + (0,) * len(shape))
    return pl.pallas_call(
        functools.partial(_s5_chain_kernel, batch, nseg),
        grid=(2,),
        in_specs=[per_dir((nrow, 2 * n)), per_dir((nrow, 2 * n)), per_dir((1, n)), per_dir((1, n)), per_dir((1, n))],
        out_specs=per_dir((nrow, 2 * n)),
        out_shape=jax.ShapeDtypeStruct((2, nrow, 2 * n), F32),
        compiler_params=_cp("arbitrary"),
        name="s5_chain",
    )(fin, h0rows, wl["s5_are"], wl["s5_aim"], wl["s5_ldt"])


def _gla_kernel(seq_len, nb, n_aliased, q_ref, k_ref, v_ref, g_ref, gw_ref, gb_ref, s0_ref, *rest):
    o_ref, sfin_ref, qe_sc, upd_sc, dec_sc, sall_sc, lhs_sc, kt_sc, la_sc, oi_sc = rest[n_aliased:]
    d = pl.program_id(1)
    sign = 1 - 2 * d
    ck, sup = GLA_CHUNK, GLA_SUPER
    cps = sup // ck
    nsup, nchunk = seq_len // sup, seq_len // ck
    dk, dv = GLA_HEADS * GLA_DK, GLA_HEADS * GLA_DV
    r = lax.broadcasted_iota(jnp.int32, (sup, sup), 0)
    s = lax.broadcasted_iota(jnp.int32, (sup, sup), 1)
    same = lax.shift_right_logical(r, 6) == lax.shift_right_logical(s, 6)
    tri = jnp.logical_and(same, (s - r) * sign <= 0)
    cum_lhs = jnp.where(tri, 1.0, 0.0).astype(BF16)
    t4 = lax.broadcasted_iota(jnp.int32, (ck, GLA_HEADS * ck), 0)
    s4 = jnp.bitwise_and(lax.broadcasted_iota(jnp.int32, (ck, GLA_HEADS * ck), 1), ck - 1)
    tri4 = (s4 - t4) * sign <= 0
    pos = jnp.bitwise_and(lax.broadcasted_iota(jnp.int32, (ck, 1), 0), ck - 1)
    is_last = pos == (ck - 1) * (1 - d)
    row_chunk = lax.shift_right_logical(lax.broadcasted_iota(jnp.int32, (sup, 1), 0), 6)
    head_k = lax.shift_right_logical(lax.broadcasted_iota(jnp.int32, (1, dk), 1), 5)
    head_v = lax.shift_right_logical(lax.broadcasted_iota(jnp.int32, (1, dv), 1), 6)
    blockdiag = lax.shift_right_logical(lax.broadcasted_iota(jnp.int32, (dv, 1), 0), 6) == head_k

    def group_rows(u):
        return pl.ds(u * sup, sup) if isinstance(u, int) else pl.ds(pl.multiple_of(u * sup, sup), sup)

    def stage_a(u, slot):
        rows = group_rows(u)
        q = q_ref[rows, :] * (GLA_DK ** -0.5)
        k = k_ref[rows, :]
        v = v_ref[rows, :]
        cs = _dot(cum_lhs, la_sc[rows, :])
        yield
        bc = cs[:, 0:dk] + cs[:, dk:2 * dk]
        tots = [jnp.sum(jnp.where(is_last, bc[c * ck:(c + 1) * ck], 0.0), axis=0, keepdims=True)
                for c in range(cps)]
        tot = jnp.concatenate([jnp.broadcast_to(t, (ck, dk)) for t in tots], axis=0)
        ref = 0.5 * tot
        kt_sc[slot] = (k * jnp.exp(ref - bc)).astype(BF16)
        lhs_sc[slot] = (q * jnp.exp(bc - ref)).astype(BF16)
        qe_sc[rows, :] = (q * jnp.exp(bc)).astype(BF16)
        kl = (k * jnp.exp(tot - bc)).astype(BF16)
        zero = jnp.zeros_like(kl)
        klx = jnp.concatenate([jnp.where(row_chunk == c, kl, zero) for c in range(cps)], axis=1)
        upd = _dot_tn(v.astype(BF16), klx)
        yield
        for c in range(cps):
            upd_sc[u * cps + c] = jnp.where(blockdiag, upd[:, c * dk:(c + 1) * dk], 0.0)
            dec_sc[u * cps + c] = jnp.broadcast_to(jnp.exp(tots[c]), (8, dk))

    def stage_b(u, slot):
        rows = group_rows(u)
        v = v_ref[rows, :].astype(BF16)
        qt, kt = lhs_sc[slot], kt_sc[slot]
        zk, zv = jnp.zeros_like(kt[0:ck]), jnp.zeros_like(v[0:ck])
        p = [_dot_nt(qt[c * ck:(c + 1) * ck],
                     jnp.concatenate([jnp.where(head_k == h, kt[c * ck:(c + 1) * ck], zk)
                                      for h in range(GLA_HEADS)], axis=0)) for c in range(cps)]
        yield
        o = [_dot(jnp.where(tri4, p[c], 0.0).astype(BF16),
                  jnp.concatenate([jnp.where(head_v == h, v[c * ck:(c + 1) * ck], zv)
                                   for h in range(GLA_HEADS)], axis=0)) for c in range(cps)]
        yield
        oi_sc[rows, :] = jnp.concatenate(o, axis=0)

    def run(*stages):
        live = list(stages)
        while live:
            for g in list(live):
                if next(g, StopIteration) is StopIteration:
                    live.remove(g)

    x = _dot(g_ref[...].astype(BF16), gw_ref[0]) + gb_ref[0]
    la = (jnp.minimum(x, 0.0) - jnp.log(1.0 + jnp.exp(-jnp.abs(x)))) * (1.0 / GLA_TAU)
    la_sc[...] = jnp.concatenate(_split2(la), axis=1)

    nu = nb * nsup
    if nu <= 4:
        run(stage_a(0, 0))
        for u in range(nu):
            if u + 1 < nu:
                run(stage_b(u, u % 2), stage_a(u + 1, (u + 1) % 2))
            else:
                run(stage_b(u, u % 2))
    else:
        run(stage_a(0, 0))

        def sup_body(u, carry):
            nxt = jnp.minimum(u + 1, nu - 1)
            run(stage_b(u, jnp.bitwise_and(u, 1)), stage_a(nxt, jnp.bitwise_and(u + 1, 1)))
            return carry

        lax.fori_loop(0, nu, sup_body, 0)

    for j in range(nb):
        def state_body(c, st, j=j):
            ci = j * nchunk + c + d * (nchunk - 1 - 2 * c)
            sall_sc[ci] = st.astype(BF16)
            return dec_sc[ci][0:1, :] * st + upd_sc[ci]

        st_fin = jnp.transpose(lax.fori_loop(0, nchunk, state_body, s0_ref[j, 0]))
        for h in range(GLA_HEADS):
            sfin_ref[j, 0, 0, h] = st_fin[h * GLA_DK:(h + 1) * GLA_DK, h * GLA_DV:(h + 1) * GLA_DV]

    def inter(u):
        rows = group_rows(u)
        qe = qe_sc[rows, :]
        oi = jnp.concatenate([_dot_nt(qe[c * ck:(c + 1) * ck], sall_sc[u * cps + c]) for c in range(cps)], axis=0)
        o_ref[0, rows, :] = (oi_sc[rows, :] + oi).astype(BF16)

    if nu <= 4:
        for u in range(nu):
            inter(u)
    else:
        lax.fori_loop(0, nu, lambda u, carry: (inter(u), carry)[1], 0, unroll=2 if nu % 2 == 0 else 1)


def _gla(gla_in, mla_in, wl, s0t, batch, seq_len, fin=(0, None, 1)):
    layer, prev_fin, fin_layers = fin
    aliases = {} if prev_fin is None else {7: 1}
    extra = [] if prev_fin is None else [prev_fin]
    n = gla_in.shape[0]
    dk, dv = GLA_HEADS * GLA_DK, GLA_HEADS * GLA_DV
    nb = max(1, min(batch, 1024 // seq_len))
    rows = nb * seq_len
    nchunk = nb * (seq_len // GLA_CHUNK)
    return pl.pallas_call(
        functools.partial(_gla_kernel, seq_len, nb, len(extra)),
        grid=(batch // nb, 2),
        input_output_aliases=aliases,
        in_specs=[pl.BlockSpec((rows, dk), lambda b, d: (b, 0)),
                  pl.BlockSpec((rows, dk), lambda b, d: (b, 1)),
                  pl.BlockSpec((rows, dv), lambda b, d: (b, 1)),
                  pl.BlockSpec((rows, dk), lambda b, d: (b, 1)),
                  pl.BlockSpec((1, dk, dk), lambda b, d: (d, 0, 0)),
                  pl.BlockSpec((1, 1, dk), lambda b, d: (d, 0, 0)),
                  pl.BlockSpec((nb, 1, dv, dk), lambda b, d: (b, d, 0, 0))]
                 + [pl.BlockSpec(memory_space=pl.ANY)] * len(extra),
        out_specs=[pl.BlockSpec((1, rows, dv), lambda b, d: (d, b, 0)),
                   pl.BlockSpec((nb, 1, 1, GLA_HEADS, GLA_DK, GLA_DV), lambda b, d: (b, layer, d, 0, 0, 0))],
        out_shape=[jax.ShapeDtypeStruct((2, n, dv), BF16),
                   jax.ShapeDtypeStruct((batch, fin_layers, 2, GLA_HEADS, GLA_DK, GLA_DV), F32)],
        scratch_shapes=[pltpu.VMEM((rows, dk), BF16),
                        pltpu.VMEM((nchunk, dv, dk), F32),
                        pltpu.VMEM((nchunk, 8, dk), F32),
                        pltpu.VMEM((nchunk, dv, dk), BF16),
                        pltpu.VMEM((2, GLA_SUPER, dk), BF16),
                        pltpu.VMEM((2, GLA_SUPER, dk), BF16),
                        pltpu.VMEM((rows, 2 * dk), BF16),
                        pltpu.VMEM((rows, dv), F32)],
        compiler_params=_cp("arbitrary", "arbitrary"),
        name="gla",
    )(gla_in, gla_in, gla_in, mla_in, wl["gla_gw"], wl["gla_gb"], s0t, *extra)


def _outproj_kernel(x_ref, mod_ref, g_ref, om_ref, oh_ref, su_ref, sf_ref, sb_ref, sd_ref, sw_ref, sbias_ref,
                    gf_ref, gb_ref, gn_ref, hm_ref, w_ref, y_ref):
    c = D_BRANCH
    g = g_ref[...].astype(F32)
    acc = _dot((om_ref[...].astype(F32) * _silu(g[:, 0:c])).astype(BF16), w_ref[0:c, :])
    acc += _dot((oh_ref[...].astype(F32) * _silu(g[:, c:2 * c])).astype(BF16), w_ref[c:2 * c, :])
    ys = sd_ref[...] * su_ref[...] + sf_ref[0] + sb_ref[0]
    ge = 0.5 * ys * (1.0 + jnp.tanh(math.sqrt(2.0 / math.pi) * (ys + 0.044715 * (ys * ys * ys))))
    o_s5 = ge / (1.0 + jnp.exp(-(_dot(ge.astype(BF16), sw_ref[...]) + sbias_ref[...])))
    acc += _dot((o_s5 * _silu(g[:, 2 * c:3 * c])).astype(BF16), w_ref[2 * c:3 * c, :])
    og = gf_ref[0].astype(F32) + gb_ref[0].astype(F32)
    hi, lo = _split2(og * og)
    ms = _dot(hi, hm_ref[...]) + _dot(lo, hm_ref[...])
    ogn = og * lax.rsqrt(ms + EPS) * gn_ref[...]
    acc += _dot((ogn * _silu(g[:, 3 * c:4 * c])).astype(BF16), w_ref[3 * c:4 * c, :])
    y_ref[...] = x_ref[...] + mod_ref[0, 2:3, :] * acc


def _outproj(x, mod, mod_row, gates, o_mla, o_hy, s5_u, s5_y, o_gla, wl, tm):
    n, d = x.shape
    c = D_BRANCH
    row = lambda w: pl.BlockSpec((tm, w), lambda i: (i, 0))
    per_dir = lambda k: pl.BlockSpec((1, tm, c), lambda i: (k, i, 0))
    full = lambda *shape: pl.BlockSpec(shape, lambda i: (0,) * len(shape))
    return pl.pallas_call(
        _outproj_kernel,
        grid=(n // tm,),
        in_specs=[row(d),
                  pl.BlockSpec((1, 3, d), lambda i: (mod_row(i * tm), 0, 0)),
                  row(d), row(c), row(c),
                  row(c), per_dir(0), per_dir(1), full(1, c), full(c, c), full(1, c),
                  per_dir(0), per_dir(1), full(1, c), full(c, c), full(d, d)],
        out_specs=row(d),
        out_shape=jax.ShapeDtypeStruct((n, d), F32),
        compiler_params=_cp("arbitrary"),
        name="outproj",
    )(x, mod, gates, o_mla, o_hy, s5_u, s5_y, s5_y, wl["s5_d"], wl["s5_glu_w"], wl["s5_glu_b"],
      o_gla, o_gla, wl["gla_norm"], wl["head_mean"], wl["w_out"])


def _rope_tables(seq_len):
    pos = np.arange(seq_len)
    inv = ROPE_BASE ** (-np.arange(0, 16, 2, dtype=np.float64) / 16.0)
    cos = np.ones((seq_len, HEAD_PAD))
    sin_a = np.zeros((seq_len, HEAD_PAD))
    sin_b = np.zeros((seq_len, HEAD_PAD))
    for base, p in ((MLA_NOPE, pos // GRID_W), (MLA_NOPE + 16, pos % GRID_W)):
        ang = p[:, None].astype(np.float64) * inv[None, :]
        cos[:, base:base + 8] = np.cos(ang)
        cos[:, base + 8:base + 16] = np.cos(ang)
        sin_a[:, base:base + 8] = -np.sin(ang)
        sin_b[:, base + 8:base + 16] = np.sin(ang)
    return tuple(jnp.asarray(t, F32) for t in (cos, sin_a, sin_b))


def _odd_dft(seq_len):
    bk = min(seq_len, HY_BLOCK)
    k = np.arange(bk)[:, None]
    t = np.arange(bk)[None, :]

    def mat(shift):
        ang = (np.pi / (2 * bk)) * (((2 * k + 1) * (t + shift)) % (4 * bk))
        return np.concatenate([np.cos(ang), -np.sin(ang)], axis=0)

    fo = mat(0)
    fb = -mat(bk)
    fb[:, 0] = 0.0
    const = lambda a: jnp.asarray(a, F32).astype(BF16)
    return const(fo), const(fb), const(fo.T)


def _hyena_tables(seq_len):
    lag = np.arange(-seq_len, seq_len)
    pos = np.where(lag == -seq_len, 0, np.abs(lag)).astype(np.float64)
    t = pos / seq_len
    w = 2.0 * np.pi * pos / seq_len
    bands = np.linspace(1e-4, HY_BANDS - 1, HY_BANDS)
    feat = np.zeros((2 * seq_len, HY_HIDDEN))
    feat[:, 0] = t
    feat[:, 1:1 + HY_BANDS] = np.cos(w[:, None] * bands)
    feat[:, 1 + HY_BANDS:HY_FEAT] = np.sin(w[:, None] * bands)
    feat = feat.reshape(-1, 2, 128, HY_HIDDEN).transpose(0, 2, 1, 3).reshape(seq_len, 2 * HY_HIDDEN)
    deltas = np.linspace(math.log(1.0 / HY_TARGET) / HY_FAST_DECAY, math.log(1.0 / HY_TARGET) / HY_SLOW_DECAY,
                         D_BRANCH)
    win = np.exp(-t[:, None] * deltas[None, :]) + HY_SHIFT
    return jnp.asarray(feat, F32), jnp.asarray(win, F32)


def _pad_to(a, shape):
    return jnp.pad(a, [(0, s - d) for s, d in zip(shape, a.shape)])


def _layer_weights(l, p):
    z = lambda *s: jnp.zeros(s, F32)
    w_in = p["w_in"][l]
    col = lambda lo, hi: w_in[:, lo:hi]
    d = D_MODEL
    w_p = jnp.concatenate([
        col(0, 192), col(320, 352), col(2656, 2688), col(192, 320),
        col(352, 608), col(1376, 1632), col(1888, 2144), col(2688, 2944),
        col(608, 1376), col(1632, 1888),
        col(2144, 2272), col(2272, 2400), col(2400, 2656)], axis=1).astype(BF16)
    wl = {"w_in": w_p, "norm_w": p["norm_w"][l].reshape(1, d), "w_out": p["w_out"][l].astype(BF16)}
    wl["qa_norm"] = _pad_to(p["mla_qa_norm"][l].reshape(1, -1), (1, 256))
    w_uq = _pad_to(p["mla_w_uq"][l].reshape(MLA_Q_RANK, MLA_HEADS, MLA_QK), (256, MLA_HEADS, HEAD_PAD))
    wl["w_uq"] = w_uq.reshape(256, MLA_HEADS * HEAD_PAD).astype(BF16)
    wl["q_norm"] = _pad_to(p["mla_q_norm"][l].reshape(1, -1), (1, HEAD_PAD))
    wl["k_norm"] = _pad_to(p["mla_k_norm"][l].reshape(1, -1), (1, HEAD_PAD))
    wl["kva_norm"] = p["mla_kva_norm"][l].reshape(1, -1)
    w_ukv = p["mla_w_ukv"][l].reshape(MLA_KV_RANK, MLA_HEADS, MLA_NOPE + MLA_V)
    wl["w_uk"] = _pad_to(w_ukv[:, :, :MLA_NOPE], (MLA_KV_RANK, MLA_HEADS, HEAD_PAD)).reshape(MLA_KV_RANK, -1).astype(BF16)
    wl["w_uv"] = w_ukv[:, :, MLA_NOPE:].reshape(MLA_KV_RANK, MLA_HEADS * MLA_V).astype(BF16)
    wl["hy_conv_w"] = p["hy_conv_w"][l]
    wl["hy_conv_b"] = p["hy_conv_b"][l].reshape(1, -1)
    hh = HY_HIDDEN
    twice = lambda a: jnp.tile(a.reshape(1, -1), (1, 2))
    bdiag = lambda a: jnp.concatenate([jnp.pad(a, ((0, 0), (0, a.shape[1]))), jnp.pad(a, ((0, 0), (a.shape[1], 0)))], axis=0)
    wl["hy_w1"] = bdiag(_pad_to(p["hy_w1"][l], (hh, hh)))
    wl["hy_b1"], wl["hy_f1"] = twice(p["hy_b1"][l]), twice(p["hy_freq1"][l])
    wl["hy_w2"] = bdiag(p["hy_w2"][l])
    wl["hy_b2"], wl["hy_f2"] = twice(p["hy_b2"][l]), twice(p["hy_freq2"][l])
    w3 = p["hy_w3"][l]
    wl["hy_w3"] = jnp.concatenate([bdiag(w3[:, 0:512]), bdiag(w3[:, 512:1024])], axis=1)
    wl["hy_bias"] = p["hy_bias"][l]
    flat = lambda a: a[l].reshape(2, 1, S5_N)
    wl["s5_are"], wl["s5_aim"] = flat(p["s5_a_re"]), flat(p["s5_a_im"])
    wl["s5_ldt"] = jnp.repeat(p["s5_log_dt"][l], S5_STATE, axis=-1).reshape(2, 1, S5_N)
    lanes_b = lambda a: jnp.tile(a[l].transpose(0, 1, 3, 2).reshape(2, D_BRANCH, S5_STATE), (1, 1, 128 // S5_STATE))
    lanes_c = lambda a: jnp.tile(a[l].transpose(0, 1, 3, 2).reshape(2, S5_N, S5_GROUP), (1, 1, 128 // S5_GROUP))
    wl["s5_bre"], wl["s5_bim"] = lanes_b(p["s5_b_re"]), lanes_b(p["s5_b_im"])
    wl["s5_cre"], wl["s5_cim"] = lanes_c(p["s5_c_re"]), lanes_c(p["s5_c_im"])
    wl["s5_d"] = p["s5_d"][l].reshape(1, -1)
    wl["s5_glu_w"] = p["s5_glu_w"][l].astype(BF16)
    wl["s5_glu_b"] = p["s5_glu_b"][l].reshape(1, -1)
    gw = p["gla_gw"][l]
    dk = GLA_HEADS * GLA_DK
    wl["gla_gw"] = jnp.stack([_pad_to(jnp.pad(gw[i], ((GLA_G_LANE + GLA_RANK * i, 0), (0, 0))), (dk, dk))
                              for i in range(2)]).astype(BF16)
    wl["gla_gb"] = p["gla_gb"][l].reshape(2, 1, dk)
    wl["gla_norm"] = jnp.tile(p["gla_norm"][l], GLA_HEADS).reshape(1, -1)
    head = np.arange(D_BRANCH) // GLA_DV
    wl["head_mean"] = jnp.asarray((head[:, None] == head[None, :]) / GLA_DV, BF16)
    return wl


def _hyena_filters(wl, tabs):
    feat, win, fo, fb, _ = tabs
    kern_lin, nrm = _hy_mlp(feat, wl, win)
    return _hy_kspec(kern_lin, nrm, fo, fb)


def _trunk_layer(x, mod, mod_row, wl, batch, seq_len, hy_tabs, rope_tabs=None, ctx=None, layer=0, cache_bufs=None):
    n = batch * seq_len
    tm = 512
    if ctx is None:
        mla_in, gates, hy_in, s5_in, gla_in, q, k, v, ckv, krope = _inproj(
            x, mod, mod_row, wl, tm, seq_len, (layer, cache_bufs[:2]))
        kv_parts = [(k, v, seq_len)]
    else:
        mla_in, gates, hy_in, s5_in, gla_in = _inproj(x, mod, mod_row, wl, tm, seq_len)
        q, k, v = _mla_prep(mla_in, wl, rope_tabs, seq_len, tm, True)
        ckv = krope = None
        k_ctx, v_ctx = _mla_prep(ctx["mla"], wl, None, ctx["past"], 512, False)
        kv_parts = [(k_ctx, v_ctx, ctx["past"]), (k, v, seq_len)]
    o_mla = _attention(q, kv_parts, batch, seq_len, 256)

    o_hy = _hyena(hy_in, wl, _hyena_filters(wl, hy_tabs), hy_tabs[2], hy_tabs[4], batch, seq_len)

    nseg = seq_len // S5_SEG
    nseq = batch * nseg
    u_seg = s5_in.reshape(nseq, S5_SEG, D_BRANCH)
    if ctx is None:
        hin = jnp.zeros((2, nseq, 2 * S5_N), F32)
    else:
        (fin,) = _s5_scan(u_seg, jnp.zeros((2, nseq, 2 * S5_N), F32), wl, False)
        hin = _s5_chain(fin, ctx["s5_h0"], wl, batch, nseg)
    y2, s5_fin = _s5_scan(u_seg, hin, wl, True)

    s0 = jnp.zeros((batch, 2, GLA_HEADS * GLA_DV, GLA_HEADS * GLA_DK), F32) if ctx is None else ctx["gla_s0"]
    fin = (0, None, 1) if ctx is not None else (layer, None if cache_bufs is None else cache_bufs[2], DEPTH)
    o_gla, gla_fin = _gla(gla_in, mla_in, wl, s0, batch, seq_len, fin)

    y = _outproj(x, mod, mod_row, gates, o_mla, o_hy, s5_in, y2.reshape(2, n, D_BRANCH), o_gla, wl, 2 * tm)
    return y, (ckv, krope, s5_fin, gla_fin)


def kernel(x_prompt, x_sample, c, cache_mla_ckv, cache_mla_krope, state_s5, state_gla, c_ctx, norm_w, ada_w, ada_b, w_in, w_out, mla_qa_norm, mla_kva_norm, mla_w_uq, mla_w_ukv, mla_q_norm, mla_k_norm, hy_conv_w, hy_conv_b, hy_w1, hy_b1, hy_freq1, hy_w2, hy_b2, hy_freq2, hy_w3, hy_bias, s5_a_re, s5_a_im, s5_log_dt, s5_b_re, s5_b_im, s5_c_re, s5_c_im, s5_d, s5_glu_w, s5_glu_b, gla_gw, gla_gb, gla_norm):
    params = dict(norm_w=norm_w, w_in=w_in, w_out=w_out, mla_qa_norm=mla_qa_norm, mla_kva_norm=mla_kva_norm,
                  mla_w_uq=mla_w_uq, mla_w_ukv=mla_w_ukv, mla_q_norm=mla_q_norm, mla_k_norm=mla_k_norm,
                  hy_conv_w=hy_conv_w, hy_conv_b=hy_conv_b, hy_w1=hy_w1, hy_b1=hy_b1, hy_freq1=hy_freq1,
                  hy_w2=hy_w2, hy_b2=hy_b2, hy_freq2=hy_freq2, hy_w3=hy_w3, hy_bias=hy_bias,
                  s5_a_re=s5_a_re, s5_a_im=s5_a_im, s5_log_dt=s5_log_dt, s5_b_re=s5_b_re, s5_b_im=s5_b_im,
                  s5_c_re=s5_c_re, s5_c_im=s5_c_im, s5_d=s5_d, s5_glu_w=s5_glu_w, s5_glu_b=s5_glu_b,
                  gla_gw=gla_gw, gla_gb=gla_gb, gla_norm=gla_norm)
    bp, lp, d = x_prompt.shape
    bs, ls, _ = x_sample.shape
    past = cache_mla_ckv.shape[2]
    n_s5 = S5_N

    conds = jnp.concatenate([c_ctx[None, :], c, jnp.zeros((8 - 1 - bs, d), F32)], axis=0)
    mods = _modulation(conds, ada_w, ada_b).reshape(DEPTH, 8, 3, d)

    tabs_p = _hyena_tables(lp) + _odd_dft(lp)
    tabs_s = _hyena_tables(ls) + _odd_dft(ls)
    rope_tabs = _rope_tables(ls)
    nseg = ls // S5_SEG

    y_p = x_prompt.reshape(bp * lp, d)
    y_s = x_sample.reshape(bs * ls, d)
    s5_l = []
    cache_bufs = (jnp.zeros((bp, DEPTH, lp, MLA_KV_RANK), F32), jnp.zeros((bp, DEPTH, lp, MLA_ROPE), F32),
                  jnp.zeros((bp, DEPTH, 2, GLA_HEADS, GLA_DK, GLA_DV), F32))
    for l in range(DEPTH):
        wl = _layer_weights(l, params)
        y_p, (ckv, krope, s5_fin, gla_fin) = _trunk_layer(y_p, mods[l], lambda row: 0, wl, bp, lp, tabs_p,
                                                          layer=l, cache_bufs=cache_bufs)
        cache_bufs = (ckv, krope, gla_fin)
        s5_l.append(jnp.stack([s5_fin[:, :, :n_s5], s5_fin[:, :, n_s5:]], axis=-1)
                    .reshape(2, bp, S5_GROUPS, S5_STATE, 2).transpose(1, 0, 2, 3, 4))

        mla_ctx = jnp.concatenate([cache_mla_ckv[:, l], jnp.zeros((bs, past, 64), F32), cache_mla_krope[:, l],
                                   jnp.zeros((bs, past, 32), F32)], axis=-1).reshape(bs * past, 256)
        st = state_s5[:, l]
        h0 = jnp.concatenate([st[..., 0].reshape(bs, 2, n_s5), st[..., 1].reshape(bs, 2, n_s5)], axis=-1)
        h0 = h0.transpose(1, 0, 2)
        h0rows = jnp.zeros((2, nseg * bs, 2 * n_s5), F32)
        h0rows = h0rows.at[0, 0::nseg].set(h0[0]).at[1, nseg - 1::nseg].set(h0[1])
        eye_h = jnp.eye(GLA_HEADS, dtype=F32)
        gla_s0 = jnp.einsum("bdhke,hg->bdhegk", state_gla[:, l], eye_h).reshape(
            bs, 2, GLA_HEADS * GLA_DV, GLA_HEADS * GLA_DK)
        ctx = {"mla": mla_ctx, "past": past, "s5_h0": h0rows, "gla_s0": gla_s0}
        y_s, _ = _trunk_layer(y_s, mods[l], lambda row: 1 + row // ls, wl, bs, ls, tabs_s, rope_tabs, ctx)

    return (y_p.reshape(bp, lp, d), y_s.reshape(bs, ls, d),
            cache_bufs[0], cache_bufs[1], jnp.stack(s5_l, axis=1), cache_bufs[2])
```

```python
import functools
import math

import numpy as np
import jax
import jax.numpy as jnp
from jax import lax
from jax.experimental import pallas as pl
from jax.experimental.pallas import tpu as pltpu

F32 = jnp.float32
BF16 = jnp.bfloat16

D_MODEL = 1024
DEPTH = 2
GRID_W = 64
D_BRANCH = 256
EPS = 1e-6

MLA_HEADS = 4
MLA_Q_RANK = 192
MLA_KV_RANK = 128
MLA_NOPE = 64
MLA_ROPE = 32
MLA_QK = 96
MLA_V = 64
ROPE_BASE = 10000.0
HEAD_PAD = 128

HY_BANDS = 16
HY_FEAT = 33
HY_HIDDEN = 64
HY_SHIFT = 0.05
HY_FAST_DECAY = 0.3
HY_SLOW_DECAY = 1.5
HY_TARGET = 1e-2
HY_BLOCK = 512

S5_GROUP = 16
S5_GROUPS = 16
S5_STATE = 64
S5_N = S5_GROUPS * S5_STATE
S5_ROWS = 512
S5_SEG = 256

GLA_HEADS = 4
GLA_DK = 32
GLA_DV = 64
GLA_RANK = 16
GLA_TAU = 16.0
GLA_CHUNK = 64
GLA_SUPER = 256

SEG_MLA = (0, 384)
SEG_GATE = (384, 1408)
SEG_HY = (1408, 2176)
SEG_S5 = (2176, 2432)
SEG_GLA = (2432, 2944)
N_PROJ = 2944
GLA_G_LANE = 96

VMEM_LIMIT = 48 * 1024 * 1024


def _cp(*sem):
    return pltpu.CompilerParams(dimension_semantics=sem, vmem_limit_bytes=VMEM_LIMIT)


def _dot(a, b):
    return jnp.dot(a, b, preferred_element_type=F32)


def _dot_nt(a, b):
    return lax.dot_general(a, b, (((1,), (1,)), ((), ())), preferred_element_type=F32)


def _dot_tn(a, b):
    return lax.dot_general(a, b, (((0,), (0,)), ((), ())), preferred_element_type=F32)


def _split2(x):
    hi = x.astype(BF16)
    lo = (x - hi.astype(F32)).astype(BF16)
    return hi, lo


def _split3(x):
    h1 = x.astype(BF16)
    r1 = x - h1.astype(F32)
    h2 = r1.astype(BF16)
    h3 = (r1 - h2.astype(F32)).astype(BF16)
    return h1, h2, h3


def _dot3(a, b):
    a1, a2 = _split2(a)
    b1, b2 = _split2(b)
    return _dot(a1, b1) + (_dot(a1, b2) + _dot(a2, b1))


def _silu(z):
    return z / (1.0 + jnp.exp(-z))


def _mod_kernel(c_ref, w_ref, b_ref, o_ref):
    s = _silu(c_ref[...])
    o_ref[0] = _dot(s.astype(BF16), w_ref[0].astype(BF16)) + b_ref[0]


def _modulation(conds, ada_w, ada_b):
    d = D_MODEL
    return pl.pallas_call(
        _mod_kernel,
        grid=(DEPTH, 3),
        in_specs=[pl.BlockSpec((8, d), lambda l, j: (0, 0)),
                  pl.BlockSpec((1, d, d), lambda l, j: (l, 0, j)),
                  pl.BlockSpec((1, 1, d), lambda l, j: (l, 0, j))],
        out_specs=pl.BlockSpec((1, 8, d), lambda l, j: (l, 0, j)),
        out_shape=jax.ShapeDtypeStruct((DEPTH, 8, 3 * d), F32),
        compiler_params=_cp("arbitrary", "arbitrary"),
        name="modulation",
    )(conds, ada_w, ada_b.reshape(DEPTH, 1, 3 * d))


def _head_norm(xh, w):
    ms = jnp.sum(xh * xh, axis=-1, keepdims=True) * (1.0 / MLA_QK)
    return xh * lax.rsqrt(ms + EPS) * w


def _rope(xh, cos, sin_a, sin_b):
    return xh * cos + pltpu.roll(xh, HEAD_PAD - 8, 1) * sin_a + pltpu.roll(xh, 8, 1) * sin_b


def _mla_steps(m, has_q, rope, cache_seqs, n_aliased, refs):
    refs = list(refs)
    if has_q:
        qan_ref, wuq_ref, qn_ref, kvn_ref = refs[:4]
        refs = refs[4:]
    wuk_ref, wuv_ref, kn_ref = refs[:3]
    refs = refs[3:]
    if rope:
        cos_ref, sa_ref, sb_ref = refs[:3]
        refs = refs[3:]
        cos, sa, sb = cos_ref[...], sa_ref[...], sb_ref[...]
    refs = refs[n_aliased:]
    if has_q:
        q_ref = refs.pop(0)
    k_ref, v_ref = refs[:2]
    if cache_seqs:
        ckv_ref, kro_ref = refs[2:]
    if has_q:
        lane = lax.broadcasted_iota(jnp.int32, (1, HEAD_PAD), 1)
        mixed = m[:, 128:256]
        cq = jnp.concatenate([m[:, 0:128], jnp.where(lane < MLA_NOPE, mixed, 0.0)], axis=1)
        ms = jnp.sum(cq * cq, axis=-1, keepdims=True) * (1.0 / MLA_Q_RANK)
        cqn = cq * lax.rsqrt(ms + EPS) * qan_ref[...]
        ckv = m[:, 256:384]
        ckvn = ckv * lax.rsqrt(jnp.mean(ckv * ckv, axis=-1, keepdims=True) + EPS) * kvn_ref[...]
        yield
        q = _dot(cqn.astype(BF16), wuq_ref[...])
        kr = jnp.where(jnp.logical_and(lane >= MLA_NOPE, lane < MLA_NOPE + MLA_ROPE), mixed, 0.0)
        if cache_seqs:
            seq_len = ckv_ref.shape[2]
            for s in range(cache_seqs):
                ckv_ref[s, 0] = ckvn[s * seq_len:(s + 1) * seq_len]
                kro_ref[s, 0] = kr[s * seq_len:(s + 1) * seq_len, MLA_NOPE:MLA_NOPE + MLA_ROPE]
    else:
        ckvn = m[:, 0:128]
        kr = m[:, 128:256]
    cb = ckvn.astype(BF16)
    kup = _dot(cb, wuk_ref[...])
    v_ref[...] = _dot(cb, wuv_ref[...]).astype(BF16)
    yield
    for h in range(MLA_HEADS):
        sl = slice(HEAD_PAD * h, HEAD_PAD * (h + 1))
        kh = _head_norm(kup[:, sl] + kr, kn_ref[...])
        if rope:
            kh = _rope(kh, cos, sa, sb)
        k_ref[:, sl] = kh.astype(BF16)
        if has_q:
            qh = _head_norm(q[:, sl], qn_ref[...])
            if rope:
                qh = _rope(qh, cos, sa, sb)
            q_ref[:, sl] = (qh * (MLA_QK ** -0.5)).astype(BF16)
        if h % 2 == 1:
            yield


def _mla_prep_kernel(has_q, rope, m_ref, *refs):
    for _ in _mla_steps(m_ref[...], has_q, rope, 0, 0, refs):
        pass


def _inproj_kernel(fuse_mla, cache_seqs, n_aliased, x_ref, mod_ref, nw_ref, w_ref, *refs):
    n_mla_in = 7 + n_aliased if fuse_mla else 0
    mla_refs, (o_mla, o_g, o_hy, o_s5, o_gla) = refs[:n_mla_in], refs[n_mla_in:n_mla_in + 5]
    x = x_ref[...]
    ms = jnp.mean(x * x, axis=-1, keepdims=True)
    y = x * lax.rsqrt(ms + EPS) * nw_ref[...]
    h = (y * (1.0 + mod_ref[0, 1:2, :]) + mod_ref[0, 0:1, :]).astype(BF16)
    project = lambda o, seg: o.__setitem__(Ellipsis, _dot(h, w_ref[:, seg[0]:seg[1]]).astype(o.dtype))
    m = _dot(h, w_ref[:, SEG_MLA[0]:SEG_MLA[1]])
    o_mla[...] = m
    steps = iter(()) if not fuse_mla else _mla_steps(
        m, True, False, cache_seqs, n_aliased, list(mla_refs) + list(refs[n_mla_in + 5:]))
    project(o_g, SEG_GATE)
    next(steps, None)
    next(steps, None)
    project(o_hy, SEG_HY)
    next(steps, None)
    project(o_s5, SEG_S5)
    project(o_gla, SEG_GLA)
    for _ in steps:
        pass


def _inproj(x, mod, mod_row, wl, tm, seq_len, cache=None):
    n, d = x.shape
    fuse_mla = cache is not None
    full = lambda shape: pl.BlockSpec(shape, lambda i: (0,) * len(shape))
    row = lambda w: pl.BlockSpec((tm, w), lambda i: (i, 0))
    args = [x, mod, wl["norm_w"], wl["w_in"]]
    specs = [row(d), pl.BlockSpec((1, 3, d), lambda i: (mod_row(i * tm), 0, 0)), full((1, d)), full((d, N_PROJ))]
    widths = [hi - lo for lo, hi in (SEG_MLA, SEG_GATE, SEG_HY, SEG_S5, SEG_GLA)]
    dtypes = [F32, BF16, F32, F32, F32]
    aliases = {}
    nseq = 0
    if fuse_mla:
        args += [wl["qa_norm"], wl["w_uq"], wl["q_norm"], wl["kva_norm"], wl["w_uk"], wl["w_uv"], wl["k_norm"]]
        specs += [full((1, 256)), full((256, 512)), full((1, 128)), full((1, 128)),
                  full((128, 512)), full((128, 256)), full((1, 128))]
        widths += [512, 512, 256]
        dtypes += [BF16, BF16, BF16]
    out_specs = [row(w) for w in widths]
    out_shape = [jax.ShapeDtypeStruct((n, w), t) for w, t in zip(widths, dtypes)]
    if fuse_mla:
        layer, prev = cache
        nseq = tm // seq_len
        for w in (MLA_KV_RANK, MLA_ROPE):
            out_specs.append(pl.BlockSpec((nseq, 1, seq_len, w), lambda i: (i, layer, 0, 0)))
            out_shape.append(jax.ShapeDtypeStruct((n // seq_len, DEPTH, seq_len, w), F32))
        for k, buf in enumerate(prev):
            aliases[len(args)] = len(out_shape) - 2 + k
            args.append(buf)
            specs.append(pl.BlockSpec(memory_space=pl.ANY))
    return pl.pallas_call(
        functools.partial(_inproj_kernel, fuse_mla, nseq, len(aliases)),
        grid=(n // tm,),
        in_specs=specs, out_specs=out_specs, out_shape=out_shape,
        input_output_aliases=aliases,
        compiler_params=_cp("arbitrary"),
        name="inproj",
    )(*args)


def _mla_prep(m, wl, rope_tabs, seq_len, tm, has_q):
    n, wm = m.shape
    rope = rope_tabs is not None
    full = lambda shape: pl.BlockSpec(shape, lambda i: (0,) * len(shape))
    row = lambda w: pl.BlockSpec((tm, w), lambda i: (i, 0))
    args, specs = [m], [row(wm)]
    if has_q:
        args += [wl["qa_norm"], wl["w_uq"], wl["q_norm"], wl["kva_norm"]]
        specs += [full((1, 256)), full((256, 512)), full((1, 128)), full((1, 128))]
    args += [wl["w_uk"], wl["w_uv"], wl["k_norm"]]
    specs += [full((128, 512)), full((128, 256)), full((1, 128))]
    if rope:
        nt = seq_len // tm
        args += list(rope_tabs)
        specs += [pl.BlockSpec((tm, HEAD_PAD), lambda i: (i % nt, 0))] * 3
    widths = ([512] if has_q else []) + [512, 256]
    return pl.pallas_call(
        functools.partial(_mla_prep_kernel, has_q, rope),
        grid=(n // tm,),
        in_specs=specs, out_specs=[row(w) for w in widths],
        out_shape=[jax.ShapeDtypeStruct((n, w), BF16) for w in widths],
        compiler_params=_cp("arbitrary"),
        name="mla_prep",
    )(*args)


def _attn_kernel(nparts, nseq, q_ref, *refs):
    kv = [(refs[2 * i], refs[2 * i + 1]) for i in range(nparts)]
    o_ref = refs[2 * nparts]
    tq = q_ref.shape[0] // nseq
    low = lax.broadcasted_iota(jnp.int32, (1, HEAD_PAD), 1) < MLA_V
    units = [(s, h) for s in range(nseq) for h in range(MLA_HEADS)]

    def keys(ref, s):
        lk = ref.shape[0] // nseq
        return slice(s * lk, (s + 1) * lk)

    def scores(s, h):
        sl = slice(HEAD_PAD * h, HEAD_PAD * (h + 1))
        return [_dot_nt(q_ref[s * tq:(s + 1) * tq, sl], k_ref[keys(k_ref, s), sl]) for k_ref, _ in kv]

    s_next = scores(*units[0])
    acc = None
    for n, (s, h) in enumerate(units):
        pair, j = divmod(h, 2)
        sc = s_next
        if n + 1 < len(units):
            s_next = scores(*units[n + 1])
        if j == 0:
            v_half = []
            for _, v_ref in kv:
                vp = v_ref[keys(v_ref, s), HEAD_PAD * pair:HEAD_PAD * (pair + 1)]
                zero = jnp.zeros_like(vp)
                v_half.append((jnp.where(low, vp, zero), jnp.where(low, zero, vp)))
        m = functools.reduce(jnp.maximum, [jnp.max(x, axis=-1, keepdims=True) for x in sc])
        p = [jnp.exp(x - m) for x in sc]
        den = functools.reduce(jnp.add, [jnp.sum(x, axis=-1, keepdims=True) for x in p])
        num = functools.reduce(jnp.add, [_dot(x.astype(BF16), vh[j]) for x, vh in zip(p, v_half)])
        o = num / den
        acc = o if j == 0 else acc + o
        if j == 1:
            o_ref[s * tq:(s + 1) * tq, HEAD_PAD * pair:HEAD_PAD * (pair + 1)] = acc.astype(BF16)


def _attention(q, kv_parts, batch, lq, tq):
    nq = lq // tq
    nseq = max(1, min(batch, 1024 // lq)) if nq == 1 else 1
    args, specs = [q], [pl.BlockSpec((nseq * tq, 512), lambda b, i: (b * nq + i, 0))]
    for k, v, lk in kv_parts:
        args += [k, v]
        specs += [pl.BlockSpec((nseq * lk, 512), lambda b, i: (b, 0)),
                  pl.BlockSpec((nseq * lk, 256), lambda b, i: (b, 0))]
    return pl.pallas_call(
        functools.partial(_attn_kernel, len(kv_parts), nseq),
        grid=(batch // nseq, nq),
        in_specs=specs,
        out_specs=pl.BlockSpec((nseq * tq, 256), lambda b, i: (b * nq + i, 0)),
        out_shape=jax.ShapeDtypeStruct((batch * lq, 256), BF16),
        compiler_params=_cp("arbitrary", "arbitrary"),
        name="attention",
    )(*args)


def _hyena_kernel(seq_len, bk, nseq, x_ref, cw_ref, cb_ref, fo_ref, go_ref, k_ref, bias_ref, o_ref,
                  u_sc, y_sc, z_sc):
    c = D_BRANCH
    n = nseq * seq_len
    nblk = seq_len // bk
    pos = jnp.bitwise_and(lax.broadcasted_iota(jnp.int32, (n, 1), 0), seq_len - 1)
    first, last = pos == 0, pos == seq_len - 1

    def short_conv(g):
        cols = slice(g * c, (g + 1) * c)
        x = x_ref[:, cols]
        xm = jnp.where(first, 0.0, pltpu.roll(x, 1, 0))
        xp = jnp.where(last, 0.0, pltpu.roll(x, n - 1, 0))
        return cw_ref[0:1, cols] * xm + cw_ref[1:2, cols] * x + cw_ref[2:3, cols] * xp + cb_ref[:, cols]

    fo, go = fo_ref[...], go_ref[...]

    def long_conv(s, v, order, emit):
        cols = slice(order * c, (order + 1) * c)
        for j in range(nblk):
            u_sc[s, j] = _dot(fo, v[j * bk:(j + 1) * bk].astype(BF16))
        yield
        rc = 32
        for i in range(nblk):
            def mix(r, carry, i=i):
                top = pl.ds(pl.multiple_of(r * rc, rc), rc)
                bot = pl.ds(pl.multiple_of(bk + r * rc, rc), rc)
                at = ab = None
                for j in range(nblk):
                    q = i - j + nblk - 1
                    kt, kb = k_ref[q, top, cols], k_ref[q, bot, cols]
                    ut, ub = u_sc[s, j, top, :], u_sc[s, j, bot, :]
                    pt, pb = ut * kt - ub * kb, ut * kb + ub * kt
                    at, ab = (pt, pb) if at is None else (at + pt, ab + pb)
                z_sc[s, top, :] = at.astype(BF16)
                z_sc[s, bot, :] = ab.astype(BF16)
                return carry

            lax.fori_loop(0, bk // rc, mix, 0, unroll=True if nblk == 1 else 2)
            y = _dot(go, z_sc[s])
            yield
            emit(i, y)

    v_all, x1_all, x2_all = short_conv(0), short_conv(1), short_conv(2)

    def sequence(s):
        base = s * seq_len
        v = v_all[base:base + seq_len]

        def emit1(i, y):
            r = slice(i * bk, (i + 1) * bk)
            y_sc[s, r, :] = x1_all[base + i * bk:base + (i + 1) * bk] * (y + bias_ref[0:1, :] * v[r])

        yield from long_conv(s, v, 0, emit1)
        y1 = y_sc[s]

        def emit2(i, y):
            r = slice(i * bk, (i + 1) * bk)
            o_ref[base + i * bk:base + (i + 1) * bk, :] = (
                x2_all[base + i * bk:base + (i + 1) * bk] * (y + bias_ref[1:2, :] * y1[r])).astype(BF16)

        yield from long_conv(s, y1, 1, emit2)

    live = [sequence(s) for s in range(nseq)]
    while live:
        for g in list(live):
            if next(g, StopIteration) is StopIteration:
                live.remove(g)


def _hyena(x, wl, kspec, fo, go, batch, seq_len):
    c = D_BRANCH
    bk = fo.shape[1]
    nseq = max(1, min(batch, 1024 // seq_len))
    rows = nseq * seq_len
    nblk = seq_len // bk
    full = lambda a: pl.BlockSpec(a.shape, lambda i: (0,) * a.ndim)
    return pl.pallas_call(
        functools.partial(_hyena_kernel, seq_len, bk, nseq),
        grid=(batch // nseq,),
        in_specs=[pl.BlockSpec((rows, 3 * c), lambda i: (i, 0)),
                  full(wl["hy_conv_w"]), full(wl["hy_conv_b"]), full(fo), full(go),
                  pl.BlockSpec(kspec.shape, lambda i: (0, 0, 0), pipeline_mode=pl.Buffered(1)),
                  full(wl["hy_bias"])],
        out_specs=pl.BlockSpec((rows, c), lambda i: (i, 0)),
        out_shape=jax.ShapeDtypeStruct((batch * seq_len, c), BF16),
        scratch_shapes=[pltpu.VMEM((nseq, nblk, 2 * bk, c), F32), pltpu.VMEM((nseq, seq_len, c), F32),
                        pltpu.VMEM((nseq, 2 * bk, c), BF16)],
        compiler_params=pltpu.CompilerParams(dimension_semantics=("arbitrary",), vmem_limit_bytes=56 * 1024 * 1024),
        name="hyena",
    )(x, wl["hy_conv_w"], wl["hy_conv_b"], fo, go, kspec, wl["hy_bias"])


def _hy_mlp_kernel(feat_ref, w1_ref, b1_ref, f1_ref, w2_ref, b2_ref, f2_ref, w3_ref, win_ref,
                   kern_ref, nrm_ref):
    i = pl.program_id(0)
    tl = win_ref.shape[0]
    h = jnp.sin(f1_ref[...] * (_dot3(feat_ref[...], w1_ref[...]) + b1_ref[...]))
    h = jnp.sin(f2_ref[...] * (_dot3(h, w2_ref[...]) + b2_ref[...]))
    filt = _dot3(h, w3_ref[...])
    filt = jnp.concatenate([filt[:, 0:512], filt[:, 512:1024]], axis=0)
    win = win_ref[...]
    row0 = (lax.broadcasted_iota(jnp.int32, (tl, 1), 0) + i * tl) == 0
    filt = jnp.where(row0, 0.0, filt * jnp.concatenate([win, win], axis=1))
    kern_ref[...] = filt
    part = jnp.sum(jnp.abs(filt), axis=0, keepdims=True)

    @pl.when(i == 0)
    def _():
        nrm_ref[...] = jnp.zeros_like(nrm_ref)
    nrm_ref[...] += jnp.broadcast_to(part, nrm_ref.shape)


def _hy_mlp(feat, wl, win):
    rows = win.shape[0]
    tl = 256
    nl = rows // (2 * tl)
    full = lambda shape: pl.BlockSpec(shape, lambda i: (0,) * len(shape))
    return pl.pallas_call(
        _hy_mlp_kernel,
        grid=(rows // tl,),
        in_specs=[pl.BlockSpec((tl // 2, 128), lambda i: (i, 0)),
                  full((128, 128)), full((1, 128)), full((1, 128)),
                  full((128, 128)), full((1, 128)), full((1, 128)),
                  pl.BlockSpec((128, 1024), lambda i: (0, jnp.where(i < nl, 1, 0))),
                  pl.BlockSpec((tl, 256), lambda i: (i, 0))],
        out_specs=[pl.BlockSpec((tl, 512), lambda i: (i, 0)), full((8, 512))],
        out_shape=[jax.ShapeDtypeStruct((rows, 512), F32), jax.ShapeDtypeStruct((8, 512), F32)],
        compiler_params=_cp("arbitrary"),
        name="hy_mlp",
    )(feat, wl["hy_w1"], wl["hy_b1"], wl["hy_f1"], wl["hy_w2"], wl["hy_b2"], wl["hy_f2"], wl["hy_w3"], win)


def _hy_kspec_kernel(lo_ref, hi_ref, fo_ref, fb_ref, n_ref, o_ref):
    bk = lo_ref.shape[0]
    k = _dot(fo_ref[...], hi_ref[...].astype(BF16)) + _dot(fb_ref[...], lo_ref[...].astype(BF16))
    o_ref[0] = k * ((1.0 / bk) / n_ref[0:1, :])


def _hy_kspec(kern_lin, nrm, fo, fb):
    n2, bk = fo.shape
    nq = kern_lin.shape[0] // bk - 1
    full = lambda a: pl.BlockSpec(a.shape, lambda q: (0,) * a.ndim)
    return pl.pallas_call(
        _hy_kspec_kernel,
        grid=(nq,),
        in_specs=[pl.BlockSpec((bk, 512), lambda q: (q, 0)),
                  pl.BlockSpec((bk, 512), lambda q: (q + 1, 0)),
                  full(fo), full(fb), full(nrm)],
        out_specs=pl.BlockSpec((1, n2, 512), lambda q: (q, 0, 0)),
        out_shape=jax.ShapeDtypeStruct((nq, n2, 512), F32),
        compiler_params=_cp("arbitrary"),
        name="hy_kspec",
    )(kern_lin, kern_lin, fo, fb, nrm)


def _s5_discretise(are_ref, aim_ref, ldt_ref):
    ar = jnp.minimum(are_ref[0], -1e-4)
    ai = aim_ref[0]
    dt = jnp.exp(ldt_ref[0])
    e = jnp.exp(ar * dt)
    return ar, ai, e * jnp.cos(ai * dt), e * jnp.sin(ai * dt)


def _s5_scan_kernel(nseq, emit_y, u_ref, hin_ref, are_ref, aim_ref, ldt_ref, bre_ref, bim_ref, *rest):
    if emit_y:
        cre_ref, cim_ref, y_ref, hfin_ref, wb_sc, ab_sc, s_sc, hc_sc, perm_sc, wc_sc = rest
    else:
        hfin_ref, wb_sc, ab_sc, s_sc, hc_sc, perm_sc = rest
    d = pl.program_id(0)
    c = pl.program_id(1)
    n = S5_N

    @pl.when(c == 0)
    def _():
        ar, ai, abr, abi = _s5_discretise(are_ref, aim_ref, ldt_ref)
        ab_sc[0:1, :] = abr
        ab_sc[1:2, :] = abi
        den = 1.0 / (ar * ar + ai * ai)
        cr = ((abr - 1.0) * ar + abi * ai) * den
        ci = (abi * ar - (abr - 1.0) * ai) * den
        grp_rows = lax.shift_right_logical(lax.broadcasted_iota(jnp.int32, (D_BRANCH, 1), 0), 4)
        grp_cols = lax.shift_right_logical(lax.broadcasted_iota(jnp.int32, (1, n), 1), 6)
        expand_b = lambda ref: jnp.where(grp_rows == grp_cols, jnp.concatenate([ref[0]] * (n // 128), axis=1), 0.0)
        bre, bim = expand_b(bre_ref), expand_b(bim_ref)
        wb_sc[:, 0:n] = (cr * bre - ci * bim).astype(BF16)
        wb_sc[:, n:2 * n] = (cr * bim + ci * bre).astype(BF16)
        if emit_y:
            st_rows = lax.shift_right_logical(lax.broadcasted_iota(jnp.int32, (n, 1), 0), 6)
            ch_cols = lax.shift_right_logical(lax.broadcasted_iota(jnp.int32, (1, D_BRANCH), 1), 4)
            expand_c = lambda ref: jnp.where(st_rows == ch_cols, jnp.concatenate([ref[0]] * 2, axis=1), 0.0)
            wc_sc[0:n, :] = expand_c(cre_ref).astype(BF16)
            wc_sc[n:2 * n, :] = (-expand_c(cim_ref)).astype(BF16)
        hc_sc[...] = hin_ref[0]

    steps = u_ref.shape[1]
    rows_c = nseq * steps

    @pl.when(c == 0)
    def _():
        i = lax.broadcasted_iota(jnp.int32, (rows_c, rows_c), 0)
        j = lax.broadcasted_iota(jnp.int32, (rows_c, rows_c), 1)
        p = lax.shift_right_logical(i, int(math.log2(nseq)))
        step = p + d * (steps - 1 - 2 * p)
        src = jnp.bitwise_and(i, nseq - 1) * steps + step
        perm_sc[...] = jnp.where(j == src, 1.0, 0.0).astype(BF16)

    lhs = _dot(perm_sc[...], u_ref[...].reshape(rows_c, D_BRANCH).astype(BF16)).astype(BF16)
    lb = 256
    y = None
    for j in range(n // lb):
        lr = slice(lb * j, lb * (j + 1))
        li = slice(n + lb * j, n + lb * (j + 1))
        bur = _dot(lhs, wb_sc[:, lr])
        bui = _dot(lhs, wb_sc[:, li])
        abr = ab_sc[0:1, lr]
        abi = ab_sc[1:2, lr]
        hr, hi = hc_sc[:, lr], hc_sc[:, li]
        for p in range(steps):
            rows = slice(p * nseq, (p + 1) * nseq)
            hr, hi = abr * hr - abi * hi + bur[rows], abr * hi + abi * hr + bui[rows]
            if emit_y:
                s_sc[rows, lr] = hr.astype(BF16)
                s_sc[rows, li] = hi.astype(BF16)
        hc_sc[:, lr] = hr
        hc_sc[:, li] = hi
        if emit_y:
            yj = _dot(s_sc[:, lr], wc_sc[lr, :]) + _dot(s_sc[:, li], wc_sc[li, :])
            y = yj if y is None else y + yj

    if emit_y:
        @pl.when(d == 0)
        def _():
            for p in range(steps):
                y_ref[0, :, p, :] = y[p * nseq:(p + 1) * nseq]

        @pl.when(d == 1)
        def _():
            for p in range(steps):
                y_ref[0, :, steps - 1 - p, :] = y[p * nseq:(p + 1) * nseq]

    @pl.when(c == pl.num_programs(1) - 1)
    def _():
        hfin_ref[0] = hc_sc[...]


def _s5_scan(u, hin, wl, emit_y):
    nseq, nstep, _ = u.shape
    steps = S5_ROWS // nseq
    nc = nstep // steps
    n = S5_N
    chunk = lambda d, c: c + d * (nc - 1 - 2 * c)
    per_dir = lambda shape: pl.BlockSpec((1,) + shape, lambda d, c: (d,) + (0,) * len(shape))
    args = [u, hin, wl["s5_are"], wl["s5_aim"], wl["s5_ldt"], wl["s5_bre"], wl["s5_bim"]]
    specs = [pl.BlockSpec((nseq, steps, D_BRANCH), lambda d, c: (0, chunk(d, c), 0)),
             per_dir((nseq, 2 * n)), per_dir((1, n)), per_dir((1, n)), per_dir((1, n)),
             per_dir((D_BRANCH, 128)), per_dir((D_BRANCH, 128))]
    out_specs = [per_dir((nseq, 2 * n))]
    out_shape = [jax.ShapeDtypeStruct((2, nseq, 2 * n), F32)]
    scratch = [pltpu.VMEM((D_BRANCH, 2 * n), BF16), pltpu.VMEM((8, n), F32),
               pltpu.VMEM((S5_ROWS, 2 * n), BF16), pltpu.VMEM((nseq, 2 * n), F32),
               pltpu.VMEM((S5_ROWS, S5_ROWS), BF16)]
    if emit_y:
        args += [wl["s5_cre"], wl["s5_cim"]]
        specs += [per_dir((n, 128)), per_dir((n, 128))]
        out_specs = [pl.BlockSpec((1, nseq, steps, D_BRANCH), lambda d, c: (d, 0, chunk(d, c), 0))] + out_specs
        out_shape = [jax.ShapeDtypeStruct((2, nseq, nstep, D_BRANCH), F32)] + out_shape
        scratch += [pltpu.VMEM((2 * n, D_BRANCH), BF16)]
    return pl.pallas_call(
        functools.partial(_s5_scan_kernel, nseq, emit_y),
        grid=(2, nc),
        in_specs=specs, out_specs=out_specs, out_shape=out_shape, scratch_shapes=scratch,
        compiler_params=_cp("arbitrary", "arbitrary"),
        name="s5_scan" if emit_y else "s5_scan_finals",
    )(*args)


def _s5_chain_kernel(batch, nseg, f_ref, h0_ref, are_ref, aim_ref, ldt_ref, o_ref):
    d = pl.program_id(0)
    n = S5_N
    _, _, pr, pi = _s5_discretise(are_ref, aim_ref, ldt_ref)
    for _ in range(int(math.log2(S5_SEG))):
        pr, pi = pr * pr - pi * pi, 2.0 * pr * pi
    f = f_ref[0]
    fr, fi = f[:, 0:n], f[:, n:2 * n]
    h0 = h0_ref[0]
    h0r, h0i = h0[:, 0:n], h0[:, n:2 * n]
    nrow = batch * nseg
    seg = jnp.bitwise_and(lax.broadcasted_iota(jnp.int32, (nrow, 1), 0), nseg - 1)

    def run(shift, keep):
        xr, xi = h0r, h0i
        for _ in range(nseg - 1):
            zr = fr + pr * xr - pi * xi
            zi = fi + pr * xi + pi * xr
            xr = h0r + jnp.where(keep, pltpu.roll(zr, shift, 0), 0.0)
            xi = h0i + jnp.where(keep, pltpu.roll(zi, shift, 0), 0.0)
        o_ref[0, :, 0:n] = xr
        o_ref[0, :, n:2 * n] = xi

    @pl.when(d == 0)
    def _():
        run(1, seg != 0)

    @pl.when(d == 1)
    def _():
        run(nrow - 1, seg != nseg - 1)


def _s5_chain(fin, h0rows, wl, batch, nseg):
    nrow = batch * nseg
    n = S5_N
    per_dir = lambda shape: pl.BlockSpec((1,) + shape, lambda d: (d,) + (0,) * len(shape))
    return pl.pallas_call(
        functools.partial(_s5_chain_kernel, batch, nseg),
        grid=(2,),
        in_specs=[per_dir((nrow, 2 * n)), per_dir((nrow, 2 * n)), per_dir((1, n)), per_dir((1, n)), per_dir((1, n))],
        out_specs=per_dir((nrow, 2 * n)),
        out_shape=jax.ShapeDtypeStruct((2, nrow, 2 * n), F32),
        compiler_params=_cp("arbitrary"),
        name="s5_chain",
    )(fin, h0rows, wl["s5_are"], wl["s5_aim"], wl["s5_ldt"])


def _gla_kernel(seq_len, nb, n_aliased, q_ref, k_ref, v_ref, g_ref, gw_ref, gb_ref, s0_ref, *rest):
    o_ref, sfin_ref, qe_sc, upd_sc, dec_sc, sall_sc, lhs_sc, kt_sc, la_sc, oi_sc = rest[n_aliased:]
    d = pl.program_id(1)
    sign = 1 - 2 * d
    ck, sup = GLA_CHUNK, GLA_SUPER
    cps = sup // ck
    nsup, nchunk = seq_len // sup, seq_len // ck
    dk, dv = GLA_HEADS * GLA_DK, GLA_HEADS * GLA_DV
    r = lax.broadcasted_iota(jnp.int32, (sup, sup), 0)
    s = lax.broadcasted_iota(jnp.int32, (sup, sup), 1)
    same = lax.shift_right_logical(r, 6) == lax.shift_right_logical(s, 6)
    tri = jnp.logical_and(same, (s - r) * sign <= 0)
    cum_lhs = jnp.where(tri, 1.0, 0.0).astype(BF16)
    t4 = lax.broadcasted_iota(jnp.int32, (ck, GLA_HEADS * ck), 0)
    s4 = jnp.bitwise_and(lax.broadcasted_iota(jnp.int32, (ck, GLA_HEADS * ck), 1), ck - 1)
    tri4 = (s4 - t4) * sign <= 0
    pos = jnp.bitwise_and(lax.broadcasted_iota(jnp.int32, (ck, 1), 0), ck - 1)
    is_last = pos == (ck - 1) * (1 - d)
    row_chunk = lax.shift_right_logical(lax.broadcasted_iota(jnp.int32, (sup, 1), 0), 6)
    head_k = lax.shift_right_logical(lax.broadcasted_iota(jnp.int32, (1, dk), 1), 5)
    head_v = lax.shift_right_logical(lax.broadcasted_iota(jnp.int32, (1, dv), 1), 6)
    blockdiag = lax.shift_right_logical(lax.broadcasted_iota(jnp.int32, (dv, 1), 0), 6) == head_k

    def group_rows(u):
        return pl.ds(u * sup, sup) if isinstance(u, int) else pl.ds(pl.multiple_of(u * sup, sup), sup)

    def stage_a(u, slot):
        rows = group_rows(u)
        q = q_ref[rows, :] * (GLA_DK ** -0.5)
        k = k_ref[rows, :]
        v = v_ref[rows, :]
        cs = _dot(cum_lhs, la_sc[rows, :])
        yield
        bc = cs[:, 0:dk] + cs[:, dk:2 * dk]
        tots = [jnp.sum(jnp.where(is_last, bc[c * ck:(c + 1) * ck], 0.0), axis=0, keepdims=True)
                for c in range(cps)]
        tot = jnp.concatenate([jnp.broadcast_to(t, (ck, dk)) for t in tots], axis=0)
        ref = 0.5 * tot
        kt_sc[slot] = (k * jnp.exp(ref - bc)).astype(BF16)
        lhs_sc[slot] = (q * jnp.exp(bc - ref)).astype(BF16)
        qe_sc[rows, :] = (q * jnp.exp(bc)).astype(BF16)
        kl = (k * jnp.exp(tot - bc)).astype(BF16)
        zero = jnp.zeros_like(kl)
        klx = jnp.concatenate([jnp.where(row_chunk == c, kl, zero) for c in range(cps)], axis=1)
        upd = _dot_tn(v.astype(BF16), klx)
        yield
        for c in range(cps):
            upd_sc[u * cps + c] = jnp.where(blockdiag, upd[:, c * dk:(c + 1) * dk], 0.0)
            dec_sc[u * cps + c] = jnp.broadcast_to(jnp.exp(tots[c]), (8, dk))

    def stage_b(u, slot):
        rows = group_rows(u)
        v = v_ref[rows, :].astype(BF16)
        qt, kt = lhs_sc[slot], kt_sc[slot]
        zk, zv = jnp.zeros_like(kt[0:ck]), jnp.zeros_like(v[0:ck])
        p = [_dot_nt(qt[c * ck:(c + 1) * ck],
                     jnp.concatenate([jnp.where(head_k == h, kt[c * ck:(c + 1) * ck], zk)
                                      for h in range(GLA_HEADS)], axis=0)) for c in range(cps)]
        yield
        o = [_dot(jnp.where(tri4, p[c], 0.0).astype(BF16),
                  jnp.concatenate([jnp.where(head_v == h, v[c * ck:(c + 1) * ck], zv)
                                   for h in range(GLA_HEADS)], axis=0)) for c in range(cps)]
        yield
        oi_sc[rows, :] = jnp.concatenate(o, axis=0)

    def run(*stages):
        live = list(stages)
        while live:
            for g in list(live):
                if next(g, StopIteration) is StopIteration:
                    live.remove(g)

    x = _dot(g_ref[...].astype(BF16), gw_ref[0]) + gb_ref[0]
    la = (jnp.minimum(x, 0.0) - jnp.log(1.0 + jnp.exp(-jnp.abs(x)))) * (1.0 / GLA_TAU)
    la_sc[...] = jnp.concatenate(_split2(la), axis=1)

    nu = nb * nsup
    if nu <= 4:
        run(*[stage_a(u, u) for u in range(min(2, nu))])
        for u in range(0, nu, 2):
            run(*([stage_b(v, v % 4) for v in range(u, min(u + 2, nu))]
                  + [stage_a(v, v % 4) for v in range(u + 2, min(u + 4, nu))]))
    else:
        run(stage_a(0, 0), stage_a(1, 1))

        def sup_body(t, carry):
            u = 2 * t
            n0, n1 = jnp.minimum(u + 2, nu - 2), jnp.minimum(u + 3, nu - 1)
            run(stage_b(u, jnp.bitwise_and(u, 3)), stage_b(u + 1, jnp.bitwise_and(u + 1, 3)),
                stage_a(n0, jnp.bitwise_and(u + 2, 3)), stage_a(n1, jnp.bitwise_and(u + 3, 3)))
            return carry

        lax.fori_loop(0, nu // 2, sup_body, 0)

    for j in range(nb):
        def state_body(c, st, j=j):
            ci = j * nchunk + c + d * (nchunk - 1 - 2 * c)
            sall_sc[ci] = st.astype(BF16)
            return dec_sc[ci][0:1, :] * st + upd_sc[ci]

        st_fin = jnp.transpose(lax.fori_loop(0, nchunk, state_body, s0_ref[j, 0]))
        for h in range(GLA_HEADS):
            sfin_ref[j, 0, 0, h] = st_fin[h * GLA_DK:(h + 1) * GLA_DK, h * GLA_DV:(h + 1) * GLA_DV]

    def inter(u):
        rows = group_rows(u)
        qe = qe_sc[rows, :]
        oi = jnp.concatenate([_dot_nt(qe[c * ck:(c + 1) * ck], sall_sc[u * cps + c]) for c in range(cps)], axis=0)
        o_ref[0, rows, :] = (oi_sc[rows, :] + oi).astype(BF16)

    if nu <= 4:
        for u in range(nu):
            inter(u)
    else:
        lax.fori_loop(0, nu, lambda u, carry: (inter(u), carry)[1], 0, unroll=2 if nu % 2 == 0 else 1)


def _gla(gla_in, mla_in, wl, s0t, batch, seq_len, fin=(0, None, 1)):
    layer, prev_fin, fin_layers = fin
    aliases = {} if prev_fin is None else {7: 1}
    extra = [] if prev_fin is None else [prev_fin]
    n = gla_in.shape[0]
    dk, dv = GLA_HEADS * GLA_DK, GLA_HEADS * GLA_DV
    nb = max(1, min(batch, 1024 // seq_len))
    rows = nb * seq_len
    nchunk = nb * (seq_len // GLA_CHUNK)
    return pl.pallas_call(
        functools.partial(_gla_kernel, seq_len, nb, len(extra)),
        grid=(batch // nb, 2),
        input_output_aliases=aliases,
        in_specs=[pl.BlockSpec((rows, dk), lambda b, d: (b, 0)),
                  pl.BlockSpec((rows, dk), lambda b, d: (b, 1)),
                  pl.BlockSpec((rows, dv), lambda b, d: (b, 1)),
                  pl.BlockSpec((rows, dk), lambda b, d: (b, 1)),
                  pl.BlockSpec((1, dk, dk), lambda b, d: (d, 0, 0)),
                  pl.BlockSpec((1, 1, dk), lambda b, d: (d, 0, 0)),
                  pl.BlockSpec((nb, 1, dv, dk), lambda b, d: (b, d, 0, 0))]
                 + [pl.BlockSpec(memory_space=pl.ANY)] * len(extra),
        out_specs=[pl.BlockSpec((1, rows, dv), lambda b, d: (d, b, 0)),
                   pl.BlockSpec((nb, 1, 1, GLA_HEADS, GLA_DK, GLA_DV), lambda b, d: (b, layer, d, 0, 0, 0))],
        out_shape=[jax.ShapeDtypeStruct((2, n, dv), BF16),
                   jax.ShapeDtypeStruct((batch, fin_layers, 2, GLA_HEADS, GLA_DK, GLA_DV), F32)],
        scratch_shapes=[pltpu.VMEM((rows, dk), BF16),
                        pltpu.VMEM((nchunk, dv, dk), F32),
                        pltpu.VMEM((nchunk, 8, dk), F32),
                        pltpu.VMEM((nchunk, dv, dk), BF16),
                        pltpu.VMEM((4, GLA_SUPER, dk), BF16),
                        pltpu.VMEM((4, GLA_SUPER, dk), BF16),
                        pltpu.VMEM((rows, 2 * dk), BF16),
                        pltpu.VMEM((rows, dv), F32)],
        compiler_params=_cp("arbitrary", "arbitrary"),
        name="gla",
    )(gla_in, gla_in, gla_in, mla_in, wl["gla_gw"], wl["gla_gb"], s0t, *extra)


def _outproj_kernel(x_ref, mod_ref, g_ref, om_ref, oh_ref, su_ref, sf_ref, sb_ref, sd_ref, sw_ref, sbias_ref,
                    gf_ref, gb_ref, gn_ref, hm_ref, w_ref, y_ref):
    c = D_BRANCH
    g = g_ref[...].astype(F32)
    acc = _dot((om_ref[...].astype(F32) * _silu(g[:, 0:c])).astype(BF16), w_ref[0:c, :])
    acc += _dot((oh_ref[...].astype(F32) * _silu(g[:, c:2 * c])).astype(BF16), w_ref[c:2 * c, :])
    ys = sd_ref[...] * su_ref[...] + sf_ref[0] + sb_ref[0]
    ge = 0.5 * ys * (1.0 + jnp.tanh(math.sqrt(2.0 / math.pi) * (ys + 0.044715 * (ys * ys * ys))))
    o_s5 = ge / (1.0 + jnp.exp(-(_dot(ge.astype(BF16), sw_ref[...]) + sbias_ref[...])))
    acc += _dot((o_s5 * _silu(g[:, 2 * c:3 * c])).astype(BF16), w_ref[2 * c:3 * c, :])
    og = gf_ref[0].astype(F32) + gb_ref[0].astype(F32)
    hi, lo = _split2(og * og)
    ms = _dot(hi, hm_ref[...]) + _dot(lo, hm_ref[...])
    ogn = og * lax.rsqrt(ms + EPS) * gn_ref[...]
    acc += _dot((ogn * _silu(g[:, 3 * c:4 * c])).astype(BF16), w_ref[3 * c:4 * c, :])
    y_ref[...] = x_ref[...] + mod_ref[0, 2:3, :] * acc


def _outproj(x, mod, mod_row, gates, o_mla, o_hy, s5_u, s5_y, o_gla, wl, tm):
    n, d = x.shape
    c = D_BRANCH
    row = lambda w: pl.BlockSpec((tm, w), lambda i: (i, 0))
    per_dir = lambda k: pl.BlockSpec((1, tm, c), lambda i: (k, i, 0))
    full = lambda *shape: pl.BlockSpec(shape, lambda i: (0,) * len(shape))
    return pl.pallas_call(
        _outproj_kernel,
        grid=(n // tm,),
        in_specs=[row(d),
                  pl.BlockSpec((1, 3, d), lambda i: (mod_row(i * tm), 0, 0)),
                  row(d), row(c), row(c),
                  row(c), per_dir(0), per_dir(1), full(1, c), full(c, c), full(1, c),
                  per_dir(0), per_dir(1), full(1, c), full(c, c), full(d, d)],
        out_specs=row(d),
        out_shape=jax.ShapeDtypeStruct((n, d), F32),
        compiler_params=_cp("arbitrary"),
        name="outproj",
    )(x, mod, gates, o_mla, o_hy, s5_u, s5_y, s5_y, wl["s5_d"], wl["s5_glu_w"], wl["s5_glu_b"],
      o_gla, o_gla, wl["gla_norm"], wl["head_mean"], wl["w_out"])


def _rope_tables(seq_len):
    pos = np.arange(seq_len)
    inv = ROPE_BASE ** (-np.arange(0, 16, 2, dtype=np.float64) / 16.0)
    cos = np.ones((seq_len, HEAD_PAD))
    sin_a = np.zeros((seq_len, HEAD_PAD))
    sin_b = np.zeros((seq_len, HEAD_PAD))
    for base, p in ((MLA_NOPE, pos // GRID_W), (MLA_NOPE + 16, pos % GRID_W)):
        ang = p[:, None].astype(np.float64) * inv[None, :]
        cos[:, base:base + 8] = np.cos(ang)
        cos[:, base + 8:base + 16] = np.cos(ang)
        sin_a[:, base:base + 8] = -np.sin(ang)
        sin_b[:, base + 8:base + 16] = np.sin(ang)
    return tuple(jnp.asarray(t, F32) for t in (cos, sin_a, sin_b))


def _odd_dft(seq_len):
    bk = min(seq_len, HY_BLOCK)
    k = np.arange(bk)[:, None]
    t = np.arange(bk)[None, :]

    def mat(shift):
        ang = (np.pi / (2 * bk)) * (((2 * k + 1) * (t + shift)) % (4 * bk))
        return np.concatenate([np.cos(ang), -np.sin(ang)], axis=0)

    fo = mat(0)
    fb = -mat(bk)
    fb[:, 0] = 0.0
    const = lambda a: jnp.asarray(a, F32).astype(BF16)
    return const(fo), const(fb), const(fo.T)


def _hyena_tables(seq_len):
    lag = np.arange(-seq_len, seq_len)
    pos = np.where(lag == -seq_len, 0, np.abs(lag)).astype(np.float64)
    t = pos / seq_len
    w = 2.0 * np.pi * pos / seq_len
    bands = np.linspace(1e-4, HY_BANDS - 1, HY_BANDS)
    feat = np.zeros((2 * seq_len, HY_HIDDEN))
    feat[:, 0] = t
    feat[:, 1:1 + HY_BANDS] = np.cos(w[:, None] * bands)
    feat[:, 1 + HY_BANDS:HY_FEAT] = np.sin(w[:, None] * bands)
    feat = feat.reshape(-1, 2, 128, HY_HIDDEN).transpose(0, 2, 1, 3).reshape(seq_len, 2 * HY_HIDDEN)
    deltas = np.linspace(math.log(1.0 / HY_TARGET) / HY_FAST_DECAY, math.log(1.0 / HY_TARGET) / HY_SLOW_DECAY,
                         D_BRANCH)
    win = np.exp(-t[:, None] * deltas[None, :]) + HY_SHIFT
    return jnp.asarray(feat, F32), jnp.asarray(win, F32)


def _pad_to(a, shape):
    return jnp.pad(a, [(0, s - d) for s, d in zip(shape, a.shape)])


def _layer_weights(l, p):
    z = lambda *s: jnp.zeros(s, F32)
    w_in = p["w_in"][l]
    col = lambda lo, hi: w_in[:, lo:hi]
    d = D_MODEL
    w_p = jnp.concatenate([
        col(0, 192), col(320, 352), col(2656, 2688), col(192, 320),
        col(352, 608), col(1376, 1632), col(1888, 2144), col(2688, 2944),
        col(608, 1376), col(1632, 1888),
        col(2144, 2272), col(2272, 2400), col(2400, 2656)], axis=1).astype(BF16)
    wl = {"w_in": w_p, "norm_w": p["norm_w"][l].reshape(1, d), "w_out": p["w_out"][l].astype(BF16)}
    wl["qa_norm"] = _pad_to(p["mla_qa_norm"][l].reshape(1, -1), (1, 256))
    w_uq = _pad_to(p["mla_w_uq"][l].reshape(MLA_Q_RANK, MLA_HEADS, MLA_QK), (256, MLA_HEADS, HEAD_PAD))
    wl["w_uq"] = w_uq.reshape(256, MLA_HEADS * HEAD_PAD).astype(BF16)
    wl["q_norm"] = _pad_to(p["mla_q_norm"][l].reshape(1, -1), (1, HEAD_PAD))
    wl["k_norm"] = _pad_to(p["mla_k_norm"][l].reshape(1, -1), (1, HEAD_PAD))
    wl["kva_norm"] = p["mla_kva_norm"][l].reshape(1, -1)
    w_ukv = p["mla_w_ukv"][l].reshape(MLA_KV_RANK, MLA_HEADS, MLA_NOPE + MLA_V)
    wl["w_uk"] = _pad_to(w_ukv[:, :, :MLA_NOPE], (MLA_KV_RANK, MLA_HEADS, HEAD_PAD)).reshape(MLA_KV_RANK, -1).astype(BF16)
    wl["w_uv"] = w_ukv[:, :, MLA_NOPE:].reshape(MLA_KV_RANK, MLA_HEADS * MLA_V).astype(BF16)
    wl["hy_conv_w"] = p["hy_conv_w"][l]
    wl["hy_conv_b"] = p["hy_conv_b"][l].reshape(1, -1)
    hh = HY_HIDDEN
    twice = lambda a: jnp.tile(a.reshape(1, -1), (1, 2))
    bdiag = lambda a: jnp.concatenate([jnp.pad(a, ((0, 0), (0, a.shape[1]))), jnp.pad(a, ((0, 0), (a.shape[1], 0)))], axis=0)
    wl["hy_w1"] = bdiag(_pad_to(p["hy_w1"][l], (hh, hh)))
    wl["hy_b1"], wl["hy_f1"] = twice(p["hy_b1"][l]), twice(p["hy_freq1"][l])
    wl["hy_w2"] = bdiag(p["hy_w2"][l])
    wl["hy_b2"], wl["hy_f2"] = twice(p["hy_b2"][l]), twice(p["hy_freq2"][l])
    w3 = p["hy_w3"][l]
    wl["hy_w3"] = jnp.concatenate([bdiag(w3[:, 0:512]), bdiag(w3[:, 512:1024])], axis=1)
    wl["hy_bias"] = p["hy_bias"][l]
    flat = lambda a: a[l].reshape(2, 1, S5_N)
    wl["s5_are"], wl["s5_aim"] = flat(p["s5_a_re"]), flat(p["s5_a_im"])
    wl["s5_ldt"] = jnp.repeat(p["s5_log_dt"][l], S5_STATE, axis=-1).reshape(2, 1, S5_N)
    lanes_b = lambda a: jnp.tile(a[l].transpose(0, 1, 3, 2).reshape(2, D_BRANCH, S5_STATE), (1, 1, 128 // S5_STATE))
    lanes_c = lambda a: jnp.tile(a[l].transpose(0, 1, 3, 2).reshape(2, S5_N, S5_GROUP), (1, 1, 128 // S5_GROUP))
    wl["s5_bre"], wl["s5_bim"] = lanes_b(p["s5_b_re"]), lanes_b(p["s5_b_im"])
    wl["s5_cre"], wl["s5_cim"] = lanes_c(p["s5_c_re"]), lanes_c(p["s5_c_im"])
    wl["s5_d"] = p["s5_d"][l].reshape(1, -1)
    wl["s5_glu_w"] = p["s5_glu_w"][l].astype(BF16)
    wl["s5_glu_b"] = p["s5_glu_b"][l].reshape(1, -1)
    gw = p["gla_gw"][l]
    dk = GLA_HEADS * GLA_DK
    wl["gla_gw"] = jnp.stack([_pad_to(jnp.pad(gw[i], ((GLA_G_LANE + GLA_RANK * i, 0), (0, 0))), (dk, dk))
                              for i in range(2)]).astype(BF16)
    wl["gla_gb"] = p["gla_gb"][l].reshape(2, 1, dk)
    wl["gla_norm"] = jnp.tile(p["gla_norm"][l], GLA_HEADS).reshape(1, -1)
    head = np.arange(D_BRANCH) // GLA_DV
    wl["head_mean"] = jnp.asarray((head[:, None] == head[None, :]) / GLA_DV, BF16)
    return wl


def _hyena_filters(wl, tabs):
    feat, win, fo, fb, _ = tabs
    kern_lin, nrm = _hy_mlp(feat, wl, win)
    return _hy_kspec(kern_lin, nrm, fo, fb)


def _trunk_layer(x, mod, mod_row, wl, batch, seq_len, hy_tabs, rope_tabs=None, ctx=None, layer=0, cache_bufs=None):
    n = batch * seq_len
    tm = 512
    if ctx is None:
        mla_in, gates, hy_in, s5_in, gla_in, q, k, v, ckv, krope = _inproj(
            x, mod, mod_row, wl, tm, seq_len, (layer, cache_bufs[:2]))
        kv_parts = [(k, v, seq_len)]
    else:
        mla_in, gates, hy_in, s5_in, gla_in = _inproj(x, mod, mod_row, wl, tm, seq_len)
        q, k, v = _mla_prep(mla_in, wl, rope_tabs, seq_len, tm, True)
        ckv = krope = None
        k_ctx, v_ctx = _mla_prep(ctx["mla"], wl, None, ctx["past"], 512, False)
        kv_parts = [(k_ctx, v_ctx, ctx["past"]), (k, v, seq_len)]
    o_mla = _attention(q, kv_parts, batch, seq_len, 256)

    o_hy = _hyena(hy_in, wl, _hyena_filters(wl, hy_tabs), hy_tabs[2], hy_tabs[4], batch, seq_len)

    nseg = seq_len // S5_SEG
    nseq = batch * nseg
    u_seg = s5_in.reshape(nseq, S5_SEG, D_BRANCH)
    if ctx is None:
        hin = jnp.zeros((2, nseq, 2 * S5_N), F32)
    else:
        (fin,) = _s5_scan(u_seg, jnp.zeros((2, nseq, 2 * S5_N), F32), wl, False)
        hin = _s5_chain(fin, ctx["s5_h0"], wl, batch, nseg)
    y2, s5_fin = _s5_scan(u_seg, hin, wl, True)

    s0 = jnp.zeros((batch, 2, GLA_HEADS * GLA_DV, GLA_HEADS * GLA_DK), F32) if ctx is None else ctx["gla_s0"]
    fin = (0, None, 1) if ctx is not None else (layer, None if cache_bufs is None else cache_bufs[2], DEPTH)
    o_gla, gla_fin = _gla(gla_in, mla_in, wl, s0, batch, seq_len, fin)

    y = _outproj(x, mod, mod_row, gates, o_mla, o_hy, s5_in, y2.reshape(2, n, D_BRANCH), o_gla, wl, 2 * tm)
    return y, (ckv, krope, s5_fin, gla_fin)


def kernel(x_prompt, x_sample, c, cache_mla_ckv, cache_mla_krope, state_s5, state_gla, c_ctx, norm_w, ada_w, ada_b, w_in, w_out, mla_qa_norm, mla_kva_norm, mla_w_uq, mla_w_ukv, mla_q_norm, mla_k_norm, hy_conv_w, hy_conv_b, hy_w1, hy_b1, hy_freq1, hy_w2, hy_b2, hy_freq2, hy_w3, hy_bias, s5_a_re, s5_a_im, s5_log_dt, s5_b_re, s5_b_im, s5_c_re, s5_c_im, s5_d, s5_glu_w, s5_glu_b, gla_gw, gla_gb, gla_norm):
    params = dict(norm_w=norm_w, w_in=w_in, w_out=w_out, mla_qa_norm=mla_qa_norm, mla_kva_norm=mla_kva_norm,
                  mla_w_uq=mla_w_uq, mla_w_ukv=mla_w_ukv, mla_q_norm=mla_q_norm, mla_k_norm=mla_k_norm,
                  hy_conv_w=hy_conv_w, hy_conv_b=hy_conv_b, hy_w1=hy_w1, hy_b1=hy_b1, hy_freq1=hy_freq1,
                  hy_w2=hy_w2, hy_b2=hy_b2, hy_freq2=hy_freq2, hy_w3=hy_w3, hy_bias=hy_bias,
                  s5_a_re=s5_a_re, s5_a_im=s5_a_im, s5_log_dt=s5_log_dt, s5_b_re=s5_b_re, s5_b_im=s5_b_im,
                  s5_c_re=s5_c_re, s5_c_im=s5_c_im, s5_d=s5_d, s5_glu_w=s5_glu_w, s5_glu_b=s5_glu_b,
                  gla_gw=gla_gw, gla_gb=gla_gb, gla_norm=gla_norm)
    bp, lp, d = x_prompt.shape
    bs, ls, _ = x_sample.shape
    past = cache_mla_ckv.shape[2]
    n_s5 = S5_N

    conds = jnp.concatenate([c_ctx[None, :], c, jnp.zeros((8 - 1 - bs, d), F32)], axis=0)
    mods = _modulation(conds, ada_w, ada_b).reshape(DEPTH, 8, 3, d)

    tabs_p = _hyena_tables(lp) + _odd_dft(lp)
    tabs_s = _hyena_tables(ls) + _odd_dft(ls)
    rope_tabs = _rope_tables(ls)
    nseg = ls // S5_SEG

    y_p = x_prompt.reshape(bp * lp, d)
    y_s = x_sample.reshape(bs * ls, d)
    s5_l = []
    cache_bufs = (jnp.zeros((bp, DEPTH, lp, MLA_KV_RANK), F32), jnp.zeros((bp, DEPTH, lp, MLA_ROPE), F32),
                  jnp.zeros((bp, DEPTH, 2, GLA_HEADS, GLA_DK, GLA_DV), F32))
    for l in range(DEPTH):
        wl = _layer_weights(l, params)
        y_p, (ckv, krope, s5_fin, gla_fin) = _trunk_layer(y_p, mods[l], lambda row: 0, wl, bp, lp, tabs_p,
                                                          layer=l, cache_bufs=cache_bufs)
        cache_bufs = (ckv, krope, gla_fin)
        s5_l.append(jnp.stack([s5_fin[:, :, :n_s5], s5_fin[:, :, n_s5:]], axis=-1)
                    .reshape(2, bp, S5_GROUPS, S5_STATE, 2).transpose(1, 0, 2, 3, 4))

        mla_ctx = jnp.concatenate([cache_mla_ckv[:, l], jnp.zeros((bs, past, 64), F32), cache_mla_krope[:, l],
                                   jnp.zeros((bs, past, 32), F32)], axis=-1).reshape(bs * past, 256)
        st = state_s5[:, l]
        h0 = jnp.concatenate([st[..., 0].reshape(bs, 2, n_s5), st[..., 1].reshape(bs, 2, n_s5)], axis=-1)
        h0 = h0.transpose(1, 0, 2)
        h0rows = jnp.zeros((2, nseg * bs, 2 * n_s5), F32)
        h0rows = h0rows.at[0, 0::nseg].set(h0[0]).at[1, nseg - 1::nseg].set(h0[1])
        eye_h = jnp.eye(GLA_HEADS, dtype=F32)
        gla_s0 = jnp.einsum("bdhke,hg->bdhegk", state_gla[:, l], eye_h).reshape(
            bs, 2, GLA_HEADS * GLA_DV, GLA_HEADS * GLA_DK)
        ctx = {"mla": mla_ctx, "past": past, "s5_h0": h0rows, "gla_s0": gla_s0}
        y_s, _ = _trunk_layer(y_s, mods[l], lambda row: 1 + row // ls, wl, bs, ls, tabs_s, rope_tabs, ctx)

    return (y_p.reshape(bp, lp, d), y_s.reshape(bs, ls, d),
            cache_bufs[0], cache_bufs[1], jnp.stack(s5_l, axis=1), cache_bufs[2])
```

```python
import functools
import math

import numpy as np
import jax
import jax.numpy as jnp
from jax import lax
from jax.experimental import pallas as pl
from jax.experimental.pallas import tpu as pltpu

F32 = jnp.float32
BF16 = jnp.bfloat16

D_MODEL = 1024
DEPTH = 2
GRID_W = 64
D_BRANCH = 256
EPS = 1e-6

MLA_HEADS = 4
MLA_Q_RANK = 192
MLA_KV_RANK = 128
MLA_NOPE = 64
MLA_ROPE = 32
MLA_QK = 96
MLA_V = 64
ROPE_BASE = 10000.0
HEAD_PAD = 128

HY_BANDS = 16
HY_FEAT = 33
HY_HIDDEN = 64
HY_SHIFT = 0.05
HY_FAST_DECAY = 0.3
HY_SLOW_DECAY = 1.5
HY_TARGET = 1e-2
HY_BLOCK = 512

S5_GROUP = 16
S5_GROUPS = 16
S5_STATE = 64
S5_N = S5_GROUPS * S5_STATE
S5_ROWS = 512
S5_SEG = 256

GLA_HEADS = 4
GLA_DK = 32
GLA_DV = 64
GLA_RANK = 16
GLA_TAU = 16.0
GLA_CHUNK = 64
GLA_SUPER = 256

SEG_MLA = (0, 384)
SEG_GATE = (384, 1408)
SEG_HY = (1408, 2176)
SEG_S5 = (2176, 2432)
SEG_GLA = (2432, 2944)
N_PROJ = 2944
GLA_G_LANE = 96

VMEM_LIMIT = 48 * 1024 * 1024


def _cp(*sem):
    return pltpu.CompilerParams(dimension_semantics=sem, vmem_limit_bytes=VMEM_LIMIT)


def _dot(a, b):
    return jnp.dot(a, b, preferred_element_type=F32)


def _dot_nt(a, b):
    return lax.dot_general(a, b, (((1,), (1,)), ((), ())), preferred_element_type=F32)


def _dot_tn(a, b):
    return lax.dot_general(a, b, (((0,), (0,)), ((), ())), preferred_element_type=F32)


def _split2(x):
    hi = x.astype(BF16)
    lo = (x - hi.astype(F32)).astype(BF16)
    return hi, lo


def _split3(x):
    h1 = x.astype(BF16)
    r1 = x - h1.astype(F32)
    h2 = r1.astype(BF16)
    h3 = (r1 - h2.astype(F32)).astype(BF16)
    return h1, h2, h3


def _dot3(a, b):
    a1, a2 = _split2(a)
    b1, b2 = _split2(b)
    return _dot(a1, b1) + (_dot(a1, b2) + _dot(a2, b1))


def _silu(z):
    return z / (1.0 + jnp.exp(-z))


def _mod_kernel(c_ref, w_ref, b_ref, o_ref):
    s = _silu(c_ref[...])
    o_ref[0] = _dot(s.astype(BF16), w_ref[0].astype(BF16)) + b_ref[0]


def _modulation(conds, ada_w, ada_b):
    d = D_MODEL
    return pl.pallas_call(
        _mod_kernel,
        grid=(DEPTH, 3),
        in_specs=[pl.BlockSpec((8, d), lambda l, j: (0, 0)),
                  pl.BlockSpec((1, d, d), lambda l, j: (l, 0, j)),
                  pl.BlockSpec((1, 1, d), lambda l, j: (l, 0, j))],
        out_specs=pl.BlockSpec((1, 8, d), lambda l, j: (l, 0, j)),
        out_shape=jax.ShapeDtypeStruct((DEPTH, 8, 3 * d), F32),
        compiler_params=_cp("arbitrary", "arbitrary"),
        name="modulation",
    )(conds, ada_w, ada_b.reshape(DEPTH, 1, 3 * d))


def _head_norm(xh, w):
    ms = jnp.sum(xh * xh, axis=-1, keepdims=True) * (1.0 / MLA_QK)
    return xh * lax.rsqrt(ms + EPS) * w


def _rope(xh, cos, sin_a, sin_b):
    return xh * cos + pltpu.roll(xh, HEAD_PAD - 8, 1) * sin_a + pltpu.roll(xh, 8, 1) * sin_b


def _mla_steps(m, has_q, rope, cache_seqs, n_aliased, refs):
    refs = list(refs)
    if has_q:
        qan_ref, wuq_ref, qn_ref, kvn_ref = refs[:4]
        refs = refs[4:]
    wuk_ref, wuv_ref, kn_ref = refs[:3]
    refs = refs[3:]
    if rope:
        cos_ref, sa_ref, sb_ref = refs[:3]
        refs = refs[3:]
        cos, sa, sb = cos_ref[...], sa_ref[...], sb_ref[...]
    refs = refs[n_aliased:]
    if has_q:
        q_ref = refs.pop(0)
    k_ref, v_ref = refs[:2]
    if cache_seqs:
        ckv_ref, kro_ref = refs[2:]
    if has_q:
        lane = lax.broadcasted_iota(jnp.int32, (1, HEAD_PAD), 1)
        mixed = m[:, 128:256]
        cq = jnp.concatenate([m[:, 0:128], jnp.where(lane < MLA_NOPE, mixed, 0.0)], axis=1)
        ms = jnp.sum(cq * cq, axis=-1, keepdims=True) * (1.0 / MLA_Q_RANK)
        cqn = cq * lax.rsqrt(ms + EPS) * qan_ref[...]
        ckv = m[:, 256:384]
        ckvn = ckv * lax.rsqrt(jnp.mean(ckv * ckv, axis=-1, keepdims=True) + EPS) * kvn_ref[...]
        yield
        q = _dot(cqn.astype(BF16), wuq_ref[...])
        kr = jnp.where(jnp.logical_and(lane >= MLA_NOPE, lane < MLA_NOPE + MLA_ROPE), mixed, 0.0)
        if cache_seqs:
            seq_len = ckv_ref.shape[2]
            for s in range(cache_seqs):
                ckv_ref[s, 0] = ckvn[s * seq_len:(s + 1) * seq_len]
                kro_ref[s, 0] = kr[s * seq_len:(s + 1) * seq_len, MLA_NOPE:MLA_NOPE + MLA_ROPE]
    else:
        ckvn = m[:, 0:128]
        kr = m[:, 128:256]
    cb = ckvn.astype(BF16)
    kup = _dot(cb, wuk_ref[...])
    v_ref[...] = _dot(cb, wuv_ref[...]).astype(BF16)
    yield
    for h in range(MLA_HEADS):
        sl = slice(HEAD_PAD * h, HEAD_PAD * (h + 1))
        kh = _head_norm(kup[:, sl] + kr, kn_ref[...])
        if rope:
            kh = _rope(kh, cos, sa, sb)
        k_ref[:, sl] = kh.astype(BF16)
        if has_q:
            qh = _head_norm(q[:, sl], qn_ref[...])
            if rope:
                qh = _rope(qh, cos, sa, sb)
            q_ref[:, sl] = (qh * (MLA_QK ** -0.5)).astype(BF16)
        if h % 2 == 1:
            yield


def _mla_prep_kernel(has_q, rope, m_ref, *refs):
    for _ in _mla_steps(m_ref[...], has_q, rope, 0, 0, refs):
        pass


def _inproj_kernel(fuse_mla, cache_seqs, n_aliased, x_ref, mod_ref, nw_ref, w_ref, *refs):
    n_mla_in = 7 + n_aliased if fuse_mla else 0
    mla_refs, (o_mla, o_g, o_hy, o_s5, o_gla) = refs[:n_mla_in], refs[n_mla_in:n_mla_in + 5]
    x = x_ref[...]
    ms = jnp.mean(x * x, axis=-1, keepdims=True)
    y = x * lax.rsqrt(ms + EPS) * nw_ref[...]
    h = (y * (1.0 + mod_ref[0, 1:2, :]) + mod_ref[0, 0:1, :]).astype(BF16)
    project = lambda o, seg: o.__setitem__(Ellipsis, _dot(h, w_ref[:, seg[0]:seg[1]]).astype(o.dtype))
    m = _dot(h, w_ref[:, SEG_MLA[0]:SEG_MLA[1]])
    o_mla[...] = m
    steps = iter(()) if not fuse_mla else _mla_steps(
        m, True, False, cache_seqs, n_aliased, list(mla_refs) + list(refs[n_mla_in + 5:]))
    project(o_g, SEG_GATE)
    next(steps, None)
    next(steps, None)
    project(o_hy, SEG_HY)
    next(steps, None)
    project(o_s5, SEG_S5)
    project(o_gla, SEG_GLA)
    for _ in steps:
        pass


def _inproj(x, mod, mod_row, wl, tm, seq_len, cache=None):
    n, d = x.shape
    fuse_mla = cache is not None
    full = lambda shape: pl.BlockSpec(shape, lambda i: (0,) * len(shape))
    row = lambda w: pl.BlockSpec((tm, w), lambda i: (i, 0))
    args = [x, mod, wl["norm_w"], wl["w_in"]]
    specs = [row(d), pl.BlockSpec((1, 3, d), lambda i: (mod_row(i * tm), 0, 0)), full((1, d)), full((d, N_PROJ))]
    widths = [hi - lo for lo, hi in (SEG_MLA, SEG_GATE, SEG_HY, SEG_S5, SEG_GLA)]
    dtypes = [F32, BF16, F32, F32, F32]
    aliases = {}
    nseq = 0
    if fuse_mla:
        args += [wl["qa_norm"], wl["w_uq"], wl["q_norm"], wl["kva_norm"], wl["w_uk"], wl["w_uv"], wl["k_norm"]]
        specs += [full((1, 256)), full((256, 512)), full((1, 128)), full((1, 128)),
                  full((128, 512)), full((128, 256)), full((1, 128))]
        widths += [512, 512, 256]
        dtypes += [BF16, BF16, BF16]
    out_specs = [row(w) for w in widths]
    out_shape = [jax.ShapeDtypeStruct((n, w), t) for w, t in zip(widths, dtypes)]
    if fuse_mla:
        layer, prev = cache
        nseq = tm // seq_len
        for w in (MLA_KV_RANK, MLA_ROPE):
            out_specs.append(pl.BlockSpec((nseq, 1, seq_len, w), lambda i: (i, layer, 0, 0)))
            out_shape.append(jax.ShapeDtypeStruct((n // seq_len, DEPTH, seq_len, w), F32))
        for k, buf in enumerate(prev):
            aliases[len(args)] = len(out_shape) - 2 + k
            args.append(buf)
            specs.append(pl.BlockSpec(memory_space=pl.ANY))
    return pl.pallas_call(
        functools.partial(_inproj_kernel, fuse_mla, nseq, len(aliases)),
        grid=(n // tm,),
        in_specs=specs, out_specs=out_specs, out_shape=out_shape,
        input_output_aliases=aliases,
        compiler_params=_cp("arbitrary"),
        name="inproj",
    )(*args)


def _mla_prep(m, wl, rope_tabs, seq_len, tm, has_q):
    n, wm = m.shape
    rope = rope_tabs is not None
    full = lambda shape: pl.BlockSpec(shape, lambda i: (0,) * len(shape))
    row = lambda w: pl.BlockSpec((tm, w), lambda i: (i, 0))
    args, specs = [m], [row(wm)]
    if has_q:
        args += [wl["qa_norm"], wl["w_uq"], wl["q_norm"], wl["kva_norm"]]
        specs += [full((1, 256)), full((256, 512)), full((1, 128)), full((1, 128))]
    args += [wl["w_uk"], wl["w_uv"], wl["k_norm"]]
    specs += [full((128, 512)), full((128, 256)), full((1, 128))]
    if rope:
        nt = seq_len // tm
        args += list(rope_tabs)
        specs += [pl.BlockSpec((tm, HEAD_PAD), lambda i: (i % nt, 0))] * 3
    widths = ([512] if has_q else []) + [512, 256]
    return pl.pallas_call(
        functools.partial(_mla_prep_kernel, has_q, rope),
        grid=(n // tm,),
        in_specs=specs, out_specs=[row(w) for w in widths],
        out_shape=[jax.ShapeDtypeStruct((n, w), BF16) for w in widths],
        compiler_params=_cp("arbitrary"),
        name="mla_prep",
    )(*args)


def _attn_kernel(nparts, nseq, q_ref, *refs):
    kv = [(refs[2 * i], refs[2 * i + 1]) for i in range(nparts)]
    o_ref = refs[2 * nparts]
    tq = q_ref.shape[0] // nseq
    low = lax.broadcasted_iota(jnp.int32, (1, HEAD_PAD), 1) < MLA_V
    units = [(s, h) for s in range(nseq) for h in range(MLA_HEADS)]

    def keys(ref, s):
        lk = ref.shape[0] // nseq
        return slice(s * lk, (s + 1) * lk)

    def scores(s, h):
        sl = slice(HEAD_PAD * h, HEAD_PAD * (h + 1))
        return [_dot_nt(q_ref[s * tq:(s + 1) * tq, sl], k_ref[keys(k_ref, s), sl]) for k_ref, _ in kv]

    s_next = scores(*units[0])
    acc = None
    for n, (s, h) in enumerate(units):
        pair, j = divmod(h, 2)
        sc = s_next
        if n + 1 < len(units):
            s_next = scores(*units[n + 1])
        if j == 0:
            v_half = []
            for _, v_ref in kv:
                vp = v_ref[keys(v_ref, s), HEAD_PAD * pair:HEAD_PAD * (pair + 1)]
                zero = jnp.zeros_like(vp)
                v_half.append((jnp.where(low, vp, zero), jnp.where(low, zero, vp)))
        m = functools.reduce(jnp.maximum, [jnp.max(x, axis=-1, keepdims=True) for x in sc])
        p = [jnp.exp(x - m) for x in sc]
        den = functools.reduce(jnp.add, [jnp.sum(x, axis=-1, keepdims=True) for x in p])
        num = functools.reduce(jnp.add, [_dot(x.astype(BF16), vh[j]) for x, vh in zip(p, v_half)])
        o = num / den
        acc = o if j == 0 else acc + o
        if j == 1:
            o_ref[s * tq:(s + 1) * tq, HEAD_PAD * pair:HEAD_PAD * (pair + 1)] = acc.astype(BF16)


def _attention(q, kv_parts, batch, lq, tq):
    nq = lq // tq
    nseq = max(1, min(batch, 1024 // lq)) if nq == 1 else 1
    args, specs = [q], [pl.BlockSpec((nseq * tq, 512), lambda b, i: (b * nq + i, 0))]
    for k, v, lk in kv_parts:
        args += [k, v]
        specs += [pl.BlockSpec((nseq * lk, 512), lambda b, i: (b, 0)),
                  pl.BlockSpec((nseq * lk, 256), lambda b, i: (b, 0))]
    return pl.pallas_call(
        functools.partial(_attn_kernel, len(kv_parts), nseq),
        grid=(batch // nseq, nq),
        in_specs=specs,
        out_specs=pl.BlockSpec((nseq * tq, 256), lambda b, i: (b * nq + i, 0)),
        out_shape=jax.ShapeDtypeStruct((batch * lq, 256), BF16),
        compiler_params=_cp("arbitrary", "arbitrary"),
        name="attention",
    )(*args)


def _hyena_kernel(seq_len, bk, nseq, x_ref, cw_ref, cb_ref, fo_ref, go_ref, k_ref, bias_ref, o_ref,
                  u_sc, y_sc, z_sc):
    c = D_BRANCH
    n = nseq * seq_len
    nblk = seq_len // bk
    pos = jnp.bitwise_and(lax.broadcasted_iota(jnp.int32, (n, 1), 0), seq_len - 1)
    first, last = pos == 0, pos == seq_len - 1

    def short_conv(g):
        cols = slice(g * c, (g + 1) * c)
        x = x_ref[:, cols]
        xm = jnp.where(first, 0.0, pltpu.roll(x, 1, 0))
        xp = jnp.where(last, 0.0, pltpu.roll(x, n - 1, 0))
        return cw_ref[0:1, cols] * xm + cw_ref[1:2, cols] * x + cw_ref[2:3, cols] * xp + cb_ref[:, cols]

    fo, go = fo_ref[...], go_ref[...]

    def long_conv(s, v, order, emit):
        cols = slice(order * c, (order + 1) * c)
        for j in range(nblk):
            u_sc[s, j] = _dot(fo, v[j * bk:(j + 1) * bk].astype(BF16))
        yield
        rc = 32
        for i in range(nblk):
            def mix(r, carry, i=i):
                top = pl.ds(pl.multiple_of(r * rc, rc), rc)
                bot = pl.ds(pl.multiple_of(bk + r * rc, rc), rc)
                at = ab = None
                for j in range(nblk):
                    q = i - j + nblk - 1
                    kt, kb = k_ref[q, top, cols], k_ref[q, bot, cols]
                    ut, ub = u_sc[s, j, top, :], u_sc[s, j, bot, :]
                    pt, pb = ut * kt - ub * kb, ut * kb + ub * kt
                    at, ab = (pt, pb) if at is None else (at + pt, ab + pb)
                z_sc[s, top, :] = at.astype(BF16)
                z_sc[s, bot, :] = ab.astype(BF16)
                return carry

            lax.fori_loop(0, bk // rc, mix, 0, unroll=True if nblk == 1 else 2)
            y = _dot(go, z_sc[s])
            yield
            emit(i, y)

    v_all, x1_all, x2_all = short_conv(0), short_conv(1), short_conv(2)

    def sequence(s):
        base = s * seq_len
        v = v_all[base:base + seq_len]

        def emit1(i, y):
            r = slice(i * bk, (i + 1) * bk)
            y_sc[s, r, :] = x1_all[base + i * bk:base + (i + 1) * bk] * (y + bias_ref[0:1, :] * v[r])

        yield from long_conv(s, v, 0, emit1)
        y1 = y_sc[s]

        def emit2(i, y):
            r = slice(i * bk, (i + 1) * bk)
            o_ref[base + i * bk:base + (i + 1) * bk, :] = (
                x2_all[base + i * bk:base + (i + 1) * bk] * (y + bias_ref[1:2, :] * y1[r])).astype(BF16)

        yield from long_conv(s, y1, 1, emit2)

    live = [sequence(s) for s in range(nseq)]
    while live:
        for g in list(live):
            if next(g, StopIteration) is StopIteration:
                live.remove(g)


def _hyena(x, wl, kspec, fo, go, batch, seq_len):
    c = D_BRANCH
    bk = fo.shape[1]
    nseq = max(1, min(batch, 1024 // seq_len))
    rows = nseq * seq_len
    nblk = seq_len // bk
    full = lambda a: pl.BlockSpec(a.shape, lambda i: (0,) * a.ndim)
    return pl.pallas_call(
        functools.partial(_hyena_kernel, seq_len, bk, nseq),
        grid=(batch // nseq,),
        in_specs=[pl.BlockSpec((rows, 3 * c), lambda i: (i, 0)),
                  full(wl["hy_conv_w"]), full(wl["hy_conv_b"]), full(fo), full(go),
                  pl.BlockSpec(kspec.shape, lambda i: (0, 0, 0), pipeline_mode=pl.Buffered(1)),
                  full(wl["hy_bias"])],
        out_specs=pl.BlockSpec((rows, c), lambda i: (i, 0)),
        out_shape=jax.ShapeDtypeStruct((batch * seq_len, c), BF16),
        scratch_shapes=[pltpu.VMEM((nseq, nblk, 2 * bk, c), F32), pltpu.VMEM((nseq, seq_len, c), F32),
                        pltpu.VMEM((nseq, 2 * bk, c), BF16)],
        compiler_params=pltpu.CompilerParams(dimension_semantics=("arbitrary",), vmem_limit_bytes=56 * 1024 * 1024),
        name="hyena",
    )(x, wl["hy_conv_w"], wl["hy_conv_b"], fo, go, kspec, wl["hy_bias"])


def _hy_mlp_kernel(feat_ref, w1_ref, b1_ref, f1_ref, w2_ref, b2_ref, f2_ref, w3_ref, win_ref,
                   kern_ref, nrm_ref):
    i = pl.program_id(0)
    tl = win_ref.shape[0]
    h = jnp.sin(f1_ref[...] * (_dot3(feat_ref[...], w1_ref[...]) + b1_ref[...]))
    h = jnp.sin(f2_ref[...] * (_dot3(h, w2_ref[...]) + b2_ref[...]))
    filt = _dot3(h, w3_ref[...])
    filt = jnp.concatenate([filt[:, 0:512], filt[:, 512:1024]], axis=0)
    win = win_ref[...]
    row0 = (lax.broadcasted_iota(jnp.int32, (tl, 1), 0) + i * tl) == 0
    filt = jnp.where(row0, 0.0, filt * jnp.concatenate([win, win], axis=1))
    kern_ref[...] = filt
    part = jnp.sum(jnp.abs(filt), axis=0, keepdims=True)

    @pl.when(i == 0)
    def _():
        nrm_ref[...] = jnp.zeros_like(nrm_ref)
    nrm_ref[...] += jnp.broadcast_to(part, nrm_ref.shape)


def _hy_mlp(feat, wl, win):
    rows = win.shape[0]
    tl = 256
    nl = rows // (2 * tl)
    full = lambda shape: pl.BlockSpec(shape, lambda i: (0,) * len(shape))
    return pl.pallas_call(
        _hy_mlp_kernel,
        grid=(rows // tl,),
        in_specs=[pl.BlockSpec((tl // 2, 128), lambda i: (i, 0)),
                  full((128, 128)), full((1, 128)), full((1, 128)),
                  full((128, 128)), full((1, 128)), full((1, 128)),
                  pl.BlockSpec((128, 1024), lambda i: (0, jnp.where(i < nl, 1, 0))),
                  pl.BlockSpec((tl, 256), lambda i: (i, 0))],
        out_specs=[pl.BlockSpec((tl, 512), lambda i: (i, 0)), full((8, 512))],
        out_shape=[jax.ShapeDtypeStruct((rows, 512), F32), jax.ShapeDtypeStruct((8, 512), F32)],
        compiler_params=_cp("arbitrary"),
        name="hy_mlp",
    )(feat, wl["hy_w1"], wl["hy_b1"], wl["hy_f1"], wl["hy_w2"], wl["hy_b2"], wl["hy_f2"], wl["hy_w3"], win)


def _hy_kspec_kernel(lo_ref, hi_ref, fo_ref, fb_ref, n_ref, o_ref):
    bk = lo_ref.shape[0]
    k = _dot(fo_ref[...], hi_ref[...].astype(BF16)) + _dot(fb_ref[...], lo_ref[...].astype(BF16))
    o_ref[0] = k * ((1.0 / bk) / n_ref[0:1, :])


def _hy_kspec(kern_lin, nrm, fo, fb):
    n2, bk = fo.shape
    nq = kern_lin.shape[0] // bk - 1
    full = lambda a: pl.BlockSpec(a.shape, lambda q: (0,) * a.ndim)
    return pl.pallas_call(
        _hy_kspec_kernel,
        grid=(nq,),
        in_specs=[pl.BlockSpec((bk, 512), lambda q: (q, 0)),
                  pl.BlockSpec((bk, 512), lambda q: (q + 1, 0)),
                  full(fo), full(fb), full(nrm)],
        out_specs=pl.BlockSpec((1, n2, 512), lambda q: (q, 0, 0)),
        out_shape=jax.ShapeDtypeStruct((nq, n2, 512), F32),
        compiler_params=_cp("arbitrary"),
        name="hy_kspec",
    )(kern_lin, kern_lin, fo, fb, nrm)


def _s5_discretise(are_ref, aim_ref, ldt_ref):
    ar = jnp.minimum(are_ref[0], -1e-4)
    ai = aim_ref[0]
    dt = jnp.exp(ldt_ref[0])
    e = jnp.exp(ar * dt)
    return ar, ai, e * jnp.cos(ai * dt), e * jnp.sin(ai * dt)


def _s5_scan_kernel(nseq, emit_y, u_ref, hin_ref, are_ref, aim_ref, ldt_ref, bre_ref, bim_ref, *rest):
    if emit_y:
        cre_ref, cim_ref, y_ref, hfin_ref, wb_sc, ab_sc, s_sc, hc_sc, perm_sc, wc_sc, permt_sc = rest
    else:
        hfin_ref, wb_sc, ab_sc, s_sc, hc_sc, perm_sc = rest
    d = pl.program_id(0)
    c = pl.program_id(1)
    n = S5_N

    @pl.when(c == 0)
    def _():
        ar, ai, abr, abi = _s5_discretise(are_ref, aim_ref, ldt_ref)
        ab_sc[0:1, :] = abr
        ab_sc[1:2, :] = abi
        den = 1.0 / (ar * ar + ai * ai)
        cr = ((abr - 1.0) * ar + abi * ai) * den
        ci = (abi * ar - (abr - 1.0) * ai) * den
        grp_rows = lax.shift_right_logical(lax.broadcasted_iota(jnp.int32, (D_BRANCH, 1), 0), 4)
        grp_cols = lax.shift_right_logical(lax.broadcasted_iota(jnp.int32, (1, n), 1), 6)
        expand_b = lambda ref: jnp.where(grp_rows == grp_cols, jnp.concatenate([ref[0]] * (n // 128), axis=1), 0.0)
        bre, bim = expand_b(bre_ref), expand_b(bim_ref)
        wb_sc[:, 0:n] = (cr * bre - ci * bim).astype(BF16)
        wb_sc[:, n:2 * n] = (cr * bim + ci * bre).astype(BF16)
        if emit_y:
            st_rows = lax.shift_right_logical(lax.broadcasted_iota(jnp.int32, (n, 1), 0), 6)
            ch_cols = lax.shift_right_logical(lax.broadcasted_iota(jnp.int32, (1, D_BRANCH), 1), 4)
            expand_c = lambda ref: jnp.where(st_rows == ch_cols, jnp.concatenate([ref[0]] * 2, axis=1), 0.0)
            wc_sc[0:n, :] = expand_c(cre_ref).astype(BF16)
            wc_sc[n:2 * n, :] = (-expand_c(cim_ref)).astype(BF16)
        hc_sc[...] = hin_ref[0]

    steps = u_ref.shape[1]
    rows_c = nseq * steps

    @pl.when(c == 0)
    def _():
        i = lax.broadcasted_iota(jnp.int32, (rows_c, rows_c), 0)
        j = lax.broadcasted_iota(jnp.int32, (rows_c, rows_c), 1)
        p = lax.shift_right_logical(i, int(math.log2(nseq)))
        step = p + d * (steps - 1 - 2 * p)
        src = jnp.bitwise_and(i, nseq - 1) * steps + step
        perm_sc[...] = jnp.where(j == src, 1.0, 0.0).astype(BF16)
        if emit_y:
            pj = lax.shift_right_logical(j, int(math.log2(nseq)))
            src_j = jnp.bitwise_and(j, nseq - 1) * steps + pj + d * (steps - 1 - 2 * pj)
            permt_sc[...] = jnp.where(i == src_j, 1.0, 0.0).astype(BF16)

    lhs = _dot(perm_sc[...], u_ref[...].reshape(rows_c, D_BRANCH).astype(BF16)).astype(BF16)
    lb = 256
    y = None
    for j in range(n // lb):
        lr = slice(lb * j, lb * (j + 1))
        li = slice(n + lb * j, n + lb * (j + 1))
        bur = _dot(lhs, wb_sc[:, lr])
        bui = _dot(lhs, wb_sc[:, li])
        abr = ab_sc[0:1, lr]
        abi = ab_sc[1:2, lr]
        hr, hi = hc_sc[:, lr], hc_sc[:, li]
        for p in range(steps):
            rows = slice(p * nseq, (p + 1) * nseq)
            hr, hi = abr * hr - abi * hi + bur[rows], abr * hi + abi * hr + bui[rows]
            if emit_y:
                s_sc[rows, lr] = hr.astype(BF16)
                s_sc[rows, li] = hi.astype(BF16)
        hc_sc[:, lr] = hr
        hc_sc[:, li] = hi
        if emit_y:
            yj = _dot(s_sc[:, lr], wc_sc[lr, :]) + _dot(s_sc[:, li], wc_sc[li, :])
            y = yj if y is None else y + yj

    if emit_y:
        y_seq = _dot(permt_sc[...], y.astype(BF16)).astype(BF16)
        y_ref[0] = y_seq.reshape(nseq, steps, D_BRANCH)

    @pl.when(c == pl.num_programs(1) - 1)
    def _():
        hfin_ref[0] = hc_sc[...]


def _s5_scan(u, hin, wl, emit_y):
    nseq, nstep, _ = u.shape
    steps = S5_ROWS // nseq
    nc = nstep // steps
    n = S5_N
    chunk = lambda d, c: c + d * (nc - 1 - 2 * c)
    per_dir = lambda shape: pl.BlockSpec((1,) + shape, lambda d, c: (d,) + (0,) * len(shape))
    args = [u, hin, wl["s5_are"], wl["s5_aim"], wl["s5_ldt"], wl["s5_bre"], wl["s5_bim"]]
    specs = [pl.BlockSpec((nseq, steps, D_BRANCH), lambda d, c: (0, chunk(d, c), 0)),
             per_dir((nseq, 2 * n)), per_dir((1, n)), per_dir((1, n)), per_dir((1, n)),
             per_dir((D_BRANCH, 128)), per_dir((D_BRANCH, 128))]
    out_specs = [per_dir((nseq, 2 * n))]
    out_shape = [jax.ShapeDtypeStruct((2, nseq, 2 * n), F32)]
    scratch = [pltpu.VMEM((D_BRANCH, 2 * n), BF16), pltpu.VMEM((8, n), F32),
               pltpu.VMEM((S5_ROWS, 2 * n), BF16), pltpu.VMEM((nseq, 2 * n), F32),
               pltpu.VMEM((S5_ROWS, S5_ROWS), BF16)]
    if emit_y:
        args += [wl["s5_cre"], wl["s5_cim"]]
        specs += [per_dir((n, 128)), per_dir((n, 128))]
        out_specs = [pl.BlockSpec((1, nseq, steps, D_BRANCH), lambda d, c: (d, 0, chunk(d, c), 0))] + out_specs
        out_shape = [jax.ShapeDtypeStruct((2, nseq, nstep, D_BRANCH), BF16)] + out_shape
        scratch += [pltpu.VMEM((2 * n, D_BRANCH), BF16), pltpu.VMEM((S5_ROWS, S5_ROWS), BF16)]
    return pl.pallas_call(
        functools.partial(_s5_scan_kernel, nseq, emit_y),
        grid=(2, nc),
        in_specs=specs, out_specs=out_specs, out_shape=out_shape, scratch_shapes=scratch,
        compiler_params=_cp("arbitrary", "arbitrary"),
        name="s5_scan" if emit_y else "s5_scan_finals",
    )(*args)


def _s5_chain_kernel(batch, nseg, f_ref, h0_ref, are_ref, aim_ref, ldt_ref, o_ref):
    d = pl.program_id(0)
    n = S5_N
    _, _, pr, pi = _s5_discretise(are_ref, aim_ref, ldt_ref)
    for _ in range(int(math.log2(S5_SEG))):
        pr, pi = pr * pr - pi * pi, 2.0 * pr * pi
    f = f_ref[0]
    fr, fi = f[:, 0:n], f[:, n:2 * n]
    h0 = h0_ref[0]
    h0r, h0i = h0[:, 0:n], h0[:, n:2 * n]
    nrow = batch * nseg
    seg = jnp.bitwise_and(lax.broadcasted_iota(jnp.int32, (nrow, 1), 0), nseg - 1)

    def run(shift, keep):
        xr, xi = h0r, h0i
        for _ in range(nseg - 1):
            zr = fr + pr * xr - pi * xi
            zi = fi + pr * xi + pi * xr
            xr = h0r + jnp.where(keep, pltpu.roll(zr, shift, 0), 0.0)
            xi = h0i + jnp.where(keep, pltpu.roll(zi, shift, 0), 0.0)
        o_ref[0, :, 0:n] = xr
        o_ref[0, :, n:2 * n] = xi

    @pl.when(d == 0)
    def _():
        run(1, seg != 0)

    @pl.when(d == 1)
    def _():
        run(nrow - 1, seg != nseg - 1)


def _s5_chain(fin, h0rows, wl, batch, nseg):
    nrow = batch * nseg
    n = S5_N
    per_dir = lambda shape: pl.BlockSpec((1,) + shape, lambda d: (d,) + (0,) * len(shape))
    return pl.pallas_call(
        functools.partial(_s5_chain_kernel, batch, nseg),
        grid=(2,),
        in_specs=[per_dir((nrow, 2 * n)), per_dir((nrow, 2 * n)), per_dir((1, n)), per_dir((1, n)), per_dir((1, n))],
        out_specs=per_dir((nrow, 2 * n)),
        out_shape=jax.ShapeDtypeStruct((2, nrow, 2 * n), F32),
        compiler_params=_cp("arbitrary"),
        name="s5_chain",
    )(fin, h0rows, wl["s5_are"], wl["s5_aim"], wl["s5_ldt"])


def _gla_kernel(seq_len, nb, n_aliased, q_ref, k_ref, v_ref, g_ref, gw_ref, gb_ref, s0_ref, *rest):
    o_ref, sfin_ref, qe_sc, upd_sc, dec_sc, sall_sc, lhs_sc, kt_sc, la_sc, oi_sc = rest[n_aliased:]
    d = pl.program_id(1)
    sign = 1 - 2 * d
    ck, sup = GLA_CHUNK, GLA_SUPER
    cps = sup // ck
    nsup, nchunk = seq_len // sup, seq_len // ck
    dk, dv = GLA_HEADS * GLA_DK, GLA_HEADS * GLA_DV
    r = lax.broadcasted_iota(jnp.int32, (sup, sup), 0)
    s = lax.broadcasted_iota(jnp.int32, (sup, sup), 1)
    same = lax.shift_right_logical(r, 6) == lax.shift_right_logical(s, 6)
    tri = jnp.logical_and(same, (s - r) * sign <= 0)
    cum_lhs = jnp.where(tri, 1.0, 0.0).astype(BF16)
    t4 = lax.broadcasted_iota(jnp.int32, (ck, GLA_HEADS * ck), 0)
    s4 = jnp.bitwise_and(lax.broadcasted_iota(jnp.int32, (ck, GLA_HEADS * ck), 1), ck - 1)
    tri4 = (s4 - t4) * sign <= 0
    pos = jnp.bitwise_and(lax.broadcasted_iota(jnp.int32, (ck, 1), 0), ck - 1)
    is_last = pos == (ck - 1) * (1 - d)
    row_chunk = lax.shift_right_logical(lax.broadcasted_iota(jnp.int32, (sup, 1), 0), 6)
    head_k = lax.shift_right_logical(lax.broadcasted_iota(jnp.int32, (1, dk), 1), 5)
    head_v = lax.shift_right_logical(lax.broadcasted_iota(jnp.int32, (1, dv), 1), 6)
    blockdiag = lax.shift_right_logical(lax.broadcasted_iota(jnp.int32, (dv, 1), 0), 6) == head_k

    def group_rows(u):
        return pl.ds(u * sup, sup) if isinstance(u, int) else pl.ds(pl.multiple_of(u * sup, sup), sup)

    def stage_a(u, slot):
        rows = group_rows(u)
        q = q_ref[rows, :] * (GLA_DK ** -0.5)
        k = k_ref[rows, :]
        v = v_ref[rows, :]
        cs = _dot(cum_lhs, la_sc[rows, :])
        yield
        bc = cs[:, 0:dk] + cs[:, dk:2 * dk]
        tots = [jnp.sum(jnp.where(is_last, bc[c * ck:(c + 1) * ck], 0.0), axis=0, keepdims=True)
                for c in range(cps)]
        tot = jnp.concatenate([jnp.broadcast_to(t, (ck, dk)) for t in tots], axis=0)
        ref = 0.5 * tot
        kt_sc[slot] = (k * jnp.exp(ref - bc)).astype(BF16)
        lhs_sc[slot] = (q * jnp.exp(bc - ref)).astype(BF16)
        qe_sc[rows, :] = (q * jnp.exp(bc)).astype(BF16)
        kl = (k * jnp.exp(tot - bc)).astype(BF16)
        zero = jnp.zeros_like(kl)
        klx = jnp.concatenate([jnp.where(row_chunk == c, kl, zero) for c in range(cps)], axis=1)
        upd = _dot_tn(v.astype(BF16), klx)
        yield
        for c in range(cps):
            upd_sc[u * cps + c] = jnp.where(blockdiag, upd[:, c * dk:(c + 1) * dk], 0.0)
            dec_sc[u * cps + c] = jnp.broadcast_to(jnp.exp(tots[c]), (8, dk))

    def stage_b(u, slot):
        rows = group_rows(u)
        v = v_ref[rows, :].astype(BF16)
        qt, kt = lhs_sc[slot], kt_sc[slot]
        zk, zv = jnp.zeros_like(kt[0:ck]), jnp.zeros_like(v[0:ck])
        p = [_dot_nt(qt[c * ck:(c + 1) * ck],
                     jnp.concatenate([jnp.where(head_k == h, kt[c * ck:(c + 1) * ck], zk)
                                      for h in range(GLA_HEADS)], axis=0)) for c in range(cps)]
        yield
        o = [_dot(jnp.where(tri4, p[c], 0.0).astype(BF16),
                  jnp.concatenate([jnp.where(head_v == h, v[c * ck:(c + 1) * ck], zv)
                                   for h in range(GLA_HEADS)], axis=0)) for c in range(cps)]
        yield
        oi_sc[rows, :] = jnp.concatenate(o, axis=0)

    def run(*stages):
        live = list(stages)
        while live:
            for g in list(live):
                if next(g, StopIteration) is StopIteration:
                    live.remove(g)

    x = _dot(g_ref[...].astype(BF16), gw_ref[0]) + gb_ref[0]
    la = (jnp.minimum(x, 0.0) - jnp.log(1.0 + jnp.exp(-jnp.abs(x)))) * (1.0 / GLA_TAU)
    la_sc[...] = jnp.concatenate(_split2(la), axis=1)

    nu = nb * nsup
    if nu <= 4:
        run(*[stage_a(u, u) for u in range(min(2, nu))])
        for u in range(0, nu, 2):
            run(*([stage_b(v, v % 4) for v in range(u, min(u + 2, nu))]
                  + [stage_a(v, v % 4) for v in range(u + 2, min(u + 4, nu))]))
    else:
        run(stage_a(0, 0), stage_a(1, 1))

        def sup_body(t, carry):
            u = 2 * t
            n0, n1 = jnp.minimum(u + 2, nu - 2), jnp.minimum(u + 3, nu - 1)
            run(stage_b(u, jnp.bitwise_and(u, 3)), stage_b(u + 1, jnp.bitwise_and(u + 1, 3)),
                stage_a(n0, jnp.bitwise_and(u + 2, 3)), stage_a(n1, jnp.bitwise_and(u + 3, 3)))
            return carry

        lax.fori_loop(0, nu // 2, sup_body, 0)

    for j in range(nb):
        def state_body(c, st, j=j):
            ci = j * nchunk + c + d * (nchunk - 1 - 2 * c)
            sall_sc[ci] = st.astype(BF16)
            return dec_sc[ci][0:1, :] * st + upd_sc[ci]

        st_fin = jnp.transpose(lax.fori_loop(0, nchunk, state_body, s0_ref[j, 0]))
        for h in range(GLA_HEADS):
            sfin_ref[j, 0, 0, h] = st_fin[h * GLA_DK:(h + 1) * GLA_DK, h * GLA_DV:(h + 1) * GLA_DV]

    def inter(u):
        rows = group_rows(u)
        qe = qe_sc[rows, :]
        oi = jnp.concatenate([_dot_nt(qe[c * ck:(c + 1) * ck], sall_sc[u * cps + c]) for c in range(cps)], axis=0)
        o_ref[0, rows, :] = (oi_sc[rows, :] + oi).astype(BF16)

    if nu <= 4:
        for u in range(nu):
            inter(u)
    else:
        lax.fori_loop(0, nu, lambda u, carry: (inter(u), carry)[1], 0, unroll=2 if nu % 2 == 0 else 1)


def _gla(gla_in, mla_in, wl, s0t, batch, seq_len, fin=(0, None, 1)):
    layer, prev_fin, fin_layers = fin
    aliases = {} if prev_fin is None else {7: 1}
    extra = [] if prev_fin is None else [prev_fin]
    n = gla_in.shape[0]
    dk, dv = GLA_HEADS * GLA_DK, GLA_HEADS * GLA_DV
    nb = max(1, min(batch, 1024 // seq_len))
    rows = nb * seq_len
    nchunk = nb * (seq_len // GLA_CHUNK)
    return pl.pallas_call(
        functools.partial(_gla_kernel, seq_len, nb, len(extra)),
        grid=(batch // nb, 2),
        input_output_aliases=aliases,
        in_specs=[pl.BlockSpec((rows, dk), lambda b, d: (b, 0)),
                  pl.BlockSpec((rows, dk), lambda b, d: (b, 1)),
                  pl.BlockSpec((rows, dv), lambda b, d: (b, 1)),
                  pl.BlockSpec((rows, dk), lambda b, d: (b, 1)),
                  pl.BlockSpec((1, dk, dk), lambda b, d: (d, 0, 0)),
                  pl.BlockSpec((1, 1, dk), lambda b, d: (d, 0, 0)),
                  pl.BlockSpec((nb, 1, dv, dk), lambda b, d: (b, d, 0, 0))]
                 + [pl.BlockSpec(memory_space=pl.ANY)] * len(extra),
        out_specs=[pl.BlockSpec((1, rows, dv), lambda b, d: (d, b, 0)),
                   pl.BlockSpec((nb, 1, 1, GLA_HEADS, GLA_DK, GLA_DV), lambda b, d: (b, layer, d, 0, 0, 0))],
        out_shape=[jax.ShapeDtypeStruct((2, n, dv), BF16),
                   jax.ShapeDtypeStruct((batch, fin_layers, 2, GLA_HEADS, GLA_DK, GLA_DV), F32)],
        scratch_shapes=[pltpu.VMEM((rows, dk), BF16),
                        pltpu.VMEM((nchunk, dv, dk), F32),
                        pltpu.VMEM((nchunk, 8, dk), F32),
                        pltpu.VMEM((nchunk, dv, dk), BF16),
                        pltpu.VMEM((4, GLA_SUPER, dk), BF16),
                        pltpu.VMEM((4, GLA_SUPER, dk), BF16),
                        pltpu.VMEM((rows, 2 * dk), BF16),
                        pltpu.VMEM((rows, dv), F32)],
        compiler_params=_cp("arbitrary", "arbitrary"),
        name="gla",
    )(gla_in, gla_in, gla_in, mla_in, wl["gla_gw"], wl["gla_gb"], s0t, *extra)


def _outproj_kernel(x_ref, mod_ref, g_ref, om_ref, oh_ref, su_ref, sf_ref, sb_ref, sd_ref, sw_ref, sbias_ref,
                    gf_ref, gb_ref, gn_ref, hm_ref, w_ref, y_ref):
    c = D_BRANCH
    g = g_ref[...].astype(F32)
    acc = _dot((om_ref[...].astype(F32) * _silu(g[:, 0:c])).astype(BF16), w_ref[0:c, :])
    acc += _dot((oh_ref[...].astype(F32) * _silu(g[:, c:2 * c])).astype(BF16), w_ref[c:2 * c, :])
    ys = sd_ref[...] * su_ref[...] + sf_ref[0].astype(F32) + sb_ref[0].astype(F32)
    ge = 0.5 * ys * (1.0 + jnp.tanh(math.sqrt(2.0 / math.pi) * (ys + 0.044715 * (ys * ys * ys))))
    o_s5 = ge / (1.0 + jnp.exp(-(_dot(ge.astype(BF16), sw_ref[...]) + sbias_ref[...])))
    acc += _dot((o_s5 * _silu(g[:, 2 * c:3 * c])).astype(BF16), w_ref[2 * c:3 * c, :])
    og = gf_ref[0].astype(F32) + gb_ref[0].astype(F32)
    hi, lo = _split2(og * og)
    ms = _dot(hi, hm_ref[...]) + _dot(lo, hm_ref[...])
    ogn = og * lax.rsqrt(ms + EPS) * gn_ref[...]
    acc += _dot((ogn * _silu(g[:, 3 * c:4 * c])).astype(BF16), w_ref[3 * c:4 * c, :])
    y_ref[...] = x_ref[...] + mod_ref[0, 2:3, :] * acc


def _outproj(x, mod, mod_row, gates, o_mla, o_hy, s5_u, s5_y, o_gla, wl, tm):
    n, d = x.shape
    c = D_BRANCH
    row = lambda w: pl.BlockSpec((tm, w), lambda i: (i, 0))
    per_dir = lambda k: pl.BlockSpec((1, tm, c), lambda i: (k, i, 0))
    full = lambda *shape: pl.BlockSpec(shape, lambda i: (0,) * len(shape))
    return pl.pallas_call(
        _outproj_kernel,
        grid=(n // tm,),
        in_specs=[row(d),
                  pl.BlockSpec((1, 3, d), lambda i: (mod_row(i * tm), 0, 0)),
                  row(d), row(c), row(c),
                  row(c), per_dir(0), per_dir(1), full(1, c), full(c, c), full(1, c),
                  per_dir(0), per_dir(1), full(1, c), full(c, c), full(d, d)],
        out_specs=row(d),
        out_shape=jax.ShapeDtypeStruct((n, d), F32),
        compiler_params=_cp("arbitrary"),
        name="outproj",
    )(x, mod, gates, o_mla, o_hy, s5_u, s5_y, s5_y, wl["s5_d"], wl["s5_glu_w"], wl["s5_glu_b"],
      o_gla, o_gla, wl["gla_norm"], wl["head_mean"], wl["w_out"])


def _rope_tables(seq_len):
    pos = np.arange(seq_len)
    inv = ROPE_BASE ** (-np.arange(0, 16, 2, dtype=np.float64) / 16.0)
    cos = np.ones((seq_len, HEAD_PAD))
    sin_a = np.zeros((seq_len, HEAD_PAD))
    sin_b = np.zeros((seq_len, HEAD_PAD))
    for base, p in ((MLA_NOPE, pos // GRID_W), (MLA_NOPE + 16, pos % GRID_W)):
        ang = p[:, None].astype(np.float64) * inv[None, :]
        cos[:, base:base + 8] = np.cos(ang)
        cos[:, base + 8:base + 16] = np.cos(ang)
        sin_a[:, base:base + 8] = -np.sin(ang)
        sin_b[:, base + 8:base + 16] = np.sin(ang)
    return tuple(jnp.asarray(t, F32) for t in (cos, sin_a, sin_b))


def _odd_dft(seq_len):
    bk = min(seq_len, HY_BLOCK)
    k = np.arange(bk)[:, None]
    t = np.arange(bk)[None, :]

    def mat(shift):
        ang = (np.pi / (2 * bk)) * (((2 * k + 1) * (t + shift)) % (4 * bk))
        return np.concatenate([np.cos(ang), -np.sin(ang)], axis=0)

    fo = mat(0)
    fb = -mat(bk)
    fb[:, 0] = 0.0
    const = lambda a: jnp.asarray(a, F32).astype(BF16)
    return const(fo), const(fb), const(fo.T)


def _hyena_tables(seq_len):
    lag = np.arange(-seq_len, seq_len)
    pos = np.where(lag == -seq_len, 0, np.abs(lag)).astype(np.float64)
    t = pos / seq_len
    w = 2.0 * np.pi * pos / seq_len
    bands = np.linspace(1e-4, HY_BANDS - 1, HY_BANDS)
    feat = np.zeros((2 * seq_len, HY_HIDDEN))
    feat[:, 0] = t
    feat[:, 1:1 + HY_BANDS] = np.cos(w[:, None] * bands)
    feat[:, 1 + HY_BANDS:HY_FEAT] = np.sin(w[:, None] * bands)
    feat = feat.reshape(-1, 2, 128, HY_HIDDEN).transpose(0, 2, 1, 3).reshape(seq_len, 2 * HY_HIDDEN)
    deltas = np.linspace(math.log(1.0 / HY_TARGET) / HY_FAST_DECAY, math.log(1.0 / HY_TARGET) / HY_SLOW_DECAY,
                         D_BRANCH)
    win = np.exp(-t[:, None] * deltas[None, :]) + HY_SHIFT
    return jnp.asarray(feat, F32), jnp.asarray(win, F32)


def _pad_to(a, shape):
    return jnp.pad(a, [(0, s - d) for s, d in zip(shape, a.shape)])


def _layer_weights(l, p):
    z = lambda *s: jnp.zeros(s, F32)
    w_in = p["w_in"][l]
    col = lambda lo, hi: w_in[:, lo:hi]
    d = D_MODEL
    w_p = jnp.concatenate([
        col(0, 192), col(320, 352), col(2656, 2688), col(192, 320),
        col(352, 608), col(1376, 1632), col(1888, 2144), col(2688, 2944),
        col(608, 1376), col(1632, 1888),
        col(2144, 2272), col(2272, 2400), col(2400, 2656)], axis=1).astype(BF16)
    wl = {"w_in": w_p, "norm_w": p["norm_w"][l].reshape(1, d), "w_out": p["w_out"][l].astype(BF16)}
    wl["qa_norm"] = _pad_to(p["mla_qa_norm"][l].reshape(1, -1), (1, 256))
    w_uq = _pad_to(p["mla_w_uq"][l].reshape(MLA_Q_RANK, MLA_HEADS, MLA_QK), (256, MLA_HEADS, HEAD_PAD))
    wl["w_uq"] = w_uq.reshape(256, MLA_HEADS * HEAD_PAD).astype(BF16)
    wl["q_norm"] = _pad_to(p["mla_q_norm"][l].reshape(1, -1), (1, HEAD_PAD))
    wl["k_norm"] = _pad_to(p["mla_k_norm"][l].reshape(1, -1), (1, HEAD_PAD))
    wl["kva_norm"] = p["mla_kva_norm"][l].reshape(1, -1)
    w_ukv = p["mla_w_ukv"][l].reshape(MLA_KV_RANK, MLA_HEADS, MLA_NOPE + MLA_V)
    wl["w_uk"] = _pad_to(w_ukv[:, :, :MLA_NOPE], (MLA_KV_RANK, MLA_HEADS, HEAD_PAD)).reshape(MLA_KV_RANK, -1).astype(BF16)
    wl["w_uv"] = w_ukv[:, :, MLA_NOPE:].reshape(MLA_KV_RANK, MLA_HEADS * MLA_V).astype(BF16)
    wl["hy_conv_w"] = p["hy_conv_w"][l]
    wl["hy_conv_b"] = p["hy_conv_b"][l].reshape(1, -1)
    hh = HY_HIDDEN
    twice = lambda a: jnp.tile(a.reshape(1, -1), (1, 2))
    bdiag = lambda a: jnp.concatenate([jnp.pad(a, ((0, 0), (0, a.shape[1]))), jnp.pad(a, ((0, 0), (a.shape[1], 0)))], axis=0)
    wl["hy_w1"] = bdiag(_pad_to(p["hy_w1"][l], (hh, hh)))
    wl["hy_b1"], wl["hy_f1"] = twice(p["hy_b1"][l]), twice(p["hy_freq1"][l])
    wl["hy_w2"] = bdiag(p["hy_w2"][l])
    wl["hy_b2"], wl["hy_f2"] = twice(p["hy_b2"][l]), twice(p["hy_freq2"][l])
    w3 = p["hy_w3"][l]
    wl["hy_w3"] = jnp.concatenate([bdiag(w3[:, 0:512]), bdiag(w3[:, 512:1024])], axis=1)
    wl["hy_bias"] = p["hy_bias"][l]
    flat = lambda a: a[l].reshape(2, 1, S5_N)
    wl["s5_are"], wl["s5_aim"] = flat(p["s5_a_re"]), flat(p["s5_a_im"])
    wl["s5_ldt"] = jnp.repeat(p["s5_log_dt"][l], S5_STATE, axis=-1).reshape(2, 1, S5_N)
    lanes_b = lambda a: jnp.tile(a[l].transpose(0, 1, 3, 2).reshape(2, D_BRANCH, S5_STATE), (1, 1, 128 // S5_STATE))
    lanes_c = lambda a: jnp.tile(a[l].transpose(0, 1, 3, 2).reshape(2, S5_N, S5_GROUP), (1, 1, 128 // S5_GROUP))
    wl["s5_bre"], wl["s5_bim"] = lanes_b(p["s5_b_re"]), lanes_b(p["s5_b_im"])
    wl["s5_cre"], wl["s5_cim"] = lanes_c(p["s5_c_re"]), lanes_c(p["s5_c_im"])
    wl["s5_d"] = p["s5_d"][l].reshape(1, -1)
    wl["s5_glu_w"] = p["s5_glu_w"][l].astype(BF16)
    wl["s5_glu_b"] = p["s5_glu_b"][l].reshape(1, -1)
    gw = p["gla_gw"][l]
    dk = GLA_HEADS * GLA_DK
    wl["gla_gw"] = jnp.stack([_pad_to(jnp.pad(gw[i], ((GLA_G_LANE + GLA_RANK * i, 0), (0, 0))), (dk, dk))
                              for i in range(2)]).astype(BF16)
    wl["gla_gb"] = p["gla_gb"][l].reshape(2, 1, dk)
    wl["gla_norm"] = jnp.tile(p["gla_norm"][l], GLA_HEADS).reshape(1, -1)
    head = np.arange(D_BRANCH) // GLA_DV
    wl["head_mean"] = jnp.asarray((head[:, None] == head[None, :]) / GLA_DV, BF16)
    return wl


def _hyena_filters(wl, tabs):
    feat, win, fo, fb, _ = tabs
    kern_lin, nrm = _hy_mlp(feat, wl, win)
    return _hy_kspec(kern_lin, nrm, fo, fb)


def _trunk_layer(x, mod, mod_row, wl, batch, seq_len, hy_tabs, rope_tabs=None, ctx=None, layer=0, cache_bufs=None):
    n = batch * seq_len
    tm = 512
    if ctx is None:
        mla_in, gates, hy_in, s5_in, gla_in, q, k, v, ckv, krope = _inproj(
            x, mod, mod_row, wl, tm, seq_len, (layer, cache_bufs[:2]))
        kv_parts = [(k, v, seq_len)]
    else:
        mla_in, gates, hy_in, s5_in, gla_in = _inproj(x, mod, mod_row, wl, tm, seq_len)
        q, k, v = _mla_prep(mla_in, wl, rope_tabs, seq_len, tm, True)
        ckv = krope = None
        k_ctx, v_ctx = _mla_prep(ctx["mla"], wl, None, ctx["past"], 512, False)
        kv_parts = [(k_ctx, v_ctx, ctx["past"]), (k, v, seq_len)]
    o_mla = _attention(q, kv_parts, batch, seq_len, 256)

    o_hy = _hyena(hy_in, wl, _hyena_filters(wl, hy_tabs), hy_tabs[2], hy_tabs[4], batch, seq_len)

    nseg = seq_len // S5_SEG
    nseq = batch * nseg
    u_seg = s5_in.reshape(nseq, S5_SEG, D_BRANCH)
    if ctx is None:
        hin = jnp.zeros((2, nseq, 2 * S5_N), F32)
    else:
        (fin,) = _s5_scan(u_seg, jnp.zeros((2, nseq, 2 * S5_N), F32), wl, False)
        hin = _s5_chain(fin, ctx["s5_h0"], wl, batch, nseg)
    y2, s5_fin = _s5_scan(u_seg, hin, wl, True)

    s0 = jnp.zeros((batch, 2, GLA_HEADS * GLA_DV, GLA_HEADS * GLA_DK), F32) if ctx is None else ctx["gla_s0"]
    fin = (0, None, 1) if ctx is not None else (layer, None if cache_bufs is None else cache_bufs[2], DEPTH)
    o_gla, gla_fin = _gla(gla_in, mla_in, wl, s0, batch, seq_len, fin)

    y = _outproj(x, mod, mod_row, gates, o_mla, o_hy, s5_in, y2.reshape(2, n, D_BRANCH), o_gla, wl, 2 * tm)
    return y, (ckv, krope, s5_fin, gla_fin)


def kernel(x_prompt, x_sample, c, cache_mla_ckv, cache_mla_krope, state_s5, state_gla, c_ctx, norm_w, ada_w, ada_b, w_in, w_out, mla_qa_norm, mla_kva_norm, mla_w_uq, mla_w_ukv, mla_q_norm, mla_k_norm, hy_conv_w, hy_conv_b, hy_w1, hy_b1, hy_freq1, hy_w2, hy_b2, hy_freq2, hy_w3, hy_bias, s5_a_re, s5_a_im, s5_log_dt, s5_b_re, s5_b_im, s5_c_re, s5_c_im, s5_d, s5_glu_w, s5_glu_b, gla_gw, gla_gb, gla_norm):
    params = dict(norm_w=norm_w, w_in=w_in, w_out=w_out, mla_qa_norm=mla_qa_norm, mla_kva_norm=mla_kva_norm,
                  mla_w_uq=mla_w_uq, mla_w_ukv=mla_w_ukv, mla_q_norm=mla_q_norm, mla_k_norm=mla_k_norm,
                  hy_conv_w=hy_conv_w, hy_conv_b=hy_conv_b, hy_w1=hy_w1, hy_b1=hy_b1, hy_freq1=hy_freq1,
                  hy_w2=hy_w2, hy_b2=hy_b2, hy_freq2=hy_freq2, hy_w3=hy_w3, hy_bias=hy_bias,
                  s5_a_re=s5_a_re, s5_a_im=s5_a_im, s5_log_dt=s5_log_dt, s5_b_re=s5_b_re, s5_b_im=s5_b_im,
                  s5_c_re=s5_c_re, s5_c_im=s5_c_im, s5_d=s5_d, s5_glu_w=s5_glu_w, s5_glu_b=s5_glu_b,
                  gla_gw=gla_gw, gla_gb=gla_gb, gla_norm=gla_norm)
    bp, lp, d = x_prompt.shape
    bs, ls, _ = x_sample.shape
    past = cache_mla_ckv.shape[2]
    n_s5 = S5_N

    conds = jnp.concatenate([c_ctx[None, :], c, jnp.zeros((8 - 1 - bs, d), F32)], axis=0)
    mods = _modulation(conds, ada_w, ada_b).reshape(DEPTH, 8, 3, d)

    tabs_p = _hyena_tables(lp) + _odd_dft(lp)
    tabs_s = _hyena_tables(ls) + _odd_dft(ls)
    rope_tabs = _rope_tables(ls)
    nseg = ls // S5_SEG

    y_p = x_prompt.reshape(bp * lp, d)
    y_s = x_sample.reshape(bs * ls, d)
    s5_l = []
    cache_bufs = (jnp.zeros((bp, DEPTH, lp, MLA_KV_RANK), F32), jnp.zeros((bp, DEPTH, lp, MLA_ROPE), F32),
                  jnp.zeros((bp, DEPTH, 2, GLA_HEADS, GLA_DK, GLA_DV), F32))
    for l in range(DEPTH):
        wl = _layer_weights(l, params)
        y_p, (ckv, krope, s5_fin, gla_fin) = _trunk_layer(y_p, mods[l], lambda row: 0, wl, bp, lp, tabs_p,
                                                          layer=l, cache_bufs=cache_bufs)
        cache_bufs = (ckv, krope, gla_fin)
        s5_l.append(jnp.stack([s5_fin[:, :, :n_s5], s5_fin[:, :, n_s5:]], axis=-1)
                    .reshape(2, bp, S5_GROUPS, S5_STATE, 2).transpose(1, 0, 2, 3, 4))

        mla_ctx = jnp.concatenate([cache_mla_ckv[:, l], jnp.zeros((bs, past, 64), F32), cache_mla_krope[:, l],
                                   jnp.zeros((bs, past, 32), F32)], axis=-1).reshape(bs * past, 256)
        st = state_s5[:, l]
        h0 = jnp.concatenate([st[..., 0].reshape(bs, 2, n_s5), st[..., 1].reshape(bs, 2, n_s5)], axis=-1)
        h0 = h0.transpose(1, 0, 2)
        h0rows = jnp.zeros((2, nseg * bs, 2 * n_s5), F32)
        h0rows = h0rows.at[0, 0::nseg].set(h0[0]).at[1, nseg - 1::nseg].set(h0[1])
        eye_h = jnp.eye(GLA_HEADS, dtype=F32)
        gla_s0 = jnp.einsum("bdhke,hg->bdhegk", state_gla[:, l], eye_h).reshape(
            bs, 2, GLA_HEADS * GLA_DV, GLA_HEADS * GLA_DK)
        ctx = {"mla": mla_ctx, "past": past, "s5_h0": h0rows, "gla_s0": gla_s0}
        y_s, _ = _trunk_layer(y_s, mods[l], lambda row: 1 + row // ls, wl, bs, ls, tabs_s, rope_tabs, ctx)

    return (y_p.reshape(bp, lp, d), y_s.reshape(bs, ls, d),
            cache_bufs[0], cache_bufs[1], jnp.stack(s5_l, axis=1), cache_bufs[2])
```

```python
import functools
import math

import numpy as np
import jax
import jax.numpy as jnp
from jax import lax
from jax.experimental import pallas as pl
from jax.experimental.pallas import tpu as pltpu

F32 = jnp.float32
BF16 = jnp.bfloat16

D_MODEL = 1024
DEPTH = 2
GRID_W = 64
D_BRANCH = 256
EPS = 1e-6

MLA_HEADS = 4
MLA_Q_RANK = 192
MLA_KV_RANK = 128
MLA_NOPE = 64
MLA_ROPE = 32
MLA_QK = 96
MLA_V = 64
ROPE_BASE = 10000.0
HEAD_PAD = 128

HY_BANDS = 16
HY_FEAT = 33
HY_HIDDEN = 64
HY_SHIFT = 0.05
HY_FAST_DECAY = 0.3
HY_SLOW_DECAY = 1.5
HY_TARGET = 1e-2
HY_BLOCK = 512

S5_GROUP = 16
S5_GROUPS = 16
S5_STATE = 64
S5_N = S5_GROUPS * S5_STATE
S5_ROWS = 512
S5_SEG = 256

GLA_HEADS = 4
GLA_DK = 32
GLA_DV = 64
GLA_RANK = 16
GLA_TAU = 16.0
GLA_CHUNK = 64
GLA_SUPER = 256

SEG_MLA = (0, 384)
SEG_GATE = (384, 1408)
SEG_HY = (1408, 2176)
SEG_S5 = (2176, 2432)
SEG_GLA = (2432, 2944)
N_PROJ = 2944
W_IN_ORDER = ((0, 192), (320, 352), (2656, 2688), (192, 320),
              (352, 608), (1376, 1632), (1888, 2144), (2688, 2944),
              (608, 1376), (1632, 1888), (2144, 2656))
GLA_G_LANE = 96

VMEM_LIMIT = 48 * 1024 * 1024


def _cp(*sem):
    return pltpu.CompilerParams(dimension_semantics=sem, vmem_limit_bytes=VMEM_LIMIT)


def _dot(a, b):
    return jnp.dot(a, b, preferred_element_type=F32)


def _dot_nt(a, b):
    return lax.dot_general(a, b, (((1,), (1,)), ((), ())), preferred_element_type=F32)


def _dot_tn(a, b):
    return lax.dot_general(a, b, (((0,), (0,)), ((), ())), preferred_element_type=F32)


def _split2(x):
    hi = x.astype(BF16)
    lo = (x - hi.astype(F32)).astype(BF16)
    return hi, lo


def _split3(x):
    h1 = x.astype(BF16)
    r1 = x - h1.astype(F32)
    h2 = r1.astype(BF16)
    h3 = (r1 - h2.astype(F32)).astype(BF16)
    return h1, h2, h3


def _dot3(a, b):
    a1, a2 = _split2(a)
    b1, b2 = _split2(b)
    return _dot(a1, b1) + (_dot(a1, b2) + _dot(a2, b1))


def _silu(z):
    return z / (1.0 + jnp.exp(-z))


def _mod_kernel(c_ref, w_ref, b_ref, o_ref):
    s = _silu(c_ref[...])
    o_ref[0] = _dot(s.astype(BF16), w_ref[0].astype(BF16)) + b_ref[0]


def _modulation(conds, ada_w, ada_b):
    d = D_MODEL
    return pl.pallas_call(
        _mod_kernel,
        grid=(DEPTH, 3),
        in_specs=[pl.BlockSpec((8, d), lambda l, j: (0, 0)),
                  pl.BlockSpec((1, d, d), lambda l, j: (l, 0, j)),
                  pl.BlockSpec((1, 1, d), lambda l, j: (l, 0, j))],
        out_specs=pl.BlockSpec((1, 8, d), lambda l, j: (l, 0, j)),
        out_shape=jax.ShapeDtypeStruct((DEPTH, 8, 3 * d), F32),
        compiler_params=_cp("arbitrary", "arbitrary"),
        name="modulation",
    )(conds, ada_w, ada_b.reshape(DEPTH, 1, 3 * d))


def _head_norm(xh, w):
    ms = jnp.sum(xh * xh, axis=-1, keepdims=True) * (1.0 / MLA_QK)
    return xh * lax.rsqrt(ms + EPS) * w


def _rope(xh, cos, sin_a, sin_b):
    return xh * cos + pltpu.roll(xh, HEAD_PAD - 8, 1) * sin_a + pltpu.roll(xh, 8, 1) * sin_b


def _mla_steps(m, has_q, rope, cache_seqs, n_aliased, refs):
    refs = list(refs)
    if has_q:
        qan_ref, wuq_ref, qn_ref, kvn_ref = refs[:4]
        refs = refs[4:]
    wuk_ref, wuv_ref, kn_ref = refs[:3]
    refs = refs[3:]
    if rope:
        cos_ref, sa_ref, sb_ref = refs[:3]
        refs = refs[3:]
        cos, sa, sb = cos_ref[...], sa_ref[...], sb_ref[...]
    refs = refs[n_aliased:]
    if has_q:
        q_ref = refs.pop(0)
    k_ref, v_ref = refs[:2]
    if cache_seqs:
        ckv_ref, kro_ref = refs[2:]
    if has_q:
        lane = lax.broadcasted_iota(jnp.int32, (1, HEAD_PAD), 1)
        mixed = m[:, 128:256]
        cq = jnp.concatenate([m[:, 0:128], jnp.where(lane < MLA_NOPE, mixed, 0.0)], axis=1)
        ms = jnp.sum(cq * cq, axis=-1, keepdims=True) * (1.0 / MLA_Q_RANK)
        cqn = cq * lax.rsqrt(ms + EPS) * qan_ref[...]
        ckv = m[:, 256:384]
        ckvn = ckv * lax.rsqrt(jnp.mean(ckv * ckv, axis=-1, keepdims=True) + EPS) * kvn_ref[...]
        yield
        q = _dot(cqn.astype(BF16), wuq_ref[...])
        kr = jnp.where(jnp.logical_and(lane >= MLA_NOPE, lane < MLA_NOPE + MLA_ROPE), mixed, 0.0)
        if cache_seqs:
            seq_len = ckv_ref.shape[2]
            for s in range(cache_seqs):
                ckv_ref[s, 0] = ckvn[s * seq_len:(s + 1) * seq_len]
                kro_ref[s, 0] = kr[s * seq_len:(s + 1) * seq_len, MLA_NOPE:MLA_NOPE + MLA_ROPE]
    else:
        ckvn = m[:, 0:128]
        kr = m[:, 128:256]
    cb = ckvn.astype(BF16)
    kup = _dot(cb, wuk_ref[...])
    v_ref[...] = _dot(cb, wuv_ref[...]).astype(BF16)
    yield
    for h in range(MLA_HEADS):
        sl = slice(HEAD_PAD * h, HEAD_PAD * (h + 1))
        kh = _head_norm(kup[:, sl] + kr, kn_ref[...])
        if rope:
            kh = _rope(kh, cos, sa, sb)
        k_ref[:, sl] = kh.astype(BF16)
        if has_q:
            qh = _head_norm(q[:, sl], qn_ref[...])
            if rope:
                qh = _rope(qh, cos, sa, sb)
            q_ref[:, sl] = (qh * (MLA_QK ** -0.5)).astype(BF16)
        if h % 2 == 1:
            yield


def _mla_prep_kernel(has_q, rope, m_ref, *refs):
    for _ in _mla_steps(m_ref[...], has_q, rope, 0, 0, refs):
        pass


def _inproj_kernel(fuse_mla, cache_seqs, n_aliased, x_ref, mod_ref, nw_ref, w_ref, *refs):
    n_mla_in = 7 + n_aliased if fuse_mla else 0
    mla_refs, (o_mla, o_g, o_hy, o_s5, o_gla) = refs[:n_mla_in], refs[n_mla_in:n_mla_in + 5]
    w_src, w_ref = w_ref, refs[-1]

    @pl.when(pl.program_id(0) == 0)
    def _():
        dst = 0
        for lo, hi in W_IN_ORDER:
            w_ref[:, dst:dst + hi - lo] = w_src[:, lo:hi]
            dst += hi - lo

    x = x_ref[...]
    ms = jnp.mean(x * x, axis=-1, keepdims=True)
    y = x * lax.rsqrt(ms + EPS) * nw_ref[...]
    h = (y * (1.0 + mod_ref[0, 1:2, :]) + mod_ref[0, 0:1, :]).astype(BF16)
    project = lambda o, seg: o.__setitem__(Ellipsis, _dot(h, w_ref[:, seg[0]:seg[1]]).astype(o.dtype))
    m = _dot(h, w_ref[:, SEG_MLA[0]:SEG_MLA[1]])
    o_mla[...] = m
    steps = iter(()) if not fuse_mla else _mla_steps(
        m, True, False, cache_seqs, n_aliased, list(mla_refs) + list(refs[n_mla_in + 5:-1]))
    project(o_g, SEG_GATE)
    next(steps, None)
    next(steps, None)
    project(o_hy, SEG_HY)
    next(steps, None)
    project(o_s5, SEG_S5)
    project(o_gla, SEG_GLA)
    for _ in steps:
        pass


def _inproj(x, mod, mod_row, wl, tm, seq_len, cache=None):
    n, d = x.shape
    fuse_mla = cache is not None
    full = lambda shape: pl.BlockSpec(shape, lambda i: (0,) * len(shape))
    row = lambda w: pl.BlockSpec((tm, w), lambda i: (i, 0))
    args = [x, mod, wl["norm_w"], wl["w_in"]]
    specs = [row(d), pl.BlockSpec((1, 3, d), lambda i: (mod_row(i * tm), 0, 0)), full((1, d)), full((d, N_PROJ))]
    widths = [hi - lo for lo, hi in (SEG_MLA, SEG_GATE, SEG_HY, SEG_S5, SEG_GLA)]
    dtypes = [F32, BF16, F32, F32, F32]
    aliases = {}
    nseq = 0
    if fuse_mla:
        args += [wl["qa_norm"], wl["w_uq"], wl["q_norm"], wl["kva_norm"], wl["w_uk"], wl["w_uv"], wl["k_norm"]]
        specs += [full((1, 256)), full((256, 512)), full((1, 128)), full((1, 128)),
                  full((128, 512)), full((128, 256)), full((1, 128))]
        widths += [512, 512, 256]
        dtypes += [BF16, BF16, BF16]
    out_specs = [row(w) for w in widths]
    out_shape = [jax.ShapeDtypeStruct((n, w), t) for w, t in zip(widths, dtypes)]
    if fuse_mla:
        layer, prev = cache
        nseq = tm // seq_len
        for w in (MLA_KV_RANK, MLA_ROPE):
            out_specs.append(pl.BlockSpec((nseq, 1, seq_len, w), lambda i: (i, layer, 0, 0)))
            out_shape.append(jax.ShapeDtypeStruct((n // seq_len, DEPTH, seq_len, w), F32))
        for k, buf in enumerate(prev):
            aliases[len(args)] = len(out_shape) - 2 + k
            args.append(buf)
            specs.append(pl.BlockSpec(memory_space=pl.ANY))
    return pl.pallas_call(
        functools.partial(_inproj_kernel, fuse_mla, nseq, len(aliases)),
        grid=(n // tm,),
        in_specs=specs, out_specs=out_specs, out_shape=out_shape,
        input_output_aliases=aliases,
        scratch_shapes=[pltpu.VMEM((d, N_PROJ), BF16)],
        compiler_params=_cp("arbitrary"),
        name="inproj",
    )(*args)


def _mla_prep(m, wl, rope_tabs, seq_len, tm, has_q):
    n, wm = m.shape
    rope = rope_tabs is not None
    full = lambda shape: pl.BlockSpec(shape, lambda i: (0,) * len(shape))
    row = lambda w: pl.BlockSpec((tm, w), lambda i: (i, 0))
    args, specs = [m], [row(wm)]
    if has_q:
        args += [wl["qa_norm"], wl["w_uq"], wl["q_norm"], wl["kva_norm"]]
        specs += [full((1, 256)), full((256, 512)), full((1, 128)), full((1, 128))]
    args += [wl["w_uk"], wl["w_uv"], wl["k_norm"]]
    specs += [full((128, 512)), full((128, 256)), full((1, 128))]
    if rope:
        nt = seq_len // tm
        args += list(rope_tabs)
        specs += [pl.BlockSpec((tm, HEAD_PAD), lambda i: (i % nt, 0))] * 3
    widths = ([512] if has_q else []) + [512, 256]
    return pl.pallas_call(
        functools.partial(_mla_prep_kernel, has_q, rope),
        grid=(n // tm,),
        in_specs=specs, out_specs=[row(w) for w in widths],
        out_shape=[jax.ShapeDtypeStruct((n, w), BF16) for w in widths],
        compiler_params=_cp("arbitrary"),
        name="mla_prep",
    )(*args)


def _attn_kernel(nparts, nseq, q_ref, *refs):
    kv = [(refs[2 * i], refs[2 * i + 1]) for i in range(nparts)]
    o_ref = refs[2 * nparts]
    tq = q_ref.shape[0] // nseq
    low = lax.broadcasted_iota(jnp.int32, (1, HEAD_PAD), 1) < MLA_V
    units = [(s, h) for s in range(nseq) for h in range(MLA_HEADS)]

    def keys(ref, s):
        lk = ref.shape[0] // nseq
        return slice(s * lk, (s + 1) * lk)

    def scores(s, h):
        sl = slice(HEAD_PAD * h, HEAD_PAD * (h + 1))
        return [_dot_nt(q_ref[s * tq:(s + 1) * tq, sl], k_ref[keys(k_ref, s), sl]) for k_ref, _ in kv]

    s_next = scores(*units[0])
    acc = None
    for n, (s, h) in enumerate(units):
        pair, j = divmod(h, 2)
        sc = s_next
        if n + 1 < len(units):
            s_next = scores(*units[n + 1])
        if j == 0:
            v_half = []
            for _, v_ref in kv:
                vp = v_ref[keys(v_ref, s), HEAD_PAD * pair:HEAD_PAD * (pair + 1)]
                zero = jnp.zeros_like(vp)
                v_half.append((jnp.where(low, vp, zero), jnp.where(low, zero, vp)))
        m = functools.reduce(jnp.maximum, [jnp.max(x, axis=-1, keepdims=True) for x in sc])
        p = [jnp.exp(x - m) for x in sc]
        den = functools.reduce(jnp.add, [jnp.sum(x, axis=-1, keepdims=True) for x in p])
        num = functools.reduce(jnp.add, [_dot(x.astype(BF16), vh[j]) for x, vh in zip(p, v_half)])
        o = num / den
        acc = o if j == 0 else acc + o
        if j == 1:
            o_ref[s * tq:(s + 1) * tq, HEAD_PAD * pair:HEAD_PAD * (pair + 1)] = acc.astype(BF16)


def _attention(q, kv_parts, batch, lq, tq):
    nq = lq // tq
    nseq = max(1, min(batch, 1024 // lq)) if nq == 1 else 1
    args, specs = [q], [pl.BlockSpec((nseq * tq, 512), lambda b, i: (b * nq + i, 0))]
    for k, v, lk in kv_parts:
        args += [k, v]
        specs += [pl.BlockSpec((nseq * lk, 512), lambda b, i: (b, 0)),
                  pl.BlockSpec((nseq * lk, 256), lambda b, i: (b, 0))]
    return pl.pallas_call(
        functools.partial(_attn_kernel, len(kv_parts), nseq),
        grid=(batch // nseq, nq),
        in_specs=specs,
        out_specs=pl.BlockSpec((nseq * tq, 256), lambda b, i: (b * nq + i, 0)),
        out_shape=jax.ShapeDtypeStruct((batch * lq, 256), BF16),
        compiler_params=_cp("arbitrary", "arbitrary"),
        name="attention",
    )(*args)


def _hyena_kernel(seq_len, bk, nseq, x_ref, cw_ref, cb_ref, fo_ref, go_ref, k_ref, bias_ref, o_ref,
                  u_sc, y_sc, z_sc):
    c = D_BRANCH
    n = nseq * seq_len
    nblk = seq_len // bk
    pos = jnp.bitwise_and(lax.broadcasted_iota(jnp.int32, (n, 1), 0), seq_len - 1)
    first, last = pos == 0, pos == seq_len - 1

    def short_conv(g):
        cols = slice(g * c, (g + 1) * c)
        x = x_ref[:, cols]
        xm = jnp.where(first, 0.0, pltpu.roll(x, 1, 0))
        xp = jnp.where(last, 0.0, pltpu.roll(x, n - 1, 0))
        return cw_ref[0:1, cols] * xm + cw_ref[1:2, cols] * x + cw_ref[2:3, cols] * xp + cb_ref[:, cols]

    fo, go = fo_ref[...], go_ref[...]

    def long_conv(s, v, order, emit):
        cols = slice(order * c, (order + 1) * c)
        for j in range(nblk):
            u_sc[s, j] = _dot(fo, v[j * bk:(j + 1) * bk].astype(BF16))
        yield
        rc = 32
        for i in range(nblk):
            def mix(r, carry, i=i):
                top = pl.ds(pl.multiple_of(r * rc, rc), rc)
                bot = pl.ds(pl.multiple_of(bk + r * rc, rc), rc)
                at = ab = None
                for j in range(nblk):
                    q = i - j + nblk - 1
                    kt, kb = k_ref[q, top, cols], k_ref[q, bot, cols]
                    ut, ub = u_sc[s, j, top, :], u_sc[s, j, bot, :]
                    pt, pb = ut * kt - ub * kb, ut * kb + ub * kt
                    at, ab = (pt, pb) if at is None else (at + pt, ab + pb)
                z_sc[s, top, :] = at.astype(BF16)
                z_sc[s, bot, :] = ab.astype(BF16)
                return carry

            lax.fori_loop(0, bk // rc, mix, 0, unroll=True if nblk == 1 else 2)
            y = _dot(go, z_sc[s])
            yield
            emit(i, y)

    v_all, x1_all, x2_all = short_conv(0), short_conv(1), short_conv(2)

    def sequence(s):
        base = s * seq_len
        v = v_all[base:base + seq_len]

        def emit1(i, y):
            r = slice(i * bk, (i + 1) * bk)
            y_sc[s, r, :] = x1_all[base + i * bk:base + (i + 1) * bk] * (y + bias_ref[0:1, :] * v[r])

        yield from long_conv(s, v, 0, emit1)
        y1 = y_sc[s]

        def emit2(i, y):
            r = slice(i * bk, (i + 1) * bk)
            o_ref[base + i * bk:base + (i + 1) * bk, :] = (
                x2_all[base + i * bk:base + (i + 1) * bk] * (y + bias_ref[1:2, :] * y1[r])).astype(BF16)

        yield from long_conv(s, y1, 1, emit2)

    live = [sequence(s) for s in range(nseq)]
    while live:
        for g in list(live):
            if next(g, StopIteration) is StopIteration:
                live.remove(g)


def _hyena(x, wl, kspec, fo, go, batch, seq_len):
    c = D_BRANCH
    bk = fo.shape[1]
    nseq = max(1, min(batch, 1024 // seq_len))
    rows = nseq * seq_len
    nblk = seq_len // bk
    full = lambda a: pl.BlockSpec(a.shape, lambda i: (0,) * a.ndim)
    return pl.pallas_call(
        functools.partial(_hyena_kernel, seq_len, bk, nseq),
        grid=(batch // nseq,),
        in_specs=[pl.BlockSpec((rows, 3 * c), lambda i: (i, 0)),
                  full(wl["hy_conv_w"]), full(wl["hy_conv_b"]), full(fo), full(go),
                  pl.BlockSpec(kspec.shape, lambda i: (0, 0, 0), pipeline_mode=pl.Buffered(1)),
                  full(wl["hy_bias"])],
        out_specs=pl.BlockSpec((rows, c), lambda i: (i, 0)),
        out_shape=jax.ShapeDtypeStruct((batch * seq_len, c), BF16),
        scratch_shapes=[pltpu.VMEM((nseq, nblk, 2 * bk, c), F32), pltpu.VMEM((nseq, seq_len, c), F32),
                        pltpu.VMEM((nseq, 2 * bk, c), BF16)],
        compiler_params=pltpu.CompilerParams(dimension_semantics=("arbitrary",), vmem_limit_bytes=56 * 1024 * 1024),
        name="hyena",
    )(x, wl["hy_conv_w"], wl["hy_conv_b"], fo, go, kspec, wl["hy_bias"])


def _hy_mlp_kernel(feat_ref, w1_ref, b1_ref, f1_ref, w2_ref, b2_ref, f2_ref, w3_ref, win_ref,
                   kern_ref, nrm_ref):
    i = pl.program_id(0)
    tl = win_ref.shape[0]
    h = jnp.sin(f1_ref[...] * (_dot3(feat_ref[...], w1_ref[...]) + b1_ref[...]))
    h = jnp.sin(f2_ref[...] * (_dot3(h, w2_ref[...]) + b2_ref[...]))
    filt = _dot3(h, w3_ref[...])
    filt = jnp.concatenate([filt[:, 0:512], filt[:, 512:1024]], axis=0)
    win = win_ref[...]
    row0 = (lax.broadcasted_iota(jnp.int32, (tl, 1), 0) + i * tl) == 0
    filt = jnp.where(row0, 0.0, filt * jnp.concatenate([win, win], axis=1))
    kern_ref[...] = filt
    part = jnp.sum(jnp.abs(filt), axis=0, keepdims=True)

    @pl.when(i == 0)
    def _():
        nrm_ref[...] = jnp.zeros_like(nrm_ref)
    nrm_ref[...] += jnp.broadcast_to(part, nrm_ref.shape)


def _hy_mlp(feat, wl, win):
    rows = win.shape[0]
    tl = 256
    nl = rows // (2 * tl)
    full = lambda shape: pl.BlockSpec(shape, lambda i: (0,) * len(shape))
    return pl.pallas_call(
        _hy_mlp_kernel,
        grid=(rows // tl,),
        in_specs=[pl.BlockSpec((tl // 2, 128), lambda i: (i, 0)),
                  full((128, 128)), full((1, 128)), full((1, 128)),
                  full((128, 128)), full((1, 128)), full((1, 128)),
                  pl.BlockSpec((128, 1024), lambda i: (0, jnp.where(i < nl, 1, 0))),
                  pl.BlockSpec((tl, 256), lambda i: (i, 0))],
        out_specs=[pl.BlockSpec((tl, 512), lambda i: (i, 0)), full((8, 512))],
        out_shape=[jax.ShapeDtypeStruct((rows, 512), F32), jax.ShapeDtypeStruct((8, 512), F32)],
        compiler_params=_cp("arbitrary"),
        name="hy_mlp",
    )(feat, wl["hy_w1"], wl["hy_b1"], wl["hy_f1"], wl["hy_w2"], wl["hy_b2"], wl["hy_f2"], wl["hy_w3"], win)


def _hy_kspec_kernel(lo_ref, hi_ref, fo_ref, fb_ref, n_ref, o_ref):
    bk = lo_ref.shape[0]
    k = _dot(fo_ref[...], hi_ref[...].astype(BF16)) + _dot(fb_ref[...], lo_ref[...].astype(BF16))
    o_ref[0] = k * ((1.0 / bk) / n_ref[0:1, :])


def _hy_kspec(kern_lin, nrm, fo, fb):
    n2, bk = fo.shape
    nq = kern_lin.shape[0] // bk - 1
    full = lambda a: pl.BlockSpec(a.shape, lambda q: (0,) * a.ndim)
    return pl.pallas_call(
        _hy_kspec_kernel,
        grid=(nq,),
        in_specs=[pl.BlockSpec((bk, 512), lambda q: (q, 0)),
                  pl.BlockSpec((bk, 512), lambda q: (q + 1, 0)),
                  full(fo), full(fb), full(nrm)],
        out_specs=pl.BlockSpec((1, n2, 512), lambda q: (q, 0, 0)),
        out_shape=jax.ShapeDtypeStruct((nq, n2, 512), F32),
        compiler_params=_cp("arbitrary"),
        name="hy_kspec",
    )(kern_lin, kern_lin, fo, fb, nrm)


def _s5_discretise(are_ref, aim_ref, ldt_ref):
    ar = jnp.minimum(are_ref[0], -1e-4)
    ai = aim_ref[0]
    dt = jnp.exp(ldt_ref[0])
    e = jnp.exp(ar * dt)
    return ar, ai, e * jnp.cos(ai * dt), e * jnp.sin(ai * dt)


def _s5_scan_kernel(nseq, emit_y, u_ref, hin_ref, are_ref, aim_ref, ldt_ref, bre_ref, bim_ref, *rest):
    if emit_y:
        cre_ref, cim_ref, y_ref, hfin_ref, wb_sc, ab_sc, s_sc, hc_sc, perm_sc, wc_sc = rest
    else:
        hfin_ref, wb_sc, ab_sc, s_sc, hc_sc, perm_sc = rest
    d = pl.program_id(0)
    c = pl.program_id(1)
    n = S5_N

    @pl.when(c == 0)
    def _():
        ar, ai, abr, abi = _s5_discretise(are_ref, aim_ref, ldt_ref)
        ab_sc[0:1, :] = abr
        ab_sc[1:2, :] = abi
        den = 1.0 / (ar * ar + ai * ai)
        cr = ((abr - 1.0) * ar + abi * ai) * den
        ci = (abi * ar - (abr - 1.0) * ai) * den
        grp_rows = lax.shift_right_logical(lax.broadcasted_iota(jnp.int32, (D_BRANCH, 1), 0), 4)
        grp_cols = lax.shift_right_logical(lax.broadcasted_iota(jnp.int32, (1, n), 1), 6)
        expand_b = lambda ref: jnp.where(grp_rows == grp_cols, jnp.concatenate([ref[0]] * (n // 128), axis=1), 0.0)
        bre, bim = expand_b(bre_ref), expand_b(bim_ref)
        wb_sc[:, 0:n] = (cr * bre - ci * bim).astype(BF16)
        wb_sc[:, n:2 * n] = (cr * bim + ci * bre).astype(BF16)
        if emit_y:
            st_rows = lax.shift_right_logical(lax.broadcasted_iota(jnp.int32, (n, 1), 0), 6)
            ch_cols = lax.shift_right_logical(lax.broadcasted_iota(jnp.int32, (1, D_BRANCH), 1), 4)
            expand_c = lambda ref: jnp.where(st_rows == ch_cols, jnp.concatenate([ref[0]] * 2, axis=1), 0.0)
            wc_sc[0:n, :] = expand_c(cre_ref).astype(BF16)
            wc_sc[n:2 * n, :] = (-expand_c(cim_ref)).astype(BF16)
        hc_sc[...] = hin_ref[0]

    steps = u_ref.shape[1]
    rows_c = nseq * steps

    @pl.when(c == 0)
    def _():
        i = lax.broadcasted_iota(jnp.int32, (rows_c, rows_c), 0)
        j = lax.broadcasted_iota(jnp.int32, (rows_c, rows_c), 1)
        p = lax.shift_right_logical(i, int(math.log2(nseq)))
        step = p + d * (steps - 1 - 2 * p)
        src = jnp.bitwise_and(i, nseq - 1) * steps + step
        perm_sc[...] = jnp.where(j == src, 1.0, 0.0).astype(BF16)

    lhs = _dot(perm_sc[...], u_ref[...].reshape(rows_c, D_BRANCH).astype(BF16)).astype(BF16)
    lb = 256
    y = None
    for j in range(n // lb):
        lr = slice(lb * j, lb * (j + 1))
        li = slice(n + lb * j, n + lb * (j + 1))
        bur = _dot(lhs, wb_sc[:, lr])
        bui = _dot(lhs, wb_sc[:, li])
        abr = ab_sc[0:1, lr]
        abi = ab_sc[1:2, lr]
        hr, hi = hc_sc[:, lr], hc_sc[:, li]
        for p in range(steps):
            rows = slice(p * nseq, (p + 1) * nseq)
            hr, hi = abr * hr - abi * hi + bur[rows], abr * hi + abi * hr + bui[rows]
            if emit_y:
                s_sc[rows, lr] = hr.astype(BF16)
                s_sc[rows, li] = hi.astype(BF16)
        hc_sc[:, lr] = hr
        hc_sc[:, li] = hi
        if emit_y:
            yj = _dot(s_sc[:, lr], wc_sc[lr, :]) + _dot(s_sc[:, li], wc_sc[li, :])
            y = yj if y is None else y + yj

    if emit_y:
        @pl.when(d == 0)
        def _():
            for p in range(steps):
                y_ref[0, :, p, :] = y[p * nseq:(p + 1) * nseq]

        @pl.when(d == 1)
        def _():
            for p in range(steps):
                y_ref[0, :, steps - 1 - p, :] = y[p * nseq:(p + 1) * nseq]

    @pl.when(c == pl.num_programs(1) - 1)
    def _():
        hfin_ref[0] = hc_sc[...]


def _s5_scan(u, hin, wl, emit_y):
    nseq, nstep, _ = u.shape
    steps = S5_ROWS // nseq
    nc = nstep // steps
    n = S5_N
    chunk = lambda d, c: c + d * (nc - 1 - 2 * c)
    per_dir = lambda shape: pl.BlockSpec((1,) + shape, lambda d, c: (d,) + (0,) * len(shape))
    args = [u, hin, wl["s5_are"], wl["s5_aim"], wl["s5_ldt"], wl["s5_bre"], wl["s5_bim"]]
    specs = [pl.BlockSpec((nseq, steps, D_BRANCH), lambda d, c: (0, chunk(d, c), 0)),
             per_dir((nseq, 2 * n)), per_dir((1, n)), per_dir((1, n)), per_dir((1, n)),
             per_dir((D_BRANCH, 128)), per_dir((D_BRANCH, 128))]
    out_specs = [per_dir((nseq, 2 * n))]
    out_shape = [jax.ShapeDtypeStruct((2, nseq, 2 * n), F32)]
    scratch = [pltpu.VMEM((D_BRANCH, 2 * n), BF16), pltpu.VMEM((8, n), F32),
               pltpu.VMEM((S5_ROWS, 2 * n), BF16), pltpu.VMEM((nseq, 2 * n), F32),
               pltpu.VMEM((S5_ROWS, S5_ROWS), BF16)]
    if emit_y:
        args += [wl["s5_cre"], wl["s5_cim"]]
        specs += [per_dir((n, 128)), per_dir((n, 128))]
        out_specs = [pl.BlockSpec((1, nseq, steps, D_BRANCH), lambda d, c: (d, 0, chunk(d, c), 0))] + out_specs
        out_shape = [jax.ShapeDtypeStruct((2, nseq, nstep, D_BRANCH), F32)] + out_shape
        scratch += [pltpu.VMEM((2 * n, D_BRANCH), BF16)]
    return pl.pallas_call(
        functools.partial(_s5_scan_kernel, nseq, emit_y),
        grid=(2, nc),
        in_specs=specs, out_specs=out_specs, out_shape=out_shape, scratch_shapes=scratch,
        compiler_params=_cp("arbitrary", "arbitrary"),
        name="s5_scan" if emit_y else "s5_scan_finals",
    )(*args)


def _s5_chain_kernel(batch, nseg, f_ref, h0_ref, are_ref, aim_ref, ldt_ref, o_ref):
    d = pl.program_id(0)
    n = S5_N
    _, _, pr, pi = _s5_discretise(are_ref, aim_ref, ldt_ref)
    for _ in range(int(math.log2(S5_SEG))):
        pr, pi = pr * pr - pi * pi, 2.0 * pr * pi
    f = f_ref[0]
    fr, fi = f[:, 0:n], f[:, n:2 * n]
    h0 = h0_ref[0]
    h0r, h0i = h0[:, 0:n], h0[:, n:2 * n]
    nrow = batch * nseg
    seg = jnp.bitwise_and(lax.broadcasted_iota(jnp.int32, (nrow, 1), 0), nseg - 1)

    def run(shift, keep):
        xr, xi = h0r, h0i
        for _ in range(nseg - 1):
            zr = fr + pr * xr - pi * xi
            zi = fi + pr * xi + pi * xr
            xr = h0r + jnp.where(keep, pltpu.roll(zr, shift, 0), 0.0)
            xi = h0i + jnp.where(keep, pltpu.roll(zi, shift, 0), 0.0)
        o_ref[0, :, 0:n] = xr
        o_ref[0, :, n:2 * n] = xi

    @pl.when(d == 0)
    def _():
        run(1, seg != 0)

    @pl.when(d == 1)
    def _():
        run(nrow - 1, seg != nseg - 1)


def _s5_chain(fin, h0rows, wl, batch, nseg):
    nrow = batch * nseg
    n = S5_N
    per_dir = lambda shape: pl.BlockSpec((1,) + shape, lambda d: (d,) + (0,) * len(shape))
    return pl.pallas_call(
        functools.partial(_s5_chain_kernel, batch, nseg),
        grid=(2,),
        in_specs=[per_dir((nrow, 2 * n)), per_dir((nrow, 2 * n)), per_dir((1, n)), per_dir((1, n)), per_dir((1, n))],
        out_specs=per_dir((nrow, 2 * n)),
        out_shape=jax.ShapeDtypeStruct((2, nrow, 2 * n), F32),
        compiler_params=_cp("arbitrary"),
        name="s5_chain",
    )(fin, h0rows, wl["s5_are"], wl["s5_aim"], wl["s5_ldt"])


def _gla_kernel(seq_len, nb, n_aliased, q_ref, k_ref, v_ref, g_ref, gw_ref, gb_ref, s0_ref, *rest):
    o_ref, sfin_ref, qe_sc, upd_sc, dec_sc, sall_sc, lhs_sc, kt_sc, la_sc, oi_sc = rest[n_aliased:]
    d = pl.program_id(1)
    sign = 1 - 2 * d
    ck, sup = GLA_CHUNK, GLA_SUPER
    cps = sup // ck
    nsup, nchunk = seq_len // sup, seq_len // ck
    dk, dv = GLA_HEADS * GLA_DK, GLA_HEADS * GLA_DV
    r = lax.broadcasted_iota(jnp.int32, (sup, sup), 0)
    s = lax.broadcasted_iota(jnp.int32, (sup, sup), 1)
    same = lax.shift_right_logical(r, 6) == lax.shift_right_logical(s, 6)
    tri = jnp.logical_and(same, (s - r) * sign <= 0)
    cum_lhs = jnp.where(tri, 1.0, 0.0).astype(BF16)
    t4 = lax.broadcasted_iota(jnp.int32, (ck, GLA_HEADS * ck), 0)
    s4 = jnp.bitwise_and(lax.broadcasted_iota(jnp.int32, (ck, GLA_HEADS * ck), 1), ck - 1)
    tri4 = (s4 - t4) * sign <= 0
    pos = jnp.bitwise_and(lax.broadcasted_iota(jnp.int32, (ck, 1), 0), ck - 1)
    is_last = pos == (ck - 1) * (1 - d)
    row_chunk = lax.shift_right_logical(lax.broadcasted_iota(jnp.int32, (sup, 1), 0), 6)
    head_k = lax.shift_right_logical(lax.broadcasted_iota(jnp.int32, (1, dk), 1), 5)
    head_v = lax.shift_right_logical(lax.broadcasted_iota(jnp.int32, (1, dv), 1), 6)
    blockdiag = lax.shift_right_logical(lax.broadcasted_iota(jnp.int32, (dv, 1), 0), 6) == head_k

    def group_rows(u):
        return pl.ds(u * sup, sup) if isinstance(u, int) else pl.ds(pl.multiple_of(u * sup, sup), sup)

    def stage_a(u, slot):
        rows = group_rows(u)
        q = q_ref[rows, :] * (GLA_DK ** -0.5)
        k = k_ref[rows, :]
        v = v_ref[rows, :]
        cs = _dot(cum_lhs, la_sc[rows, :])
        yield
        bc = cs[:, 0:dk] + cs[:, dk:2 * dk]
        tots = [jnp.sum(jnp.where(is_last, bc[c * ck:(c + 1) * ck], 0.0), axis=0, keepdims=True)
                for c in range(cps)]
        tot = jnp.concatenate([jnp.broadcast_to(t, (ck, dk)) for t in tots], axis=0)
        ref = 0.5 * tot
        kt_sc[slot] = (k * jnp.exp(ref - bc)).astype(BF16)
        lhs_sc[slot] = (q * jnp.exp(bc - ref)).astype(BF16)
        qe_sc[rows, :] = (q * jnp.exp(bc)).astype(BF16)
        kl = (k * jnp.exp(tot - bc)).astype(BF16)
        zero = jnp.zeros_like(kl)
        klx = jnp.concatenate([jnp.where(row_chunk == c, kl, zero) for c in range(cps)], axis=1)
        upd = _dot_tn(v.astype(BF16), klx)
        yield
        for c in range(cps):
            upd_sc[u * cps + c] = jnp.where(blockdiag, upd[:, c * dk:(c + 1) * dk], 0.0)
            dec_sc[u * cps + c] = jnp.broadcast_to(jnp.exp(tots[c]), (8, dk))

    def stage_b(u, slot):
        rows = group_rows(u)
        v = v_ref[rows, :].astype(BF16)
        qt, kt = lhs_sc[slot], kt_sc[slot]
        zk, zv = jnp.zeros_like(kt[0:ck]), jnp.zeros_like(v[0:ck])
        p = [_dot_nt(qt[c * ck:(c + 1) * ck],
                     jnp.concatenate([jnp.where(head_k == h, kt[c * ck:(c + 1) * ck], zk)
                                      for h in range(GLA_HEADS)], axis=0)) for c in range(cps)]
        yield
        o = [_dot(jnp.where(tri4, p[c], 0.0).astype(BF16),
                  jnp.concatenate([jnp.where(head_v == h, v[c * ck:(c + 1) * ck], zv)
                                   for h in range(GLA_HEADS)], axis=0)) for c in range(cps)]
        yield
        oi_sc[rows, :] = jnp.concatenate(o, axis=0)

    def run(*stages):
        live = list(stages)
        while live:
            for g in list(live):
                if next(g, StopIteration) is StopIteration:
                    live.remove(g)

    x = _dot(g_ref[...].astype(BF16), gw_ref[0]) + gb_ref[0]
    la = (jnp.minimum(x, 0.0) - jnp.log(1.0 + jnp.exp(-jnp.abs(x)))) * (1.0 / GLA_TAU)
    la_sc[...] = jnp.concatenate(_split2(la), axis=1)

    nu = nb * nsup
    if nu <= 4:
        run(*[stage_a(u, u) for u in range(min(2, nu))])
        for u in range(0, nu, 2):
            run(*([stage_b(v, v % 4) for v in range(u, min(u + 2, nu))]
                  + [stage_a(v, v % 4) for v in range(u + 2, min(u + 4, nu))]))
    else:
        run(stage_a(0, 0), stage_a(1, 1))

        def sup_body(t, carry):
            u = 2 * t
            n0, n1 = jnp.minimum(u + 2, nu - 2), jnp.minimum(u + 3, nu - 1)
            run(stage_b(u, jnp.bitwise_and(u, 3)), stage_b(u + 1, jnp.bitwise_and(u + 1, 3)),
                stage_a(n0, jnp.bitwise_and(u + 2, 3)), stage_a(n1, jnp.bitwise_and(u + 3, 3)))
            return carry

        lax.fori_loop(0, nu // 2, sup_body, 0)

    for j in range(nb):
        def state_body(c, st, j=j):
            ci = j * nchunk + c + d * (nchunk - 1 - 2 * c)
            sall_sc[ci] = st.astype(BF16)
            return dec_sc[ci][0:1, :] * st + upd_sc[ci]

        st_fin = jnp.transpose(lax.fori_loop(0, nchunk, state_body, s0_ref[j, 0]))
        for h in range(GLA_HEADS):
            sfin_ref[j, 0, 0, h] = st_fin[h * GLA_DK:(h + 1) * GLA_DK, h * GLA_DV:(h + 1) * GLA_DV]

    def inter(u):
        rows = group_rows(u)
        qe = qe_sc[rows, :]
        oi = jnp.concatenate([_dot_nt(qe[c * ck:(c + 1) * ck], sall_sc[u * cps + c]) for c in range(cps)], axis=0)
        o_ref[0, rows, :] = (oi_sc[rows, :] + oi).astype(BF16)

    if nu <= 4:
        for u in range(nu):
            inter(u)
    else:
        lax.fori_loop(0, nu, lambda u, carry: (inter(u), carry)[1], 0, unroll=2 if nu % 2 == 0 else 1)


def _gla(gla_in, mla_in, wl, s0t, batch, seq_len, fin=(0, None, 1)):
    layer, prev_fin, fin_layers = fin
    aliases = {} if prev_fin is None else {7: 1}
    extra = [] if prev_fin is None else [prev_fin]
    n = gla_in.shape[0]
    dk, dv = GLA_HEADS * GLA_DK, GLA_HEADS * GLA_DV
    nb = max(1, min(batch, 1024 // seq_len))
    rows = nb * seq_len
    nchunk = nb * (seq_len // GLA_CHUNK)
    return pl.pallas_call(
        functools.partial(_gla_kernel, seq_len, nb, len(extra)),
        grid=(batch // nb, 2),
        input_output_aliases=aliases,
        in_specs=[pl.BlockSpec((rows, dk), lambda b, d: (b, 0)),
                  pl.BlockSpec((rows, dk), lambda b, d: (b, 1)),
                  pl.BlockSpec((rows, dv), lambda b, d: (b, 1)),
                  pl.BlockSpec((rows, dk), lambda b, d: (b, 1)),
                  pl.BlockSpec((1, dk, dk), lambda b, d: (d, 0, 0)),
                  pl.BlockSpec((1, 1, dk), lambda b, d: (d, 0, 0)),
                  pl.BlockSpec((nb, 1, dv, dk), lambda b, d: (b, d, 0, 0))]
                 + [pl.BlockSpec(memory_space=pl.ANY)] * len(extra),
        out_specs=[pl.BlockSpec((1, rows, dv), lambda b, d: (d, b, 0)),
                   pl.BlockSpec((nb, 1, 1, GLA_HEADS, GLA_DK, GLA_DV), lambda b, d: (b, layer, d, 0, 0, 0))],
        out_shape=[jax.ShapeDtypeStruct((2, n, dv), BF16),
                   jax.ShapeDtypeStruct((batch, fin_layers, 2, GLA_HEADS, GLA_DK, GLA_DV), F32)],
        scratch_shapes=[pltpu.VMEM((rows, dk), BF16),
                        pltpu.VMEM((nchunk, dv, dk), F32),
                        pltpu.VMEM((nchunk, 8, dk), F32),
                        pltpu.VMEM((nchunk, dv, dk), BF16),
                        pltpu.VMEM((4, GLA_SUPER, dk), BF16),
                        pltpu.VMEM((4, GLA_SUPER, dk), BF16),
                        pltpu.VMEM((rows, 2 * dk), BF16),
                        pltpu.VMEM((rows, dv), F32)],
        compiler_params=_cp("arbitrary", "arbitrary"),
        name="gla",
    )(gla_in, gla_in, gla_in, mla_in, wl["gla_gw"], wl["gla_gb"], s0t, *extra)


def _outproj_kernel(x_ref, mod_ref, g_ref, om_ref, oh_ref, su_ref, sf_ref, sb_ref, sd_ref, sw_ref, sbias_ref,
                    gf_ref, gb_ref, gn_ref, hm_ref, w_ref, y_ref):
    c = D_BRANCH
    g = g_ref[...].astype(F32)
    acc = _dot((om_ref[...].astype(F32) * _silu(g[:, 0:c])).astype(BF16), w_ref[0:c, :])
    acc += _dot((oh_ref[...].astype(F32) * _silu(g[:, c:2 * c])).astype(BF16), w_ref[c:2 * c, :])
    ys = sd_ref[...] * su_ref[...] + sf_ref[0] + sb_ref[0]
    ge = 0.5 * ys * (1.0 + jnp.tanh(math.sqrt(2.0 / math.pi) * (ys + 0.044715 * (ys * ys * ys))))
    o_s5 = ge / (1.0 + jnp.exp(-(_dot(ge.astype(BF16), sw_ref[...]) + sbias_ref[...])))
    acc += _dot((o_s5 * _silu(g[:, 2 * c:3 * c])).astype(BF16), w_ref[2 * c:3 * c, :])
    og = gf_ref[0].astype(F32) + gb_ref[0].astype(F32)
    hi, lo = _split2(og * og)
    ms = _dot(hi, hm_ref[...]) + _dot(lo, hm_ref[...])
    ogn = og * lax.rsqrt(ms + EPS) * gn_ref[...]
    acc += _dot((ogn * _silu(g[:, 3 * c:4 * c])).astype(BF16), w_ref[3 * c:4 * c, :])
    y_ref[...] = x_ref[...] + mod_ref[0, 2:3, :] * acc


def _outproj(x, mod, mod_row, gates, o_mla, o_hy, s5_u, s5_y, o_gla, wl, tm):
    n, d = x.shape
    c = D_BRANCH
    row = lambda w: pl.BlockSpec((tm, w), lambda i: (i, 0))
    per_dir = lambda k: pl.BlockSpec((1, tm, c), lambda i: (k, i, 0))
    full = lambda *shape: pl.BlockSpec(shape, lambda i: (0,) * len(shape))
    return pl.pallas_call(
        _outproj_kernel,
        grid=(n // tm,),
        in_specs=[row(d),
                  pl.BlockSpec((1, 3, d), lambda i: (mod_row(i * tm), 0, 0)),
                  row(d), row(c), row(c),
                  row(c), per_dir(0), per_dir(1), full(1, c), full(c, c), full(1, c),
                  per_dir(0), per_dir(1), full(1, c), full(c, c), full(d, d)],
        out_specs=row(d),
        out_shape=jax.ShapeDtypeStruct((n, d), F32),
        compiler_params=_cp("arbitrary"),
        name="outproj",
    )(x, mod, gates, o_mla, o_hy, s5_u, s5_y, s5_y, wl["s5_d"], wl["s5_glu_w"], wl["s5_glu_b"],
      o_gla, o_gla, wl["gla_norm"], wl["head_mean"], wl["w_out"])


def _rope_tables(seq_len):
    pos = np.arange(seq_len)
    inv = ROPE_BASE ** (-np.arange(0, 16, 2, dtype=np.float64) / 16.0)
    cos = np.ones((seq_len, HEAD_PAD))
    sin_a = np.zeros((seq_len, HEAD_PAD))
    sin_b = np.zeros((seq_len, HEAD_PAD))
    for base, p in ((MLA_NOPE, pos // GRID_W), (MLA_NOPE + 16, pos % GRID_W)):
        ang = p[:, None].astype(np.float64) * inv[None, :]
        cos[:, base:base + 8] = np.cos(ang)
        cos[:, base + 8:base + 16] = np.cos(ang)
        sin_a[:, base:base + 8] = -np.sin(ang)
        sin_b[:, base + 8:base + 16] = np.sin(ang)
    return tuple(jnp.asarray(t, F32) for t in (cos, sin_a, sin_b))


def _odd_dft(seq_len):
    bk = min(seq_len, HY_BLOCK)
    k = np.arange(bk)[:, None]
    t = np.arange(bk)[None, :]

    def mat(shift):
        ang = (np.pi / (2 * bk)) * (((2 * k + 1) * (t + shift)) % (4 * bk))
        return np.concatenate([np.cos(ang), -np.sin(ang)], axis=0)

    fo = mat(0)
    fb = -mat(bk)
    fb[:, 0] = 0.0
    const = lambda a: jnp.asarray(a, F32).astype(BF16)
    return const(fo), const(fb), const(fo.T)


def _hyena_tables(seq_len):
    lag = np.arange(-seq_len, seq_len)
    pos = np.where(lag == -seq_len, 0, np.abs(lag)).astype(np.float64)
    t = pos / seq_len
    w = 2.0 * np.pi * pos / seq_len
    bands = np.linspace(1e-4, HY_BANDS - 1, HY_BANDS)
    feat = np.zeros((2 * seq_len, HY_HIDDEN))
    feat[:, 0] = t
    feat[:, 1:1 + HY_BANDS] = np.cos(w[:, None] * bands)
    feat[:, 1 + HY_BANDS:HY_FEAT] = np.sin(w[:, None] * bands)
    feat = feat.reshape(-1, 2, 128, HY_HIDDEN).transpose(0, 2, 1, 3).reshape(seq_len, 2 * HY_HIDDEN)
    deltas = np.linspace(math.log(1.0 / HY_TARGET) / HY_FAST_DECAY, math.log(1.0 / HY_TARGET) / HY_SLOW_DECAY,
                         D_BRANCH)
    win = np.exp(-t[:, None] * deltas[None, :]) + HY_SHIFT
    return jnp.asarray(feat, F32), jnp.asarray(win, F32)


def _pad_to(a, shape):
    return jnp.pad(a, [(0, s - d) for s, d in zip(shape, a.shape)])


def _layer_weights(l, p):
    z = lambda *s: jnp.zeros(s, F32)
    d = D_MODEL
    wl = {"w_in": p["w_in"][l].astype(BF16), "norm_w": p["norm_w"][l].reshape(1, d),
          "w_out": p["w_out"][l].astype(BF16)}
    wl["qa_norm"] = _pad_to(p["mla_qa_norm"][l].reshape(1, -1), (1, 256))
    w_uq = _pad_to(p["mla_w_uq"][l].reshape(MLA_Q_RANK, MLA_HEADS, MLA_QK), (256, MLA_HEADS, HEAD_PAD))
    wl["w_uq"] = w_uq.reshape(256, MLA_HEADS * HEAD_PAD).astype(BF16)
    wl["q_norm"] = _pad_to(p["mla_q_norm"][l].reshape(1, -1), (1, HEAD_PAD))
    wl["k_norm"] = _pad_to(p["mla_k_norm"][l].reshape(1, -1), (1, HEAD_PAD))
    wl["kva_norm"] = p["mla_kva_norm"][l].reshape(1, -1)
    w_ukv = p["mla_w_ukv"][l].reshape(MLA_KV_RANK, MLA_HEADS, MLA_NOPE + MLA_V)
    wl["w_uk"] = _pad_to(w_ukv[:, :, :MLA_NOPE], (MLA_KV_RANK, MLA_HEADS, HEAD_PAD)).reshape(MLA_KV_RANK, -1).astype(BF16)
    wl["w_uv"] = w_ukv[:, :, MLA_NOPE:].reshape(MLA_KV_RANK, MLA_HEADS * MLA_V).astype(BF16)
    wl["hy_conv_w"] = p["hy_conv_w"][l]
    wl["hy_conv_b"] = p["hy_conv_b"][l].reshape(1, -1)
    hh = HY_HIDDEN
    twice = lambda a: jnp.tile(a.reshape(1, -1), (1, 2))
    bdiag = lambda a: jnp.concatenate([jnp.pad(a, ((0, 0), (0, a.shape[1]))), jnp.pad(a, ((0, 0), (a.shape[1], 0)))], axis=0)
    wl["hy_w1"] = bdiag(_pad_to(p["hy_w1"][l], (hh, hh)))
    wl["hy_b1"], wl["hy_f1"] = twice(p["hy_b1"][l]), twice(p["hy_freq1"][l])
    wl["hy_w2"] = bdiag(p["hy_w2"][l])
    wl["hy_b2"], wl["hy_f2"] = twice(p["hy_b2"][l]), twice(p["hy_freq2"][l])
    w3 = p["hy_w3"][l]
    wl["hy_w3"] = jnp.concatenate([bdiag(w3[:, 0:512]), bdiag(w3[:, 512:1024])], axis=1)
    wl["hy_bias"] = p["hy_bias"][l]
    flat = lambda a: a[l].reshape(2, 1, S5_N)
    wl["s5_are"], wl["s5_aim"] = flat(p["s5_a_re"]), flat(p["s5_a_im"])
    wl["s5_ldt"] = jnp.repeat(p["s5_log_dt"][l], S5_STATE, axis=-1).reshape(2, 1, S5_N)
    lanes_b = lambda a: jnp.tile(a[l].transpose(0, 1, 3, 2).reshape(2, D_BRANCH, S5_STATE), (1, 1, 128 // S5_STATE))
    lanes_c = lambda a: jnp.tile(a[l].transpose(0, 1, 3, 2).reshape(2, S5_N, S5_GROUP), (1, 1, 128 // S5_GROUP))
    wl["s5_bre"], wl["s5_bim"] = lanes_b(p["s5_b_re"]), lanes_b(p["s5_b_im"])
    wl["s5_cre"], wl["s5_cim"] = lanes_c(p["s5_c_re"]), lanes_c(p["s5_c_im"])
    wl["s5_d"] = p["s5_d"][l].reshape(1, -1)
    wl["s5_glu_w"] = p["s5_glu_w"][l].astype(BF16)
    wl["s5_glu_b"] = p["s5_glu_b"][l].reshape(1, -1)
    gw = p["gla_gw"][l]
    dk = GLA_HEADS * GLA_DK
    wl["gla_gw"] = jnp.stack([_pad_to(jnp.pad(gw[i], ((GLA_G_LANE + GLA_RANK * i, 0), (0, 0))), (dk, dk))
                              for i in range(2)]).astype(BF16)
    wl["gla_gb"] = p["gla_gb"][l].reshape(2, 1, dk)
    wl["gla_norm"] = jnp.tile(p["gla_norm"][l], GLA_HEADS).reshape(1, -1)
    head = np.arange(D_BRANCH) // GLA_DV
    wl["head_mean"] = jnp.asarray((head[:, None] == head[None, :]) / GLA_DV, BF16)
    return wl


def _hyena_filters(wl, tabs):
    feat, win, fo, fb, _ = tabs
    kern_lin, nrm = _hy_mlp(feat, wl, win)
    return _hy_kspec(kern_lin, nrm, fo, fb)


def _trunk_layer(x, mod, mod_row, wl, batch, seq_len, hy_tabs, rope_tabs=None, ctx=None, layer=0, cache_bufs=None):
    n = batch * seq_len
    tm = 512
    if ctx is None:
        mla_in, gates, hy_in, s5_in, gla_in, q, k, v, ckv, krope = _inproj(
            x, mod, mod_row, wl, tm, seq_len, (layer, cache_bufs[:2]))
        kv_parts = [(k, v, seq_len)]
    else:
        mla_in, gates, hy_in, s5_in, gla_in = _inproj(x, mod, mod_row, wl, tm, seq_len)
        q, k, v = _mla_prep(mla_in, wl, rope_tabs, seq_len, tm, True)
        ckv = krope = None
        k_ctx, v_ctx = _mla_prep(ctx["mla"], wl, None, ctx["past"], 512, False)
        kv_parts = [(k_ctx, v_ctx, ctx["past"]), (k, v, seq_len)]
    o_mla = _attention(q, kv_parts, batch, seq_len, 256)

    o_hy = _hyena(hy_in, wl, _hyena_filters(wl, hy_tabs), hy_tabs[2], hy_tabs[4], batch, seq_len)

    nseg = seq_len // S5_SEG
    nseq = batch * nseg
    u_seg = s5_in.reshape(nseq, S5_SEG, D_BRANCH)
    if ctx is None:
        hin = jnp.zeros((2, nseq, 2 * S5_N), F32)
    else:
        (fin,) = _s5_scan(u_seg, jnp.zeros((2, nseq, 2 * S5_N), F32), wl, False)
        hin = _s5_chain(fin, ctx["s5_h0"], wl, batch, nseg)
    y2, s5_fin = _s5_scan(u_seg, hin, wl, True)

    s0 = jnp.zeros((batch, 2, GLA_HEADS * GLA_DV, GLA_HEADS * GLA_DK), F32) if ctx is None else ctx["gla_s0"]
    fin = (0, None, 1) if ctx is not None else (layer, None if cache_bufs is None else cache_bufs[2], DEPTH)
    o_gla, gla_fin = _gla(gla_in, mla_in, wl, s0, batch, seq_len, fin)

    y = _outproj(x, mod, mod_row, gates, o_mla, o_hy, s5_in, y2.reshape(2, n, D_BRANCH), o_gla, wl, 2 * tm)
    return y, (ckv, krope, s5_fin, gla_fin)


def kernel(x_prompt, x_sample, c, cache_mla_ckv, cache_mla_krope, state_s5, state_gla, c_ctx, norm_w, ada_w, ada_b, w_in, w_out, mla_qa_norm, mla_kva_norm, mla_w_uq, mla_w_ukv, mla_q_norm, mla_k_norm, hy_conv_w, hy_conv_b, hy_w1, hy_b1, hy_freq1, hy_w2, hy_b2, hy_freq2, hy_w3, hy_bias, s5_a_re, s5_a_im, s5_log_dt, s5_b_re, s5_b_im, s5_c_re, s5_c_im, s5_d, s5_glu_w, s5_glu_b, gla_gw, gla_gb, gla_norm):
    params = dict(norm_w=norm_w, w_in=w_in, w_out=w_out, mla_qa_norm=mla_qa_norm, mla_kva_norm=mla_kva_norm,
                  mla_w_uq=mla_w_uq, mla_w_ukv=mla_w_ukv, mla_q_norm=mla_q_norm, mla_k_norm=mla_k_norm,
                  hy_conv_w=hy_conv_w, hy_conv_b=hy_conv_b, hy_w1=hy_w1, hy_b1=hy_b1, hy_freq1=hy_freq1,
                  hy_w2=hy_w2, hy_b2=hy_b2, hy_freq2=hy_freq2, hy_w3=hy_w3, hy_bias=hy_bias,
                  s5_a_re=s5_a_re, s5_a_im=s5_a_im, s5_log_dt=s5_log_dt, s5_b_re=s5_b_re, s5_b_im=s5_b_im,
                  s5_c_re=s5_c_re, s5_c_im=s5_c_im, s5_d=s5_d, s5_glu_w=s5_glu_w, s5_glu_b=s5_glu_b,
                  gla_gw=gla_gw, gla_gb=gla_gb, gla_norm=gla_norm)
    bp, lp, d = x_prompt.shape
    bs, ls, _ = x_sample.shape
    past = cache_mla_ckv.shape[2]
    n_s5 = S5_N

    conds = jnp.concatenate([c_ctx[None, :], c, jnp.zeros((8 - 1 - bs, d), F32)], axis=0)
    mods = _modulation(conds, ada_w, ada_b).reshape(DEPTH, 8, 3, d)

    tabs_p = _hyena_tables(lp) + _odd_dft(lp)
    tabs_s = _hyena_tables(ls) + _odd_dft(ls)
    rope_tabs = _rope_tables(ls)
    nseg = ls // S5_SEG

    y_p = x_prompt.reshape(bp * lp, d)
    y_s = x_sample.reshape(bs * ls, d)
    s5_l = []
    cache_bufs = (jnp.zeros((bp, DEPTH, lp, MLA_KV_RANK), F32), jnp.zeros((bp, DEPTH, lp, MLA_ROPE), F32),
                  jnp.zeros((bp, DEPTH, 2, GLA_HEADS, GLA_DK, GLA_DV), F32))
    for l in range(DEPTH):
        wl = _layer_weights(l, params)
        y_p, (ckv, krope, s5_fin, gla_fin) = _trunk_layer(y_p, mods[l], lambda row: 0, wl, bp, lp, tabs_p,
                                                          layer=l, cache_bufs=cache_bufs)
        cache_bufs = (ckv, krope, gla_fin)
        s5_l.append(jnp.stack([s5_fin[:, :, :n_s5], s5_fin[:, :, n_s5:]], axis=-1)
                    .reshape(2, bp, S5_GROUPS, S5_STATE, 2).transpose(1, 0, 2, 3, 4))

        mla_ctx = jnp.concatenate([cache_mla_ckv[:, l], jnp.zeros((bs, past, 64), F32), cache_mla_krope[:, l],
                                   jnp.zeros((bs, past, 32), F32)], axis=-1).reshape(bs * past, 256)
        st = state_s5[:, l]
        h0 = jnp.concatenate([st[..., 0].reshape(bs, 2, n_s5), st[..., 1].reshape(bs, 2, n_s5)], axis=-1)
        h0 = h0.transpose(1, 0, 2)
        h0rows = jnp.zeros((2, nseg * bs, 2 * n_s5), F32)
        h0rows = h0rows.at[0, 0::nseg].set(h0[0]).at[1, nseg - 1::nseg].set(h0[1])
        eye_h = jnp.eye(GLA_HEADS, dtype=F32)
        gla_s0 = jnp.einsum("bdhke,hg->bdhegk", state_gla[:, l], eye_h).reshape(
            bs, 2, GLA_HEADS * GLA_DV, GLA_HEADS * GLA_DK)
        ctx = {"mla": mla_ctx, "past": past, "s5_h0": h0rows, "gla_s0": gla_s0}
        y_s, _ = _trunk_layer(y_s, mods[l], lambda row: 1 + row // ls, wl, bs, ls, tabs_s, rope_tabs, ctx)

    return (y_p.reshape(bp, lp, d), y_s.reshape(bs, ls, d),
            cache_bufs[0], cache_bufs[1], jnp.stack(s5_l, axis=1), cache_bufs[2])
```

```python
import functools
import math

import numpy as np
import jax
import jax.numpy as jnp
from jax import lax
from jax.experimental import pallas as pl
from jax.experimental.pallas import tpu as pltpu

F32 = jnp.float32
BF16 = jnp.bfloat16

D_MODEL = 1024
DEPTH = 2
GRID_W = 64
D_BRANCH = 256
EPS = 1e-6

MLA_HEADS = 4
MLA_Q_RANK = 192
MLA_KV_RANK = 128
MLA_NOPE = 64
MLA_ROPE = 32
MLA_QK = 96
MLA_V = 64
ROPE_BASE = 10000.0
HEAD_PAD = 128

HY_BANDS = 16
HY_FEAT = 33
HY_HIDDEN = 64
HY_SHIFT = 0.05
HY_FAST_DECAY = 0.3
HY_SLOW_DECAY = 1.5
HY_TARGET = 1e-2
HY_BLOCK = 512

S5_GROUP = 16
S5_GROUPS = 16
S5_STATE = 64
S5_N = S5_GROUPS * S5_STATE
S5_ROWS = 512
S5_SEG = 256

GLA_HEADS = 4
GLA_DK = 32
GLA_DV = 64
GLA_RANK = 16
GLA_TAU = 16.0
GLA_CHUNK = 64
GLA_SUPER = 256

SEG_MLA = (0, 384)
SEG_GATE = (384, 1408)
SEG_HY = (1408, 2176)
SEG_S5 = (2176, 2432)
SEG_GLA = (2432, 2944)
N_PROJ = 2944
W_IN_ORDER = ((0, 192), (320, 352), (2656, 2688), (192, 320),
              (352, 608), (1376, 1632), (1888, 2144), (2688, 2944),
              (608, 1376), (1632, 1888), (2144, 2656))
GLA_G_LANE = 96

VMEM_LIMIT = 48 * 1024 * 1024


def _cp(*sem):
    return pltpu.CompilerParams(dimension_semantics=sem, vmem_limit_bytes=VMEM_LIMIT)


def _dot(a, b):
    return jnp.dot(a, b, preferred_element_type=F32)


def _dot_nt(a, b):
    return lax.dot_general(a, b, (((1,), (1,)), ((), ())), preferred_element_type=F32)


def _dot_tn(a, b):
    return lax.dot_general(a, b, (((0,), (0,)), ((), ())), preferred_element_type=F32)


def _split2(x):
    hi = x.astype(BF16)
    lo = (x - hi.astype(F32)).astype(BF16)
    return hi, lo


def _split3(x):
    h1 = x.astype(BF16)
    r1 = x - h1.astype(F32)
    h2 = r1.astype(BF16)
    h3 = (r1 - h2.astype(F32)).astype(BF16)
    return h1, h2, h3


def _dot3(a, b):
    a1, a2 = _split2(a)
    b1, b2 = _split2(b)
    return _dot(a1, b1) + (_dot(a1, b2) + _dot(a2, b1))


def _silu(z):
    return z / (1.0 + jnp.exp(-z))


def _mod_kernel(c_ref, w_ref, b_ref, o_ref):
    s = _silu(c_ref[...])
    o_ref[0] = _dot(s.astype(BF16), w_ref[0].astype(BF16)) + b_ref[0]


def _modulation(conds, ada_w, ada_b):
    d = D_MODEL
    return pl.pallas_call(
        _mod_kernel,
        grid=(DEPTH, 3),
        in_specs=[pl.BlockSpec((8, d), lambda l, j: (0, 0)),
                  pl.BlockSpec((1, d, d), lambda l, j: (l, 0, j)),
                  pl.BlockSpec((1, 1, d), lambda l, j: (l, 0, j))],
        out_specs=pl.BlockSpec((1, 8, d), lambda l, j: (l, 0, j)),
        out_shape=jax.ShapeDtypeStruct((DEPTH, 8, 3 * d), F32),
        compiler_params=_cp("arbitrary", "arbitrary"),
        name="modulation",
    )(conds, ada_w, ada_b.reshape(DEPTH, 1, 3 * d))


def _head_norm(xh, w):
    ms = jnp.sum(xh * xh, axis=-1, keepdims=True) * (1.0 / MLA_QK)
    return xh * lax.rsqrt(ms + EPS) * w


def _rope(xh, cos, sin_a, sin_b):
    return xh * cos + pltpu.roll(xh, HEAD_PAD - 8, 1) * sin_a + pltpu.roll(xh, 8, 1) * sin_b


def _mla_steps(m, has_q, rope, cache_seqs, n_aliased, refs):
    refs = list(refs)
    if has_q:
        qan_ref, wuq_ref, qn_ref, kvn_ref = refs[:4]
        refs = refs[4:]
    wuk_ref, wuv_ref, kn_ref = refs[:3]
    refs = refs[3:]
    if rope:
        cos_ref, sa_ref, sb_ref = refs[:3]
        refs = refs[3:]
        cos, sa, sb = cos_ref[...], sa_ref[...], sb_ref[...]
    refs = refs[n_aliased:]
    if has_q:
        q_ref = refs.pop(0)
    k_ref, v_ref = refs[:2]
    if cache_seqs:
        ckv_ref, kro_ref = refs[2:]
    if has_q:
        lane = lax.broadcasted_iota(jnp.int32, (1, HEAD_PAD), 1)
        mixed = m[:, 128:256]
        cq = jnp.concatenate([m[:, 0:128], jnp.where(lane < MLA_NOPE, mixed, 0.0)], axis=1)
        ms = jnp.sum(cq * cq, axis=-1, keepdims=True) * (1.0 / MLA_Q_RANK)
        cqn = cq * lax.rsqrt(ms + EPS) * qan_ref[...]
        ckv = m[:, 256:384]
        ckvn = ckv * lax.rsqrt(jnp.mean(ckv * ckv, axis=-1, keepdims=True) + EPS) * kvn_ref[...]
        yield
        q = _dot(cqn.astype(BF16), wuq_ref[...])
        kr = jnp.where(jnp.logical_and(lane >= MLA_NOPE, lane < MLA_NOPE + MLA_ROPE), mixed, 0.0)
        if cache_seqs:
            seq_len = ckv_ref.shape[2]
            for s in range(cache_seqs):
                ckv_ref[s, 0] = ckvn[s * seq_len:(s + 1) * seq_len]
                kro_ref[s, 0] = kr[s * seq_len:(s + 1) * seq_len, MLA_NOPE:MLA_NOPE + MLA_ROPE]
    else:
        ckvn = m[:, 0:128]
        kr = m[:, 128:256]
    cb = ckvn.astype(BF16)
    kup = _dot(cb, wuk_ref[...])
    v_ref[...] = _dot(cb, wuv_ref[...]).astype(BF16)
    yield
    for h in range(MLA_HEADS):
        sl = slice(HEAD_PAD * h, HEAD_PAD * (h + 1))
        kh = _head_norm(kup[:, sl] + kr, kn_ref[...])
        if rope:
            kh = _rope(kh, cos, sa, sb)
        k_ref[:, sl] = kh.astype(BF16)
        if has_q:
            qh = _head_norm(q[:, sl], qn_ref[...])
            if rope:
                qh = _rope(qh, cos, sa, sb)
            q_ref[:, sl] = (qh * (MLA_QK ** -0.5)).astype(BF16)
        if h % 2 == 1:
            yield


def _mla_prep_kernel(has_q, rope, m_ref, *refs):
    for _ in _mla_steps(m_ref[...], has_q, rope, 0, 0, refs):
        pass


def _inproj_kernel(fuse_mla, cache_seqs, n_aliased, x_ref, mod_ref, nw_ref, w_ref, *refs):
    n_mla_in = 7 + n_aliased if fuse_mla else 0
    mla_refs, (o_mla, o_g, o_hy, o_s5, o_gla) = refs[:n_mla_in], refs[n_mla_in:n_mla_in + 5]
    w_src, w_ref = w_ref, refs[-1]

    @pl.when(pl.program_id(0) == 0)
    def _():
        dst = 0
        for lo, hi in W_IN_ORDER:
            w_ref[:, dst:dst + hi - lo] = w_src[0, :, lo:hi].astype(BF16)
            dst += hi - lo

    x = x_ref[...]
    ms = jnp.mean(x * x, axis=-1, keepdims=True)
    y = x * lax.rsqrt(ms + EPS) * nw_ref[...]
    h = (y * (1.0 + mod_ref[0, 1:2, :]) + mod_ref[0, 0:1, :]).astype(BF16)
    project = lambda o, seg: o.__setitem__(Ellipsis, _dot(h, w_ref[:, seg[0]:seg[1]]).astype(o.dtype))
    m = _dot(h, w_ref[:, SEG_MLA[0]:SEG_MLA[1]])
    o_mla[...] = m
    steps = iter(()) if not fuse_mla else _mla_steps(
        m, True, False, cache_seqs, n_aliased, list(mla_refs) + list(refs[n_mla_in + 5:-1]))
    project(o_g, SEG_GATE)
    next(steps, None)
    next(steps, None)
    project(o_hy, SEG_HY)
    next(steps, None)
    project(o_s5, SEG_S5)
    project(o_gla, SEG_GLA)
    for _ in steps:
        pass


def _inproj(x, mod, mod_row, wl, tm, seq_len, cache=None):
    n, d = x.shape
    fuse_mla = cache is not None
    full = lambda shape: pl.BlockSpec(shape, lambda i: (0,) * len(shape))
    row = lambda w: pl.BlockSpec((tm, w), lambda i: (i, 0))
    args = [x, mod, wl["norm_w"], wl["w_in"]]
    specs = [row(d), pl.BlockSpec((1, 3, d), lambda i: (mod_row(i * tm), 0, 0)), full((1, d)),
             pl.BlockSpec((1, d, N_PROJ), lambda i: (wl["layer"], 0, 0), pipeline_mode=pl.Buffered(1))]
    widths = [hi - lo for lo, hi in (SEG_MLA, SEG_GATE, SEG_HY, SEG_S5, SEG_GLA)]
    dtypes = [F32, BF16, F32, F32, F32]
    aliases = {}
    nseq = 0
    if fuse_mla:
        args += [wl["qa_norm"], wl["w_uq"], wl["q_norm"], wl["kva_norm"], wl["w_uk"], wl["w_uv"], wl["k_norm"]]
        specs += [full((1, 256)), full((256, 512)), full((1, 128)), full((1, 128)),
                  full((128, 512)), full((128, 256)), full((1, 128))]
        widths += [512, 512, 256]
        dtypes += [BF16, BF16, BF16]
    out_specs = [row(w) for w in widths]
    out_shape = [jax.ShapeDtypeStruct((n, w), t) for w, t in zip(widths, dtypes)]
    if fuse_mla:
        layer, prev = cache
        nseq = tm // seq_len
        for w in (MLA_KV_RANK, MLA_ROPE):
            out_specs.append(pl.BlockSpec((nseq, 1, seq_len, w), lambda i: (i, layer, 0, 0)))
            out_shape.append(jax.ShapeDtypeStruct((n // seq_len, DEPTH, seq_len, w), F32))
        for k, buf in enumerate(prev):
            aliases[len(args)] = len(out_shape) - 2 + k
            args.append(buf)
            specs.append(pl.BlockSpec(memory_space=pl.ANY))
    return pl.pallas_call(
        functools.partial(_inproj_kernel, fuse_mla, nseq, len(aliases)),
        grid=(n // tm,),
        in_specs=specs, out_specs=out_specs, out_shape=out_shape,
        input_output_aliases=aliases,
        scratch_shapes=[pltpu.VMEM((d, N_PROJ), BF16)],
        compiler_params=_cp("arbitrary"),
        name="inproj",
    )(*args)


def _mla_prep(m, wl, rope_tabs, seq_len, tm, has_q):
    n, wm = m.shape
    rope = rope_tabs is not None
    full = lambda shape: pl.BlockSpec(shape, lambda i: (0,) * len(shape))
    row = lambda w: pl.BlockSpec((tm, w), lambda i: (i, 0))
    args, specs = [m], [row(wm)]
    if has_q:
        args += [wl["qa_norm"], wl["w_uq"], wl["q_norm"], wl["kva_norm"]]
        specs += [full((1, 256)), full((256, 512)), full((1, 128)), full((1, 128))]
    args += [wl["w_uk"], wl["w_uv"], wl["k_norm"]]
    specs += [full((128, 512)), full((128, 256)), full((1, 128))]
    if rope:
        nt = seq_len // tm
        args += list(rope_tabs)
        specs += [pl.BlockSpec((tm, HEAD_PAD), lambda i: (i % nt, 0))] * 3
    widths = ([512] if has_q else []) + [512, 256]
    return pl.pallas_call(
        functools.partial(_mla_prep_kernel, has_q, rope),
        grid=(n // tm,),
        in_specs=specs, out_specs=[row(w) for w in widths],
        out_shape=[jax.ShapeDtypeStruct((n, w), BF16) for w in widths],
        compiler_params=_cp("arbitrary"),
        name="mla_prep",
    )(*args)


def _attn_kernel(nparts, nseq, q_ref, *refs):
    kv = [(refs[2 * i], refs[2 * i + 1]) for i in range(nparts)]
    o_ref = refs[2 * nparts]
    tq = q_ref.shape[0] // nseq
    low = lax.broadcasted_iota(jnp.int32, (1, HEAD_PAD), 1) < MLA_V
    units = [(s, h) for s in range(nseq) for h in range(MLA_HEADS)]

    def keys(ref, s):
        lk = ref.shape[0] // nseq
        return slice(s * lk, (s + 1) * lk)

    def scores(s, h):
        sl = slice(HEAD_PAD * h, HEAD_PAD * (h + 1))
        return [_dot_nt(q_ref[s * tq:(s + 1) * tq, sl], k_ref[keys(k_ref, s), sl]) for k_ref, _ in kv]

    s_next = scores(*units[0])
    acc = None
    for n, (s, h) in enumerate(units):
        pair, j = divmod(h, 2)
        sc = s_next
        if n + 1 < len(units):
            s_next = scores(*units[n + 1])
        if j == 0:
            v_half = []
            for _, v_ref in kv:
                vp = v_ref[keys(v_ref, s), HEAD_PAD * pair:HEAD_PAD * (pair + 1)]
                zero = jnp.zeros_like(vp)
                v_half.append((jnp.where(low, vp, zero), jnp.where(low, zero, vp)))
        m = functools.reduce(jnp.maximum, [jnp.max(x, axis=-1, keepdims=True) for x in sc])
        p = [jnp.exp(x - m) for x in sc]
        den = functools.reduce(jnp.add, [jnp.sum(x, axis=-1, keepdims=True) for x in p])
        num = functools.reduce(jnp.add, [_dot(x.astype(BF16), vh[j]) for x, vh in zip(p, v_half)])
        o = num / den
        acc = o if j == 0 else acc + o
        if j == 1:
            o_ref[s * tq:(s + 1) * tq, HEAD_PAD * pair:HEAD_PAD * (pair + 1)] = acc.astype(BF16)


def _attention(q, kv_parts, batch, lq, tq):
    nq = lq // tq
    nseq = max(1, min(batch, 1024 // lq)) if nq == 1 else 1
    args, specs = [q], [pl.BlockSpec((nseq * tq, 512), lambda b, i: (b * nq + i, 0))]
    for k, v, lk in kv_parts:
        args += [k, v]
        specs += [pl.BlockSpec((nseq * lk, 512), lambda b, i: (b, 0)),
                  pl.BlockSpec((nseq * lk, 256), lambda b, i: (b, 0))]
    return pl.pallas_call(
        functools.partial(_attn_kernel, len(kv_parts), nseq),
        grid=(batch // nseq, nq),
        in_specs=specs,
        out_specs=pl.BlockSpec((nseq * tq, 256), lambda b, i: (b * nq + i, 0)),
        out_shape=jax.ShapeDtypeStruct((batch * lq, 256), BF16),
        compiler_params=_cp("arbitrary", "arbitrary"),
        name="attention",
    )(*args)


def _hyena_kernel(seq_len, bk, nseq, x_ref, cw_ref, cb_ref, fo_ref, go_ref, k_ref, bias_ref, o_ref,
                  u_sc, y_sc, z_sc):
    c = D_BRANCH
    n = nseq * seq_len
    nblk = seq_len // bk
    pos = jnp.bitwise_and(lax.broadcasted_iota(jnp.int32, (n, 1), 0), seq_len - 1)
    first, last = pos == 0, pos == seq_len - 1

    def short_conv(g):
        cols = slice(g * c, (g + 1) * c)
        x = x_ref[:, cols]
        xm = jnp.where(first, 0.0, pltpu.roll(x, 1, 0))
        xp = jnp.where(last, 0.0, pltpu.roll(x, n - 1, 0))
        return cw_ref[0:1, cols] * xm + cw_ref[1:2, cols] * x + cw_ref[2:3, cols] * xp + cb_ref[:, cols]

    fo, go = fo_ref[...], go_ref[...]

    def long_conv(s, v, order, emit):
        cols = slice(order * c, (order + 1) * c)
        for j in range(nblk):
            u_sc[s, j] = _dot(fo, v[j * bk:(j + 1) * bk].astype(BF16))
        yield
        rc = 32
        for i in range(nblk):
            def mix(r, carry, i=i):
                top = pl.ds(pl.multiple_of(r * rc, rc), rc)
                bot = pl.ds(pl.multiple_of(bk + r * rc, rc), rc)
                at = ab = None
                for j in range(nblk):
                    q = i - j + nblk - 1
                    kt, kb = k_ref[q, top, cols], k_ref[q, bot, cols]
                    ut, ub = u_sc[s, j, top, :], u_sc[s, j, bot, :]
                    pt, pb = ut * kt - ub * kb, ut * kb + ub * kt
                    at, ab = (pt, pb) if at is None else (at + pt, ab + pb)
                z_sc[s, top, :] = at.astype(BF16)
                z_sc[s, bot, :] = ab.astype(BF16)
                return carry

            lax.fori_loop(0, bk // rc, mix, 0, unroll=True if nblk == 1 else 2)
            y = _dot(go, z_sc[s])
            yield
            emit(i, y)

    v_all, x1_all, x2_all = short_conv(0), short_conv(1), short_conv(2)

    def sequence(s):
        base = s * seq_len
        v = v_all[base:base + seq_len]

        def emit1(i, y):
            r = slice(i * bk, (i + 1) * bk)
            y_sc[s, r, :] = x1_all[base + i * bk:base + (i + 1) * bk] * (y + bias_ref[0:1, :] * v[r])

        yield from long_conv(s, v, 0, emit1)
        y1 = y_sc[s]

        def emit2(i, y):
            r = slice(i * bk, (i + 1) * bk)
            o_ref[base + i * bk:base + (i + 1) * bk, :] = (
                x2_all[base + i * bk:base + (i + 1) * bk] * (y + bias_ref[1:2, :] * y1[r])).astype(BF16)

        yield from long_conv(s, y1, 1, emit2)

    live = [sequence(s) for s in range(nseq)]
    while live:
        for g in list(live):
            if next(g, StopIteration) is StopIteration:
                live.remove(g)


def _hyena(x, wl, kspec, fo, go, batch, seq_len):
    c = D_BRANCH
    bk = fo.shape[1]
    nseq = max(1, min(batch, 1024 // seq_len))
    rows = nseq * seq_len
    nblk = seq_len // bk
    full = lambda a: pl.BlockSpec(a.shape, lambda i: (0,) * a.ndim)
    return pl.pallas_call(
        functools.partial(_hyena_kernel, seq_len, bk, nseq),
        grid=(batch // nseq,),
        in_specs=[pl.BlockSpec((rows, 3 * c), lambda i: (i, 0)),
                  full(wl["hy_conv_w"]), full(wl["hy_conv_b"]), full(fo), full(go),
                  pl.BlockSpec(kspec.shape, lambda i: (0, 0, 0), pipeline_mode=pl.Buffered(1)),
                  full(wl["hy_bias"])],
        out_specs=pl.BlockSpec((rows, c), lambda i: (i, 0)),
        out_shape=jax.ShapeDtypeStruct((batch * seq_len, c), BF16),
        scratch_shapes=[pltpu.VMEM((nseq, nblk, 2 * bk, c), F32), pltpu.VMEM((nseq, seq_len, c), F32),
                        pltpu.VMEM((nseq, 2 * bk, c), BF16)],
        compiler_params=pltpu.CompilerParams(dimension_semantics=("arbitrary",), vmem_limit_bytes=56 * 1024 * 1024),
        name="hyena",
    )(x, wl["hy_conv_w"], wl["hy_conv_b"], fo, go, kspec, wl["hy_bias"])


def _hy_mlp_kernel(feat_ref, w1_ref, b1_ref, f1_ref, w2_ref, b2_ref, f2_ref, w3_ref, win_ref,
                   kern_ref, nrm_ref):
    i = pl.program_id(0)
    tl = win_ref.shape[0]
    h = jnp.sin(f1_ref[...] * (_dot3(feat_ref[...], w1_ref[...]) + b1_ref[...]))
    h = jnp.sin(f2_ref[...] * (_dot3(h, w2_ref[...]) + b2_ref[...]))
    filt = _dot3(h, w3_ref[...])
    filt = jnp.concatenate([filt[:, 0:512], filt[:, 512:1024]], axis=0)
    win = win_ref[...]
    row0 = (lax.broadcasted_iota(jnp.int32, (tl, 1), 0) + i * tl) == 0
    filt = jnp.where(row0, 0.0, filt * jnp.concatenate([win, win], axis=1))
    kern_ref[...] = filt
    part = jnp.sum(jnp.abs(filt), axis=0, keepdims=True)

    @pl.when(i == 0)
    def _():
        nrm_ref[...] = jnp.zeros_like(nrm_ref)
    nrm_ref[...] += jnp.broadcast_to(part, nrm_ref.shape)


def _hy_mlp(feat, wl, win):
    rows = win.shape[0]
    tl = 256
    nl = rows // (2 * tl)
    full = lambda shape: pl.BlockSpec(shape, lambda i: (0,) * len(shape))
    return pl.pallas_call(
        _hy_mlp_kernel,
        grid=(rows // tl,),
        in_specs=[pl.BlockSpec((tl // 2, 128), lambda i: (i, 0)),
                  full((128, 128)), full((1, 128)), full((1, 128)),
                  full((128, 128)), full((1, 128)), full((1, 128)),
                  pl.BlockSpec((128, 1024), lambda i: (0, jnp.where(i < nl, 1, 0))),
                  pl.BlockSpec((tl, 256), lambda i: (i, 0))],
        out_specs=[pl.BlockSpec((tl, 512), lambda i: (i, 0)), full((8, 512))],
        out_shape=[jax.ShapeDtypeStruct((rows, 512), F32), jax.ShapeDtypeStruct((8, 512), F32)],
        compiler_params=_cp("arbitrary"),
        name="hy_mlp",
    )(feat, wl["hy_w1"], wl["hy_b1"], wl["hy_f1"], wl["hy_w2"], wl["hy_b2"], wl["hy_f2"], wl["hy_w3"], win)


def _hy_kspec_kernel(lo_ref, hi_ref, fo_ref, fb_ref, n_ref, o_ref):
    bk = lo_ref.shape[0]
    k = _dot(fo_ref[...], hi_ref[...].astype(BF16)) + _dot(fb_ref[...], lo_ref[...].astype(BF16))
    o_ref[0] = k * ((1.0 / bk) / n_ref[0:1, :])


def _hy_kspec(kern_lin, nrm, fo, fb):
    n2, bk = fo.shape
    nq = kern_lin.shape[0] // bk - 1
    full = lambda a: pl.BlockSpec(a.shape, lambda q: (0,) * a.ndim)
    return pl.pallas_call(
        _hy_kspec_kernel,
        grid=(nq,),
        in_specs=[pl.BlockSpec((bk, 512), lambda q: (q, 0)),
                  pl.BlockSpec((bk, 512), lambda q: (q + 1, 0)),
                  full(fo), full(fb), full(nrm)],
        out_specs=pl.BlockSpec((1, n2, 512), lambda q: (q, 0, 0)),
        out_shape=jax.ShapeDtypeStruct((nq, n2, 512), F32),
        compiler_params=_cp("arbitrary"),
        name="hy_kspec",
    )(kern_lin, kern_lin, fo, fb, nrm)


def _s5_discretise(are_ref, aim_ref, ldt_ref):
    ar = jnp.minimum(are_ref[0], -1e-4)
    ai = aim_ref[0]
    dt = jnp.exp(ldt_ref[0])
    e = jnp.exp(ar * dt)
    return ar, ai, e * jnp.cos(ai * dt), e * jnp.sin(ai * dt)


def _s5_scan_kernel(nseq, emit_y, u_ref, hin_ref, are_ref, aim_ref, ldt_ref, bre_ref, bim_ref, *rest):
    if emit_y:
        cre_ref, cim_ref, y_ref, hfin_ref, wb_sc, ab_sc, s_sc, hc_sc, perm_sc, wc_sc = rest
    else:
        hfin_ref, wb_sc, ab_sc, s_sc, hc_sc, perm_sc = rest
    d = pl.program_id(0)
    c = pl.program_id(1)
    n = S5_N

    @pl.when(c == 0)
    def _():
        ar, ai, abr, abi = _s5_discretise(are_ref, aim_ref, ldt_ref)
        ab_sc[0:1, :] = abr
        ab_sc[1:2, :] = abi
        den = 1.0 / (ar * ar + ai * ai)
        cr = ((abr - 1.0) * ar + abi * ai) * den
        ci = (abi * ar - (abr - 1.0) * ai) * den
        grp_rows = lax.shift_right_logical(lax.broadcasted_iota(jnp.int32, (D_BRANCH, 1), 0), 4)
        grp_cols = lax.shift_right_logical(lax.broadcasted_iota(jnp.int32, (1, n), 1), 6)
        expand_b = lambda ref: jnp.where(grp_rows == grp_cols, jnp.concatenate([ref[0]] * (n // 128), axis=1), 0.0)
        bre, bim = expand_b(bre_ref), expand_b(bim_ref)
        wb_sc[:, 0:n] = (cr * bre - ci * bim).astype(BF16)
        wb_sc[:, n:2 * n] = (cr * bim + ci * bre).astype(BF16)
        if emit_y:
            st_rows = lax.shift_right_logical(lax.broadcasted_iota(jnp.int32, (n, 1), 0), 6)
            ch_cols = lax.shift_right_logical(lax.broadcasted_iota(jnp.int32, (1, D_BRANCH), 1), 4)
            expand_c = lambda ref: jnp.where(st_rows == ch_cols, jnp.concatenate([ref[0]] * 2, axis=1), 0.0)
            wc_sc[0:n, :] = expand_c(cre_ref).astype(BF16)
            wc_sc[n:2 * n, :] = (-expand_c(cim_ref)).astype(BF16)
        hc_sc[...] = hin_ref[0]

    steps = u_ref.shape[1]
    rows_c = nseq * steps

    @pl.when(c == 0)
    def _():
        i = lax.broadcasted_iota(jnp.int32, (rows_c, rows_c), 0)
        j = lax.broadcasted_iota(jnp.int32, (rows_c, rows_c), 1)
        p = lax.shift_right_logical(i, int(math.log2(nseq)))
        step = p + d * (steps - 1 - 2 * p)
        src = jnp.bitwise_and(i, nseq - 1) * steps + step
        perm_sc[...] = jnp.where(j == src, 1.0, 0.0).astype(BF16)

    lhs = _dot(perm_sc[...], u_ref[...].reshape(rows_c, D_BRANCH).astype(BF16)).astype(BF16)
    lb = 256
    y = None
    for j in range(n // lb):
        lr = slice(lb * j, lb * (j + 1))
        li = slice(n + lb * j, n + lb * (j + 1))
        bur = _dot(lhs, wb_sc[:, lr])
        bui = _dot(lhs, wb_sc[:, li])
        abr = ab_sc[0:1, lr]
        abi = ab_sc[1:2, lr]
        hr, hi = hc_sc[:, lr], hc_sc[:, li]
        for p in range(steps):
            rows = slice(p * nseq, (p + 1) * nseq)
            hr, hi = abr * hr - abi * hi + bur[rows], abr * hi + abi * hr + bui[rows]
            if emit_y:
                s_sc[rows, lr] = hr.astype(BF16)
                s_sc[rows, li] = hi.astype(BF16)
        hc_sc[:, lr] = hr
        hc_sc[:, li] = hi
        if emit_y:
            yj = _dot(s_sc[:, lr], wc_sc[lr, :]) + _dot(s_sc[:, li], wc_sc[li, :])
            y = yj if y is None else y + yj

    if emit_y:
        @pl.when(d == 0)
        def _():
            for p in range(steps):
                y_ref[0, :, p, :] = y[p * nseq:(p + 1) * nseq]

        @pl.when(d == 1)
        def _():
            for p in range(steps):
                y_ref[0, :, steps - 1 - p, :] = y[p * nseq:(p + 1) * nseq]

    @pl.when(c == pl.num_programs(1) - 1)
    def _():
        hfin_ref[0] = hc_sc[...]


def _s5_scan(u, hin, wl, emit_y):
    nseq, nstep, _ = u.shape
    steps = S5_ROWS // nseq
    nc = nstep // steps
    n = S5_N
    chunk = lambda d, c: c + d * (nc - 1 - 2 * c)
    per_dir = lambda shape: pl.BlockSpec((1,) + shape, lambda d, c: (d,) + (0,) * len(shape))
    args = [u, hin, wl["s5_are"], wl["s5_aim"], wl["s5_ldt"], wl["s5_bre"], wl["s5_bim"]]
    specs = [pl.BlockSpec((nseq, steps, D_BRANCH), lambda d, c: (0, chunk(d, c), 0)),
             per_dir((nseq, 2 * n)), per_dir((1, n)), per_dir((1, n)), per_dir((1, n)),
             per_dir((D_BRANCH, 128)), per_dir((D_BRANCH, 128))]
    out_specs = [per_dir((nseq, 2 * n))]
    out_shape = [jax.ShapeDtypeStruct((2, nseq, 2 * n), F32)]
    scratch = [pltpu.VMEM((D_BRANCH, 2 * n), BF16), pltpu.VMEM((8, n), F32),
               pltpu.VMEM((S5_ROWS, 2 * n), BF16), pltpu.VMEM((nseq, 2 * n), F32),
               pltpu.VMEM((S5_ROWS, S5_ROWS), BF16)]
    if emit_y:
        args += [wl["s5_cre"], wl["s5_cim"]]
        specs += [per_dir((n, 128)), per_dir((n, 128))]
        out_specs = [pl.BlockSpec((1, nseq, steps, D_BRANCH), lambda d, c: (d, 0, chunk(d, c), 0))] + out_specs
        out_shape = [jax.ShapeDtypeStruct((2, nseq, nstep, D_BRANCH), F32)] + out_shape
        scratch += [pltpu.VMEM((2 * n, D_BRANCH), BF16)]
    return pl.pallas_call(
        functools.partial(_s5_scan_kernel, nseq, emit_y),
        grid=(2, nc),
        in_specs=specs, out_specs=out_specs, out_shape=out_shape, scratch_shapes=scratch,
        compiler_params=_cp("arbitrary", "arbitrary"),
        name="s5_scan" if emit_y else "s5_scan_finals",
    )(*args)


def _s5_chain_kernel(batch, nseg, f_ref, h0_ref, are_ref, aim_ref, ldt_ref, o_ref):
    d = pl.program_id(0)
    n = S5_N
    _, _, pr, pi = _s5_discretise(are_ref, aim_ref, ldt_ref)
    for _ in range(int(math.log2(S5_SEG))):
        pr, pi = pr * pr - pi * pi, 2.0 * pr * pi
    f = f_ref[0]
    fr, fi = f[:, 0:n], f[:, n:2 * n]
    h0 = h0_ref[0]
    h0r, h0i = h0[:, 0:n], h0[:, n:2 * n]
    nrow = batch * nseg
    seg = jnp.bitwise_and(lax.broadcasted_iota(jnp.int32, (nrow, 1), 0), nseg - 1)

    def run(shift, keep):
        xr, xi = h0r, h0i
        for _ in range(nseg - 1):
            zr = fr + pr * xr - pi * xi
            zi = fi + pr * xi + pi * xr
            xr = h0r + jnp.where(keep, pltpu.roll(zr, shift, 0), 0.0)
            xi = h0i + jnp.where(keep, pltpu.roll(zi, shift, 0), 0.0)
        o_ref[0, :, 0:n] = xr
        o_ref[0, :, n:2 * n] = xi

    @pl.when(d == 0)
    def _():
        run(1, seg != 0)

    @pl.when(d == 1)
    def _():
        run(nrow - 1, seg != nseg - 1)


def _s5_chain(fin, h0rows, wl, batch, nseg):
    nrow = batch * nseg
    n = S5_N
    per_dir = lambda shape: pl.BlockSpec((1,) + shape, lambda d: (d,) + (0,) * len(shape))
    return pl.pallas_call(
        functools.partial(_s5_chain_kernel, batch, nseg),
        grid=(2,),
        in_specs=[per_dir((nrow, 2 * n)), per_dir((nrow, 2 * n)), per_dir((1, n)), per_dir((1, n)), per_dir((1, n))],
        out_specs=per_dir((nrow, 2 * n)),
        out_shape=jax.ShapeDtypeStruct((2, nrow, 2 * n), F32),
        compiler_params=_cp("arbitrary"),
        name="s5_chain",
    )(fin, h0rows, wl["s5_are"], wl["s5_aim"], wl["s5_ldt"])


def _gla_kernel(seq_len, nb, n_aliased, q_ref, k_ref, v_ref, g_ref, gw_ref, gb_ref, s0_ref, *rest):
    o_ref, sfin_ref, qe_sc, upd_sc, dec_sc, sall_sc, lhs_sc, kt_sc, la_sc, oi_sc = rest[n_aliased:]
    d = pl.program_id(1)
    sign = 1 - 2 * d
    ck, sup = GLA_CHUNK, GLA_SUPER
    cps = sup // ck
    nsup, nchunk = seq_len // sup, seq_len // ck
    dk, dv = GLA_HEADS * GLA_DK, GLA_HEADS * GLA_DV
    r = lax.broadcasted_iota(jnp.int32, (sup, sup), 0)
    s = lax.broadcasted_iota(jnp.int32, (sup, sup), 1)
    same = lax.shift_right_logical(r, 6) == lax.shift_right_logical(s, 6)
    tri = jnp.logical_and(same, (s - r) * sign <= 0)
    cum_lhs = jnp.where(tri, 1.0, 0.0).astype(BF16)
    t4 = lax.broadcasted_iota(jnp.int32, (ck, GLA_HEADS * ck), 0)
    s4 = jnp.bitwise_and(lax.broadcasted_iota(jnp.int32, (ck, GLA_HEADS * ck), 1), ck - 1)
    tri4 = (s4 - t4) * sign <= 0
    pos = jnp.bitwise_and(lax.broadcasted_iota(jnp.int32, (ck, 1), 0), ck - 1)
    is_last = pos == (ck - 1) * (1 - d)
    row_chunk = lax.shift_right_logical(lax.broadcasted_iota(jnp.int32, (sup, 1), 0), 6)
    head_k = lax.shift_right_logical(lax.broadcasted_iota(jnp.int32, (1, dk), 1), 5)
    head_v = lax.shift_right_logical(lax.broadcasted_iota(jnp.int32, (1, dv), 1), 6)
    blockdiag = lax.shift_right_logical(lax.broadcasted_iota(jnp.int32, (dv, 1), 0), 6) == head_k

    def group_rows(u):
        return pl.ds(u * sup, sup) if isinstance(u, int) else pl.ds(pl.multiple_of(u * sup, sup), sup)

    def stage_a(u, slot):
        rows = group_rows(u)
        q = q_ref[rows, :] * (GLA_DK ** -0.5)
        k = k_ref[rows, :]
        v = v_ref[rows, :]
        cs = _dot(cum_lhs, la_sc[rows, :])
        yield
        bc = cs[:, 0:dk] + cs[:, dk:2 * dk]
        tots = [jnp.sum(jnp.where(is_last, bc[c * ck:(c + 1) * ck], 0.0), axis=0, keepdims=True)
                for c in range(cps)]
        tot = jnp.concatenate([jnp.broadcast_to(t, (ck, dk)) for t in tots], axis=0)
        ref = 0.5 * tot
        kt_sc[slot] = (k * jnp.exp(ref - bc)).astype(BF16)
        lhs_sc[slot] = (q * jnp.exp(bc - ref)).astype(BF16)
        qe_sc[rows, :] = (q * jnp.exp(bc)).astype(BF16)
        kl = (k * jnp.exp(tot - bc)).astype(BF16)
        zero = jnp.zeros_like(kl)
        klx = jnp.concatenate([jnp.where(row_chunk == c, kl, zero) for c in range(cps)], axis=1)
        upd = _dot_tn(v.astype(BF16), klx)
        yield
        for c in range(cps):
            upd_sc[u * cps + c] = jnp.where(blockdiag, upd[:, c * dk:(c + 1) * dk], 0.0)
            dec_sc[u * cps + c] = jnp.broadcast_to(jnp.exp(tots[c]), (8, dk))

    def stage_b(u, slot):
        rows = group_rows(u)
        v = v_ref[rows, :].astype(BF16)
        qt, kt = lhs_sc[slot], kt_sc[slot]
        zk, zv = jnp.zeros_like(kt[0:ck]), jnp.zeros_like(v[0:ck])
        p = [_dot_nt(qt[c * ck:(c + 1) * ck],
                     jnp.concatenate([jnp.where(head_k == h, kt[c * ck:(c + 1) * ck], zk)
                                      for h in range(GLA_HEADS)], axis=0)) for c in range(cps)]
        yield
        o = [_dot(jnp.where(tri4, p[c], 0.0).astype(BF16),
                  jnp.concatenate([jnp.where(head_v == h, v[c * ck:(c + 1) * ck], zv)
                                   for h in range(GLA_HEADS)], axis=0)) for c in range(cps)]
        yield
        oi_sc[rows, :] = jnp.concatenate(o, axis=0)

    def run(*stages):
        live = list(stages)
        while live:
            for g in list(live):
                if next(g, StopIteration) is StopIteration:
                    live.remove(g)

    x = _dot(g_ref[...].astype(BF16), gw_ref[0]) + gb_ref[0]
    la = (jnp.minimum(x, 0.0) - jnp.log(1.0 + jnp.exp(-jnp.abs(x)))) * (1.0 / GLA_TAU)
    la_sc[...] = jnp.concatenate(_split2(la), axis=1)

    nu = nb * nsup
    if nu <= 4:
        run(*[stage_a(u, u) for u in range(min(2, nu))])
        for u in range(0, nu, 2):
            run(*([stage_b(v, v % 4) for v in range(u, min(u + 2, nu))]
                  + [stage_a(v, v % 4) for v in range(u + 2, min(u + 4, nu))]))
    else:
        run(stage_a(0, 0), stage_a(1, 1))

        def sup_body(t, carry):
            u = 2 * t
            n0, n1 = jnp.minimum(u + 2, nu - 2), jnp.minimum(u + 3, nu - 1)
            run(stage_b(u, jnp.bitwise_and(u, 3)), stage_b(u + 1, jnp.bitwise_and(u + 1, 3)),
                stage_a(n0, jnp.bitwise_and(u + 2, 3)), stage_a(n1, jnp.bitwise_and(u + 3, 3)))
            return carry

        lax.fori_loop(0, nu // 2, sup_body, 0)

    for j in range(nb):
        def state_body(c, st, j=j):
            ci = j * nchunk + c + d * (nchunk - 1 - 2 * c)
            sall_sc[ci] = st.astype(BF16)
            return dec_sc[ci][0:1, :] * st + upd_sc[ci]

        st_fin = jnp.transpose(lax.fori_loop(0, nchunk, state_body, s0_ref[j, 0]))
        for h in range(GLA_HEADS):
            sfin_ref[j, 0, 0, h] = st_fin[h * GLA_DK:(h + 1) * GLA_DK, h * GLA_DV:(h + 1) * GLA_DV]

    def inter(u):
        rows = group_rows(u)
        qe = qe_sc[rows, :]
        oi = jnp.concatenate([_dot_nt(qe[c * ck:(c + 1) * ck], sall_sc[u * cps + c]) for c in range(cps)], axis=0)
        o_ref[0, rows, :] = (oi_sc[rows, :] + oi).astype(BF16)

    if nu <= 4:
        for u in range(nu):
            inter(u)
    else:
        lax.fori_loop(0, nu, lambda u, carry: (inter(u), carry)[1], 0, unroll=2 if nu % 2 == 0 else 1)


def _gla(gla_in, mla_in, wl, s0t, batch, seq_len, fin=(0, None, 1)):
    layer, prev_fin, fin_layers = fin
    aliases = {} if prev_fin is None else {7: 1}
    extra = [] if prev_fin is None else [prev_fin]
    n = gla_in.shape[0]
    dk, dv = GLA_HEADS * GLA_DK, GLA_HEADS * GLA_DV
    nb = max(1, min(batch, 1024 // seq_len))
    rows = nb * seq_len
    nchunk = nb * (seq_len // GLA_CHUNK)
    return pl.pallas_call(
        functools.partial(_gla_kernel, seq_len, nb, len(extra)),
        grid=(batch // nb, 2),
        input_output_aliases=aliases,
        in_specs=[pl.BlockSpec((rows, dk), lambda b, d: (b, 0)),
                  pl.BlockSpec((rows, dk), lambda b, d: (b, 1)),
                  pl.BlockSpec((rows, dv), lambda b, d: (b, 1)),
                  pl.BlockSpec((rows, dk), lambda b, d: (b, 1)),
                  pl.BlockSpec((1, dk, dk), lambda b, d: (d, 0, 0)),
                  pl.BlockSpec((1, 1, dk), lambda b, d: (d, 0, 0)),
                  pl.BlockSpec((nb, 1, dv, dk), lambda b, d: (b, d, 0, 0))]
                 + [pl.BlockSpec(memory_space=pl.ANY)] * len(extra),
        out_specs=[pl.BlockSpec((1, rows, dv), lambda b, d: (d, b, 0)),
                   pl.BlockSpec((nb, 1, 1, GLA_HEADS, GLA_DK, GLA_DV), lambda b, d: (b, layer, d, 0, 0, 0))],
        out_shape=[jax.ShapeDtypeStruct((2, n, dv), BF16),
                   jax.ShapeDtypeStruct((batch, fin_layers, 2, GLA_HEADS, GLA_DK, GLA_DV), F32)],
        scratch_shapes=[pltpu.VMEM((rows, dk), BF16),
                        pltpu.VMEM((nchunk, dv, dk), F32),
                        pltpu.VMEM((nchunk, 8, dk), F32),
                        pltpu.VMEM((nchunk, dv, dk), BF16),
                        pltpu.VMEM((4, GLA_SUPER, dk), BF16),
                        pltpu.VMEM((4, GLA_SUPER, dk), BF16),
                        pltpu.VMEM((rows, 2 * dk), BF16),
                        pltpu.VMEM((rows, dv), F32)],
        compiler_params=_cp("arbitrary", "arbitrary"),
        name="gla",
    )(gla_in, gla_in, gla_in, mla_in, wl["gla_gw"], wl["gla_gb"], s0t, *extra)


def _outproj_kernel(x_ref, mod_ref, g_ref, om_ref, oh_ref, su_ref, sf_ref, sb_ref, sd_ref, sw_ref, sbias_ref,
                    gf_ref, gb_ref, gn_ref, hm_ref, w_ref, y_ref):
    c = D_BRANCH
    g = g_ref[...].astype(F32)
    acc = _dot((om_ref[...].astype(F32) * _silu(g[:, 0:c])).astype(BF16), w_ref[0:c, :])
    acc += _dot((oh_ref[...].astype(F32) * _silu(g[:, c:2 * c])).astype(BF16), w_ref[c:2 * c, :])
    ys = sd_ref[...] * su_ref[...] + sf_ref[0] + sb_ref[0]
    ge = 0.5 * ys * (1.0 + jnp.tanh(math.sqrt(2.0 / math.pi) * (ys + 0.044715 * (ys * ys * ys))))
    o_s5 = ge / (1.0 + jnp.exp(-(_dot(ge.astype(BF16), sw_ref[...]) + sbias_ref[...])))
    acc += _dot((o_s5 * _silu(g[:, 2 * c:3 * c])).astype(BF16), w_ref[2 * c:3 * c, :])
    og = gf_ref[0].astype(F32) + gb_ref[0].astype(F32)
    hi, lo = _split2(og * og)
    ms = _dot(hi, hm_ref[...]) + _dot(lo, hm_ref[...])
    ogn = og * lax.rsqrt(ms + EPS) * gn_ref[...]
    acc += _dot((ogn * _silu(g[:, 3 * c:4 * c])).astype(BF16), w_ref[3 * c:4 * c, :])
    y_ref[...] = x_ref[...] + mod_ref[0, 2:3, :] * acc


def _outproj(x, mod, mod_row, gates, o_mla, o_hy, s5_u, s5_y, o_gla, wl, tm):
    n, d = x.shape
    c = D_BRANCH
    row = lambda w: pl.BlockSpec((tm, w), lambda i: (i, 0))
    per_dir = lambda k: pl.BlockSpec((1, tm, c), lambda i: (k, i, 0))
    full = lambda *shape: pl.BlockSpec(shape, lambda i: (0,) * len(shape))
    return pl.pallas_call(
        _outproj_kernel,
        grid=(n // tm,),
        in_specs=[row(d),
                  pl.BlockSpec((1, 3, d), lambda i: (mod_row(i * tm), 0, 0)),
                  row(d), row(c), row(c),
                  row(c), per_dir(0), per_dir(1), full(1, c), full(c, c), full(1, c),
                  per_dir(0), per_dir(1), full(1, c), full(c, c), full(d, d)],
        out_specs=row(d),
        out_shape=jax.ShapeDtypeStruct((n, d), F32),
        compiler_params=_cp("arbitrary"),
        name="outproj",
    )(x, mod, gates, o_mla, o_hy, s5_u, s5_y, s5_y, wl["s5_d"], wl["s5_glu_w"], wl["s5_glu_b"],
      o_gla, o_gla, wl["gla_norm"], wl["head_mean"], wl["w_out"])


def _rope_tables(seq_len):
    pos = np.arange(seq_len)
    inv = ROPE_BASE ** (-np.arange(0, 16, 2, dtype=np.float64) / 16.0)
    cos = np.ones((seq_len, HEAD_PAD))
    sin_a = np.zeros((seq_len, HEAD_PAD))
    sin_b = np.zeros((seq_len, HEAD_PAD))
    for base, p in ((MLA_NOPE, pos // GRID_W), (MLA_NOPE + 16, pos % GRID_W)):
        ang = p[:, None].astype(np.float64) * inv[None, :]
        cos[:, base:base + 8] = np.cos(ang)
        cos[:, base + 8:base + 16] = np.cos(ang)
        sin_a[:, base:base + 8] = -np.sin(ang)
        sin_b[:, base + 8:base + 16] = np.sin(ang)
    return tuple(jnp.asarray(t, F32) for t in (cos, sin_a, sin_b))


def _odd_dft(seq_len):
    bk = min(seq_len, HY_BLOCK)
    k = np.arange(bk)[:, None]
    t = np.arange(bk)[None, :]

    def mat(shift):
        ang = (np.pi / (2 * bk)) * (((2 * k + 1) * (t + shift)) % (4 * bk))
        return np.concatenate([np.cos(ang), -np.sin(ang)], axis=0)

    fo = mat(0)
    fb = -mat(bk)
    fb[:, 0] = 0.0
    const = lambda a: jnp.asarray(a, F32).astype(BF16)
    return const(fo), const(fb), const(fo.T)


def _hyena_tables(seq_len):
    lag = np.arange(-seq_len, seq_len)
    pos = np.where(lag == -seq_len, 0, np.abs(lag)).astype(np.float64)
    t = pos / seq_len
    w = 2.0 * np.pi * pos / seq_len
    bands = np.linspace(1e-4, HY_BANDS - 1, HY_BANDS)
    feat = np.zeros((2 * seq_len, HY_HIDDEN))
    feat[:, 0] = t
    feat[:, 1:1 + HY_BANDS] = np.cos(w[:, None] * bands)
    feat[:, 1 + HY_BANDS:HY_FEAT] = np.sin(w[:, None] * bands)
    feat = feat.reshape(-1, 2, 128, HY_HIDDEN).transpose(0, 2, 1, 3).reshape(seq_len, 2 * HY_HIDDEN)
    deltas = np.linspace(math.log(1.0 / HY_TARGET) / HY_FAST_DECAY, math.log(1.0 / HY_TARGET) / HY_SLOW_DECAY,
                         D_BRANCH)
    win = np.exp(-t[:, None] * deltas[None, :]) + HY_SHIFT
    return jnp.asarray(feat, F32), jnp.asarray(win, F32)


def _pad_to(a, shape):
    return jnp.pad(a, [(0, s - d) for s, d in zip(shape, a.shape)])


def _layer_weights(l, p):
    z = lambda *s: jnp.zeros(s, F32)
    d = D_MODEL
    wl = {"layer": l, "w_in": p["w_in"], "norm_w": p["norm_w"][l].reshape(1, d), "w_out": p["w_out"][l].astype(BF16)}
    wl["qa_norm"] = _pad_to(p["mla_qa_norm"][l].reshape(1, -1), (1, 256))
    w_uq = _pad_to(p["mla_w_uq"][l].reshape(MLA_Q_RANK, MLA_HEADS, MLA_QK), (256, MLA_HEADS, HEAD_PAD))
    wl["w_uq"] = w_uq.reshape(256, MLA_HEADS * HEAD_PAD).astype(BF16)
    wl["q_norm"] = _pad_to(p["mla_q_norm"][l].reshape(1, -1), (1, HEAD_PAD))
    wl["k_norm"] = _pad_to(p["mla_k_norm"][l].reshape(1, -1), (1, HEAD_PAD))
    wl["kva_norm"] = p["mla_kva_norm"][l].reshape(1, -1)
    w_ukv = p["mla_w_ukv"][l].reshape(MLA_KV_RANK, MLA_HEADS, MLA_NOPE + MLA_V)
    wl["w_uk"] = _pad_to(w_ukv[:, :, :MLA_NOPE], (MLA_KV_RANK, MLA_HEADS, HEAD_PAD)).reshape(MLA_KV_RANK, -1).astype(BF16)
    wl["w_uv"] = w_ukv[:, :, MLA_NOPE:].reshape(MLA_KV_RANK, MLA_HEADS * MLA_V).astype(BF16)
    wl["hy_conv_w"] = p["hy_conv_w"][l]
    wl["hy_conv_b"] = p["hy_conv_b"][l].reshape(1, -1)
    hh = HY_HIDDEN
    twice = lambda a: jnp.tile(a.reshape(1, -1), (1, 2))
    bdiag = lambda a: jnp.concatenate([jnp.pad(a, ((0, 0), (0, a.shape[1]))), jnp.pad(a, ((0, 0), (a.shape[1], 0)))], axis=0)
    wl["hy_w1"] = bdiag(_pad_to(p["hy_w1"][l], (hh, hh)))
    wl["hy_b1"], wl["hy_f1"] = twice(p["hy_b1"][l]), twice(p["hy_freq1"][l])
    wl["hy_w2"] = bdiag(p["hy_w2"][l])
    wl["hy_b2"], wl["hy_f2"] = twice(p["hy_b2"][l]), twice(p["hy_freq2"][l])
    w3 = p["hy_w3"][l]
    wl["hy_w3"] = jnp.concatenate([bdiag(w3[:, 0:512]), bdiag(w3[:, 512:1024])], axis=1)
    wl["hy_bias"] = p["hy_bias"][l]
    flat = lambda a: a[l].reshape(2, 1, S5_N)
    wl["s5_are"], wl["s5_aim"] = flat(p["s5_a_re"]), flat(p["s5_a_im"])
    wl["s5_ldt"] = jnp.repeat(p["s5_log_dt"][l], S5_STATE, axis=-1).reshape(2, 1, S5_N)
    lanes_b = lambda a: jnp.tile(a[l].transpose(0, 1, 3, 2).reshape(2, D_BRANCH, S5_STATE), (1, 1, 128 // S5_STATE))
    lanes_c = lambda a: jnp.tile(a[l].transpose(0, 1, 3, 2).reshape(2, S5_N, S5_GROUP), (1, 1, 128 // S5_GROUP))
    wl["s5_bre"], wl["s5_bim"] = lanes_b(p["s5_b_re"]), lanes_b(p["s5_b_im"])
    wl["s5_cre"], wl["s5_cim"] = lanes_c(p["s5_c_re"]), lanes_c(p["s5_c_im"])
    wl["s5_d"] = p["s5_d"][l].reshape(1, -1)
    wl["s5_glu_w"] = p["s5_glu_w"][l].astype(BF16)
    wl["s5_glu_b"] = p["s5_glu_b"][l].reshape(1, -1)
    gw = p["gla_gw"][l]
    dk = GLA_HEADS * GLA_DK
    wl["gla_gw"] = jnp.stack([_pad_to(jnp.pad(gw[i], ((GLA_G_LANE + GLA_RANK * i, 0), (0, 0))), (dk, dk))
                              for i in range(2)]).astype(BF16)
    wl["gla_gb"] = p["gla_gb"][l].reshape(2, 1, dk)
    wl["gla_norm"] = jnp.tile(p["gla_norm"][l], GLA_HEADS).reshape(1, -1)
    head = np.arange(D_BRANCH) // GLA_DV
    wl["head_mean"] = jnp.asarray((head[:, None] == head[None, :]) / GLA_DV, BF16)
    return wl


def _hyena_filters(wl, tabs):
    feat, win, fo, fb, _ = tabs
    kern_lin, nrm = _hy_mlp(feat, wl, win)
    return _hy_kspec(kern_lin, nrm, fo, fb)


def _trunk_layer(x, mod, mod_row, wl, batch, seq_len, hy_tabs, rope_tabs=None, ctx=None, layer=0, cache_bufs=None):
    n = batch * seq_len
    tm = 512
    if ctx is None:
        mla_in, gates, hy_in, s5_in, gla_in, q, k, v, ckv, krope = _inproj(
            x, mod, mod_row, wl, tm, seq_len, (layer, cache_bufs[:2]))
        kv_parts = [(k, v, seq_len)]
    else:
        mla_in, gates, hy_in, s5_in, gla_in = _inproj(x, mod, mod_row, wl, tm, seq_len)
        q, k, v = _mla_prep(mla_in, wl, rope_tabs, seq_len, tm, True)
        ckv = krope = None
        k_ctx, v_ctx = _mla_prep(ctx["mla"], wl, None, ctx["past"], 512, False)
        kv_parts = [(k_ctx, v_ctx, ctx["past"]), (k, v, seq_len)]
    o_mla = _attention(q, kv_parts, batch, seq_len, 256)

    o_hy = _hyena(hy_in, wl, _hyena_filters(wl, hy_tabs), hy_tabs[2], hy_tabs[4], batch, seq_len)

    nseg = seq_len // S5_SEG
    nseq = batch * nseg
    u_seg = s5_in.reshape(nseq, S5_SEG, D_BRANCH)
    if ctx is None:
        hin = jnp.zeros((2, nseq, 2 * S5_N), F32)
    else:
        (fin,) = _s5_scan(u_seg, jnp.zeros((2, nseq, 2 * S5_N), F32), wl, False)
        hin = _s5_chain(fin, ctx["s5_h0"], wl, batch, nseg)
    y2, s5_fin = _s5_scan(u_seg, hin, wl, True)

    s0 = jnp.zeros((batch, 2, GLA_HEADS * GLA_DV, GLA_HEADS * GLA_DK), F32) if ctx is None else ctx["gla_s0"]
    fin = (0, None, 1) if ctx is not None else (layer, None if cache_bufs is None else cache_bufs[2], DEPTH)
    o_gla, gla_fin = _gla(gla_in, mla_in, wl, s0, batch, seq_len, fin)

    y = _outproj(x, mod, mod_row, gates, o_mla, o_hy, s5_in, y2.reshape(2, n, D_BRANCH), o_gla, wl, 2 * tm)
    return y, (ckv, krope, s5_fin, gla_fin)


def kernel(x_prompt, x_sample, c, cache_mla_ckv, cache_mla_krope, state_s5, state_gla, c_ctx, norm_w, ada_w, ada_b, w_in, w_out, mla_qa_norm, mla_kva_norm, mla_w_uq, mla_w_ukv, mla_q_norm, mla_k_norm, hy_conv_w, hy_conv_b, hy_w1, hy_b1, hy_freq1, hy_w2, hy_b2, hy_freq2, hy_w3, hy_bias, s5_a_re, s5_a_im, s5_log_dt, s5_b_re, s5_b_im, s5_c_re, s5_c_im, s5_d, s5_glu_w, s5_glu_b, gla_gw, gla_gb, gla_norm):
    params = dict(norm_w=norm_w, w_in=w_in, w_out=w_out, mla_qa_norm=mla_qa_norm, mla_kva_norm=mla_kva_norm,
                  mla_w_uq=mla_w_uq, mla_w_ukv=mla_w_ukv, mla_q_norm=mla_q_norm, mla_k_norm=mla_k_norm,
                  hy_conv_w=hy_conv_w, hy_conv_b=hy_conv_b, hy_w1=hy_w1, hy_b1=hy_b1, hy_freq1=hy_freq1,
                  hy_w2=hy_w2, hy_b2=hy_b2, hy_freq2=hy_freq2, hy_w3=hy_w3, hy_bias=hy_bias,
                  s5_a_re=s5_a_re, s5_a_im=s5_a_im, s5_log_dt=s5_log_dt, s5_b_re=s5_b_re, s5_b_im=s5_b_im,
                  s5_c_re=s5_c_re, s5_c_im=s5_c_im, s5_d=s5_d, s5_glu_w=s5_glu_w, s5_glu_b=s5_glu_b,
                  gla_gw=gla_gw, gla_gb=gla_gb, gla_norm=gla_norm)
    bp, lp, d = x_prompt.shape
    bs, ls, _ = x_sample.shape
    past = cache_mla_ckv.shape[2]
    n_s5 = S5_N

    conds = jnp.concatenate([c_ctx[None, :], c, jnp.zeros((8 - 1 - bs, d), F32)], axis=0)
    mods = _modulation(conds, ada_w, ada_b).reshape(DEPTH, 8, 3, d)

    tabs_p = _hyena_tables(lp) + _odd_dft(lp)
    tabs_s = _hyena_tables(ls) + _odd_dft(ls)
    rope_tabs = _rope_tables(ls)
    nseg = ls // S5_SEG

    y_p = x_prompt.reshape(bp * lp, d)
    y_s = x_sample.reshape(bs * ls, d)
    s5_l = []
    cache_bufs = (jnp.zeros((bp, DEPTH, lp, MLA_KV_RANK), F32), jnp.zeros((bp, DEPTH, lp, MLA_ROPE), F32),
                  jnp.zeros((bp, DEPTH, 2, GLA_HEADS, GLA_DK, GLA_DV), F32))
    for l in range(DEPTH):
        wl = _layer_weights(l, params)
        y_p, (ckv, krope, s5_fin, gla_fin) = _trunk_layer(y_p, mods[l], lambda row: 0, wl, bp, lp, tabs_p,
                                                          layer=l, cache_bufs=cache_bufs)
        cache_bufs = (ckv, krope, gla_fin)
        s5_l.append(jnp.stack([s5_fin[:, :, :n_s5], s5_fin[:, :, n_s5:]], axis=-1)
                    .reshape(2, bp, S5_GROUPS, S5_STATE, 2).transpose(1, 0, 2, 3, 4))

        mla_ctx = jnp.concatenate([cache_mla_ckv[:, l], jnp.zeros((bs, past, 64), F32), cache_mla_krope[:, l],
                                   jnp.zeros((bs, past, 32), F32)], axis=-1).reshape(bs * past, 256)
        st = state_s5[:, l]
        h0 = jnp.concatenate([st[..., 0].reshape(bs, 2, n_s5), st[..., 1].reshape(bs, 2, n_s5)], axis=-1)
        h0 = h0.transpose(1, 0, 2)
        h0rows = jnp.zeros((2, nseg * bs, 2 * n_s5), F32)
        h0rows = h0rows.at[0, 0::nseg].set(h0[0]).at[1, nseg - 1::nseg].set(h0[1])
        eye_h = jnp.eye(GLA_HEADS, dtype=F32)
        gla_s0 = jnp.einsum("bdhke,hg->bdhegk", state_gla[:, l], eye_h).reshape(
            bs, 2, GLA_HEADS * GLA_DV, GLA_HEADS * GLA_DK)
        ctx = {"mla": mla_ctx, "past": past, "s5_h0": h0rows, "gla_s0": gla_s0}
        y_s, _ = _trunk_layer(y_s, mods[l], lambda row: 1 + row // ls, wl, bs, ls, tabs_s, rope_tabs, ctx)

    return (y_p.reshape(bp, lp, d), y_s.reshape(bs, ls, d),
            cache_bufs[0], cache_bufs[1], jnp.stack(s5_l, axis=1), cache_bufs[2])
```

```python
import functools
import math

import numpy as np
import jax
import jax.numpy as jnp
from jax import lax
from jax.experimental import pallas as pl
from jax.experimental.pallas import tpu as pltpu

F32 = jnp.float32
BF16 = jnp.bfloat16

D_MODEL = 1024
DEPTH = 2
GRID_W = 64
D_BRANCH = 256
EPS = 1e-6

MLA_HEADS = 4
MLA_Q_RANK = 192
MLA_KV_RANK = 128
MLA_NOPE = 64
MLA_ROPE = 32
MLA_QK = 96
MLA_V = 64
ROPE_BASE = 10000.0
HEAD_PAD = 128

HY_BANDS = 16
HY_FEAT = 33
HY_HIDDEN = 64
HY_SHIFT = 0.05
HY_FAST_DECAY = 0.3
HY_SLOW_DECAY = 1.5
HY_TARGET = 1e-2
HY_BLOCK = 512

S5_GROUP = 16
S5_GROUPS = 16
S5_STATE = 64
S5_N = S5_GROUPS * S5_STATE
S5_ROWS = 512
S5_SEG = 256

GLA_HEADS = 4
GLA_DK = 32
GLA_DV = 64
GLA_RANK = 16
GLA_TAU = 16.0
GLA_CHUNK = 64
GLA_SUPER = 256

SEG_MLA = (0, 384)
SEG_GATE = (384, 1408)
SEG_HY = (1408, 2176)
SEG_S5 = (2176, 2432)
SEG_GLA = (2432, 2944)
N_PROJ = 2944
W_IN_ORDER = ((0, 192), (320, 352), (2656, 2688), (192, 320),
              (352, 608), (1376, 1632), (1888, 2144), (2688, 2944),
              (608, 1376), (1632, 1888), (2144, 2656))
GLA_G_LANE = 96

VMEM_LIMIT = 48 * 1024 * 1024


def _cp(*sem):
    return pltpu.CompilerParams(dimension_semantics=sem, vmem_limit_bytes=VMEM_LIMIT)


def _dot(a, b):
    return jnp.dot(a, b, preferred_element_type=F32)


def _dot_nt(a, b):
    return lax.dot_general(a, b, (((1,), (1,)), ((), ())), preferred_element_type=F32)


def _dot_tn(a, b):
    return lax.dot_general(a, b, (((0,), (0,)), ((), ())), preferred_element_type=F32)


def _split2(x):
    hi = x.astype(BF16)
    lo = (x - hi.astype(F32)).astype(BF16)
    return hi, lo


def _split3(x):
    h1 = x.astype(BF16)
    r1 = x - h1.astype(F32)
    h2 = r1.astype(BF16)
    h3 = (r1 - h2.astype(F32)).astype(BF16)
    return h1, h2, h3


def _dot3(a, b):
    a1, a2 = _split2(a)
    b1, b2 = _split2(b)
    return _dot(a1, b1) + (_dot(a1, b2) + _dot(a2, b1))


def _silu(z):
    return z / (1.0 + jnp.exp(-z))


def _mod_kernel(c_ref, w_ref, b_ref, o_ref):
    s = _silu(c_ref[...])
    o_ref[0] = _dot(s.astype(BF16), w_ref[0].astype(BF16)) + b_ref[0]


def _modulation(conds, ada_w, ada_b):
    d = D_MODEL
    return pl.pallas_call(
        _mod_kernel,
        grid=(DEPTH, 3),
        in_specs=[pl.BlockSpec((8, d), lambda l, j: (0, 0)),
                  pl.BlockSpec((1, d, d), lambda l, j: (l, 0, j)),
                  pl.BlockSpec((1, 1, d), lambda l, j: (l, 0, j))],
        out_specs=pl.BlockSpec((1, 8, d), lambda l, j: (l, 0, j)),
        out_shape=jax.ShapeDtypeStruct((DEPTH, 8, 3 * d), F32),
        compiler_params=_cp("arbitrary", "arbitrary"),
        name="modulation",
    )(conds, ada_w, ada_b.reshape(DEPTH, 1, 3 * d))


def _head_norm(xh, w):
    ms = jnp.sum(xh * xh, axis=-1, keepdims=True) * (1.0 / MLA_QK)
    return xh * lax.rsqrt(ms + EPS) * w


def _rope(xh, cos, sin_a, sin_b):
    return xh * cos + pltpu.roll(xh, HEAD_PAD - 8, 1) * sin_a + pltpu.roll(xh, 8, 1) * sin_b


def _mla_steps(m, has_q, rope, cache_seqs, n_aliased, refs):
    refs = list(refs)
    if has_q:
        qan_ref, wuq_ref, qn_ref, kvn_ref = refs[:4]
        refs = refs[4:]
    wuk_ref, wuv_ref, kn_ref = refs[:3]
    refs = refs[3:]
    if rope:
        cos_ref, sa_ref, sb_ref = refs[:3]
        refs = refs[3:]
        cos, sa, sb = cos_ref[...], sa_ref[...], sb_ref[...]
    refs = refs[n_aliased:]
    if has_q:
        q_ref = refs.pop(0)
    k_ref, v_ref = refs[:2]
    if cache_seqs:
        ckv_ref, kro_ref = refs[2:]
    if has_q:
        lane = lax.broadcasted_iota(jnp.int32, (1, HEAD_PAD), 1)
        mixed = m[:, 128:256]
        cq = jnp.concatenate([m[:, 0:128], jnp.where(lane < MLA_NOPE, mixed, 0.0)], axis=1)
        ms = jnp.sum(cq * cq, axis=-1, keepdims=True) * (1.0 / MLA_Q_RANK)
        cqn = cq * lax.rsqrt(ms + EPS) * qan_ref[...]
        ckv = m[:, 256:384]
        ckvn = ckv * lax.rsqrt(jnp.mean(ckv * ckv, axis=-1, keepdims=True) + EPS) * kvn_ref[...]
        yield
        q = _dot(cqn.astype(BF16), wuq_ref[...])
        kr = jnp.where(jnp.logical_and(lane >= MLA_NOPE, lane < MLA_NOPE + MLA_ROPE), mixed, 0.0)
        if cache_seqs:
            seq_len = ckv_ref.shape[2]
            for s in range(cache_seqs):
                ckv_ref[s, 0] = ckvn[s * seq_len:(s + 1) * seq_len]
                kro_ref[s, 0] = kr[s * seq_len:(s + 1) * seq_len, MLA_NOPE:MLA_NOPE + MLA_ROPE]
    else:
        ckvn = m[:, 0:128]
        kr = m[:, 128:256]
    cb = ckvn.astype(BF16)
    kup = _dot(cb, wuk_ref[...])
    v_ref[...] = _dot(cb, wuv_ref[...]).astype(BF16)
    yield
    for h in range(MLA_HEADS):
        sl = slice(HEAD_PAD * h, HEAD_PAD * (h + 1))
        kh = _head_norm(kup[:, sl] + kr, kn_ref[...])
        if rope:
            kh = _rope(kh, cos, sa, sb)
        k_ref[:, sl] = kh.astype(BF16)
        if has_q:
            qh = _head_norm(q[:, sl], qn_ref[...])
            if rope:
                qh = _rope(qh, cos, sa, sb)
            q_ref[:, sl] = (qh * (MLA_QK ** -0.5)).astype(BF16)
        if h % 2 == 1:
            yield


def _mla_prep_kernel(has_q, rope, m_ref, *refs):
    for _ in _mla_steps(m_ref[...], has_q, rope, 0, 0, refs):
        pass


def _inproj_kernel(fuse_mla, cache_seqs, n_aliased, x_ref, mod_ref, nw_ref, w_ref, *refs):
    n_mla_in = 7 + n_aliased if fuse_mla else 0
    mla_refs, (o_mla, o_g, o_hy, o_s5, o_gla) = refs[:n_mla_in], refs[n_mla_in:n_mla_in + 5]
    w_src, w_ref = w_ref, refs[-1]

    @pl.when(pl.program_id(0) == 0)
    def _():
        dst = 0
        for lo, hi in W_IN_ORDER:
            w_ref[:, dst:dst + hi - lo] = w_src[0, :, lo:hi].astype(BF16)
            dst += hi - lo

    x = x_ref[...]
    ms = jnp.mean(x * x, axis=-1, keepdims=True)
    y = x * lax.rsqrt(ms + EPS) * nw_ref[...]
    h = (y * (1.0 + mod_ref[0, 1:2, :]) + mod_ref[0, 0:1, :]).astype(BF16)
    project = lambda o, seg: o.__setitem__(Ellipsis, _dot(h, w_ref[:, seg[0]:seg[1]]).astype(o.dtype))
    m = _dot(h, w_ref[:, SEG_MLA[0]:SEG_MLA[1]])
    o_mla[...] = m
    steps = iter(()) if not fuse_mla else _mla_steps(
        m, True, False, cache_seqs, n_aliased, list(mla_refs) + list(refs[n_mla_in + 5:-1]))
    project(o_g, SEG_GATE)
    next(steps, None)
    next(steps, None)
    project(o_hy, SEG_HY)
    next(steps, None)
    project(o_s5, SEG_S5)
    project(o_gla, SEG_GLA)
    for _ in steps:
        pass


def _inproj(x, mod, mod_row, wl, tm, seq_len, cache=None):
    n, d = x.shape
    fuse_mla = cache is not None
    full = lambda shape: pl.BlockSpec(shape, lambda i: (0,) * len(shape))
    row = lambda w: pl.BlockSpec((tm, w), lambda i: (i, 0))
    args = [x, mod, wl["norm_w"], wl["w_in"]]
    specs = [row(d), pl.BlockSpec((1, 3, d), lambda i: (mod_row(i * tm), 0, 0)), full((1, d)),
             pl.BlockSpec((1, d, N_PROJ), lambda i: (wl["layer"], 0, 0), pipeline_mode=pl.Buffered(1))]
    widths = [hi - lo for lo, hi in (SEG_MLA, SEG_GATE, SEG_HY, SEG_S5, SEG_GLA)]
    dtypes = [F32, BF16, F32, F32, F32]
    aliases = {}
    nseq = 0
    if fuse_mla:
        args += [wl["qa_norm"], wl["w_uq"], wl["q_norm"], wl["kva_norm"], wl["w_uk"], wl["w_uv"], wl["k_norm"]]
        specs += [full((1, 256)), full((256, 512)), full((1, 128)), full((1, 128)),
                  full((128, 512)), full((128, 256)), full((1, 128))]
        widths += [512, 512, 256]
        dtypes += [BF16, BF16, BF16]
    out_specs = [row(w) for w in widths]
    out_shape = [jax.ShapeDtypeStruct((n, w), t) for w, t in zip(widths, dtypes)]
    if fuse_mla:
        layer, prev = cache
        nseq = tm // seq_len
        for w in (MLA_KV_RANK, MLA_ROPE):
            out_specs.append(pl.BlockSpec((nseq, 1, seq_len, w), lambda i: (i, layer, 0, 0)))
            out_shape.append(jax.ShapeDtypeStruct((n // seq_len, DEPTH, seq_len, w), F32))
        for k, buf in enumerate(prev):
            aliases[len(args)] = len(out_shape) - 2 + k
            args.append(buf)
            specs.append(pl.BlockSpec(memory_space=pl.ANY))
    return pl.pallas_call(
        functools.partial(_inproj_kernel, fuse_mla, nseq, len(aliases)),
        grid=(n // tm,),
        in_specs=specs, out_specs=out_specs, out_shape=out_shape,
        input_output_aliases=aliases,
        scratch_shapes=[pltpu.VMEM((d, N_PROJ), BF16)],
        compiler_params=_cp("arbitrary"),
        name="inproj",
    )(*args)


def _mla_prep(m, wl, rope_tabs, seq_len, tm, has_q):
    n, wm = m.shape
    rope = rope_tabs is not None
    full = lambda shape: pl.BlockSpec(shape, lambda i: (0,) * len(shape))
    row = lambda w: pl.BlockSpec((tm, w), lambda i: (i, 0))
    args, specs = [m], [row(wm)]
    if has_q:
        args += [wl["qa_norm"], wl["w_uq"], wl["q_norm"], wl["kva_norm"]]
        specs += [full((1, 256)), full((256, 512)), full((1, 128)), full((1, 128))]
    args += [wl["w_uk"], wl["w_uv"], wl["k_norm"]]
    specs += [full((128, 512)), full((128, 256)), full((1, 128))]
    if rope:
        nt = seq_len // tm
        args += list(rope_tabs)
        specs += [pl.BlockSpec((tm, HEAD_PAD), lambda i: (i % nt, 0))] * 3
    widths = ([512] if has_q else []) + [512, 256]
    return pl.pallas_call(
        functools.partial(_mla_prep_kernel, has_q, rope),
        grid=(n // tm,),
        in_specs=specs, out_specs=[row(w) for w in widths],
        out_shape=[jax.ShapeDtypeStruct((n, w), BF16) for w in widths],
        compiler_params=_cp("arbitrary"),
        name="mla_prep",
    )(*args)


def _attn_kernel(nparts, nseq, q_ref, *refs):
    kv = [(refs[2 * i], refs[2 * i + 1]) for i in range(nparts)]
    o_ref = refs[2 * nparts]
    tq = q_ref.shape[0] // nseq
    low = lax.broadcasted_iota(jnp.int32, (1, HEAD_PAD), 1) < MLA_V
    units = [(s, h) for s in range(nseq) for h in range(MLA_HEADS)]

    def keys(ref, s):
        lk = ref.shape[0] // nseq
        return slice(s * lk, (s + 1) * lk)

    def scores(s, h):
        sl = slice(HEAD_PAD * h, HEAD_PAD * (h + 1))
        return [_dot_nt(q_ref[s * tq:(s + 1) * tq, sl], k_ref[keys(k_ref, s), sl]) for k_ref, _ in kv]

    s_next = scores(*units[0])
    acc = None
    for n, (s, h) in enumerate(units):
        pair, j = divmod(h, 2)
        sc = s_next
        if n + 1 < len(units):
            s_next = scores(*units[n + 1])
        if j == 0:
            v_half = []
            for _, v_ref in kv:
                vp = v_ref[keys(v_ref, s), HEAD_PAD * pair:HEAD_PAD * (pair + 1)]
                zero = jnp.zeros_like(vp)
                v_half.append((jnp.where(low, vp, zero), jnp.where(low, zero, vp)))
        m = functools.reduce(jnp.maximum, [jnp.max(x, axis=-1, keepdims=True) for x in sc])
        p = [jnp.exp(x - m) for x in sc]
        den = functools.reduce(jnp.add, [jnp.sum(x, axis=-1, keepdims=True) for x in p])
        num = functools.reduce(jnp.add, [_dot(x.astype(BF16), vh[j]) for x, vh in zip(p, v_half)])
        o = num / den
        acc = o if j == 0 else acc + o
        if j == 1:
            o_ref[s * tq:(s + 1) * tq, HEAD_PAD * pair:HEAD_PAD * (pair + 1)] = acc.astype(BF16)


def _attention(q, kv_parts, batch, lq, tq):
    nq = lq // tq
    nseq = max(1, min(batch, 1024 // lq)) if nq == 1 else 1
    args, specs = [q], [pl.BlockSpec((nseq * tq, 512), lambda b, i: (b * nq + i, 0))]
    for k, v, lk in kv_parts:
        args += [k, v]
        specs += [pl.BlockSpec((nseq * lk, 512), lambda b, i: (b, 0)),
                  pl.BlockSpec((nseq * lk, 256), lambda b, i: (b, 0))]
    return pl.pallas_call(
        functools.partial(_attn_kernel, len(kv_parts), nseq),
        grid=(batch // nseq, nq),
        in_specs=specs,
        out_specs=pl.BlockSpec((nseq * tq, 256), lambda b, i: (b * nq + i, 0)),
        out_shape=jax.ShapeDtypeStruct((batch * lq, 256), BF16),
        compiler_params=_cp("arbitrary", "arbitrary"),
        name="attention",
    )(*args)


def _hyena_kernel(seq_len, bk, nseq, x_ref, cw_ref, cb_ref, fo_ref, go_ref, k_ref, bias_ref, o_ref,
                  u_sc, y_sc, z_sc):
    c = D_BRANCH
    n = nseq * seq_len
    nblk = seq_len // bk
    pos = jnp.bitwise_and(lax.broadcasted_iota(jnp.int32, (n, 1), 0), seq_len - 1)
    first, last = pos == 0, pos == seq_len - 1

    def short_conv(g):
        cols = slice(g * c, (g + 1) * c)
        x = x_ref[:, cols]
        xm = jnp.where(first, 0.0, pltpu.roll(x, 1, 0))
        xp = jnp.where(last, 0.0, pltpu.roll(x, n - 1, 0))
        return cw_ref[0:1, cols] * xm + cw_ref[1:2, cols] * x + cw_ref[2:3, cols] * xp + cb_ref[:, cols]

    fo, go = fo_ref[...], go_ref[...]

    def long_conv(s, v, order, emit):
        cols = slice(order * c, (order + 1) * c)
        for j in range(nblk):
            u_sc[s, j] = _dot(fo, v[j * bk:(j + 1) * bk].astype(BF16))
        yield
        rc = 32
        for i in range(nblk):
            def mix(r, carry, i=i):
                top = pl.ds(pl.multiple_of(r * rc, rc), rc)
                bot = pl.ds(pl.multiple_of(bk + r * rc, rc), rc)
                at = ab = None
                for j in range(nblk):
                    q = i - j + nblk - 1
                    kt, kb = k_ref[q, top, cols], k_ref[q, bot, cols]
                    ut, ub = u_sc[s, j, top, :], u_sc[s, j, bot, :]
                    pt, pb = ut * kt - ub * kb, ut * kb + ub * kt
                    at, ab = (pt, pb) if at is None else (at + pt, ab + pb)
                z_sc[s, top, :] = at.astype(BF16)
                z_sc[s, bot, :] = ab.astype(BF16)
                return carry

            lax.fori_loop(0, bk // rc, mix, 0, unroll=True if nblk == 1 else 2)
            y = _dot(go, z_sc[s])
            yield
            emit(i, y)

    v_all, x1_all, x2_all = short_conv(0), short_conv(1), short_conv(2)

    def sequence(s):
        base = s * seq_len
        v = v_all[base:base + seq_len]

        def emit1(i, y):
            r = slice(i * bk, (i + 1) * bk)
            y_sc[s, r, :] = x1_all[base + i * bk:base + (i + 1) * bk] * (y + bias_ref[0:1, :] * v[r])

        yield from long_conv(s, v, 0, emit1)
        y1 = y_sc[s]

        def emit2(i, y):
            r = slice(i * bk, (i + 1) * bk)
            o_ref[base + i * bk:base + (i + 1) * bk, :] = (
                x2_all[base + i * bk:base + (i + 1) * bk] * (y + bias_ref[1:2, :] * y1[r])).astype(BF16)

        yield from long_conv(s, y1, 1, emit2)

    live = [sequence(s) for s in range(nseq)]
    while live:
        for g in list(live):
            if next(g, StopIteration) is StopIteration:
                live.remove(g)


def _hyena(x, wl, kspec, fo, go, batch, seq_len):
    c = D_BRANCH
    bk = fo.shape[1]
    nseq = max(1, min(batch, 1024 // seq_len))
    rows = nseq * seq_len
    nblk = seq_len // bk
    full = lambda a: pl.BlockSpec(a.shape, lambda i: (0,) * a.ndim)
    return pl.pallas_call(
        functools.partial(_hyena_kernel, seq_len, bk, nseq),
        grid=(batch // nseq,),
        in_specs=[pl.BlockSpec((rows, 3 * c), lambda i: (i, 0)),
                  full(wl["hy_conv_w"]), full(wl["hy_conv_b"]), full(fo), full(go),
                  pl.BlockSpec(kspec.shape, lambda i: (0, 0, 0), pipeline_mode=pl.Buffered(1)),
                  full(wl["hy_bias"])],
        out_specs=pl.BlockSpec((rows, c), lambda i: (i, 0)),
        out_shape=jax.ShapeDtypeStruct((batch * seq_len, c), BF16),
        scratch_shapes=[pltpu.VMEM((nseq, nblk, 2 * bk, c), F32), pltpu.VMEM((nseq, seq_len, c), F32),
                        pltpu.VMEM((nseq, 2 * bk, c), BF16)],
        compiler_params=pltpu.CompilerParams(dimension_semantics=("arbitrary",), vmem_limit_bytes=56 * 1024 * 1024),
        name="hyena",
    )(x, wl["hy_conv_w"], wl["hy_conv_b"], fo, go, kspec, wl["hy_bias"])


def _hy_mlp_kernel(feat_ref, w1_ref, b1_ref, f1_ref, w2_ref, b2_ref, f2_ref, w3_ref, win_ref,
                   kern_ref, nrm_ref):
    i = pl.program_id(0)
    tl = win_ref.shape[0]
    h = jnp.sin(f1_ref[...] * (_dot3(feat_ref[...], w1_ref[...]) + b1_ref[...]))
    h = jnp.sin(f2_ref[...] * (_dot3(h, w2_ref[...]) + b2_ref[...]))
    filt = _dot3(h, w3_ref[...])
    filt = jnp.concatenate([filt[:, 0:512], filt[:, 512:1024]], axis=0)
    win = win_ref[...]
    row0 = (lax.broadcasted_iota(jnp.int32, (tl, 1), 0) + i * tl) == 0
    filt = jnp.where(row0, 0.0, filt * jnp.concatenate([win, win], axis=1))
    kern_ref[...] = filt
    part = jnp.sum(jnp.abs(filt), axis=0, keepdims=True)

    @pl.when(i == 0)
    def _():
        nrm_ref[...] = jnp.zeros_like(nrm_ref)
    nrm_ref[...] += jnp.broadcast_to(part, nrm_ref.shape)


def _hy_mlp(feat, wl, win):
    rows = win.shape[0]
    tl = 256
    nl = rows // (2 * tl)
    full = lambda shape: pl.BlockSpec(shape, lambda i: (0,) * len(shape))
    return pl.pallas_call(
        _hy_mlp_kernel,
        grid=(rows // tl,),
        in_specs=[pl.BlockSpec((tl // 2, 128), lambda i: (i, 0)),
                  full((128, 128)), full((1, 128)), full((1, 128)),
                  full((128, 128)), full((1, 128)), full((1, 128)),
                  pl.BlockSpec((128, 1024), lambda i: (0, jnp.where(i < nl, 1, 0))),
                  pl.BlockSpec((tl, 256), lambda i: (i, 0))],
        out_specs=[pl.BlockSpec((tl, 512), lambda i: (i, 0)), full((8, 512))],
        out_shape=[jax.ShapeDtypeStruct((rows, 512), F32), jax.ShapeDtypeStruct((8, 512), F32)],
        compiler_params=_cp("arbitrary"),
        name="hy_mlp",
    )(feat, wl["hy_w1"], wl["hy_b1"], wl["hy_f1"], wl["hy_w2"], wl["hy_b2"], wl["hy_f2"], wl["hy_w3"], win)


def _hy_kspec_kernel(lo_ref, hi_ref, fo_ref, fb_ref, n_ref, o_ref):
    bk = lo_ref.shape[0]
    k = _dot(fo_ref[...], hi_ref[...].astype(BF16)) + _dot(fb_ref[...], lo_ref[...].astype(BF16))
    o_ref[0] = k * ((1.0 / bk) / n_ref[0:1, :])


def _hy_kspec(kern_lin, nrm, fo, fb):
    n2, bk = fo.shape
    nq = kern_lin.shape[0] // bk - 1
    full = lambda a: pl.BlockSpec(a.shape, lambda q: (0,) * a.ndim)
    return pl.pallas_call(
        _hy_kspec_kernel,
        grid=(nq,),
        in_specs=[pl.BlockSpec((bk, 512), lambda q: (q, 0)),
                  pl.BlockSpec((bk, 512), lambda q: (q + 1, 0)),
                  full(fo), full(fb), full(nrm)],
        out_specs=pl.BlockSpec((1, n2, 512), lambda q: (q, 0, 0)),
        out_shape=jax.ShapeDtypeStruct((nq, n2, 512), F32),
        compiler_params=_cp("arbitrary"),
        name="hy_kspec",
    )(kern_lin, kern_lin, fo, fb, nrm)


def _s5_discretise(are_ref, aim_ref, ldt_ref):
    ar = jnp.minimum(are_ref[0], -1e-4)
    ai = aim_ref[0]
    dt = jnp.exp(ldt_ref[0])
    e = jnp.exp(ar * dt)
    return ar, ai, e * jnp.cos(ai * dt), e * jnp.sin(ai * dt)


def _s5_scan_kernel(nseq, emit_y, u_ref, hin_ref, are_ref, aim_ref, ldt_ref, bre_ref, bim_ref, *rest):
    if emit_y:
        cre_ref, cim_ref, y_ref, hfin_ref, wb_sc, ab_sc, s_sc, hc_sc, perm_sc, wc_sc = rest
    else:
        hfin_ref, wb_sc, ab_sc, s_sc, hc_sc, perm_sc = rest
    d = pl.program_id(0)
    c = pl.program_id(1)
    n = S5_N

    @pl.when(c == 0)
    def _():
        ar, ai, abr, abi = _s5_discretise(are_ref, aim_ref, ldt_ref)
        ab_sc[0:1, :] = abr
        ab_sc[1:2, :] = abi
        den = 1.0 / (ar * ar + ai * ai)
        cr = ((abr - 1.0) * ar + abi * ai) * den
        ci = (abi * ar - (abr - 1.0) * ai) * den
        grp_rows = lax.shift_right_logical(lax.broadcasted_iota(jnp.int32, (D_BRANCH, 1), 0), 4)
        grp_cols = lax.shift_right_logical(lax.broadcasted_iota(jnp.int32, (1, n), 1), 6)
        expand_b = lambda ref: jnp.where(grp_rows == grp_cols, jnp.concatenate([ref[0]] * (n // 128), axis=1), 0.0)
        bre, bim = expand_b(bre_ref), expand_b(bim_ref)
        wb_sc[:, 0:n] = (cr * bre - ci * bim).astype(BF16)
        wb_sc[:, n:2 * n] = (cr * bim + ci * bre).astype(BF16)
        if emit_y:
            st_rows = lax.shift_right_logical(lax.broadcasted_iota(jnp.int32, (n, 1), 0), 6)
            ch_cols = lax.shift_right_logical(lax.broadcasted_iota(jnp.int32, (1, D_BRANCH), 1), 4)
            expand_c = lambda ref: jnp.where(st_rows == ch_cols, jnp.concatenate([ref[0]] * 2, axis=1), 0.0)
            wc_sc[0:n, :] = expand_c(cre_ref).astype(BF16)
            wc_sc[n:2 * n, :] = (-expand_c(cim_ref)).astype(BF16)
        hc_sc[...] = hin_ref[0]

    steps = u_ref.shape[1]
    rows_c = nseq * steps

    @pl.when(c == 0)
    def _():
        i = lax.broadcasted_iota(jnp.int32, (rows_c, rows_c), 0)
        j = lax.broadcasted_iota(jnp.int32, (rows_c, rows_c), 1)
        p = lax.shift_right_logical(i, int(math.log2(nseq)))
        step = p + d * (steps - 1 - 2 * p)
        src = jnp.bitwise_and(i, nseq - 1) * steps + step
        perm_sc[...] = jnp.where(j == src, 1.0, 0.0).astype(BF16)

    lhs = _dot(perm_sc[...], u_ref[...].reshape(rows_c, D_BRANCH).astype(BF16)).astype(BF16)
    lb = 256
    y = None
    for j in range(n // lb):
        lr = slice(lb * j, lb * (j + 1))
        li = slice(n + lb * j, n + lb * (j + 1))
        bur = _dot(lhs, wb_sc[:, lr])
        bui = _dot(lhs, wb_sc[:, li])
        abr = ab_sc[0:1, lr]
        abi = ab_sc[1:2, lr]
        hr, hi = hc_sc[:, lr], hc_sc[:, li]
        for p in range(steps):
            rows = slice(p * nseq, (p + 1) * nseq)
            hr, hi = abr * hr - abi * hi + bur[rows], abr * hi + abi * hr + bui[rows]
            if emit_y:
                s_sc[rows, lr] = hr.astype(BF16)
                s_sc[rows, li] = hi.astype(BF16)
        hc_sc[:, lr] = hr
        hc_sc[:, li] = hi
        if emit_y:
            yj = _dot(s_sc[:, lr], wc_sc[lr, :]) + _dot(s_sc[:, li], wc_sc[li, :])
            y = yj if y is None else y + yj

    if emit_y:
        @pl.when(d == 0)
        def _():
            for p in range(steps):
                y_ref[0, :, p, :] = y[p * nseq:(p + 1) * nseq]

        @pl.when(d == 1)
        def _():
            for p in range(steps):
                y_ref[0, :, steps - 1 - p, :] = y[p * nseq:(p + 1) * nseq]

    @pl.when(c == pl.num_programs(1) - 1)
    def _():
        hfin_ref[0] = hc_sc[...]


def _s5_scan(u, hin, wl, emit_y):
    nseq, nstep, _ = u.shape
    steps = S5_ROWS // nseq
    nc = nstep // steps
    n = S5_N
    chunk = lambda d, c: c + d * (nc - 1 - 2 * c)
    per_dir = lambda shape: pl.BlockSpec((1,) + shape, lambda d, c: (d,) + (0,) * len(shape))
    args = [u, hin, wl["s5_are"], wl["s5_aim"], wl["s5_ldt"], wl["s5_bre"], wl["s5_bim"]]
    specs = [pl.BlockSpec((nseq, steps, D_BRANCH), lambda d, c: (0, chunk(d, c), 0)),
             per_dir((nseq, 2 * n)), per_dir((1, n)), per_dir((1, n)), per_dir((1, n)),
             per_dir((D_BRANCH, 128)), per_dir((D_BRANCH, 128))]
    out_specs = [per_dir((nseq, 2 * n))]
    out_shape = [jax.ShapeDtypeStruct((2, nseq, 2 * n), F32)]
    scratch = [pltpu.VMEM((D_BRANCH, 2 * n), BF16), pltpu.VMEM((8, n), F32),
               pltpu.VMEM((S5_ROWS, 2 * n), BF16), pltpu.VMEM((nseq, 2 * n), F32),
               pltpu.VMEM((S5_ROWS, S5_ROWS), BF16)]
    if emit_y:
        args += [wl["s5_cre"], wl["s5_cim"]]
        specs += [per_dir((n, 128)), per_dir((n, 128))]
        out_specs = [pl.BlockSpec((1, nseq, steps, D_BRANCH), lambda d, c: (d, 0, chunk(d, c), 0))] + out_specs
        out_shape = [jax.ShapeDtypeStruct((2, nseq, nstep, D_BRANCH), F32)] + out_shape
        scratch += [pltpu.VMEM((2 * n, D_BRANCH), BF16)]
    return pl.pallas_call(
        functools.partial(_s5_scan_kernel, nseq, emit_y),
        grid=(2, nc),
        in_specs=specs, out_specs=out_specs, out_shape=out_shape, scratch_shapes=scratch,
        compiler_params=_cp("arbitrary", "arbitrary"),
        name="s5_scan" if emit_y else "s5_scan_finals",
    )(*args)


def _s5_chain_kernel(batch, nseg, f_ref, h0_ref, are_ref, aim_ref, ldt_ref, o_ref):
    d = pl.program_id(0)
    n = S5_N
    _, _, pr, pi = _s5_discretise(are_ref, aim_ref, ldt_ref)
    for _ in range(int(math.log2(S5_SEG))):
        pr, pi = pr * pr - pi * pi, 2.0 * pr * pi
    f = f_ref[0]
    fr, fi = f[:, 0:n], f[:, n:2 * n]
    h0 = h0_ref[0]
    h0r, h0i = h0[:, 0:n], h0[:, n:2 * n]
    nrow = batch * nseg
    seg = jnp.bitwise_and(lax.broadcasted_iota(jnp.int32, (nrow, 1), 0), nseg - 1)

    def run(shift, keep):
        xr, xi = h0r, h0i
        for _ in range(nseg - 1):
            zr = fr + pr * xr - pi * xi
            zi = fi + pr * xi + pi * xr
            xr = h0r + jnp.where(keep, pltpu.roll(zr, shift, 0), 0.0)
            xi = h0i + jnp.where(keep, pltpu.roll(zi, shift, 0), 0.0)
        o_ref[0, :, 0:n] = xr
        o_ref[0, :, n:2 * n] = xi

    @pl.when(d == 0)
    def _():
        run(1, seg != 0)

    @pl.when(d == 1)
    def _():
        run(nrow - 1, seg != nseg - 1)


def _s5_chain(fin, h0rows, wl, batch, nseg):
    nrow = batch * nseg
    n = S5_N
    per_dir = lambda shape: pl.BlockSpec((1,) + shape, lambda d: (d,) + (0,) * len(shape))
    return pl.pallas_call(
        functools.partial(_s5_chain_kernel, batch, nseg),
        grid=(2,),
        in_specs=[per_dir((nrow, 2 * n)), per_dir((nrow, 2 * n)), per_dir((1, n)), per_dir((1, n)), per_dir((1, n))],
        out_specs=per_dir((nrow, 2 * n)),
        out_shape=jax.ShapeDtypeStruct((2, nrow, 2 * n), F32),
        compiler_params=_cp("arbitrary"),
        name="s5_chain",
    )(fin, h0rows, wl["s5_are"], wl["s5_aim"], wl["s5_ldt"])


def _gla_kernel(seq_len, nb, n_aliased, q_ref, k_ref, v_ref, g_ref, gw_ref, gb_ref, s0_ref, *rest):
    o_ref, sfin_ref, qe_sc, upd_sc, dec_sc, sall_sc, lhs_sc, kt_sc, la_sc, oi_sc = rest[n_aliased:]
    d = pl.program_id(1)
    sign = 1 - 2 * d
    ck, sup = GLA_CHUNK, GLA_SUPER
    cps = sup // ck
    nsup, nchunk = seq_len // sup, seq_len // ck
    dk, dv = GLA_HEADS * GLA_DK, GLA_HEADS * GLA_DV
    r = lax.broadcasted_iota(jnp.int32, (sup, sup), 0)
    s = lax.broadcasted_iota(jnp.int32, (sup, sup), 1)
    same = lax.shift_right_logical(r, 6) == lax.shift_right_logical(s, 6)
    tri = jnp.logical_and(same, (s - r) * sign <= 0)
    cum_lhs = jnp.where(tri, 1.0, 0.0).astype(BF16)
    t4 = lax.broadcasted_iota(jnp.int32, (ck, GLA_HEADS * ck), 0)
    s4 = jnp.bitwise_and(lax.broadcasted_iota(jnp.int32, (ck, GLA_HEADS * ck), 1), ck - 1)
    tri4 = (s4 - t4) * sign <= 0
    pos = jnp.bitwise_and(lax.broadcasted_iota(jnp.int32, (ck, 1), 0), ck - 1)
    is_last = pos == (ck - 1) * (1 - d)
    row_chunk = lax.shift_right_logical(lax.broadcasted_iota(jnp.int32, (sup, 1), 0), 6)
    head_k = lax.shift_right_logical(lax.broadcasted_iota(jnp.int32, (1, dk), 1), 5)
    head_v = lax.shift_right_logical(lax.broadcasted_iota(jnp.int32, (1, dv), 1), 6)
    blockdiag = lax.shift_right_logical(lax.broadcasted_iota(jnp.int32, (dv, 1), 0), 6) == head_k

    def group_rows(u):
        return pl.ds(u * sup, sup) if isinstance(u, int) else pl.ds(pl.multiple_of(u * sup, sup), sup)

    def stage_a(u, slot):
        rows = group_rows(u)
        q = q_ref[rows, :] * (GLA_DK ** -0.5)
        k = k_ref[rows, :]
        v = v_ref[rows, :]
        cs = _dot(cum_lhs, la_sc[rows, :])
        yield
        bc = cs[:, 0:dk] + cs[:, dk:2 * dk]
        tots = [jnp.sum(jnp.where(is_last, bc[c * ck:(c + 1) * ck], 0.0), axis=0, keepdims=True)
                for c in range(cps)]
        tot = jnp.concatenate([jnp.broadcast_to(t, (ck, dk)) for t in tots], axis=0)
        ref = 0.5 * tot
        kt_sc[slot] = (k * jnp.exp(ref - bc)).astype(BF16)
        lhs_sc[slot] = (q * jnp.exp(bc - ref)).astype(BF16)
        qe_sc[rows, :] = (q * jnp.exp(bc)).astype(BF16)
        kl = (k * jnp.exp(tot - bc)).astype(BF16)
        zero = jnp.zeros_like(kl)
        klx = jnp.concatenate([jnp.where(row_chunk == c, kl, zero) for c in range(cps)], axis=1)
        upd = _dot_tn(v.astype(BF16), klx)
        yield
        for c in range(cps):
            upd_sc[u * cps + c] = jnp.where(blockdiag, upd[:, c * dk:(c + 1) * dk], 0.0)
            dec_sc[u * cps + c] = jnp.broadcast_to(jnp.exp(tots[c]), (8, dk))

    def stage_b(u, slot):
        rows = group_rows(u)
        v = v_ref[rows, :].astype(BF16)
        qt, kt = lhs_sc[slot], kt_sc[slot]
        zk, zv = jnp.zeros_like(kt[0:ck]), jnp.zeros_like(v[0:ck])
        p = [_dot_nt(qt[c * ck:(c + 1) * ck],
                     jnp.concatenate([jnp.where(head_k == h, kt[c * ck:(c + 1) * ck], zk)
                                      for h in range(GLA_HEADS)], axis=0)) for c in range(cps)]
        yield
        o = [_dot(jnp.where(tri4, p[c], 0.0).astype(BF16),
                  jnp.concatenate([jnp.where(head_v == h, v[c * ck:(c + 1) * ck], zv)
                                   for h in range(GLA_HEADS)], axis=0)) for c in range(cps)]
        yield
        oi_sc[rows, :] = jnp.concatenate(o, axis=0)

    def run(*stages):
        live = list(stages)
        while live:
            for g in list(live):
                if next(g, StopIteration) is StopIteration:
                    live.remove(g)

    x = _dot(g_ref[...].astype(BF16), gw_ref[0]) + gb_ref[0]
    la = (jnp.minimum(x, 0.0) - jnp.log(1.0 + jnp.exp(-jnp.abs(x)))) * (1.0 / GLA_TAU)
    la_sc[...] = jnp.concatenate(_split2(la), axis=1)

    nu = nb * nsup
    if nu <= 4:
        run(*[stage_a(u, u) for u in range(min(2, nu))])
        for u in range(0, nu, 2):
            run(*([stage_b(v, v % 4) for v in range(u, min(u + 2, nu))]
                  + [stage_a(v, v % 4) for v in range(u + 2, min(u + 4, nu))]))
    else:
        run(stage_a(0, 0), stage_a(1, 1))

        def sup_body(t, carry):
            u = 2 * t
            n0, n1 = jnp.minimum(u + 2, nu - 2), jnp.minimum(u + 3, nu - 1)
            run(stage_b(u, jnp.bitwise_and(u, 3)), stage_b(u + 1, jnp.bitwise_and(u + 1, 3)),
                stage_a(n0, jnp.bitwise_and(u + 2, 3)), stage_a(n1, jnp.bitwise_and(u + 3, 3)))
            return carry

        lax.fori_loop(0, nu // 2, sup_body, 0)

    for j in range(nb):
        def state_body(c, st, j=j):
            ci = j * nchunk + c + d * (nchunk - 1 - 2 * c)
            sall_sc[ci] = st.astype(BF16)
            return dec_sc[ci][0:1, :] * st + upd_sc[ci]

        st_fin = jnp.transpose(lax.fori_loop(0, nchunk, state_body, s0_ref[j, 0]))
        for h in range(GLA_HEADS):
            sfin_ref[j, 0, 0, h] = st_fin[h * GLA_DK:(h + 1) * GLA_DK, h * GLA_DV:(h + 1) * GLA_DV]

    def inter(u):
        rows = group_rows(u)
        qe = qe_sc[rows, :]
        oi = jnp.concatenate([_dot_nt(qe[c * ck:(c + 1) * ck], sall_sc[u * cps + c]) for c in range(cps)], axis=0)
        o_ref[0, rows, :] = (oi_sc[rows, :] + oi).astype(BF16)

    if nu <= 4:
        for u in range(nu):
            inter(u)
    else:
        lax.fori_loop(0, nu, lambda u, carry: (inter(u), carry)[1], 0, unroll=2 if nu % 2 == 0 else 1)


def _gla(gla_in, mla_in, wl, s0t, batch, seq_len, fin=(0, None, 1)):
    layer, prev_fin, fin_layers = fin
    aliases = {} if prev_fin is None else {7: 1}
    extra = [] if prev_fin is None else [prev_fin]
    n = gla_in.shape[0]
    dk, dv = GLA_HEADS * GLA_DK, GLA_HEADS * GLA_DV
    nb = max(1, min(batch, 1024 // seq_len))
    rows = nb * seq_len
    nchunk = nb * (seq_len // GLA_CHUNK)
    return pl.pallas_call(
        functools.partial(_gla_kernel, seq_len, nb, len(extra)),
        grid=(batch // nb, 2),
        input_output_aliases=aliases,
        in_specs=[pl.BlockSpec((rows, dk), lambda b, d: (b, 0)),
                  pl.BlockSpec((rows, dk), lambda b, d: (b, 1)),
                  pl.BlockSpec((rows, dv), lambda b, d: (b, 1)),
                  pl.BlockSpec((rows, dk), lambda b, d: (b, 1)),
                  pl.BlockSpec((1, dk, dk), lambda b, d: (d, 0, 0)),
                  pl.BlockSpec((1, 1, dk), lambda b, d: (d, 0, 0)),
                  pl.BlockSpec((nb, 1, dv, dk), lambda b, d: (b, d, 0, 0))]
                 + [pl.BlockSpec(memory_space=pl.ANY)] * len(extra),
        out_specs=[pl.BlockSpec((1, rows, dv), lambda b, d: (d, b, 0)),
                   pl.BlockSpec((nb, 1, 1, GLA_HEADS, GLA_DK, GLA_DV), lambda b, d: (b, layer, d, 0, 0, 0))],
        out_shape=[jax.ShapeDtypeStruct((2, n, dv), BF16),
                   jax.ShapeDtypeStruct((batch, fin_layers, 2, GLA_HEADS, GLA_DK, GLA_DV), F32)],
        scratch_shapes=[pltpu.VMEM((rows, dk), BF16),
                        pltpu.VMEM((nchunk, dv, dk), F32),
                        pltpu.VMEM((nchunk, 8, dk), F32),
                        pltpu.VMEM((nchunk, dv, dk), BF16),
                        pltpu.VMEM((4, GLA_SUPER, dk), BF16),
                        pltpu.VMEM((4, GLA_SUPER, dk), BF16),
                        pltpu.VMEM((rows, 2 * dk), BF16),
                        pltpu.VMEM((rows, dv), F32)],
        compiler_params=_cp("arbitrary", "arbitrary"),
        name="gla",
    )(gla_in, gla_in, gla_in, mla_in, wl["gla_gw"], wl["gla_gb"], s0t, *extra)


def _outproj_kernel(x_ref, mod_ref, g_ref, om_ref, oh_ref, su_ref, sf_ref, sb_ref, sd_ref, sw_ref, sbias_ref,
                    gf_ref, gb_ref, gn_ref, hm_ref, w_ref, y_ref):
    c = D_BRANCH
    g = g_ref[...].astype(F32)
    acc = _dot((om_ref[...].astype(F32) * _silu(g[:, 0:c])).astype(BF16), w_ref[0:c, :])
    acc += _dot((oh_ref[...].astype(F32) * _silu(g[:, c:2 * c])).astype(BF16), w_ref[c:2 * c, :])
    ys = sd_ref[...] * su_ref[...] + sf_ref[0] + sb_ref[0]
    ge = 0.5 * ys * (1.0 + jnp.tanh(math.sqrt(2.0 / math.pi) * (ys + 0.044715 * (ys * ys * ys))))
    o_s5 = ge / (1.0 + jnp.exp(-(_dot(ge.astype(BF16), sw_ref[...]) + sbias_ref[...])))
    acc += _dot((o_s5 * _silu(g[:, 2 * c:3 * c])).astype(BF16), w_ref[2 * c:3 * c, :])
    og = gf_ref[0].astype(F32) + gb_ref[0].astype(F32)
    hi, lo = _split2(og * og)
    ms = _dot(hi, hm_ref[...]) + _dot(lo, hm_ref[...])
    ogn = og * lax.rsqrt(ms + EPS) * gn_ref[...]
    acc += _dot((ogn * _silu(g[:, 3 * c:4 * c])).astype(BF16), w_ref[3 * c:4 * c, :])
    y_ref[...] = x_ref[...] + mod_ref[0, 2:3, :] * acc


def _outproj(x, mod, mod_row, gates, o_mla, o_hy, s5_u, s5_y, o_gla, wl, tm):
    n, d = x.shape
    c = D_BRANCH
    row = lambda w: pl.BlockSpec((tm, w), lambda i: (i, 0))
    per_dir = lambda k: pl.BlockSpec((1, tm, c), lambda i: (k, i, 0))
    full = lambda *shape: pl.BlockSpec(shape, lambda i: (0,) * len(shape))
    return pl.pallas_call(
        _outproj_kernel,
        grid=(n // tm,),
        in_specs=[row(d),
                  pl.BlockSpec((1, 3, d), lambda i: (mod_row(i * tm), 0, 0)),
                  row(d), row(c), row(c),
                  row(c), per_dir(0), per_dir(1), full(1, c), full(c, c), full(1, c),
                  per_dir(0), per_dir(1), full(1, c), full(c, c), full(d, d)],
        out_specs=row(d),
        out_shape=jax.ShapeDtypeStruct((n, d), F32),
        compiler_params=_cp("arbitrary"),
        name="outproj",
    )(x, mod, gates, o_mla, o_hy, s5_u, s5_y, s5_y, wl["s5_d"], wl["s5_glu_w"], wl["s5_glu_b"],
      o_gla, o_gla, wl["gla_norm"], wl["head_mean"], wl["w_out"])


def _rope_tables(seq_len):
    pos = np.arange(seq_len)
    inv = ROPE_BASE ** (-np.arange(0, 16, 2, dtype=np.float64) / 16.0)
    cos = np.ones((seq_len, HEAD_PAD))
    sin_a = np.zeros((seq_len, HEAD_PAD))
    sin_b = np.zeros((seq_len, HEAD_PAD))
    for base, p in ((MLA_NOPE, pos // GRID_W), (MLA_NOPE + 16, pos % GRID_W)):
        ang = p[:, None].astype(np.float64) * inv[None, :]
        cos[:, base:base + 8] = np.cos(ang)
        cos[:, base + 8:base + 16] = np.cos(ang)
        sin_a[:, base:base + 8] = -np.sin(ang)
        sin_b[:, base + 8:base + 16] = np.sin(ang)
    return tuple(jnp.asarray(t, F32) for t in (cos, sin_a, sin_b))


def _odd_dft(seq_len):
    bk = min(seq_len, HY_BLOCK)
    k = np.arange(bk)[:, None]
    t = np.arange(bk)[None, :]

    def mat(shift):
        ang = (np.pi / (2 * bk)) * (((2 * k + 1) * (t + shift)) % (4 * bk))
        return np.concatenate([np.cos(ang), -np.sin(ang)], axis=0)

    fo = mat(0)
    fb = -mat(bk)
    fb[:, 0] = 0.0
    const = lambda a: jnp.asarray(a, F32).astype(BF16)
    return const(fo), const(fb), const(fo.T)


def _hyena_tables(seq_len):
    lag = np.arange(-seq_len, seq_len)
    pos = np.where(lag == -seq_len, 0, np.abs(lag)).astype(np.float64)
    t = pos / seq_len
    w = 2.0 * np.pi * pos / seq_len
    bands = np.linspace(1e-4, HY_BANDS - 1, HY_BANDS)
    feat = np.zeros((2 * seq_len, HY_HIDDEN))
    feat[:, 0] = t
    feat[:, 1:1 + HY_BANDS] = np.cos(w[:, None] * bands)
    feat[:, 1 + HY_BANDS:HY_FEAT] = np.sin(w[:, None] * bands)
    feat = feat.reshape(-1, 2, 128, HY_HIDDEN).transpose(0, 2, 1, 3).reshape(seq_len, 2 * HY_HIDDEN)
    deltas = np.linspace(math.log(1.0 / HY_TARGET) / HY_FAST_DECAY, math.log(1.0 / HY_TARGET) / HY_SLOW_DECAY,
                         D_BRANCH)
    win = np.exp(-t[:, None] * deltas[None, :]) + HY_SHIFT
    return jnp.asarray(feat, F32), jnp.asarray(win, F32)


def _pad_to(a, shape):
    return jnp.pad(a, [(0, s - d) for s, d in zip(shape, a.shape)])


def _layer_weights(l, p):
    z = lambda *s: jnp.zeros(s, F32)
    d = D_MODEL
    wl = {"layer": l, "w_in": p["w_in"], "norm_w": p["norm_w"][l].reshape(1, d), "w_out": p["w_out"][l].astype(BF16)}
    wl["qa_norm"] = _pad_to(p["mla_qa_norm"][l].reshape(1, -1), (1, 256))
    w_uq = _pad_to(p["mla_w_uq"][l].reshape(MLA_Q_RANK, MLA_HEADS, MLA_QK), (256, MLA_HEADS, HEAD_PAD))
    wl["w_uq"] = w_uq.reshape(256, MLA_HEADS * HEAD_PAD).astype(BF16)
    wl["q_norm"] = _pad_to(p["mla_q_norm"][l].reshape(1, -1), (1, HEAD_PAD))
    wl["k_norm"] = _pad_to(p["mla_k_norm"][l].reshape(1, -1), (1, HEAD_PAD))
    wl["kva_norm"] = p["mla_kva_norm"][l].reshape(1, -1)
    w_ukv = p["mla_w_ukv"][l].reshape(MLA_KV_RANK, MLA_HEADS, MLA_NOPE + MLA_V)
    wl["w_uk"] = _pad_to(w_ukv[:, :, :MLA_NOPE], (MLA_KV_RANK, MLA_HEADS, HEAD_PAD)).reshape(MLA_KV_RANK, -1).astype(BF16)
    wl["w_uv"] = w_ukv[:, :, MLA_NOPE:].reshape(MLA_KV_RANK, MLA_HEADS * MLA_V).astype(BF16)
    wl["hy_conv_w"] = p["hy_conv_w"][l]
    wl["hy_conv_b"] = p["hy_conv_b"][l].reshape(1, -1)
    hh = HY_HIDDEN
    twice = lambda a: jnp.tile(a.reshape(1, -1), (1, 2))
    bdiag = lambda a: jnp.concatenate([jnp.pad(a, ((0, 0), (0, a.shape[1]))), jnp.pad(a, ((0, 0), (a.shape[1], 0)))], axis=0)
    wl["hy_w1"] = bdiag(_pad_to(p["hy_w1"][l], (hh, hh)))
    wl["hy_b1"], wl["hy_f1"] = twice(p["hy_b1"][l]), twice(p["hy_freq1"][l])
    wl["hy_w2"] = bdiag(p["hy_w2"][l])
    wl["hy_b2"], wl["hy_f2"] = twice(p["hy_b2"][l]), twice(p["hy_freq2"][l])
    w3 = p["hy_w3"][l]
    wl["hy_w3"] = jnp.concatenate([bdiag(w3[:, 0:512]), bdiag(w3[:, 512:1024])], axis=1)
    wl["hy_bias"] = p["hy_bias"][l]
    flat = lambda a: a[l].reshape(2, 1, S5_N)
    wl["s5_are"], wl["s5_aim"] = flat(p["s5_a_re"]), flat(p["s5_a_im"])
    wl["s5_ldt"] = jnp.repeat(p["s5_log_dt"][l], S5_STATE, axis=-1).reshape(2, 1, S5_N)
    lanes_b = lambda a: jnp.tile(a[l].transpose(0, 1, 3, 2).reshape(2, D_BRANCH, S5_STATE), (1, 1, 128 // S5_STATE))
    lanes_c = lambda a: jnp.tile(a[l].transpose(0, 1, 3, 2).reshape(2, S5_N, S5_GROUP), (1, 1, 128 // S5_GROUP))
    wl["s5_bre"], wl["s5_bim"] = lanes_b(p["s5_b_re"]), lanes_b(p["s5_b_im"])
    wl["s5_cre"], wl["s5_cim"] = lanes_c(p["s5_c_re"]), lanes_c(p["s5_c_im"])
    wl["s5_d"] = p["s5_d"][l].reshape(1, -1)
    wl["s5_glu_w"] = p["s5_glu_w"][l].astype(BF16)
    wl["s5_glu_b"] = p["s5_glu_b"][l].reshape(1, -1)
    gw = p["gla_gw"][l]
    dk = GLA_HEADS * GLA_DK
    wl["gla_gw"] = jnp.stack([_pad_to(jnp.pad(gw[i], ((GLA_G_LANE + GLA_RANK * i, 0), (0, 0))), (dk, dk))
                              for i in range(2)]).astype(BF16)
    wl["gla_gb"] = p["gla_gb"][l].reshape(2, 1, dk)
    wl["gla_norm"] = jnp.tile(p["gla_norm"][l], GLA_HEADS).reshape(1, -1)
    head = np.arange(D_BRANCH) // GLA_DV
    wl["head_mean"] = jnp.asarray((head[:, None] == head[None, :]) / GLA_DV, BF16)
    return wl


def _hyena_filters(wl, tabs):
    feat, win, fo, fb, _ = tabs
    kern_lin, nrm = _hy_mlp(feat, wl, win)
    return _hy_kspec(kern_lin, nrm, fo, fb)


def _trunk_layer(x, mod, mod_row, wl, batch, seq_len, hy_tabs, rope_tabs=None, ctx=None, layer=0, cache_bufs=None):
    n = batch * seq_len
    tm = 512
    if ctx is None:
        mla_in, gates, hy_in, s5_in, gla_in, q, k, v, ckv, krope = _inproj(
            x, mod, mod_row, wl, tm, seq_len, (layer, cache_bufs[:2]))
        kv_parts = [(k, v, seq_len)]
    else:
        mla_in, gates, hy_in, s5_in, gla_in = _inproj(x, mod, mod_row, wl, tm, seq_len)
        q, k, v = _mla_prep(mla_in, wl, rope_tabs, seq_len, tm, True)
        ckv = krope = None
        k_ctx, v_ctx = _mla_prep(ctx["mla"], wl, None, ctx["past"], 512, False)
        kv_parts = [(k_ctx, v_ctx, ctx["past"]), (k, v, seq_len)]
    o_mla = _attention(q, kv_parts, batch, seq_len, min(seq_len, 512))

    o_hy = _hyena(hy_in, wl, _hyena_filters(wl, hy_tabs), hy_tabs[2], hy_tabs[4], batch, seq_len)

    nseg = seq_len // S5_SEG
    nseq = batch * nseg
    u_seg = s5_in.reshape(nseq, S5_SEG, D_BRANCH)
    if ctx is None:
        hin = jnp.zeros((2, nseq, 2 * S5_N), F32)
    else:
        (fin,) = _s5_scan(u_seg, jnp.zeros((2, nseq, 2 * S5_N), F32), wl, False)
        hin = _s5_chain(fin, ctx["s5_h0"], wl, batch, nseg)
    y2, s5_fin = _s5_scan(u_seg, hin, wl, True)

    s0 = jnp.zeros((batch, 2, GLA_HEADS * GLA_DV, GLA_HEADS * GLA_DK), F32) if ctx is None else ctx["gla_s0"]
    fin = (0, None, 1) if ctx is not None else (layer, None if cache_bufs is None else cache_bufs[2], DEPTH)
    o_gla, gla_fin = _gla(gla_in, mla_in, wl, s0, batch, seq_len, fin)

    y = _outproj(x, mod, mod_row, gates, o_mla, o_hy, s5_in, y2.reshape(2, n, D_BRANCH), o_gla, wl, 2 * tm)
    return y, (ckv, krope, s5_fin, gla_fin)


def kernel(x_prompt, x_sample, c, cache_mla_ckv, cache_mla_krope, state_s5, state_gla, c_ctx, norm_w, ada_w, ada_b, w_in, w_out, mla_qa_norm, mla_kva_norm, mla_w_uq, mla_w_ukv, mla_q_norm, mla_k_norm, hy_conv_w, hy_conv_b, hy_w1, hy_b1, hy_freq1, hy_w2, hy_b2, hy_freq2, hy_w3, hy_bias, s5_a_re, s5_a_im, s5_log_dt, s5_b_re, s5_b_im, s5_c_re, s5_c_im, s5_d, s5_glu_w, s5_glu_b, gla_gw, gla_gb, gla_norm):
    params = dict(norm_w=norm_w, w_in=w_in, w_out=w_out, mla_qa_norm=mla_qa_norm, mla_kva_norm=mla_kva_norm,
                  mla_w_uq=mla_w_uq, mla_w_ukv=mla_w_ukv, mla_q_norm=mla_q_norm, mla_k_norm=mla_k_norm,
                  hy_conv_w=hy_conv_w, hy_conv_b=hy_conv_b, hy_w1=hy_w1, hy_b1=hy_b1, hy_freq1=hy_freq1,
                  hy_w2=hy_w2, hy_b2=hy_b2, hy_freq2=hy_freq2, hy_w3=hy_w3, hy_bias=hy_bias,
                  s5_a_re=s5_a_re, s5_a_im=s5_a_im, s5_log_dt=s5_log_dt, s5_b_re=s5_b_re, s5_b_im=s5_b_im,
                  s5_c_re=s5_c_re, s5_c_im=s5_c_im, s5_d=s5_d, s5_glu_w=s5_glu_w, s5_glu_b=s5_glu_b,
                  gla_gw=gla_gw, gla_gb=gla_gb, gla_norm=gla_norm)
    bp, lp, d = x_prompt.shape
    bs, ls, _ = x_sample.shape
    past = cache_mla_ckv.shape[2]
    n_s5 = S5_N

    conds = jnp.concatenate([c_ctx[None, :], c, jnp.zeros((8 - 1 - bs, d), F32)], axis=0)
    mods = _modulation(conds, ada_w, ada_b).reshape(DEPTH, 8, 3, d)

    tabs_p = _hyena_tables(lp) + _odd_dft(lp)
    tabs_s = _hyena_tables(ls) + _odd_dft(ls)
    rope_tabs = _rope_tables(ls)
    nseg = ls // S5_SEG

    y_p = x_prompt.reshape(bp * lp, d)
    y_s = x_sample.reshape(bs * ls, d)
    s5_l = []
    cache_bufs = (jnp.zeros((bp, DEPTH, lp, MLA_KV_RANK), F32), jnp.zeros((bp, DEPTH, lp, MLA_ROPE), F32),
                  jnp.zeros((bp, DEPTH, 2, GLA_HEADS, GLA_DK, GLA_DV), F32))
    for l in range(DEPTH):
        wl = _layer_weights(l, params)
        y_p, (ckv, krope, s5_fin, gla_fin) = _trunk_layer(y_p, mods[l], lambda row: 0, wl, bp, lp, tabs_p,
                                                          layer=l, cache_bufs=cache_bufs)
        cache_bufs = (ckv, krope, gla_fin)
        s5_l.append(jnp.stack([s5_fin[:, :, :n_s5], s5_fin[:, :, n_s5:]], axis=-1)
                    .reshape(2, bp, S5_GROUPS, S5_STATE, 2).transpose(1, 0, 2, 3, 4))

        mla_ctx = jnp.concatenate([cache_mla_ckv[:, l], jnp.zeros((bs, past, 64), F32), cache_mla_krope[:, l],
                                   jnp.zeros((bs, past, 32), F32)], axis=-1).reshape(bs * past, 256)
        st = state_s5[:, l]
        h0 = jnp.concatenate([st[..., 0].reshape(bs, 2, n_s5), st[..., 1].reshape(bs, 2, n_s5)], axis=-1)
        h0 = h0.transpose(1, 0, 2)
        h0rows = jnp.zeros((2, nseg * bs, 2 * n_s5), F32)
        h0rows = h0rows.at[0, 0::nseg].set(h0[0]).at[1, nseg - 1::nseg].set(h0[1])
        eye_h = jnp.eye(GLA_HEADS, dtype=F32)
        gla_s0 = jnp.einsum("bdhke,hg->bdhegk", state_gla[:, l], eye_h).reshape(
            bs, 2, GLA_HEADS * GLA_DV, GLA_HEADS * GLA_DK)
        ctx = {"mla": mla_ctx, "past": past, "s5_h0": h0rows, "gla_s0": gla_s0}
        y_s, _ = _trunk_layer(y_s, mods[l], lambda row: 1 + row // ls, wl, bs, ls, tabs_s, rope_tabs, ctx)

    return (y_p.reshape(bp, lp, d), y_s.reshape(bs, ls, d),
            cache_bufs[0], cache_bufs[1], jnp.stack(s5_l, axis=1), cache_bufs[2])
```

```python
import functools
import math

import numpy as np
import jax
import jax.numpy as jnp
from jax import lax
from jax.experimental import pallas as pl
from jax.experimental.pallas import tpu as pltpu

F32 = jnp.float32
BF16 = jnp.bfloat16

D_MODEL = 1024
DEPTH = 2
GRID_W = 64
D_BRANCH = 256
EPS = 1e-6

MLA_HEADS = 4
MLA_Q_RANK = 192
MLA_KV_RANK = 128
MLA_NOPE = 64
MLA_ROPE = 32
MLA_QK = 96
MLA_V = 64
ROPE_BASE = 10000.0
HEAD_PAD = 128

HY_BANDS = 16
HY_FEAT = 33
HY_HIDDEN = 64
HY_SHIFT = 0.05
HY_FAST_DECAY = 0.3
HY_SLOW_DECAY = 1.5
HY_TARGET = 1e-2
HY_BLOCK = 512

S5_GROUP = 16
S5_GROUPS = 16
S5_STATE = 64
S5_N = S5_GROUPS * S5_STATE
S5_ROWS = 512
S5_SEG = 256

GLA_HEADS = 4
GLA_DK = 32
GLA_DV = 64
GLA_RANK = 16
GLA_TAU = 16.0
GLA_CHUNK = 64
GLA_SUPER = 256

SEG_MLA = (0, 384)
SEG_GATE = (384, 1408)
SEG_HY = (1408, 2176)
SEG_S5 = (2176, 2432)
SEG_GLA = (2432, 2944)
N_PROJ = 2944
W_IN_ORDER = ((0, 192), (320, 352), (2656, 2688), (192, 320),
              (352, 608), (1376, 1632), (1888, 2144), (2688, 2944),
              (608, 1376), (1632, 1888), (2144, 2656))
GLA_G_LANE = 96

VMEM_LIMIT = 48 * 1024 * 1024


def _cp(*sem):
    return pltpu.CompilerParams(dimension_semantics=sem, vmem_limit_bytes=VMEM_LIMIT)


def _dot(a, b):
    return jnp.dot(a, b, preferred_element_type=F32)


def _dot_nt(a, b):
    return lax.dot_general(a, b, (((1,), (1,)), ((), ())), preferred_element_type=F32)


def _dot_tn(a, b):
    return lax.dot_general(a, b, (((0,), (0,)), ((), ())), preferred_element_type=F32)


def _split2(x):
    hi = x.astype(BF16)
    lo = (x - hi.astype(F32)).astype(BF16)
    return hi, lo


def _split3(x):
    h1 = x.astype(BF16)
    r1 = x - h1.astype(F32)
    h2 = r1.astype(BF16)
    h3 = (r1 - h2.astype(F32)).astype(BF16)
    return h1, h2, h3


def _dot3(a, b):
    a1, a2 = _split2(a)
    b1, b2 = _split2(b)
    return _dot(a1, b1) + (_dot(a1, b2) + _dot(a2, b1))


def _silu(z):
    return z / (1.0 + jnp.exp(-z))


def _mod_kernel(c_ref, w_ref, b_ref, o_ref):
    s = _silu(c_ref[...])
    o_ref[0] = _dot(s.astype(BF16), w_ref[0].astype(BF16)) + b_ref[0]


def _modulation(conds, ada_w, ada_b):
    d = D_MODEL
    return pl.pallas_call(
        _mod_kernel,
        grid=(DEPTH, 3),
        in_specs=[pl.BlockSpec((8, d), lambda l, j: (0, 0)),
                  pl.BlockSpec((1, d, d), lambda l, j: (l, 0, j)),
                  pl.BlockSpec((1, 1, d), lambda l, j: (l, 0, j))],
        out_specs=pl.BlockSpec((1, 8, d), lambda l, j: (l, 0, j)),
        out_shape=jax.ShapeDtypeStruct((DEPTH, 8, 3 * d), F32),
        compiler_params=_cp("arbitrary", "arbitrary"),
        name="modulation",
    )(conds, ada_w, ada_b.reshape(DEPTH, 1, 3 * d))


def _head_norm(xh, w):
    ms = jnp.sum(xh * xh, axis=-1, keepdims=True) * (1.0 / MLA_QK)
    return xh * lax.rsqrt(ms + EPS) * w


def _rope(xh, cos, sin_a, sin_b):
    return xh * cos + pltpu.roll(xh, HEAD_PAD - 8, 1) * sin_a + pltpu.roll(xh, 8, 1) * sin_b


def _mla_steps(m, has_q, rope, cache_seqs, n_aliased, refs):
    refs = list(refs)
    if has_q:
        qan_ref, wuq_ref, qn_ref, kvn_ref = refs[:4]
        refs = refs[4:]
    wuk_ref, wuv_ref, kn_ref = refs[:3]
    refs = refs[3:]
    if rope:
        cos_ref, sa_ref, sb_ref = refs[:3]
        refs = refs[3:]
        cos, sa, sb = cos_ref[...], sa_ref[...], sb_ref[...]
    refs = refs[n_aliased:]
    if has_q:
        q_ref = refs.pop(0)
    k_ref, v_ref = refs[:2]
    if cache_seqs:
        ckv_ref, kro_ref = refs[2:]
    if has_q:
        lane = lax.broadcasted_iota(jnp.int32, (1, HEAD_PAD), 1)
        mixed = m[:, 128:256]
        cq = jnp.concatenate([m[:, 0:128], jnp.where(lane < MLA_NOPE, mixed, 0.0)], axis=1)
        ms = jnp.sum(cq * cq, axis=-1, keepdims=True) * (1.0 / MLA_Q_RANK)
        cqn = cq * lax.rsqrt(ms + EPS) * qan_ref[...]
        ckv = m[:, 256:384]
        ckvn = ckv * lax.rsqrt(jnp.mean(ckv * ckv, axis=-1, keepdims=True) + EPS) * kvn_ref[...]
        yield
        q = _dot(cqn.astype(BF16), wuq_ref[...])
        kr = jnp.where(jnp.logical_and(lane >= MLA_NOPE, lane < MLA_NOPE + MLA_ROPE), mixed, 0.0)
        if cache_seqs:
            seq_len = ckv_ref.shape[2]
            for s in range(cache_seqs):
                ckv_ref[s, 0] = ckvn[s * seq_len:(s + 1) * seq_len]
                kro_ref[s, 0] = kr[s * seq_len:(s + 1) * seq_len, MLA_NOPE:MLA_NOPE + MLA_ROPE]
    else:
        ckvn = m[:, 0:128]
        kr = m[:, 128:256]
    cb = ckvn.astype(BF16)
    kup = _dot(cb, wuk_ref[...])
    v_ref[...] = _dot(cb, wuv_ref[...]).astype(BF16)
    yield
    for h in range(MLA_HEADS):
        sl = slice(HEAD_PAD * h, HEAD_PAD * (h + 1))
        kh = _head_norm(kup[:, sl] + kr, kn_ref[...])
        if rope:
            kh = _rope(kh, cos, sa, sb)
        k_ref[:, sl] = kh.astype(BF16)
        if has_q:
            qh = _head_norm(q[:, sl], qn_ref[...])
            if rope:
                qh = _rope(qh, cos, sa, sb)
            q_ref[:, sl] = (qh * (MLA_QK ** -0.5)).astype(BF16)
        if h % 2 == 1:
            yield


def _mla_prep_kernel(has_q, rope, m_ref, *refs):
    for _ in _mla_steps(m_ref[...], has_q, rope, 0, 0, refs):
        pass


def _inproj_kernel(fuse_mla, cache_seqs, n_aliased, x_ref, mod_ref, nw_ref, w_ref, *refs):
    n_mla_in = 7 + n_aliased if fuse_mla else 0
    mla_refs, (o_mla, o_g, o_hy, o_s5, o_gla) = refs[:n_mla_in], refs[n_mla_in:n_mla_in + 5]
    w_src, w_ref = w_ref, refs[-1]

    @pl.when(pl.program_id(0) == 0)
    def _():
        dst = 0
        for lo, hi in W_IN_ORDER:
            w_ref[:, dst:dst + hi - lo] = w_src[0, :, lo:hi].astype(BF16)
            dst += hi - lo

    x = x_ref[...]
    ms = jnp.mean(x * x, axis=-1, keepdims=True)
    y = x * lax.rsqrt(ms + EPS) * nw_ref[...]
    h = (y * (1.0 + mod_ref[0, 1:2, :]) + mod_ref[0, 0:1, :]).astype(BF16)
    project = lambda o, seg: o.__setitem__(Ellipsis, _dot(h, w_ref[:, seg[0]:seg[1]]).astype(o.dtype))
    m = _dot(h, w_ref[:, SEG_MLA[0]:SEG_MLA[1]])
    o_mla[...] = m
    steps = iter(()) if not fuse_mla else _mla_steps(
        m, True, False, cache_seqs, n_aliased, list(mla_refs) + list(refs[n_mla_in + 5:-1]))
    project(o_g, SEG_GATE)
    next(steps, None)
    next(steps, None)
    project(o_hy, SEG_HY)
    next(steps, None)
    project(o_s5, SEG_S5)
    project(o_gla, SEG_GLA)
    for _ in steps:
        pass


def _inproj(x, mod, mod_row, wl, tm, seq_len, cache=None):
    n, d = x.shape
    fuse_mla = cache is not None
    full = lambda shape: pl.BlockSpec(shape, lambda i: (0,) * len(shape))
    row = lambda w: pl.BlockSpec((tm, w), lambda i: (i, 0))
    args = [x, mod, wl["norm_w"], wl["w_in"]]
    specs = [row(d), pl.BlockSpec((1, 3, d), lambda i: (mod_row(i * tm), 0, 0)), full((1, d)),
             pl.BlockSpec((1, d, N_PROJ), lambda i: (wl["layer"], 0, 0), pipeline_mode=pl.Buffered(1))]
    widths = [hi - lo for lo, hi in (SEG_MLA, SEG_GATE, SEG_HY, SEG_S5, SEG_GLA)]
    dtypes = [F32, BF16, F32, F32, F32]
    aliases = {}
    nseq = 0
    if fuse_mla:
        args += [wl["qa_norm"], wl["w_uq"], wl["q_norm"], wl["kva_norm"], wl["w_uk"], wl["w_uv"], wl["k_norm"]]
        specs += [full((1, 256)), full((256, 512)), full((1, 128)), full((1, 128)),
                  full((128, 512)), full((128, 256)), full((1, 128))]
        widths += [512, 512, 256]
        dtypes += [BF16, BF16, BF16]
    out_specs = [row(w) for w in widths]
    out_shape = [jax.ShapeDtypeStruct((n, w), t) for w, t in zip(widths, dtypes)]
    if fuse_mla:
        layer, prev = cache
        nseq = tm // seq_len
        for w in (MLA_KV_RANK, MLA_ROPE):
            out_specs.append(pl.BlockSpec((nseq, 1, seq_len, w), lambda i: (i, layer, 0, 0)))
            out_shape.append(jax.ShapeDtypeStruct((n // seq_len, DEPTH, seq_len, w), F32))
        for k, buf in enumerate(prev):
            aliases[len(args)] = len(out_shape) - 2 + k
            args.append(buf)
            specs.append(pl.BlockSpec(memory_space=pl.ANY))
    return pl.pallas_call(
        functools.partial(_inproj_kernel, fuse_mla, nseq, len(aliases)),
        grid=(n // tm,),
        in_specs=specs, out_specs=out_specs, out_shape=out_shape,
        input_output_aliases=aliases,
        scratch_shapes=[pltpu.VMEM((d, N_PROJ), BF16)],
        compiler_params=_cp("arbitrary"),
        name="inproj",
    )(*args)


def _mla_prep(m, wl, rope_tabs, seq_len, tm, has_q):
    n, wm = m.shape
    rope = rope_tabs is not None
    full = lambda shape: pl.BlockSpec(shape, lambda i: (0,) * len(shape))
    row = lambda w: pl.BlockSpec((tm, w), lambda i: (i, 0))
    args, specs = [m], [row(wm)]
    if has_q:
        args += [wl["qa_norm"], wl["w_uq"], wl["q_norm"], wl["kva_norm"]]
        specs += [full((1, 256)), full((256, 512)), full((1, 128)), full((1, 128))]
    args += [wl["w_uk"], wl["w_uv"], wl["k_norm"]]
    specs += [full((128, 512)), full((128, 256)), full((1, 128))]
    if rope:
        nt = seq_len // tm
        args += list(rope_tabs)
        specs += [pl.BlockSpec((tm, HEAD_PAD), lambda i: (i % nt, 0))] * 3
    widths = ([512] if has_q else []) + [512, 256]
    return pl.pallas_call(
        functools.partial(_mla_prep_kernel, has_q, rope),
        grid=(n // tm,),
        in_specs=specs, out_specs=[row(w) for w in widths],
        out_shape=[jax.ShapeDtypeStruct((n, w), BF16) for w in widths],
        compiler_params=_cp("arbitrary"),
        name="mla_prep",
    )(*args)


def _attn_kernel(nparts, nseq, q_ref, *refs):
    kv = [(refs[2 * i], refs[2 * i + 1]) for i in range(nparts)]
    o_ref = refs[2 * nparts]
    tq = q_ref.shape[0] // nseq
    low = lax.broadcasted_iota(jnp.int32, (1, HEAD_PAD), 1) < MLA_V
    units = [(s, h) for s in range(nseq) for h in range(MLA_HEADS)]

    def keys(ref, s):
        lk = ref.shape[0] // nseq
        return slice(s * lk, (s + 1) * lk)

    def scores(s, h):
        sl = slice(HEAD_PAD * h, HEAD_PAD * (h + 1))
        return [_dot_nt(q_ref[s * tq:(s + 1) * tq, sl], k_ref[keys(k_ref, s), sl]) for k_ref, _ in kv]

    s_next = scores(*units[0])
    acc = None
    for n, (s, h) in enumerate(units):
        pair, j = divmod(h, 2)
        sc = s_next
        if n + 1 < len(units):
            s_next = scores(*units[n + 1])
        if j == 0:
            v_half = []
            for _, v_ref in kv:
                vp = v_ref[keys(v_ref, s), HEAD_PAD * pair:HEAD_PAD * (pair + 1)]
                zero = jnp.zeros_like(vp)
                v_half.append((jnp.where(low, vp, zero), jnp.where(low, zero, vp)))
        m = functools.reduce(jnp.maximum, [jnp.max(x, axis=-1, keepdims=True) for x in sc])
        p = [jnp.exp(x - m) for x in sc]
        den = functools.reduce(jnp.add, [jnp.sum(x, axis=-1, keepdims=True) for x in p])
        num = functools.reduce(jnp.add, [_dot(x.astype(BF16), vh[j]) for x, vh in zip(p, v_half)])
        o = num / den
        acc = o if j == 0 else acc + o
        if j == 1:
            o_ref[s * tq:(s + 1) * tq, HEAD_PAD * pair:HEAD_PAD * (pair + 1)] = acc.astype(BF16)


def _attention(q, kv_parts, batch, lq, tq):
    nq = lq // tq
    nseq = max(1, min(batch, 1024 // lq)) if nq == 1 else 1
    args, specs = [q], [pl.BlockSpec((nseq * tq, 512), lambda b, i: (b * nq + i, 0))]
    for k, v, lk in kv_parts:
        args += [k, v]
        specs += [pl.BlockSpec((nseq * lk, 512), lambda b, i: (b, 0)),
                  pl.BlockSpec((nseq * lk, 256), lambda b, i: (b, 0))]
    return pl.pallas_call(
        functools.partial(_attn_kernel, len(kv_parts), nseq),
        grid=(batch // nseq, nq),
        in_specs=specs,
        out_specs=pl.BlockSpec((nseq * tq, 256), lambda b, i: (b * nq + i, 0)),
        out_shape=jax.ShapeDtypeStruct((batch * lq, 256), BF16),
        compiler_params=_cp("arbitrary", "arbitrary"),
        name="attention",
    )(*args)


def _hyena_kernel(seq_len, bk, nseq, x_ref, cw_ref, cb_ref, fo_ref, go_ref, k_ref, bias_ref, o_ref,
                  u_sc, y_sc, z_sc):
    c = D_BRANCH
    n = nseq * seq_len
    nblk = seq_len // bk
    pos = jnp.bitwise_and(lax.broadcasted_iota(jnp.int32, (n, 1), 0), seq_len - 1)
    first, last = pos == 0, pos == seq_len - 1

    def short_conv(g):
        cols = slice(g * c, (g + 1) * c)
        x = x_ref[:, cols]
        xm = jnp.where(first, 0.0, pltpu.roll(x, 1, 0))
        xp = jnp.where(last, 0.0, pltpu.roll(x, n - 1, 0))
        return cw_ref[0:1, cols] * xm + cw_ref[1:2, cols] * x + cw_ref[2:3, cols] * xp + cb_ref[:, cols]

    fo, go = fo_ref[...], go_ref[...]

    def long_conv(s, v, order, emit):
        cols = slice(order * c, (order + 1) * c)
        for j in range(nblk):
            u_sc[s, j] = _dot(fo, v[j * bk:(j + 1) * bk].astype(BF16))
        yield
        rc = 32
        for i in range(nblk):
            def mix(r, carry, i=i):
                top = pl.ds(pl.multiple_of(r * rc, rc), rc)
                bot = pl.ds(pl.multiple_of(bk + r * rc, rc), rc)
                at = ab = None
                for j in range(nblk):
                    q = i - j + nblk - 1
                    kt, kb = k_ref[q, top, cols], k_ref[q, bot, cols]
                    ut, ub = u_sc[s, j, top, :], u_sc[s, j, bot, :]
                    pt, pb = ut * kt - ub * kb, ut * kb + ub * kt
                    at, ab = (pt, pb) if at is None else (at + pt, ab + pb)
                z_sc[s, top, :] = at.astype(BF16)
                z_sc[s, bot, :] = ab.astype(BF16)
                return carry

            lax.fori_loop(0, bk // rc, mix, 0, unroll=True if nblk == 1 else 2)
            y = _dot(go, z_sc[s])
            yield
            emit(i, y)

    v_all, x1_all, x2_all = short_conv(0), short_conv(1), short_conv(2)

    def sequence(s):
        base = s * seq_len
        v = v_all[base:base + seq_len]

        def emit1(i, y):
            r = slice(i * bk, (i + 1) * bk)
            y_sc[s, r, :] = x1_all[base + i * bk:base + (i + 1) * bk] * (y + bias_ref[0:1, :] * v[r])

        yield from long_conv(s, v, 0, emit1)
        y1 = y_sc[s]

        def emit2(i, y):
            r = slice(i * bk, (i + 1) * bk)
            o_ref[base + i * bk:base + (i + 1) * bk, :] = (
                x2_all[base + i * bk:base + (i + 1) * bk] * (y + bias_ref[1:2, :] * y1[r])).astype(BF16)

        yield from long_conv(s, y1, 1, emit2)

    live = [sequence(s) for s in range(nseq)]
    while live:
        for g in list(live):
            if next(g, StopIteration) is StopIteration:
                live.remove(g)


def _hyena(x, wl, kspec, fo, go, batch, seq_len):
    c = D_BRANCH
    bk = fo.shape[1]
    nseq = max(1, min(batch, 1024 // seq_len))
    rows = nseq * seq_len
    nblk = seq_len // bk
    full = lambda a: pl.BlockSpec(a.shape, lambda i: (0,) * a.ndim)
    return pl.pallas_call(
        functools.partial(_hyena_kernel, seq_len, bk, nseq),
        grid=(batch // nseq,),
        in_specs=[pl.BlockSpec((rows, 3 * c), lambda i: (i, 0)),
                  full(wl["hy_conv_w"]), full(wl["hy_conv_b"]), full(fo), full(go),
                  pl.BlockSpec(kspec.shape, lambda i: (0, 0, 0), pipeline_mode=pl.Buffered(1)),
                  full(wl["hy_bias"])],
        out_specs=pl.BlockSpec((rows, c), lambda i: (i, 0)),
        out_shape=jax.ShapeDtypeStruct((batch * seq_len, c), BF16),
        scratch_shapes=[pltpu.VMEM((nseq, nblk, 2 * bk, c), F32), pltpu.VMEM((nseq, seq_len, c), F32),
                        pltpu.VMEM((nseq, 2 * bk, c), BF16)],
        compiler_params=pltpu.CompilerParams(dimension_semantics=("arbitrary",), vmem_limit_bytes=56 * 1024 * 1024),
        name="hyena",
    )(x, wl["hy_conv_w"], wl["hy_conv_b"], fo, go, kspec, wl["hy_bias"])


def _hy_mlp_kernel(feat_ref, w1_ref, b1_ref, f1_ref, w2_ref, b2_ref, f2_ref, w3_ref, win_ref,
                   kern_ref, nrm_ref):
    i = pl.program_id(0)
    tl = win_ref.shape[0]
    h = jnp.sin(f1_ref[...] * (_dot3(feat_ref[...], w1_ref[...]) + b1_ref[...]))
    h = jnp.sin(f2_ref[...] * (_dot3(h, w2_ref[...]) + b2_ref[...]))
    filt = _dot3(h, w3_ref[...])
    filt = jnp.concatenate([filt[:, 0:512], filt[:, 512:1024]], axis=0)
    win = win_ref[...]
    row0 = (lax.broadcasted_iota(jnp.int32, (tl, 1), 0) + i * tl) == 0
    filt = jnp.where(row0, 0.0, filt * jnp.concatenate([win, win], axis=1))
    kern_ref[...] = filt
    part = jnp.sum(jnp.abs(filt), axis=0, keepdims=True)

    @pl.when(i == 0)
    def _():
        nrm_ref[...] = jnp.zeros_like(nrm_ref)
    nrm_ref[...] += jnp.broadcast_to(part, nrm_ref.shape)


def _hy_mlp(feat, wl, win):
    rows = win.shape[0]
    tl = 256
    nl = rows // (2 * tl)
    full = lambda shape: pl.BlockSpec(shape, lambda i: (0,) * len(shape))
    return pl.pallas_call(
        _hy_mlp_kernel,
        grid=(rows // tl,),
        in_specs=[pl.BlockSpec((tl // 2, 128), lambda i: (i, 0)),
                  full((128, 128)), full((1, 128)), full((1, 128)),
                  full((128, 128)), full((1, 128)), full((1, 128)),
                  pl.BlockSpec((128, 1024), lambda i: (0, jnp.where(i < nl, 1, 0))),
                  pl.BlockSpec((tl, 256), lambda i: (i, 0))],
        out_specs=[pl.BlockSpec((tl, 512), lambda i: (i, 0)), full((8, 512))],
        out_shape=[jax.ShapeDtypeStruct((rows, 512), F32), jax.ShapeDtypeStruct((8, 512), F32)],
        compiler_params=_cp("arbitrary"),
        name="hy_mlp",
    )(feat, wl["hy_w1"], wl["hy_b1"], wl["hy_f1"], wl["hy_w2"], wl["hy_b2"], wl["hy_f2"], wl["hy_w3"], win)


def _hy_kspec_kernel(lo_ref, hi_ref, fo_ref, fb_ref, n_ref, o_ref):
    bk = lo_ref.shape[0]
    k = _dot(fo_ref[...], hi_ref[...].astype(BF16)) + _dot(fb_ref[...], lo_ref[...].astype(BF16))
    o_ref[0] = k * ((1.0 / bk) / n_ref[0:1, :])


def _hy_kspec(kern_lin, nrm, fo, fb):
    n2, bk = fo.shape
    nq = kern_lin.shape[0] // bk - 1
    full = lambda a: pl.BlockSpec(a.shape, lambda q: (0,) * a.ndim)
    return pl.pallas_call(
        _hy_kspec_kernel,
        grid=(nq,),
        in_specs=[pl.BlockSpec((bk, 512), lambda q: (q, 0)),
                  pl.BlockSpec((bk, 512), lambda q: (q + 1, 0)),
                  full(fo), full(fb), full(nrm)],
        out_specs=pl.BlockSpec((1, n2, 512), lambda q: (q, 0, 0)),
        out_shape=jax.ShapeDtypeStruct((nq, n2, 512), F32),
        compiler_params=_cp("arbitrary"),
        name="hy_kspec",
    )(kern_lin, kern_lin, fo, fb, nrm)


def _s5_discretise(are_ref, aim_ref, ldt_ref):
    ar = jnp.minimum(are_ref[0], -1e-4)
    ai = aim_ref[0]
    dt = jnp.exp(ldt_ref[0])
    e = jnp.exp(ar * dt)
    return ar, ai, e * jnp.cos(ai * dt), e * jnp.sin(ai * dt)


def _s5_scan_kernel(nseq, emit_y, u_ref, hin_ref, are_ref, aim_ref, ldt_ref, bre_ref, bim_ref, *rest):
    if emit_y:
        cre_ref, cim_ref, y_ref, hfin_ref, wb_sc, ab_sc, s_sc, hc_sc, perm_sc, wc_sc = rest
    else:
        hfin_ref, wb_sc, ab_sc, s_sc, hc_sc, perm_sc = rest
    d = pl.program_id(0)
    c = pl.program_id(1)
    n = S5_N

    @pl.when(c == 0)
    def _():
        ar, ai, abr, abi = _s5_discretise(are_ref, aim_ref, ldt_ref)
        ab_sc[0:1, :] = abr
        ab_sc[1:2, :] = abi
        den = 1.0 / (ar * ar + ai * ai)
        cr = ((abr - 1.0) * ar + abi * ai) * den
        ci = (abi * ar - (abr - 1.0) * ai) * den
        grp_rows = lax.shift_right_logical(lax.broadcasted_iota(jnp.int32, (D_BRANCH, 1), 0), 4)
        grp_cols = lax.shift_right_logical(lax.broadcasted_iota(jnp.int32, (1, n), 1), 6)
        expand_b = lambda ref: jnp.where(grp_rows == grp_cols, jnp.concatenate([ref[0]] * (n // 128), axis=1), 0.0)
        bre, bim = expand_b(bre_ref), expand_b(bim_ref)
        wb_sc[:, 0:n] = (cr * bre - ci * bim).astype(BF16)
        wb_sc[:, n:2 * n] = (cr * bim + ci * bre).astype(BF16)
        if emit_y:
            st_rows = lax.shift_right_logical(lax.broadcasted_iota(jnp.int32, (n, 1), 0), 6)
            ch_cols = lax.shift_right_logical(lax.broadcasted_iota(jnp.int32, (1, D_BRANCH), 1), 4)
            expand_c = lambda ref: jnp.where(st_rows == ch_cols, jnp.concatenate([ref[0]] * 2, axis=1), 0.0)
            wc_sc[0:n, :] = expand_c(cre_ref).astype(BF16)
            wc_sc[n:2 * n, :] = (-expand_c(cim_ref)).astype(BF16)
        hc_sc[...] = hin_ref[0]

    steps = u_ref.shape[1]
    rows_c = nseq * steps

    @pl.when(c == 0)
    def _():
        i = lax.broadcasted_iota(jnp.int32, (rows_c, rows_c), 0)
        j = lax.broadcasted_iota(jnp.int32, (rows_c, rows_c), 1)
        p = lax.shift_right_logical(i, int(math.log2(nseq)))
        step = p + d * (steps - 1 - 2 * p)
        src = jnp.bitwise_and(i, nseq - 1) * steps + step
        perm_sc[...] = jnp.where(j == src, 1.0, 0.0).astype(BF16)

    lhs = _dot(perm_sc[...], u_ref[...].reshape(rows_c, D_BRANCH).astype(BF16)).astype(BF16)
    lb = 256
    y = None
    for j in range(n // lb):
        lr = slice(lb * j, lb * (j + 1))
        li = slice(n + lb * j, n + lb * (j + 1))
        bur = _dot(lhs, wb_sc[:, lr])
        bui = _dot(lhs, wb_sc[:, li])
        abr = ab_sc[0:1, lr]
        abi = ab_sc[1:2, lr]
        hr, hi = hc_sc[:, lr], hc_sc[:, li]
        for p in range(steps):
            rows = slice(p * nseq, (p + 1) * nseq)
            hr, hi = abr * hr - abi * hi + bur[rows], abr * hi + abi * hr + bui[rows]
            if emit_y:
                s_sc[rows, lr] = hr.astype(BF16)
                s_sc[rows, li] = hi.astype(BF16)
        hc_sc[:, lr] = hr
        hc_sc[:, li] = hi
        if emit_y:
            yj = _dot(s_sc[:, lr], wc_sc[lr, :]) + _dot(s_sc[:, li], wc_sc[li, :])
            y = yj if y is None else y + yj

    if emit_y:
        @pl.when(d == 0)
        def _():
            for p in range(steps):
                y_ref[0, :, p, :] = y[p * nseq:(p + 1) * nseq]

        @pl.when(d == 1)
        def _():
            for p in range(steps):
                y_ref[0, :, steps - 1 - p, :] = y[p * nseq:(p + 1) * nseq]

    @pl.when(c == pl.num_programs(1) - 1)
    def _():
        hfin_ref[0] = hc_sc[...]


def _s5_scan(u, hin, wl, emit_y):
    nseq, nstep, _ = u.shape
    steps = S5_ROWS // nseq
    nc = nstep // steps
    n = S5_N
    chunk = lambda d, c: c + d * (nc - 1 - 2 * c)
    per_dir = lambda shape: pl.BlockSpec((1,) + shape, lambda d, c: (d,) + (0,) * len(shape))
    args = [u, hin, wl["s5_are"], wl["s5_aim"], wl["s5_ldt"], wl["s5_bre"], wl["s5_bim"]]
    specs = [pl.BlockSpec((nseq, steps, D_BRANCH), lambda d, c: (0, chunk(d, c), 0)),
             per_dir((nseq, 2 * n)), per_dir((1, n)), per_dir((1, n)), per_dir((1, n)),
             per_dir((D_BRANCH, 128)), per_dir((D_BRANCH, 128))]
    out_specs = [per_dir((nseq, 2 * n))]
    out_shape = [jax.ShapeDtypeStruct((2, nseq, 2 * n), F32)]
    scratch = [pltpu.VMEM((D_BRANCH, 2 * n), BF16), pltpu.VMEM((8, n), F32),
               pltpu.VMEM((S5_ROWS, 2 * n), BF16), pltpu.VMEM((nseq, 2 * n), F32),
               pltpu.VMEM((S5_ROWS, S5_ROWS), BF16)]
    if emit_y:
        args += [wl["s5_cre"], wl["s5_cim"]]
        specs += [per_dir((n, 128)), per_dir((n, 128))]
        out_specs = [pl.BlockSpec((1, nseq, steps, D_BRANCH), lambda d, c: (d, 0, chunk(d, c), 0))] + out_specs
        out_shape = [jax.ShapeDtypeStruct((2, nseq, nstep, D_BRANCH), F32)] + out_shape
        scratch += [pltpu.VMEM((2 * n, D_BRANCH), BF16)]
    return pl.pallas_call(
        functools.partial(_s5_scan_kernel, nseq, emit_y),
        grid=(2, nc),
        in_specs=specs, out_specs=out_specs, out_shape=out_shape, scratch_shapes=scratch,
        compiler_params=_cp("arbitrary", "arbitrary"),
        name="s5_scan" if emit_y else "s5_scan_finals",
    )(*args)


def _s5_chain_kernel(batch, nseg, f_ref, h0_ref, are_ref, aim_ref, ldt_ref, o_ref):
    d = pl.program_id(0)
    n = S5_N
    _, _, pr, pi = _s5_discretise(are_ref, aim_ref, ldt_ref)
    for _ in range(int(math.log2(S5_SEG))):
        pr, pi = pr * pr - pi * pi, 2.0 * pr * pi
    f = f_ref[0]
    fr, fi = f[:, 0:n], f[:, n:2 * n]
    h0 = h0_ref[0]
    h0r, h0i = h0[:, 0:n], h0[:, n:2 * n]
    nrow = batch * nseg
    seg = jnp.bitwise_and(lax.broadcasted_iota(jnp.int32, (nrow, 1), 0), nseg - 1)

    def run(shift, keep):
        xr, xi = h0r, h0i
        for _ in range(nseg - 1):
            zr = fr + pr * xr - pi * xi
            zi = fi + pr * xi + pi * xr
            xr = h0r + jnp.where(keep, pltpu.roll(zr, shift, 0), 0.0)
            xi = h0i + jnp.where(keep, pltpu.roll(zi, shift, 0), 0.0)
        o_ref[0, :, 0:n] = xr
        o_ref[0, :, n:2 * n] = xi

    @pl.when(d == 0)
    def _():
        run(1, seg != 0)

    @pl.when(d == 1)
    def _():
        run(nrow - 1, seg != nseg - 1)


def _s5_chain(fin, h0rows, wl, batch, nseg):
    nrow = batch * nseg
    n = S5_N
    per_dir = lambda shape: pl.BlockSpec((1,) + shape, lambda d: (d,) + (0,) * len(shape))
    return pl.pallas_call(
        functools.partial(_s5_chain_kernel, batch, nseg),
        grid=(2,),
        in_specs=[per_dir((nrow, 2 * n)), per_dir((nrow, 2 * n)), per_dir((1, n)), per_dir((1, n)), per_dir((1, n))],
        out_specs=per_dir((nrow, 2 * n)),
        out_shape=jax.ShapeDtypeStruct((2, nrow, 2 * n), F32),
        compiler_params=_cp("arbitrary"),
        name="s5_chain",
    )(fin, h0rows, wl["s5_are"], wl["s5_aim"], wl["s5_ldt"])


def _gla_kernel(seq_len, nb, n_aliased, q_ref, k_ref, v_ref, g_ref, gw_ref, gb_ref, s0_ref, *rest):
    o_ref, sfin_ref, qe_sc, upd_sc, dec_sc, sall_sc, lhs_sc, kt_sc, la_sc, oi_sc = rest[n_aliased:]
    d = pl.program_id(1)
    sign = 1 - 2 * d
    ck, sup = GLA_CHUNK, GLA_SUPER
    cps = sup // ck
    nsup, nchunk = seq_len // sup, seq_len // ck
    dk, dv = GLA_HEADS * GLA_DK, GLA_HEADS * GLA_DV
    r = lax.broadcasted_iota(jnp.int32, (sup, sup), 0)
    s = lax.broadcasted_iota(jnp.int32, (sup, sup), 1)
    same = lax.shift_right_logical(r, 6) == lax.shift_right_logical(s, 6)
    tri = jnp.logical_and(same, (s - r) * sign <= 0)
    cum_lhs = jnp.where(tri, 1.0, 0.0).astype(BF16)
    t4 = lax.broadcasted_iota(jnp.int32, (ck, GLA_HEADS * ck), 0)
    s4 = jnp.bitwise_and(lax.broadcasted_iota(jnp.int32, (ck, GLA_HEADS * ck), 1), ck - 1)
    tri4 = (s4 - t4) * sign <= 0
    pos = jnp.bitwise_and(lax.broadcasted_iota(jnp.int32, (ck, 1), 0), ck - 1)
    is_last = pos == (ck - 1) * (1 - d)
    row_chunk = lax.shift_right_logical(lax.broadcasted_iota(jnp.int32, (sup, 1), 0), 6)
    head_k = lax.shift_right_logical(lax.broadcasted_iota(jnp.int32, (1, dk), 1), 5)
    head_v = lax.shift_right_logical(lax.broadcasted_iota(jnp.int32, (1, dv), 1), 6)
    blockdiag = lax.shift_right_logical(lax.broadcasted_iota(jnp.int32, (dv, 1), 0), 6) == head_k

    def group_rows(u):
        return pl.ds(u * sup, sup) if isinstance(u, int) else pl.ds(pl.multiple_of(u * sup, sup), sup)

    def stage_a(u, slot):
        rows = group_rows(u)
        q = q_ref[rows, :] * (GLA_DK ** -0.5)
        k = k_ref[rows, :]
        v = v_ref[rows, :]
        cs = _dot(cum_lhs, la_sc[rows, :])
        yield
        bc = cs[:, 0:dk] + cs[:, dk:2 * dk]
        tots = [jnp.sum(jnp.where(is_last, bc[c * ck:(c + 1) * ck], 0.0), axis=0, keepdims=True)
                for c in range(cps)]
        tot = jnp.concatenate([jnp.broadcast_to(t, (ck, dk)) for t in tots], axis=0)
        ref = 0.5 * tot
        kt_sc[slot] = (k * jnp.exp(ref - bc)).astype(BF16)
        lhs_sc[slot] = (q * jnp.exp(bc - ref)).astype(BF16)
        qe_sc[rows, :] = (q * jnp.exp(bc)).astype(BF16)
        kl = (k * jnp.exp(tot - bc)).astype(BF16)
        zero = jnp.zeros_like(kl)
        klx = jnp.concatenate([jnp.where(row_chunk == c, kl, zero) for c in range(cps)], axis=1)
        upd = _dot_tn(v.astype(BF16), klx)
        yield
        for c in range(cps):
            upd_sc[u * cps + c] = jnp.where(blockdiag, upd[:, c * dk:(c + 1) * dk], 0.0)
            dec_sc[u * cps + c] = jnp.broadcast_to(jnp.exp(tots[c]), (8, dk))

    def stage_b(u, slot):
        rows = group_rows(u)
        v = v_ref[rows, :].astype(BF16)
        qt, kt = lhs_sc[slot], kt_sc[slot]
        zk, zv = jnp.zeros_like(kt[0:ck]), jnp.zeros_like(v[0:ck])
        p = [_dot_nt(qt[c * ck:(c + 1) * ck],
                     jnp.concatenate([jnp.where(head_k == h, kt[c * ck:(c + 1) * ck], zk)
                                      for h in range(GLA_HEADS)], axis=0)) for c in range(cps)]
        yield
        o = [_dot(jnp.where(tri4, p[c], 0.0).astype(BF16),
                  jnp.concatenate([jnp.where(head_v == h, v[c * ck:(c + 1) * ck], zv)
                                   for h in range(GLA_HEADS)], axis=0)) for c in range(cps)]
        yield
        oi_sc[rows, :] = jnp.concatenate(o, axis=0)

    def run(*stages):
        live = list(stages)
        while live:
            for g in list(live):
                if next(g, StopIteration) is StopIteration:
                    live.remove(g)

    x = _dot(g_ref[...].astype(BF16), gw_ref[0]) + gb_ref[0]
    la = (jnp.minimum(x, 0.0) - jnp.log(1.0 + jnp.exp(-jnp.abs(x)))) * (1.0 / GLA_TAU)
    la_sc[...] = jnp.concatenate(_split2(la), axis=1)

    nu = nb * nsup
    if nu <= 4:
        run(*[stage_a(u, u) for u in range(min(2, nu))])
        for u in range(0, nu, 2):
            run(*([stage_b(v, v % 4) for v in range(u, min(u + 2, nu))]
                  + [stage_a(v, v % 4) for v in range(u + 2, min(u + 4, nu))]))
    else:
        run(stage_a(0, 0), stage_a(1, 1))

        def sup_body(t, carry):
            u = 2 * t
            n0, n1 = jnp.minimum(u + 2, nu - 2), jnp.minimum(u + 3, nu - 1)
            run(stage_b(u, jnp.bitwise_and(u, 3)), stage_b(u + 1, jnp.bitwise_and(u + 1, 3)),
                stage_a(n0, jnp.bitwise_and(u + 2, 3)), stage_a(n1, jnp.bitwise_and(u + 3, 3)))
            return carry

        lax.fori_loop(0, nu // 2, sup_body, 0)

    for j in range(nb):
        def state_body(c, st, j=j):
            ci = j * nchunk + c + d * (nchunk - 1 - 2 * c)
            sall_sc[ci] = st.astype(BF16)
            return dec_sc[ci][0:1, :] * st + upd_sc[ci]

        st_fin = jnp.transpose(lax.fori_loop(0, nchunk, state_body, s0_ref[j, 0]))
        for h in range(GLA_HEADS):
            sfin_ref[j, 0, 0, h] = st_fin[h * GLA_DK:(h + 1) * GLA_DK, h * GLA_DV:(h + 1) * GLA_DV]

    def inter(u):
        rows = group_rows(u)
        qe = qe_sc[rows, :]
        oi = jnp.concatenate([_dot_nt(qe[c * ck:(c + 1) * ck], sall_sc[u * cps + c]) for c in range(cps)], axis=0)
        o_ref[0, rows, :] = (oi_sc[rows, :] + oi).astype(BF16)

    if nu <= 4:
        for u in range(nu):
            inter(u)
    else:
        lax.fori_loop(0, nu, lambda u, carry: (inter(u), carry)[1], 0, unroll=2 if nu % 2 == 0 else 1)


def _gla(gla_in, mla_in, wl, s0t, batch, seq_len, fin=(0, None, 1)):
    layer, prev_fin, fin_layers = fin
    aliases = {} if prev_fin is None else {7: 1}
    extra = [] if prev_fin is None else [prev_fin]
    n = gla_in.shape[0]
    dk, dv = GLA_HEADS * GLA_DK, GLA_HEADS * GLA_DV
    nb = max(1, min(batch, 1024 // seq_len))
    rows = nb * seq_len
    nchunk = nb * (seq_len // GLA_CHUNK)
    return pl.pallas_call(
        functools.partial(_gla_kernel, seq_len, nb, len(extra)),
        grid=(batch // nb, 2),
        input_output_aliases=aliases,
        in_specs=[pl.BlockSpec((rows, dk), lambda b, d: (b, 0)),
                  pl.BlockSpec((rows, dk), lambda b, d: (b, 1)),
                  pl.BlockSpec((rows, dv), lambda b, d: (b, 1)),
                  pl.BlockSpec((rows, dk), lambda b, d: (b, 1)),
                  pl.BlockSpec((1, dk, dk), lambda b, d: (d, 0, 0)),
                  pl.BlockSpec((1, 1, dk), lambda b, d: (d, 0, 0)),
                  pl.BlockSpec((nb, 1, dv, dk), lambda b, d: (b, d, 0, 0))]
                 + [pl.BlockSpec(memory_space=pl.ANY)] * len(extra),
        out_specs=[pl.BlockSpec((1, rows, dv), lambda b, d: (d, b, 0)),
                   pl.BlockSpec((nb, 1, 1, GLA_HEADS, GLA_DK, GLA_DV), lambda b, d: (b, layer, d, 0, 0, 0))],
        out_shape=[jax.ShapeDtypeStruct((2, n, dv), BF16),
                   jax.ShapeDtypeStruct((batch, fin_layers, 2, GLA_HEADS, GLA_DK, GLA_DV), F32)],
        scratch_shapes=[pltpu.VMEM((rows, dk), BF16),
                        pltpu.VMEM((nchunk, dv, dk), F32),
                        pltpu.VMEM((nchunk, 8, dk), F32),
                        pltpu.VMEM((nchunk, dv, dk), BF16),
                        pltpu.VMEM((4, GLA_SUPER, dk), BF16),
                        pltpu.VMEM((4, GLA_SUPER, dk), BF16),
                        pltpu.VMEM((rows, 2 * dk), BF16),
                        pltpu.VMEM((rows, dv), F32)],
        compiler_params=_cp("arbitrary", "arbitrary"),
        name="gla",
    )(gla_in, gla_in, gla_in, mla_in, wl["gla_gw"], wl["gla_gb"], s0t, *extra)


def _outproj_kernel(x_ref, mod_ref, g_ref, om_ref, oh_ref, su_ref, sf_ref, sb_ref, sd_ref, sw_ref, sbias_ref,
                    gf_ref, gb_ref, gn_ref, hm_ref, w_src, y_ref, w_ref):
    c = D_BRANCH

    @pl.when(pl.program_id(0) == 0)
    def _():
        w_ref[...] = w_src[0].astype(BF16)

    g = g_ref[...].astype(F32)
    acc = _dot((om_ref[...].astype(F32) * _silu(g[:, 0:c])).astype(BF16), w_ref[0:c, :])
    acc += _dot((oh_ref[...].astype(F32) * _silu(g[:, c:2 * c])).astype(BF16), w_ref[c:2 * c, :])
    ys = sd_ref[...] * su_ref[...] + sf_ref[0] + sb_ref[0]
    ge = 0.5 * ys * (1.0 + jnp.tanh(math.sqrt(2.0 / math.pi) * (ys + 0.044715 * (ys * ys * ys))))
    o_s5 = ge / (1.0 + jnp.exp(-(_dot(ge.astype(BF16), sw_ref[...]) + sbias_ref[...])))
    acc += _dot((o_s5 * _silu(g[:, 2 * c:3 * c])).astype(BF16), w_ref[2 * c:3 * c, :])
    og = gf_ref[0].astype(F32) + gb_ref[0].astype(F32)
    hi, lo = _split2(og * og)
    ms = _dot(hi, hm_ref[...]) + _dot(lo, hm_ref[...])
    ogn = og * lax.rsqrt(ms + EPS) * gn_ref[...]
    acc += _dot((ogn * _silu(g[:, 3 * c:4 * c])).astype(BF16), w_ref[3 * c:4 * c, :])
    y_ref[...] = x_ref[...] + mod_ref[0, 2:3, :] * acc


def _outproj(x, mod, mod_row, gates, o_mla, o_hy, s5_u, s5_y, o_gla, wl, tm):
    n, d = x.shape
    c = D_BRANCH
    row = lambda w: pl.BlockSpec((tm, w), lambda i: (i, 0))
    per_dir = lambda k: pl.BlockSpec((1, tm, c), lambda i: (k, i, 0))
    full = lambda *shape: pl.BlockSpec(shape, lambda i: (0,) * len(shape))
    return pl.pallas_call(
        _outproj_kernel,
        grid=(n // tm,),
        in_specs=[row(d),
                  pl.BlockSpec((1, 3, d), lambda i: (mod_row(i * tm), 0, 0)),
                  row(d), row(c), row(c),
                  row(c), per_dir(0), per_dir(1), full(1, c), full(c, c), full(1, c),
                  per_dir(0), per_dir(1), full(1, c), full(c, c),
                  pl.BlockSpec((1, d, d), lambda i: (wl["layer"], 0, 0), pipeline_mode=pl.Buffered(1))],
        out_specs=row(d),
        out_shape=jax.ShapeDtypeStruct((n, d), F32),
        scratch_shapes=[pltpu.VMEM((d, d), BF16)],
        compiler_params=_cp("arbitrary"),
        name="outproj",
    )(x, mod, gates, o_mla, o_hy, s5_u, s5_y, s5_y, wl["s5_d"], wl["s5_glu_w"], wl["s5_glu_b"],
      o_gla, o_gla, wl["gla_norm"], wl["head_mean"], wl["w_out"])


def _rope_tables(seq_len):
    pos = np.arange(seq_len)
    inv = ROPE_BASE ** (-np.arange(0, 16, 2, dtype=np.float64) / 16.0)
    cos = np.ones((seq_len, HEAD_PAD))
    sin_a = np.zeros((seq_len, HEAD_PAD))
    sin_b = np.zeros((seq_len, HEAD_PAD))
    for base, p in ((MLA_NOPE, pos // GRID_W), (MLA_NOPE + 16, pos % GRID_W)):
        ang = p[:, None].astype(np.float64) * inv[None, :]
        cos[:, base:base + 8] = np.cos(ang)
        cos[:, base + 8:base + 16] = np.cos(ang)
        sin_a[:, base:base + 8] = -np.sin(ang)
        sin_b[:, base + 8:base + 16] = np.sin(ang)
    return tuple(jnp.asarray(t, F32) for t in (cos, sin_a, sin_b))


def _odd_dft(seq_len):
    bk = min(seq_len, HY_BLOCK)
    k = np.arange(bk)[:, None]
    t = np.arange(bk)[None, :]

    def mat(shift):
        ang = (np.pi / (2 * bk)) * (((2 * k + 1) * (t + shift)) % (4 * bk))
        return np.concatenate([np.cos(ang), -np.sin(ang)], axis=0)

    fo = mat(0)
    fb = -mat(bk)
    fb[:, 0] = 0.0
    const = lambda a: jnp.asarray(a, F32).astype(BF16)
    return const(fo), const(fb), const(fo.T)


def _hyena_tables(seq_len):
    lag = np.arange(-seq_len, seq_len)
    pos = np.where(lag == -seq_len, 0, np.abs(lag)).astype(np.float64)
    t = pos / seq_len
    w = 2.0 * np.pi * pos / seq_len
    bands = np.linspace(1e-4, HY_BANDS - 1, HY_BANDS)
    feat = np.zeros((2 * seq_len, HY_HIDDEN))
    feat[:, 0] = t
    feat[:, 1:1 + HY_BANDS] = np.cos(w[:, None] * bands)
    feat[:, 1 + HY_BANDS:HY_FEAT] = np.sin(w[:, None] * bands)
    feat = feat.reshape(-1, 2, 128, HY_HIDDEN).transpose(0, 2, 1, 3).reshape(seq_len, 2 * HY_HIDDEN)
    deltas = np.linspace(math.log(1.0 / HY_TARGET) / HY_FAST_DECAY, math.log(1.0 / HY_TARGET) / HY_SLOW_DECAY,
                         D_BRANCH)
    win = np.exp(-t[:, None] * deltas[None, :]) + HY_SHIFT
    return jnp.asarray(feat, F32), jnp.asarray(win, F32)


def _pad_to(a, shape):
    return jnp.pad(a, [(0, s - d) for s, d in zip(shape, a.shape)])


def _layer_weights(l, p):
    z = lambda *s: jnp.zeros(s, F32)
    d = D_MODEL
    wl = {"layer": l, "w_in": p["w_in"], "norm_w": p["norm_w"][l].reshape(1, d), "w_out": p["w_out"]}
    wl["qa_norm"] = _pad_to(p["mla_qa_norm"][l].reshape(1, -1), (1, 256))
    w_uq = _pad_to(p["mla_w_uq"][l].reshape(MLA_Q_RANK, MLA_HEADS, MLA_QK), (256, MLA_HEADS, HEAD_PAD))
    wl["w_uq"] = w_uq.reshape(256, MLA_HEADS * HEAD_PAD).astype(BF16)
    wl["q_norm"] = _pad_to(p["mla_q_norm"][l].reshape(1, -1), (1, HEAD_PAD))
    wl["k_norm"] = _pad_to(p["mla_k_norm"][l].reshape(1, -1), (1, HEAD_PAD))
    wl["kva_norm"] = p["mla_kva_norm"][l].reshape(1, -1)
    w_ukv = p["mla_w_ukv"][l].reshape(MLA_KV_RANK, MLA_HEADS, MLA_NOPE + MLA_V)
    wl["w_uk"] = _pad_to(w_ukv[:, :, :MLA_NOPE], (MLA_KV_RANK, MLA_HEADS, HEAD_PAD)).reshape(MLA_KV_RANK, -1).astype(BF16)
    wl["w_uv"] = w_ukv[:, :, MLA_NOPE:].reshape(MLA_KV_RANK, MLA_HEADS * MLA_V).astype(BF16)
    wl["hy_conv_w"] = p["hy_conv_w"][l]
    wl["hy_conv_b"] = p["hy_conv_b"][l].reshape(1, -1)
    hh = HY_HIDDEN
    twice = lambda a: jnp.tile(a.reshape(1, -1), (1, 2))
    bdiag = lambda a: jnp.concatenate([jnp.pad(a, ((0, 0), (0, a.shape[1]))), jnp.pad(a, ((0, 0), (a.shape[1], 0)))], axis=0)
    wl["hy_w1"] = bdiag(_pad_to(p["hy_w1"][l], (hh, hh)))
    wl["hy_b1"], wl["hy_f1"] = twice(p["hy_b1"][l]), twice(p["hy_freq1"][l])
    wl["hy_w2"] = bdiag(p["hy_w2"][l])
    wl["hy_b2"], wl["hy_f2"] = twice(p["hy_b2"][l]), twice(p["hy_freq2"][l])
    w3 = p["hy_w3"][l]
    wl["hy_w3"] = jnp.concatenate([bdiag(w3[:, 0:512]), bdiag(w3[:, 512:1024])], axis=1)
    wl["hy_bias"] = p["hy_bias"][l]
    flat = lambda a: a[l].reshape(2, 1, S5_N)
    wl["s5_are"], wl["s5_aim"] = flat(p["s5_a_re"]), flat(p["s5_a_im"])
    wl["s5_ldt"] = jnp.repeat(p["s5_log_dt"][l], S5_STATE, axis=-1).reshape(2, 1, S5_N)
    lanes_b = lambda a: jnp.tile(a[l].transpose(0, 1, 3, 2).reshape(2, D_BRANCH, S5_STATE), (1, 1, 128 // S5_STATE))
    lanes_c = lambda a: jnp.tile(a[l].transpose(0, 1, 3, 2).reshape(2, S5_N, S5_GROUP), (1, 1, 128 // S5_GROUP))
    wl["s5_bre"], wl["s5_bim"] = lanes_b(p["s5_b_re"]), lanes_b(p["s5_b_im"])
    wl["s5_cre"], wl["s5_cim"] = lanes_c(p["s5_c_re"]), lanes_c(p["s5_c_im"])
    wl["s5_d"] = p["s5_d"][l].reshape(1, -1)
    wl["s5_glu_w"] = p["s5_glu_w"][l].astype(BF16)
    wl["s5_glu_b"] = p["s5_glu_b"][l].reshape(1, -1)
    gw = p["gla_gw"][l]
    dk = GLA_HEADS * GLA_DK
    wl["gla_gw"] = jnp.stack([_pad_to(jnp.pad(gw[i], ((GLA_G_LANE + GLA_RANK * i, 0), (0, 0))), (dk, dk))
                              for i in range(2)]).astype(BF16)
    wl["gla_gb"] = p["gla_gb"][l].reshape(2, 1, dk)
    wl["gla_norm"] = jnp.tile(p["gla_norm"][l], GLA_HEADS).reshape(1, -1)
    head = np.arange(D_BRANCH) // GLA_DV
    wl["head_mean"] = jnp.asarray((head[:, None] == head[None, :]) / GLA_DV, BF16)
    return wl


def _hyena_filters(wl, tabs):
    feat, win, fo, fb, _ = tabs
    kern_lin, nrm = _hy_mlp(feat, wl, win)
    return _hy_kspec(kern_lin, nrm, fo, fb)


def _trunk_layer(x, mod, mod_row, wl, batch, seq_len, hy_tabs, rope_tabs=None, ctx=None, layer=0, cache_bufs=None):
    n = batch * seq_len
    tm = 512
    if ctx is None:
        mla_in, gates, hy_in, s5_in, gla_in, q, k, v, ckv, krope = _inproj(
            x, mod, mod_row, wl, tm, seq_len, (layer, cache_bufs[:2]))
        kv_parts = [(k, v, seq_len)]
    else:
        mla_in, gates, hy_in, s5_in, gla_in = _inproj(x, mod, mod_row, wl, tm, seq_len)
        q, k, v = _mla_prep(mla_in, wl, rope_tabs, seq_len, tm, True)
        ckv = krope = None
        k_ctx, v_ctx = _mla_prep(ctx["mla"], wl, None, ctx["past"], 512, False)
        kv_parts = [(k_ctx, v_ctx, ctx["past"]), (k, v, seq_len)]
    o_mla = _attention(q, kv_parts, batch, seq_len, min(seq_len, 512))

    o_hy = _hyena(hy_in, wl, _hyena_filters(wl, hy_tabs), hy_tabs[2], hy_tabs[4], batch, seq_len)

    nseg = seq_len // S5_SEG
    nseq = batch * nseg
    u_seg = s5_in.reshape(nseq, S5_SEG, D_BRANCH)
    if ctx is None:
        hin = jnp.zeros((2, nseq, 2 * S5_N), F32)
    else:
        (fin,) = _s5_scan(u_seg, jnp.zeros((2, nseq, 2 * S5_N), F32), wl, False)
        hin = _s5_chain(fin, ctx["s5_h0"], wl, batch, nseg)
    y2, s5_fin = _s5_scan(u_seg, hin, wl, True)

    s0 = jnp.zeros((batch, 2, GLA_HEADS * GLA_DV, GLA_HEADS * GLA_DK), F32) if ctx is None else ctx["gla_s0"]
    fin = (0, None, 1) if ctx is not None else (layer, None if cache_bufs is None else cache_bufs[2], DEPTH)
    o_gla, gla_fin = _gla(gla_in, mla_in, wl, s0, batch, seq_len, fin)

    y = _outproj(x, mod, mod_row, gates, o_mla, o_hy, s5_in, y2.reshape(2, n, D_BRANCH), o_gla, wl, 2 * tm)
    return y, (ckv, krope, s5_fin, gla_fin)


def kernel(x_prompt, x_sample, c, cache_mla_ckv, cache_mla_krope, state_s5, state_gla, c_ctx, norm_w, ada_w, ada_b, w_in, w_out, mla_qa_norm, mla_kva_norm, mla_w_uq, mla_w_ukv, mla_q_norm, mla_k_norm, hy_conv_w, hy_conv_b, hy_w1, hy_b1, hy_freq1, hy_w2, hy_b2, hy_freq2, hy_w3, hy_bias, s5_a_re, s5_a_im, s5_log_dt, s5_b_re, s5_b_im, s5_c_re, s5_c_im, s5_d, s5_glu_w, s5_glu_b, gla_gw, gla_gb, gla_norm):
    params = dict(norm_w=norm_w, w_in=w_in, w_out=w_out, mla_qa_norm=mla_qa_norm, mla_kva_norm=mla_kva_norm,
                  mla_w_uq=mla_w_uq, mla_w_ukv=mla_w_ukv, mla_q_norm=mla_q_norm, mla_k_norm=mla_k_norm,
                  hy_conv_w=hy_conv_w, hy_conv_b=hy_conv_b, hy_w1=hy_w1, hy_b1=hy_b1, hy_freq1=hy_freq1,
                  hy_w2=hy_w2, hy_b2=hy_b2, hy_freq2=hy_freq2, hy_w3=hy_w3, hy_bias=hy_bias,
                  s5_a_re=s5_a_re, s5_a_im=s5_a_im, s5_log_dt=s5_log_dt, s5_b_re=s5_b_re, s5_b_im=s5_b_im,
                  s5_c_re=s5_c_re, s5_c_im=s5_c_im, s5_d=s5_d, s5_glu_w=s5_glu_w, s5_glu_b=s5_glu_b,
                  gla_gw=gla_gw, gla_gb=gla_gb, gla_norm=gla_norm)
    bp, lp, d = x_prompt.shape
    bs, ls, _ = x_sample.shape
    past = cache_mla_ckv.shape[2]
    n_s5 = S5_N

    conds = jnp.concatenate([c_ctx[None, :], c, jnp.zeros((8 - 1 - bs, d), F32)], axis=0)
    mods = _modulation(conds, ada_w, ada_b).reshape(DEPTH, 8, 3, d)

    tabs_p = _hyena_tables(lp) + _odd_dft(lp)
    tabs_s = _hyena_tables(ls) + _odd_dft(ls)
    rope_tabs = _rope_tables(ls)
    nseg = ls // S5_SEG

    y_p = x_prompt.reshape(bp * lp, d)
    y_s = x_sample.reshape(bs * ls, d)
    s5_l = []
    cache_bufs = (jnp.zeros((bp, DEPTH, lp, MLA_KV_RANK), F32), jnp.zeros((bp, DEPTH, lp, MLA_ROPE), F32),
                  jnp.zeros((bp, DEPTH, 2, GLA_HEADS, GLA_DK, GLA_DV), F32))
    for l in range(DEPTH):
        wl = _layer_weights(l, params)
        y_p, (ckv, krope, s5_fin, gla_fin) = _trunk_layer(y_p, mods[l], lambda row: 0, wl, bp, lp, tabs_p,
                                                          layer=l, cache_bufs=cache_bufs)
        cache_bufs = (ckv, krope, gla_fin)
        s5_l.append(jnp.stack([s5_fin[:, :, :n_s5], s5_fin[:, :, n_s5:]], axis=-1)
                    .reshape(2, bp, S5_GROUPS, S5_STATE, 2).transpose(1, 0, 2, 3, 4))

        mla_ctx = jnp.concatenate([cache_mla_ckv[:, l], jnp.zeros((bs, past, 64), F32), cache_mla_krope[:, l],
                                   jnp.zeros((bs, past, 32), F32)], axis=-1).reshape(bs * past, 256)
        st = state_s5[:, l]
        h0 = jnp.concatenate([st[..., 0].reshape(bs, 2, n_s5), st[..., 1].reshape(bs, 2, n_s5)], axis=-1)
        h0 = h0.transpose(1, 0, 2)
        h0rows = jnp.zeros((2, nseg * bs, 2 * n_s5), F32)
        h0rows = h0rows.at[0, 0::nseg].set(h0[0]).at[1, nseg - 1::nseg].set(h0[1])
        eye_h = jnp.eye(GLA_HEADS, dtype=F32)
        gla_s0 = jnp.einsum("bdhke,hg->bdhegk", state_gla[:, l], eye_h).reshape(
            bs, 2, GLA_HEADS * GLA_DV, GLA_HEADS * GLA_DK)
        ctx = {"mla": mla_ctx, "past": past, "s5_h0": h0rows, "gla_s0": gla_s0}
        y_s, _ = _trunk_layer(y_s, mods[l], lambda row: 1 + row // ls, wl, bs, ls, tabs_s, rope_tabs, ctx)

    return (y_p.reshape(bp, lp, d), y_s.reshape(bs, ls, d),
            cache_bufs[0], cache_bufs[1], jnp.stack(s5_l, axis=1), cache_bufs[2])
```

```python
import functools
import math

import numpy as np
import jax
import jax.numpy as jnp
from jax import lax
from jax.experimental import pallas as pl
from jax.experimental.pallas import tpu as pltpu

F32 = jnp.float32
BF16 = jnp.bfloat16

D_MODEL = 1024
DEPTH = 2
GRID_W = 64
D_BRANCH = 256
EPS = 1e-6

MLA_HEADS = 4
MLA_Q_RANK = 192
MLA_KV_RANK = 128
MLA_NOPE = 64
MLA_ROPE = 32
MLA_QK = 96
MLA_V = 64
ROPE_BASE = 10000.0
HEAD_PAD = 128

HY_BANDS = 16
HY_FEAT = 33
HY_HIDDEN = 64
HY_SHIFT = 0.05
HY_FAST_DECAY = 0.3
HY_SLOW_DECAY = 1.5
HY_TARGET = 1e-2
HY_BLOCK = 512

S5_GROUP = 16
S5_GROUPS = 16
S5_STATE = 64
S5_N = S5_GROUPS * S5_STATE
S5_ROWS = 512
S5_SEG = 256

GLA_HEADS = 4
GLA_DK = 32
GLA_DV = 64
GLA_RANK = 16
GLA_TAU = 16.0
GLA_CHUNK = 64
GLA_SUPER = 256

SEG_MLA = (0, 384)
SEG_GATE = (384, 1408)
SEG_HY = (1408, 2176)
SEG_S5 = (2176, 2432)
SEG_GLA = (2432, 2944)
N_PROJ = 2944
W_IN_ORDER = ((0, 192), (320, 352), (2656, 2688), (192, 320),
              (352, 608), (1376, 1632), (1888, 2144), (2688, 2944),
              (608, 1376), (1632, 1888), (2144, 2656))
GLA_G_LANE = 96

VMEM_LIMIT = 48 * 1024 * 1024


def _cp(*sem):
    return pltpu.CompilerParams(dimension_semantics=sem, vmem_limit_bytes=VMEM_LIMIT)


def _dot(a, b):
    return jnp.dot(a, b, preferred_element_type=F32)


def _dot_nt(a, b):
    return lax.dot_general(a, b, (((1,), (1,)), ((), ())), preferred_element_type=F32)


def _dot_tn(a, b):
    return lax.dot_general(a, b, (((0,), (0,)), ((), ())), preferred_element_type=F32)


def _split2(x):
    hi = x.astype(BF16)
    lo = (x - hi.astype(F32)).astype(BF16)
    return hi, lo


def _split3(x):
    h1 = x.astype(BF16)
    r1 = x - h1.astype(F32)
    h2 = r1.astype(BF16)
    h3 = (r1 - h2.astype(F32)).astype(BF16)
    return h1, h2, h3


def _dot3(a, b):
    a1, a2 = _split2(a)
    b1, b2 = _split2(b)
    return _dot(a1, b1) + (_dot(a1, b2) + _dot(a2, b1))


def _silu(z):
    return z / (1.0 + jnp.exp(-z))


def _mod_kernel(c_ref, w_ref, b_ref, o_ref):
    s = _silu(c_ref[...])
    o_ref[0] = _dot(s.astype(BF16), w_ref[0].astype(BF16)) + b_ref[0]


def _modulation(conds, ada_w, ada_b):
    d = D_MODEL
    return pl.pallas_call(
        _mod_kernel,
        grid=(DEPTH, 3),
        in_specs=[pl.BlockSpec((8, d), lambda l, j: (0, 0)),
                  pl.BlockSpec((1, d, d), lambda l, j: (l, 0, j)),
                  pl.BlockSpec((1, 1, d), lambda l, j: (l, 0, j))],
        out_specs=pl.BlockSpec((1, 8, d), lambda l, j: (l, 0, j)),
        out_shape=jax.ShapeDtypeStruct((DEPTH, 8, 3 * d), F32),
        compiler_params=_cp("arbitrary", "arbitrary"),
        name="modulation",
    )(conds, ada_w, ada_b.reshape(DEPTH, 1, 3 * d))


def _head_norm(xh, w):
    ms = jnp.sum(xh * xh, axis=-1, keepdims=True) * (1.0 / MLA_QK)
    return xh * lax.rsqrt(ms + EPS) * w


def _rope(xh, cos, sin_a, sin_b):
    return xh * cos + pltpu.roll(xh, HEAD_PAD - 8, 1) * sin_a + pltpu.roll(xh, 8, 1) * sin_b


def _mla_steps(m, has_q, rope, cache_seqs, n_aliased, refs):
    refs = list(refs)
    if has_q:
        qan_ref, wuq_ref, qn_ref, kvn_ref = refs[:4]
        refs = refs[4:]
    wuk_ref, wuv_ref, kn_ref = refs[:3]
    refs = refs[3:]
    if rope:
        cos_ref, sa_ref, sb_ref = refs[:3]
        refs = refs[3:]
        cos, sa, sb = cos_ref[...], sa_ref[...], sb_ref[...]
    refs = refs[n_aliased:]
    if has_q:
        q_ref = refs.pop(0)
    k_ref, v_ref = refs[:2]
    if cache_seqs:
        ckv_ref, kro_ref = refs[2:]
    if has_q:
        lane = lax.broadcasted_iota(jnp.int32, (1, HEAD_PAD), 1)
        mixed = m[:, 128:256]
        cq = jnp.concatenate([m[:, 0:128], jnp.where(lane < MLA_NOPE, mixed, 0.0)], axis=1)
        ms = jnp.sum(cq * cq, axis=-1, keepdims=True) * (1.0 / MLA_Q_RANK)
        cqn = cq * lax.rsqrt(ms + EPS) * qan_ref[...]
        ckv = m[:, 256:384]
        ckvn = ckv * lax.rsqrt(jnp.mean(ckv * ckv, axis=-1, keepdims=True) + EPS) * kvn_ref[...]
        yield
        q = _dot(cqn.astype(BF16), wuq_ref[...])
        kr = jnp.where(jnp.logical_and(lane >= MLA_NOPE, lane < MLA_NOPE + MLA_ROPE), mixed, 0.0)
        if cache_seqs:
            seq_len = ckv_ref.shape[2]
            for s in range(cache_seqs):
                ckv_ref[s, 0] = ckvn[s * seq_len:(s + 1) * seq_len]
                kro_ref[s, 0] = kr[s * seq_len:(s + 1) * seq_len, MLA_NOPE:MLA_NOPE + MLA_ROPE]
    else:
        ckvn = m[:, 0:128]
        kr = m[:, 128:256]
    cb = ckvn.astype(BF16)
    kup = _dot(cb, wuk_ref[...])
    v_ref[...] = _dot(cb, wuv_ref[...]).astype(BF16)
    yield
    for h in range(MLA_HEADS):
        sl = slice(HEAD_PAD * h, HEAD_PAD * (h + 1))
        kh = _head_norm(kup[:, sl] + kr, kn_ref[...])
        if rope:
            kh = _rope(kh, cos, sa, sb)
        k_ref[:, sl] = kh.astype(BF16)
        if has_q:
            qh = _head_norm(q[:, sl], qn_ref[...])
            if rope:
                qh = _rope(qh, cos, sa, sb)
            q_ref[:, sl] = (qh * (MLA_QK ** -0.5)).astype(BF16)
        if h % 2 == 1:
            yield


def _mla_prep_kernel(has_q, rope, m_ref, *refs):
    for _ in _mla_steps(m_ref[...], has_q, rope, 0, 0, refs):
        pass


def _inproj_kernel(fuse_mla, cache_seqs, n_aliased, x_ref, mod_ref, nw_ref, w_ref, *refs):
    n_mla_in = 7 + n_aliased if fuse_mla else 0
    mla_refs, (o_mla, o_g, o_hy, o_s5, o_gla) = refs[:n_mla_in], refs[n_mla_in:n_mla_in + 5]
    w_src, w_ref = w_ref, refs[-1]

    @pl.when(pl.program_id(0) == 0)
    def _():
        dst = 0
        for lo, hi in W_IN_ORDER:
            w_ref[:, dst:dst + hi - lo] = w_src[0, :, lo:hi].astype(BF16)
            dst += hi - lo

    x = x_ref[...]
    ms = jnp.mean(x * x, axis=-1, keepdims=True)
    y = x * lax.rsqrt(ms + EPS) * nw_ref[...]
    h = (y * (1.0 + mod_ref[0, 1:2, :]) + mod_ref[0, 0:1, :]).astype(BF16)
    project = lambda o, seg: o.__setitem__(Ellipsis, _dot(h, w_ref[:, seg[0]:seg[1]]).astype(o.dtype))
    m = _dot(h, w_ref[:, SEG_MLA[0]:SEG_MLA[1]])
    o_mla[...] = m
    steps = iter(()) if not fuse_mla else _mla_steps(
        m, True, False, cache_seqs, n_aliased, list(mla_refs) + list(refs[n_mla_in + 5:-1]))
    project(o_g, SEG_GATE)
    next(steps, None)
    next(steps, None)
    project(o_hy, SEG_HY)
    next(steps, None)
    project(o_s5, SEG_S5)
    project(o_gla, SEG_GLA)
    for _ in steps:
        pass


def _inproj(x, mod, mod_row, wl, tm, seq_len, cache=None):
    n, d = x.shape
    fuse_mla = cache is not None
    full = lambda shape: pl.BlockSpec(shape, lambda i: (0,) * len(shape))
    row = lambda w: pl.BlockSpec((tm, w), lambda i: (i, 0))
    args = [x, mod, wl["norm_w"], wl["w_in"]]
    specs = [row(d), pl.BlockSpec((1, 3, d), lambda i: (mod_row(i * tm), 0, 0)), full((1, d)),
             pl.BlockSpec((1, d, N_PROJ), lambda i: (wl["layer"], 0, 0), pipeline_mode=pl.Buffered(1))]
    widths = [hi - lo for lo, hi in (SEG_MLA, SEG_GATE, SEG_HY, SEG_S5, SEG_GLA)]
    dtypes = [F32, BF16, F32, F32, F32]
    aliases = {}
    nseq = 0
    if fuse_mla:
        args += [wl["qa_norm"], wl["w_uq"], wl["q_norm"], wl["kva_norm"], wl["w_uk"], wl["w_uv"], wl["k_norm"]]
        specs += [full((1, 256)), full((256, 512)), full((1, 128)), full((1, 128)),
                  full((128, 512)), full((128, 256)), full((1, 128))]
        widths += [512, 512, 256]
        dtypes += [BF16, BF16, BF16]
    out_specs = [row(w) for w in widths]
    out_shape = [jax.ShapeDtypeStruct((n, w), t) for w, t in zip(widths, dtypes)]
    if fuse_mla:
        layer, prev = cache
        nseq = tm // seq_len
        for w in (MLA_KV_RANK, MLA_ROPE):
            out_specs.append(pl.BlockSpec((nseq, 1, seq_len, w), lambda i: (i, layer, 0, 0)))
            out_shape.append(jax.ShapeDtypeStruct((n // seq_len, DEPTH, seq_len, w), F32))
        for k, buf in enumerate(prev):
            aliases[len(args)] = len(out_shape) - 2 + k
            args.append(buf)
            specs.append(pl.BlockSpec(memory_space=pl.ANY))
    return pl.pallas_call(
        functools.partial(_inproj_kernel, fuse_mla, nseq, len(aliases)),
        grid=(n // tm,),
        in_specs=specs, out_specs=out_specs, out_shape=out_shape,
        input_output_aliases=aliases,
        scratch_shapes=[pltpu.VMEM((d, N_PROJ), BF16)],
        compiler_params=_cp("arbitrary"),
        name="inproj",
    )(*args)


def _mla_prep(m, wl, rope_tabs, seq_len, tm, has_q):
    n, wm = m.shape
    rope = rope_tabs is not None
    full = lambda shape: pl.BlockSpec(shape, lambda i: (0,) * len(shape))
    row = lambda w: pl.BlockSpec((tm, w), lambda i: (i, 0))
    args, specs = [m], [row(wm)]
    if has_q:
        args += [wl["qa_norm"], wl["w_uq"], wl["q_norm"], wl["kva_norm"]]
        specs += [full((1, 256)), full((256, 512)), full((1, 128)), full((1, 128))]
    args += [wl["w_uk"], wl["w_uv"], wl["k_norm"]]
    specs += [full((128, 512)), full((128, 256)), full((1, 128))]
    if rope:
        nt = seq_len // tm
        args += list(rope_tabs)
        specs += [pl.BlockSpec((tm, HEAD_PAD), lambda i: (i % nt, 0))] * 3
    widths = ([512] if has_q else []) + [512, 256]
    return pl.pallas_call(
        functools.partial(_mla_prep_kernel, has_q, rope),
        grid=(n // tm,),
        in_specs=specs, out_specs=[row(w) for w in widths],
        out_shape=[jax.ShapeDtypeStruct((n, w), BF16) for w in widths],
        compiler_params=_cp("arbitrary"),
        name="mla_prep",
    )(*args)


def _attn_kernel(nparts, nseq, q_ref, *refs):
    kv = [(refs[2 * i], refs[2 * i + 1]) for i in range(nparts)]
    o_ref = refs[2 * nparts]
    tq = q_ref.shape[0] // nseq
    low = lax.broadcasted_iota(jnp.int32, (1, HEAD_PAD), 1) < MLA_V
    units = [(s, h) for s in range(nseq) for h in range(MLA_HEADS)]

    def keys(ref, s):
        lk = ref.shape[0] // nseq
        return slice(s * lk, (s + 1) * lk)

    def scores(s, h):
        sl = slice(HEAD_PAD * h, HEAD_PAD * (h + 1))
        return [_dot_nt(q_ref[s * tq:(s + 1) * tq, sl], k_ref[keys(k_ref, s), sl]) for k_ref, _ in kv]

    s_next = scores(*units[0])
    acc = None
    for n, (s, h) in enumerate(units):
        pair, j = divmod(h, 2)
        sc = s_next
        if n + 1 < len(units):
            s_next = scores(*units[n + 1])
        if j == 0:
            v_half = []
            for _, v_ref in kv:
                vp = v_ref[keys(v_ref, s), HEAD_PAD * pair:HEAD_PAD * (pair + 1)]
                zero = jnp.zeros_like(vp)
                v_half.append((jnp.where(low, vp, zero), jnp.where(low, zero, vp)))
        m = functools.reduce(jnp.maximum, [jnp.max(x, axis=-1, keepdims=True) for x in sc])
        p = [jnp.exp(x - m) for x in sc]
        den = functools.reduce(jnp.add, [jnp.sum(x, axis=-1, keepdims=True) for x in p])
        num = functools.reduce(jnp.add, [_dot(x.astype(BF16), vh[j]) for x, vh in zip(p, v_half)])
        o = num / den
        acc = o if j == 0 else acc + o
        if j == 1:
            o_ref[s * tq:(s + 1) * tq, HEAD_PAD * pair:HEAD_PAD * (pair + 1)] = acc.astype(BF16)


def _attention(q, kv_parts, batch, lq, tq):
    nq = lq // tq
    nseq = max(1, min(batch, 1024 // lq)) if nq == 1 else 1
    args, specs = [q], [pl.BlockSpec((nseq * tq, 512), lambda b, i: (b * nq + i, 0))]
    for k, v, lk in kv_parts:
        args += [k, v]
        specs += [pl.BlockSpec((nseq * lk, 512), lambda b, i: (b, 0)),
                  pl.BlockSpec((nseq * lk, 256), lambda b, i: (b, 0))]
    return pl.pallas_call(
        functools.partial(_attn_kernel, len(kv_parts), nseq),
        grid=(batch // nseq, nq),
        in_specs=specs,
        out_specs=pl.BlockSpec((nseq * tq, 256), lambda b, i: (b * nq + i, 0)),
        out_shape=jax.ShapeDtypeStruct((batch * lq, 256), BF16),
        compiler_params=_cp("arbitrary", "arbitrary"),
        name="attention",
    )(*args)


def _hyena_kernel(seq_len, bk, nseq, x_ref, cw_ref, cb_ref, fo_ref, go_ref, k_ref, bias_ref, o_ref,
                  u_sc, y_sc, z_sc):
    c = D_BRANCH
    n = nseq * seq_len
    nblk = seq_len // bk
    pos = jnp.bitwise_and(lax.broadcasted_iota(jnp.int32, (n, 1), 0), seq_len - 1)
    first, last = pos == 0, pos == seq_len - 1

    def short_conv(g):
        cols = slice(g * c, (g + 1) * c)
        x = x_ref[:, cols]
        xm = jnp.where(first, 0.0, pltpu.roll(x, 1, 0))
        xp = jnp.where(last, 0.0, pltpu.roll(x, n - 1, 0))
        return cw_ref[0:1, cols] * xm + cw_ref[1:2, cols] * x + cw_ref[2:3, cols] * xp + cb_ref[:, cols]

    fo, go = fo_ref[...], go_ref[...]

    def long_conv(s, v, order, emit):
        cols = slice(order * c, (order + 1) * c)
        for j in range(nblk):
            u_sc[s, j] = _dot(fo, v[j * bk:(j + 1) * bk].astype(BF16))
        yield
        rc = 32
        for i in range(nblk):
            def mix(r, carry, i=i):
                top = pl.ds(pl.multiple_of(r * rc, rc), rc)
                bot = pl.ds(pl.multiple_of(bk + r * rc, rc), rc)
                at = ab = None
                for j in range(nblk):
                    q = i - j + nblk - 1
                    kt, kb = k_ref[q, top, cols], k_ref[q, bot, cols]
                    ut, ub = u_sc[s, j, top, :], u_sc[s, j, bot, :]
                    pt, pb = ut * kt - ub * kb, ut * kb + ub * kt
                    at, ab = (pt, pb) if at is None else (at + pt, ab + pb)
                z_sc[s, top, :] = at.astype(BF16)
                z_sc[s, bot, :] = ab.astype(BF16)
                return carry

            lax.fori_loop(0, bk // rc, mix, 0, unroll=True if nblk == 1 else 2)
            y = _dot(go, z_sc[s])
            yield
            emit(i, y)

    v_all, x1_all, x2_all = short_conv(0), short_conv(1), short_conv(2)

    def sequence(s):
        base = s * seq_len
        v = v_all[base:base + seq_len]

        def emit1(i, y):
            r = slice(i * bk, (i + 1) * bk)
            y_sc[s, r, :] = x1_all[base + i * bk:base + (i + 1) * bk] * (y + bias_ref[0:1, :] * v[r])

        yield from long_conv(s, v, 0, emit1)
        y1 = y_sc[s]

        def emit2(i, y):
            r = slice(i * bk, (i + 1) * bk)
            o_ref[base + i * bk:base + (i + 1) * bk, :] = (
                x2_all[base + i * bk:base + (i + 1) * bk] * (y + bias_ref[1:2, :] * y1[r])).astype(BF16)

        yield from long_conv(s, y1, 1, emit2)

    live = [sequence(s) for s in range(nseq)]
    while live:
        for g in list(live):
            if next(g, StopIteration) is StopIteration:
                live.remove(g)


def _hyena(x, wl, kspec, fo, go, batch, seq_len):
    c = D_BRANCH
    bk = fo.shape[1]
    nseq = max(1, min(batch, 1024 // seq_len))
    rows = nseq * seq_len
    nblk = seq_len // bk
    full = lambda a: pl.BlockSpec(a.shape, lambda i: (0,) * a.ndim)
    return pl.pallas_call(
        functools.partial(_hyena_kernel, seq_len, bk, nseq),
        grid=(batch // nseq,),
        in_specs=[pl.BlockSpec((rows, 3 * c), lambda i: (i, 0)),
                  full(wl["hy_conv_w"]), full(wl["hy_conv_b"]), full(fo), full(go),
                  pl.BlockSpec(kspec.shape, lambda i: (0, 0, 0), pipeline_mode=pl.Buffered(1)),
                  full(wl["hy_bias"])],
        out_specs=pl.BlockSpec((rows, c), lambda i: (i, 0)),
        out_shape=jax.ShapeDtypeStruct((batch * seq_len, c), BF16),
        scratch_shapes=[pltpu.VMEM((nseq, nblk, 2 * bk, c), F32), pltpu.VMEM((nseq, seq_len, c), F32),
                        pltpu.VMEM((nseq, 2 * bk, c), BF16)],
        compiler_params=pltpu.CompilerParams(dimension_semantics=("arbitrary",), vmem_limit_bytes=56 * 1024 * 1024),
        name="hyena",
    )(x, wl["hy_conv_w"], wl["hy_conv_b"], fo, go, kspec, wl["hy_bias"])


def _hy_mlp_kernel(feat_ref, w1_ref, b1_ref, f1_ref, w2_ref, b2_ref, f2_ref, w3_ref, win_ref,
                   kern_ref, nrm_ref):
    i = pl.program_id(0)
    tl = win_ref.shape[0]
    h = jnp.sin(f1_ref[...] * (_dot3(feat_ref[...], w1_ref[...]) + b1_ref[...]))
    h = jnp.sin(f2_ref[...] * (_dot3(h, w2_ref[...]) + b2_ref[...]))
    filt = _dot3(h, w3_ref[...])
    filt = jnp.concatenate([filt[:, 0:512], filt[:, 512:1024]], axis=0)
    win = win_ref[...]
    row0 = (lax.broadcasted_iota(jnp.int32, (tl, 1), 0) + i * tl) == 0
    filt = jnp.where(row0, 0.0, filt * jnp.concatenate([win, win], axis=1))
    kern_ref[...] = filt
    part = jnp.sum(jnp.abs(filt), axis=0, keepdims=True)

    @pl.when(i == 0)
    def _():
        nrm_ref[...] = jnp.zeros_like(nrm_ref)
    nrm_ref[...] += jnp.broadcast_to(part, nrm_ref.shape)


def _hy_mlp(feat, wl, win):
    rows = win.shape[0]
    tl = 256
    nl = rows // (2 * tl)
    full = lambda shape: pl.BlockSpec(shape, lambda i: (0,) * len(shape))
    return pl.pallas_call(
        _hy_mlp_kernel,
        grid=(rows // tl,),
        in_specs=[pl.BlockSpec((tl // 2, 128), lambda i: (i, 0)),
                  full((128, 128)), full((1, 128)), full((1, 128)),
                  full((128, 128)), full((1, 128)), full((1, 128)),
                  pl.BlockSpec((128, 1024), lambda i: (0, jnp.where(i < nl, 1, 0))),
                  pl.BlockSpec((tl, 256), lambda i: (i, 0))],
        out_specs=[pl.BlockSpec((tl, 512), lambda i: (i, 0)), full((8, 512))],
        out_shape=[jax.ShapeDtypeStruct((rows, 512), F32), jax.ShapeDtypeStruct((8, 512), F32)],
        compiler_params=_cp("arbitrary"),
        name="hy_mlp",
    )(feat, wl["hy_w1"], wl["hy_b1"], wl["hy_f1"], wl["hy_w2"], wl["hy_b2"], wl["hy_f2"], wl["hy_w3"], win)


def _hy_kspec_kernel(lo_ref, hi_ref, fo_ref, fb_ref, n_ref, o_ref):
    bk = lo_ref.shape[0]
    k = _dot(fo_ref[...], hi_ref[...].astype(BF16)) + _dot(fb_ref[...], lo_ref[...].astype(BF16))
    o_ref[0] = k * ((1.0 / bk) / n_ref[0:1, :])


def _hy_kspec(kern_lin, nrm, fo, fb):
    n2, bk = fo.shape
    nq = kern_lin.shape[0] // bk - 1
    full = lambda a: pl.BlockSpec(a.shape, lambda q: (0,) * a.ndim)
    return pl.pallas_call(
        _hy_kspec_kernel,
        grid=(nq,),
        in_specs=[pl.BlockSpec((bk, 512), lambda q: (q, 0)),
                  pl.BlockSpec((bk, 512), lambda q: (q + 1, 0)),
                  full(fo), full(fb), full(nrm)],
        out_specs=pl.BlockSpec((1, n2, 512), lambda q: (q, 0, 0)),
        out_shape=jax.ShapeDtypeStruct((nq, n2, 512), F32),
        compiler_params=_cp("arbitrary"),
        name="hy_kspec",
    )(kern_lin, kern_lin, fo, fb, nrm)


def _s5_discretise(are_ref, aim_ref, ldt_ref):
    ar = jnp.minimum(are_ref[0], -1e-4)
    ai = aim_ref[0]
    dt = jnp.exp(ldt_ref[0])
    e = jnp.exp(ar * dt)
    return ar, ai, e * jnp.cos(ai * dt), e * jnp.sin(ai * dt)


def _s5_scan_kernel(nseq, emit_y, u_ref, hin_ref, are_ref, aim_ref, ldt_ref, bre_ref, bim_ref, *rest):
    if emit_y:
        cre_ref, cim_ref, y_ref, hfin_ref, wb_sc, ab_sc, s_sc, hc_sc, perm_sc, wc_sc = rest
    else:
        hfin_ref, wb_sc, ab_sc, s_sc, hc_sc, perm_sc = rest
    d = pl.program_id(0)
    c = pl.program_id(1)
    n = S5_N

    @pl.when(c == 0)
    def _():
        ar, ai, abr, abi = _s5_discretise(are_ref, aim_ref, ldt_ref)
        ab_sc[0:1, :] = abr
        ab_sc[1:2, :] = abi
        den = 1.0 / (ar * ar + ai * ai)
        cr = ((abr - 1.0) * ar + abi * ai) * den
        ci = (abi * ar - (abr - 1.0) * ai) * den
        grp_rows = lax.shift_right_logical(lax.broadcasted_iota(jnp.int32, (D_BRANCH, 1), 0), 4)
        grp_cols = lax.shift_right_logical(lax.broadcasted_iota(jnp.int32, (1, n), 1), 6)
        expand_b = lambda ref: jnp.where(grp_rows == grp_cols, jnp.concatenate([ref[0]] * (n // 128), axis=1), 0.0)
        bre, bim = expand_b(bre_ref), expand_b(bim_ref)
        wb_sc[:, 0:n] = (cr * bre - ci * bim).astype(BF16)
        wb_sc[:, n:2 * n] = (cr * bim + ci * bre).astype(BF16)
        if emit_y:
            st_rows = lax.shift_right_logical(lax.broadcasted_iota(jnp.int32, (n, 1), 0), 6)
            ch_cols = lax.shift_right_logical(lax.broadcasted_iota(jnp.int32, (1, D_BRANCH), 1), 4)
            expand_c = lambda ref: jnp.where(st_rows == ch_cols, jnp.concatenate([ref[0]] * 2, axis=1), 0.0)
            wc_sc[0:n, :] = expand_c(cre_ref).astype(BF16)
            wc_sc[n:2 * n, :] = (-expand_c(cim_ref)).astype(BF16)
        hc_sc[...] = hin_ref[0]

    steps = u_ref.shape[1]
    rows_c = nseq * steps

    @pl.when(c == 0)
    def _():
        i = lax.broadcasted_iota(jnp.int32, (rows_c, rows_c), 0)
        j = lax.broadcasted_iota(jnp.int32, (rows_c, rows_c), 1)
        p = lax.shift_right_logical(i, int(math.log2(nseq)))
        step = p + d * (steps - 1 - 2 * p)
        src = jnp.bitwise_and(i, nseq - 1) * steps + step
        perm_sc[...] = jnp.where(j == src, 1.0, 0.0).astype(BF16)

    lhs = _dot(perm_sc[...], u_ref[...].reshape(rows_c, D_BRANCH).astype(BF16)).astype(BF16)
    lb = 256
    y = None
    for j in range(n // lb):
        lr = slice(lb * j, lb * (j + 1))
        li = slice(n + lb * j, n + lb * (j + 1))
        bur = _dot(lhs, wb_sc[:, lr])
        bui = _dot(lhs, wb_sc[:, li])
        abr = ab_sc[0:1, lr]
        abi = ab_sc[1:2, lr]
        hr, hi = hc_sc[:, lr], hc_sc[:, li]
        for p in range(steps):
            rows = slice(p * nseq, (p + 1) * nseq)
            hr, hi = abr * hr - abi * hi + bur[rows], abr * hi + abi * hr + bui[rows]
            if emit_y:
                s_sc[rows, lr] = hr.astype(BF16)
                s_sc[rows, li] = hi.astype(BF16)
        hc_sc[:, lr] = hr
        hc_sc[:, li] = hi
        if emit_y:
            yj = _dot(s_sc[:, lr], wc_sc[lr, :]) + _dot(s_sc[:, li], wc_sc[li, :])
            y = yj if y is None else y + yj

    if emit_y:
        @pl.when(d == 0)
        def _():
            for p in range(steps):
                y_ref[0, :, p, :] = y[p * nseq:(p + 1) * nseq]

        @pl.when(d == 1)
        def _():
            for p in range(steps):
                y_ref[0, :, steps - 1 - p, :] = y[p * nseq:(p + 1) * nseq]

    @pl.when(c == pl.num_programs(1) - 1)
    def _():
        hfin_ref[0] = hc_sc[...]


def _s5_scan(u, hin, wl, emit_y):
    nseq, nstep, _ = u.shape
    steps = S5_ROWS // nseq
    nc = nstep // steps
    n = S5_N
    chunk = lambda d, c: c + d * (nc - 1 - 2 * c)
    per_dir = lambda shape: pl.BlockSpec((1,) + shape, lambda d, c: (d,) + (0,) * len(shape))
    args = [u, hin, wl["s5_are"], wl["s5_aim"], wl["s5_ldt"], wl["s5_bre"], wl["s5_bim"]]
    specs = [pl.BlockSpec((nseq, steps, D_BRANCH), lambda d, c: (0, chunk(d, c), 0)),
             per_dir((nseq, 2 * n)), per_dir((1, n)), per_dir((1, n)), per_dir((1, n)),
             per_dir((D_BRANCH, 128)), per_dir((D_BRANCH, 128))]
    out_specs = [per_dir((nseq, 2 * n))]
    out_shape = [jax.ShapeDtypeStruct((2, nseq, 2 * n), F32)]
    scratch = [pltpu.VMEM((D_BRANCH, 2 * n), BF16), pltpu.VMEM((8, n), F32),
               pltpu.VMEM((S5_ROWS, 2 * n), BF16), pltpu.VMEM((nseq, 2 * n), F32),
               pltpu.VMEM((S5_ROWS, S5_ROWS), BF16)]
    if emit_y:
        args += [wl["s5_cre"], wl["s5_cim"]]
        specs += [per_dir((n, 128)), per_dir((n, 128))]
        out_specs = [pl.BlockSpec((1, nseq, steps, D_BRANCH), lambda d, c: (d, 0, chunk(d, c), 0))] + out_specs
        out_shape = [jax.ShapeDtypeStruct((2, nseq, nstep, D_BRANCH), F32)] + out_shape
        scratch += [pltpu.VMEM((2 * n, D_BRANCH), BF16)]
    return pl.pallas_call(
        functools.partial(_s5_scan_kernel, nseq, emit_y),
        grid=(2, nc),
        in_specs=specs, out_specs=out_specs, out_shape=out_shape, scratch_shapes=scratch,
        compiler_params=_cp("arbitrary", "arbitrary"),
        name="s5_scan" if emit_y else "s5_scan_finals",
    )(*args)


def _s5_chain_kernel(batch, nseg, f_ref, h0_ref, are_ref, aim_ref, ldt_ref, o_ref):
    d = pl.program_id(0)
    n = S5_N
    _, _, pr, pi = _s5_discretise(are_ref, aim_ref, ldt_ref)
    for _ in range(int(math.log2(S5_SEG))):
        pr, pi = pr * pr - pi * pi, 2.0 * pr * pi
    f = f_ref[0]
    fr, fi = f[:, 0:n], f[:, n:2 * n]
    h0 = h0_ref[0]
    h0r, h0i = h0[:, 0:n], h0[:, n:2 * n]
    nrow = batch * nseg
    seg = jnp.bitwise_and(lax.broadcasted_iota(jnp.int32, (nrow, 1), 0), nseg - 1)

    def run(shift, keep):
        xr, xi = h0r, h0i
        for _ in range(nseg - 1):
            zr = fr + pr * xr - pi * xi
            zi = fi + pr * xi + pi * xr
            xr = h0r + jnp.where(keep, pltpu.roll(zr, shift, 0), 0.0)
            xi = h0i + jnp.where(keep, pltpu.roll(zi, shift, 0), 0.0)
        o_ref[0, :, 0:n] = xr
        o_ref[0, :, n:2 * n] = xi

    @pl.when(d == 0)
    def _():
        run(1, seg != 0)

    @pl.when(d == 1)
    def _():
        run(nrow - 1, seg != nseg - 1)


def _s5_chain(fin, h0rows, wl, batch, nseg):
    nrow = batch * nseg
    n = S5_N
    per_dir = lambda shape: pl.BlockSpec((1,) + shape, lambda d: (d,) + (0,) * len(shape))
    return pl.pallas_call(
        functools.partial(_s5_chain_kernel, batch, nseg),
        grid=(2,),
        in_specs=[per_dir((nrow, 2 * n)), per_dir((nrow, 2 * n)), per_dir((1, n)), per_dir((1, n)), per_dir((1, n))],
        out_specs=per_dir((nrow, 2 * n)),
        out_shape=jax.ShapeDtypeStruct((2, nrow, 2 * n), F32),
        compiler_params=_cp("arbitrary"),
        name="s5_chain",
    )(fin, h0rows, wl["s5_are"], wl["s5_aim"], wl["s5_ldt"])


def _gla_kernel(seq_len, nb, n_aliased, q_ref, k_ref, v_ref, g_ref, gw_ref, gb_ref, s0_ref, *rest):
    o_ref, sfin_ref, qe_sc, upd_sc, dec_sc, sall_sc, lhs_sc, kt_sc, la_sc, oi_sc = rest[n_aliased:]
    d = pl.program_id(1)
    sign = 1 - 2 * d
    ck, sup = GLA_CHUNK, GLA_SUPER
    cps = sup // ck
    nsup, nchunk = seq_len // sup, seq_len // ck
    dk, dv = GLA_HEADS * GLA_DK, GLA_HEADS * GLA_DV
    r = lax.broadcasted_iota(jnp.int32, (sup, sup), 0)
    s = lax.broadcasted_iota(jnp.int32, (sup, sup), 1)
    same = lax.shift_right_logical(r, 6) == lax.shift_right_logical(s, 6)
    tri = jnp.logical_and(same, (s - r) * sign <= 0)
    cum_lhs = jnp.where(tri, 1.0, 0.0).astype(BF16)
    t4 = lax.broadcasted_iota(jnp.int32, (ck, GLA_HEADS * ck), 0)
    s4 = jnp.bitwise_and(lax.broadcasted_iota(jnp.int32, (ck, GLA_HEADS * ck), 1), ck - 1)
    tri4 = (s4 - t4) * sign <= 0
    pos = jnp.bitwise_and(lax.broadcasted_iota(jnp.int32, (ck, 1), 0), ck - 1)
    is_last = pos == (ck - 1) * (1 - d)
    row_chunk = lax.shift_right_logical(lax.broadcasted_iota(jnp.int32, (sup, 1), 0), 6)
    head_k = lax.shift_right_logical(lax.broadcasted_iota(jnp.int32, (1, dk), 1), 5)
    head_v = lax.shift_right_logical(lax.broadcasted_iota(jnp.int32, (1, dv), 1), 6)
    blockdiag = lax.shift_right_logical(lax.broadcasted_iota(jnp.int32, (dv, 1), 0), 6) == head_k

    def group_rows(u):
        return pl.ds(u * sup, sup) if isinstance(u, int) else pl.ds(pl.multiple_of(u * sup, sup), sup)

    def stage_a(u, slot):
        rows = group_rows(u)
        q = q_ref[rows, :] * (GLA_DK ** -0.5)
        k = k_ref[rows, :]
        v = v_ref[rows, :]
        cs = _dot(cum_lhs, la_sc[rows, :])
        yield
        bc = cs[:, 0:dk] + cs[:, dk:2 * dk]
        tots = [jnp.sum(jnp.where(is_last, bc[c * ck:(c + 1) * ck], 0.0), axis=0, keepdims=True)
                for c in range(cps)]
        tot = jnp.concatenate([jnp.broadcast_to(t, (ck, dk)) for t in tots], axis=0)
        ref = 0.5 * tot
        kt_sc[slot] = (k * jnp.exp(ref - bc)).astype(BF16)
        lhs_sc[slot] = (q * jnp.exp(bc - ref)).astype(BF16)
        qe_sc[rows, :] = (q * jnp.exp(bc)).astype(BF16)
        kl = (k * jnp.exp(tot - bc)).astype(BF16)
        zero = jnp.zeros_like(kl)
        klx = jnp.concatenate([jnp.where(row_chunk == c, kl, zero) for c in range(cps)], axis=1)
        upd = _dot_tn(v.astype(BF16), klx)
        yield
        for c in range(cps):
            upd_sc[u * cps + c] = jnp.where(blockdiag, upd[:, c * dk:(c + 1) * dk], 0.0)
            dec_sc[u * cps + c] = jnp.broadcast_to(jnp.exp(tots[c]), (8, dk))

    def stage_b(u, slot):
        rows = group_rows(u)
        v = v_ref[rows, :].astype(BF16)
        qt, kt = lhs_sc[slot], kt_sc[slot]
        zk, zv = jnp.zeros_like(kt[0:ck]), jnp.zeros_like(v[0:ck])
        p = [_dot_nt(qt[c * ck:(c + 1) * ck],
                     jnp.concatenate([jnp.where(head_k == h, kt[c * ck:(c + 1) * ck], zk)
                                      for h in range(GLA_HEADS)], axis=0)) for c in range(cps)]
        yield
        o = [_dot(jnp.where(tri4, p[c], 0.0).astype(BF16),
                  jnp.concatenate([jnp.where(head_v == h, v[c * ck:(c + 1) * ck], zv)
                                   for h in range(GLA_HEADS)], axis=0)) for c in range(cps)]
        yield
        oi_sc[rows, :] = jnp.concatenate(o, axis=0)

    def run(*stages):
        live = list(stages)
        while live:
            for g in list(live):
                if next(g, StopIteration) is StopIteration:
                    live.remove(g)

    x = _dot(g_ref[...].astype(BF16), gw_ref[0]) + gb_ref[0]
    la = (jnp.minimum(x, 0.0) - jnp.log(1.0 + jnp.exp(-jnp.abs(x)))) * (1.0 / GLA_TAU)
    la_sc[...] = jnp.concatenate(_split2(la), axis=1)

    nu = nb * nsup
    if nu <= 4:
        run(*[stage_a(u, u) for u in range(min(2, nu))])
        for u in range(0, nu, 2):
            run(*([stage_b(v, v % 4) for v in range(u, min(u + 2, nu))]
                  + [stage_a(v, v % 4) for v in range(u + 2, min(u + 4, nu))]))
    else:
        run(stage_a(0, 0), stage_a(1, 1))

        def sup_body(t, carry):
            u = 2 * t
            n0, n1 = jnp.minimum(u + 2, nu - 2), jnp.minimum(u + 3, nu - 1)
            run(stage_b(u, jnp.bitwise_and(u, 3)), stage_b(u + 1, jnp.bitwise_and(u + 1, 3)),
                stage_a(n0, jnp.bitwise_and(u + 2, 3)), stage_a(n1, jnp.bitwise_and(u + 3, 3)))
            return carry

        lax.fori_loop(0, nu // 2, sup_body, 0)

    for j in range(nb):
        def state_body(c, st, j=j):
            ci = j * nchunk + c + d * (nchunk - 1 - 2 * c)
            sall_sc[ci] = st.astype(BF16)
            return dec_sc[ci][0:1, :] * st + upd_sc[ci]

        st_fin = jnp.transpose(lax.fori_loop(0, nchunk, state_body, s0_ref[j, 0]))
        for h in range(GLA_HEADS):
            sfin_ref[j, 0, 0, h] = st_fin[h * GLA_DK:(h + 1) * GLA_DK, h * GLA_DV:(h + 1) * GLA_DV]

    def inter(u):
        rows = group_rows(u)
        qe = qe_sc[rows, :]
        oi = jnp.concatenate([_dot_nt(qe[c * ck:(c + 1) * ck], sall_sc[u * cps + c]) for c in range(cps)], axis=0)
        o_ref[0, rows, :] = (oi_sc[rows, :] + oi).astype(BF16)

    if nu <= 4:
        for u in range(nu):
            inter(u)
    else:
        lax.fori_loop(0, nu, lambda u, carry: (inter(u), carry)[1], 0, unroll=2 if nu % 2 == 0 else 1)


def _gla(gla_in, mla_in, wl, s0t, batch, seq_len, fin=(0, None, 1)):
    layer, prev_fin, fin_layers = fin
    aliases = {} if prev_fin is None else {7: 1}
    extra = [] if prev_fin is None else [prev_fin]
    n = gla_in.shape[0]
    dk, dv = GLA_HEADS * GLA_DK, GLA_HEADS * GLA_DV
    nb = max(1, min(batch, 1024 // seq_len))
    rows = nb * seq_len
    nchunk = nb * (seq_len // GLA_CHUNK)
    return pl.pallas_call(
        functools.partial(_gla_kernel, seq_len, nb, len(extra)),
        grid=(batch // nb, 2),
        input_output_aliases=aliases,
        in_specs=[pl.BlockSpec((rows, dk), lambda b, d: (b, 0)),
                  pl.BlockSpec((rows, dk), lambda b, d: (b, 1)),
                  pl.BlockSpec((rows, dv), lambda b, d: (b, 1)),
                  pl.BlockSpec((rows, dk), lambda b, d: (b, 1)),
                  pl.BlockSpec((1, dk, dk), lambda b, d: (d, 0, 0)),
                  pl.BlockSpec((1, 1, dk), lambda b, d: (d, 0, 0)),
                  pl.BlockSpec((nb, 1, dv, dk), lambda b, d: (b, d, 0, 0))]
                 + [pl.BlockSpec(memory_space=pl.ANY)] * len(extra),
        out_specs=[pl.BlockSpec((1, rows, dv), lambda b, d: (d, b, 0)),
                   pl.BlockSpec((nb, 1, 1, GLA_HEADS, GLA_DK, GLA_DV), lambda b, d: (b, layer, d, 0, 0, 0))],
        out_shape=[jax.ShapeDtypeStruct((2, n, dv), BF16),
                   jax.ShapeDtypeStruct((batch, fin_layers, 2, GLA_HEADS, GLA_DK, GLA_DV), F32)],
        scratch_shapes=[pltpu.VMEM((rows, dk), BF16),
                        pltpu.VMEM((nchunk, dv, dk), F32),
                        pltpu.VMEM((nchunk, 8, dk), F32),
                        pltpu.VMEM((nchunk, dv, dk), BF16),
                        pltpu.VMEM((4, GLA_SUPER, dk), BF16),
                        pltpu.VMEM((4, GLA_SUPER, dk), BF16),
                        pltpu.VMEM((rows, 2 * dk), BF16),
                        pltpu.VMEM((rows, dv), F32)],
        compiler_params=_cp("arbitrary", "arbitrary"),
        name="gla",
    )(gla_in, gla_in, gla_in, mla_in, wl["gla_gw"], wl["gla_gb"], s0t, *extra)


def _outproj_kernel(x_ref, mod_ref, g_ref, om_ref, oh_ref, su_ref, sf_ref, sb_ref, sd_ref, sw_ref, sbias_ref,
                    gf_ref, gb_ref, gn_ref, hm_ref, w_ref, y_ref):
    c = D_BRANCH
    g = _silu(g_ref[...])
    acc = _dot(om_ref[...] * g[:, 0:c], w_ref[0:c, :])
    acc += _dot(oh_ref[...] * g[:, c:2 * c], w_ref[c:2 * c, :])
    ys = sd_ref[...] * su_ref[...] + sf_ref[0] + sb_ref[0]
    ge = 0.5 * ys * (1.0 + jnp.tanh(math.sqrt(2.0 / math.pi) * (ys + 0.044715 * (ys * ys * ys))))
    o_s5 = ge / (1.0 + jnp.exp(-(_dot(ge.astype(BF16), sw_ref[...]) + sbias_ref[...])))
    acc += _dot((o_s5 * g[:, 2 * c:3 * c].astype(F32)).astype(BF16), w_ref[2 * c:3 * c, :])
    og = gf_ref[0].astype(F32) + gb_ref[0].astype(F32)
    hi, lo = _split2(og * og)
    ms = _dot(hi, hm_ref[...]) + _dot(lo, hm_ref[...])
    ogn = og * lax.rsqrt(ms + EPS) * gn_ref[...]
    acc += _dot((ogn * g[:, 3 * c:4 * c].astype(F32)).astype(BF16), w_ref[3 * c:4 * c, :])
    y_ref[...] = x_ref[...] + mod_ref[0, 2:3, :] * acc


def _outproj(x, mod, mod_row, gates, o_mla, o_hy, s5_u, s5_y, o_gla, wl, tm):
    n, d = x.shape
    c = D_BRANCH
    row = lambda w: pl.BlockSpec((tm, w), lambda i: (i, 0))
    per_dir = lambda k: pl.BlockSpec((1, tm, c), lambda i: (k, i, 0))
    full = lambda *shape: pl.BlockSpec(shape, lambda i: (0,) * len(shape))
    return pl.pallas_call(
        _outproj_kernel,
        grid=(n // tm,),
        in_specs=[row(d),
                  pl.BlockSpec((1, 3, d), lambda i: (mod_row(i * tm), 0, 0)),
                  row(d), row(c), row(c),
                  row(c), per_dir(0), per_dir(1), full(1, c), full(c, c), full(1, c),
                  per_dir(0), per_dir(1), full(1, c), full(c, c), full(d, d)],
        out_specs=row(d),
        out_shape=jax.ShapeDtypeStruct((n, d), F32),
        compiler_params=_cp("arbitrary"),
        name="outproj",
    )(x, mod, gates, o_mla, o_hy, s5_u, s5_y, s5_y, wl["s5_d"], wl["s5_glu_w"], wl["s5_glu_b"],
      o_gla, o_gla, wl["gla_norm"], wl["head_mean"], wl["w_out"])


def _rope_tables(seq_len):
    pos = np.arange(seq_len)
    inv = ROPE_BASE ** (-np.arange(0, 16, 2, dtype=np.float64) / 16.0)
    cos = np.ones((seq_len, HEAD_PAD))
    sin_a = np.zeros((seq_len, HEAD_PAD))
    sin_b = np.zeros((seq_len, HEAD_PAD))
    for base, p in ((MLA_NOPE, pos // GRID_W), (MLA_NOPE + 16, pos % GRID_W)):
        ang = p[:, None].astype(np.float64) * inv[None, :]
        cos[:, base:base + 8] = np.cos(ang)
        cos[:, base + 8:base + 16] = np.cos(ang)
        sin_a[:, base:base + 8] = -np.sin(ang)
        sin_b[:, base + 8:base + 16] = np.sin(ang)
    return tuple(jnp.asarray(t, F32) for t in (cos, sin_a, sin_b))


def _odd_dft(seq_len):
    bk = min(seq_len, HY_BLOCK)
    k = np.arange(bk)[:, None]
    t = np.arange(bk)[None, :]

    def mat(shift):
        ang = (np.pi / (2 * bk)) * (((2 * k + 1) * (t + shift)) % (4 * bk))
        return np.concatenate([np.cos(ang), -np.sin(ang)], axis=0)

    fo = mat(0)
    fb = -mat(bk)
    fb[:, 0] = 0.0
    const = lambda a: jnp.asarray(a, F32).astype(BF16)
    return const(fo), const(fb), const(fo.T)


def _hyena_tables(seq_len):
    lag = np.arange(-seq_len, seq_len)
    pos = np.where(lag == -seq_len, 0, np.abs(lag)).astype(np.float64)
    t = pos / seq_len
    w = 2.0 * np.pi * pos / seq_len
    bands = np.linspace(1e-4, HY_BANDS - 1, HY_BANDS)
    feat = np.zeros((2 * seq_len, HY_HIDDEN))
    feat[:, 0] = t
    feat[:, 1:1 + HY_BANDS] = np.cos(w[:, None] * bands)
    feat[:, 1 + HY_BANDS:HY_FEAT] = np.sin(w[:, None] * bands)
    feat = feat.reshape(-1, 2, 128, HY_HIDDEN).transpose(0, 2, 1, 3).reshape(seq_len, 2 * HY_HIDDEN)
    deltas = np.linspace(math.log(1.0 / HY_TARGET) / HY_FAST_DECAY, math.log(1.0 / HY_TARGET) / HY_SLOW_DECAY,
                         D_BRANCH)
    win = np.exp(-t[:, None] * deltas[None, :]) + HY_SHIFT
    return jnp.asarray(feat, F32), jnp.asarray(win, F32)


def _pad_to(a, shape):
    return jnp.pad(a, [(0, s - d) for s, d in zip(shape, a.shape)])


def _layer_weights(l, p):
    z = lambda *s: jnp.zeros(s, F32)
    d = D_MODEL
    wl = {"layer": l, "w_in": p["w_in"], "norm_w": p["norm_w"][l].reshape(1, d), "w_out": p["w_out"][l].astype(BF16)}
    wl["qa_norm"] = _pad_to(p["mla_qa_norm"][l].reshape(1, -1), (1, 256))
    w_uq = _pad_to(p["mla_w_uq"][l].reshape(MLA_Q_RANK, MLA_HEADS, MLA_QK), (256, MLA_HEADS, HEAD_PAD))
    wl["w_uq"] = w_uq.reshape(256, MLA_HEADS * HEAD_PAD).astype(BF16)
    wl["q_norm"] = _pad_to(p["mla_q_norm"][l].reshape(1, -1), (1, HEAD_PAD))
    wl["k_norm"] = _pad_to(p["mla_k_norm"][l].reshape(1, -1), (1, HEAD_PAD))
    wl["kva_norm"] = p["mla_kva_norm"][l].reshape(1, -1)
    w_ukv = p["mla_w_ukv"][l].reshape(MLA_KV_RANK, MLA_HEADS, MLA_NOPE + MLA_V)
    wl["w_uk"] = _pad_to(w_ukv[:, :, :MLA_NOPE], (MLA_KV_RANK, MLA_HEADS, HEAD_PAD)).reshape(MLA_KV_RANK, -1).astype(BF16)
    wl["w_uv"] = w_ukv[:, :, MLA_NOPE:].reshape(MLA_KV_RANK, MLA_HEADS * MLA_V).astype(BF16)
    wl["hy_conv_w"] = p["hy_conv_w"][l]
    wl["hy_conv_b"] = p["hy_conv_b"][l].reshape(1, -1)
    hh = HY_HIDDEN
    twice = lambda a: jnp.tile(a.reshape(1, -1), (1, 2))
    bdiag = lambda a: jnp.concatenate([jnp.pad(a, ((0, 0), (0, a.shape[1]))), jnp.pad(a, ((0, 0), (a.shape[1], 0)))], axis=0)
    wl["hy_w1"] = bdiag(_pad_to(p["hy_w1"][l], (hh, hh)))
    wl["hy_b1"], wl["hy_f1"] = twice(p["hy_b1"][l]), twice(p["hy_freq1"][l])
    wl["hy_w2"] = bdiag(p["hy_w2"][l])
    wl["hy_b2"], wl["hy_f2"] = twice(p["hy_b2"][l]), twice(p["hy_freq2"][l])
    w3 = p["hy_w3"][l]
    wl["hy_w3"] = jnp.concatenate([bdiag(w3[:, 0:512]), bdiag(w3[:, 512:1024])], axis=1)
    wl["hy_bias"] = p["hy_bias"][l]
    flat = lambda a: a[l].reshape(2, 1, S5_N)
    wl["s5_are"], wl["s5_aim"] = flat(p["s5_a_re"]), flat(p["s5_a_im"])
    wl["s5_ldt"] = jnp.repeat(p["s5_log_dt"][l], S5_STATE, axis=-1).reshape(2, 1, S5_N)
    lanes_b = lambda a: jnp.tile(a[l].transpose(0, 1, 3, 2).reshape(2, D_BRANCH, S5_STATE), (1, 1, 128 // S5_STATE))
    lanes_c = lambda a: jnp.tile(a[l].transpose(0, 1, 3, 2).reshape(2, S5_N, S5_GROUP), (1, 1, 128 // S5_GROUP))
    wl["s5_bre"], wl["s5_bim"] = lanes_b(p["s5_b_re"]), lanes_b(p["s5_b_im"])
    wl["s5_cre"], wl["s5_cim"] = lanes_c(p["s5_c_re"]), lanes_c(p["s5_c_im"])
    wl["s5_d"] = p["s5_d"][l].reshape(1, -1)
    wl["s5_glu_w"] = p["s5_glu_w"][l].astype(BF16)
    wl["s5_glu_b"] = p["s5_glu_b"][l].reshape(1, -1)
    gw = p["gla_gw"][l]
    dk = GLA_HEADS * GLA_DK
    wl["gla_gw"] = jnp.stack([_pad_to(jnp.pad(gw[i], ((GLA_G_LANE + GLA_RANK * i, 0), (0, 0))), (dk, dk))
                              for i in range(2)]).astype(BF16)
    wl["gla_gb"] = p["gla_gb"][l].reshape(2, 1, dk)
    wl["gla_norm"] = jnp.tile(p["gla_norm"][l], GLA_HEADS).reshape(1, -1)
    head = np.arange(D_BRANCH) // GLA_DV
    wl["head_mean"] = jnp.asarray((head[:, None] == head[None, :]) / GLA_DV, BF16)
    return wl


def _hyena_filters(wl, tabs):
    feat, win, fo, fb, _ = tabs
    kern_lin, nrm = _hy_mlp(feat, wl, win)
    return _hy_kspec(kern_lin, nrm, fo, fb)


def _trunk_layer(x, mod, mod_row, wl, batch, seq_len, hy_tabs, rope_tabs=None, ctx=None, layer=0, cache_bufs=None):
    n = batch * seq_len
    tm = 512
    if ctx is None:
        mla_in, gates, hy_in, s5_in, gla_in, q, k, v, ckv, krope = _inproj(
            x, mod, mod_row, wl, tm, seq_len, (layer, cache_bufs[:2]))
        kv_parts = [(k, v, seq_len)]
    else:
        mla_in, gates, hy_in, s5_in, gla_in = _inproj(x, mod, mod_row, wl, tm, seq_len)
        q, k, v = _mla_prep(mla_in, wl, rope_tabs, seq_len, tm, True)
        ckv = krope = None
        k_ctx, v_ctx = _mla_prep(ctx["mla"], wl, None, ctx["past"], 512, False)
        kv_parts = [(k_ctx, v_ctx, ctx["past"]), (k, v, seq_len)]
    o_mla = _attention(q, kv_parts, batch, seq_len, min(seq_len, 512))

    o_hy = _hyena(hy_in, wl, _hyena_filters(wl, hy_tabs), hy_tabs[2], hy_tabs[4], batch, seq_len)

    nseg = seq_len // S5_SEG
    nseq = batch * nseg
    u_seg = s5_in.reshape(nseq, S5_SEG, D_BRANCH)
    if ctx is None:
        hin = jnp.zeros((2, nseq, 2 * S5_N), F32)
    else:
        (fin,) = _s5_scan(u_seg, jnp.zeros((2, nseq, 2 * S5_N), F32), wl, False)
        hin = _s5_chain(fin, ctx["s5_h0"], wl, batch, nseg)
    y2, s5_fin = _s5_scan(u_seg, hin, wl, True)

    s0 = jnp.zeros((batch, 2, GLA_HEADS * GLA_DV, GLA_HEADS * GLA_DK), F32) if ctx is None else ctx["gla_s0"]
    fin = (0, None, 1) if ctx is not None else (layer, None if cache_bufs is None else cache_bufs[2], DEPTH)
    o_gla, gla_fin = _gla(gla_in, mla_in, wl, s0, batch, seq_len, fin)

    y = _outproj(x, mod, mod_row, gates, o_mla, o_hy, s5_in, y2.reshape(2, n, D_BRANCH), o_gla, wl, 2 * tm)
    return y, (ckv, krope, s5_fin, gla_fin)


def kernel(x_prompt, x_sample, c, cache_mla_ckv, cache_mla_krope, state_s5, state_gla, c_ctx, norm_w, ada_w, ada_b, w_in, w_out, mla_qa_norm, mla_kva_norm, mla_w_uq, mla_w_ukv, mla_q_norm, mla_k_norm, hy_conv_w, hy_conv_b, hy_w1, hy_b1, hy_freq1, hy_w2, hy_b2, hy_freq2, hy_w3, hy_bias, s5_a_re, s5_a_im, s5_log_dt, s5_b_re, s5_b_im, s5_c_re, s5_c_im, s5_d, s5_glu_w, s5_glu_b, gla_gw, gla_gb, gla_norm):
    params = dict(norm_w=norm_w, w_in=w_in, w_out=w_out, mla_qa_norm=mla_qa_norm, mla_kva_norm=mla_kva_norm,
                  mla_w_uq=mla_w_uq, mla_w_ukv=mla_w_ukv, mla_q_norm=mla_q_norm, mla_k_norm=mla_k_norm,
                  hy_conv_w=hy_conv_w, hy_conv_b=hy_conv_b, hy_w1=hy_w1, hy_b1=hy_b1, hy_freq1=hy_freq1,
                  hy_w2=hy_w2, hy_b2=hy_b2, hy_freq2=hy_freq2, hy_w3=hy_w3, hy_bias=hy_bias,
                  s5_a_re=s5_a_re, s5_a_im=s5_a_im, s5_log_dt=s5_log_dt, s5_b_re=s5_b_re, s5_b_im=s5_b_im,
                  s5_c_re=s5_c_re, s5_c_im=s5_c_im, s5_d=s5_d, s5_glu_w=s5_glu_w, s5_glu_b=s5_glu_b,
                  gla_gw=gla_gw, gla_gb=gla_gb, gla_norm=gla_norm)
    bp, lp, d = x_prompt.shape
    bs, ls, _ = x_sample.shape
    past = cache_mla_ckv.shape[2]
    n_s5 = S5_N

    conds = jnp.concatenate([c_ctx[None, :], c, jnp.zeros((8 - 1 - bs, d), F32)], axis=0)
    mods = _modulation(conds, ada_w, ada_b).reshape(DEPTH, 8, 3, d)

    tabs_p = _hyena_tables(lp) + _odd_dft(lp)
    tabs_s = _hyena_tables(ls) + _odd_dft(ls)
    rope_tabs = _rope_tables(ls)
    nseg = ls // S5_SEG

    y_p = x_prompt.reshape(bp * lp, d)
    y_s = x_sample.reshape(bs * ls, d)
    s5_l = []
    cache_bufs = (jnp.zeros((bp, DEPTH, lp, MLA_KV_RANK), F32), jnp.zeros((bp, DEPTH, lp, MLA_ROPE), F32),
                  jnp.zeros((bp, DEPTH, 2, GLA_HEADS, GLA_DK, GLA_DV), F32))
    for l in range(DEPTH):
        wl = _layer_weights(l, params)
        y_p, (ckv, krope, s5_fin, gla_fin) = _trunk_layer(y_p, mods[l], lambda row: 0, wl, bp, lp, tabs_p,
                                                          layer=l, cache_bufs=cache_bufs)
        cache_bufs = (ckv, krope, gla_fin)
        s5_l.append(jnp.stack([s5_fin[:, :, :n_s5], s5_fin[:, :, n_s5:]], axis=-1)
                    .reshape(2, bp, S5_GROUPS, S5_STATE, 2).transpose(1, 0, 2, 3, 4))

        mla_ctx = jnp.concatenate([cache_mla_ckv[:, l], jnp.zeros((bs, past, 64), F32), cache_mla_krope[:, l],
                                   jnp.zeros((bs, past, 32), F32)], axis=-1).reshape(bs * past, 256)
        st = state_s5[:, l]
        h0 = jnp.concatenate([st[..., 0].reshape(bs, 2, n_s5), st[..., 1].reshape(bs, 2, n_s5)], axis=-1)
        h0 = h0.transpose(1, 0, 2)
        h0rows = jnp.zeros((2, nseg * bs, 2 * n_s5), F32)
        h0rows = h0rows.at[0, 0::nseg].set(h0[0]).at[1, nseg - 1::nseg].set(h0[1])
        eye_h = jnp.eye(GLA_HEADS, dtype=F32)
        gla_s0 = jnp.einsum("bdhke,hg->bdhegk", state_gla[:, l], eye_h).reshape(
            bs, 2, GLA_HEADS * GLA_DV, GLA_HEADS * GLA_DK)
        ctx = {"mla": mla_ctx, "past": past, "s5_h0": h0rows, "gla_s0": gla_s0}
        y_s, _ = _trunk_layer(y_s, mods[l], lambda row: 1 + row // ls, wl, bs, ls, tabs_s, rope_tabs, ctx)

    return (y_p.reshape(bp, lp, d), y_s.reshape(bs, ls, d),
            cache_bufs[0], cache_bufs[1], jnp.stack(s5_l, axis=1), cache_bufs[2])
```
